```python
import math
import jax, jax.numpy as jnp
from jax import lax
import numpy as np

D_MODEL = 1024
BATCH = 8
SEQ = 8192
DEPTH = 1

CHUNK = 64
Q_BLOCK = 128
EPS = 1e-6
CONV_CH = D_MODEL // 2
CONV_WIDTH = 31
MLA_HEADS = 8
MLA_NOPE = 64
MLA_ROPE = 32
MLA_V = 64
MLA_Q_RANK = 256
MLA_KV_RANK = 128
ROPE_THETA = 10000.0
MIX_WIDTH = CONV_CH + MLA_HEADS * MLA_V
IN_COLS = 2 * CONV_CH + MLA_Q_RANK + MLA_KV_RANK + MLA_ROPE
MEM_LEN = 256
MEM_HEADS = 4
MEM_HEAD_DIM = D_MODEL // MEM_HEADS
D_FF = 2816
FFN_CONV_WIDTH = 3
MAX_START = 4096

kernel_name = "hybrid_conformer_mla_stream_layer"


def rms_norm(x, g):
    xf = x.astype(jnp.float32)
    y = xf * lax.rsqrt(jnp.mean(xf * xf, axis=-1, keepdims=True) + EPS)
    return (y * g.astype(jnp.float32)).astype(x.dtype)


def layer_norm(x, g, b):
    xf = x.astype(jnp.float32)
    mu = jnp.mean(xf, axis=-1, keepdims=True)
    xc = xf - mu
    y = xc * lax.rsqrt(jnp.mean(xc * xc, axis=-1, keepdims=True) + EPS)
    return (y * g.astype(jnp.float32) + b.astype(jnp.float32)).astype(x.dtype)


def causal_depthwise_conv(x, w, b):
    k_width, ch = w.shape
    y = lax.conv_general_dilated(
        x, w[:, None, :].astype(x.dtype), window_strides=(1,), padding=[(k_width - 1, 0)],
        dimension_numbers=('NWC', 'WIO', 'NWC'), feature_group_count=ch)
    return y + b.astype(x.dtype)


def rope_tables(positions, dim):
    inv_freq = ROPE_THETA ** (-jnp.arange(0, dim, 2, dtype=jnp.float32) / dim)
    ang = positions.astype(jnp.float32)[..., None] * inv_freq
    return jnp.cos(ang), jnp.sin(ang)


def apply_rope(x, cos, sin):
    x1, x2 = jnp.split(x.astype(jnp.float32), 2, axis=-1)
    return jnp.concatenate([x1 * cos - x2 * sin, x1 * sin + x2 * cos], axis=-1).astype(x.dtype)


def chunk_causal_attention(q, k, v, scale):
    bsz, seq, heads, dk = q.shape
    n_blocks = seq // Q_BLOCK
    q_blocks = q.reshape(bsz, n_blocks, Q_BLOCK, heads, dk).transpose(1, 0, 2, 3, 4)
    key_chunk = jnp.arange(seq) // CHUNK

    def one_block(args):
        i, qi = args
        s = jnp.einsum('bqhd,bkhd->bhqk', qi, k, preferred_element_type=jnp.float32) * scale
        q_chunk = (i * Q_BLOCK + jnp.arange(Q_BLOCK)) // CHUNK
        mask = key_chunk[None, :] <= q_chunk[:, None]
        s = jnp.where(mask[None, None], s, -jnp.inf)
        p = jax.nn.softmax(s, axis=-1)
        return jnp.einsum('bhqk,bkhd->bqhd', p.astype(v.dtype), v)

    out = lax.map(one_block, (jnp.arange(n_blocks), q_blocks))
    return out.transpose(1, 0, 2, 3, 4).reshape(bsz, seq, heads, v.shape[-1])


def hybrid_mixer(h, cos, sin, w_in, b_conv_in, w_conv_dw, b_conv_dw, conv_ln_g, conv_ln_b,
                 q_lat_norm_g, w_uq, kv_lat_norm_g, w_ukv, q_norm_g, k_norm_g, w_out):
    bsz, seq, _ = h.shape
    z = h @ w_in
    s1 = 2 * CONV_CH
    s2 = s1 + MLA_Q_RANK
    s3 = s2 + MLA_KV_RANK
    conv_in, c_q, c_kv, k_rope = jnp.split(z, [s1, s2, s3], axis=-1)

    a, gate = jnp.split(conv_in + b_conv_in, 2, axis=-1)
    u = a * jax.nn.sigmoid(gate)
    u = causal_depthwise_conv(u, w_conv_dw, b_conv_dw)
    u = jax.nn.silu(layer_norm(u, conv_ln_g, conv_ln_b))

    q = (rms_norm(c_q, q_lat_norm_g) @ w_uq).reshape(bsz, seq, MLA_HEADS, MLA_NOPE + MLA_ROPE)
    kv = (rms_norm(c_kv, kv_lat_norm_g) @ w_ukv).reshape(bsz, seq, MLA_HEADS, MLA_NOPE + MLA_V)
    k_nope, v = jnp.split(kv, [MLA_NOPE], axis=-1)
    k_r = jnp.broadcast_to(k_rope[:, :, None, :], (bsz, seq, MLA_HEADS, MLA_ROPE))
    k = jnp.concatenate([k_nope, k_r], axis=-1)
    q = rms_norm(q, q_norm_g)
    k = rms_norm(k, k_norm_g)
    q = jnp.concatenate([q[..., :MLA_NOPE], apply_rope(q[..., MLA_NOPE:], cos, sin)], axis=-1)
    k = jnp.concatenate([k[..., :MLA_NOPE], apply_rope(k[..., MLA_NOPE:], cos, sin)], axis=-1)
    attn = chunk_causal_attention(q, k, v, 1.0 / math.sqrt(MLA_NOPE + MLA_ROPE))
    attn = attn.reshape(bsz, seq, MLA_HEADS * MLA_V)

    return jnp.concatenate([u, attn], axis=-1) @ w_out


def memory_cross_attention(hq, hm, w_mem_q, w_mem_kv, mem_q_norm_g, mem_k_norm_g, w_mem_o):
    bsz, seq, _ = hq.shape
    q = (hq @ w_mem_q).reshape(bsz, seq, MEM_HEADS, MEM_HEAD_DIM)
    k, v = jnp.split(hm @ w_mem_kv, 2, axis=-1)
    k = k.reshape(bsz, MEM_LEN, MEM_HEADS, MEM_HEAD_DIM)
    v = v.reshape(bsz, MEM_LEN, MEM_HEADS, MEM_HEAD_DIM)
    q = rms_norm(q, mem_q_norm_g)
    k = rms_norm(k, mem_k_norm_g)
    s = jnp.einsum('bqhd,bkhd->bhqk', q, k, preferred_element_type=jnp.float32) / math.sqrt(MEM_HEAD_DIM)
    p = jax.nn.softmax(s, axis=-1)
    o = jnp.einsum('bhqk,bkhd->bqhd', p.astype(v.dtype), v).reshape(bsz, seq, D_MODEL)
    return o @ w_mem_o


def conv_gated_ffn(h, w_up, w_ffn_dw, b_ffn_dw, w_down):
    up = causal_depthwise_conv(h @ w_up, w_ffn_dw, b_ffn_dw)
    g, val = jnp.split(up, 2, axis=-1)
    return (jax.nn.silu(g) * val) @ w_down


def _fwd_setup_inputs(seed: int = 0) -> dict:
    key = jax.random.key(seed)
    ks = iter(jax.random.split(key, 40))
    L = DEPTH

    def w(shape, fan_in):
        return jax.random.normal(next(ks), shape, jnp.float32) * fan_in ** -0.5

    def gain(shape):
        return 1.0 + 0.05 * jax.random.normal(next(ks), shape, jnp.float32)

    def bias(shape):
        return 0.02 * jax.random.normal(next(ks), shape, jnp.float32)

    x = jax.random.normal(next(ks), (BATCH, SEQ, D_MODEL), jnp.float32)
    mem = jax.random.normal(next(ks), (BATCH, MEM_LEN, D_MODEL), jnp.float32)
    start = jax.random.randint(next(ks), (BATCH, 1), 0, MAX_START, dtype=jnp.int32)
    positions = start + jnp.arange(SEQ, dtype=jnp.int32)[None, :]
    return {
        "x": x,
        "mem": mem,
        "positions": positions,
        "mix_norm_g": gain((L, D_MODEL)),
        "w_in": w((L, D_MODEL, IN_COLS), D_MODEL),
        "b_conv_in": bias((L, 2 * CONV_CH)),
        "w_conv_dw": w((L, CONV_WIDTH, CONV_CH), CONV_WIDTH),
        "b_conv_dw": bias((L, CONV_CH)),
        "conv_ln_g": gain((L, CONV_CH)),
        "conv_ln_b": bias((L, CONV_CH)),
        "q_lat_norm_g": gain((L, MLA_Q_RANK)),
        "w_uq": w((L, MLA_Q_RANK, MLA_HEADS * (MLA_NOPE + MLA_ROPE)), MLA_Q_RANK),
        "kv_lat_norm_g": gain((L, MLA_KV_RANK)),
        "w_ukv": w((L, MLA_KV_RANK, MLA_HEADS * (MLA_NOPE + MLA_V)), MLA_KV_RANK),
        "q_norm_g": gain((L, MLA_NOPE + MLA_ROPE)),
        "k_norm_g": gain((L, MLA_NOPE + MLA_ROPE)),
        "w_out": w((L, MIX_WIDTH, D_MODEL), MIX_WIDTH),
        "mem_norm_x_g": gain((L, D_MODEL)),
        "mem_norm_m_g": gain((L, D_MODEL)),
        "w_mem_q": w((L, D_MODEL, D_MODEL), D_MODEL),
        "w_mem_kv": w((L, D_MODEL, 2 * D_MODEL), D_MODEL),
        "mem_q_norm_g": gain((L, MEM_HEAD_DIM)),
        "mem_k_norm_g": gain((L, MEM_HEAD_DIM)),
        "w_mem_o": w((L, D_MODEL, D_MODEL), D_MODEL),
        "ffn_norm_g": gain((L, D_MODEL)),
        "w_up": w((L, D_MODEL, 2 * D_FF), D_MODEL),
        "w_ffn_dw": w((L, FFN_CONV_WIDTH, 2 * D_FF), FFN_CONV_WIDTH),
        "b_ffn_dw": bias((L, 2 * D_FF)),
        "w_down": w((L, D_FF, D_MODEL), D_FF),
    }


def _fwd_reference(x, mem, positions, mix_norm_g, w_in, b_conv_in, w_conv_dw, b_conv_dw, conv_ln_g,
              conv_ln_b, q_lat_norm_g, w_uq, kv_lat_norm_g, w_ukv, q_norm_g, k_norm_g, w_out,
              mem_norm_x_g, mem_norm_m_g, w_mem_q, w_mem_kv, mem_q_norm_g, mem_k_norm_g, w_mem_o,
              ffn_norm_g, w_up, w_ffn_dw, b_ffn_dw, w_down):
    cos, sin = rope_tables(positions, MLA_ROPE)
    cos, sin = cos[:, :, None, :], sin[:, :, None, :]
    for l in range(DEPTH):
        h = rms_norm(x, mix_norm_g[l])
        x = x + hybrid_mixer(h, cos, sin, w_in[l], b_conv_in[l], w_conv_dw[l], b_conv_dw[l],
                             conv_ln_g[l], conv_ln_b[l], q_lat_norm_g[l], w_uq[l],
                             kv_lat_norm_g[l], w_ukv[l], q_norm_g[l], k_norm_g[l], w_out[l])
        hq = rms_norm(x, mem_norm_x_g[l])
        hm = rms_norm(mem, mem_norm_m_g[l])
        x = x + memory_cross_attention(hq, hm, w_mem_q[l], w_mem_kv[l], mem_q_norm_g[l],
                                       mem_k_norm_g[l], w_mem_o[l])
        h = rms_norm(x, ffn_norm_g[l])
        x = x + conv_gated_ffn(h, w_up[l], w_ffn_dw[l], b_ffn_dw[l], w_down[l])
    return x


import jax as _jax
import jax.numpy as _jnp

TWIN_FORMAT = 'train_step'
FWD_PARAMS = ['x', 'mem', 'positions', 'mix_norm_g', 'w_in', 'b_conv_in', 'w_conv_dw', 'b_conv_dw', 'conv_ln_g', 'conv_ln_b', 'q_lat_norm_g', 'w_uq', 'kv_lat_norm_g', 'w_ukv', 'q_norm_g', 'k_norm_g', 'w_out', 'mem_norm_x_g', 'mem_norm_m_g', 'w_mem_q', 'w_mem_kv', 'mem_q_norm_g', 'mem_k_norm_g', 'w_mem_o', 'ffn_norm_g', 'w_up', 'w_ffn_dw', 'b_ffn_dw', 'w_down']
TWIN_WEIGHTS = ['mix_norm_g', 'w_in', 'b_conv_in', 'w_conv_dw', 'b_conv_dw', 'conv_ln_g', 'conv_ln_b', 'q_lat_norm_g', 'w_uq', 'kv_lat_norm_g', 'w_ukv', 'q_norm_g', 'k_norm_g', 'w_out', 'mem_norm_x_g', 'mem_norm_m_g', 'w_mem_q', 'w_mem_kv', 'mem_q_norm_g', 'mem_k_norm_g', 'w_mem_o', 'ffn_norm_g', 'w_up', 'w_ffn_dw', 'b_ffn_dw', 'w_down']
TWIN_DIFF_INPUT = 'x'
TWIN_INPUTS = ['x', 'mem', 'positions', 'mix_norm_g', 'w_in', 'b_conv_in', 'w_conv_dw', 'b_conv_dw', 'conv_ln_g', 'conv_ln_b', 'q_lat_norm_g', 'w_uq', 'kv_lat_norm_g', 'w_ukv', 'q_norm_g', 'k_norm_g', 'w_out', 'mem_norm_x_g', 'mem_norm_m_g', 'w_mem_q', 'w_mem_kv', 'mem_q_norm_g', 'mem_k_norm_g', 'w_mem_o', 'ffn_norm_g', 'w_up', 'w_ffn_dw', 'b_ffn_dw', 'w_down', 'loss_target', 'm_mix_norm_g', 'm_w_in', 'm_b_conv_in', 'm_w_conv_dw', 'm_b_conv_dw', 'm_conv_ln_g', 'm_conv_ln_b', 'm_q_lat_norm_g', 'm_w_uq', 'm_kv_lat_norm_g', 'm_w_ukv', 'm_q_norm_g', 'm_k_norm_g', 'm_w_out', 'm_mem_norm_x_g', 'm_mem_norm_m_g', 'm_w_mem_q', 'm_w_mem_kv', 'm_mem_q_norm_g', 'm_mem_k_norm_g', 'm_w_mem_o', 'm_ffn_norm_g', 'm_w_up', 'm_w_ffn_dw', 'm_b_ffn_dw', 'm_w_down', 'v_mix_norm_g', 'v_w_in', 'v_b_conv_in', 'v_w_conv_dw', 'v_b_conv_dw', 'v_conv_ln_g', 'v_conv_ln_b', 'v_q_lat_norm_g', 'v_w_uq', 'v_kv_lat_norm_g', 'v_w_ukv', 'v_q_norm_g', 'v_k_norm_g', 'v_w_out', 'v_mem_norm_x_g', 'v_mem_norm_m_g', 'v_w_mem_q', 'v_w_mem_kv', 'v_mem_q_norm_g', 'v_mem_k_norm_g', 'v_w_mem_o', 'v_ffn_norm_g', 'v_w_up', 'v_w_ffn_dw', 'v_b_ffn_dw', 'v_w_down']
TWIN_OUTPUTS = ['loss', 'grad_x', 'grad_mix_norm_g', 'grad_w_in', 'grad_b_conv_in', 'grad_w_conv_dw', 'grad_b_conv_dw', 'grad_conv_ln_g', 'grad_conv_ln_b', 'grad_q_lat_norm_g', 'grad_w_uq', 'grad_kv_lat_norm_g', 'grad_w_ukv', 'grad_q_norm_g', 'grad_k_norm_g', 'grad_w_out', 'grad_mem_norm_x_g', 'grad_mem_norm_m_g', 'grad_w_mem_q', 'grad_w_mem_kv', 'grad_mem_q_norm_g', 'grad_mem_k_norm_g', 'grad_w_mem_o', 'grad_ffn_norm_g', 'grad_w_up', 'grad_w_ffn_dw', 'grad_b_ffn_dw', 'grad_w_down', 'delta_mix_norm_g', 'delta_w_in', 'delta_b_conv_in', 'delta_w_conv_dw', 'delta_b_conv_dw', 'delta_conv_ln_g', 'delta_conv_ln_b', 'delta_q_lat_norm_g', 'delta_w_uq', 'delta_kv_lat_norm_g', 'delta_w_ukv', 'delta_q_norm_g', 'delta_k_norm_g', 'delta_w_out', 'delta_mem_norm_x_g', 'delta_mem_norm_m_g', 'delta_w_mem_q', 'delta_w_mem_kv', 'delta_mem_q_norm_g', 'delta_mem_k_norm_g', 'delta_w_mem_o', 'delta_ffn_norm_g', 'delta_w_up', 'delta_w_ffn_dw', 'delta_b_ffn_dw', 'delta_w_down', 'new_m_mix_norm_g', 'new_m_w_in', 'new_m_b_conv_in', 'new_m_w_conv_dw', 'new_m_b_conv_dw', 'new_m_conv_ln_g', 'new_m_conv_ln_b', 'new_m_q_lat_norm_g', 'new_m_w_uq', 'new_m_kv_lat_norm_g', 'new_m_w_ukv', 'new_m_q_norm_g', 'new_m_k_norm_g', 'new_m_w_out', 'new_m_mem_norm_x_g', 'new_m_mem_norm_m_g', 'new_m_w_mem_q', 'new_m_w_mem_kv', 'new_m_mem_q_norm_g', 'new_m_mem_k_norm_g', 'new_m_w_mem_o', 'new_m_ffn_norm_g', 'new_m_w_up', 'new_m_w_ffn_dw', 'new_m_b_ffn_dw', 'new_m_w_down', 'new_v_mix_norm_g', 'new_v_w_in', 'new_v_b_conv_in', 'new_v_w_conv_dw', 'new_v_b_conv_dw', 'new_v_conv_ln_g', 'new_v_conv_ln_b', 'new_v_q_lat_norm_g', 'new_v_w_uq', 'new_v_kv_lat_norm_g', 'new_v_w_ukv', 'new_v_q_norm_g', 'new_v_k_norm_g', 'new_v_w_out', 'new_v_mem_norm_x_g', 'new_v_mem_norm_m_g', 'new_v_w_mem_q', 'new_v_w_mem_kv', 'new_v_mem_q_norm_g', 'new_v_mem_k_norm_g', 'new_v_w_mem_o', 'new_v_ffn_norm_g', 'new_v_w_up', 'new_v_w_ffn_dw', 'new_v_b_ffn_dw', 'new_v_w_down']
TWIN_LEAF_KINDS = {'loss': 'loss', 'grad_x': 'grad_x', 'grad_mix_norm_g': 'grad_w', 'grad_w_in': 'grad_w', 'grad_b_conv_in': 'grad_w', 'grad_w_conv_dw': 'grad_w', 'grad_b_conv_dw': 'grad_w', 'grad_conv_ln_g': 'grad_w', 'grad_conv_ln_b': 'grad_w', 'grad_q_lat_norm_g': 'grad_w', 'grad_w_uq': 'grad_w', 'grad_kv_lat_norm_g': 'grad_w', 'grad_w_ukv': 'grad_w', 'grad_q_norm_g': 'grad_w', 'grad_k_norm_g': 'grad_w', 'grad_w_out': 'grad_w', 'grad_mem_norm_x_g': 'grad_w', 'grad_mem_norm_m_g': 'grad_w', 'grad_w_mem_q': 'grad_w', 'grad_w_mem_kv': 'grad_w', 'grad_mem_q_norm_g': 'grad_w', 'grad_mem_k_norm_g': 'grad_w', 'grad_w_mem_o': 'grad_w', 'grad_ffn_norm_g': 'grad_w', 'grad_w_up': 'grad_w', 'grad_w_ffn_dw': 'grad_w', 'grad_b_ffn_dw': 'grad_w', 'grad_w_down': 'grad_w', 'delta_mix_norm_g': 'delta_w', 'delta_w_in': 'delta_w', 'delta_b_conv_in': 'delta_w', 'delta_w_conv_dw': 'delta_w', 'delta_b_conv_dw': 'delta_w', 'delta_conv_ln_g': 'delta_w', 'delta_conv_ln_b': 'delta_w', 'delta_q_lat_norm_g': 'delta_w', 'delta_w_uq': 'delta_w', 'delta_kv_lat_norm_g': 'delta_w', 'delta_w_ukv': 'delta_w', 'delta_q_norm_g': 'delta_w', 'delta_k_norm_g': 'delta_w', 'delta_w_out': 'delta_w', 'delta_mem_norm_x_g': 'delta_w', 'delta_mem_norm_m_g': 'delta_w', 'delta_w_mem_q': 'delta_w', 'delta_w_mem_kv': 'delta_w', 'delta_mem_q_norm_g': 'delta_w', 'delta_mem_k_norm_g': 'delta_w', 'delta_w_mem_o': 'delta_w', 'delta_ffn_norm_g': 'delta_w', 'delta_w_up': 'delta_w', 'delta_w_ffn_dw': 'delta_w', 'delta_b_ffn_dw': 'delta_w', 'delta_w_down': 'delta_w', 'new_m_mix_norm_g': 'new_m', 'new_m_w_in': 'new_m', 'new_m_b_conv_in': 'new_m', 'new_m_w_conv_dw': 'new_m', 'new_m_b_conv_dw': 'new_m', 'new_m_conv_ln_g': 'new_m', 'new_m_conv_ln_b': 'new_m', 'new_m_q_lat_norm_g': 'new_m', 'new_m_w_uq': 'new_m', 'new_m_kv_lat_norm_g': 'new_m', 'new_m_w_ukv': 'new_m', 'new_m_q_norm_g': 'new_m', 'new_m_k_norm_g': 'new_m', 'new_m_w_out': 'new_m', 'new_m_mem_norm_x_g': 'new_m', 'new_m_mem_norm_m_g': 'new_m', 'new_m_w_mem_q': 'new_m', 'new_m_w_mem_kv': 'new_m', 'new_m_mem_q_norm_g': 'new_m', 'new_m_mem_k_norm_g': 'new_m', 'new_m_w_mem_o': 'new_m', 'new_m_ffn_norm_g': 'new_m', 'new_m_w_up': 'new_m', 'new_m_w_ffn_dw': 'new_m', 'new_m_b_ffn_dw': 'new_m', 'new_m_w_down': 'new_m', 'new_v_mix_norm_g': 'new_v', 'new_v_w_in': 'new_v', 'new_v_b_conv_in': 'new_v', 'new_v_w_conv_dw': 'new_v', 'new_v_b_conv_dw': 'new_v', 'new_v_conv_ln_g': 'new_v', 'new_v_conv_ln_b': 'new_v', 'new_v_q_lat_norm_g': 'new_v', 'new_v_w_uq': 'new_v', 'new_v_kv_lat_norm_g': 'new_v', 'new_v_w_ukv': 'new_v', 'new_v_q_norm_g': 'new_v', 'new_v_k_norm_g': 'new_v', 'new_v_w_out': 'new_v', 'new_v_mem_norm_x_g': 'new_v', 'new_v_mem_norm_m_g': 'new_v', 'new_v_w_mem_q': 'new_v', 'new_v_w_mem_kv': 'new_v', 'new_v_mem_q_norm_g': 'new_v', 'new_v_mem_k_norm_g': 'new_v', 'new_v_w_mem_o': 'new_v', 'new_v_ffn_norm_g': 'new_v', 'new_v_w_up': 'new_v', 'new_v_w_ffn_dw': 'new_v', 'new_v_b_ffn_dw': 'new_v', 'new_v_w_down': 'new_v'}


def _forward(args):
    return _fwd_reference(*[args[k] for k in FWD_PARAMS])


def _output_shape():
    def fwd():
        inp = _fwd_setup_inputs(0)
        return _fwd_reference(*[inp[k] for k in FWD_PARAMS])
    out = _jax.eval_shape(fwd)
    return out.shape, out.dtype

N_MICROBATCH = 1
ADAM_LR = 0.001
ADAM_B1 = 0.9
ADAM_B2 = 0.999
ADAM_EPS = 1e-08
ADAM_WD = 0.01
ADAM_STEP = 10
PER_EXAMPLE_BATCH_AXIS = {'x': 0, 'mem': 0, 'positions': 0, 'loss_target': 0}
SHARED_INPUTS = []
_WEIGHT_DTYPES = {'mix_norm_g': _jnp.float32, 'w_in': _jnp.float32, 'b_conv_in': _jnp.float32, 'w_conv_dw': _jnp.float32, 'b_conv_dw': _jnp.float32, 'conv_ln_g': _jnp.float32, 'conv_ln_b': _jnp.float32, 'q_lat_norm_g': _jnp.float32, 'w_uq': _jnp.float32, 'kv_lat_norm_g': _jnp.float32, 'w_ukv': _jnp.float32, 'q_norm_g': _jnp.float32, 'k_norm_g': _jnp.float32, 'w_out': _jnp.float32, 'mem_norm_x_g': _jnp.float32, 'mem_norm_m_g': _jnp.float32, 'w_mem_q': _jnp.float32, 'w_mem_kv': _jnp.float32, 'mem_q_norm_g': _jnp.float32, 'mem_k_norm_g': _jnp.float32, 'w_mem_o': _jnp.float32, 'ffn_norm_g': _jnp.float32, 'w_up': _jnp.float32, 'w_ffn_dw': _jnp.float32, 'b_ffn_dw': _jnp.float32, 'w_down': _jnp.float32}
MOMENT_SCALE = {'mix_norm_g': 3.926688e-01, 'w_in': 3.095627e-01, 'b_conv_in': 7.970704e+00, 'w_conv_dw': 1.185680e+00, 'b_conv_dw': 2.039199e+01, 'conv_ln_g': 2.884556e+01, 'conv_ln_b': 2.118571e+01, 'q_lat_norm_g': 1.514039e-01, 'w_uq': 8.974302e-02, 'kv_lat_norm_g': 1.848748e+00, 'w_ukv': 1.390169e-01, 'q_norm_g': 2.125654e+00, 'k_norm_g': 2.124591e+00, 'w_out': 3.167045e+00, 'mem_norm_x_g': 1.265632e-01, 'mem_norm_m_g': 8.498720e-01, 'w_mem_q': 1.146046e-01, 'w_mem_kv': 2.984166e-01, 'mem_q_norm_g': 2.687653e+00, 'mem_k_norm_g': 2.688170e+00, 'w_mem_o': 4.694075e-01, 'ffn_norm_g': 5.405769e+01, 'w_up': 1.184247e+00, 'w_ffn_dw': 7.828395e+00, 'b_ffn_dw': 6.947320e+00, 'w_down': 7.423182e-01}


def _to_microbatches(a, axis):
    t = _jnp.moveaxis(a, axis, 0)
    t = t.reshape((N_MICROBATCH, t.shape[0] // N_MICROBATCH) + t.shape[1:])
    return _jnp.moveaxis(t, 1, axis + 1)


def setup_inputs(seed: int = 0) -> dict:
    inp = _fwd_setup_inputs(seed)
    key = _jax.random.fold_in(_jax.random.key(seed), 7919)
    shape, _ = _output_shape()
    out = dict(inp)
    out["loss_target"] = _jax.random.normal(_jax.random.fold_in(key, 0), shape, _jnp.float32)
    for i, name in enumerate(TWIN_WEIGHTS):
        w = inp[name].astype(_jnp.float32)
        if MOMENT_SCALE is None:
            s = _jnp.sqrt(_jnp.mean(_jnp.square(w)) + 1e-30)
        else:
            s = MOMENT_SCALE[name]
        km, kv = _jax.random.split(_jax.random.fold_in(key, i + 1))
        out[name] = w
        out["m_" + name] = s * _jax.random.normal(km, w.shape, _jnp.float32)
        out["v_" + name] = (s * s) * _jax.random.uniform(kv, w.shape, _jnp.float32, 0.5, 1.5)
    if N_MICROBATCH > 1:
        for name, axis in PER_EXAMPLE_BATCH_AXIS.items():
            out[name] = _to_microbatches(out[name], axis)
    return {'x': out['x'], 'mem': out['mem'], 'positions': out['positions'], 'mix_norm_g': out['mix_norm_g'], 'w_in': out['w_in'], 'b_conv_in': out['b_conv_in'], 'w_conv_dw': out['w_conv_dw'], 'b_conv_dw': out['b_conv_dw'], 'conv_ln_g': out['conv_ln_g'], 'conv_ln_b': out['conv_ln_b'], 'q_lat_norm_g': out['q_lat_norm_g'], 'w_uq': out['w_uq'], 'kv_lat_norm_g': out['kv_lat_norm_g'], 'w_ukv': out['w_ukv'], 'q_norm_g': out['q_norm_g'], 'k_norm_g': out['k_norm_g'], 'w_out': out['w_out'], 'mem_norm_x_g': out['mem_norm_x_g'], 'mem_norm_m_g': out['mem_norm_m_g'], 'w_mem_q': out['w_mem_q'], 'w_mem_kv': out['w_mem_kv'], 'mem_q_norm_g': out['mem_q_norm_g'], 'mem_k_norm_g': out['mem_k_norm_g'], 'w_mem_o': out['w_mem_o'], 'ffn_norm_g': out['ffn_norm_g'], 'w_up': out['w_up'], 'w_ffn_dw': out['w_ffn_dw'], 'b_ffn_dw': out['b_ffn_dw'], 'w_down': out['w_down'], 'loss_target': out['loss_target'], 'm_mix_norm_g': out['m_mix_norm_g'], 'm_w_in': out['m_w_in'], 'm_b_conv_in': out['m_b_conv_in'], 'm_w_conv_dw': out['m_w_conv_dw'], 'm_b_conv_dw': out['m_b_conv_dw'], 'm_conv_ln_g': out['m_conv_ln_g'], 'm_conv_ln_b': out['m_conv_ln_b'], 'm_q_lat_norm_g': out['m_q_lat_norm_g'], 'm_w_uq': out['m_w_uq'], 'm_kv_lat_norm_g': out['m_kv_lat_norm_g'], 'm_w_ukv': out['m_w_ukv'], 'm_q_norm_g': out['m_q_norm_g'], 'm_k_norm_g': out['m_k_norm_g'], 'm_w_out': out['m_w_out'], 'm_mem_norm_x_g': out['m_mem_norm_x_g'], 'm_mem_norm_m_g': out['m_mem_norm_m_g'], 'm_w_mem_q': out['m_w_mem_q'], 'm_w_mem_kv': out['m_w_mem_kv'], 'm_mem_q_norm_g': out['m_mem_q_norm_g'], 'm_mem_k_norm_g': out['m_mem_k_norm_g'], 'm_w_mem_o': out['m_w_mem_o'], 'm_ffn_norm_g': out['m_ffn_norm_g'], 'm_w_up': out['m_w_up'], 'm_w_ffn_dw': out['m_w_ffn_dw'], 'm_b_ffn_dw': out['m_b_ffn_dw'], 'm_w_down': out['m_w_down'], 'v_mix_norm_g': out['v_mix_norm_g'], 'v_w_in': out['v_w_in'], 'v_b_conv_in': out['v_b_conv_in'], 'v_w_conv_dw': out['v_w_conv_dw'], 'v_b_conv_dw': out['v_b_conv_dw'], 'v_conv_ln_g': out['v_conv_ln_g'], 'v_conv_ln_b': out['v_conv_ln_b'], 'v_q_lat_norm_g': out['v_q_lat_norm_g'], 'v_w_uq': out['v_w_uq'], 'v_kv_lat_norm_g': out['v_kv_lat_norm_g'], 'v_w_ukv': out['v_w_ukv'], 'v_q_norm_g': out['v_q_norm_g'], 'v_k_norm_g': out['v_k_norm_g'], 'v_w_out': out['v_w_out'], 'v_mem_norm_x_g': out['v_mem_norm_x_g'], 'v_mem_norm_m_g': out['v_mem_norm_m_g'], 'v_w_mem_q': out['v_w_mem_q'], 'v_w_mem_kv': out['v_w_mem_kv'], 'v_mem_q_norm_g': out['v_mem_q_norm_g'], 'v_mem_k_norm_g': out['v_mem_k_norm_g'], 'v_w_mem_o': out['v_w_mem_o'], 'v_ffn_norm_g': out['v_ffn_norm_g'], 'v_w_up': out['v_w_up'], 'v_w_ffn_dw': out['v_w_ffn_dw'], 'v_b_ffn_dw': out['v_b_ffn_dw'], 'v_w_down': out['v_w_down']}


def _loss(weights, diff, rest, loss_target):
    with _jax.named_scope("forward"):
        args = {**rest, TWIN_DIFF_INPUT: diff, **{k: w.astype(_WEIGHT_DTYPES[k]) for k, w in weights.items()}}
        y = _forward(args)
    with _jax.named_scope("loss_head"):
        err = _jnp.square(y.astype(_jnp.float32) - loss_target)
        return 0.5 * _jnp.sum(_jnp.mean(err, axis=-1)) if err.ndim else 0.5 * err


def _adamw(w, g, m, v):
    m = ADAM_B1 * m + (1.0 - ADAM_B1) * g
    v = ADAM_B2 * v + (1.0 - ADAM_B2) * _jnp.square(g)
    m_hat = m / (1.0 - ADAM_B1 ** ADAM_STEP)
    v_hat = v / (1.0 - ADAM_B2 ** ADAM_STEP)
    delta = -ADAM_LR * (m_hat / (_jnp.sqrt(v_hat) + ADAM_EPS) + ADAM_WD * w)
    return delta, m, v


def reference(x, mem, positions, mix_norm_g, w_in, b_conv_in, w_conv_dw, b_conv_dw, conv_ln_g, conv_ln_b, q_lat_norm_g, w_uq, kv_lat_norm_g, w_ukv, q_norm_g, k_norm_g, w_out, mem_norm_x_g, mem_norm_m_g, w_mem_q, w_mem_kv, mem_q_norm_g, mem_k_norm_g, w_mem_o, ffn_norm_g, w_up, w_ffn_dw, b_ffn_dw, w_down, loss_target, m_mix_norm_g, m_w_in, m_b_conv_in, m_w_conv_dw, m_b_conv_dw, m_conv_ln_g, m_conv_ln_b, m_q_lat_norm_g, m_w_uq, m_kv_lat_norm_g, m_w_ukv, m_q_norm_g, m_k_norm_g, m_w_out, m_mem_norm_x_g, m_mem_norm_m_g, m_w_mem_q, m_w_mem_kv, m_mem_q_norm_g, m_mem_k_norm_g, m_w_mem_o, m_ffn_norm_g, m_w_up, m_w_ffn_dw, m_b_ffn_dw, m_w_down, v_mix_norm_g, v_w_in, v_b_conv_in, v_w_conv_dw, v_b_conv_dw, v_conv_ln_g, v_conv_ln_b, v_q_lat_norm_g, v_w_uq, v_kv_lat_norm_g, v_w_ukv, v_q_norm_g, v_k_norm_g, v_w_out, v_mem_norm_x_g, v_mem_norm_m_g, v_w_mem_q, v_w_mem_kv, v_mem_q_norm_g, v_mem_k_norm_g, v_w_mem_o, v_ffn_norm_g, v_w_up, v_w_ffn_dw, v_b_ffn_dw, v_w_down):
    given = dict(x=x, mem=mem, positions=positions, mix_norm_g=mix_norm_g, w_in=w_in, b_conv_in=b_conv_in, w_conv_dw=w_conv_dw, b_conv_dw=b_conv_dw, conv_ln_g=conv_ln_g, conv_ln_b=conv_ln_b, q_lat_norm_g=q_lat_norm_g, w_uq=w_uq, kv_lat_norm_g=kv_lat_norm_g, w_ukv=w_ukv, q_norm_g=q_norm_g, k_norm_g=k_norm_g, w_out=w_out, mem_norm_x_g=mem_norm_x_g, mem_norm_m_g=mem_norm_m_g, w_mem_q=w_mem_q, w_mem_kv=w_mem_kv, mem_q_norm_g=mem_q_norm_g, mem_k_norm_g=mem_k_norm_g, w_mem_o=w_mem_o, ffn_norm_g=ffn_norm_g, w_up=w_up, w_ffn_dw=w_ffn_dw, b_ffn_dw=b_ffn_dw, w_down=w_down, loss_target=loss_target, m_mix_norm_g=m_mix_norm_g, m_w_in=m_w_in, m_b_conv_in=m_b_conv_in, m_w_conv_dw=m_w_conv_dw, m_b_conv_dw=m_b_conv_dw, m_conv_ln_g=m_conv_ln_g, m_conv_ln_b=m_conv_ln_b, m_q_lat_norm_g=m_q_lat_norm_g, m_w_uq=m_w_uq, m_kv_lat_norm_g=m_kv_lat_norm_g, m_w_ukv=m_w_ukv, m_q_norm_g=m_q_norm_g, m_k_norm_g=m_k_norm_g, m_w_out=m_w_out, m_mem_norm_x_g=m_mem_norm_x_g, m_mem_norm_m_g=m_mem_norm_m_g, m_w_mem_q=m_w_mem_q, m_w_mem_kv=m_w_mem_kv, m_mem_q_norm_g=m_mem_q_norm_g, m_mem_k_norm_g=m_mem_k_norm_g, m_w_mem_o=m_w_mem_o, m_ffn_norm_g=m_ffn_norm_g, m_w_up=m_w_up, m_w_ffn_dw=m_w_ffn_dw, m_b_ffn_dw=m_b_ffn_dw, m_w_down=m_w_down, v_mix_norm_g=v_mix_norm_g, v_w_in=v_w_in, v_b_conv_in=v_b_conv_in, v_w_conv_dw=v_w_conv_dw, v_b_conv_dw=v_b_conv_dw, v_conv_ln_g=v_conv_ln_g, v_conv_ln_b=v_conv_ln_b, v_q_lat_norm_g=v_q_lat_norm_g, v_w_uq=v_w_uq, v_kv_lat_norm_g=v_kv_lat_norm_g, v_w_ukv=v_w_ukv, v_q_norm_g=v_q_norm_g, v_k_norm_g=v_k_norm_g, v_w_out=v_w_out, v_mem_norm_x_g=v_mem_norm_x_g, v_mem_norm_m_g=v_mem_norm_m_g, v_w_mem_q=v_w_mem_q, v_w_mem_kv=v_w_mem_kv, v_mem_q_norm_g=v_mem_q_norm_g, v_mem_k_norm_g=v_mem_k_norm_g, v_w_mem_o=v_w_mem_o, v_ffn_norm_g=v_ffn_norm_g, v_w_up=v_w_up, v_w_ffn_dw=v_w_ffn_dw, v_b_ffn_dw=v_b_ffn_dw, v_w_down=v_w_down)
    weights = {n: given[n] for n in TWIN_WEIGHTS}
    shared = {n: given[n] for n in SHARED_INPUTS}
    per_example = {n: given[n] for n in ['x', 'mem', 'positions']}
    grad_fn = _jax.value_and_grad(_loss, argnums=(0, 1))

    def one_microbatch(ex, loss_target):
        ex = dict(ex)
        diff = ex.pop(TWIN_DIFF_INPUT)
        return grad_fn(weights, diff, {**shared, **ex}, loss_target)

    if N_MICROBATCH == 1:
        loss, (grad_w, grad_x) = one_microbatch(per_example, given["loss_target"])
    else:
        def body(carry, xs):
            loss_sum, grad_sum = carry
            l_k, (gw_k, gx_k) = one_microbatch(xs[0], xs[1])
            with _jax.named_scope("update"):
                return (loss_sum + l_k, _jax.tree.map(_jnp.add, grad_sum, gw_k)), gx_k

        init = (_jnp.zeros((), _jnp.float32), _jax.tree.map(_jnp.zeros_like, weights))
        (loss, grad_w), grad_x = _jax.lax.scan(body, init, (per_example, given["loss_target"]))
    with _jax.named_scope("update"):
        delta_w, new_m, new_v = {}, {}, {}
        for n in TWIN_WEIGHTS:
            delta_w[n], new_m[n], new_v[n] = _adamw(weights[n], grad_w[n], given["m_" + n], given["v_" + n])
    return (loss, grad_x, *[grad_w[n] for n in TWIN_WEIGHTS], *[delta_w[n] for n in TWIN_WEIGHTS],
            *[new_m[n] for n in TWIN_WEIGHTS], *[new_v[n] for n in TWIN_WEIGHTS])
```

```python
import math

import numpy as np
import jax
import jax.numpy as jnp
from jax import lax
from jax.experimental import pallas as pl
from jax.experimental.pallas import tpu as pltpu

F32 = jnp.float32
BF16 = jnp.bfloat16
SDS = jax.ShapeDtypeStruct
MESH = pl.DeviceIdType.MESH

D_MODEL = 1024
EPS = 1e-6
CONV_CH = 512
CONV_WIDTH = 31
CONV_HALO = 32
HEADS = 8
NOPE = 64
ROPE = 32
HEAD_DIM = NOPE + ROPE
HEAD_PAD = 128
Q_RANK = 256
KV_RANK = 128
CHUNK = 64
ROPE_THETA = 10000.0
IN_COLS_PAD = 1536
MEM_HEADS = 4
MEM_HEAD_DIM = 256
MEM_LEN = 256
D_FF = 2816
FFN_HALO = 8
ATT_SCALE = 1.0 / math.sqrt(HEAD_DIM)

ADAM_LR = 0.001
ADAM_B1 = 0.9
ADAM_B2 = 0.999
ADAM_EPS = 1e-08
ADAM_WD = 0.01
ADAM_STEP = 10

VMEM_LIMIT_V7X = 56 * 1024 * 1024
PACK_COLS = 1024
PACK_ROWS = 3840
SMALL_COLS = 128


def _cp(n_axes):
    return pltpu.CompilerParams(dimension_semantics=("arbitrary",) * n_axes, vmem_limit_bytes=VMEM_LIMIT_V7X)


def _row_tile(s, want):
    return want if s % want == 0 else s


def _norm_linear(x, xcol, kdim, g, w, out_dtype, tm, tn, name):
    s = x.shape[0]
    n = w.shape[1]

    def body(x_ref, g_ref, w_ref, y_ref, hn_ref):
        @pl.when(pl.program_id(1) == 0)
        def _():
            xv = x_ref[...]
            r = lax.rsqrt(jnp.mean(xv * xv, axis=-1, keepdims=True) + EPS)
            hn_ref[...] = ((xv * r) * g_ref[...]).astype(BF16)

        y_ref[...] = jnp.dot(hn_ref[...], w_ref[...], preferred_element_type=F32).astype(y_ref.dtype)

    return pl.pallas_call(
        body, grid=(s // tm, n // tn),
        in_specs=[pl.BlockSpec((tm, kdim), lambda i, j: (i, xcol)), pl.BlockSpec((1, kdim), lambda i, j: (0, 0)),
                  pl.BlockSpec((kdim, tn), lambda i, j: (0, j))],
        out_specs=[pl.BlockSpec((tm, tn), lambda i, j: (i, j)), pl.BlockSpec((tm, kdim), lambda i, j: (i, 0))],
        out_shape=[SDS((s, n), out_dtype), SDS((s, kdim), BF16)],
        compiler_params=_cp(2), name=name)(x, g, w)


def _linear(pairs, nt, residual, out_dtypes, tm, tn, name):
    s = pairs[0][0].shape[0]
    n = pairs[0][1].shape[0] if nt else pairs[0][1].shape[1]
    n_pairs = len(pairs)
    has_res = residual is not None

    def body(*refs):
        a_refs = refs[:n_pairs]
        w_refs = refs[n_pairs:2 * n_pairs]
        res_ref = refs[2 * n_pairs] if has_res else None
        outs = refs[2 * n_pairs + int(has_res):]
        acc = None
        for a_ref, w_ref in zip(a_refs, w_refs):
            a = a_ref[...].astype(BF16)
            if nt:
                d = lax.dot_general(a, w_ref[...], (((1,), (1,)), ((), ())), preferred_element_type=F32)
            else:
                d = jnp.dot(a, w_ref[...], preferred_element_type=F32)
            acc = d if acc is None else acc + d
        if has_res:
            acc = res_ref[...] + acc
        for o in outs:
            o[...] = acc.astype(o.dtype)

    in_specs = [pl.BlockSpec((tm, a.shape[1]), lambda i, j: (i, 0)) for a, _ in pairs]
    if nt:
        in_specs += [pl.BlockSpec((tn, w.shape[1]), lambda i, j: (j, 0)) for _, w in pairs]
    else:
        in_specs += [pl.BlockSpec((w.shape[0], tn), lambda i, j: (0, j)) for _, w in pairs]
    args = [a for a, _ in pairs] + [w for _, w in pairs]
    if has_res:
        in_specs.append(pl.BlockSpec((tm, tn), lambda i, j: (i, j)))
        args.append(residual)
    outs = pl.pallas_call(
        body, grid=(s // tm, n // tn), in_specs=in_specs,
        out_specs=[pl.BlockSpec((tm, tn), lambda i, j: (i, j)) for _ in out_dtypes],
        out_shape=[SDS((s, n), dt) for dt in out_dtypes],
        compiler_params=_cp(2), name=name)(*args)
    return outs


def _linear_normbwd(pairs, x, xcol, g, d_res, out_dtypes, tm, name):
    s = pairs[0][0].shape[0]
    dn = pairs[0][1].shape[0]
    n_pairs = len(pairs)
    has_res = d_res is not None

    def body(*refs):
        a_refs = refs[:n_pairs]
        w_refs = refs[n_pairs:2 * n_pairs]
        x_ref, g_ref = refs[2 * n_pairs], refs[2 * n_pairs + 1]
        k = 2 * n_pairs + 2
        res_ref = refs[k] if has_res else None
        k += int(has_res)
        outs = refs[k:-1]
        dg_ref = refs[-1]
        dh = None
        for a_ref, w_ref in zip(a_refs, w_refs):
            d = lax.dot_general(a_ref[...].astype(BF16), w_ref[...], (((1,), (1,)), ((), ())), preferred_element_type=F32)
            dh = d if dh is None else dh + d
        xv = x_ref[...]
        r = lax.rsqrt(jnp.mean(xv * xv, axis=-1, keepdims=True) + EPS)
        y = xv * r

        @pl.when(pl.program_id(0) == 0)
        def _():
            dg_ref[...] = jnp.zeros_like(dg_ref)

        dg_ref[...] += jnp.sum(dh * y, axis=0, keepdims=True)
        dy = dh * g_ref[...]
        dx = r * (dy - y * jnp.mean(dy * y, axis=-1, keepdims=True))
        if has_res:
            dx = res_ref[...] + dx
        for o in outs:
            o[...] = dx.astype(o.dtype)

    in_specs = [pl.BlockSpec((tm, a.shape[1]), lambda i: (i, 0)) for a, _ in pairs]
    in_specs += [pl.BlockSpec((dn, w.shape[1]), lambda i: (0, 0)) for _, w in pairs]
    in_specs += [pl.BlockSpec((tm, dn), lambda i: (i, xcol)), pl.BlockSpec((1, dn), lambda i: (0, 0))]
    args = [a for a, _ in pairs] + [w for _, w in pairs] + [x, g]
    if has_res:
        in_specs.append(pl.BlockSpec((tm, dn), lambda i: (i, 0)))
        args.append(d_res)
    outs = pl.pallas_call(
        body, grid=(s // tm,), in_specs=in_specs,
        out_specs=[pl.BlockSpec((tm, dn), lambda i: (i, 0)) for _ in out_dtypes] + [pl.BlockSpec((1, dn), lambda i: (0, 0))],
        out_shape=[SDS((s, dn), dt) for dt in out_dtypes] + [SDS((1, dn), F32)],
        compiler_params=_cp(1), name=name)(*args)
    return outs


def _dw_matmul(a, b, tk, tn, ts, name):
    s, ka = a.shape
    n = b.shape[1]

    def body(a_ref, b_ref, o_ref):
        @pl.when(pl.program_id(2) == 0)
        def _():
            o_ref[...] = jnp.zeros_like(o_ref)

        o_ref[...] += lax.dot_general(a_ref[...].astype(BF16), b_ref[...].astype(BF16), (((0,), (0,)), ((), ())),
                                      preferred_element_type=F32)

    return pl.pallas_call(
        body, grid=(ka // tk, n // tn, s // ts),
        in_specs=[pl.BlockSpec((ts, tk), lambda k, j, t: (t, k)), pl.BlockSpec((ts, tn), lambda k, j, t: (t, j))],
        out_specs=pl.BlockSpec((tk, tn), lambda k, j, t: (k, j)),
        out_shape=SDS((ka, n), F32), compiler_params=_cp(3), name=name)(a, b)


def _dw(a, b, name):
    s, ka = a.shape
    n = b.shape[1]
    tk = ka if ka <= 1024 else ka // 2
    tn = n if n <= 1024 else (n // 2 if n == D_FF else 512)
    return _dw_matmul(a, b, tk, tn, _row_tile(s, 512), name)


def _prev_halo(tm, halo):
    return lambda i: (jnp.maximum(i * (tm // halo) - 1, 0), 0)


def _next_halo(tm, halo, s):
    return lambda i: (jnp.minimum((i + 1) * (tm // halo), s // halo - 1), 0)


def _conv_fwd(z, b_in, w32, b_dw, ln_g, ln_b, tm):
    s = z.shape[0]
    c = CONV_CH

    def body(z_ref, zh_ref, bin_ref, w_ref, bdw_ref, lg_ref, lb_ref, u_ref, u0_ref, u1_ref, ext):
        i = pl.program_id(0)

        def glu(zz):
            zz = zz + bin_ref[...]
            return zz[:, :c] * jax.nn.sigmoid(zz[:, c:])

        u0 = glu(z_ref[...])
        u0_ref[...] = u0
        ext[0:CONV_HALO, :] = jnp.where(i > 0, glu(zh_ref[...]), 0.0)
        ext[CONV_HALO:, :] = u0
        off = CONV_HALO - (CONV_WIDTH - 1)
        for r in range(tm // 64):
            for cb in range(c // 128):
                cs = slice(cb * 128, (cb + 1) * 128)
                acc = jnp.zeros((64, 128), F32)
                for k in range(CONV_WIDTH):
                    acc = acc + ext[r * 64 + off + k: r * 64 + off + k + 64, cs] * w_ref[k:k + 1, cs]
                u1_ref[r * 64:(r + 1) * 64, cs] = acc + bdw_ref[:, cs]
        u1 = u1_ref[...]
        mu = jnp.mean(u1, axis=-1, keepdims=True)
        xc = u1 - mu
        y = xc * lax.rsqrt(jnp.mean(xc * xc, axis=-1, keepdims=True) + EPS)
        y = y * lg_ref[...] + lb_ref[...]
        u_ref[...] = (y * jax.nn.sigmoid(y)).astype(BF16)

    row = lambda i: (i, 0)
    fix = lambda i: (0, 0)
    return pl.pallas_call(
        body, grid=(s // tm,),
        in_specs=[pl.BlockSpec((tm, 2 * c), row), pl.BlockSpec((CONV_HALO, 2 * c), _prev_halo(tm, CONV_HALO)),
                  pl.BlockSpec((1, 2 * c), fix), pl.BlockSpec((32, c), fix), pl.BlockSpec((1, c), fix),
                  pl.BlockSpec((1, c), fix), pl.BlockSpec((1, c), fix)],
        out_specs=[pl.BlockSpec((tm, c), row)] * 3,
        out_shape=[SDS((s, c), BF16), SDS((s, c), F32), SDS((s, c), F32)],
        scratch_shapes=[pltpu.VMEM((tm + CONV_HALO, c), F32)],
        compiler_params=_cp(1), name="conv_fwd")(z, z, b_in, w32, b_dw, ln_g, ln_b)


def _conv_bwd_ln(d_u, u1, ln_g, ln_b, tm):
    s = d_u.shape[0]
    c = CONV_CH

    def body(du_ref, u1_ref, lg_ref, lb_ref, du1_ref, dlg_ref, dlb_ref, dbdw_ref):
        @pl.when(pl.program_id(0) == 0)
        def _():
            dlg_ref[...] = jnp.zeros_like(dlg_ref)
            dlb_ref[...] = jnp.zeros_like(dlb_ref)
            dbdw_ref[...] = jnp.zeros_like(dbdw_ref)

        u1 = u1_ref[...]
        mu = jnp.mean(u1, axis=-1, keepdims=True)
        xc = u1 - mu
        rs = lax.rsqrt(jnp.mean(xc * xc, axis=-1, keepdims=True) + EPS)
        xh = xc * rs
        y = xh * lg_ref[...] + lb_ref[...]
        sg = jax.nn.sigmoid(y)
        dy = du_ref[...] * (sg * (1.0 + y * (1.0 - sg)))
        dlg_ref[...] += jnp.sum(dy * xh, axis=0, keepdims=True)
        dlb_ref[...] += jnp.sum(dy, axis=0, keepdims=True)
        dxh = dy * lg_ref[...]
        du1 = rs * (dxh - jnp.mean(dxh, axis=-1, keepdims=True) - xh * jnp.mean(dxh * xh, axis=-1, keepdims=True))
        dbdw_ref[...] += jnp.sum(du1, axis=0, keepdims=True)
        du1_ref[...] = du1

    row = lambda i: (i, 0)
    fix = lambda i: (0, 0)
    return pl.pallas_call(
        body, grid=(s // tm,),
        in_specs=[pl.BlockSpec((tm, c), row), pl.BlockSpec((tm, c), row), pl.BlockSpec((1, c), fix), pl.BlockSpec((1, c), fix)],
        out_specs=[pl.BlockSpec((tm, c), row)] + [pl.BlockSpec((1, c), fix)] * 3,
        out_shape=[SDS((s, c), F32)] + [SDS((1, c), F32)] * 3,
        compiler_params=_cp(1), name="conv_bwd_ln")(d_u, u1, ln_g, ln_b)


def _conv_bwd_dw(d_u1, u0, z, b_in, w32, tm):
    s = d_u1.shape[0]
    c = CONV_CH

    def body(d_ref, dn_ref, u0_ref, u0p_ref, z_ref, bin_ref, w_ref, dz_ref, dw_ref, dbin_ref, extd, extu, du0):
        i = pl.program_id(0)
        last = pl.num_programs(0) - 1

        @pl.when(i == 0)
        def _():
            dw_ref[...] = jnp.zeros_like(dw_ref)
            dbin_ref[...] = jnp.zeros_like(dbin_ref)

        extd[0:tm, :] = d_ref[...]
        extd[tm:, :] = jnp.where(i < last, dn_ref[...], 0.0)
        extu[0:CONV_HALO, :] = jnp.where(i > 0, u0p_ref[...], 0.0)
        extu[CONV_HALO:, :] = u0_ref[...]
        off = CONV_HALO - (CONV_WIDTH - 1)
        for r in range(tm // 64):
            for cb in range(c // 128):
                cs = slice(cb * 128, (cb + 1) * 128)
                acc = jnp.zeros((64, 128), F32)
                for k in range(CONV_WIDTH):
                    o = r * 64 + (CONV_WIDTH - 1) - k
                    acc = acc + extd[o:o + 64, cs] * w_ref[k:k + 1, cs]
                du0[r * 64:(r + 1) * 64, cs] = acc
        for cb in range(c // 128):
            cs = slice(cb * 128, (cb + 1) * 128)
            for k in range(CONV_WIDTH):
                part = jnp.zeros((8, 128), F32)
                for r in range(tm // 64):
                    p = d_ref[r * 64:(r + 1) * 64, cs] * extu[r * 64 + off + k: r * 64 + off + k + 64, cs]
                    for q in range(8):
                        part = part + p[q * 8:(q + 1) * 8, :]
                dw_ref[k:k + 1, cs] += jnp.sum(part, axis=0, keepdims=True)
        zz = z_ref[...] + bin_ref[...]
        a = zz[:, :c]
        sg = jax.nn.sigmoid(zz[:, c:])
        d0 = du0[...]
        da = d0 * sg
        dgt = d0 * a * (sg * (1.0 - sg))
        dbin_ref[:, :c] += jnp.sum(da, axis=0, keepdims=True)
        dbin_ref[:, c:] += jnp.sum(dgt, axis=0, keepdims=True)
        dz_ref[:, :c] = da.astype(BF16)
        dz_ref[:, c:] = dgt.astype(BF16)

    row = lambda i: (i, 0)
    fix = lambda i: (0, 0)
    return pl.pallas_call(
        body, grid=(s // tm,),
        in_specs=[pl.BlockSpec((tm, c), row), pl.BlockSpec((CONV_HALO, c), _next_halo(tm, CONV_HALO, s)),
                  pl.BlockSpec((tm, c), row), pl.BlockSpec((CONV_HALO, c), _prev_halo(tm, CONV_HALO)),
                  pl.BlockSpec((tm, 2 * c), row), pl.BlockSpec((1, 2 * c), fix), pl.BlockSpec((32, c), fix)],
        out_specs=[pl.BlockSpec((tm, 2 * c), row), pl.BlockSpec((32, c), fix), pl.BlockSpec((1, 2 * c), fix)],
        out_shape=[SDS((s, 2 * c), BF16), SDS((32, c), F32), SDS((1, 2 * c), F32)],
        scratch_shapes=[pltpu.VMEM((tm + CONV_HALO, c), F32), pltpu.VMEM((tm + CONV_HALO, c), F32), pltpu.VMEM((tm, c), F32)],
        compiler_params=_cp(1), name="conv_bwd_dw")(d_u1, d_u1, u0, u0, z, b_in, w32)


def _partner(v, lane):
    up = pltpu.roll(v, HEAD_PAD - ROPE // 2, 1)
    dn = pltpu.roll(v, ROPE // 2, 1)
    lo = (lane >= NOPE) & (lane < NOPE + ROPE // 2)
    hi = (lane >= NOPE + ROPE // 2) & (lane < HEAD_DIM)
    return jnp.where(lo, up, jnp.where(hi, dn, 0.0))


def _mla_prep(q_raw, kv_raw, z, cosf, sinf, gq, gk, tm):
    s = q_raw.shape[0]

    def body(q_ref, kv_ref, kr_ref, c_ref, s_ref, gq_ref, gk_ref, qo_ref, ko_ref, vo_ref):
        lane = lax.broadcasted_iota(jnp.int32, (tm, HEAD_PAD), 1)
        cf = c_ref[...]
        sf = s_ref[...]

        def norm_rope(t, g_ref):
            r = lax.rsqrt(jnp.sum(t * t, axis=-1, keepdims=True) * (1.0 / HEAD_DIM) + EPS)
            tn = (t * r) * g_ref[...]
            return tn * cf + _partner(tn, lane) * sf

        q = q_ref[...]
        qo_ref[...] = (norm_rope(q, gq_ref) * ATT_SCALE).astype(BF16)
        kv = kv_ref[...]
        kpre = jnp.where(lane < NOPE, kv, 0.0) + kr_ref[...]
        ko_ref[...] = norm_rope(kpre, gk_ref).astype(BF16)
        vo_ref[...] = jnp.where(lane >= NOPE, kv, 0.0).astype(BF16)

    hb = lambda i, h: (i, h)
    return pl.pallas_call(
        body, grid=(s // tm, HEADS),
        in_specs=[pl.BlockSpec((tm, HEAD_PAD), hb), pl.BlockSpec((tm, HEAD_PAD), hb),
                  pl.BlockSpec((tm, HEAD_PAD), lambda i, h: (i, IN_COLS_PAD // HEAD_PAD - 1)),
                  pl.BlockSpec((tm, HEAD_PAD), lambda i, h: (i, 0)), pl.BlockSpec((tm, HEAD_PAD), lambda i, h: (i, 0)),
                  pl.BlockSpec((1, HEAD_PAD), lambda i, h: (0, 0)), pl.BlockSpec((1, HEAD_PAD), lambda i, h: (0, 0))],
        out_specs=[pl.BlockSpec((tm, HEAD_PAD), hb)] * 3,
        out_shape=[SDS((s, HEADS * HEAD_PAD), BF16)] * 3,
        compiler_params=_cp(2), name="mla_prep")(q_raw, kv_raw, z, cosf, sinf, gq, gk)


def _mla_prep_bwd(dqp, dkp, dvp, q_raw, kv_raw, z, cosf, sinf, gq, gk, tm):
    s = q_raw.shape[0]

    def body(dq_ref, dk_ref, dv_ref, q_ref, kv_ref, kr_ref, c_ref, s_ref, gq_ref, gk_ref,
             dqo_ref, dkvo_ref, dkr_ref, dgq_ref, dgk_ref):
        i = pl.program_id(0)
        h = pl.program_id(1)
        lane = lax.broadcasted_iota(jnp.int32, (tm, HEAD_PAD), 1)
        cf = c_ref[...]
        sf = s_ref[...]

        @pl.when((i == 0) & (h == 0))
        def _():
            dgq_ref[...] = jnp.zeros_like(dgq_ref)
            dgk_ref[...] = jnp.zeros_like(dgk_ref)

        def norm_rope_bwd(t, d_out, g_ref, dg_ref):
            r = lax.rsqrt(jnp.sum(t * t, axis=-1, keepdims=True) * (1.0 / HEAD_DIM) + EPS)
            th = t * r
            dn = d_out * cf + _partner(d_out * sf, lane)
            dg_ref[...] += jnp.sum(dn * th, axis=0, keepdims=True)
            dh = dn * g_ref[...]
            return r * (dh - th * (jnp.sum(dh * th, axis=-1, keepdims=True) * (1.0 / HEAD_DIM)))

        dq = norm_rope_bwd(q_ref[...], dq_ref[...] * ATT_SCALE, gq_ref, dgq_ref)
        dqo_ref[...] = dq.astype(BF16)
        kv = kv_ref[...]
        kpre = jnp.where(lane < NOPE, kv, 0.0) + kr_ref[...]
        dkpre = norm_rope_bwd(kpre, dk_ref[...], gk_ref, dgk_ref)
        dkvo_ref[...] = jnp.where(lane < NOPE, dkpre, dv_ref[...]).astype(BF16)
        dkr = jnp.where((lane >= NOPE) & (lane < HEAD_DIM), dkpre, 0.0)

        @pl.when(h == 0)
        def _():
            dkr_ref[...] = dkr

        @pl.when(h > 0)
        def _():
            dkr_ref[...] += dkr

    hb = lambda i, h: (i, h)
    r0 = lambda i, h: (i, 0)
    fix = lambda i, h: (0, 0)
    blk = pl.BlockSpec((tm, HEAD_PAD), hb)
    return pl.pallas_call(
        body, grid=(s // tm, HEADS),
        in_specs=[blk, blk, blk, blk, blk, pl.BlockSpec((tm, HEAD_PAD), lambda i, h: (i, IN_COLS_PAD // HEAD_PAD - 1)),
                  pl.BlockSpec((tm, HEAD_PAD), r0), pl.BlockSpec((tm, HEAD_PAD), r0),
                  pl.BlockSpec((1, HEAD_PAD), fix), pl.BlockSpec((1, HEAD_PAD), fix)],
        out_specs=[blk, blk, pl.BlockSpec((tm, HEAD_PAD), r0), pl.BlockSpec((1, HEAD_PAD), fix), pl.BlockSpec((1, HEAD_PAD), fix)],
        out_shape=[SDS((s, HEADS * HEAD_PAD), BF16), SDS((s, HEADS * HEAD_PAD), BF16), SDS((s, HEAD_PAD), F32),
                   SDS((1, HEAD_PAD), F32), SDS((1, HEAD_PAD), F32)],
        compiler_params=_cp(2), name="mla_prep_bwd")(dqp, dkp, dvp, q_raw, kv_raw, z, cosf, sinf, gq, gk)


def _tri_pairs(n, row_major):
    if row_major:
        pairs = [(i, j) for i in range(n) for j in range(i + 1)]
    else:
        pairs = [(i, j) for j in range(n) for i in range(j, n)]
    ii = np.array([p[0] for p in pairs], np.int32)
    jj = np.array([p[1] for p in pairs], np.int32)
    return jnp.asarray(ii), jnp.asarray(jj)


def _chunk_mask(tb):
    r = lax.broadcasted_iota(jnp.int32, (tb, tb), 0)
    c = lax.broadcasted_iota(jnp.int32, (tb, tb), 1)
    return (c // CHUNK) <= (r // CHUNK)


def _attn_fwd(qp, kp, vp, tb):
    s = qp.shape[0]
    nb = s // tb
    ii, jj = _tri_pairs(nb, True)

    def body(ii_ref, jj_ref, q_ref, k_ref, v_ref, of_ref, ob_ref, lse_ref, m_sc, l_sc, acc_sc):
        t = pl.program_id(1)
        i = ii_ref[t]
        j = jj_ref[t]

        @pl.when(j == 0)
        def _():
            m_sc[...] = jnp.full_like(m_sc, -jnp.inf)
            l_sc[...] = jnp.zeros_like(l_sc)
            acc_sc[...] = jnp.zeros_like(acc_sc)

        def step(masked):
            sc = lax.dot_general(q_ref[...], k_ref[...], (((1,), (1,)), ((), ())), preferred_element_type=F32)
            if masked:
                sc = jnp.where(_chunk_mask(tb), sc, -jnp.inf)
            m_old = m_sc[:, 0:1]
            m_new = jnp.maximum(m_old, jnp.max(sc, axis=-1, keepdims=True))
            alpha = jnp.exp(m_old - m_new)
            p = jnp.exp(sc - m_new)
            l_new = alpha * l_sc[:, 0:1] + jnp.sum(p, axis=-1, keepdims=True)
            acc_sc[...] = alpha * acc_sc[...] + jnp.dot(p.astype(BF16), v_ref[...], preferred_element_type=F32)
            m_sc[...] = jnp.broadcast_to(m_new, m_sc.shape)
            l_sc[...] = jnp.broadcast_to(l_new, l_sc.shape)

        @pl.when(j < i)
        def _():
            step(False)

        @pl.when(j == i)
        def _():
            step(True)
            l = l_sc[...]
            o = acc_sc[...] / l
            of_ref[...] = o
            ob_ref[...] = o.astype(BF16)
            lse_ref[...] = m_sc[...] + jnp.log(l)

    qmap = lambda h, t, ii_ref, jj_ref: (ii_ref[t], h)
    kmap = lambda h, t, ii_ref, jj_ref: (jj_ref[t], h)
    blk = (tb, HEAD_PAD)
    gs = pltpu.PrefetchScalarGridSpec(
        num_scalar_prefetch=2, grid=(HEADS, int(ii.shape[0])),
        in_specs=[pl.BlockSpec(blk, qmap), pl.BlockSpec(blk, kmap), pl.BlockSpec(blk, kmap)],
        out_specs=[pl.BlockSpec(blk, qmap)] * 3,
        scratch_shapes=[pltpu.VMEM(blk, F32)] * 3)
    w = HEADS * HEAD_PAD
    return pl.pallas_call(body, grid_spec=gs, out_shape=[SDS((s, w), F32), SDS((s, w), BF16), SDS((s, w), F32)],
                          compiler_params=_cp(2), name="attn_fwd")(ii, jj, qp, kp, vp)


def _attn_delta(do, o, tm):
    s = do.shape[0]

    def body(do_ref, o_ref, d_ref):
        d_ref[...] = jnp.broadcast_to(jnp.sum(do_ref[...] * o_ref[...], axis=-1, keepdims=True), d_ref.shape)

    blk = pl.BlockSpec((tm, HEAD_PAD), lambda i, h: (i, h))
    return pl.pallas_call(body, grid=(s // tm, HEADS), in_specs=[blk, blk], out_specs=blk,
                          out_shape=SDS(do.shape, F32), compiler_params=_cp(2), name="attn_delta")(do, o)


def _attn_bwd(qp, kp, vp, dob, lse, delta, tb):
    s = qp.shape[0]
    nb = s // tb
    ii, jj = _tri_pairs(nb, False)

    def body(ii_ref, jj_ref, q_ref, k_ref, v_ref, do_ref, lse_ref, dl_ref, dq_ref, dk_ref, dv_ref):
        t = pl.program_id(1)
        i = ii_ref[t]
        j = jj_ref[t]

        @pl.when(t == 0)
        def _():
            dq_ref[...] = jnp.zeros_like(dq_ref)

        @pl.when(i == j)
        def _():
            dk_ref[...] = jnp.zeros_like(dk_ref)
            dv_ref[...] = jnp.zeros_like(dv_ref)

        def step(masked):
            q = q_ref[...]
            k = k_ref[...]
            do = do_ref[...]
            sc = lax.dot_general(q, k, (((1,), (1,)), ((), ())), preferred_element_type=F32)
            p = jnp.exp(sc - lse_ref[:, 0:1])
            if masked:
                p = jnp.where(_chunk_mask(tb), p, 0.0)
            dv_ref[...] += lax.dot_general(p.astype(BF16), do, (((0,), (0,)), ((), ())), preferred_element_type=F32)
            dp = lax.dot_general(do, v_ref[...], (((1,), (1,)), ((), ())), preferred_element_type=F32)
            ds = (p * (dp - dl_ref[:, 0:1])).astype(BF16)
            dk_ref[...] += lax.dot_general(ds, q, (((0,), (0,)), ((), ())), preferred_element_type=F32)
            rows = pl.ds(pl.multiple_of(i * tb, tb), tb)
            dq_ref[rows, :] += jnp.dot(ds, k, preferred_element_type=F32)

        @pl.when(j < i)
        def _():
            step(False)

        @pl.when(j == i)
        def _():
            step(True)

    qmap = lambda h, t, ii_ref, jj_ref: (ii_ref[t], h)
    kmap = lambda h, t, ii_ref, jj_ref: (jj_ref[t], h)
    blk = (tb, HEAD_PAD)
    gs = pltpu.PrefetchScalarGridSpec(
        num_scalar_prefetch=2, grid=(HEADS, int(ii.shape[0])),
        in_specs=[pl.BlockSpec(blk, qmap), pl.BlockSpec(blk, kmap), pl.BlockSpec(blk, kmap), pl.BlockSpec(blk, qmap),
                  pl.BlockSpec(blk, qmap), pl.BlockSpec(blk, qmap)],
        out_specs=[pl.BlockSpec((s, HEAD_PAD), lambda h, t, ii_ref, jj_ref: (0, h)), pl.BlockSpec(blk, kmap), pl.BlockSpec(blk, kmap)])
    w = HEADS * HEAD_PAD
    return pl.pallas_call(body, grid_spec=gs, out_shape=[SDS((s, w), F32)] * 3,
                          compiler_params=_cp(2), name="attn_bwd")(ii, jj, qp, kp, vp, dob, lse, delta)


def _head_norm(t, g):
    r = lax.rsqrt(jnp.mean(t * t, axis=-1, keepdims=True) + EPS)
    th = t * r
    return r, th, th * g


def _softmax_rows(sc):
    m = jnp.max(sc, axis=-1, keepdims=True)
    e = jnp.exp(sc - m)
    return e / jnp.sum(e, axis=-1, keepdims=True)


def _memattn_fwd(qm, kvm, gq, gk, tm):
    s = qm.shape[0]
    hd = MEM_HEAD_DIM

    def body(q_ref, k_ref, v_ref, gq_ref, gk_ref, o_ref):
        _, _, qn = _head_norm(q_ref[...], gq_ref[...])
        _, _, kn = _head_norm(k_ref[...], gk_ref[...])
        sc = lax.dot_general(qn.astype(BF16), kn.astype(BF16), (((1,), (1,)), ((), ())), preferred_element_type=F32)
        p = _softmax_rows(sc * (1.0 / math.sqrt(hd)))
        o_ref[...] = jnp.dot(p.astype(BF16), v_ref[...].astype(BF16), preferred_element_type=F32).astype(BF16)

    fix = lambda i, h: (0, 0)
    return pl.pallas_call(
        body, grid=(s // tm, MEM_HEADS),
        in_specs=[pl.BlockSpec((tm, hd), lambda i, h: (i, h)), pl.BlockSpec((MEM_LEN, hd), lambda i, h: (0, h)),
                  pl.BlockSpec((MEM_LEN, hd), lambda i, h: (0, MEM_HEADS + h)), pl.BlockSpec((1, hd), fix), pl.BlockSpec((1, hd), fix)],
        out_specs=pl.BlockSpec((tm, hd), lambda i, h: (i, h)),
        out_shape=SDS((s, MEM_HEADS * hd), BF16), compiler_params=_cp(2), name="memattn_fwd")(qm, kvm, kvm, gq, gk)


def _memattn_bwd(qm, kvm, d_o, gq, gk, tm):
    s = qm.shape[0]
    hd = MEM_HEAD_DIM

    def body(q_ref, k_ref, v_ref, do_ref, gq_ref, gk_ref, dq_ref, dk_ref, dv_ref, dgq_ref, dgk_ref, dkn_sc):
        h = pl.program_id(0)
        i = pl.program_id(1)
        last = pl.num_programs(1) - 1

        @pl.when((h == 0) & (i == 0))
        def _():
            dgq_ref[...] = jnp.zeros_like(dgq_ref)
            dgk_ref[...] = jnp.zeros_like(dgk_ref)

        @pl.when(i == 0)
        def _():
            dv_ref[...] = jnp.zeros_like(dv_ref)
            dkn_sc[...] = jnp.zeros_like(dkn_sc)

        rq, qh, qn = _head_norm(q_ref[...], gq_ref[...])
        rk, kh, kn = _head_norm(k_ref[...], gk_ref[...])
        qnb = qn.astype(BF16)
        knb = kn.astype(BF16)
        scale = 1.0 / math.sqrt(hd)
        sc = lax.dot_general(qnb, knb, (((1,), (1,)), ((), ())), preferred_element_type=F32)
        p = _softmax_rows(sc * scale)
        do = do_ref[...].astype(BF16)
        dp = lax.dot_general(do, v_ref[...].astype(BF16), (((1,), (1,)), ((), ())), preferred_element_type=F32)
        dv_ref[...] += lax.dot_general(p.astype(BF16), do, (((0,), (0,)), ((), ())), preferred_element_type=F32)
        ds = ((p * (dp - jnp.sum(dp * p, axis=-1, keepdims=True))) * scale).astype(BF16)
        dqn = jnp.dot(ds, knb, preferred_element_type=F32)
        dkn_sc[...] += lax.dot_general(ds, qnb, (((0,), (0,)), ((), ())), preferred_element_type=F32)
        dgq_ref[...] += jnp.sum(dqn * qh, axis=0, keepdims=True)
        dqh = dqn * gq_ref[...]
        dq_ref[...] = (rq * (dqh - qh * jnp.mean(dqh * qh, axis=-1, keepdims=True))).astype(BF16)

        @pl.when(i == last)
        def _():
            dkn = dkn_sc[...]
            dgk_ref[...] += jnp.sum(dkn * kh, axis=0, keepdims=True)
            dkh = dkn * gk_ref[...]
            dk_ref[...] = rk * (dkh - kh * jnp.mean(dkh * kh, axis=-1, keepdims=True))

    fix = lambda h, i: (0, 0)
    qb = pl.BlockSpec((tm, hd), lambda h, i: (i, h))
    kb = pl.BlockSpec((MEM_LEN, hd), lambda h, i: (0, h))
    return pl.pallas_call(
        body, grid=(MEM_HEADS, s // tm),
        in_specs=[qb, kb, pl.BlockSpec((MEM_LEN, hd), lambda h, i: (0, MEM_HEADS + h)), qb,
                  pl.BlockSpec((1, hd), fix), pl.BlockSpec((1, hd), fix)],
        out_specs=[qb, kb, kb, pl.BlockSpec((1, hd), fix), pl.BlockSpec((1, hd), fix)],
        out_shape=[SDS((s, MEM_HEADS * hd), BF16), SDS((MEM_LEN, MEM_HEADS * hd), F32), SDS((MEM_LEN, MEM_HEADS * hd), F32),
                   SDS((1, hd), F32), SDS((1, hd), F32)],
        scratch_shapes=[pltpu.VMEM((MEM_LEN, hd), F32)],
        compiler_params=_cp(2), name="memattn_bwd")(qm, kvm, kvm, d_o, gq, gk)


def _ffn_specs(tm, tn, nbj, s, order_ji):
    if order_ji:
        ij = lambda f: (lambda j, i: f(i, j))
    else:
        ij = lambda f: f
    prev = lambda i: jnp.maximum(i * (tm // FFN_HALO) - 1, 0)
    cur_g = pl.BlockSpec((tm, tn), ij(lambda i, j: (i, j)))
    cur_v = pl.BlockSpec((tm, tn), ij(lambda i, j: (i, j + nbj)))
    halo_g = pl.BlockSpec((FFN_HALO, tn), ij(lambda i, j: (prev(i), j)))
    halo_v = pl.BlockSpec((FFN_HALO, tn), ij(lambda i, j: (prev(i), j + nbj)))
    w_g = pl.BlockSpec((8, tn), ij(lambda i, j: (0, j)))
    w_v = pl.BlockSpec((8, tn), ij(lambda i, j: (0, j + nbj)))
    b_g = pl.BlockSpec((1, tn), ij(lambda i, j: (0, j)))
    b_v = pl.BlockSpec((1, tn), ij(lambda i, j: (0, j + nbj)))
    return cur_g, cur_v, halo_g, halo_v, w_g, w_v, b_g, b_v


def _conv3(cur_ref, halo_ref, w_ref, b_ref, ext, first, tm):
    ext[0:FFN_HALO, :] = jnp.where(first, 0.0, halo_ref[...])
    ext[FFN_HALO:, :] = cur_ref[...]
    return (w_ref[0:1, :] * ext[FFN_HALO - 2:FFN_HALO - 2 + tm, :] + w_ref[1:2, :] * ext[FFN_HALO - 1:FFN_HALO - 1 + tm, :]
            + w_ref[2:3, :] * ext[FFN_HALO:FFN_HALO + tm, :] + b_ref[...])


def _ffn_fwd(up0, w8, b, tm, tn):
    s = up0.shape[0]
    nbj = D_FF // tn

    def body(g_ref, v_ref, gh_ref, vh_ref, wg_ref, wv_ref, bg_ref, bv_ref, act_ref, ext):
        first = pl.program_id(0) == 0
        ug = _conv3(g_ref, gh_ref, wg_ref, bg_ref, ext, first, tm)
        uv = _conv3(v_ref, vh_ref, wv_ref, bv_ref, ext, first, tm)
        act_ref[...] = ((ug * jax.nn.sigmoid(ug)) * uv).astype(BF16)

    specs = _ffn_specs(tm, tn, nbj, s, False)
    return pl.pallas_call(
        body, grid=(s // tm, nbj), in_specs=list(specs),
        out_specs=pl.BlockSpec((tm, tn), lambda i, j: (i, j)), out_shape=SDS((s, D_FF), BF16),
        scratch_shapes=[pltpu.VMEM((tm + FFN_HALO, tn), F32)],
        compiler_params=_cp(2), name="ffn_fwd")(up0, up0, up0, up0, w8, w8, b, b)


def _ffn_bwd_act(d_act, up0, w8, b, tm, tn):
    s = up0.shape[0]
    nbj = D_FF // tn

    def body(da_ref, g_ref, v_ref, gh_ref, vh_ref, wg_ref, wv_ref, bg_ref, bv_ref, dg_ref, dv_ref, dbg_ref, dbv_ref, ext):
        i = pl.program_id(1)
        first = i == 0

        @pl.when(first)
        def _():
            dbg_ref[...] = jnp.zeros_like(dbg_ref)
            dbv_ref[...] = jnp.zeros_like(dbv_ref)

        ug = _conv3(g_ref, gh_ref, wg_ref, bg_ref, ext, first, tm)
        uv = _conv3(v_ref, vh_ref, wv_ref, bv_ref, ext, first, tm)
        sg = jax.nn.sigmoid(ug)
        da = da_ref[...]
        dval = da * (ug * sg)
        dgate = da * uv * (sg * (1.0 + ug * (1.0 - sg)))
        dg_ref[...] = dgate
        dv_ref[...] = dval
        dbg_ref[...] += jnp.sum(dgate, axis=0, keepdims=True)
        dbv_ref[...] += jnp.sum(dval, axis=0, keepdims=True)

    specs = _ffn_specs(tm, tn, nbj, s, True)
    cur = pl.BlockSpec((tm, tn), lambda j, i: (i, j))
    acc = pl.BlockSpec((1, tn), lambda j, i: (0, j))
    return pl.pallas_call(
        body, grid=(nbj, s // tm), in_specs=[cur] + list(specs),
        out_specs=[cur, cur, acc, acc],
        out_shape=[SDS((s, D_FF), F32), SDS((s, D_FF), F32), SDS((1, D_FF), F32), SDS((1, D_FF), F32)],
        scratch_shapes=[pltpu.VMEM((tm + FFN_HALO, tn), F32)],
        compiler_params=_cp(2), name="ffn_bwd_act")(d_act, up0, up0, up0, up0, w8, w8, b, b)


def _ffn_bwd_conv(d_upg, d_upv, up0, w8, tm, tn):
    s = up0.shape[0]
    nbj = D_FF // tn

    def body(dg_ref, dv_ref, dgn_ref, dvn_ref, g_ref, v_ref, gh_ref, vh_ref, wg_ref, wv_ref,
             og_ref, ov_ref, dwg_ref, dwv_ref, extd, extu):
        i = pl.program_id(1)
        first = i == 0
        last = i == pl.num_programs(1) - 1

        @pl.when(first)
        def _():
            dwg_ref[...] = jnp.zeros_like(dwg_ref)
            dwv_ref[...] = jnp.zeros_like(dwv_ref)

        def half(d_ref, dn_ref, u_ref, uh_ref, w_ref, o_ref, dw_ref):
            extd[0:tm, :] = d_ref[...]
            extd[tm:, :] = jnp.where(last, 0.0, dn_ref[...])
            o_ref[...] = (w_ref[2:3, :] * extd[0:tm, :] + w_ref[1:2, :] * extd[1:1 + tm, :]
                          + w_ref[0:1, :] * extd[2:2 + tm, :]).astype(BF16)
            extu[0:FFN_HALO, :] = jnp.where(first, 0.0, uh_ref[...])
            extu[FFN_HALO:, :] = u_ref[...]
            d = d_ref[...]
            for k in range(3):
                o = FFN_HALO - 2 + k
                dw_ref[k:k + 1, :] += jnp.sum(d * extu[o:o + tm, :], axis=0, keepdims=True)

        half(dg_ref, dgn_ref, g_ref, gh_ref, wg_ref, og_ref, dwg_ref)
        half(dv_ref, dvn_ref, v_ref, vh_ref, wv_ref, ov_ref, dwv_ref)

    cur_g, cur_v, halo_g, halo_v, w_g, w_v, _, _ = _ffn_specs(tm, tn, nbj, s, True)
    cur = pl.BlockSpec((tm, tn), lambda j, i: (i, j))
    nxt = pl.BlockSpec((FFN_HALO, tn), lambda j, i: (jnp.minimum((i + 1) * (tm // FFN_HALO), s // FFN_HALO - 1), j))
    acc = pl.BlockSpec((8, tn), lambda j, i: (0, j))
    return pl.pallas_call(
        body, grid=(nbj, s // tm), in_specs=[cur, cur, nxt, nxt, cur_g, cur_v, halo_g, halo_v, w_g, w_v],
        out_specs=[cur, cur, acc, acc],
        out_shape=[SDS((s, D_FF), BF16), SDS((s, D_FF), BF16), SDS((8, D_FF), F32), SDS((8, D_FF), F32)],
        scratch_shapes=[pltpu.VMEM((tm + FFN_HALO, tn), F32), pltpu.VMEM((tm + FFN_HALO, tn), F32)],
        compiler_params=_cp(2), name="ffn_bwd_conv")(d_upg, d_upv, d_upg, d_upv, up0, up0, up0, up0, w8, w8)


def _down_loss(act, w_down, x2, target, tm):
    s = act.shape[0]

    def body(a_ref, w_ref, x_ref, t_ref, dyf_ref, dyb_ref, ls_ref):
        @pl.when(pl.program_id(0) == 0)
        def _():
            ls_ref[...] = jnp.zeros_like(ls_ref)

        y = x_ref[...] + jnp.dot(a_ref[...], w_ref[...], preferred_element_type=F32)
        e = y - t_ref[...]
        ls_ref[...] += jnp.sum(e * e)
        dy = e * (1.0 / D_MODEL)
        dyf_ref[...] = dy
        dyb_ref[...] = dy.astype(BF16)

    row = lambda i: (i, 0)
    return pl.pallas_call(
        body, grid=(s // tm,),
        in_specs=[pl.BlockSpec((tm, D_FF), row), pl.BlockSpec((D_FF, D_MODEL), lambda i: (0, 0)),
                  pl.BlockSpec((tm, D_MODEL), row), pl.BlockSpec((tm, D_MODEL), row)],
        out_specs=[pl.BlockSpec((tm, D_MODEL), row), pl.BlockSpec((tm, D_MODEL), row), pl.BlockSpec((8, 128), lambda i: (0, 0))],
        out_shape=[SDS((s, D_MODEL), F32), SDS((s, D_MODEL), BF16), SDS((8, 128), F32)],
        compiler_params=_cp(1), name="down_loss")(act, w_down, x2, target)


def _adamw(w, g, m, v, tr, name):
    rows, cols = w.shape

    def body(w_ref, g_ref, m_ref, v_ref, d_ref, mo_ref, vo_ref):
        gv = g_ref[...]
        mn = ADAM_B1 * m_ref[...] + (1.0 - ADAM_B1) * gv
        vn = ADAM_B2 * v_ref[...] + (1.0 - ADAM_B2) * (gv * gv)
        m_hat = mn / (1.0 - ADAM_B1 ** ADAM_STEP)
        v_hat = vn / (1.0 - ADAM_B2 ** ADAM_STEP)
        d_ref[...] = -ADAM_LR * (m_hat / (jnp.sqrt(v_hat) + ADAM_EPS) + ADAM_WD * w_ref[...])
        mo_ref[...] = mn
        vo_ref[...] = vn

    blk = pl.BlockSpec((tr, cols), lambda i: (i, 0))
    return pl.pallas_call(body, grid=(rows // tr,), in_specs=[blk] * 4, out_specs=[blk] * 3,
                          out_shape=[SDS((rows, cols), F32)] * 3, compiler_params=_cp(1), name=name)(w, g, m, v)


def _add_n(xs, tr, name):
    rows, cols = xs[0].shape

    def body(*refs):
        acc = refs[0][...]
        for r in refs[1:-1]:
            acc = acc + r[...]
        refs[-1][...] = acc

    blk = pl.BlockSpec((tr, cols), lambda i: (i, 0))
    return pl.pallas_call(body, grid=(rows // tr,), in_specs=[blk] * len(xs), out_specs=blk,
                          out_shape=SDS((rows, cols), F32), compiler_params=_cp(1), name=name)(*xs)


ANY = pl.BlockSpec(memory_space=pl.ANY)


def _coords():
    return lax.axis_index("x"), lax.axis_index("y"), lax.axis_index("c")


def _other_chips(x, y):
    return [(1 - x, y), (x, 1 - y), (1 - x, 1 - y)]


def _ag_weights(wsh):
    rows, cols = wsh.shape
    half_rows = rows // 2

    def body(w_ref, out_ref, send_sems, recv_sems, local_sem):
        x, y, c = _coords()
        s_me = 2 * x + y
        chips = _other_chips(x, y)
        sibling = (x, y, 1 - c)
        my_half = pl.ds(pl.multiple_of(c * half_rows, 16), half_rows)
        sib_half = pl.ds(pl.multiple_of((1 - c) * half_rows, 16), half_rows)

        def copy(k, shard, rows_, to, src=None):
            dst = out_ref.at[shard, rows_]
            return pltpu.make_async_remote_copy(src_ref=dst if src is None else src, dst_ref=dst, send_sem=send_sems.at[k],
                                                recv_sem=recv_sems.at[k], device_id=to, device_id_type=MESH)

        mine = pltpu.make_async_copy(w_ref, out_ref.at[s_me], local_sem)
        mine.start()
        first = [copy(k, s_me, my_half, (px, py, c), src=w_ref.at[my_half]) for k, (px, py) in enumerate(chips)]
        for cp in first:
            cp.start()
        passed = []
        for k, (px, py) in enumerate(chips):
            copy(k, 2 * px + py, my_half, (px, py, c)).wait_recv()
            fwd = copy(3 + k, 2 * px + py, my_half, sibling)
            fwd.start()
            passed.append(fwd)
        for k, (px, py) in enumerate(chips):
            copy(3 + k, 2 * px + py, sib_half, sibling).wait_recv()
        for cp in first + passed:
            cp.wait_send()
        mine.wait()

    return pl.pallas_call(
        body, in_specs=[ANY], out_specs=ANY, out_shape=SDS((4, rows, cols), wsh.dtype),
        scratch_shapes=[pltpu.SemaphoreType.DMA((6,)), pltpu.SemaphoreType.DMA((6,)), pltpu.SemaphoreType.DMA],
        name="ag_weights")(wsh)


def _rs_swap_halves(gfull):
    n_sh, rows, cols = gfull.shape
    half_rows = rows // 2

    def body(g_ref, mine_ref, recv_ref, send_sem, recv_sem, local_sem):
        x, y, c = _coords()
        my_half = pl.ds(pl.multiple_of(c * half_rows, 8), half_rows)
        sib_half = pl.ds(pl.multiple_of((1 - c) * half_rows, 8), half_rows)
        loc = pltpu.make_async_copy(g_ref.at[:, my_half], mine_ref, local_sem)
        rem = pltpu.make_async_remote_copy(src_ref=g_ref.at[:, sib_half], dst_ref=recv_ref, send_sem=send_sem, recv_sem=recv_sem,
                                           device_id=(x, y, 1 - c), device_id_type=MESH)
        loc.start()
        rem.start()
        rem.wait()
        loc.wait()

    out = SDS((n_sh, half_rows, cols), gfull.dtype)
    return pl.pallas_call(
        body, in_specs=[ANY], out_specs=[ANY, ANY], out_shape=[out, out],
        scratch_shapes=[pltpu.SemaphoreType.DMA, pltpu.SemaphoreType.DMA, pltpu.SemaphoreType.DMA],
        name="rs_swap_halves")(gfull)


def _rs_to_owner(chipsum):
    n_sh, half_rows, cols = chipsum.shape

    def body(cs_ref, mine_ref, recv_ref, send_sems, recv_sems, local_sem):
        x, y, c = _coords()
        loc = pltpu.make_async_copy(cs_ref.at[2 * x + y], mine_ref, local_sem)
        loc.start()
        cps = []
        for k, (px, py) in enumerate(_other_chips(x, y)):
            cp = pltpu.make_async_remote_copy(src_ref=cs_ref.at[2 * px + py], dst_ref=recv_ref.at[k], send_sem=send_sems.at[k],
                                              recv_sem=recv_sems.at[k], device_id=(px, py, c), device_id_type=MESH)
            cp.start()
            cps.append(cp)
        for cp in cps:
            cp.wait()
        loc.wait()

    return pl.pallas_call(
        body, in_specs=[ANY], out_specs=[ANY, ANY],
        out_shape=[SDS((half_rows, cols), chipsum.dtype), SDS((3, half_rows, cols), chipsum.dtype)],
        scratch_shapes=[pltpu.SemaphoreType.DMA((3,)), pltpu.SemaphoreType.DMA((3,)), pltpu.SemaphoreType.DMA],
        name="rs_to_owner")(chipsum)


def _rs_join_halves(red):
    half_rows, cols = red.shape

    def body(r_ref, out_ref, send_sem, recv_sem, local_sem):
        x, y, c = _coords()
        loc = pltpu.make_async_copy(r_ref, out_ref.at[c], local_sem)
        rem = pltpu.make_async_remote_copy(src_ref=r_ref, dst_ref=out_ref.at[c], send_sem=send_sem, recv_sem=recv_sem,
                                           device_id=(x, y, 1 - c), device_id_type=MESH)
        loc.start()
        rem.start()
        rem.wait()
        loc.wait()

    return pl.pallas_call(
        body, in_specs=[ANY], out_specs=ANY, out_shape=SDS((2, half_rows, cols), red.dtype),
        scratch_shapes=[pltpu.SemaphoreType.DMA, pltpu.SemaphoreType.DMA, pltpu.SemaphoreType.DMA],
        name="rs_join_halves")(red)


BIG = [("w_in", (1024, 1440), 1), ("w_uq", (256, 768), 1), ("w_ukv", (128, 1024), 1), ("w_out", (1024, 1024), 0),
       ("w_mem_q", (1024, 1024), 0), ("w_mem_kv", (1024, 2048), 1), ("w_mem_o", (1024, 1024), 0),
       ("w_up", (1024, 5632), 1), ("w_down", (2816, 1024), 0)]
SMALL_REP = [("mix_norm_g", 1024), ("b_conv_in", 1024), ("b_conv_dw", 512), ("conv_ln_g", 512), ("conv_ln_b", 512),
             ("q_lat_norm_g", 256), ("kv_lat_norm_g", 128), ("q_norm_g", 96), ("k_norm_g", 96), ("mem_norm_x_g", 1024),
             ("mem_norm_m_g", 1024), ("mem_q_norm_g", 256), ("mem_k_norm_g", 256), ("ffn_norm_g", 1024), ("b_ffn_dw", 5632)]
SMALL_SH = [("w_conv_dw", (31, 512)), ("w_ffn_dw", (3, 5632))]


def _shard_shape(shape, axis):
    return tuple(d // 4 if a == axis else d for a, d in enumerate(shape))


def _pack_rows(parts, rows, cols):
    flat = jnp.concatenate([p.reshape(-1) for p in parts])
    flat = jnp.pad(flat, (0, rows * cols - flat.shape[0]))
    return flat.reshape(rows, cols)


def _pack_big_shards(ws):
    return _pack_rows([ws[n] for n, _, _ in BIG], PACK_ROWS, PACK_COLS)


def _unpack_big_shards(packed):
    out, r = {}, 0
    for n, shape, axis in BIG:
        sh = _shard_shape(shape, axis)
        nr = sh[0] * sh[1] // PACK_COLS
        out[n] = packed[r:r + nr].reshape(sh)
        r += nr
    return out


def _unpack_gathered(g):
    out, r = {}, 0
    for n, shape, axis in BIG:
        sh = _shard_shape(shape, axis)
        nr = sh[0] * sh[1] // PACK_COLS
        parts = g[:, r:r + nr].reshape((4,) + sh)
        out[n] = jnp.concatenate([parts[i] for i in range(4)], axis=axis)
        r += nr
    return out


def _pack_full_grads(gs):
    slabs = []
    for i in range(4):
        parts = []
        for n, shape, axis in BIG:
            sh = _shard_shape(shape, axis)
            parts.append(lax.slice_in_dim(gs[n], i * sh[axis], (i + 1) * sh[axis], axis=axis))
        slabs.append(_pack_rows(parts, PACK_ROWS, PACK_COLS))
    return jnp.stack(slabs)


def _rope_tables(positions):
    inv_freq = ROPE_THETA ** (-jnp.arange(0, ROPE, 2, dtype=F32) / ROPE)
    ang = positions.astype(F32)[:, None] * inv_freq
    cos, sin = jnp.cos(ang), jnp.sin(ang)
    s = positions.shape[0]
    cosf = jnp.concatenate([jnp.ones((s, NOPE), F32), cos, cos, jnp.ones((s, HEAD_PAD - HEAD_DIM), F32)], axis=-1)
    sinf = jnp.concatenate([jnp.zeros((s, NOPE), F32), -sin, sin, jnp.zeros((s, HEAD_PAD - HEAD_DIM), F32)], axis=-1)
    return cosf, sinf


def _pad_heads(w, per_head):
    k = w.shape[0]
    w3 = w.reshape(k, HEADS, per_head)
    return jnp.pad(w3, ((0, 0), (0, 0), (0, HEAD_PAD - per_head))).reshape(k, HEADS * HEAD_PAD)


def _layer_grads(x, mem, positions, target, wf, sp):
    s = x.shape[0]
    tm = _row_tile(s, 512)
    tc = _row_tile(s, 256)
    tb = 512 if s % 512 == 0 and s > 512 else s // 2
    row2 = lambda a: a.reshape(1, -1)

    w_in = wf["w_in"]
    w_in_pad = jnp.concatenate([w_in[:, :1408], jnp.zeros((D_MODEL, NOPE), BF16), w_in[:, 1408:],
                                jnp.zeros((D_MODEL, HEAD_PAD - HEAD_DIM), BF16)], axis=1)
    w_uq_pad = _pad_heads(wf["w_uq"], HEAD_DIM)
    w_ukv = wf["w_ukv"]
    w_out_u = wf["w_out"][:CONV_CH]
    w_out_o = jnp.pad(wf["w_out"][CONV_CH:].reshape(HEADS, NOPE, D_MODEL), ((0, 0), (NOPE, 0), (0, 0))).reshape(HEADS * HEAD_PAD, D_MODEL)
    w_up_g, w_up_v = wf["w_up"][:, :D_FF], wf["w_up"][:, D_FF:]
    gq_pad = jnp.pad(sp["q_norm_g"], (0, HEAD_PAD - HEAD_DIM)).reshape(1, HEAD_PAD)
    gk_pad = jnp.pad(sp["k_norm_g"], (0, HEAD_PAD - HEAD_DIM)).reshape(1, HEAD_PAD)
    w_dw32 = jnp.pad(sp["w_conv_dw"], ((0, 1), (0, 0)))
    w_ffn8 = jnp.pad(sp["w_ffn_dw"], ((0, 5), (0, 0)))
    b_ffn = row2(sp["b_ffn_dw"])
    cosf, sinf = _rope_tables(positions)

    z, h1 = _norm_linear(x, 0, D_MODEL, row2(sp["mix_norm_g"]), w_in_pad, F32, tm, 512, "in_proj")
    u, u0, u1 = _conv_fwd(z, row2(sp["b_conv_in"]), w_dw32, row2(sp["b_conv_dw"]), row2(sp["conv_ln_g"]), row2(sp["conv_ln_b"]), tc)
    q_raw, cqn = _norm_linear(z, 1024 // Q_RANK, Q_RANK, row2(sp["q_lat_norm_g"]), w_uq_pad, F32, tm, 1024, "q_up")
    kv_raw, ckvn = _norm_linear(z, 1280 // KV_RANK, KV_RANK, row2(sp["kv_lat_norm_g"]), w_ukv, F32, tm, 1024, "kv_up")
    qp, kp, vp = _mla_prep(q_raw, kv_raw, z, cosf, sinf, gq_pad, gk_pad, tm)
    o_f, o_b, lse = _attn_fwd(qp, kp, vp, tb)
    (x1,) = _linear([(u, w_out_u), (o_b, w_out_o)], False, x, [F32], tm, 1024, "out_proj")

    qm, hq = _norm_linear(x1, 0, D_MODEL, row2(sp["mem_norm_x_g"]), wf["w_mem_q"], F32, tm, 1024, "memq_proj")
    kvm, hm = _norm_linear(mem, 0, D_MODEL, row2(sp["mem_norm_m_g"]), wf["w_mem_kv"], F32, MEM_LEN, 1024, "memkv_proj")
    gmq, gmk = row2(sp["mem_q_norm_g"]), row2(sp["mem_k_norm_g"])
    o_m = _memattn_fwd(qm, kvm, gmq, gmk, tm)
    (x2,) = _linear([(o_m, wf["w_mem_o"])], False, x1, [F32], tm, 1024, "memo_proj")

    up0, h3 = _norm_linear(x2, 0, D_MODEL, row2(sp["ffn_norm_g"]), wf["w_up"], F32, tm, 512, "up_proj")
    act = _ffn_fwd(up0, w_ffn8, b_ffn, tc, D_FF // 2)
    dy_f, dy_b, lsum = _down_loss(act, wf["w_down"], x2, target, tm)

    g = {}
    (d_act,) = _linear([(dy_b, wf["w_down"])], True, None, [F32], tm, D_FF // 2, "down_bwd")
    g["w_down"] = _dw(act, dy_b, "dw_down")
    d_upg, d_upv, dbg, dbv = _ffn_bwd_act(d_act, up0, w_ffn8, b_ffn, tc, D_FF // 2)
    d_up0g, d_up0v, dwg, dwv = _ffn_bwd_conv(d_upg, d_upv, up0, w_ffn8, tc, D_FF // 2)
    g["b_ffn_dw"] = jnp.concatenate([dbg, dbv], axis=1).reshape(-1)
    g["w_ffn_dw"] = jnp.concatenate([dwg[:3], dwv[:3]], axis=1)
    g["w_up"] = jnp.concatenate([_dw(h3, d_up0g, "dw_up_g"), _dw(h3, d_up0v, "dw_up_v")], axis=1)
    d_x2f, d_x2b, dg = _linear_normbwd([(d_up0g, w_up_g), (d_up0v, w_up_v)], x2, 0, row2(sp["ffn_norm_g"]), dy_f,
                                       [F32, BF16], tc, "up_bwd")
    g["ffn_norm_g"] = dg.reshape(-1)

    (d_om,) = _linear([(d_x2b, wf["w_mem_o"])], True, None, [BF16], tm, 1024, "memo_bwd")
    g["w_mem_o"] = _dw(o_m, d_x2b, "dw_mem_o")
    d_qm, d_km, d_vm, dgq, dgk = _memattn_bwd(qm, kvm, d_om, gmq, gmk, tm)
    g["mem_q_norm_g"], g["mem_k_norm_g"] = dgq.reshape(-1), dgk.reshape(-1)
    d_kvm = jnp.concatenate([d_km, d_vm], axis=1)
    g["w_mem_q"] = _dw(hq, d_qm, "dw_mem_q")
    g["w_mem_kv"] = _dw(hm, d_kvm, "dw_mem_kv")
    d_x1f, d_x1b, dg = _linear_normbwd([(d_qm, wf["w_mem_q"])], x1, 0, row2(sp["mem_norm_x_g"]), d_x2f, [F32, BF16], tm, "memq_bwd")
    g["mem_norm_x_g"] = dg.reshape(-1)
    _, dg = _linear_normbwd([(d_kvm, wf["w_mem_kv"])], mem, 0, row2(sp["mem_norm_m_g"]), None, [BF16], MEM_LEN, "memkv_bwd")
    g["mem_norm_m_g"] = dg.reshape(-1)

    (d_u,) = _linear([(d_x1b, w_out_u)], True, None, [F32], tm, CONV_CH, "out_bwd_u")
    d_of, d_ob = _linear([(d_x1b, w_out_o)], True, None, [F32, BF16], tm, 1024, "out_bwd_o")
    dw_out_u = _dw(u, d_x1b, "dw_out_u")
    dw_out_o = _dw(o_b, d_x1b, "dw_out_o")
    g["w_out"] = jnp.concatenate([dw_out_u, dw_out_o.reshape(HEADS, HEAD_PAD, D_MODEL)[:, NOPE:].reshape(HEADS * NOPE, D_MODEL)], axis=0)
    delta = _attn_delta(d_of, o_f, tm)
    dqp, dkp, dvp = _attn_bwd(qp, kp, vp, d_ob, lse, delta, tb)
    d_qraw, d_kvraw, d_kr, dgq, dgk = _mla_prep_bwd(dqp, dkp, dvp, q_raw, kv_raw, z, cosf, sinf, gq_pad, gk_pad, tm)
    g["q_norm_g"], g["k_norm_g"] = dgq.reshape(-1)[:HEAD_DIM], dgk.reshape(-1)[:HEAD_DIM]
    g["w_uq"] = _dw(cqn, d_qraw, "dw_uq").reshape(Q_RANK, HEADS, HEAD_PAD)[:, :, :HEAD_DIM].reshape(Q_RANK, HEADS * HEAD_DIM)
    g["w_ukv"] = _dw(ckvn, d_kvraw, "dw_ukv")
    d_cq, dg = _linear_normbwd([(d_qraw, w_uq_pad)], z, 1024 // Q_RANK, row2(sp["q_lat_norm_g"]), None, [BF16], tm, "q_up_bwd")
    g["q_lat_norm_g"] = dg.reshape(-1)
    d_ckv, dg = _linear_normbwd([(d_kvraw, w_ukv)], z, 1280 // KV_RANK, row2(sp["kv_lat_norm_g"]), None, [BF16], tm, "kv_up_bwd")
    g["kv_lat_norm_g"] = dg.reshape(-1)
    d_u1, dlg, dlb, dbdw = _conv_bwd_ln(d_u, u1, row2(sp["conv_ln_g"]), row2(sp["conv_ln_b"]), tc)
    g["conv_ln_g"], g["conv_ln_b"], g["b_conv_dw"] = dlg.reshape(-1), dlb.reshape(-1), dbdw.reshape(-1)
    d_conv, dw_dw, dbin = _conv_bwd_dw(d_u1, u0, z, row2(sp["b_conv_in"]), w_dw32, tc)
    g["w_conv_dw"], g["b_conv_in"] = dw_dw[:CONV_WIDTH], dbin.reshape(-1)
    pieces = [(d_conv, w_in_pad[:, :1024]), (d_cq, w_in_pad[:, 1024:1280]), (d_ckv, w_in_pad[:, 1280:1408]), (d_kr, w_in_pad[:, 1408:])]
    dw_in = [_dw(h1, d, "dw_in_%d" % k) for k, (d, _) in enumerate(pieces)]
    g["w_in"] = jnp.concatenate([dw_in[0], dw_in[1], dw_in[2], dw_in[3][:, NOPE:HEAD_DIM]], axis=1)
    grad_x, dg = _linear_normbwd(pieces, x, 0, row2(sp["mix_norm_g"]), d_x1f, [F32], tm, "in_bwd")
    g["mix_norm_g"] = dg.reshape(-1)
    return lsum[0, 0], grad_x, g


def kernel(x, mem, positions, mix_norm_g, w_in, b_conv_in, w_conv_dw, b_conv_dw, conv_ln_g, conv_ln_b, q_lat_norm_g, w_uq, kv_lat_norm_g, w_ukv, q_norm_g, k_norm_g, w_out, mem_norm_x_g, mem_norm_m_g, w_mem_q, w_mem_kv, mem_q_norm_g, mem_k_norm_g, w_mem_o, ffn_norm_g, w_up, w_ffn_dw, b_ffn_dw, w_down, loss_target, m_mix_norm_g, m_w_in, m_b_conv_in, m_w_conv_dw, m_b_conv_dw, m_conv_ln_g, m_conv_ln_b, m_q_lat_norm_g, m_w_uq, m_kv_lat_norm_g, m_w_ukv, m_q_norm_g, m_k_norm_g, m_w_out, m_mem_norm_x_g, m_mem_norm_m_g, m_w_mem_q, m_w_mem_kv, m_mem_q_norm_g, m_mem_k_norm_g, m_w_mem_o, m_ffn_norm_g, m_w_up, m_w_ffn_dw, m_b_ffn_dw, m_w_down, v_mix_norm_g, v_w_in, v_b_conv_in, v_w_conv_dw, v_b_conv_dw, v_conv_ln_g, v_conv_ln_b, v_q_lat_norm_g, v_w_uq, v_kv_lat_norm_g, v_w_ukv, v_q_norm_g, v_k_norm_g, v_w_out, v_mem_norm_x_g, v_mem_norm_m_g, v_w_mem_q, v_w_mem_kv, v_mem_q_norm_g, v_mem_k_norm_g, v_w_mem_o, v_ffn_norm_g, v_w_up, v_w_ffn_dw, v_b_ffn_dw, v_w_down):
    names = ["mix_norm_g", "w_in", "b_conv_in", "w_conv_dw", "b_conv_dw", "conv_ln_g", "conv_ln_b", "q_lat_norm_g", "w_uq",
             "kv_lat_norm_g", "w_ukv", "q_norm_g", "k_norm_g", "w_out", "mem_norm_x_g", "mem_norm_m_g", "w_mem_q", "w_mem_kv",
             "mem_q_norm_g", "mem_k_norm_g", "w_mem_o", "ffn_norm_g", "w_up", "w_ffn_dw", "b_ffn_dw", "w_down"]
    loc = locals()
    w = {n: loc[n] for n in names}
    m = {n: loc["m_" + n] for n in names}
    v = {n: loc["v_" + n] for n in names}
    shard_idx = 2 * lax.axis_index("x") + lax.axis_index("y")

    w_packed = _pack_big_shards({n: w[n][0] for n, _, _ in BIG})
    gathered = _ag_weights(w_packed.astype(BF16))
    wf = _unpack_gathered(gathered)

    small_sh_full = {}
    gather_in = []
    for n, (r, c) in SMALL_SH:
        csh = c // 4
        slab = lax.dynamic_update_slice(jnp.zeros((r, c), F32), w[n][0], (0, shard_idx * csh))
        gather_in.append(slab.reshape(-1))
    gather_rows = 256
    gathered_small = _allreduce_small_named(_pack_rows(gather_in, gather_rows, SMALL_COLS), "gather_small") * 0.5
    off = 0
    for n, (r, c) in SMALL_SH:
        small_sh_full[n] = gathered_small.reshape(-1)[off:off + r * c].reshape(r, c)
        off += r * c
    sp = {n: w[n][0] for n, _ in SMALL_REP}
    sp.update(small_sh_full)

    lsum, grad_x, g = _layer_grads(x[0], mem[0], positions[0], loss_target[0], wf, sp)

    small_parts = [jnp.full((SMALL_COLS,), lsum, F32)] + [g[n] for n, _ in SMALL_REP] + [g[n] for n, _ in SMALL_SH]
    small_rows = 368
    small_sum = _allreduce_small_named(_pack_rows(small_parts, small_rows, SMALL_COLS), "allreduce_small").reshape(-1)
    loss = small_sum[0] * (0.5 / D_MODEL)
    gs = {}
    off = SMALL_COLS
    for n, sz in SMALL_REP:
        gs[n] = small_sum[off:off + sz].reshape(w[n].shape)
        off += sz
    for n, (r, c) in SMALL_SH:
        full = small_sum[off:off + r * c].reshape(r, c)
        gs[n] = lax.dynamic_slice(full, (0, shard_idx * (c // 4)), (r, c // 4)).reshape(w[n].shape)
        off += r * c

    gfull = _pack_full_grads(g)
    mine_a, recv_a = _rs_swap_halves(gfull)
    hr = PACK_ROWS // 2
    chipsum = _add_n([mine_a.reshape(4 * hr, PACK_COLS), recv_a.reshape(4 * hr, PACK_COLS)], 256, "rs_add_pair").reshape(4, hr, PACK_COLS)
    mine_b, recv_b = _rs_to_owner(chipsum)
    red = _add_n([mine_b, recv_b[0], recv_b[1], recv_b[2]], 192, "rs_add_chips")
    g_packed = _rs_join_halves(red).reshape(PACK_ROWS, PACK_COLS)
    gs.update({n: a.reshape(w[n].shape) for n, a in _unpack_big_shards(g_packed).items()})

    m_packed = _pack_big_shards({n: m[n][0] for n, _, _ in BIG})
    v_packed = _pack_big_shards({n: v[n][0] for n, _, _ in BIG})
    d_p, m_p, v_p = _adamw(w_packed, g_packed, m_packed, v_packed, 256, "adamw_big")
    delta, new_m, new_v = {}, {}, {}
    for dst, src in ((delta, d_p), (new_m, m_p), (new_v, v_p)):
        dst.update({n: a.reshape(w[n].shape) for n, a in _unpack_big_shards(src).items()})
    small_names = [n for n, _ in SMALL_REP] + [n for n, _ in SMALL_SH]
    adam_rows = 176
    pk = lambda d: _pack_rows([d[n].reshape(-1) for n in small_names], adam_rows, SMALL_COLS)
    d_s, m_s, v_s = _adamw(pk(w), pk(gs), pk(m), pk(v), adam_rows, "adamw_small")
    for dst, src in ((delta, d_s), (new_m, m_s), (new_v, v_s)):
        off = 0
        flat = src.reshape(-1)
        for n in small_names:
            sz = int(np.prod(w[n].shape))
            dst[n] = flat[off:off + sz].reshape(w[n].shape)
            off += sz

    return (loss, grad_x[None], *[gs[n] for n in names], *[delta[n] for n in names], *[new_m[n] for n in names],
            *[new_v[n] for n in names])


def _allreduce_small_named(v, name):
    rows, cols = v.shape

    def body(v_ref, out_ref, buf, send_sems, recv_sems):
        x, y, c = _coords()
        me = 4 * x + 2 * y + c
        buf[me] = v_ref[...]
        cps = []
        for r in range(1, 8):
            dx, dy, dc = (r >> 2) & 1, (r >> 1) & 1, r & 1
            to = (x + dx - 2 * x * dx, y + dy - 2 * y * dy, c + dc - 2 * c * dc)
            cp = pltpu.make_async_remote_copy(src_ref=v_ref, dst_ref=buf.at[me], send_sem=send_sems.at[r - 1],
                                              recv_sem=recv_sems.at[r - 1], device_id=to, device_id_type=MESH)
            cp.start()
            cps.append(cp)
        for cp in cps:
            cp.wait()
        acc = buf[0]
        for d in range(1, 8):
            acc = acc + buf[d]
        out_ref[...] = acc

    vm = pl.BlockSpec(memory_space=pltpu.VMEM)
    return pl.pallas_call(
        body, in_specs=[vm], out_specs=vm, out_shape=SDS((rows, cols), F32),
        scratch_shapes=[pltpu.VMEM((8, rows, cols), F32), pltpu.SemaphoreType.DMA((7,)), pltpu.SemaphoreType.DMA((7,))],
        name=name)(v)
```

```python
import math

import numpy as np
import jax
import jax.numpy as jnp
from jax import lax
from jax.experimental import pallas as pl
from jax.experimental.pallas import tpu as pltpu

F32 = jnp.float32
BF16 = jnp.bfloat16
SDS = jax.ShapeDtypeStruct
MESH = pl.DeviceIdType.MESH

D_MODEL = 1024
EPS = 1e-6
CONV_CH = 512
CONV_WIDTH = 31
CONV_HALO = 32
HEADS = 8
NOPE = 64
ROPE = 32
HEAD_DIM = NOPE + ROPE
HEAD_PAD = 128
Q_RANK = 256
KV_RANK = 128
CHUNK = 64
ROPE_THETA = 10000.0
IN_COLS_PAD = 1536
MEM_HEADS = 4
MEM_HEAD_DIM = 256
MEM_LEN = 256
D_FF = 2816
FFN_HALO = 8
ATT_SCALE = 1.0 / math.sqrt(HEAD_DIM)
LOG2E = math.log2(math.e)
LN2 = math.log(2.0)

ADAM_LR = 0.001
ADAM_B1 = 0.9
ADAM_B2 = 0.999
ADAM_EPS = 1e-08
ADAM_WD = 0.01
ADAM_STEP = 10

VMEM_LIMIT_V7X = 56 * 1024 * 1024
PACK_COLS = 1024
PACK_ROWS = 3840
SMALL_COLS = 128


def _cp(n_axes):
    return pltpu.CompilerParams(dimension_semantics=("arbitrary",) * n_axes, vmem_limit_bytes=VMEM_LIMIT_V7X)


def _row_tile(s, want):
    return want if s % want == 0 else s


def _norm_linear(x, xcol, kdim, g, w, out_dtype, tm, tn, name):
    s = x.shape[0]
    n = w.shape[1]

    def body(x_ref, g_ref, w_ref, y_ref, hn_ref):
        @pl.when(pl.program_id(1) == 0)
        def _():
            xv = x_ref[...]
            r = lax.rsqrt(jnp.mean(xv * xv, axis=-1, keepdims=True) + EPS)
            hn_ref[...] = ((xv * r) * g_ref[...]).astype(BF16)

        y_ref[...] = jnp.dot(hn_ref[...], w_ref[...], preferred_element_type=F32).astype(y_ref.dtype)

    return pl.pallas_call(
        body, grid=(s // tm, n // tn),
        in_specs=[pl.BlockSpec((tm, kdim), lambda i, j: (i, xcol)), pl.BlockSpec((1, kdim), lambda i, j: (0, 0)),
                  pl.BlockSpec((kdim, tn), lambda i, j: (0, j))],
        out_specs=[pl.BlockSpec((tm, tn), lambda i, j: (i, j)), pl.BlockSpec((tm, kdim), lambda i, j: (i, 0))],
        out_shape=[SDS((s, n), out_dtype), SDS((s, kdim), BF16)],
        compiler_params=_cp(2), name=name)(x, g, w)


def _linear(pairs, nt, residual, out_dtypes, tm, tn, name):
    s = pairs[0][0].shape[0]
    n = pairs[0][1].shape[0] if nt else pairs[0][1].shape[1]
    n_pairs = len(pairs)
    has_res = residual is not None

    def body(*refs):
        a_refs = refs[:n_pairs]
        w_refs = refs[n_pairs:2 * n_pairs]
        res_ref = refs[2 * n_pairs] if has_res else None
        outs = refs[2 * n_pairs + int(has_res):]
        acc = None
        for a_ref, w_ref in zip(a_refs, w_refs):
            a = a_ref[...].astype(BF16)
            if nt:
                d = lax.dot_general(a, w_ref[...], (((1,), (1,)), ((), ())), preferred_element_type=F32)
            else:
                d = jnp.dot(a, w_ref[...], preferred_element_type=F32)
            acc = d if acc is None else acc + d
        if has_res:
            acc = res_ref[...] + acc
        for o in outs:
            o[...] = acc.astype(o.dtype)

    in_specs = [pl.BlockSpec((tm, a.shape[1]), lambda i, j: (i, 0)) for a, _ in pairs]
    if nt:
        in_specs += [pl.BlockSpec((tn, w.shape[1]), lambda i, j: (j, 0)) for _, w in pairs]
    else:
        in_specs += [pl.BlockSpec((w.shape[0], tn), lambda i, j: (0, j)) for _, w in pairs]
    args = [a for a, _ in pairs] + [w for _, w in pairs]
    if has_res:
        in_specs.append(pl.BlockSpec((tm, tn), lambda i, j: (i, j)))
        args.append(residual)
    outs = pl.pallas_call(
        body, grid=(s // tm, n // tn), in_specs=in_specs,
        out_specs=[pl.BlockSpec((tm, tn), lambda i, j: (i, j)) for _ in out_dtypes],
        out_shape=[SDS((s, n), dt) for dt in out_dtypes],
        compiler_params=_cp(2), name=name)(*args)
    return outs


def _linear_normbwd(pairs, x, xcol, g, d_res, out_dtypes, tm, name):
    s = pairs[0][0].shape[0]
    dn = pairs[0][1].shape[0]
    n_pairs = len(pairs)
    has_res = d_res is not None

    def body(*refs):
        a_refs = refs[:n_pairs]
        w_refs = refs[n_pairs:2 * n_pairs]
        x_ref, g_ref = refs[2 * n_pairs], refs[2 * n_pairs + 1]
        k = 2 * n_pairs + 2
        res_ref = refs[k] if has_res else None
        k += int(has_res)
        outs = refs[k:-1]
        dg_ref = refs[-1]
        dh = None
        for a_ref, w_ref in zip(a_refs, w_refs):
            d = lax.dot_general(a_ref[...].astype(BF16), w_ref[...], (((1,), (1,)), ((), ())), preferred_element_type=F32)
            dh = d if dh is None else dh + d
        xv = x_ref[...]
        r = lax.rsqrt(jnp.mean(xv * xv, axis=-1, keepdims=True) + EPS)
        y = xv * r

        @pl.when(pl.program_id(0) == 0)
        def _():
            dg_ref[...] = jnp.zeros_like(dg_ref)

        dg_ref[...] += jnp.sum(dh * y, axis=0, keepdims=True)
        dy = dh * g_ref[...]
        dx = r * (dy - y * jnp.mean(dy * y, axis=-1, keepdims=True))
        if has_res:
            dx = res_ref[...] + dx
        for o in outs:
            o[...] = dx.astype(o.dtype)

    in_specs = [pl.BlockSpec((tm, a.shape[1]), lambda i: (i, 0)) for a, _ in pairs]
    in_specs += [pl.BlockSpec((dn, w.shape[1]), lambda i: (0, 0)) for _, w in pairs]
    in_specs += [pl.BlockSpec((tm, dn), lambda i: (i, xcol)), pl.BlockSpec((1, dn), lambda i: (0, 0))]
    args = [a for a, _ in pairs] + [w for _, w in pairs] + [x, g]
    if has_res:
        in_specs.append(pl.BlockSpec((tm, dn), lambda i: (i, 0)))
        args.append(d_res)
    outs = pl.pallas_call(
        body, grid=(s // tm,), in_specs=in_specs,
        out_specs=[pl.BlockSpec((tm, dn), lambda i: (i, 0)) for _ in out_dtypes] + [pl.BlockSpec((1, dn), lambda i: (0, 0))],
        out_shape=[SDS((s, dn), dt) for dt in out_dtypes] + [SDS((1, dn), F32)],
        compiler_params=_cp(1), name=name)(*args)
    return outs


def _dw_matmul(a, b, tk, tn, ts, name):
    s, ka = a.shape
    n = b.shape[1]

    def body(a_ref, b_ref, o_ref):
        @pl.when(pl.program_id(2) == 0)
        def _():
            o_ref[...] = jnp.zeros_like(o_ref)

        o_ref[...] += lax.dot_general(a_ref[...].astype(BF16), b_ref[...].astype(BF16), (((0,), (0,)), ((), ())),
                                      preferred_element_type=F32)

    return pl.pallas_call(
        body, grid=(ka // tk, n // tn, s // ts),
        in_specs=[pl.BlockSpec((ts, tk), lambda k, j, t: (t, k)), pl.BlockSpec((ts, tn), lambda k, j, t: (t, j))],
        out_specs=pl.BlockSpec((tk, tn), lambda k, j, t: (k, j)),
        out_shape=SDS((ka, n), F32), compiler_params=_cp(3), name=name)(a, b)


def _dw(a, b, name):
    s, ka = a.shape
    n = b.shape[1]
    tk = ka if ka <= 1024 else ka // 2
    tn = n if n <= 1024 else (n // 2 if n == D_FF else 512)
    return _dw_matmul(a, b, tk, tn, _row_tile(s, 512), name)


def _prev_halo(tm, halo):
    return lambda i: (jnp.maximum(i * (tm // halo) - 1, 0), 0)


def _next_halo(tm, halo, s):
    return lambda i: (jnp.minimum((i + 1) * (tm // halo), s // halo - 1), 0)


def _conv_fwd(z, b_in, w32, b_dw, ln_g, ln_b, tm):
    s = z.shape[0]
    c = CONV_CH

    def body(z_ref, zh_ref, bin_ref, w_ref, bdw_ref, lg_ref, lb_ref, u_ref, u0_ref, u1_ref, ext):
        i = pl.program_id(0)

        def glu(zz):
            zz = zz + bin_ref[...]
            return zz[:, :c] * jax.nn.sigmoid(zz[:, c:])

        u0 = glu(z_ref[...])
        u0_ref[...] = u0
        ext[0:CONV_HALO, :] = jnp.where(i > 0, glu(zh_ref[...]), 0.0)
        ext[CONV_HALO:, :] = u0
        off = CONV_HALO - (CONV_WIDTH - 1)
        for r in range(tm // 64):
            for cb in range(c // 128):
                cs = slice(cb * 128, (cb + 1) * 128)
                acc = jnp.zeros((64, 128), F32)
                for k in range(CONV_WIDTH):
                    acc = acc + ext[r * 64 + off + k: r * 64 + off + k + 64, cs] * w_ref[k:k + 1, cs]
                u1_ref[r * 64:(r + 1) * 64, cs] = acc + bdw_ref[:, cs]
        u1 = u1_ref[...]
        mu = jnp.mean(u1, axis=-1, keepdims=True)
        xc = u1 - mu
        y = xc * lax.rsqrt(jnp.mean(xc * xc, axis=-1, keepdims=True) + EPS)
        y = y * lg_ref[...] + lb_ref[...]
        u_ref[...] = (y * jax.nn.sigmoid(y)).astype(BF16)

    row = lambda i: (i, 0)
    fix = lambda i: (0, 0)
    return pl.pallas_call(
        body, grid=(s // tm,),
        in_specs=[pl.BlockSpec((tm, 2 * c), row), pl.BlockSpec((CONV_HALO, 2 * c), _prev_halo(tm, CONV_HALO)),
                  pl.BlockSpec((1, 2 * c), fix), pl.BlockSpec((32, c), fix), pl.BlockSpec((1, c), fix),
                  pl.BlockSpec((1, c), fix), pl.BlockSpec((1, c), fix)],
        out_specs=[pl.BlockSpec((tm, c), row)] * 3,
        out_shape=[SDS((s, c), BF16), SDS((s, c), F32), SDS((s, c), F32)],
        scratch_shapes=[pltpu.VMEM((tm + CONV_HALO, c), F32)],
        compiler_params=_cp(1), name="conv_fwd")(z, z, b_in, w32, b_dw, ln_g, ln_b)


def _conv_bwd_ln(d_u, u1, ln_g, ln_b, tm):
    s = d_u.shape[0]
    c = CONV_CH

    def body(du_ref, u1_ref, lg_ref, lb_ref, du1_ref, dlg_ref, dlb_ref, dbdw_ref):
        @pl.when(pl.program_id(0) == 0)
        def _():
            dlg_ref[...] = jnp.zeros_like(dlg_ref)
            dlb_ref[...] = jnp.zeros_like(dlb_ref)
            dbdw_ref[...] = jnp.zeros_like(dbdw_ref)

        u1 = u1_ref[...]
        mu = jnp.mean(u1, axis=-1, keepdims=True)
        xc = u1 - mu
        rs = lax.rsqrt(jnp.mean(xc * xc, axis=-1, keepdims=True) + EPS)
        xh = xc * rs
        y = xh * lg_ref[...] + lb_ref[...]
        sg = jax.nn.sigmoid(y)
        dy = du_ref[...] * (sg * (1.0 + y * (1.0 - sg)))
        dlg_ref[...] += jnp.sum(dy * xh, axis=0, keepdims=True)
        dlb_ref[...] += jnp.sum(dy, axis=0, keepdims=True)
        dxh = dy * lg_ref[...]
        du1 = rs * (dxh - jnp.mean(dxh, axis=-1, keepdims=True) - xh * jnp.mean(dxh * xh, axis=-1, keepdims=True))
        dbdw_ref[...] += jnp.sum(du1, axis=0, keepdims=True)
        du1_ref[...] = du1

    row = lambda i: (i, 0)
    fix = lambda i: (0, 0)
    return pl.pallas_call(
        body, grid=(s // tm,),
        in_specs=[pl.BlockSpec((tm, c), row), pl.BlockSpec((tm, c), row), pl.BlockSpec((1, c), fix), pl.BlockSpec((1, c), fix)],
        out_specs=[pl.BlockSpec((tm, c), row)] + [pl.BlockSpec((1, c), fix)] * 3,
        out_shape=[SDS((s, c), F32)] + [SDS((1, c), F32)] * 3,
        compiler_params=_cp(1), name="conv_bwd_ln")(d_u, u1, ln_g, ln_b)


def _conv_bwd_dw(d_u1, u0, z, b_in, w32, tm):
    s = d_u1.shape[0]
    c = CONV_CH

    def body(d_ref, dn_ref, u0_ref, u0p_ref, z_ref, bin_ref, w_ref, dz_ref, dw_ref, dbin_ref, extd, extu, du0):
        i = pl.program_id(0)
        last = pl.num_programs(0) - 1

        @pl.when(i == 0)
        def _():
            dw_ref[...] = jnp.zeros_like(dw_ref)
            dbin_ref[...] = jnp.zeros_like(dbin_ref)

        extd[0:tm, :] = d_ref[...]
        extd[tm:, :] = jnp.where(i < last, dn_ref[...], 0.0)
        extu[0:CONV_HALO, :] = jnp.where(i > 0, u0p_ref[...], 0.0)
        extu[CONV_HALO:, :] = u0_ref[...]
        off = CONV_HALO - (CONV_WIDTH - 1)
        for r in range(tm // 64):
            for cb in range(c // 128):
                cs = slice(cb * 128, (cb + 1) * 128)
                acc = jnp.zeros((64, 128), F32)
                for k in range(CONV_WIDTH):
                    o = r * 64 + (CONV_WIDTH - 1) - k
                    acc = acc + extd[o:o + 64, cs] * w_ref[k:k + 1, cs]
                du0[r * 64:(r + 1) * 64, cs] = acc
        for cb in range(c // 128):
            cs = slice(cb * 128, (cb + 1) * 128)
            for k in range(CONV_WIDTH):
                part = jnp.zeros((8, 128), F32)
                for r in range(tm // 64):
                    p = d_ref[r * 64:(r + 1) * 64, cs] * extu[r * 64 + off + k: r * 64 + off + k + 64, cs]
                    for q in range(8):
                        part = part + p[q * 8:(q + 1) * 8, :]
                dw_ref[k:k + 1, cs] += jnp.sum(part, axis=0, keepdims=True)
        zz = z_ref[...] + bin_ref[...]
        a = zz[:, :c]
        sg = jax.nn.sigmoid(zz[:, c:])
        d0 = du0[...]
        da = d0 * sg
        dgt = d0 * a * (sg * (1.0 - sg))
        dbin_ref[:, :c] += jnp.sum(da, axis=0, keepdims=True)
        dbin_ref[:, c:] += jnp.sum(dgt, axis=0, keepdims=True)
        dz_ref[:, :c] = da.astype(BF16)
        dz_ref[:, c:] = dgt.astype(BF16)

    row = lambda i: (i, 0)
    fix = lambda i: (0, 0)
    return pl.pallas_call(
        body, grid=(s // tm,),
        in_specs=[pl.BlockSpec((tm, c), row), pl.BlockSpec((CONV_HALO, c), _next_halo(tm, CONV_HALO, s)),
                  pl.BlockSpec((tm, c), row), pl.BlockSpec((CONV_HALO, c), _prev_halo(tm, CONV_HALO)),
                  pl.BlockSpec((tm, 2 * c), row), pl.BlockSpec((1, 2 * c), fix), pl.BlockSpec((32, c), fix)],
        out_specs=[pl.BlockSpec((tm, 2 * c), row), pl.BlockSpec((32, c), fix), pl.BlockSpec((1, 2 * c), fix)],
        out_shape=[SDS((s, 2 * c), BF16), SDS((32, c), F32), SDS((1, 2 * c), F32)],
        scratch_shapes=[pltpu.VMEM((tm + CONV_HALO, c), F32), pltpu.VMEM((tm + CONV_HALO, c), F32), pltpu.VMEM((tm, c), F32)],
        compiler_params=_cp(1), name="conv_bwd_dw")(d_u1, d_u1, u0, u0, z, b_in, w32)


def _partner(v, lane):
    up = pltpu.roll(v, HEAD_PAD - ROPE // 2, 1)
    dn = pltpu.roll(v, ROPE // 2, 1)
    lo = (lane >= NOPE) & (lane < NOPE + ROPE // 2)
    hi = (lane >= NOPE + ROPE // 2) & (lane < HEAD_DIM)
    return jnp.where(lo, up, jnp.where(hi, dn, 0.0))


def _mla_prep(q_raw, kv_raw, z, cosf, sinf, gq, gk, tm):
    s = q_raw.shape[0]

    def body(q_ref, kv_ref, kr_ref, c_ref, s_ref, gq_ref, gk_ref, qo_ref, ko_ref, vo_ref):
        lane = lax.broadcasted_iota(jnp.int32, (tm, HEAD_PAD), 1)
        cf = c_ref[...]
        sf = s_ref[...]

        def norm_rope(t, g_ref):
            r = lax.rsqrt(jnp.sum(t * t, axis=-1, keepdims=True) * (1.0 / HEAD_DIM) + EPS)
            tn = (t * r) * g_ref[...]
            return tn * cf + _partner(tn, lane) * sf

        q = q_ref[...]
        qo_ref[...] = (norm_rope(q, gq_ref) * (ATT_SCALE * LOG2E)).astype(BF16)
        kv = kv_ref[...]
        kpre = jnp.where(lane < NOPE, kv, 0.0) + kr_ref[...]
        ko_ref[...] = norm_rope(kpre, gk_ref).astype(BF16)
        vo_ref[...] = jnp.where(lane >= NOPE, kv, 0.0).astype(BF16)

    hb = lambda i, h: (i, h)
    return pl.pallas_call(
        body, grid=(s // tm, HEADS),
        in_specs=[pl.BlockSpec((tm, HEAD_PAD), hb), pl.BlockSpec((tm, HEAD_PAD), hb),
                  pl.BlockSpec((tm, HEAD_PAD), lambda i, h: (i, IN_COLS_PAD // HEAD_PAD - 1)),
                  pl.BlockSpec((tm, HEAD_PAD), lambda i, h: (i, 0)), pl.BlockSpec((tm, HEAD_PAD), lambda i, h: (i, 0)),
                  pl.BlockSpec((1, HEAD_PAD), lambda i, h: (0, 0)), pl.BlockSpec((1, HEAD_PAD), lambda i, h: (0, 0))],
        out_specs=[pl.BlockSpec((tm, HEAD_PAD), hb)] * 3,
        out_shape=[SDS((s, HEADS * HEAD_PAD), BF16)] * 3,
        compiler_params=_cp(2), name="mla_prep")(q_raw, kv_raw, z, cosf, sinf, gq, gk)


def _mla_prep_bwd(dqp, dkp, dvp, q_raw, kv_raw, z, cosf, sinf, gq, gk, tm):
    s = q_raw.shape[0]

    def body(dq_ref, dk_ref, dv_ref, q_ref, kv_ref, kr_ref, c_ref, s_ref, gq_ref, gk_ref,
             dqo_ref, dkvo_ref, dkr_ref, dgq_ref, dgk_ref):
        i = pl.program_id(0)
        h = pl.program_id(1)
        lane = lax.broadcasted_iota(jnp.int32, (tm, HEAD_PAD), 1)
        cf = c_ref[...]
        sf = s_ref[...]

        @pl.when((i == 0) & (h == 0))
        def _():
            dgq_ref[...] = jnp.zeros_like(dgq_ref)
            dgk_ref[...] = jnp.zeros_like(dgk_ref)

        def norm_rope_bwd(t, d_out, g_ref, dg_ref):
            r = lax.rsqrt(jnp.sum(t * t, axis=-1, keepdims=True) * (1.0 / HEAD_DIM) + EPS)
            th = t * r
            dn = d_out * cf + _partner(d_out * sf, lane)
            dg_ref[...] += jnp.sum(dn * th, axis=0, keepdims=True)
            dh = dn * g_ref[...]
            return r * (dh - th * (jnp.sum(dh * th, axis=-1, keepdims=True) * (1.0 / HEAD_DIM)))

        dq = norm_rope_bwd(q_ref[...], dq_ref[...] * ATT_SCALE, gq_ref, dgq_ref)
        dqo_ref[...] = dq.astype(BF16)
        kv = kv_ref[...]
        kpre = jnp.where(lane < NOPE, kv, 0.0) + kr_ref[...]
        dkpre = norm_rope_bwd(kpre, dk_ref[...] * LN2, gk_ref, dgk_ref)
        dkvo_ref[...] = jnp.where(lane < NOPE, dkpre, dv_ref[...]).astype(BF16)
        dkr = jnp.where((lane >= NOPE) & (lane < HEAD_DIM), dkpre, 0.0)

        @pl.when(h == 0)
        def _():
            dkr_ref[...] = dkr

        @pl.when(h > 0)
        def _():
            dkr_ref[...] += dkr

    hb = lambda i, h: (i, h)
    r0 = lambda i, h: (i, 0)
    fix = lambda i, h: (0, 0)
    blk = pl.BlockSpec((tm, HEAD_PAD), hb)
    return pl.pallas_call(
        body, grid=(s // tm, HEADS),
        in_specs=[blk, blk, blk, blk, blk, pl.BlockSpec((tm, HEAD_PAD), lambda i, h: (i, IN_COLS_PAD // HEAD_PAD - 1)),
                  pl.BlockSpec((tm, HEAD_PAD), r0), pl.BlockSpec((tm, HEAD_PAD), r0),
                  pl.BlockSpec((1, HEAD_PAD), fix), pl.BlockSpec((1, HEAD_PAD), fix)],
        out_specs=[blk, blk, pl.BlockSpec((tm, HEAD_PAD), r0), pl.BlockSpec((1, HEAD_PAD), fix), pl.BlockSpec((1, HEAD_PAD), fix)],
        out_shape=[SDS((s, HEADS * HEAD_PAD), BF16), SDS((s, HEADS * HEAD_PAD), BF16), SDS((s, HEAD_PAD), F32),
                   SDS((1, HEAD_PAD), F32), SDS((1, HEAD_PAD), F32)],
        compiler_params=_cp(2), name="mla_prep_bwd")(dqp, dkp, dvp, q_raw, kv_raw, z, cosf, sinf, gq, gk)


def _tri_pairs(n, row_major):
    if row_major:
        pairs = [(i, j) for i in range(n) for j in range(i + 1)]
    else:
        pairs = [(i, j) for j in range(n) for i in range(j, n)]
    ii = np.array([p[0] for p in pairs], np.int32)
    jj = np.array([p[1] for p in pairs], np.int32)
    return jnp.asarray(ii), jnp.asarray(jj)


def _visible_cols(r, tb):
    vis = (r + 1) * CHUNK
    return vis, min(tb, -(-vis // 128) * 128)


def _attn_fwd(qp, kp, vp, tb):
    s = qp.shape[0]
    nb = s // tb
    ii, jj = _tri_pairs(nb, True)

    def body(ii_ref, jj_ref, q_ref, k_ref, v_ref, of_ref, ob_ref, lse_ref, m_sc, l_sc, acc_sc, sc_sc, p_sc):
        t = pl.program_id(1)
        i = ii_ref[t]
        j = jj_ref[t]

        @pl.when(j == 0)
        def _():
            m_sc[...] = jnp.full_like(m_sc, -jnp.inf)
            l_sc[...] = jnp.zeros_like(l_sc)
            acc_sc[...] = jnp.zeros_like(acc_sc)

        def step(diag):
            sc_sc[...] = lax.dot_general(q_ref[...], k_ref[...], (((1,), (1,)), ((), ())), preferred_element_type=F32)
            for r in range(tb // CHUNK):
                rows = slice(r * CHUNK, (r + 1) * CHUNK)
                vis, ncol = _visible_cols(r, tb) if diag else (tb, tb)
                sc = sc_sc[rows, :ncol]
                if vis < ncol:
                    col = lax.broadcasted_iota(jnp.int32, (CHUNK, ncol), 1)
                    sc = jnp.where(col < vis, sc, -jnp.inf)
                m_old = m_sc[rows, 0:1]
                m_new = jnp.maximum(m_old, jnp.max(sc, axis=-1, keepdims=True))
                alpha = jnp.exp2(m_old - m_new)
                p = jnp.exp2(sc - m_new)
                l_new = alpha * l_sc[rows, 0:1] + jnp.sum(p, axis=-1, keepdims=True)
                m_sc[rows, :] = jnp.broadcast_to(m_new, (CHUNK, HEAD_PAD))
                l_sc[rows, :] = jnp.broadcast_to(l_new, (CHUNK, HEAD_PAD))
                acc_sc[rows, :] = alpha * acc_sc[rows, :]
                p_sc[rows, :ncol] = p.astype(BF16)
                if ncol < tb:
                    p_sc[rows, ncol:] = jnp.zeros((CHUNK, tb - ncol), BF16)
            acc_sc[...] += jnp.dot(p_sc[...], v_ref[...], preferred_element_type=F32)

        @pl.when(j < i)
        def _():
            step(False)

        @pl.when(j == i)
        def _():
            step(True)
            l = l_sc[...]
            o = acc_sc[...] / l
            of_ref[...] = o
            ob_ref[...] = o.astype(BF16)
            lse_ref[...] = m_sc[...] + jnp.log(l) * LOG2E

    qmap = lambda h, t, ii_ref, jj_ref: (ii_ref[t], h)
    kmap = lambda h, t, ii_ref, jj_ref: (jj_ref[t], h)
    blk = (tb, HEAD_PAD)
    gs = pltpu.PrefetchScalarGridSpec(
        num_scalar_prefetch=2, grid=(HEADS, int(ii.shape[0])),
        in_specs=[pl.BlockSpec(blk, qmap), pl.BlockSpec(blk, kmap), pl.BlockSpec(blk, kmap)],
        out_specs=[pl.BlockSpec(blk, qmap)] * 3,
        scratch_shapes=[pltpu.VMEM(blk, F32)] * 3 + [pltpu.VMEM((tb, tb), F32), pltpu.VMEM((tb, tb), BF16)])
    w = HEADS * HEAD_PAD
    return pl.pallas_call(body, grid_spec=gs, out_shape=[SDS((s, w), F32), SDS((s, w), BF16), SDS((s, w), F32)],
                          compiler_params=_cp(2), name="attn_fwd")(ii, jj, qp, kp, vp)


def _attn_delta(do, o, tm):
    s = do.shape[0]

    def body(do_ref, o_ref, d_ref):
        d_ref[...] = jnp.broadcast_to(jnp.sum(do_ref[...] * o_ref[...], axis=-1, keepdims=True), d_ref.shape)

    blk = pl.BlockSpec((tm, HEAD_PAD), lambda i, h: (i, h))
    return pl.pallas_call(body, grid=(s // tm, HEADS), in_specs=[blk, blk], out_specs=blk,
                          out_shape=SDS(do.shape, F32), compiler_params=_cp(2), name="attn_delta")(do, o)


def _attn_bwd(qp, kp, vp, dob, lse, delta, tb):
    s = qp.shape[0]
    nb = s // tb
    ii, jj = _tri_pairs(nb, False)

    def body(ii_ref, jj_ref, q_ref, k_ref, v_ref, do_ref, lse_ref, dl_ref, dq_ref, dk_ref, dv_ref, sc_sc, dp_sc, p_sc, ds_sc):
        t = pl.program_id(1)
        i = ii_ref[t]
        j = jj_ref[t]

        @pl.when(t == 0)
        def _():
            dq_ref[...] = jnp.zeros_like(dq_ref)

        @pl.when(i == j)
        def _():
            dk_ref[...] = jnp.zeros_like(dk_ref)
            dv_ref[...] = jnp.zeros_like(dv_ref)

        def step(diag):
            q = q_ref[...]
            k = k_ref[...]
            do = do_ref[...]
            sc_sc[...] = lax.dot_general(q, k, (((1,), (1,)), ((), ())), preferred_element_type=F32)
            dp_sc[...] = lax.dot_general(do, v_ref[...], (((1,), (1,)), ((), ())), preferred_element_type=F32)
            for r in range(tb // CHUNK):
                rows = slice(r * CHUNK, (r + 1) * CHUNK)
                vis, ncol = _visible_cols(r, tb) if diag else (tb, tb)
                p = jnp.exp2(sc_sc[rows, :ncol] - lse_ref[rows, 0:1])
                if vis < ncol:
                    col = lax.broadcasted_iota(jnp.int32, (CHUNK, ncol), 1)
                    p = jnp.where(col < vis, p, 0.0)
                ds = p * (dp_sc[rows, :ncol] - dl_ref[rows, 0:1])
                p_sc[rows, :ncol] = p.astype(BF16)
                ds_sc[rows, :ncol] = ds.astype(BF16)
                if ncol < tb:
                    p_sc[rows, ncol:] = jnp.zeros((CHUNK, tb - ncol), BF16)
                    ds_sc[rows, ncol:] = jnp.zeros((CHUNK, tb - ncol), BF16)
            dv_ref[...] += lax.dot_general(p_sc[...], do, (((0,), (0,)), ((), ())), preferred_element_type=F32)
            ds = ds_sc[...]
            dk_ref[...] += lax.dot_general(ds, q, (((0,), (0,)), ((), ())), preferred_element_type=F32)
            rows = pl.ds(pl.multiple_of(i * tb, tb), tb)
            dq_ref[rows, :] += jnp.dot(ds, k, preferred_element_type=F32)

        @pl.when(j < i)
        def _():
            step(False)

        @pl.when(j == i)
        def _():
            step(True)

    qmap = lambda h, t, ii_ref, jj_ref: (ii_ref[t], h)
    kmap = lambda h, t, ii_ref, jj_ref: (jj_ref[t], h)
    blk = (tb, HEAD_PAD)
    gs = pltpu.PrefetchScalarGridSpec(
        num_scalar_prefetch=2, grid=(HEADS, int(ii.shape[0])),
        in_specs=[pl.BlockSpec(blk, qmap), pl.BlockSpec(blk, kmap), pl.BlockSpec(blk, kmap), pl.BlockSpec(blk, qmap),
                  pl.BlockSpec(blk, qmap), pl.BlockSpec(blk, qmap)],
        out_specs=[pl.BlockSpec((s, HEAD_PAD), lambda h, t, ii_ref, jj_ref: (0, h)), pl.BlockSpec(blk, kmap), pl.BlockSpec(blk, kmap)],
        scratch_shapes=[pltpu.VMEM((tb, tb), F32), pltpu.VMEM((tb, tb), F32), pltpu.VMEM((tb, tb), BF16), pltpu.VMEM((tb, tb), BF16)])
    w = HEADS * HEAD_PAD
    return pl.pallas_call(body, grid_spec=gs, out_shape=[SDS((s, w), F32)] * 3,
                          compiler_params=_cp(2), name="attn_bwd")(ii, jj, qp, kp, vp, dob, lse, delta)


def _head_norm(t, g):
    r = lax.rsqrt(jnp.mean(t * t, axis=-1, keepdims=True) + EPS)
    th = t * r
    return r, th, th * g


def _softmax_rows(sc):
    m = jnp.max(sc, axis=-1, keepdims=True)
    e = jnp.exp(sc - m)
    return e / jnp.sum(e, axis=-1, keepdims=True)


def _memattn_fwd(qm, kvm, gq, gk, tm):
    s = qm.shape[0]
    hd = MEM_HEAD_DIM

    def body(q_ref, k_ref, v_ref, gq_ref, gk_ref, o_ref):
        _, _, qn = _head_norm(q_ref[...], gq_ref[...])
        _, _, kn = _head_norm(k_ref[...], gk_ref[...])
        sc = lax.dot_general(qn.astype(BF16), kn.astype(BF16), (((1,), (1,)), ((), ())), preferred_element_type=F32)
        p = _softmax_rows(sc * (1.0 / math.sqrt(hd)))
        o_ref[...] = jnp.dot(p.astype(BF16), v_ref[...].astype(BF16), preferred_element_type=F32).astype(BF16)

    fix = lambda i, h: (0, 0)
    return pl.pallas_call(
        body, grid=(s // tm, MEM_HEADS),
        in_specs=[pl.BlockSpec((tm, hd), lambda i, h: (i, h)), pl.BlockSpec((MEM_LEN, hd), lambda i, h: (0, h)),
                  pl.BlockSpec((MEM_LEN, hd), lambda i, h: (0, MEM_HEADS + h)), pl.BlockSpec((1, hd), fix), pl.BlockSpec((1, hd), fix)],
        out_specs=pl.BlockSpec((tm, hd), lambda i, h: (i, h)),
        out_shape=SDS((s, MEM_HEADS * hd), BF16), compiler_params=_cp(2), name="memattn_fwd")(qm, kvm, kvm, gq, gk)


def _memattn_bwd(qm, kvm, d_o, gq, gk, tm):
    s = qm.shape[0]
    hd = MEM_HEAD_DIM

    def body(q_ref, k_ref, v_ref, do_ref, gq_ref, gk_ref, dq_ref, dk_ref, dv_ref, dgq_ref, dgk_ref, dkn_sc):
        h = pl.program_id(0)
        i = pl.program_id(1)
        last = pl.num_programs(1) - 1

        @pl.when((h == 0) & (i == 0))
        def _():
            dgq_ref[...] = jnp.zeros_like(dgq_ref)
            dgk_ref[...] = jnp.zeros_like(dgk_ref)

        @pl.when(i == 0)
        def _():
            dv_ref[...] = jnp.zeros_like(dv_ref)
            dkn_sc[...] = jnp.zeros_like(dkn_sc)

        rq, qh, qn = _head_norm(q_ref[...], gq_ref[...])
        rk, kh, kn = _head_norm(k_ref[...], gk_ref[...])
        qnb = qn.astype(BF16)
        knb = kn.astype(BF16)
        scale = 1.0 / math.sqrt(hd)
        sc = lax.dot_general(qnb, knb, (((1,), (1,)), ((), ())), preferred_element_type=F32)
        p = _softmax_rows(sc * scale)
        do = do_ref[...].astype(BF16)
        dp = lax.dot_general(do, v_ref[...].astype(BF16), (((1,), (1,)), ((), ())), preferred_element_type=F32)
        dv_ref[...] += lax.dot_general(p.astype(BF16), do, (((0,), (0,)), ((), ())), preferred_element_type=F32)
        ds = ((p * (dp - jnp.sum(dp * p, axis=-1, keepdims=True))) * scale).astype(BF16)
        dqn = jnp.dot(ds, knb, preferred_element_type=F32)
        dkn_sc[...] += lax.dot_general(ds, qnb, (((0,), (0,)), ((), ())), preferred_element_type=F32)
        dgq_ref[...] += jnp.sum(dqn * qh, axis=0, keepdims=True)
        dqh = dqn * gq_ref[...]
        dq_ref[...] = (rq * (dqh - qh * jnp.mean(dqh * qh, axis=-1, keepdims=True))).astype(BF16)

        @pl.when(i == last)
        def _():
            dkn = dkn_sc[...]
            dgk_ref[...] += jnp.sum(dkn * kh, axis=0, keepdims=True)
            dkh = dkn * gk_ref[...]
            dk_ref[...] = rk * (dkh - kh * jnp.mean(dkh * kh, axis=-1, keepdims=True))

    fix = lambda h, i: (0, 0)
    qb = pl.BlockSpec((tm, hd), lambda h, i: (i, h))
    kb = pl.BlockSpec((MEM_LEN, hd), lambda h, i: (0, h))
    return pl.pallas_call(
        body, grid=(MEM_HEADS, s // tm),
        in_specs=[qb, kb, pl.BlockSpec((MEM_LEN, hd), lambda h, i: (0, MEM_HEADS + h)), qb,
                  pl.BlockSpec((1, hd), fix), pl.BlockSpec((1, hd), fix)],
        out_specs=[qb, kb, kb, pl.BlockSpec((1, hd), fix), pl.BlockSpec((1, hd), fix)],
        out_shape=[SDS((s, MEM_HEADS * hd), BF16), SDS((MEM_LEN, MEM_HEADS * hd), F32), SDS((MEM_LEN, MEM_HEADS * hd), F32),
                   SDS((1, hd), F32), SDS((1, hd), F32)],
        scratch_shapes=[pltpu.VMEM((MEM_LEN, hd), F32)],
        compiler_params=_cp(2), name="memattn_bwd")(qm, kvm, kvm, d_o, gq, gk)


def _ffn_specs(tm, tn, nbj, s, order_ji):
    if order_ji:
        ij = lambda f: (lambda j, i: f(i, j))
    else:
        ij = lambda f: f
    prev = lambda i: jnp.maximum(i * (tm // FFN_HALO) - 1, 0)
    cur_g = pl.BlockSpec((tm, tn), ij(lambda i, j: (i, j)))
    cur_v = pl.BlockSpec((tm, tn), ij(lambda i, j: (i, j + nbj)))
    halo_g = pl.BlockSpec((FFN_HALO, tn), ij(lambda i, j: (prev(i), j)))
    halo_v = pl.BlockSpec((FFN_HALO, tn), ij(lambda i, j: (prev(i), j + nbj)))
    w_g = pl.BlockSpec((8, tn), ij(lambda i, j: (0, j)))
    w_v = pl.BlockSpec((8, tn), ij(lambda i, j: (0, j + nbj)))
    b_g = pl.BlockSpec((1, tn), ij(lambda i, j: (0, j)))
    b_v = pl.BlockSpec((1, tn), ij(lambda i, j: (0, j + nbj)))
    return cur_g, cur_v, halo_g, halo_v, w_g, w_v, b_g, b_v


def _conv3(cur_ref, halo_ref, w_ref, b_ref, ext, first, tm):
    ext[0:FFN_HALO, :] = jnp.where(first, 0.0, halo_ref[...])
    ext[FFN_HALO:, :] = cur_ref[...]
    return (w_ref[0:1, :] * ext[FFN_HALO - 2:FFN_HALO - 2 + tm, :] + w_ref[1:2, :] * ext[FFN_HALO - 1:FFN_HALO - 1 + tm, :]
            + w_ref[2:3, :] * ext[FFN_HALO:FFN_HALO + tm, :] + b_ref[...])


def _ffn_fwd(up0, w8, b, tm, tn):
    s = up0.shape[0]
    nbj = D_FF // tn

    def body(g_ref, v_ref, gh_ref, vh_ref, wg_ref, wv_ref, bg_ref, bv_ref, act_ref, ext):
        first = pl.program_id(0) == 0
        ug = _conv3(g_ref, gh_ref, wg_ref, bg_ref, ext, first, tm)
        uv = _conv3(v_ref, vh_ref, wv_ref, bv_ref, ext, first, tm)
        act_ref[...] = ((ug * jax.nn.sigmoid(ug)) * uv).astype(BF16)

    specs = _ffn_specs(tm, tn, nbj, s, False)
    return pl.pallas_call(
        body, grid=(s // tm, nbj), in_specs=list(specs),
        out_specs=pl.BlockSpec((tm, tn), lambda i, j: (i, j)), out_shape=SDS((s, D_FF), BF16),
        scratch_shapes=[pltpu.VMEM((tm + FFN_HALO, tn), F32)],
        compiler_params=_cp(2), name="ffn_fwd")(up0, up0, up0, up0, w8, w8, b, b)


def _ffn_bwd_act(d_act, up0, w8, b, tm, tn):
    s = up0.shape[0]
    nbj = D_FF // tn

    def body(da_ref, g_ref, v_ref, gh_ref, vh_ref, wg_ref, wv_ref, bg_ref, bv_ref, dg_ref, dv_ref, dbg_ref, dbv_ref, ext):
        i = pl.program_id(1)
        first = i == 0

        @pl.when(first)
        def _():
            dbg_ref[...] = jnp.zeros_like(dbg_ref)
            dbv_ref[...] = jnp.zeros_like(dbv_ref)

        ug = _conv3(g_ref, gh_ref, wg_ref, bg_ref, ext, first, tm)
        uv = _conv3(v_ref, vh_ref, wv_ref, bv_ref, ext, first, tm)
        sg = jax.nn.sigmoid(ug)
        da = da_ref[...]
        dval = da * (ug * sg)
        dgate = da * uv * (sg * (1.0 + ug * (1.0 - sg)))
        dg_ref[...] = dgate
        dv_ref[...] = dval
        dbg_ref[...] += jnp.sum(dgate, axis=0, keepdims=True)
        dbv_ref[...] += jnp.sum(dval, axis=0, keepdims=True)

    specs = _ffn_specs(tm, tn, nbj, s, True)
    cur = pl.BlockSpec((tm, tn), lambda j, i: (i, j))
    acc = pl.BlockSpec((1, tn), lambda j, i: (0, j))
    return pl.pallas_call(
        body, grid=(nbj, s // tm), in_specs=[cur] + list(specs),
        out_specs=[cur, cur, acc, acc],
        out_shape=[SDS((s, D_FF), F32), SDS((s, D_FF), F32), SDS((1, D_FF), F32), SDS((1, D_FF), F32)],
        scratch_shapes=[pltpu.VMEM((tm + FFN_HALO, tn), F32)],
        compiler_params=_cp(2), name="ffn_bwd_act")(d_act, up0, up0, up0, up0, w8, w8, b, b)


def _ffn_bwd_conv(d_upg, d_upv, up0, w8, tm, tn):
    s = up0.shape[0]
    nbj = D_FF // tn

    def body(dg_ref, dv_ref, dgn_ref, dvn_ref, g_ref, v_ref, gh_ref, vh_ref, wg_ref, wv_ref,
             og_ref, ov_ref, dwg_ref, dwv_ref, extd, extu):
        i = pl.program_id(1)
        first = i == 0
        last = i == pl.num_programs(1) - 1

        @pl.when(first)
        def _():
            dwg_ref[...] = jnp.zeros_like(dwg_ref)
            dwv_ref[...] = jnp.zeros_like(dwv_ref)

        def half(d_ref, dn_ref, u_ref, uh_ref, w_ref, o_ref, dw_ref):
            extd[0:tm, :] = d_ref[...]
            extd[tm:, :] = jnp.where(last, 0.0, dn_ref[...])
            o_ref[...] = (w_ref[2:3, :] * extd[0:tm, :] + w_ref[1:2, :] * extd[1:1 + tm, :]
                          + w_ref[0:1, :] * extd[2:2 + tm, :]).astype(BF16)
            extu[0:FFN_HALO, :] = jnp.where(first, 0.0, uh_ref[...])
            extu[FFN_HALO:, :] = u_ref[...]
            d = d_ref[...]
            for k in range(3):
                o = FFN_HALO - 2 + k
                dw_ref[k:k + 1, :] += jnp.sum(d * extu[o:o + tm, :], axis=0, keepdims=True)

        half(dg_ref, dgn_ref, g_ref, gh_ref, wg_ref, og_ref, dwg_ref)
        half(dv_ref, dvn_ref, v_ref, vh_ref, wv_ref, ov_ref, dwv_ref)

    cur_g, cur_v, halo_g, halo_v, w_g, w_v, _, _ = _ffn_specs(tm, tn, nbj, s, True)
    cur = pl.BlockSpec((tm, tn), lambda j, i: (i, j))
    nxt = pl.BlockSpec((FFN_HALO, tn), lambda j, i: (jnp.minimum((i + 1) * (tm // FFN_HALO), s // FFN_HALO - 1), j))
    acc = pl.BlockSpec((8, tn), lambda j, i: (0, j))
    return pl.pallas_call(
        body, grid=(nbj, s // tm), in_specs=[cur, cur, nxt, nxt, cur_g, cur_v, halo_g, halo_v, w_g, w_v],
        out_specs=[cur, cur, acc, acc],
        out_shape=[SDS((s, D_FF), BF16), SDS((s, D_FF), BF16), SDS((8, D_FF), F32), SDS((8, D_FF), F32)],
        scratch_shapes=[pltpu.VMEM((tm + FFN_HALO, tn), F32), pltpu.VMEM((tm + FFN_HALO, tn), F32)],
        compiler_params=_cp(2), name="ffn_bwd_conv")(d_upg, d_upv, d_upg, d_upv, up0, up0, up0, up0, w8, w8)


def _down_loss(act, w_down, x2, target, tm):
    s = act.shape[0]

    def body(a_ref, w_ref, x_ref, t_ref, dyf_ref, dyb_ref, ls_ref):
        @pl.when(pl.program_id(0) == 0)
        def _():
            ls_ref[...] = jnp.zeros_like(ls_ref)

        y = x_ref[...] + jnp.dot(a_ref[...], w_ref[...], preferred_element_type=F32)
        e = y - t_ref[...]
        ls_ref[...] += jnp.sum(e * e)
        dy = e * (1.0 / D_MODEL)
        dyf_ref[...] = dy
        dyb_ref[...] = dy.astype(BF16)

    row = lambda i: (i, 0)
    return pl.pallas_call(
        body, grid=(s // tm,),
        in_specs=[pl.BlockSpec((tm, D_FF), row), pl.BlockSpec((D_FF, D_MODEL), lambda i: (0, 0)),
                  pl.BlockSpec((tm, D_MODEL), row), pl.BlockSpec((tm, D_MODEL), row)],
        out_specs=[pl.BlockSpec((tm, D_MODEL), row), pl.BlockSpec((tm, D_MODEL), row), pl.BlockSpec((8, 128), lambda i: (0, 0))],
        out_shape=[SDS((s, D_MODEL), F32), SDS((s, D_MODEL), BF16), SDS((8, 128), F32)],
        compiler_params=_cp(1), name="down_loss")(act, w_down, x2, target)


def _adamw(w, g, m, v, tr, name):
    rows, cols = w.shape

    def body(w_ref, g_ref, m_ref, v_ref, d_ref, mo_ref, vo_ref):
        gv = g_ref[...]
        mn = ADAM_B1 * m_ref[...] + (1.0 - ADAM_B1) * gv
        vn = ADAM_B2 * v_ref[...] + (1.0 - ADAM_B2) * (gv * gv)
        m_hat = mn / (1.0 - ADAM_B1 ** ADAM_STEP)
        v_hat = vn / (1.0 - ADAM_B2 ** ADAM_STEP)
        d_ref[...] = -ADAM_LR * (m_hat / (jnp.sqrt(v_hat) + ADAM_EPS) + ADAM_WD * w_ref[...])
        mo_ref[...] = mn
        vo_ref[...] = vn

    blk = pl.BlockSpec((tr, cols), lambda i: (i, 0))
    return pl.pallas_call(body, grid=(rows // tr,), in_specs=[blk] * 4, out_specs=[blk] * 3,
                          out_shape=[SDS((rows, cols), F32)] * 3, compiler_params=_cp(1), name=name)(w, g, m, v)


ANY = pl.BlockSpec(memory_space=pl.ANY)


def _coords():
    return lax.axis_index("x"), lax.axis_index("y"), lax.axis_index("c")


def _other_chips(x, y):
    return [(1 - x, y), (x, 1 - y), (1 - x, 1 - y)]


D2D_CHUNKS = 8
ICI_CHUNKS = 4


def _row_chunks(n_rows, n_chunks, align):
    step = -(-n_rows // (n_chunks * align)) * align
    return [(r, min(step, n_rows - r)) for r in range(0, n_rows, step)]


def _ag_weights(wsh):
    rows, cols = wsh.shape
    half_rows = rows // 2

    def body(w_ref, out_ref, send_sems, recv_sems, local_sem):
        x, y, c = _coords()
        s_me = 2 * x + y
        chips = _other_chips(x, y)
        sibling = (x, y, 1 - c)
        my_base = c * half_rows
        sib_base = (1 - c) * half_rows

        def piece(base, r0, nr):
            return pl.ds(pl.multiple_of(base + r0, 16), nr)

        def copy(k, shard, rows_, to, src=None):
            dst = out_ref.at[shard, rows_]
            return pltpu.make_async_remote_copy(src_ref=dst if src is None else src, dst_ref=dst, send_sem=send_sems.at[k],
                                                recv_sem=recv_sems.at[k], device_id=to, device_id_type=MESH)

        for r0, nr in _row_chunks(rows, D2D_CHUNKS, 16):
            pltpu.make_async_copy(w_ref.at[pl.ds(r0, nr)], out_ref.at[s_me, pl.ds(r0, nr)], local_sem).start()
        for k, (px, py) in enumerate(chips):
            for r0, nr in _row_chunks(half_rows, ICI_CHUNKS, 16):
                copy(k, s_me, piece(my_base, r0, nr), (px, py, c), src=w_ref.at[piece(my_base, r0, nr)]).start()
        for k, (px, py) in enumerate(chips):
            copy(k, 2 * px + py, piece(my_base, 0, half_rows), (px, py, c)).wait_recv()
            for r0, nr in _row_chunks(half_rows, ICI_CHUNKS, 16):
                copy(3 + k, 2 * px + py, piece(my_base, r0, nr), sibling).start()
        for k, (px, py) in enumerate(chips):
            copy(3 + k, 2 * px + py, piece(sib_base, 0, half_rows), sibling).wait_recv()
        for k in range(6):
            copy(k, s_me, piece(my_base, 0, half_rows), sibling).wait_send()
        pltpu.make_async_copy(w_ref, out_ref.at[s_me], local_sem).wait()

    return pl.pallas_call(
        body, in_specs=[ANY], out_specs=ANY, out_shape=SDS((4, rows, cols), wsh.dtype),
        scratch_shapes=[pltpu.SemaphoreType.DMA((6,)), pltpu.SemaphoreType.DMA((6,)), pltpu.SemaphoreType.DMA],
        name="ag_weights")(wsh)


def _rs_swap_halves(gfull):
    n_sh, rows, cols = gfull.shape
    half_rows = rows // 2

    def body(g_ref, recv_ref, send_sem, recv_sem):
        x, y, c = _coords()
        sib_base = (1 - c) * half_rows
        for sh in range(n_sh):
            for r0, nr in _row_chunks(half_rows, D2D_CHUNKS, 8):
                pltpu.make_async_remote_copy(
                    src_ref=g_ref.at[sh, pl.ds(pl.multiple_of(sib_base + r0, 8), nr)], dst_ref=recv_ref.at[sh, pl.ds(r0, nr)],
                    send_sem=send_sem, recv_sem=recv_sem, device_id=(x, y, 1 - c), device_id_type=MESH).start()
        pltpu.make_async_remote_copy(src_ref=recv_ref, dst_ref=recv_ref, send_sem=send_sem, recv_sem=recv_sem,
                                     device_id=(x, y, 1 - c), device_id_type=MESH).wait()

    return pl.pallas_call(
        body, in_specs=[ANY], out_specs=ANY, out_shape=SDS((n_sh, half_rows, cols), gfull.dtype),
        scratch_shapes=[pltpu.SemaphoreType.DMA, pltpu.SemaphoreType.DMA], name="rs_swap_halves")(gfull)


def _rs_add_pair(gfull, recv, core, tr):
    n_sh, rows, cols = gfull.shape
    half_rows = rows // 2
    nblk = half_rows // tr

    def body(c_ref, g_ref, r_ref, o_ref):
        o_ref[...] = g_ref[...] + r_ref[...]

    gs = pltpu.PrefetchScalarGridSpec(
        num_scalar_prefetch=1, grid=(n_sh, nblk),
        in_specs=[pl.BlockSpec((None, tr, cols), lambda sh, i, c_ref: (sh, c_ref[0] * nblk + i, 0)),
                  pl.BlockSpec((None, tr, cols), lambda sh, i, c_ref: (sh, i, 0))],
        out_specs=pl.BlockSpec((None, tr, cols), lambda sh, i, c_ref: (sh, i, 0)))
    return pl.pallas_call(body, grid_spec=gs, out_shape=SDS((n_sh, half_rows, cols), F32),
                          compiler_params=_cp(2), name="rs_add_pair")(core, gfull, recv)


def _rs_to_owner(chipsum):
    n_sh, half_rows, cols = chipsum.shape

    def body(cs_ref, recv_ref, send_sems, recv_sems):
        x, y, c = _coords()
        chips = _other_chips(x, y)
        for k, (px, py) in enumerate(chips):
            for r0, nr in _row_chunks(half_rows, ICI_CHUNKS, 8):
                pltpu.make_async_remote_copy(
                    src_ref=cs_ref.at[2 * px + py, pl.ds(r0, nr)], dst_ref=recv_ref.at[k, pl.ds(r0, nr)],
                    send_sem=send_sems.at[k], recv_sem=recv_sems.at[k], device_id=(px, py, c), device_id_type=MESH).start()
        for k, (px, py) in enumerate(chips):
            pltpu.make_async_remote_copy(src_ref=recv_ref.at[k], dst_ref=recv_ref.at[k], send_sem=send_sems.at[k],
                                         recv_sem=recv_sems.at[k], device_id=(px, py, c), device_id_type=MESH).wait()

    return pl.pallas_call(
        body, in_specs=[ANY], out_specs=ANY, out_shape=SDS((3, half_rows, cols), chipsum.dtype),
        scratch_shapes=[pltpu.SemaphoreType.DMA((3,)), pltpu.SemaphoreType.DMA((3,))], name="rs_to_owner")(chipsum)


def _rs_add_chips(chipsum, recv, shard, tr):
    _, half_rows, cols = chipsum.shape

    def body(s_ref, m_ref, r0_ref, r1_ref, r2_ref, o_ref):
        o_ref[...] = ((m_ref[...] + r0_ref[...]) + r1_ref[...]) + r2_ref[...]

    gs = pltpu.PrefetchScalarGridSpec(
        num_scalar_prefetch=1, grid=(half_rows // tr,),
        in_specs=[pl.BlockSpec((None, tr, cols), lambda i, s_ref: (s_ref[0], i, 0))]
        + [pl.BlockSpec((None, tr, cols), (lambda k: lambda i, s_ref: (k, i, 0))(k)) for k in range(3)],
        out_specs=pl.BlockSpec((tr, cols), lambda i, s_ref: (i, 0)))
    return pl.pallas_call(body, grid_spec=gs, out_shape=SDS((half_rows, cols), F32),
                          compiler_params=_cp(1), name="rs_add_chips")(shard, chipsum, recv, recv, recv)


def _rs_join_halves(red):
    half_rows, cols = red.shape

    def body(r_ref, out_ref, send_sem, recv_sem, local_sem):
        x, y, c = _coords()
        for r0, nr in _row_chunks(half_rows, D2D_CHUNKS, 8):
            pltpu.make_async_copy(r_ref.at[pl.ds(r0, nr)], out_ref.at[c, pl.ds(r0, nr)], local_sem).start()
            pltpu.make_async_remote_copy(src_ref=r_ref.at[pl.ds(r0, nr)], dst_ref=out_ref.at[c, pl.ds(r0, nr)], send_sem=send_sem,
                                         recv_sem=recv_sem, device_id=(x, y, 1 - c), device_id_type=MESH).start()
        pltpu.make_async_remote_copy(src_ref=r_ref, dst_ref=out_ref.at[c], send_sem=send_sem, recv_sem=recv_sem,
                                     device_id=(x, y, 1 - c), device_id_type=MESH).wait()
        pltpu.make_async_copy(r_ref, out_ref.at[c], local_sem).wait()

    return pl.pallas_call(
        body, in_specs=[ANY], out_specs=ANY, out_shape=SDS((2, half_rows, cols), red.dtype),
        scratch_shapes=[pltpu.SemaphoreType.DMA, pltpu.SemaphoreType.DMA, pltpu.SemaphoreType.DMA],
        name="rs_join_halves")(red)


BIG = [("w_in", (1024, 1440), 1), ("w_uq", (256, 768), 1), ("w_ukv", (128, 1024), 1), ("w_out", (1024, 1024), 0),
       ("w_mem_q", (1024, 1024), 0), ("w_mem_kv", (1024, 2048), 1), ("w_mem_o", (1024, 1024), 0),
       ("w_up", (1024, 5632), 1), ("w_down", (2816, 1024), 0)]
SMALL_REP = [("mix_norm_g", 1024), ("b_conv_in", 1024), ("b_conv_dw", 512), ("conv_ln_g", 512), ("conv_ln_b", 512),
             ("q_lat_norm_g", 256), ("kv_lat_norm_g", 128), ("q_norm_g", 96), ("k_norm_g", 96), ("mem_norm_x_g", 1024),
             ("mem_norm_m_g", 1024), ("mem_q_norm_g", 256), ("mem_k_norm_g", 256), ("ffn_norm_g", 1024), ("b_ffn_dw", 5632)]
SMALL_SH = [("w_conv_dw", (31, 512)), ("w_ffn_dw", (3, 5632))]


def _shard_shape(shape, axis):
    return tuple(d // 4 if a == axis else d for a, d in enumerate(shape))


def _pack_rows(parts, rows, cols):
    flat = jnp.concatenate([p.reshape(-1) for p in parts])
    flat = jnp.pad(flat, (0, rows * cols - flat.shape[0]))
    return flat.reshape(rows, cols)


def _pack_big_shards(ws):
    return _pack_rows([ws[n] for n, _, _ in BIG], PACK_ROWS, PACK_COLS)


def _unpack_big_shards(packed):
    out, r = {}, 0
    for n, shape, axis in BIG:
        sh = _shard_shape(shape, axis)
        nr = sh[0] * sh[1] // PACK_COLS
        out[n] = packed[r:r + nr].reshape(sh)
        r += nr
    return out


def _unpack_gathered(g):
    out, r = {}, 0
    for n, shape, axis in BIG:
        sh = _shard_shape(shape, axis)
        nr = sh[0] * sh[1] // PACK_COLS
        parts = g[:, r:r + nr].reshape((4,) + sh)
        out[n] = jnp.concatenate([parts[i] for i in range(4)], axis=axis)
        r += nr
    return out


def _pack_full_grads(gs):
    slabs = []
    for i in range(4):
        parts = []
        for n, shape, axis in BIG:
            sh = _shard_shape(shape, axis)
            parts.append(lax.slice_in_dim(gs[n], i * sh[axis], (i + 1) * sh[axis], axis=axis))
        slabs.append(_pack_rows(parts, PACK_ROWS, PACK_COLS))
    return jnp.stack(slabs)


def _rope_tables(positions):
    inv_freq = ROPE_THETA ** (-jnp.arange(0, ROPE, 2, dtype=F32) / ROPE)
    ang = positions.astype(F32)[:, None] * inv_freq
    cos, sin = jnp.cos(ang), jnp.sin(ang)
    s = positions.shape[0]
    cosf = jnp.concatenate([jnp.ones((s, NOPE), F32), cos, cos, jnp.ones((s, HEAD_PAD - HEAD_DIM), F32)], axis=-1)
    sinf = jnp.concatenate([jnp.zeros((s, NOPE), F32), -sin, sin, jnp.zeros((s, HEAD_PAD - HEAD_DIM), F32)], axis=-1)
    return cosf, sinf


def _pad_heads(w, per_head):
    k = w.shape[0]
    w3 = w.reshape(k, HEADS, per_head)
    return jnp.pad(w3, ((0, 0), (0, 0), (0, HEAD_PAD - per_head))).reshape(k, HEADS * HEAD_PAD)


def _layer_grads(x, mem, positions, target, wf, sp):
    s = x.shape[0]
    tm = _row_tile(s, 512)
    tc = _row_tile(s, 256)
    tb = 512 if s % 512 == 0 and s > 512 else s // 2
    row2 = lambda a: a.reshape(1, -1)

    w_in = wf["w_in"]
    w_in_pad = jnp.concatenate([w_in[:, :1408], jnp.zeros((D_MODEL, NOPE), BF16), w_in[:, 1408:],
                                jnp.zeros((D_MODEL, HEAD_PAD - HEAD_DIM), BF16)], axis=1)
    w_uq_pad = _pad_heads(wf["w_uq"], HEAD_DIM)
    w_ukv = wf["w_ukv"]
    w_out_u = wf["w_out"][:CONV_CH]
    w_out_o = jnp.pad(wf["w_out"][CONV_CH:].reshape(HEADS, NOPE, D_MODEL), ((0, 0), (NOPE, 0), (0, 0))).reshape(HEADS * HEAD_PAD, D_MODEL)
    w_up_g, w_up_v = wf["w_up"][:, :D_FF], wf["w_up"][:, D_FF:]
    gq_pad = jnp.pad(sp["q_norm_g"], (0, HEAD_PAD - HEAD_DIM)).reshape(1, HEAD_PAD)
    gk_pad = jnp.pad(sp["k_norm_g"], (0, HEAD_PAD - HEAD_DIM)).reshape(1, HEAD_PAD)
    w_dw32 = jnp.pad(sp["w_conv_dw"], ((0, 1), (0, 0)))
    w_ffn8 = jnp.pad(sp["w_ffn_dw"], ((0, 5), (0, 0)))
    b_ffn = row2(sp["b_ffn_dw"])
    cosf, sinf = _rope_tables(positions)

    z, h1 = _norm_linear(x, 0, D_MODEL, row2(sp["mix_norm_g"]), w_in_pad, F32, tm, 512, "in_proj")
    u, u0, u1 = _conv_fwd(z, row2(sp["b_conv_in"]), w_dw32, row2(sp["b_conv_dw"]), row2(sp["conv_ln_g"]), row2(sp["conv_ln_b"]), tc)
    q_raw, cqn = _norm_linear(z, 1024 // Q_RANK, Q_RANK, row2(sp["q_lat_norm_g"]), w_uq_pad, F32, tm, 1024, "q_up")
    kv_raw, ckvn = _norm_linear(z, 1280 // KV_RANK, KV_RANK, row2(sp["kv_lat_norm_g"]), w_ukv, F32, tm, 1024, "kv_up")
    qp, kp, vp = _mla_prep(q_raw, kv_raw, z, cosf, sinf, gq_pad, gk_pad, tm)
    o_f, o_b, lse = _attn_fwd(qp, kp, vp, tb)
    (x1,) = _linear([(u, w_out_u), (o_b, w_out_o)], False, x, [F32], tm, 1024, "out_proj")

    qm, hq = _norm_linear(x1, 0, D_MODEL, row2(sp["mem_norm_x_g"]), wf["w_mem_q"], F32, tm, 1024, "memq_proj")
    kvm, hm = _norm_linear(mem, 0, D_MODEL, row2(sp["mem_norm_m_g"]), wf["w_mem_kv"], F32, MEM_LEN, 1024, "memkv_proj")
    gmq, gmk = row2(sp["mem_q_norm_g"]), row2(sp["mem_k_norm_g"])
    o_m = _memattn_fwd(qm, kvm, gmq, gmk, tm)
    (x2,) = _linear([(o_m, wf["w_mem_o"])], False, x1, [F32], tm, 1024, "memo_proj")

    up0, h3 = _norm_linear(x2, 0, D_MODEL, row2(sp["ffn_norm_g"]), wf["w_up"], F32, tm, 512, "up_proj")
    act = _ffn_fwd(up0, w_ffn8, b_ffn, tc, D_FF // 2)
    dy_f, dy_b, lsum = _down_loss(act, wf["w_down"], x2, target, tm)

    g = {}
    (d_act,) = _linear([(dy_b, wf["w_down"])], True, None, [F32], tm, D_FF // 2, "down_bwd")
    g["w_down"] = _dw(act, dy_b, "dw_down")
    d_upg, d_upv, dbg, dbv = _ffn_bwd_act(d_act, up0, w_ffn8, b_ffn, tc, D_FF // 2)
    d_up0g, d_up0v, dwg, dwv = _ffn_bwd_conv(d_upg, d_upv, up0, w_ffn8, tc, D_FF // 2)
    g["b_ffn_dw"] = jnp.concatenate([dbg, dbv], axis=1).reshape(-1)
    g["w_ffn_dw"] = jnp.concatenate([dwg[:3], dwv[:3]], axis=1)
    g["w_up"] = jnp.concatenate([_dw(h3, d_up0g, "dw_up_g"), _dw(h3, d_up0v, "dw_up_v")], axis=1)
    d_x2f, d_x2b, dg = _linear_normbwd([(d_up0g, w_up_g), (d_up0v, w_up_v)], x2, 0, row2(sp["ffn_norm_g"]), dy_f,
                                       [F32, BF16], tc, "up_bwd")
    g["ffn_norm_g"] = dg.reshape(-1)

    (d_om,) = _linear([(d_x2b, wf["w_mem_o"])], True, None, [BF16], tm, 1024, "memo_bwd")
    g["w_mem_o"] = _dw(o_m, d_x2b, "dw_mem_o")
    d_qm, d_km, d_vm, dgq, dgk = _memattn_bwd(qm, kvm, d_om, gmq, gmk, tm)
    g["mem_q_norm_g"], g["mem_k_norm_g"] = dgq.reshape(-1), dgk.reshape(-1)
    d_kvm = jnp.concatenate([d_km, d_vm], axis=1)
    g["w_mem_q"] = _dw(hq, d_qm, "dw_mem_q")
    g["w_mem_kv"] = _dw(hm, d_kvm, "dw_mem_kv")
    d_x1f, d_x1b, dg = _linear_normbwd([(d_qm, wf["w_mem_q"])], x1, 0, row2(sp["mem_norm_x_g"]), d_x2f, [F32, BF16], tm, "memq_bwd")
    g["mem_norm_x_g"] = dg.reshape(-1)
    _, dg = _linear_normbwd([(d_kvm, wf["w_mem_kv"])], mem, 0, row2(sp["mem_norm_m_g"]), None, [BF16], MEM_LEN, "memkv_bwd")
    g["mem_norm_m_g"] = dg.reshape(-1)

    (d_u,) = _linear([(d_x1b, w_out_u)], True, None, [F32], tm, CONV_CH, "out_bwd_u")
    d_of, d_ob = _linear([(d_x1b, w_out_o)], True, None, [F32, BF16], tm, 1024, "out_bwd_o")
    dw_out_u = _dw(u, d_x1b, "dw_out_u")
    dw_out_o = _dw(o_b, d_x1b, "dw_out_o")
    g["w_out"] = jnp.concatenate([dw_out_u, dw_out_o.reshape(HEADS, HEAD_PAD, D_MODEL)[:, NOPE:].reshape(HEADS * NOPE, D_MODEL)], axis=0)
    delta = _attn_delta(d_of, o_f, tm)
    dqp, dkp, dvp = _attn_bwd(qp, kp, vp, d_ob, lse, delta, tb)
    d_qraw, d_kvraw, d_kr, dgq, dgk = _mla_prep_bwd(dqp, dkp, dvp, q_raw, kv_raw, z, cosf, sinf, gq_pad, gk_pad, tm)
    g["q_norm_g"], g["k_norm_g"] = dgq.reshape(-1)[:HEAD_DIM], dgk.reshape(-1)[:HEAD_DIM]
    g["w_uq"] = _dw(cqn, d_qraw, "dw_uq").reshape(Q_RANK, HEADS, HEAD_PAD)[:, :, :HEAD_DIM].reshape(Q_RANK, HEADS * HEAD_DIM)
    g["w_ukv"] = _dw(ckvn, d_kvraw, "dw_ukv")
    d_cq, dg = _linear_normbwd([(d_qraw, w_uq_pad)], z, 1024 // Q_RANK, row2(sp["q_lat_norm_g"]), None, [BF16], tm, "q_up_bwd")
    g["q_lat_norm_g"] = dg.reshape(-1)
    d_ckv, dg = _linear_normbwd([(d_kvraw, w_ukv)], z, 1280 // KV_RANK, row2(sp["kv_lat_norm_g"]), None, [BF16], tm, "kv_up_bwd")
    g["kv_lat_norm_g"] = dg.reshape(-1)
    d_u1, dlg, dlb, dbdw = _conv_bwd_ln(d_u, u1, row2(sp["conv_ln_g"]), row2(sp["conv_ln_b"]), tc)
    g["conv_ln_g"], g["conv_ln_b"], g["b_conv_dw"] = dlg.reshape(-1), dlb.reshape(-1), dbdw.reshape(-1)
    d_conv, dw_dw, dbin = _conv_bwd_dw(d_u1, u0, z, row2(sp["b_conv_in"]), w_dw32, tc)
    g["w_conv_dw"], g["b_conv_in"] = dw_dw[:CONV_WIDTH], dbin.reshape(-1)
    pieces = [(d_conv, w_in_pad[:, :1024]), (d_cq, w_in_pad[:, 1024:1280]), (d_ckv, w_in_pad[:, 1280:1408]), (d_kr, w_in_pad[:, 1408:])]
    dw_in = [_dw(h1, d, "dw_in_%d" % k) for k, (d, _) in enumerate(pieces)]
    g["w_in"] = jnp.concatenate([dw_in[0], dw_in[1], dw_in[2], dw_in[3][:, NOPE:HEAD_DIM]], axis=1)
    grad_x, dg = _linear_normbwd(pieces, x, 0, row2(sp["mix_norm_g"]), d_x1f, [F32], tm, "in_bwd")
    g["mix_norm_g"] = dg.reshape(-1)
    return lsum[0, 0], grad_x, g


def kernel(x, mem, positions, mix_norm_g, w_in, b_conv_in, w_conv_dw, b_conv_dw, conv_ln_g, conv_ln_b, q_lat_norm_g, w_uq, kv_lat_norm_g, w_ukv, q_norm_g, k_norm_g, w_out, mem_norm_x_g, mem_norm_m_g, w_mem_q, w_mem_kv, mem_q_norm_g, mem_k_norm_g, w_mem_o, ffn_norm_g, w_up, w_ffn_dw, b_ffn_dw, w_down, loss_target, m_mix_norm_g, m_w_in, m_b_conv_in, m_w_conv_dw, m_b_conv_dw, m_conv_ln_g, m_conv_ln_b, m_q_lat_norm_g, m_w_uq, m_kv_lat_norm_g, m_w_ukv, m_q_norm_g, m_k_norm_g, m_w_out, m_mem_norm_x_g, m_mem_norm_m_g, m_w_mem_q, m_w_mem_kv, m_mem_q_norm_g, m_mem_k_norm_g, m_w_mem_o, m_ffn_norm_g, m_w_up, m_w_ffn_dw, m_b_ffn_dw, m_w_down, v_mix_norm_g, v_w_in, v_b_conv_in, v_w_conv_dw, v_b_conv_dw, v_conv_ln_g, v_conv_ln_b, v_q_lat_norm_g, v_w_uq, v_kv_lat_norm_g, v_w_ukv, v_q_norm_g, v_k_norm_g, v_w_out, v_mem_norm_x_g, v_mem_norm_m_g, v_w_mem_q, v_w_mem_kv, v_mem_q_norm_g, v_mem_k_norm_g, v_w_mem_o, v_ffn_norm_g, v_w_up, v_w_ffn_dw, v_b_ffn_dw, v_w_down):
    names = ["mix_norm_g", "w_in", "b_conv_in", "w_conv_dw", "b_conv_dw", "conv_ln_g", "conv_ln_b", "q_lat_norm_g", "w_uq",
             "kv_lat_norm_g", "w_ukv", "q_norm_g", "k_norm_g", "w_out", "mem_norm_x_g", "mem_norm_m_g", "w_mem_q", "w_mem_kv",
             "mem_q_norm_g", "mem_k_norm_g", "w_mem_o", "ffn_norm_g", "w_up", "w_ffn_dw", "b_ffn_dw", "w_down"]
    loc = locals()
    w = {n: loc[n] for n in names}
    m = {n: loc["m_" + n] for n in names}
    v = {n: loc["v_" + n] for n in names}
    shard_idx = 2 * lax.axis_index("x") + lax.axis_index("y")

    w_packed = _pack_big_shards({n: w[n][0] for n, _, _ in BIG})
    gathered = _ag_weights(w_packed.astype(BF16))
    wf = _unpack_gathered(gathered)

    small_sh_full = {}
    gather_in = []
    for n, (r, c) in SMALL_SH:
        csh = c // 4
        slab = lax.dynamic_update_slice(jnp.zeros((r, c), F32), w[n][0], (0, shard_idx * csh))
        gather_in.append(slab.reshape(-1))
    gather_rows = 256
    gathered_small = _allreduce_small_named(_pack_rows(gather_in, gather_rows, SMALL_COLS), "gather_small") * 0.5
    off = 0
    for n, (r, c) in SMALL_SH:
        small_sh_full[n] = gathered_small.reshape(-1)[off:off + r * c].reshape(r, c)
        off += r * c
    sp = {n: w[n][0] for n, _ in SMALL_REP}
    sp.update(small_sh_full)

    lsum, grad_x, g = _layer_grads(x[0], mem[0], positions[0], loss_target[0], wf, sp)

    small_parts = [jnp.full((SMALL_COLS,), lsum, F32)] + [g[n] for n, _ in SMALL_REP] + [g[n] for n, _ in SMALL_SH]
    small_rows = 368
    small_sum = _allreduce_small_named(_pack_rows(small_parts, small_rows, SMALL_COLS), "allreduce_small").reshape(-1)
    loss = small_sum[0] * (0.5 / D_MODEL)
    gs = {}
    off = SMALL_COLS
    for n, sz in SMALL_REP:
        gs[n] = small_sum[off:off + sz].reshape(w[n].shape)
        off += sz
    for n, (r, c) in SMALL_SH:
        full = small_sum[off:off + r * c].reshape(r, c)
        gs[n] = lax.dynamic_slice(full, (0, shard_idx * (c // 4)), (r, c // 4)).reshape(w[n].shape)
        off += r * c

    gfull = _pack_full_grads(g)
    core_idx = lax.axis_index("c").astype(jnp.int32).reshape(1)
    chipsum = _rs_add_pair(gfull, _rs_swap_halves(gfull), core_idx, 240)
    red = _rs_add_chips(chipsum, _rs_to_owner(chipsum), shard_idx.astype(jnp.int32).reshape(1), 240)
    g_packed = _rs_join_halves(red).reshape(PACK_ROWS, PACK_COLS)
    gs.update({n: a.reshape(w[n].shape) for n, a in _unpack_big_shards(g_packed).items()})

    m_packed = _pack_big_shards({n: m[n][0] for n, _, _ in BIG})
    v_packed = _pack_big_shards({n: v[n][0] for n, _, _ in BIG})
    d_p, m_p, v_p = _adamw(w_packed, g_packed, m_packed, v_packed, 256, "adamw_big")
    delta, new_m, new_v = {}, {}, {}
    for dst, src in ((delta, d_p), (new_m, m_p), (new_v, v_p)):
        dst.update({n: a.reshape(w[n].shape) for n, a in _unpack_big_shards(src).items()})
    small_names = [n for n, _ in SMALL_REP] + [n for n, _ in SMALL_SH]
    adam_rows = 176
    pk = lambda d: _pack_rows([d[n].reshape(-1) for n in small_names], adam_rows, SMALL_COLS)
    d_s, m_s, v_s = _adamw(pk(w), pk(gs), pk(m), pk(v), adam_rows, "adamw_small")
    for dst, src in ((delta, d_s), (new_m, m_s), (new_v, v_s)):
        off = 0
        flat = src.reshape(-1)
        for n in small_names:
            sz = int(np.prod(w[n].shape))
            dst[n] = flat[off:off + sz].reshape(w[n].shape)
            off += sz

    return (loss, grad_x[None], *[gs[n] for n in names], *[delta[n] for n in names], *[new_m[n] for n in names],
            *[new_v[n] for n in names])


def _allreduce_small_named(v, name):
    rows, cols = v.shape

    def body(v_ref, out_ref, buf, send_sems, recv_sems):
        x, y, c = _coords()
        me = 4 * x + 2 * y + c
        buf[me] = v_ref[...]
        cps = []
        for r in range(1, 8):
            dx, dy, dc = (r >> 2) & 1, (r >> 1) & 1, r & 1
            to = (x + dx - 2 * x * dx, y + dy - 2 * y * dy, c + dc - 2 * c * dc)
            cp = pltpu.make_async_remote_copy(src_ref=v_ref, dst_ref=buf.at[me], send_sem=send_sems.at[r - 1],
                                              recv_sem=recv_sems.at[r - 1], device_id=to, device_id_type=MESH)
            cp.start()
            cps.append(cp)
        for cp in cps:
            cp.wait()
        acc = buf[0]
        for d in range(1, 8):
            acc = acc + buf[d]
        out_ref[...] = acc

    vm = pl.BlockSpec(memory_space=pltpu.VMEM)
    return pl.pallas_call(
        body, in_specs=[vm], out_specs=vm, out_shape=SDS((rows, cols), F32),
        scratch_shapes=[pltpu.VMEM((8, rows, cols), F32), pltpu.SemaphoreType.DMA((7,)), pltpu.SemaphoreType.DMA((7,))],
        name=name)(v)
```

```python
import math

import numpy as np
import jax
import jax.numpy as jnp
from jax import lax
from jax.experimental import pallas as pl
from jax.experimental.pallas import tpu as pltpu

F32 = jnp.float32
BF16 = jnp.bfloat16
SDS = jax.ShapeDtypeStruct
MESH = pl.DeviceIdType.MESH

D_MODEL = 1024
EPS = 1e-6
CONV_CH = 512
CONV_WIDTH = 31
CONV_HALO = 32
HEADS = 8
NOPE = 64
ROPE = 32
HEAD_DIM = NOPE + ROPE
HEAD_PAD = 128
Q_RANK = 256
KV_RANK = 128
CHUNK = 64
ROPE_THETA = 10000.0
IN_COLS_PAD = 1536
MEM_HEADS = 4
MEM_HEAD_DIM = 256
MEM_LEN = 256
D_FF = 2816
FFN_HALO = 8
ATT_SCALE = 1.0 / math.sqrt(HEAD_DIM)
LOG2E = math.log2(math.e)
LN2 = math.log(2.0)

ADAM_LR = 0.001
ADAM_B1 = 0.9
ADAM_B2 = 0.999
ADAM_EPS = 1e-08
ADAM_WD = 0.01
ADAM_STEP = 10

VMEM_LIMIT_V7X = 56 * 1024 * 1024
PACK_COLS = 1024
PACK_ROWS = 3840
SMALL_COLS = 128


def _cp(n_axes):
    return pltpu.CompilerParams(dimension_semantics=("arbitrary",) * n_axes, vmem_limit_bytes=VMEM_LIMIT_V7X)


def _row_tile(s, want):
    return want if s % want == 0 else s


def _norm_linear(x, xcol, kdim, g, w, out_dtype, tm, tn, name):
    s = x.shape[0]
    n = w.shape[1]

    def body(x_ref, g_ref, w_ref, y_ref, hn_ref):
        @pl.when(pl.program_id(1) == 0)
        def _():
            xv = x_ref[...]
            r = lax.rsqrt(jnp.mean(xv * xv, axis=-1, keepdims=True) + EPS)
            hn_ref[...] = ((xv * r) * g_ref[...]).astype(BF16)

        y_ref[...] = jnp.dot(hn_ref[...], w_ref[...], preferred_element_type=F32).astype(y_ref.dtype)

    return pl.pallas_call(
        body, grid=(s // tm, n // tn),
        in_specs=[pl.BlockSpec((tm, kdim), lambda i, j: (i, xcol)), pl.BlockSpec((1, kdim), lambda i, j: (0, 0)),
                  pl.BlockSpec((kdim, tn), lambda i, j: (0, j))],
        out_specs=[pl.BlockSpec((tm, tn), lambda i, j: (i, j)), pl.BlockSpec((tm, kdim), lambda i, j: (i, 0))],
        out_shape=[SDS((s, n), out_dtype), SDS((s, kdim), BF16)],
        compiler_params=_cp(2), name=name)(x, g, w)


def _linear(pairs, nt, residual, out_dtypes, tm, tn, name):
    s = pairs[0][0].shape[0]
    n = pairs[0][1].shape[0] if nt else pairs[0][1].shape[1]
    n_pairs = len(pairs)
    has_res = residual is not None

    def body(*refs):
        a_refs = refs[:n_pairs]
        w_refs = refs[n_pairs:2 * n_pairs]
        res_ref = refs[2 * n_pairs] if has_res else None
        outs = refs[2 * n_pairs + int(has_res):]
        acc = None
        for a_ref, w_ref in zip(a_refs, w_refs):
            a = a_ref[...].astype(BF16)
            if nt:
                d = lax.dot_general(a, w_ref[...], (((1,), (1,)), ((), ())), preferred_element_type=F32)
            else:
                d = jnp.dot(a, w_ref[...], preferred_element_type=F32)
            acc = d if acc is None else acc + d
        if has_res:
            acc = res_ref[...] + acc
        for o in outs:
            o[...] = acc.astype(o.dtype)

    in_specs = [pl.BlockSpec((tm, a.shape[1]), lambda i, j: (i, 0)) for a, _ in pairs]
    if nt:
        in_specs += [pl.BlockSpec((tn, w.shape[1]), lambda i, j: (j, 0)) for _, w in pairs]
    else:
        in_specs += [pl.BlockSpec((w.shape[0], tn), lambda i, j: (0, j)) for _, w in pairs]
    args = [a for a, _ in pairs] + [w for _, w in pairs]
    if has_res:
        in_specs.append(pl.BlockSpec((tm, tn), lambda i, j: (i, j)))
        args.append(residual)
    outs = pl.pallas_call(
        body, grid=(s // tm, n // tn), in_specs=in_specs,
        out_specs=[pl.BlockSpec((tm, tn), lambda i, j: (i, j)) for _ in out_dtypes],
        out_shape=[SDS((s, n), dt) for dt in out_dtypes],
        compiler_params=_cp(2), name=name)(*args)
    return outs


def _linear_normbwd(pairs, x, xcol, g, d_res, out_dtypes, tm, name):
    s = pairs[0][0].shape[0]
    dn = pairs[0][1].shape[0]
    n_pairs = len(pairs)
    has_res = d_res is not None

    def body(*refs):
        a_refs = refs[:n_pairs]
        w_refs = refs[n_pairs:2 * n_pairs]
        x_ref, g_ref = refs[2 * n_pairs], refs[2 * n_pairs + 1]
        k = 2 * n_pairs + 2
        res_ref = refs[k] if has_res else None
        k += int(has_res)
        outs = refs[k:-1]
        dg_ref = refs[-1]
        dh = None
        for a_ref, w_ref in zip(a_refs, w_refs):
            d = lax.dot_general(a_ref[...].astype(BF16), w_ref[...], (((1,), (1,)), ((), ())), preferred_element_type=F32)
            dh = d if dh is None else dh + d
        xv = x_ref[...]
        r = lax.rsqrt(jnp.mean(xv * xv, axis=-1, keepdims=True) + EPS)
        y = xv * r

        @pl.when(pl.program_id(0) == 0)
        def _():
            dg_ref[...] = jnp.zeros_like(dg_ref)

        dg_ref[...] += jnp.sum(dh * y, axis=0, keepdims=True)
        dy = dh * g_ref[...]
        dx = r * (dy - y * jnp.mean(dy * y, axis=-1, keepdims=True))
        if has_res:
            dx = res_ref[...] + dx
        for o in outs:
            o[...] = dx.astype(o.dtype)

    in_specs = [pl.BlockSpec((tm, a.shape[1]), lambda i: (i, 0)) for a, _ in pairs]
    in_specs += [pl.BlockSpec((dn, w.shape[1]), lambda i: (0, 0)) for _, w in pairs]
    in_specs += [pl.BlockSpec((tm, dn), lambda i: (i, xcol)), pl.BlockSpec((1, dn), lambda i: (0, 0))]
    args = [a for a, _ in pairs] + [w for _, w in pairs] + [x, g]
    if has_res:
        in_specs.append(pl.BlockSpec((tm, dn), lambda i: (i, 0)))
        args.append(d_res)
    outs = pl.pallas_call(
        body, grid=(s // tm,), in_specs=in_specs,
        out_specs=[pl.BlockSpec((tm, dn), lambda i: (i, 0)) for _ in out_dtypes] + [pl.BlockSpec((1, dn), lambda i: (0, 0))],
        out_shape=[SDS((s, dn), dt) for dt in out_dtypes] + [SDS((1, dn), F32)],
        compiler_params=_cp(1), name=name)(*args)
    return outs


def _dw_matmul(a, b, tk, tn, ts, name):
    s, ka = a.shape
    n = b.shape[1]

    def body(a_ref, b_ref, o_ref):
        @pl.when(pl.program_id(2) == 0)
        def _():
            o_ref[...] = jnp.zeros_like(o_ref)

        o_ref[...] += lax.dot_general(a_ref[...].astype(BF16), b_ref[...].astype(BF16), (((0,), (0,)), ((), ())),
                                      preferred_element_type=F32)

    return pl.pallas_call(
        body, grid=(ka // tk, n // tn, s // ts),
        in_specs=[pl.BlockSpec((ts, tk), lambda k, j, t: (t, k)), pl.BlockSpec((ts, tn), lambda k, j, t: (t, j))],
        out_specs=pl.BlockSpec((tk, tn), lambda k, j, t: (k, j)),
        out_shape=SDS((ka, n), F32), compiler_params=_cp(3), name=name)(a, b)


def _dw(a, b, name):
    s, ka = a.shape
    n = b.shape[1]
    tk = ka if ka <= 1024 else ka // 2
    tn = n if n <= 1024 else (n // 2 if n == D_FF else 512)
    return _dw_matmul(a, b, tk, tn, _row_tile(s, 512), name)


def _prev_halo(tm, halo):
    return lambda i: (jnp.maximum(i * (tm // halo) - 1, 0), 0)


def _next_halo(tm, halo, s):
    return lambda i: (jnp.minimum((i + 1) * (tm // halo), s // halo - 1), 0)


def _conv_fwd(z, b_in, w32, b_dw, ln_g, ln_b, tm):
    s = z.shape[0]
    c = CONV_CH

    def body(z_ref, zh_ref, bin_ref, w_ref, bdw_ref, lg_ref, lb_ref, u_ref, u0_ref, u1_ref, ext):
        i = pl.program_id(0)

        def glu(zz):
            zz = zz + bin_ref[...]
            return zz[:, :c] * jax.nn.sigmoid(zz[:, c:])

        u0 = glu(z_ref[...])
        u0_ref[...] = u0
        ext[0:CONV_HALO, :] = jnp.where(i > 0, glu(zh_ref[...]), 0.0)
        ext[CONV_HALO:, :] = u0
        off = CONV_HALO - (CONV_WIDTH - 1)
        for r in range(tm // 64):
            for cb in range(c // 128):
                cs = slice(cb * 128, (cb + 1) * 128)
                acc = jnp.zeros((64, 128), F32)
                for k in range(CONV_WIDTH):
                    acc = acc + ext[r * 64 + off + k: r * 64 + off + k + 64, cs] * w_ref[k:k + 1, cs]
                u1_ref[r * 64:(r + 1) * 64, cs] = acc + bdw_ref[:, cs]
        u1 = u1_ref[...]
        mu = jnp.mean(u1, axis=-1, keepdims=True)
        xc = u1 - mu
        y = xc * lax.rsqrt(jnp.mean(xc * xc, axis=-1, keepdims=True) + EPS)
        y = y * lg_ref[...] + lb_ref[...]
        u_ref[...] = (y * jax.nn.sigmoid(y)).astype(BF16)

    row = lambda i: (i, 0)
    fix = lambda i: (0, 0)
    return pl.pallas_call(
        body, grid=(s // tm,),
        in_specs=[pl.BlockSpec((tm, 2 * c), row), pl.BlockSpec((CONV_HALO, 2 * c), _prev_halo(tm, CONV_HALO)),
                  pl.BlockSpec((1, 2 * c), fix), pl.BlockSpec((32, c), fix), pl.BlockSpec((1, c), fix),
                  pl.BlockSpec((1, c), fix), pl.BlockSpec((1, c), fix)],
        out_specs=[pl.BlockSpec((tm, c), row)] * 3,
        out_shape=[SDS((s, c), BF16), SDS((s, c), F32), SDS((s, c), F32)],
        scratch_shapes=[pltpu.VMEM((tm + CONV_HALO, c), F32)],
        compiler_params=_cp(1), name="conv_fwd")(z, z, b_in, w32, b_dw, ln_g, ln_b)


def _conv_bwd_ln(d_u, u1, ln_g, ln_b, tm):
    s = d_u.shape[0]
    c = CONV_CH

    def body(du_ref, u1_ref, lg_ref, lb_ref, du1_ref, dlg_ref, dlb_ref, dbdw_ref):
        @pl.when(pl.program_id(0) == 0)
        def _():
            dlg_ref[...] = jnp.zeros_like(dlg_ref)
            dlb_ref[...] = jnp.zeros_like(dlb_ref)
            dbdw_ref[...] = jnp.zeros_like(dbdw_ref)

        u1 = u1_ref[...]
        mu = jnp.mean(u1, axis=-1, keepdims=True)
        xc = u1 - mu
        rs = lax.rsqrt(jnp.mean(xc * xc, axis=-1, keepdims=True) + EPS)
        xh = xc * rs
        y = xh * lg_ref[...] + lb_ref[...]
        sg = jax.nn.sigmoid(y)
        dy = du_ref[...] * (sg * (1.0 + y * (1.0 - sg)))
        dlg_ref[...] += jnp.sum(dy * xh, axis=0, keepdims=True)
        dlb_ref[...] += jnp.sum(dy, axis=0, keepdims=True)
        dxh = dy * lg_ref[...]
        du1 = rs * (dxh - jnp.mean(dxh, axis=-1, keepdims=True) - xh * jnp.mean(dxh * xh, axis=-1, keepdims=True))
        dbdw_ref[...] += jnp.sum(du1, axis=0, keepdims=True)
        du1_ref[...] = du1

    row = lambda i: (i, 0)
    fix = lambda i: (0, 0)
    return pl.pallas_call(
        body, grid=(s // tm,),
        in_specs=[pl.BlockSpec((tm, c), row), pl.BlockSpec((tm, c), row), pl.BlockSpec((1, c), fix), pl.BlockSpec((1, c), fix)],
        out_specs=[pl.BlockSpec((tm, c), row)] + [pl.BlockSpec((1, c), fix)] * 3,
        out_shape=[SDS((s, c), F32)] + [SDS((1, c), F32)] * 3,
        compiler_params=_cp(1), name="conv_bwd_ln")(d_u, u1, ln_g, ln_b)


def _conv_bwd_dw(d_u1, u0, z, b_in, w32, tm):
    s = d_u1.shape[0]
    c = CONV_CH

    def body(d_ref, dn_ref, u0_ref, u0p_ref, z_ref, bin_ref, w_ref, dz_ref, dw_ref, dbin_ref, extd, extu, du0):
        i = pl.program_id(0)
        last = pl.num_programs(0) - 1

        @pl.when(i == 0)
        def _():
            dw_ref[...] = jnp.zeros_like(dw_ref)
            dbin_ref[...] = jnp.zeros_like(dbin_ref)

        extd[0:tm, :] = d_ref[...]
        extd[tm:, :] = jnp.where(i < last, dn_ref[...], 0.0)
        extu[0:CONV_HALO, :] = jnp.where(i > 0, u0p_ref[...], 0.0)
        extu[CONV_HALO:, :] = u0_ref[...]
        off = CONV_HALO - (CONV_WIDTH - 1)
        for r in range(tm // 64):
            for cb in range(c // 128):
                cs = slice(cb * 128, (cb + 1) * 128)
                acc = jnp.zeros((64, 128), F32)
                for k in range(CONV_WIDTH):
                    o = r * 64 + (CONV_WIDTH - 1) - k
                    acc = acc + extd[o:o + 64, cs] * w_ref[k:k + 1, cs]
                du0[r * 64:(r + 1) * 64, cs] = acc
        for cb in range(c // 128):
            cs = slice(cb * 128, (cb + 1) * 128)
            for k in range(CONV_WIDTH):
                part = jnp.zeros((8, 128), F32)
                for r in range(tm // 64):
                    p = d_ref[r * 64:(r + 1) * 64, cs] * extu[r * 64 + off + k: r * 64 + off + k + 64, cs]
                    for q in range(8):
                        part = part + p[q * 8:(q + 1) * 8, :]
                dw_ref[k:k + 1, cs] += jnp.sum(part, axis=0, keepdims=True)
        zz = z_ref[...] + bin_ref[...]
        a = zz[:, :c]
        sg = jax.nn.sigmoid(zz[:, c:])
        d0 = du0[...]
        da = d0 * sg
        dgt = d0 * a * (sg * (1.0 - sg))
        dbin_ref[:, :c] += jnp.sum(da, axis=0, keepdims=True)
        dbin_ref[:, c:] += jnp.sum(dgt, axis=0, keepdims=True)
        dz_ref[:, :c] = da.astype(BF16)
        dz_ref[:, c:] = dgt.astype(BF16)

    row = lambda i: (i, 0)
    fix = lambda i: (0, 0)
    return pl.pallas_call(
        body, grid=(s // tm,),
        in_specs=[pl.BlockSpec((tm, c), row), pl.BlockSpec((CONV_HALO, c), _next_halo(tm, CONV_HALO, s)),
                  pl.BlockSpec((tm, c), row), pl.BlockSpec((CONV_HALO, c), _prev_halo(tm, CONV_HALO)),
                  pl.BlockSpec((tm, 2 * c), row), pl.BlockSpec((1, 2 * c), fix), pl.BlockSpec((32, c), fix)],
        out_specs=[pl.BlockSpec((tm, 2 * c), row), pl.BlockSpec((32, c), fix), pl.BlockSpec((1, 2 * c), fix)],
        out_shape=[SDS((s, 2 * c), BF16), SDS((32, c), F32), SDS((1, 2 * c), F32)],
        scratch_shapes=[pltpu.VMEM((tm + CONV_HALO, c), F32), pltpu.VMEM((tm + CONV_HALO, c), F32), pltpu.VMEM((tm, c), F32)],
        compiler_params=_cp(1), name="conv_bwd_dw")(d_u1, d_u1, u0, u0, z, b_in, w32)


def _partner(v, lane):
    up = pltpu.roll(v, HEAD_PAD - ROPE // 2, 1)
    dn = pltpu.roll(v, ROPE // 2, 1)
    lo = (lane >= NOPE) & (lane < NOPE + ROPE // 2)
    hi = (lane >= NOPE + ROPE // 2) & (lane < HEAD_DIM)
    return jnp.where(lo, up, jnp.where(hi, dn, 0.0))


def _mla_prep(q_raw, kv_raw, z, cosf, sinf, gq, gk, tm):
    s = q_raw.shape[0]

    def body(q_ref, kv_ref, kr_ref, c_ref, s_ref, gq_ref, gk_ref, qo_ref, ko_ref, vo_ref):
        lane = lax.broadcasted_iota(jnp.int32, (tm, HEAD_PAD), 1)
        cf = c_ref[...]
        sf = s_ref[...]

        def norm_rope(t, g_ref):
            r = lax.rsqrt(jnp.sum(t * t, axis=-1, keepdims=True) * (1.0 / HEAD_DIM) + EPS)
            tn = (t * r) * g_ref[...]
            return tn * cf + _partner(tn, lane) * sf

        q = q_ref[...]
        qo_ref[...] = (norm_rope(q, gq_ref) * (ATT_SCALE * LOG2E)).astype(BF16)
        kv = kv_ref[...]
        kpre = jnp.where(lane < NOPE, kv, 0.0) + kr_ref[...]
        ko_ref[...] = norm_rope(kpre, gk_ref).astype(BF16)
        vo_ref[...] = jnp.where(lane >= NOPE, kv, 0.0).astype(BF16)

    hb = lambda i, h: (i, h)
    return pl.pallas_call(
        body, grid=(s // tm, HEADS),
        in_specs=[pl.BlockSpec((tm, HEAD_PAD), hb), pl.BlockSpec((tm, HEAD_PAD), hb),
                  pl.BlockSpec((tm, HEAD_PAD), lambda i, h: (i, IN_COLS_PAD // HEAD_PAD - 1)),
                  pl.BlockSpec((tm, HEAD_PAD), lambda i, h: (i, 0)), pl.BlockSpec((tm, HEAD_PAD), lambda i, h: (i, 0)),
                  pl.BlockSpec((1, HEAD_PAD), lambda i, h: (0, 0)), pl.BlockSpec((1, HEAD_PAD), lambda i, h: (0, 0))],
        out_specs=[pl.BlockSpec((tm, HEAD_PAD), hb)] * 3,
        out_shape=[SDS((s, HEADS * HEAD_PAD), BF16)] * 3,
        compiler_params=_cp(2), name="mla_prep")(q_raw, kv_raw, z, cosf, sinf, gq, gk)


def _mla_prep_bwd(dqp, dkp, dvp, q_raw, kv_raw, z, cosf, sinf, gq, gk, tm):
    s = q_raw.shape[0]

    def body(dq_ref, dk_ref, dv_ref, q_ref, kv_ref, kr_ref, c_ref, s_ref, gq_ref, gk_ref,
             dqo_ref, dkvo_ref, dkr_ref, dgq_ref, dgk_ref):
        i = pl.program_id(0)
        h = pl.program_id(1)
        lane = lax.broadcasted_iota(jnp.int32, (tm, HEAD_PAD), 1)
        cf = c_ref[...]
        sf = s_ref[...]

        @pl.when((i == 0) & (h == 0))
        def _():
            dgq_ref[...] = jnp.zeros_like(dgq_ref)
            dgk_ref[...] = jnp.zeros_like(dgk_ref)

        def norm_rope_bwd(t, d_out, g_ref, dg_ref):
            r = lax.rsqrt(jnp.sum(t * t, axis=-1, keepdims=True) * (1.0 / HEAD_DIM) + EPS)
            th = t * r
            dn = d_out * cf + _partner(d_out * sf, lane)
            dg_ref[...] += jnp.sum(dn * th, axis=0, keepdims=True)
            dh = dn * g_ref[...]
            return r * (dh - th * (jnp.sum(dh * th, axis=-1, keepdims=True) * (1.0 / HEAD_DIM)))

        dq = norm_rope_bwd(q_ref[...], dq_ref[...] * ATT_SCALE, gq_ref, dgq_ref)
        dqo_ref[...] = dq.astype(BF16)
        kv = kv_ref[...]
        kpre = jnp.where(lane < NOPE, kv, 0.0) + kr_ref[...]
        dkpre = norm_rope_bwd(kpre, dk_ref[...] * LN2, gk_ref, dgk_ref)
        dkvo_ref[...] = jnp.where(lane < NOPE, dkpre, dv_ref[...]).astype(BF16)
        dkr = jnp.where((lane >= NOPE) & (lane < HEAD_DIM), dkpre, 0.0)

        @pl.when(h == 0)
        def _():
            dkr_ref[...] = dkr

        @pl.when(h > 0)
        def _():
            dkr_ref[...] += dkr

    hb = lambda i, h: (i, h)
    r0 = lambda i, h: (i, 0)
    fix = lambda i, h: (0, 0)
    blk = pl.BlockSpec((tm, HEAD_PAD), hb)
    return pl.pallas_call(
        body, grid=(s // tm, HEADS),
        in_specs=[blk, blk, blk, blk, blk, pl.BlockSpec((tm, HEAD_PAD), lambda i, h: (i, IN_COLS_PAD // HEAD_PAD - 1)),
                  pl.BlockSpec((tm, HEAD_PAD), r0), pl.BlockSpec((tm, HEAD_PAD), r0),
                  pl.BlockSpec((1, HEAD_PAD), fix), pl.BlockSpec((1, HEAD_PAD), fix)],
        out_specs=[blk, blk, pl.BlockSpec((tm, HEAD_PAD), r0), pl.BlockSpec((1, HEAD_PAD), fix), pl.BlockSpec((1, HEAD_PAD), fix)],
        out_shape=[SDS((s, HEADS * HEAD_PAD), BF16), SDS((s, HEADS * HEAD_PAD), BF16), SDS((s, HEAD_PAD), F32),
                   SDS((1, HEAD_PAD), F32), SDS((1, HEAD_PAD), F32)],
        compiler_params=_cp(2), name="mla_prep_bwd")(dqp, dkp, dvp, q_raw, kv_raw, z, cosf, sinf, gq, gk)


def _tri_pairs(n, row_major):
    if row_major:
        pairs = [(i, j) for i in range(n) for j in range(i + 1)]
    else:
        pairs = [(i, j) for j in range(n) for i in range(j, n)]
    ii = np.array([p[0] for p in pairs], np.int32)
    jj = np.array([p[1] for p in pairs], np.int32)
    return jnp.asarray(ii), jnp.asarray(jj)


STRIP = 64


def _strip_mask(r, tb):
    col = lax.broadcasted_iota(jnp.int32, (STRIP, tb), 1)
    return col >= r * CHUNK


def _fold8(x):
    acc = x[0:8, :]
    for g in range(1, STRIP // 8):
        acc = acc + x[g * 8:(g + 1) * 8, :]
    return acc


def _attn_fwd(qp, kp, vp, tb):
    s = qp.shape[0]
    nb = s // tb
    ii, jj = _tri_pairs(nb, True)
    n_strips = tb // STRIP

    def body(ii_ref, jj_ref, q_ref, k_ref, v_ref, of_ref, ob_ref, lse_ref, m_sc, l_sc, acc_sc, st_sc, pt_sc):
        t = pl.program_id(1)
        i = ii_ref[t]
        j = jj_ref[t]

        @pl.when(j == 0)
        def _():
            m_sc[...] = jnp.full_like(m_sc, -jnp.inf)
            l_sc[...] = jnp.zeros_like(l_sc)
            acc_sc[...] = jnp.zeros_like(acc_sc)

        def step(diag):
            st_sc[...] = lax.dot_general(k_ref[...], q_ref[...], (((1,), (1,)), ((), ())), preferred_element_type=F32)
            mx = None
            for r in range(n_strips):
                sc = st_sc[r * STRIP:(r + 1) * STRIP, :]
                if diag and r > 0:
                    sc = jnp.where(_strip_mask(r, tb), sc, -jnp.inf)
                m8 = sc[0:8, :]
                for g in range(1, STRIP // 8):
                    m8 = jnp.maximum(m8, sc[g * 8:(g + 1) * 8, :])
                mx = m8 if mx is None else jnp.maximum(mx, m8)
            m_old = m_sc[0:1, :]
            m_new = jnp.maximum(m_old, jnp.max(mx, axis=0, keepdims=True))
            alpha = jnp.exp2(m_old - m_new)
            ps = None
            for r in range(n_strips):
                p = jnp.exp2(st_sc[r * STRIP:(r + 1) * STRIP, :] - m_new)
                if diag and r > 0:
                    p = jnp.where(_strip_mask(r, tb), p, 0.0)
                p8 = _fold8(p)
                ps = p8 if ps is None else ps + p8
                pt_sc[r * STRIP:(r + 1) * STRIP, :] = p.astype(BF16)
            l_new = alpha * l_sc[0:1, :] + jnp.sum(ps, axis=0, keepdims=True)
            m_sc[...] = jnp.broadcast_to(m_new, m_sc.shape)
            l_sc[...] = jnp.broadcast_to(l_new, l_sc.shape)
            pv = lax.dot_general(v_ref[...], pt_sc[...], (((0,), (0,)), ((), ())), preferred_element_type=F32)
            acc_sc[...] = alpha * acc_sc[...] + pv

        @pl.when(j < i)
        def _():
            step(False)

        @pl.when(j == i)
        def _():
            step(True)
            l = l_sc[0:1, :]
            o = (acc_sc[...] / l).T
            of_ref[...] = o
            ob_ref[...] = o.astype(BF16)
            lse_ref[...] = m_sc[0:1, :] + jnp.log(l) * LOG2E

    qmap = lambda h, t, ii_ref, jj_ref: (ii_ref[t], h)
    kmap = lambda h, t, ii_ref, jj_ref: (jj_ref[t], h)
    blk = (tb, HEAD_PAD)
    gs = pltpu.PrefetchScalarGridSpec(
        num_scalar_prefetch=2, grid=(HEADS, int(ii.shape[0])),
        in_specs=[pl.BlockSpec(blk, qmap), pl.BlockSpec(blk, kmap), pl.BlockSpec(blk, kmap)],
        out_specs=[pl.BlockSpec(blk, qmap), pl.BlockSpec(blk, qmap),
                   pl.BlockSpec((None, 1, tb), lambda h, t, ii_ref, jj_ref: (h, 0, ii_ref[t]))],
        scratch_shapes=[pltpu.VMEM((8, tb), F32), pltpu.VMEM((8, tb), F32), pltpu.VMEM((HEAD_PAD, tb), F32),
                        pltpu.VMEM((tb, tb), F32), pltpu.VMEM((tb, tb), BF16)])
    w = HEADS * HEAD_PAD
    return pl.pallas_call(body, grid_spec=gs, out_shape=[SDS((s, w), F32), SDS((s, w), BF16), SDS((HEADS, 1, s), F32)],
                          compiler_params=_cp(2), name="attn_fwd")(ii, jj, qp, kp, vp)


def _attn_delta(do, o, tb):
    s = do.shape[0]

    def body(do_ref, o_ref, d_ref):
        d_ref[...] = jnp.sum((do_ref[...] * o_ref[...]).T, axis=0, keepdims=True)

    blk = pl.BlockSpec((tb, HEAD_PAD), lambda h, i: (i, h))
    return pl.pallas_call(body, grid=(HEADS, s // tb), in_specs=[blk, blk],
                          out_specs=pl.BlockSpec((None, 1, tb), lambda h, i: (h, 0, i)),
                          out_shape=SDS((HEADS, 1, s), F32), compiler_params=_cp(2), name="attn_delta")(do, o)


def _attn_bwd(qp, kp, vp, dob, lse, delta, tb):
    s = qp.shape[0]
    nb = s // tb
    ii, jj = _tri_pairs(nb, False)

    def body(ii_ref, jj_ref, q_ref, k_ref, v_ref, do_ref, lse_ref, dl_ref, dq_ref, dk_ref, dv_ref, st_sc, dpt_sc, pt_sc, dst_sc):
        t = pl.program_id(1)
        i = ii_ref[t]
        j = jj_ref[t]

        @pl.when(t == 0)
        def _():
            dq_ref[...] = jnp.zeros_like(dq_ref)

        @pl.when(i == j)
        def _():
            dk_ref[...] = jnp.zeros_like(dk_ref)
            dv_ref[...] = jnp.zeros_like(dv_ref)

        def step(diag):
            q = q_ref[...]
            k = k_ref[...]
            do = do_ref[...]
            st_sc[...] = lax.dot_general(k, q, (((1,), (1,)), ((), ())), preferred_element_type=F32)
            dpt_sc[...] = lax.dot_general(v_ref[...], do, (((1,), (1,)), ((), ())), preferred_element_type=F32)
            lse = lse_ref[...]
            dl = dl_ref[...]
            for r in range(tb // STRIP):
                rows = slice(r * STRIP, (r + 1) * STRIP)
                p = jnp.exp2(st_sc[rows, :] - lse)
                if diag and r > 0:
                    p = jnp.where(_strip_mask(r, tb), p, 0.0)
                ds = p * (dpt_sc[rows, :] - dl)
                pt_sc[rows, :] = p.astype(BF16)
                dst_sc[rows, :] = ds.astype(BF16)
            dv_ref[...] += jnp.dot(pt_sc[...], do, preferred_element_type=F32)
            dst = dst_sc[...]
            dk_ref[...] += jnp.dot(dst, q, preferred_element_type=F32)
            dq_ref[i] += lax.dot_general(k, dst, (((0,), (0,)), ((), ())), preferred_element_type=F32)

        @pl.when(j < i)
        def _():
            step(False)

        @pl.when(j == i)
        def _():
            step(True)

    qmap = lambda h, t, ii_ref, jj_ref: (ii_ref[t], h)
    kmap = lambda h, t, ii_ref, jj_ref: (jj_ref[t], h)
    rowmap = lambda h, t, ii_ref, jj_ref: (h, 0, ii_ref[t])
    blk = (tb, HEAD_PAD)
    gs = pltpu.PrefetchScalarGridSpec(
        num_scalar_prefetch=2, grid=(HEADS, int(ii.shape[0])),
        in_specs=[pl.BlockSpec(blk, qmap), pl.BlockSpec(blk, kmap), pl.BlockSpec(blk, kmap), pl.BlockSpec(blk, qmap),
                  pl.BlockSpec((None, 1, tb), rowmap), pl.BlockSpec((None, 1, tb), rowmap)],
        out_specs=[pl.BlockSpec((None, nb, HEAD_PAD, tb), lambda h, t, ii_ref, jj_ref: (h, 0, 0, 0)),
                   pl.BlockSpec(blk, kmap), pl.BlockSpec(blk, kmap)],
        scratch_shapes=[pltpu.VMEM((tb, tb), F32), pltpu.VMEM((tb, tb), F32), pltpu.VMEM((tb, tb), BF16), pltpu.VMEM((tb, tb), BF16)])
    w = HEADS * HEAD_PAD
    dqt, dk, dv = pl.pallas_call(
        body, grid_spec=gs, out_shape=[SDS((HEADS, nb, HEAD_PAD, tb), F32), SDS((s, w), F32), SDS((s, w), F32)],
        compiler_params=_cp(2), name="attn_bwd")(ii, jj, qp, kp, vp, dob, lse, delta)
    return jnp.transpose(dqt, (1, 3, 0, 2)).reshape(s, w), dk, dv


def _head_norm(t, g):
    r = lax.rsqrt(jnp.mean(t * t, axis=-1, keepdims=True) + EPS)
    th = t * r
    return r, th, th * g


def _softmax_rows(sc):
    m = jnp.max(sc, axis=-1, keepdims=True)
    e = jnp.exp(sc - m)
    return e / jnp.sum(e, axis=-1, keepdims=True)


def _memattn_fwd(qm, kvm, gq, gk, tm):
    s = qm.shape[0]
    hd = MEM_HEAD_DIM

    def body(q_ref, k_ref, v_ref, gq_ref, gk_ref, o_ref):
        _, _, qn = _head_norm(q_ref[...], gq_ref[...])
        _, _, kn = _head_norm(k_ref[...], gk_ref[...])
        sc = lax.dot_general(qn.astype(BF16), kn.astype(BF16), (((1,), (1,)), ((), ())), preferred_element_type=F32)
        p = _softmax_rows(sc * (1.0 / math.sqrt(hd)))
        o_ref[...] = jnp.dot(p.astype(BF16), v_ref[...].astype(BF16), preferred_element_type=F32).astype(BF16)

    fix = lambda i, h: (0, 0)
    return pl.pallas_call(
        body, grid=(s // tm, MEM_HEADS),
        in_specs=[pl.BlockSpec((tm, hd), lambda i, h: (i, h)), pl.BlockSpec((MEM_LEN, hd), lambda i, h: (0, h)),
                  pl.BlockSpec((MEM_LEN, hd), lambda i, h: (0, MEM_HEADS + h)), pl.BlockSpec((1, hd), fix), pl.BlockSpec((1, hd), fix)],
        out_specs=pl.BlockSpec((tm, hd), lambda i, h: (i, h)),
        out_shape=SDS((s, MEM_HEADS * hd), BF16), compiler_params=_cp(2), name="memattn_fwd")(qm, kvm, kvm, gq, gk)


def _memattn_bwd(qm, kvm, d_o, gq, gk, tm):
    s = qm.shape[0]
    hd = MEM_HEAD_DIM

    def body(q_ref, k_ref, v_ref, do_ref, gq_ref, gk_ref, dq_ref, dk_ref, dv_ref, dgq_ref, dgk_ref, dkn_sc):
        h = pl.program_id(0)
        i = pl.program_id(1)
        last = pl.num_programs(1) - 1

        @pl.when((h == 0) & (i == 0))
        def _():
            dgq_ref[...] = jnp.zeros_like(dgq_ref)
            dgk_ref[...] = jnp.zeros_like(dgk_ref)

        @pl.when(i == 0)
        def _():
            dv_ref[...] = jnp.zeros_like(dv_ref)
            dkn_sc[...] = jnp.zeros_like(dkn_sc)

        rq, qh, qn = _head_norm(q_ref[...], gq_ref[...])
        rk, kh, kn = _head_norm(k_ref[...], gk_ref[...])
        qnb = qn.astype(BF16)
        knb = kn.astype(BF16)
        scale = 1.0 / math.sqrt(hd)
        sc = lax.dot_general(qnb, knb, (((1,), (1,)), ((), ())), preferred_element_type=F32)
        p = _softmax_rows(sc * scale)
        do = do_ref[...].astype(BF16)
        dp = lax.dot_general(do, v_ref[...].astype(BF16), (((1,), (1,)), ((), ())), preferred_element_type=F32)
        dv_ref[...] += lax.dot_general(p.astype(BF16), do, (((0,), (0,)), ((), ())), preferred_element_type=F32)
        ds = ((p * (dp - jnp.sum(dp * p, axis=-1, keepdims=True))) * scale).astype(BF16)
        dqn = jnp.dot(ds, knb, preferred_element_type=F32)
        dkn_sc[...] += lax.dot_general(ds, qnb, (((0,), (0,)), ((), ())), preferred_element_type=F32)
        dgq_ref[...] += jnp.sum(dqn * qh, axis=0, keepdims=True)
        dqh = dqn * gq_ref[...]
        dq_ref[...] = (rq * (dqh - qh * jnp.mean(dqh * qh, axis=-1, keepdims=True))).astype(BF16)

        @pl.when(i == last)
        def _():
            dkn = dkn_sc[...]
            dgk_ref[...] += jnp.sum(dkn * kh, axis=0, keepdims=True)
            dkh = dkn * gk_ref[...]
            dk_ref[...] = rk * (dkh - kh * jnp.mean(dkh * kh, axis=-1, keepdims=True))

    fix = lambda h, i: (0, 0)
    qb = pl.BlockSpec((tm, hd), lambda h, i: (i, h))
    kb = pl.BlockSpec((MEM_LEN, hd), lambda h, i: (0, h))
    return pl.pallas_call(
        body, grid=(MEM_HEADS, s // tm),
        in_specs=[qb, kb, pl.BlockSpec((MEM_LEN, hd), lambda h, i: (0, MEM_HEADS + h)), qb,
                  pl.BlockSpec((1, hd), fix), pl.BlockSpec((1, hd), fix)],
        out_specs=[qb, kb, kb, pl.BlockSpec((1, hd), fix), pl.BlockSpec((1, hd), fix)],
        out_shape=[SDS((s, MEM_HEADS * hd), BF16), SDS((MEM_LEN, MEM_HEADS * hd), F32), SDS((MEM_LEN, MEM_HEADS * hd), F32),
                   SDS((1, hd), F32), SDS((1, hd), F32)],
        scratch_shapes=[pltpu.VMEM((MEM_LEN, hd), F32)],
        compiler_params=_cp(2), name="memattn_bwd")(qm, kvm, kvm, d_o, gq, gk)


def _ffn_specs(tm, tn, nbj, s, order_ji):
    if order_ji:
        ij = lambda f: (lambda j, i: f(i, j))
    else:
        ij = lambda f: f
    prev = lambda i: jnp.maximum(i * (tm // FFN_HALO) - 1, 0)
    cur_g = pl.BlockSpec((tm, tn), ij(lambda i, j: (i, j)))
    cur_v = pl.BlockSpec((tm, tn), ij(lambda i, j: (i, j + nbj)))
    halo_g = pl.BlockSpec((FFN_HALO, tn), ij(lambda i, j: (prev(i), j)))
    halo_v = pl.BlockSpec((FFN_HALO, tn), ij(lambda i, j: (prev(i), j + nbj)))
    w_g = pl.BlockSpec((8, tn), ij(lambda i, j: (0, j)))
    w_v = pl.BlockSpec((8, tn), ij(lambda i, j: (0, j + nbj)))
    b_g = pl.BlockSpec((1, tn), ij(lambda i, j: (0, j)))
    b_v = pl.BlockSpec((1, tn), ij(lambda i, j: (0, j + nbj)))
    return cur_g, cur_v, halo_g, halo_v, w_g, w_v, b_g, b_v


def _conv3(cur_ref, halo_ref, w_ref, b_ref, ext, first, tm):
    ext[0:FFN_HALO, :] = jnp.where(first, 0.0, halo_ref[...])
    ext[FFN_HALO:, :] = cur_ref[...]
    return (w_ref[0:1, :] * ext[FFN_HALO - 2:FFN_HALO - 2 + tm, :] + w_ref[1:2, :] * ext[FFN_HALO - 1:FFN_HALO - 1 + tm, :]
            + w_ref[2:3, :] * ext[FFN_HALO:FFN_HALO + tm, :] + b_ref[...])


def _ffn_fwd(up0, w8, b, tm, tn):
    s = up0.shape[0]
    nbj = D_FF // tn

    def body(g_ref, v_ref, gh_ref, vh_ref, wg_ref, wv_ref, bg_ref, bv_ref, act_ref, ext):
        first = pl.program_id(0) == 0
        ug = _conv3(g_ref, gh_ref, wg_ref, bg_ref, ext, first, tm)
        uv = _conv3(v_ref, vh_ref, wv_ref, bv_ref, ext, first, tm)
        act_ref[...] = ((ug * jax.nn.sigmoid(ug)) * uv).astype(BF16)

    specs = _ffn_specs(tm, tn, nbj, s, False)
    return pl.pallas_call(
        body, grid=(s // tm, nbj), in_specs=list(specs),
        out_specs=pl.BlockSpec((tm, tn), lambda i, j: (i, j)), out_shape=SDS((s, D_FF), BF16),
        scratch_shapes=[pltpu.VMEM((tm + FFN_HALO, tn), F32)],
        compiler_params=_cp(2), name="ffn_fwd")(up0, up0, up0, up0, w8, w8, b, b)


def _ffn_bwd_act(d_act, up0, w8, b, tm, tn):
    s = up0.shape[0]
    nbj = D_FF // tn

    def body(da_ref, g_ref, v_ref, gh_ref, vh_ref, wg_ref, wv_ref, bg_ref, bv_ref, dg_ref, dv_ref, dbg_ref, dbv_ref, ext):
        i = pl.program_id(1)
        first = i == 0

        @pl.when(first)
        def _():
            dbg_ref[...] = jnp.zeros_like(dbg_ref)
            dbv_ref[...] = jnp.zeros_like(dbv_ref)

        ug = _conv3(g_ref, gh_ref, wg_ref, bg_ref, ext, first, tm)
        uv = _conv3(v_ref, vh_ref, wv_ref, bv_ref, ext, first, tm)
        sg = jax.nn.sigmoid(ug)
        da = da_ref[...]
        dval = da * (ug * sg)
        dgate = da * uv * (sg * (1.0 + ug * (1.0 - sg)))
        dg_ref[...] = dgate
        dv_ref[...] = dval
        dbg_ref[...] += jnp.sum(dgate, axis=0, keepdims=True)
        dbv_ref[...] += jnp.sum(dval, axis=0, keepdims=True)

    specs = _ffn_specs(tm, tn, nbj, s, True)
    cur = pl.BlockSpec((tm, tn), lambda j, i: (i, j))
    acc = pl.BlockSpec((1, tn), lambda j, i: (0, j))
    return pl.pallas_call(
        body, grid=(nbj, s // tm), in_specs=[cur] + list(specs),
        out_specs=[cur, cur, acc, acc],
        out_shape=[SDS((s, D_FF), F32), SDS((s, D_FF), F32), SDS((1, D_FF), F32), SDS((1, D_FF), F32)],
        scratch_shapes=[pltpu.VMEM((tm + FFN_HALO, tn), F32)],
        compiler_params=_cp(2), name="ffn_bwd_act")(d_act, up0, up0, up0, up0, w8, w8, b, b)


def _ffn_bwd_conv(d_upg, d_upv, up0, w8, tm, tn):
    s = up0.shape[0]
    nbj = D_FF // tn

    def body(dg_ref, dv_ref, dgn_ref, dvn_ref, g_ref, v_ref, gh_ref, vh_ref, wg_ref, wv_ref,
             og_ref, ov_ref, dwg_ref, dwv_ref, extd, extu):
        i = pl.program_id(1)
        first = i == 0
        last = i == pl.num_programs(1) - 1

        @pl.when(first)
        def _():
            dwg_ref[...] = jnp.zeros_like(dwg_ref)
            dwv_ref[...] = jnp.zeros_like(dwv_ref)

        def half(d_ref, dn_ref, u_ref, uh_ref, w_ref, o_ref, dw_ref):
            extd[0:tm, :] = d_ref[...]
            extd[tm:, :] = jnp.where(last, 0.0, dn_ref[...])
            o_ref[...] = (w_ref[2:3, :] * extd[0:tm, :] + w_ref[1:2, :] * extd[1:1 + tm, :]
                          + w_ref[0:1, :] * extd[2:2 + tm, :]).astype(BF16)
            extu[0:FFN_HALO, :] = jnp.where(first, 0.0, uh_ref[...])
            extu[FFN_HALO:, :] = u_ref[...]
            d = d_ref[...]
            for k in range(3):
                o = FFN_HALO - 2 + k
                dw_ref[k:k + 1, :] += jnp.sum(d * extu[o:o + tm, :], axis=0, keepdims=True)

        half(dg_ref, dgn_ref, g_ref, gh_ref, wg_ref, og_ref, dwg_ref)
        half(dv_ref, dvn_ref, v_ref, vh_ref, wv_ref, ov_ref, dwv_ref)

    cur_g, cur_v, halo_g, halo_v, w_g, w_v, _, _ = _ffn_specs(tm, tn, nbj, s, True)
    cur = pl.BlockSpec((tm, tn), lambda j, i: (i, j))
    nxt = pl.BlockSpec((FFN_HALO, tn), lambda j, i: (jnp.minimum((i + 1) * (tm // FFN_HALO), s // FFN_HALO - 1), j))
    acc = pl.BlockSpec((8, tn), lambda j, i: (0, j))
    return pl.pallas_call(
        body, grid=(nbj, s // tm), in_specs=[cur, cur, nxt, nxt, cur_g, cur_v, halo_g, halo_v, w_g, w_v],
        out_specs=[cur, cur, acc, acc],
        out_shape=[SDS((s, D_FF), BF16), SDS((s, D_FF), BF16), SDS((8, D_FF), F32), SDS((8, D_FF), F32)],
        scratch_shapes=[pltpu.VMEM((tm + FFN_HALO, tn), F32), pltpu.VMEM((tm + FFN_HALO, tn), F32)],
        compiler_params=_cp(2), name="ffn_bwd_conv")(d_upg, d_upv, d_upg, d_upv, up0, up0, up0, up0, w8, w8)


def _down_loss(act, w_down, x2, target, tm):
    s = act.shape[0]

    def body(a_ref, w_ref, x_ref, t_ref, dyf_ref, dyb_ref, ls_ref):
        @pl.when(pl.program_id(0) == 0)
        def _():
            ls_ref[...] = jnp.zeros_like(ls_ref)

        y = x_ref[...] + jnp.dot(a_ref[...], w_ref[...], preferred_element_type=F32)
        e = y - t_ref[...]
        ls_ref[...] += jnp.sum(e * e)
        dy = e * (1.0 / D_MODEL)
        dyf_ref[...] = dy
        dyb_ref[...] = dy.astype(BF16)

    row = lambda i: (i, 0)
    return pl.pallas_call(
        body, grid=(s // tm,),
        in_specs=[pl.BlockSpec((tm, D_FF), row), pl.BlockSpec((D_FF, D_MODEL), lambda i: (0, 0)),
                  pl.BlockSpec((tm, D_MODEL), row), pl.BlockSpec((tm, D_MODEL), row)],
        out_specs=[pl.BlockSpec((tm, D_MODEL), row), pl.BlockSpec((tm, D_MODEL), row), pl.BlockSpec((8, 128), lambda i: (0, 0))],
        out_shape=[SDS((s, D_MODEL), F32), SDS((s, D_MODEL), BF16), SDS((8, 128), F32)],
        compiler_params=_cp(1), name="down_loss")(act, w_down, x2, target)


def _adamw(w, g, m, v, tr, name):
    rows, cols = w.shape

    def body(w_ref, g_ref, m_ref, v_ref, d_ref, mo_ref, vo_ref):
        gv = g_ref[...]
        mn = ADAM_B1 * m_ref[...] + (1.0 - ADAM_B1) * gv
        vn = ADAM_B2 * v_ref[...] + (1.0 - ADAM_B2) * (gv * gv)
        m_hat = mn / (1.0 - ADAM_B1 ** ADAM_STEP)
        v_hat = vn / (1.0 - ADAM_B2 ** ADAM_STEP)
        d_ref[...] = -ADAM_LR * (m_hat / (jnp.sqrt(v_hat) + ADAM_EPS) + ADAM_WD * w_ref[...])
        mo_ref[...] = mn
        vo_ref[...] = vn

    blk = pl.BlockSpec((tr, cols), lambda i: (i, 0))
    return pl.pallas_call(body, grid=(rows // tr,), in_specs=[blk] * 4, out_specs=[blk] * 3,
                          out_shape=[SDS((rows, cols), F32)] * 3, compiler_params=_cp(1), name=name)(w, g, m, v)


ANY = pl.BlockSpec(memory_space=pl.ANY)


def _coords():
    return lax.axis_index("x"), lax.axis_index("y"), lax.axis_index("c")


def _other_chips(x, y):
    return [(1 - x, y), (x, 1 - y), (1 - x, 1 - y)]


D2D_CHUNKS = 8
ICI_CHUNKS = 4


def _row_chunks(n_rows, n_chunks, align):
    step = -(-n_rows // (n_chunks * align)) * align
    return [(r, min(step, n_rows - r)) for r in range(0, n_rows, step)]


def _ag_weights(wsh):
    rows, cols = wsh.shape
    half_rows = rows // 2

    def body(w_ref, out_ref, send_sems, recv_sems):
        x, y, c = _coords()
        s_me = 2 * x + y
        chips = _other_chips(x, y)
        sibling = (x, y, 1 - c)
        my_base = c * half_rows
        sib_base = (1 - c) * half_rows

        def piece(base, r0, nr):
            return pl.ds(pl.multiple_of(base + r0, 16), nr)

        def copy(k, shard, rows_, to, src=None):
            dst = out_ref.at[shard, rows_]
            return pltpu.make_async_remote_copy(src_ref=dst if src is None else src, dst_ref=dst, send_sem=send_sems.at[k],
                                                recv_sem=recv_sems.at[k], device_id=to, device_id_type=MESH)

        for k, (px, py) in enumerate(chips):
            for r0, nr in _row_chunks(half_rows, ICI_CHUNKS, 16):
                copy(k, s_me, piece(my_base, r0, nr), (px, py, c), src=w_ref.at[piece(my_base, r0, nr)]).start()
        for k, (px, py) in enumerate(chips):
            copy(k, 2 * px + py, piece(my_base, 0, half_rows), (px, py, c)).wait_recv()
            for r0, nr in _row_chunks(half_rows, ICI_CHUNKS, 16):
                copy(3 + k, 2 * px + py, piece(my_base, r0, nr), sibling).start()
        for k, (px, py) in enumerate(chips):
            copy(3 + k, 2 * px + py, piece(sib_base, 0, half_rows), sibling).wait_recv()
        for k in range(6):
            copy(k, s_me, piece(my_base, 0, half_rows), sibling).wait_send()

    return pl.pallas_call(
        body, in_specs=[ANY], out_specs=ANY, out_shape=SDS((4, rows, cols), wsh.dtype),
        scratch_shapes=[pltpu.SemaphoreType.DMA((6,)), pltpu.SemaphoreType.DMA((6,))],
        name="ag_weights")(wsh)


def _rs_swap_halves(gfull):
    n_sh, rows, cols = gfull.shape
    half_rows = rows // 2

    def body(g_ref, recv_ref, send_sem, recv_sem):
        x, y, c = _coords()
        sib_base = (1 - c) * half_rows
        for sh in range(n_sh):
            for r0, nr in _row_chunks(half_rows, D2D_CHUNKS, 8):
                pltpu.make_async_remote_copy(
                    src_ref=g_ref.at[sh, pl.ds(pl.multiple_of(sib_base + r0, 8), nr)], dst_ref=recv_ref.at[sh, pl.ds(r0, nr)],
                    send_sem=send_sem, recv_sem=recv_sem, device_id=(x, y, 1 - c), device_id_type=MESH).start()
        pltpu.make_async_remote_copy(src_ref=recv_ref, dst_ref=recv_ref, send_sem=send_sem, recv_sem=recv_sem,
                                     device_id=(x, y, 1 - c), device_id_type=MESH).wait()

    return pl.pallas_call(
        body, in_specs=[ANY], out_specs=ANY, out_shape=SDS((n_sh, half_rows, cols), gfull.dtype),
        scratch_shapes=[pltpu.SemaphoreType.DMA, pltpu.SemaphoreType.DMA], name="rs_swap_halves")(gfull)


def _rs_add_pair(gfull, recv, core, tr):
    n_sh, rows, cols = gfull.shape
    half_rows = rows // 2
    nblk = half_rows // tr

    def body(c_ref, g_ref, r_ref, o_ref, ob_ref):
        acc = g_ref[...] + r_ref[...]
        o_ref[...] = acc
        ob_ref[...] = acc.astype(BF16)

    out = pl.BlockSpec((None, tr, cols), lambda sh, i, c_ref: (sh, i, 0))
    gs = pltpu.PrefetchScalarGridSpec(
        num_scalar_prefetch=1, grid=(n_sh, nblk),
        in_specs=[pl.BlockSpec((None, tr, cols), lambda sh, i, c_ref: (sh, c_ref[0] * nblk + i, 0)), out],
        out_specs=[out, out])
    return pl.pallas_call(body, grid_spec=gs, out_shape=[SDS((n_sh, half_rows, cols), F32), SDS((n_sh, half_rows, cols), BF16)],
                          compiler_params=_cp(2), name="rs_add_pair")(core, gfull, recv)


def _rs_to_owner(chipsum):
    n_sh, half_rows, cols = chipsum.shape

    def body(cs_ref, recv_ref, send_sems, recv_sems):
        x, y, c = _coords()
        chips = _other_chips(x, y)
        for k, (px, py) in enumerate(chips):
            for r0, nr in _row_chunks(half_rows, ICI_CHUNKS, 16):
                pltpu.make_async_remote_copy(
                    src_ref=cs_ref.at[2 * px + py, pl.ds(r0, nr)], dst_ref=recv_ref.at[k, pl.ds(r0, nr)],
                    send_sem=send_sems.at[k], recv_sem=recv_sems.at[k], device_id=(px, py, c), device_id_type=MESH).start()
        for k, (px, py) in enumerate(chips):
            pltpu.make_async_remote_copy(src_ref=recv_ref.at[k], dst_ref=recv_ref.at[k], send_sem=send_sems.at[k],
                                         recv_sem=recv_sems.at[k], device_id=(px, py, c), device_id_type=MESH).wait()

    return pl.pallas_call(
        body, in_specs=[ANY], out_specs=ANY, out_shape=SDS((3, half_rows, cols), chipsum.dtype),
        scratch_shapes=[pltpu.SemaphoreType.DMA((3,)), pltpu.SemaphoreType.DMA((3,))], name="rs_to_owner")(chipsum)


def _rs_add_chips(chipsum, recv, shard_core, tr):
    _, half_rows, cols = chipsum.shape

    def body(s_ref, m_ref, r0_ref, r1_ref, r2_ref, o_ref):
        o_ref[...] = ((m_ref[...] + r0_ref[...].astype(F32)) + r1_ref[...].astype(F32)) + r2_ref[...].astype(F32)

    gs = pltpu.PrefetchScalarGridSpec(
        num_scalar_prefetch=1, grid=(half_rows // tr,),
        in_specs=[pl.BlockSpec((None, tr, cols), lambda i, s_ref: (s_ref[0], i, 0))]
        + [pl.BlockSpec((None, tr, cols), (lambda k: lambda i, s_ref: (k, i, 0))(k)) for k in range(3)],
        out_specs=pl.BlockSpec((None, tr, cols), lambda i, s_ref: (s_ref[1], i, 0)))
    return pl.pallas_call(body, grid_spec=gs, out_shape=SDS((2, half_rows, cols), F32),
                          compiler_params=_cp(1), name="rs_add_chips")(shard_core, chipsum, recv, recv, recv)


def _rs_join_halves(buf):
    _, half_rows, cols = buf.shape

    def body(b_ref, out_ref, send_sem, recv_sem):
        x, y, c = _coords()
        for r0, nr in _row_chunks(half_rows, D2D_CHUNKS, 8):
            pltpu.make_async_remote_copy(src_ref=out_ref.at[c, pl.ds(r0, nr)], dst_ref=out_ref.at[c, pl.ds(r0, nr)], send_sem=send_sem,
                                         recv_sem=recv_sem, device_id=(x, y, 1 - c), device_id_type=MESH).start()
        pltpu.make_async_remote_copy(src_ref=out_ref.at[c], dst_ref=out_ref.at[c], send_sem=send_sem, recv_sem=recv_sem,
                                     device_id=(x, y, 1 - c), device_id_type=MESH).wait()

    return pl.pallas_call(
        body, in_specs=[ANY], out_specs=ANY, out_shape=SDS(buf.shape, buf.dtype), input_output_aliases={0: 0},
        scratch_shapes=[pltpu.SemaphoreType.DMA, pltpu.SemaphoreType.DMA], name="rs_join_halves")(buf)


BIG = [("w_in", (1024, 1440), 1), ("w_uq", (256, 768), 1), ("w_ukv", (128, 1024), 1), ("w_out", (1024, 1024), 0),
       ("w_mem_q", (1024, 1024), 0), ("w_mem_kv", (1024, 2048), 1), ("w_mem_o", (1024, 1024), 0),
       ("w_up", (1024, 5632), 1), ("w_down", (2816, 1024), 0)]
SMALL_REP = [("mix_norm_g", 1024), ("b_conv_in", 1024), ("b_conv_dw", 512), ("conv_ln_g", 512), ("conv_ln_b", 512),
             ("q_lat_norm_g", 256), ("kv_lat_norm_g", 128), ("q_norm_g", 96), ("k_norm_g", 96), ("mem_norm_x_g", 1024),
             ("mem_norm_m_g", 1024), ("mem_q_norm_g", 256), ("mem_k_norm_g", 256), ("ffn_norm_g", 1024), ("b_ffn_dw", 5632)]
SMALL_SH = [("w_conv_dw", (31, 512)), ("w_ffn_dw", (3, 5632))]


def _shard_shape(shape, axis):
    return tuple(d // 4 if a == axis else d for a, d in enumerate(shape))


def _pack_rows(parts, rows, cols):
    flat = jnp.concatenate([p.reshape(-1) for p in parts])
    flat = jnp.pad(flat, (0, rows * cols - flat.shape[0]))
    return flat.reshape(rows, cols)


def _pack_big_shards(ws):
    return _pack_rows([ws[n] for n, _, _ in BIG], PACK_ROWS, PACK_COLS)


def _unpack_big_shards(packed):
    out, r = {}, 0
    for n, shape, axis in BIG:
        sh = _shard_shape(shape, axis)
        nr = sh[0] * sh[1] // PACK_COLS
        out[n] = packed[r:r + nr].reshape(sh)
        r += nr
    return out


def _unpack_gathered(g):
    out, r = {}, 0
    for n, shape, axis in BIG:
        sh = _shard_shape(shape, axis)
        nr = sh[0] * sh[1] // PACK_COLS
        parts = g[:, r:r + nr].reshape((4,) + sh)
        out[n] = jnp.concatenate([parts[i] for i in range(4)], axis=axis)
        r += nr
    return out


def _pack_full_grads(gs):
    slabs = []
    for i in range(4):
        parts = []
        for n, shape, axis in BIG:
            sh = _shard_shape(shape, axis)
            parts.append(lax.slice_in_dim(gs[n], i * sh[axis], (i + 1) * sh[axis], axis=axis))
        slabs.append(_pack_rows(parts, PACK_ROWS, PACK_COLS))
    return jnp.stack(slabs)


def _rope_tables(positions):
    inv_freq = ROPE_THETA ** (-jnp.arange(0, ROPE, 2, dtype=F32) / ROPE)
    ang = positions.astype(F32)[:, None] * inv_freq
    cos, sin = jnp.cos(ang), jnp.sin(ang)
    s = positions.shape[0]
    cosf = jnp.concatenate([jnp.ones((s, NOPE), F32), cos, cos, jnp.ones((s, HEAD_PAD - HEAD_DIM), F32)], axis=-1)
    sinf = jnp.concatenate([jnp.zeros((s, NOPE), F32), -sin, sin, jnp.zeros((s, HEAD_PAD - HEAD_DIM), F32)], axis=-1)
    return cosf, sinf


def _pad_heads(w, per_head):
    k = w.shape[0]
    w3 = w.reshape(k, HEADS, per_head)
    return jnp.pad(w3, ((0, 0), (0, 0), (0, HEAD_PAD - per_head))).reshape(k, HEADS * HEAD_PAD)


def _layer_grads(x, mem, positions, target, wf, sp):
    s = x.shape[0]
    tm = _row_tile(s, 512)
    tc = _row_tile(s, 256)
    tb = 512 if s % 512 == 0 and s > 512 else s // 2
    row2 = lambda a: a.reshape(1, -1)

    w_in = wf["w_in"]
    w_in_pad = jnp.concatenate([w_in[:, :1408], jnp.zeros((D_MODEL, NOPE), BF16), w_in[:, 1408:],
                                jnp.zeros((D_MODEL, HEAD_PAD - HEAD_DIM), BF16)], axis=1)
    w_uq_pad = _pad_heads(wf["w_uq"], HEAD_DIM)
    w_ukv = wf["w_ukv"]
    w_out_u = wf["w_out"][:CONV_CH]
    w_out_o = jnp.pad(wf["w_out"][CONV_CH:].reshape(HEADS, NOPE, D_MODEL), ((0, 0), (NOPE, 0), (0, 0))).reshape(HEADS * HEAD_PAD, D_MODEL)
    w_up_g, w_up_v = wf["w_up"][:, :D_FF], wf["w_up"][:, D_FF:]
    gq_pad = jnp.pad(sp["q_norm_g"], (0, HEAD_PAD - HEAD_DIM)).reshape(1, HEAD_PAD)
    gk_pad = jnp.pad(sp["k_norm_g"], (0, HEAD_PAD - HEAD_DIM)).reshape(1, HEAD_PAD)
    w_dw32 = jnp.pad(sp["w_conv_dw"], ((0, 1), (0, 0)))
    w_ffn8 = jnp.pad(sp["w_ffn_dw"], ((0, 5), (0, 0)))
    b_ffn = row2(sp["b_ffn_dw"])
    cosf, sinf = _rope_tables(positions)

    z, h1 = _norm_linear(x, 0, D_MODEL, row2(sp["mix_norm_g"]), w_in_pad, F32, tm, 512, "in_proj")
    u, u0, u1 = _conv_fwd(z, row2(sp["b_conv_in"]), w_dw32, row2(sp["b_conv_dw"]), row2(sp["conv_ln_g"]), row2(sp["conv_ln_b"]), tc)
    q_raw, cqn = _norm_linear(z, 1024 // Q_RANK, Q_RANK, row2(sp["q_lat_norm_g"]), w_uq_pad, F32, tm, 1024, "q_up")
    kv_raw, ckvn = _norm_linear(z, 1280 // KV_RANK, KV_RANK, row2(sp["kv_lat_norm_g"]), w_ukv, F32, tm, 1024, "kv_up")
    qp, kp, vp = _mla_prep(q_raw, kv_raw, z, cosf, sinf, gq_pad, gk_pad, tm)
    o_f, o_b, lse = _attn_fwd(qp, kp, vp, tb)
    (x1,) = _linear([(u, w_out_u), (o_b, w_out_o)], False, x, [F32], tm, 1024, "out_proj")

    qm, hq = _norm_linear(x1, 0, D_MODEL, row2(sp["mem_norm_x_g"]), wf["w_mem_q"], F32, tm, 1024, "memq_proj")
    kvm, hm = _norm_linear(mem, 0, D_MODEL, row2(sp["mem_norm_m_g"]), wf["w_mem_kv"], F32, MEM_LEN, 1024, "memkv_proj")
    gmq, gmk = row2(sp["mem_q_norm_g"]), row2(sp["mem_k_norm_g"])
    o_m = _memattn_fwd(qm, kvm, gmq, gmk, tm)
    (x2,) = _linear([(o_m, wf["w_mem_o"])], False, x1, [F32], tm, 1024, "memo_proj")

    up0, h3 = _norm_linear(x2, 0, D_MODEL, row2(sp["ffn_norm_g"]), wf["w_up"], F32, tm, 512, "up_proj")
    act = _ffn_fwd(up0, w_ffn8, b_ffn, tc, D_FF // 2)
    dy_f, dy_b, lsum = _down_loss(act, wf["w_down"], x2, target, tm)

    g = {}
    (d_act,) = _linear([(dy_b, wf["w_down"])], True, None, [F32], tm, D_FF // 2, "down_bwd")
    g["w_down"] = _dw(act, dy_b, "dw_down")
    d_upg, d_upv, dbg, dbv = _ffn_bwd_act(d_act, up0, w_ffn8, b_ffn, tc, D_FF // 2)
    d_up0g, d_up0v, dwg, dwv = _ffn_bwd_conv(d_upg, d_upv, up0, w_ffn8, tc, D_FF // 2)
    g["b_ffn_dw"] = jnp.concatenate([dbg, dbv], axis=1).reshape(-1)
    g["w_ffn_dw"] = jnp.concatenate([dwg[:3], dwv[:3]], axis=1)
    g["w_up"] = jnp.concatenate([_dw(h3, d_up0g, "dw_up_g"), _dw(h3, d_up0v, "dw_up_v")], axis=1)
    d_x2f, d_x2b, dg = _linear_normbwd([(d_up0g, w_up_g), (d_up0v, w_up_v)], x2, 0, row2(sp["ffn_norm_g"]), dy_f,
                                       [F32, BF16], tc, "up_bwd")
    g["ffn_norm_g"] = dg.reshape(-1)

    (d_om,) = _linear([(d_x2b, wf["w_mem_o"])], True, None, [BF16], tm, 1024, "memo_bwd")
    g["w_mem_o"] = _dw(o_m, d_x2b, "dw_mem_o")
    d_qm, d_km, d_vm, dgq, dgk = _memattn_bwd(qm, kvm, d_om, gmq, gmk, tm)
    g["mem_q_norm_g"], g["mem_k_norm_g"] = dgq.reshape(-1), dgk.reshape(-1)
    d_kvm = jnp.concatenate([d_km, d_vm], axis=1)
    g["w_mem_q"] = _dw(hq, d_qm, "dw_mem_q")
    g["w_mem_kv"] = _dw(hm, d_kvm, "dw_mem_kv")
    d_x1f, d_x1b, dg = _linear_normbwd([(d_qm, wf["w_mem_q"])], x1, 0, row2(sp["mem_norm_x_g"]), d_x2f, [F32, BF16], tm, "memq_bwd")
    g["mem_norm_x_g"] = dg.reshape(-1)
    _, dg = _linear_normbwd([(d_kvm, wf["w_mem_kv"])], mem, 0, row2(sp["mem_norm_m_g"]), None, [BF16], MEM_LEN, "memkv_bwd")
    g["mem_norm_m_g"] = dg.reshape(-1)

    (d_u,) = _linear([(d_x1b, w_out_u)], True, None, [F32], tm, CONV_CH, "out_bwd_u")
    d_of, d_ob = _linear([(d_x1b, w_out_o)], True, None, [F32, BF16], tm, 1024, "out_bwd_o")
    dw_out_u = _dw(u, d_x1b, "dw_out_u")
    dw_out_o = _dw(o_b, d_x1b, "dw_out_o")
    g["w_out"] = jnp.concatenate([dw_out_u, dw_out_o.reshape(HEADS, HEAD_PAD, D_MODEL)[:, NOPE:].reshape(HEADS * NOPE, D_MODEL)], axis=0)
    delta = _attn_delta(d_of, o_f, tb)
    dqp, dkp, dvp = _attn_bwd(qp, kp, vp, d_ob, lse, delta, tb)
    d_qraw, d_kvraw, d_kr, dgq, dgk = _mla_prep_bwd(dqp, dkp, dvp, q_raw, kv_raw, z, cosf, sinf, gq_pad, gk_pad, tm)
    g["q_norm_g"], g["k_norm_g"] = dgq.reshape(-1)[:HEAD_DIM], dgk.reshape(-1)[:HEAD_DIM]
    g["w_uq"] = _dw(cqn, d_qraw, "dw_uq").reshape(Q_RANK, HEADS, HEAD_PAD)[:, :, :HEAD_DIM].reshape(Q_RANK, HEADS * HEAD_DIM)
    g["w_ukv"] = _dw(ckvn, d_kvraw, "dw_ukv")
    d_cq, dg = _linear_normbwd([(d_qraw, w_uq_pad)], z, 1024 // Q_RANK, row2(sp["q_lat_norm_g"]), None, [BF16], tm, "q_up_bwd")
    g["q_lat_norm_g"] = dg.reshape(-1)
    d_ckv, dg = _linear_normbwd([(d_kvraw, w_ukv)], z, 1280 // KV_RANK, row2(sp["kv_lat_norm_g"]), None, [BF16], tm, "kv_up_bwd")
    g["kv_lat_norm_g"] = dg.reshape(-1)
    d_u1, dlg, dlb, dbdw = _conv_bwd_ln(d_u, u1, row2(sp["conv_ln_g"]), row2(sp["conv_ln_b"]), tc)
    g["conv_ln_g"], g["conv_ln_b"], g["b_conv_dw"] = dlg.reshape(-1), dlb.reshape(-1), dbdw.reshape(-1)
    d_conv, dw_dw, dbin = _conv_bwd_dw(d_u1, u0, z, row2(sp["b_conv_in"]), w_dw32, tc)
    g["w_conv_dw"], g["b_conv_in"] = dw_dw[:CONV_WIDTH], dbin.reshape(-1)
    pieces = [(d_conv, w_in_pad[:, :1024]), (d_cq, w_in_pad[:, 1024:1280]), (d_ckv, w_in_pad[:, 1280:1408]), (d_kr, w_in_pad[:, 1408:])]
    dw_in = [_dw(h1, d, "dw_in_%d" % k) for k, (d, _) in enumerate(pieces)]
    g["w_in"] = jnp.concatenate([dw_in[0], dw_in[1], dw_in[2], dw_in[3][:, NOPE:HEAD_DIM]], axis=1)
    grad_x, dg = _linear_normbwd(pieces, x, 0, row2(sp["mix_norm_g"]), d_x1f, [F32], tm, "in_bwd")
    g["mix_norm_g"] = dg.reshape(-1)
    return lsum[0, 0], grad_x, g


def kernel(x, mem, positions, mix_norm_g, w_in, b_conv_in, w_conv_dw, b_conv_dw, conv_ln_g, conv_ln_b, q_lat_norm_g, w_uq, kv_lat_norm_g, w_ukv, q_norm_g, k_norm_g, w_out, mem_norm_x_g, mem_norm_m_g, w_mem_q, w_mem_kv, mem_q_norm_g, mem_k_norm_g, w_mem_o, ffn_norm_g, w_up, w_ffn_dw, b_ffn_dw, w_down, loss_target, m_mix_norm_g, m_w_in, m_b_conv_in, m_w_conv_dw, m_b_conv_dw, m_conv_ln_g, m_conv_ln_b, m_q_lat_norm_g, m_w_uq, m_kv_lat_norm_g, m_w_ukv, m_q_norm_g, m_k_norm_g, m_w_out, m_mem_norm_x_g, m_mem_norm_m_g, m_w_mem_q, m_w_mem_kv, m_mem_q_norm_g, m_mem_k_norm_g, m_w_mem_o, m_ffn_norm_g, m_w_up, m_w_ffn_dw, m_b_ffn_dw, m_w_down, v_mix_norm_g, v_w_in, v_b_conv_in, v_w_conv_dw, v_b_conv_dw, v_conv_ln_g, v_conv_ln_b, v_q_lat_norm_g, v_w_uq, v_kv_lat_norm_g, v_w_ukv, v_q_norm_g, v_k_norm_g, v_w_out, v_mem_norm_x_g, v_mem_norm_m_g, v_w_mem_q, v_w_mem_kv, v_mem_q_norm_g, v_mem_k_norm_g, v_w_mem_o, v_ffn_norm_g, v_w_up, v_w_ffn_dw, v_b_ffn_dw, v_w_down):
    names = ["mix_norm_g", "w_in", "b_conv_in", "w_conv_dw", "b_conv_dw", "conv_ln_g", "conv_ln_b", "q_lat_norm_g", "w_uq",
             "kv_lat_norm_g", "w_ukv", "q_norm_g", "k_norm_g", "w_out", "mem_norm_x_g", "mem_norm_m_g", "w_mem_q", "w_mem_kv",
             "mem_q_norm_g", "mem_k_norm_g", "w_mem_o", "ffn_norm_g", "w_up", "w_ffn_dw", "b_ffn_dw", "w_down"]
    loc = locals()
    w = {n: loc[n] for n in names}
    m = {n: loc["m_" + n] for n in names}
    v = {n: loc["v_" + n] for n in names}
    shard_idx = 2 * lax.axis_index("x") + lax.axis_index("y")

    w_packed = _pack_big_shards({n: w[n][0] for n, _, _ in BIG})
    w_wire = w_packed.astype(BF16)
    gathered = lax.dynamic_update_index_in_dim(_ag_weights(w_wire), w_wire, shard_idx, 0)
    wf = _unpack_gathered(gathered)

    small_sh_full = {}
    gather_in = []
    for n, (r, c) in SMALL_SH:
        csh = c // 4
        slab = lax.dynamic_update_slice(jnp.zeros((r, c), F32), w[n][0], (0, shard_idx * csh))
        gather_in.append(slab.reshape(-1))
    gather_rows = 256
    gathered_small = _allreduce_small_named(_pack_rows(gather_in, gather_rows, SMALL_COLS), "gather_small") * 0.5
    off = 0
    for n, (r, c) in SMALL_SH:
        small_sh_full[n] = gathered_small.reshape(-1)[off:off + r * c].reshape(r, c)
        off += r * c
    sp = {n: w[n][0] for n, _ in SMALL_REP}
    sp.update(small_sh_full)

    lsum, grad_x, g = _layer_grads(x[0], mem[0], positions[0], loss_target[0], wf, sp)

    small_parts = [jnp.full((SMALL_COLS,), lsum, F32)] + [g[n] for n, _ in SMALL_REP] + [g[n] for n, _ in SMALL_SH]
    small_rows = 368
    small_sum = _allreduce_small_named(_pack_rows(small_parts, small_rows, SMALL_COLS), "allreduce_small").reshape(-1)
    loss = small_sum[0] * (0.5 / D_MODEL)
    gs = {}
    off = SMALL_COLS
    for n, sz in SMALL_REP:
        gs[n] = small_sum[off:off + sz].reshape(w[n].shape)
        off += sz
    for n, (r, c) in SMALL_SH:
        full = small_sum[off:off + r * c].reshape(r, c)
        gs[n] = lax.dynamic_slice(full, (0, shard_idx * (c // 4)), (r, c // 4)).reshape(w[n].shape)
        off += r * c

    gfull = _pack_full_grads(g)
    core_idx = lax.axis_index("c").astype(jnp.int32)
    chipsum, chipsum_wire = _rs_add_pair(gfull, _rs_swap_halves(gfull), core_idx.reshape(1), 240)
    red = _rs_add_chips(chipsum, _rs_to_owner(chipsum_wire), jnp.stack([shard_idx.astype(jnp.int32), core_idx]), 240)
    g_packed = _rs_join_halves(red).reshape(PACK_ROWS, PACK_COLS)
    gs.update({n: a.reshape(w[n].shape) for n, a in _unpack_big_shards(g_packed).items()})

    m_packed = _pack_big_shards({n: m[n][0] for n, _, _ in BIG})
    v_packed = _pack_big_shards({n: v[n][0] for n, _, _ in BIG})
    d_p, m_p, v_p = _adamw(w_packed, g_packed, m_packed, v_packed, 256, "adamw_big")
    delta, new_m, new_v = {}, {}, {}
    for dst, src in ((delta, d_p), (new_m, m_p), (new_v, v_p)):
        dst.update({n: a.reshape(w[n].shape) for n, a in _unpack_big_shards(src).items()})
    small_names = [n for n, _ in SMALL_REP] + [n for n, _ in SMALL_SH]
    adam_rows = 176
    pk = lambda d: _pack_rows([d[n].reshape(-1) for n in small_names], adam_rows, SMALL_COLS)
    d_s, m_s, v_s = _adamw(pk(w), pk(gs), pk(m), pk(v), adam_rows, "adamw_small")
    for dst, src in ((delta, d_s), (new_m, m_s), (new_v, v_s)):
        off = 0
        flat = src.reshape(-1)
        for n in small_names:
            sz = int(np.prod(w[n].shape))
            dst[n] = flat[off:off + sz].reshape(w[n].shape)
            off += sz

    return (loss, grad_x[None], *[gs[n] for n in names], *[delta[n] for n in names], *[new_m[n] for n in names],
            *[new_v[n] for n in names])


def _allreduce_small_named(v, name):
    rows, cols = v.shape

    def body(v_ref, out_ref, buf, send_sems, recv_sems):
        x, y, c = _coords()
        me = 4 * x + 2 * y + c
        buf[me] = v_ref[...]
        cps = []
        for r in range(1, 8):
            dx, dy, dc = (r >> 2) & 1, (r >> 1) & 1, r & 1
            to = (x + dx - 2 * x * dx, y + dy - 2 * y * dy, c + dc - 2 * c * dc)
            cp = pltpu.make_async_remote_copy(src_ref=v_ref, dst_ref=buf.at[me], send_sem=send_sems.at[r - 1],
                                              recv_sem=recv_sems.at[r - 1], device_id=to, device_id_type=MESH)
            cp.start()
            cps.append(cp)
        for cp in cps:
            cp.wait()
        acc = buf[0]
        for d in range(1, 8):
            acc = acc + buf[d]
        out_ref[...] = acc

    vm = pl.BlockSpec(memory_space=pltpu.VMEM)
    return pl.pallas_call(
        body, in_specs=[vm], out_specs=vm, out_shape=SDS((rows, cols), F32),
        scratch_shapes=[pltpu.VMEM((8, rows, cols), F32), pltpu.SemaphoreType.DMA((7,)), pltpu.SemaphoreType.DMA((7,))],
        name=name)(v)
```

```python
import math

import numpy as np
import jax
import jax.numpy as jnp
from jax import lax
from jax.experimental import pallas as pl
from jax.experimental.pallas import tpu as pltpu

F32 = jnp.float32
BF16 = jnp.bfloat16
SDS = jax.ShapeDtypeStruct
MESH = pl.DeviceIdType.MESH

D_MODEL = 1024
EPS = 1e-6
CONV_CH = 512
CONV_WIDTH = 31
CONV_HALO = 32
HEADS = 8
NOPE = 64
ROPE = 32
HEAD_DIM = NOPE + ROPE
HEAD_PAD = 128
Q_RANK = 256
KV_RANK = 128
CHUNK = 64
ROPE_THETA = 10000.0
IN_COLS_PAD = 1536
MEM_HEADS = 4
MEM_HEAD_DIM = 256
MEM_LEN = 256
D_FF = 2816
FFN_HALO = 8
ATT_SCALE = 1.0 / math.sqrt(HEAD_DIM)
LOG2E = math.log2(math.e)
LN2 = math.log(2.0)

ADAM_LR = 0.001
ADAM_B1 = 0.9
ADAM_B2 = 0.999
ADAM_EPS = 1e-08
ADAM_WD = 0.01
ADAM_STEP = 10

VMEM_LIMIT_V7X = 56 * 1024 * 1024
PACK_COLS = 1024
PACK_ROWS = 3840
SMALL_COLS = 128


def _cp(n_axes):
    return pltpu.CompilerParams(dimension_semantics=("arbitrary",) * n_axes, vmem_limit_bytes=VMEM_LIMIT_V7X)


def _row_tile(s, want):
    return want if s % want == 0 else s


def _norm_linear(x, xcol, kdim, g, w, out_dtype, tm, tn, name):
    s = x.shape[0]
    n = w.shape[1]

    def body(x_ref, g_ref, w_ref, y_ref, hn_ref):
        @pl.when(pl.program_id(1) == 0)
        def _():
            xv = x_ref[...]
            r = lax.rsqrt(jnp.mean(xv * xv, axis=-1, keepdims=True) + EPS)
            hn_ref[...] = ((xv * r) * g_ref[...]).astype(BF16)

        y_ref[...] = jnp.dot(hn_ref[...], w_ref[...], preferred_element_type=F32).astype(y_ref.dtype)

    return pl.pallas_call(
        body, grid=(s // tm, n // tn),
        in_specs=[pl.BlockSpec((tm, kdim), lambda i, j: (i, xcol)), pl.BlockSpec((1, kdim), lambda i, j: (0, 0)),
                  pl.BlockSpec((kdim, tn), lambda i, j: (0, j))],
        out_specs=[pl.BlockSpec((tm, tn), lambda i, j: (i, j)), pl.BlockSpec((tm, kdim), lambda i, j: (i, 0))],
        out_shape=[SDS((s, n), out_dtype), SDS((s, kdim), BF16)],
        compiler_params=_cp(2), name=name)(x, g, w)


def _linear(pairs, nt, residual, out_dtypes, tm, tn, name):
    s = pairs[0][0].shape[0]
    n = pairs[0][1].shape[0] if nt else pairs[0][1].shape[1]
    n_pairs = len(pairs)
    has_res = residual is not None

    def body(*refs):
        a_refs = refs[:n_pairs]
        w_refs = refs[n_pairs:2 * n_pairs]
        res_ref = refs[2 * n_pairs] if has_res else None
        outs = refs[2 * n_pairs + int(has_res):]
        acc = None
        for a_ref, w_ref in zip(a_refs, w_refs):
            a = a_ref[...].astype(BF16)
            if nt:
                d = lax.dot_general(a, w_ref[...], (((1,), (1,)), ((), ())), preferred_element_type=F32)
            else:
                d = jnp.dot(a, w_ref[...], preferred_element_type=F32)
            acc = d if acc is None else acc + d
        if has_res:
            acc = res_ref[...] + acc
        for o in outs:
            o[...] = acc.astype(o.dtype)

    in_specs = [pl.BlockSpec((tm, a.shape[1]), lambda i, j: (i, 0)) for a, _ in pairs]
    if nt:
        in_specs += [pl.BlockSpec((tn, w.shape[1]), lambda i, j: (j, 0)) for _, w in pairs]
    else:
        in_specs += [pl.BlockSpec((w.shape[0], tn), lambda i, j: (0, j)) for _, w in pairs]
    args = [a for a, _ in pairs] + [w for _, w in pairs]
    if has_res:
        in_specs.append(pl.BlockSpec((tm, tn), lambda i, j: (i, j)))
        args.append(residual)
    outs = pl.pallas_call(
        body, grid=(s // tm, n // tn), in_specs=in_specs,
        out_specs=[pl.BlockSpec((tm, tn), lambda i, j: (i, j)) for _ in out_dtypes],
        out_shape=[SDS((s, n), dt) for dt in out_dtypes],
        compiler_params=_cp(2), name=name)(*args)
    return outs


def _linear_normbwd(pairs, x, xcol, g, d_res, out_dtypes, tm, name):
    s = pairs[0][0].shape[0]
    dn = pairs[0][1].shape[0]
    n_pairs = len(pairs)
    has_res = d_res is not None

    def body(*refs):
        a_refs = refs[:n_pairs]
        w_refs = refs[n_pairs:2 * n_pairs]
        x_ref, g_ref = refs[2 * n_pairs], refs[2 * n_pairs + 1]
        k = 2 * n_pairs + 2
        res_ref = refs[k] if has_res else None
        k += int(has_res)
        outs = refs[k:-1]
        dg_ref = refs[-1]
        dh = None
        for a_ref, w_ref in zip(a_refs, w_refs):
            d = lax.dot_general(a_ref[...].astype(BF16), w_ref[...], (((1,), (1,)), ((), ())), preferred_element_type=F32)
            dh = d if dh is None else dh + d
        xv = x_ref[...]
        r = lax.rsqrt(jnp.mean(xv * xv, axis=-1, keepdims=True) + EPS)
        y = xv * r

        @pl.when(pl.program_id(0) == 0)
        def _():
            dg_ref[...] = jnp.zeros_like(dg_ref)

        dg_ref[...] += jnp.sum(dh * y, axis=0, keepdims=True)
        dy = dh * g_ref[...]
        dx = r * (dy - y * jnp.mean(dy * y, axis=-1, keepdims=True))
        if has_res:
            dx = res_ref[...] + dx
        for o in outs:
            o[...] = dx.astype(o.dtype)

    in_specs = [pl.BlockSpec((tm, a.shape[1]), lambda i: (i, 0)) for a, _ in pairs]
    in_specs += [pl.BlockSpec((dn, w.shape[1]), lambda i: (0, 0)) for _, w in pairs]
    in_specs += [pl.BlockSpec((tm, dn), lambda i: (i, xcol)), pl.BlockSpec((1, dn), lambda i: (0, 0))]
    args = [a for a, _ in pairs] + [w for _, w in pairs] + [x, g]
    if has_res:
        in_specs.append(pl.BlockSpec((tm, dn), lambda i: (i, 0)))
        args.append(d_res)
    outs = pl.pallas_call(
        body, grid=(s // tm,), in_specs=in_specs,
        out_specs=[pl.BlockSpec((tm, dn), lambda i: (i, 0)) for _ in out_dtypes] + [pl.BlockSpec((1, dn), lambda i: (0, 0))],
        out_shape=[SDS((s, dn), dt) for dt in out_dtypes] + [SDS((1, dn), F32)],
        compiler_params=_cp(1), name=name)(*args)
    return outs


def _dw_matmul(a, b, tk, tn, ts, name):
    s, ka = a.shape
    n = b.shape[1]

    def body(a_ref, b_ref, o_ref):
        @pl.when(pl.program_id(2) == 0)
        def _():
            o_ref[...] = jnp.zeros_like(o_ref)

        o_ref[...] += lax.dot_general(a_ref[...].astype(BF16), b_ref[...].astype(BF16), (((0,), (0,)), ((), ())),
                                      preferred_element_type=F32)

    return pl.pallas_call(
        body, grid=(ka // tk, n // tn, s // ts),
        in_specs=[pl.BlockSpec((ts, tk), lambda k, j, t: (t, k)), pl.BlockSpec((ts, tn), lambda k, j, t: (t, j))],
        out_specs=pl.BlockSpec((tk, tn), lambda k, j, t: (k, j)),
        out_shape=SDS((ka, n), F32), compiler_params=_cp(3), name=name)(a, b)


def _dw(a, b, name):
    s, ka = a.shape
    n = b.shape[1]
    tk = ka if ka <= 1024 else ka // 2
    tn = n if n <= 1024 else (n // 2 if n == D_FF else 512)
    return _dw_matmul(a, b, tk, tn, _row_tile(s, 512), name)


def _prev_halo(tm, halo):
    return lambda i: (jnp.maximum(i * (tm // halo) - 1, 0), 0)


def _next_halo(tm, halo, s):
    return lambda i: (jnp.minimum((i + 1) * (tm // halo), s // halo - 1), 0)


def _conv_fwd(z, b_in, w32, b_dw, ln_g, ln_b, tm):
    s = z.shape[0]
    c = CONV_CH

    def body(z_ref, zh_ref, bin_ref, w_ref, bdw_ref, lg_ref, lb_ref, u_ref, u0_ref, u1_ref, ext):
        i = pl.program_id(0)

        def glu(zz):
            zz = zz + bin_ref[...]
            return zz[:, :c] * jax.nn.sigmoid(zz[:, c:])

        u0 = glu(z_ref[...])
        u0_ref[...] = u0
        ext[0:CONV_HALO, :] = jnp.where(i > 0, glu(zh_ref[...]), 0.0)
        ext[CONV_HALO:, :] = u0
        off = CONV_HALO - (CONV_WIDTH - 1)
        for r in range(tm // 64):
            for cb in range(c // 128):
                cs = slice(cb * 128, (cb + 1) * 128)
                acc = jnp.zeros((64, 128), F32)
                for k in range(CONV_WIDTH):
                    acc = acc + ext[r * 64 + off + k: r * 64 + off + k + 64, cs] * w_ref[k:k + 1, cs]
                u1_ref[r * 64:(r + 1) * 64, cs] = acc + bdw_ref[:, cs]
        u1 = u1_ref[...]
        mu = jnp.mean(u1, axis=-1, keepdims=True)
        xc = u1 - mu
        y = xc * lax.rsqrt(jnp.mean(xc * xc, axis=-1, keepdims=True) + EPS)
        y = y * lg_ref[...] + lb_ref[...]
        u_ref[...] = (y * jax.nn.sigmoid(y)).astype(BF16)

    row = lambda i: (i, 0)
    fix = lambda i: (0, 0)
    return pl.pallas_call(
        body, grid=(s // tm,),
        in_specs=[pl.BlockSpec((tm, 2 * c), row), pl.BlockSpec((CONV_HALO, 2 * c), _prev_halo(tm, CONV_HALO)),
                  pl.BlockSpec((1, 2 * c), fix), pl.BlockSpec((32, c), fix), pl.BlockSpec((1, c), fix),
                  pl.BlockSpec((1, c), fix), pl.BlockSpec((1, c), fix)],
        out_specs=[pl.BlockSpec((tm, c), row)] * 3,
        out_shape=[SDS((s, c), BF16), SDS((s, c), F32), SDS((s, c), F32)],
        scratch_shapes=[pltpu.VMEM((tm + CONV_HALO, c), F32)],
        compiler_params=_cp(1), name="conv_fwd")(z, z, b_in, w32, b_dw, ln_g, ln_b)


def _conv_bwd_ln(d_u, u1, ln_g, ln_b, tm):
    s = d_u.shape[0]
    c = CONV_CH

    def body(du_ref, u1_ref, lg_ref, lb_ref, du1_ref, dlg_ref, dlb_ref, dbdw_ref):
        @pl.when(pl.program_id(0) == 0)
        def _():
            dlg_ref[...] = jnp.zeros_like(dlg_ref)
            dlb_ref[...] = jnp.zeros_like(dlb_ref)
            dbdw_ref[...] = jnp.zeros_like(dbdw_ref)

        u1 = u1_ref[...]
        mu = jnp.mean(u1, axis=-1, keepdims=True)
        xc = u1 - mu
        rs = lax.rsqrt(jnp.mean(xc * xc, axis=-1, keepdims=True) + EPS)
        xh = xc * rs
        y = xh * lg_ref[...] + lb_ref[...]
        sg = jax.nn.sigmoid(y)
        dy = du_ref[...] * (sg * (1.0 + y * (1.0 - sg)))
        dlg_ref[...] += jnp.sum(dy * xh, axis=0, keepdims=True)
        dlb_ref[...] += jnp.sum(dy, axis=0, keepdims=True)
        dxh = dy * lg_ref[...]
        du1 = rs * (dxh - jnp.mean(dxh, axis=-1, keepdims=True) - xh * jnp.mean(dxh * xh, axis=-1, keepdims=True))
        dbdw_ref[...] += jnp.sum(du1, axis=0, keepdims=True)
        du1_ref[...] = du1

    row = lambda i: (i, 0)
    fix = lambda i: (0, 0)
    return pl.pallas_call(
        body, grid=(s // tm,),
        in_specs=[pl.BlockSpec((tm, c), row), pl.BlockSpec((tm, c), row), pl.BlockSpec((1, c), fix), pl.BlockSpec((1, c), fix)],
        out_specs=[pl.BlockSpec((tm, c), row)] + [pl.BlockSpec((1, c), fix)] * 3,
        out_shape=[SDS((s, c), F32)] + [SDS((1, c), F32)] * 3,
        compiler_params=_cp(1), name="conv_bwd_ln")(d_u, u1, ln_g, ln_b)


def _conv_bwd_dw(d_u1, u0, z, b_in, w32, tm):
    s = d_u1.shape[0]
    c = CONV_CH

    def body(d_ref, dn_ref, u0_ref, u0p_ref, z_ref, bin_ref, w_ref, dz_ref, dw_ref, dbin_ref, extd, extu, du0):
        i = pl.program_id(0)
        last = pl.num_programs(0) - 1

        @pl.when(i == 0)
        def _():
            dw_ref[...] = jnp.zeros_like(dw_ref)
            dbin_ref[...] = jnp.zeros_like(dbin_ref)

        extd[0:tm, :] = d_ref[...]
        extd[tm:, :] = jnp.where(i < last, dn_ref[...], 0.0)
        extu[0:CONV_HALO, :] = jnp.where(i > 0, u0p_ref[...], 0.0)
        extu[CONV_HALO:, :] = u0_ref[...]
        off = CONV_HALO - (CONV_WIDTH - 1)
        for r in range(tm // 64):
            for cb in range(c // 128):
                cs = slice(cb * 128, (cb + 1) * 128)
                acc = jnp.zeros((64, 128), F32)
                for k in range(CONV_WIDTH):
                    o = r * 64 + (CONV_WIDTH - 1) - k
                    acc = acc + extd[o:o + 64, cs] * w_ref[k:k + 1, cs]
                du0[r * 64:(r + 1) * 64, cs] = acc
        for cb in range(c // 128):
            cs = slice(cb * 128, (cb + 1) * 128)
            for k in range(CONV_WIDTH):
                part = jnp.zeros((8, 128), F32)
                for r in range(tm // 64):
                    p = d_ref[r * 64:(r + 1) * 64, cs] * extu[r * 64 + off + k: r * 64 + off + k + 64, cs]
                    for q in range(8):
                        part = part + p[q * 8:(q + 1) * 8, :]
                dw_ref[k:k + 1, cs] += jnp.sum(part, axis=0, keepdims=True)
        zz = z_ref[...] + bin_ref[...]
        a = zz[:, :c]
        sg = jax.nn.sigmoid(zz[:, c:])
        d0 = du0[...]
        da = d0 * sg
        dgt = d0 * a * (sg * (1.0 - sg))
        dbin_ref[:, :c] += jnp.sum(da, axis=0, keepdims=True)
        dbin_ref[:, c:] += jnp.sum(dgt, axis=0, keepdims=True)
        dz_ref[:, :c] = da.astype(BF16)
        dz_ref[:, c:] = dgt.astype(BF16)

    row = lambda i: (i, 0)
    fix = lambda i: (0, 0)
    return pl.pallas_call(
        body, grid=(s // tm,),
        in_specs=[pl.BlockSpec((tm, c), row), pl.BlockSpec((CONV_HALO, c), _next_halo(tm, CONV_HALO, s)),
                  pl.BlockSpec((tm, c), row), pl.BlockSpec((CONV_HALO, c), _prev_halo(tm, CONV_HALO)),
                  pl.BlockSpec((tm, 2 * c), row), pl.BlockSpec((1, 2 * c), fix), pl.BlockSpec((32, c), fix)],
        out_specs=[pl.BlockSpec((tm, 2 * c), row), pl.BlockSpec((32, c), fix), pl.BlockSpec((1, 2 * c), fix)],
        out_shape=[SDS((s, 2 * c), BF16), SDS((32, c), F32), SDS((1, 2 * c), F32)],
        scratch_shapes=[pltpu.VMEM((tm + CONV_HALO, c), F32), pltpu.VMEM((tm + CONV_HALO, c), F32), pltpu.VMEM((tm, c), F32)],
        compiler_params=_cp(1), name="conv_bwd_dw")(d_u1, d_u1, u0, u0, z, b_in, w32)


def _partner(v, lane):
    up = pltpu.roll(v, HEAD_PAD - ROPE // 2, 1)
    dn = pltpu.roll(v, ROPE // 2, 1)
    lo = (lane >= NOPE) & (lane < NOPE + ROPE // 2)
    hi = (lane >= NOPE + ROPE // 2) & (lane < HEAD_DIM)
    return jnp.where(lo, up, jnp.where(hi, dn, 0.0))


def _mla_prep(q_raw, kv_raw, z, cosf, sinf, gq, gk, tm):
    s = q_raw.shape[0]

    def body(q_ref, kv_ref, kr_ref, c_ref, s_ref, gq_ref, gk_ref, qo_ref, ko_ref, vo_ref):
        lane = lax.broadcasted_iota(jnp.int32, (tm, HEAD_PAD), 1)
        cf = c_ref[...]
        sf = s_ref[...]

        def norm_rope(t, g_ref):
            r = lax.rsqrt(jnp.sum(t * t, axis=-1, keepdims=True) * (1.0 / HEAD_DIM) + EPS)
            tn = (t * r) * g_ref[...]
            return tn * cf + _partner(tn, lane) * sf

        q = q_ref[...]
        qo_ref[...] = (norm_rope(q, gq_ref) * (ATT_SCALE * LOG2E)).astype(BF16)
        kv = kv_ref[...]
        kpre = jnp.where(lane < NOPE, kv, 0.0) + kr_ref[...]
        ko_ref[...] = norm_rope(kpre, gk_ref).astype(BF16)
        vo_ref[...] = jnp.where(lane >= NOPE, kv, 0.0).astype(BF16)

    hb = lambda i, h: (i, h)
    return pl.pallas_call(
        body, grid=(s // tm, HEADS),
        in_specs=[pl.BlockSpec((tm, HEAD_PAD), hb), pl.BlockSpec((tm, HEAD_PAD), hb),
                  pl.BlockSpec((tm, HEAD_PAD), lambda i, h: (i, IN_COLS_PAD // HEAD_PAD - 1)),
                  pl.BlockSpec((tm, HEAD_PAD), lambda i, h: (i, 0)), pl.BlockSpec((tm, HEAD_PAD), lambda i, h: (i, 0)),
                  pl.BlockSpec((1, HEAD_PAD), lambda i, h: (0, 0)), pl.BlockSpec((1, HEAD_PAD), lambda i, h: (0, 0))],
        out_specs=[pl.BlockSpec((tm, HEAD_PAD), hb)] * 3,
        out_shape=[SDS((s, HEADS * HEAD_PAD), BF16)] * 3,
        compiler_params=_cp(2), name="mla_prep")(q_raw, kv_raw, z, cosf, sinf, gq, gk)


def _mla_prep_bwd(dqp, dkp, dvp, q_raw, kv_raw, z, cosf, sinf, gq, gk, tm):
    s = q_raw.shape[0]

    def body(dq_ref, dk_ref, dv_ref, q_ref, kv_ref, kr_ref, c_ref, s_ref, gq_ref, gk_ref,
             dqo_ref, dkvo_ref, dkr_ref, dgq_ref, dgk_ref):
        i = pl.program_id(0)
        h = pl.program_id(1)
        lane = lax.broadcasted_iota(jnp.int32, (tm, HEAD_PAD), 1)
        cf = c_ref[...]
        sf = s_ref[...]

        @pl.when((i == 0) & (h == 0))
        def _():
            dgq_ref[...] = jnp.zeros_like(dgq_ref)
            dgk_ref[...] = jnp.zeros_like(dgk_ref)

        def norm_rope_bwd(t, d_out, g_ref, dg_ref):
            r = lax.rsqrt(jnp.sum(t * t, axis=-1, keepdims=True) * (1.0 / HEAD_DIM) + EPS)
            th = t * r
            dn = d_out * cf + _partner(d_out * sf, lane)
            dg_ref[...] += jnp.sum(dn * th, axis=0, keepdims=True)
            dh = dn * g_ref[...]
            return r * (dh - th * (jnp.sum(dh * th, axis=-1, keepdims=True) * (1.0 / HEAD_DIM)))

        dq = norm_rope_bwd(q_ref[...], dq_ref[...] * ATT_SCALE, gq_ref, dgq_ref)
        dqo_ref[...] = dq.astype(BF16)
        kv = kv_ref[...]
        kpre = jnp.where(lane < NOPE, kv, 0.0) + kr_ref[...]
        dkpre = norm_rope_bwd(kpre, dk_ref[...] * LN2, gk_ref, dgk_ref)
        dkvo_ref[...] = jnp.where(lane < NOPE, dkpre, dv_ref[...]).astype(BF16)
        dkr = jnp.where((lane >= NOPE) & (lane < HEAD_DIM), dkpre, 0.0)

        @pl.when(h == 0)
        def _():
            dkr_ref[...] = dkr

        @pl.when(h > 0)
        def _():
            dkr_ref[...] += dkr

    hb = lambda i, h: (i, h)
    r0 = lambda i, h: (i, 0)
    fix = lambda i, h: (0, 0)
    blk = pl.BlockSpec((tm, HEAD_PAD), hb)
    return pl.pallas_call(
        body, grid=(s // tm, HEADS),
        in_specs=[blk, blk, blk, blk, blk, pl.BlockSpec((tm, HEAD_PAD), lambda i, h: (i, IN_COLS_PAD // HEAD_PAD - 1)),
                  pl.BlockSpec((tm, HEAD_PAD), r0), pl.BlockSpec((tm, HEAD_PAD), r0),
                  pl.BlockSpec((1, HEAD_PAD), fix), pl.BlockSpec((1, HEAD_PAD), fix)],
        out_specs=[blk, blk, pl.BlockSpec((tm, HEAD_PAD), r0), pl.BlockSpec((1, HEAD_PAD), fix), pl.BlockSpec((1, HEAD_PAD), fix)],
        out_shape=[SDS((s, HEADS * HEAD_PAD), BF16), SDS((s, HEADS * HEAD_PAD), BF16), SDS((s, HEAD_PAD), F32),
                   SDS((1, HEAD_PAD), F32), SDS((1, HEAD_PAD), F32)],
        compiler_params=_cp(2), name="mla_prep_bwd")(dqp, dkp, dvp, q_raw, kv_raw, z, cosf, sinf, gq, gk)


def _tri_pairs(n, row_major):
    if row_major:
        pairs = [(i, j) for i in range(n) for j in range(i + 1)]
    else:
        pairs = [(i, j) for j in range(n) for i in range(j, n)]
    ii = np.array([p[0] for p in pairs], np.int32)
    jj = np.array([p[1] for p in pairs], np.int32)
    return jnp.asarray(ii), jnp.asarray(jj)


STRIP = 64


def _strip_mask(r, tb):
    col = lax.broadcasted_iota(jnp.int32, (STRIP, tb), 1)
    return col >= r * CHUNK


def _fold8(x):
    acc = x[0:8, :]
    for g in range(1, STRIP // 8):
        acc = acc + x[g * 8:(g + 1) * 8, :]
    return acc


def _attn_fwd(qp, kp, vp, tb):
    s = qp.shape[0]
    nb = s // tb
    ii, jj = _tri_pairs(nb, True)
    n_strips = tb // STRIP

    def body(ii_ref, jj_ref, q_ref, k_ref, v_ref, of_ref, ob_ref, lse_ref, m_sc, l_sc, acc_sc, st_sc, pt_sc):
        t = pl.program_id(1)
        i = ii_ref[t]
        j = jj_ref[t]

        @pl.when(j == 0)
        def _():
            m_sc[...] = jnp.full_like(m_sc, -jnp.inf)
            l_sc[...] = jnp.zeros_like(l_sc)
            acc_sc[...] = jnp.zeros_like(acc_sc)

        def step(diag):
            st_sc[...] = lax.dot_general(k_ref[...], q_ref[...], (((1,), (1,)), ((), ())), preferred_element_type=F32)
            mx = None
            for r in range(n_strips):
                sc = st_sc[r * STRIP:(r + 1) * STRIP, :]
                if diag and r > 0:
                    sc = jnp.where(_strip_mask(r, tb), sc, -jnp.inf)
                m8 = sc[0:8, :]
                for g in range(1, STRIP // 8):
                    m8 = jnp.maximum(m8, sc[g * 8:(g + 1) * 8, :])
                mx = m8 if mx is None else jnp.maximum(mx, m8)
            m_old = m_sc[0:1, :]
            m_new = jnp.maximum(m_old, jnp.max(mx, axis=0, keepdims=True))
            alpha = jnp.exp2(m_old - m_new)
            ps = None
            for r in range(n_strips):
                p = jnp.exp2(st_sc[r * STRIP:(r + 1) * STRIP, :] - m_new)
                if diag and r > 0:
                    p = jnp.where(_strip_mask(r, tb), p, 0.0)
                p8 = _fold8(p)
                ps = p8 if ps is None else ps + p8
                pt_sc[r * STRIP:(r + 1) * STRIP, :] = p.astype(BF16)
            l_new = alpha * l_sc[0:1, :] + jnp.sum(ps, axis=0, keepdims=True)
            m_sc[...] = jnp.broadcast_to(m_new, m_sc.shape)
            l_sc[...] = jnp.broadcast_to(l_new, l_sc.shape)
            pv = lax.dot_general(v_ref[...], pt_sc[...], (((0,), (0,)), ((), ())), preferred_element_type=F32)
            acc_sc[...] = alpha * acc_sc[...] + pv

        @pl.when(j < i)
        def _():
            step(False)

        @pl.when(j == i)
        def _():
            step(True)
            l = l_sc[0:1, :]
            o = (acc_sc[...] / l).T
            of_ref[...] = o
            ob_ref[...] = o.astype(BF16)
            lse_ref[...] = m_sc[0:1, :] + jnp.log(l) * LOG2E

    qmap = lambda h, t, ii_ref, jj_ref: (ii_ref[t], h)
    kmap = lambda h, t, ii_ref, jj_ref: (jj_ref[t], h)
    blk = (tb, HEAD_PAD)
    gs = pltpu.PrefetchScalarGridSpec(
        num_scalar_prefetch=2, grid=(HEADS, int(ii.shape[0])),
        in_specs=[pl.BlockSpec(blk, qmap), pl.BlockSpec(blk, kmap), pl.BlockSpec(blk, kmap)],
        out_specs=[pl.BlockSpec(blk, qmap), pl.BlockSpec(blk, qmap),
                   pl.BlockSpec((None, 1, tb), lambda h, t, ii_ref, jj_ref: (h, 0, ii_ref[t]))],
        scratch_shapes=[pltpu.VMEM((8, tb), F32), pltpu.VMEM((8, tb), F32), pltpu.VMEM((HEAD_PAD, tb), F32),
                        pltpu.VMEM((tb, tb), F32), pltpu.VMEM((tb, tb), BF16)])
    w = HEADS * HEAD_PAD
    return pl.pallas_call(body, grid_spec=gs, out_shape=[SDS((s, w), F32), SDS((s, w), BF16), SDS((HEADS, 1, s), F32)],
                          compiler_params=_cp(2), name="attn_fwd")(ii, jj, qp, kp, vp)


def _attn_delta(do, o, tb):
    s = do.shape[0]

    def body(do_ref, o_ref, d_ref):
        d_ref[...] = jnp.sum((do_ref[...] * o_ref[...]).T, axis=0, keepdims=True)

    blk = pl.BlockSpec((tb, HEAD_PAD), lambda h, i: (i, h))
    return pl.pallas_call(body, grid=(HEADS, s // tb), in_specs=[blk, blk],
                          out_specs=pl.BlockSpec((None, 1, tb), lambda h, i: (h, 0, i)),
                          out_shape=SDS((HEADS, 1, s), F32), compiler_params=_cp(2), name="attn_delta")(do, o)


def _attn_bwd(qp, kp, vp, dob, lse, delta, tb):
    s = qp.shape[0]
    nb = s // tb
    ii, jj = _tri_pairs(nb, False)

    def body(ii_ref, jj_ref, q_ref, k_ref, v_ref, do_ref, lse_ref, dl_ref, dq_ref, dk_ref, dv_ref, st_sc, dpt_sc, pt_sc, dst_sc):
        t = pl.program_id(1)
        i = ii_ref[t]
        j = jj_ref[t]

        @pl.when(t == 0)
        def _():
            dq_ref[...] = jnp.zeros_like(dq_ref)

        @pl.when(i == j)
        def _():
            dk_ref[...] = jnp.zeros_like(dk_ref)
            dv_ref[...] = jnp.zeros_like(dv_ref)

        def step(diag):
            q = q_ref[...]
            k = k_ref[...]
            do = do_ref[...]
            st_sc[...] = lax.dot_general(k, q, (((1,), (1,)), ((), ())), preferred_element_type=F32)
            dpt_sc[...] = lax.dot_general(v_ref[...], do, (((1,), (1,)), ((), ())), preferred_element_type=F32)
            lse = lse_ref[...]
            dl = dl_ref[...]
            for r in range(tb // STRIP):
                rows = slice(r * STRIP, (r + 1) * STRIP)
                p = jnp.exp2(st_sc[rows, :] - lse)
                if diag and r > 0:
                    p = jnp.where(_strip_mask(r, tb), p, 0.0)
                ds = p * (dpt_sc[rows, :] - dl)
                pt_sc[rows, :] = p.astype(BF16)
                dst_sc[rows, :] = ds.astype(BF16)
            dv_ref[...] += jnp.dot(pt_sc[...], do, preferred_element_type=F32)
            dst = dst_sc[...]
            dk_ref[...] += jnp.dot(dst, q, preferred_element_type=F32)
            dq_ref[i] += lax.dot_general(k, dst, (((0,), (0,)), ((), ())), preferred_element_type=F32)

        @pl.when(j < i)
        def _():
            step(False)

        @pl.when(j == i)
        def _():
            step(True)

    qmap = lambda h, t, ii_ref, jj_ref: (ii_ref[t], h)
    kmap = lambda h, t, ii_ref, jj_ref: (jj_ref[t], h)
    rowmap = lambda h, t, ii_ref, jj_ref: (h, 0, ii_ref[t])
    blk = (tb, HEAD_PAD)
    gs = pltpu.PrefetchScalarGridSpec(
        num_scalar_prefetch=2, grid=(HEADS, int(ii.shape[0])),
        in_specs=[pl.BlockSpec(blk, qmap), pl.BlockSpec(blk, kmap), pl.BlockSpec(blk, kmap), pl.BlockSpec(blk, qmap),
                  pl.BlockSpec((None, 1, tb), rowmap), pl.BlockSpec((None, 1, tb), rowmap)],
        out_specs=[pl.BlockSpec((None, nb, HEAD_PAD, tb), lambda h, t, ii_ref, jj_ref: (h, 0, 0, 0)),
                   pl.BlockSpec(blk, kmap), pl.BlockSpec(blk, kmap)],
        scratch_shapes=[pltpu.VMEM((tb, tb), F32), pltpu.VMEM((tb, tb), F32), pltpu.VMEM((tb, tb), BF16), pltpu.VMEM((tb, tb), BF16)])
    w = HEADS * HEAD_PAD
    dqt, dk, dv = pl.pallas_call(
        body, grid_spec=gs, out_shape=[SDS((HEADS, nb, HEAD_PAD, tb), F32), SDS((s, w), F32), SDS((s, w), F32)],
        compiler_params=_cp(2), name="attn_bwd")(ii, jj, qp, kp, vp, dob, lse, delta)
    return jnp.transpose(dqt, (1, 3, 0, 2)).reshape(s, w), dk, dv


def _head_norm(t, g):
    r = lax.rsqrt(jnp.mean(t * t, axis=-1, keepdims=True) + EPS)
    th = t * r
    return r, th, th * g


def _softmax_rows(sc):
    m = jnp.max(sc, axis=-1, keepdims=True)
    e = jnp.exp(sc - m)
    return e / jnp.sum(e, axis=-1, keepdims=True)


def _memattn_fwd(qm, kvm, gq, gk, tm):
    s = qm.shape[0]
    hd = MEM_HEAD_DIM

    def body(q_ref, k_ref, v_ref, gq_ref, gk_ref, o_ref):
        _, _, qn = _head_norm(q_ref[...], gq_ref[...])
        _, _, kn = _head_norm(k_ref[...], gk_ref[...])
        sc = lax.dot_general(qn.astype(BF16), kn.astype(BF16), (((1,), (1,)), ((), ())), preferred_element_type=F32)
        p = _softmax_rows(sc * (1.0 / math.sqrt(hd)))
        o_ref[...] = jnp.dot(p.astype(BF16), v_ref[...].astype(BF16), preferred_element_type=F32).astype(BF16)

    fix = lambda i, h: (0, 0)
    return pl.pallas_call(
        body, grid=(s // tm, MEM_HEADS),
        in_specs=[pl.BlockSpec((tm, hd), lambda i, h: (i, h)), pl.BlockSpec((MEM_LEN, hd), lambda i, h: (0, h)),
                  pl.BlockSpec((MEM_LEN, hd), lambda i, h: (0, MEM_HEADS + h)), pl.BlockSpec((1, hd), fix), pl.BlockSpec((1, hd), fix)],
        out_specs=pl.BlockSpec((tm, hd), lambda i, h: (i, h)),
        out_shape=SDS((s, MEM_HEADS * hd), BF16), compiler_params=_cp(2), name="memattn_fwd")(qm, kvm, kvm, gq, gk)


def _memattn_bwd(qm, kvm, d_o, gq, gk, tm):
    s = qm.shape[0]
    hd = MEM_HEAD_DIM

    def body(q_ref, k_ref, v_ref, do_ref, gq_ref, gk_ref, dq_ref, dk_ref, dv_ref, dgq_ref, dgk_ref, dkn_sc):
        h = pl.program_id(0)
        i = pl.program_id(1)
        last = pl.num_programs(1) - 1

        @pl.when((h == 0) & (i == 0))
        def _():
            dgq_ref[...] = jnp.zeros_like(dgq_ref)
            dgk_ref[...] = jnp.zeros_like(dgk_ref)

        @pl.when(i == 0)
        def _():
            dv_ref[...] = jnp.zeros_like(dv_ref)
            dkn_sc[...] = jnp.zeros_like(dkn_sc)

        rq, qh, qn = _head_norm(q_ref[...], gq_ref[...])
        rk, kh, kn = _head_norm(k_ref[...], gk_ref[...])
        qnb = qn.astype(BF16)
        knb = kn.astype(BF16)
        scale = 1.0 / math.sqrt(hd)
        sc = lax.dot_general(qnb, knb, (((1,), (1,)), ((), ())), preferred_element_type=F32)
        p = _softmax_rows(sc * scale)
        do = do_ref[...].astype(BF16)
        dp = lax.dot_general(do, v_ref[...].astype(BF16), (((1,), (1,)), ((), ())), preferred_element_type=F32)
        dv_ref[...] += lax.dot_general(p.astype(BF16), do, (((0,), (0,)), ((), ())), preferred_element_type=F32)
        ds = ((p * (dp - jnp.sum(dp * p, axis=-1, keepdims=True))) * scale).astype(BF16)
        dqn = jnp.dot(ds, knb, preferred_element_type=F32)
        dkn_sc[...] += lax.dot_general(ds, qnb, (((0,), (0,)), ((), ())), preferred_element_type=F32)
        dgq_ref[...] += jnp.sum(dqn * qh, axis=0, keepdims=True)
        dqh = dqn * gq_ref[...]
        dq_ref[...] = (rq * (dqh - qh * jnp.mean(dqh * qh, axis=-1, keepdims=True))).astype(BF16)

        @pl.when(i == last)
        def _():
            dkn = dkn_sc[...]
            dgk_ref[...] += jnp.sum(dkn * kh, axis=0, keepdims=True)
            dkh = dkn * gk_ref[...]
            dk_ref[...] = rk * (dkh - kh * jnp.mean(dkh * kh, axis=-1, keepdims=True))

    fix = lambda h, i: (0, 0)
    qb = pl.BlockSpec((tm, hd), lambda h, i: (i, h))
    kb = pl.BlockSpec((MEM_LEN, hd), lambda h, i: (0, h))
    return pl.pallas_call(
        body, grid=(MEM_HEADS, s // tm),
        in_specs=[qb, kb, pl.BlockSpec((MEM_LEN, hd), lambda h, i: (0, MEM_HEADS + h)), qb,
                  pl.BlockSpec((1, hd), fix), pl.BlockSpec((1, hd), fix)],
        out_specs=[qb, kb, kb, pl.BlockSpec((1, hd), fix), pl.BlockSpec((1, hd), fix)],
        out_shape=[SDS((s, MEM_HEADS * hd), BF16), SDS((MEM_LEN, MEM_HEADS * hd), F32), SDS((MEM_LEN, MEM_HEADS * hd), F32),
                   SDS((1, hd), F32), SDS((1, hd), F32)],
        scratch_shapes=[pltpu.VMEM((MEM_LEN, hd), F32)],
        compiler_params=_cp(2), name="memattn_bwd")(qm, kvm, kvm, d_o, gq, gk)


def _ffn_specs(tm, tn, nbj, s, order_ji):
    if order_ji:
        ij = lambda f: (lambda j, i: f(i, j))
    else:
        ij = lambda f: f
    prev = lambda i: jnp.maximum(i * (tm // FFN_HALO) - 1, 0)
    cur_g = pl.BlockSpec((tm, tn), ij(lambda i, j: (i, j)))
    cur_v = pl.BlockSpec((tm, tn), ij(lambda i, j: (i, j + nbj)))
    halo_g = pl.BlockSpec((FFN_HALO, tn), ij(lambda i, j: (prev(i), j)))
    halo_v = pl.BlockSpec((FFN_HALO, tn), ij(lambda i, j: (prev(i), j + nbj)))
    w_g = pl.BlockSpec((8, tn), ij(lambda i, j: (0, j)))
    w_v = pl.BlockSpec((8, tn), ij(lambda i, j: (0, j + nbj)))
    b_g = pl.BlockSpec((1, tn), ij(lambda i, j: (0, j)))
    b_v = pl.BlockSpec((1, tn), ij(lambda i, j: (0, j + nbj)))
    return cur_g, cur_v, halo_g, halo_v, w_g, w_v, b_g, b_v


def _conv3(cur_ref, halo_ref, w_ref, b_ref, ext, first, tm):
    ext[0:FFN_HALO, :] = jnp.where(first, 0.0, halo_ref[...])
    ext[FFN_HALO:, :] = cur_ref[...]
    return (w_ref[0:1, :] * ext[FFN_HALO - 2:FFN_HALO - 2 + tm, :] + w_ref[1:2, :] * ext[FFN_HALO - 1:FFN_HALO - 1 + tm, :]
            + w_ref[2:3, :] * ext[FFN_HALO:FFN_HALO + tm, :] + b_ref[...])


def _ffn_fwd(up0, w8, b, tm, tn):
    s = up0.shape[0]
    nbj = D_FF // tn

    def body(g_ref, v_ref, gh_ref, vh_ref, wg_ref, wv_ref, bg_ref, bv_ref, act_ref, ext):
        first = pl.program_id(0) == 0
        ug = _conv3(g_ref, gh_ref, wg_ref, bg_ref, ext, first, tm)
        uv = _conv3(v_ref, vh_ref, wv_ref, bv_ref, ext, first, tm)
        act_ref[...] = ((ug * jax.nn.sigmoid(ug)) * uv).astype(BF16)

    specs = _ffn_specs(tm, tn, nbj, s, False)
    return pl.pallas_call(
        body, grid=(s // tm, nbj), in_specs=list(specs),
        out_specs=pl.BlockSpec((tm, tn), lambda i, j: (i, j)), out_shape=SDS((s, D_FF), BF16),
        scratch_shapes=[pltpu.VMEM((tm + FFN_HALO, tn), F32)],
        compiler_params=_cp(2), name="ffn_fwd")(up0, up0, up0, up0, w8, w8, b, b)


def _ffn_bwd_act(d_act, up0, w8, b, tm, tn):
    s = up0.shape[0]
    nbj = D_FF // tn

    def body(da_ref, g_ref, v_ref, gh_ref, vh_ref, wg_ref, wv_ref, bg_ref, bv_ref, dg_ref, dv_ref, dbg_ref, dbv_ref, ext):
        i = pl.program_id(1)
        first = i == 0

        @pl.when(first)
        def _():
            dbg_ref[...] = jnp.zeros_like(dbg_ref)
            dbv_ref[...] = jnp.zeros_like(dbv_ref)

        ug = _conv3(g_ref, gh_ref, wg_ref, bg_ref, ext, first, tm)
        uv = _conv3(v_ref, vh_ref, wv_ref, bv_ref, ext, first, tm)
        sg = jax.nn.sigmoid(ug)
        da = da_ref[...]
        dval = da * (ug * sg)
        dgate = da * uv * (sg * (1.0 + ug * (1.0 - sg)))
        dg_ref[...] = dgate
        dv_ref[...] = dval
        dbg_ref[...] += jnp.sum(dgate, axis=0, keepdims=True)
        dbv_ref[...] += jnp.sum(dval, axis=0, keepdims=True)

    specs = _ffn_specs(tm, tn, nbj, s, True)
    cur = pl.BlockSpec((tm, tn), lambda j, i: (i, j))
    acc = pl.BlockSpec((1, tn), lambda j, i: (0, j))
    return pl.pallas_call(
        body, grid=(nbj, s // tm), in_specs=[cur] + list(specs),
        out_specs=[cur, cur, acc, acc],
        out_shape=[SDS((s, D_FF), F32), SDS((s, D_FF), F32), SDS((1, D_FF), F32), SDS((1, D_FF), F32)],
        scratch_shapes=[pltpu.VMEM((tm + FFN_HALO, tn), F32)],
        compiler_params=_cp(2), name="ffn_bwd_act")(d_act, up0, up0, up0, up0, w8, w8, b, b)


def _ffn_bwd_conv(d_upg, d_upv, up0, w8, tm, tn):
    s = up0.shape[0]
    nbj = D_FF // tn

    def body(dg_ref, dv_ref, dgn_ref, dvn_ref, g_ref, v_ref, gh_ref, vh_ref, wg_ref, wv_ref,
             og_ref, ov_ref, dwg_ref, dwv_ref, extd, extu):
        i = pl.program_id(1)
        first = i == 0
        last = i == pl.num_programs(1) - 1

        @pl.when(first)
        def _():
            dwg_ref[...] = jnp.zeros_like(dwg_ref)
            dwv_ref[...] = jnp.zeros_like(dwv_ref)

        def half(d_ref, dn_ref, u_ref, uh_ref, w_ref, o_ref, dw_ref):
            extd[0:tm, :] = d_ref[...]
            extd[tm:, :] = jnp.where(last, 0.0, dn_ref[...])
            o_ref[...] = (w_ref[2:3, :] * extd[0:tm, :] + w_ref[1:2, :] * extd[1:1 + tm, :]
                          + w_ref[0:1, :] * extd[2:2 + tm, :]).astype(BF16)
            extu[0:FFN_HALO, :] = jnp.where(first, 0.0, uh_ref[...])
            extu[FFN_HALO:, :] = u_ref[...]
            d = d_ref[...]
            for k in range(3):
                o = FFN_HALO - 2 + k
                dw_ref[k:k + 1, :] += jnp.sum(d * extu[o:o + tm, :], axis=0, keepdims=True)

        half(dg_ref, dgn_ref, g_ref, gh_ref, wg_ref, og_ref, dwg_ref)
        half(dv_ref, dvn_ref, v_ref, vh_ref, wv_ref, ov_ref, dwv_ref)

    cur_g, cur_v, halo_g, halo_v, w_g, w_v, _, _ = _ffn_specs(tm, tn, nbj, s, True)
    cur = pl.BlockSpec((tm, tn), lambda j, i: (i, j))
    nxt = pl.BlockSpec((FFN_HALO, tn), lambda j, i: (jnp.minimum((i + 1) * (tm // FFN_HALO), s // FFN_HALO - 1), j))
    acc = pl.BlockSpec((8, tn), lambda j, i: (0, j))
    return pl.pallas_call(
        body, grid=(nbj, s // tm), in_specs=[cur, cur, nxt, nxt, cur_g, cur_v, halo_g, halo_v, w_g, w_v],
        out_specs=[cur, cur, acc, acc],
        out_shape=[SDS((s, D_FF), BF16), SDS((s, D_FF), BF16), SDS((8, D_FF), F32), SDS((8, D_FF), F32)],
        scratch_shapes=[pltpu.VMEM((tm + FFN_HALO, tn), F32), pltpu.VMEM((tm + FFN_HALO, tn), F32)],
        compiler_params=_cp(2), name="ffn_bwd_conv")(d_upg, d_upv, d_upg, d_upv, up0, up0, up0, up0, w8, w8)


def _down_loss(act, w_down, x2, target, tm):
    s = act.shape[0]

    def body(a_ref, w_ref, x_ref, t_ref, dyf_ref, dyb_ref, ls_ref):
        @pl.when(pl.program_id(0) == 0)
        def _():
            ls_ref[...] = jnp.zeros_like(ls_ref)

        y = x_ref[...] + jnp.dot(a_ref[...], w_ref[...], preferred_element_type=F32)
        e = y - t_ref[...]
        ls_ref[...] += jnp.sum(e * e)
        dy = e * (1.0 / D_MODEL)
        dyf_ref[...] = dy
        dyb_ref[...] = dy.astype(BF16)

    row = lambda i: (i, 0)
    return pl.pallas_call(
        body, grid=(s // tm,),
        in_specs=[pl.BlockSpec((tm, D_FF), row), pl.BlockSpec((D_FF, D_MODEL), lambda i: (0, 0)),
                  pl.BlockSpec((tm, D_MODEL), row), pl.BlockSpec((tm, D_MODEL), row)],
        out_specs=[pl.BlockSpec((tm, D_MODEL), row), pl.BlockSpec((tm, D_MODEL), row), pl.BlockSpec((8, 128), lambda i: (0, 0))],
        out_shape=[SDS((s, D_MODEL), F32), SDS((s, D_MODEL), BF16), SDS((8, 128), F32)],
        compiler_params=_cp(1), name="down_loss")(act, w_down, x2, target)


def _adamw_math(w, g, m, v):
    mn = ADAM_B1 * m + (1.0 - ADAM_B1) * g
    vn = ADAM_B2 * v + (1.0 - ADAM_B2) * (g * g)
    m_hat = mn / (1.0 - ADAM_B1 ** ADAM_STEP)
    v_hat = vn / (1.0 - ADAM_B2 ** ADAM_STEP)
    return -ADAM_LR * (m_hat / (jnp.sqrt(v_hat) + ADAM_EPS) + ADAM_WD * w), mn, vn


def _adamw(w, g, m, v, name):
    rows, cols = w.shape
    tr = rows if rows <= 256 else (256 if rows % 256 == 0 else rows // 2)

    def body(w_ref, g_ref, m_ref, v_ref, d_ref, mo_ref, vo_ref):
        d_ref[...], mo_ref[...], vo_ref[...] = _adamw_math(w_ref[...], g_ref[...], m_ref[...], v_ref[...])

    blk = pl.BlockSpec((tr, cols), lambda i: (i, 0))
    return pl.pallas_call(body, grid=(rows // tr,), in_specs=[blk] * 4, out_specs=[blk] * 3,
                          out_shape=[SDS((rows, cols), F32)] * 3, compiler_params=_cp(1), name=name)(w, g, m, v)


def _adamw_small(ws, gs, ms, vs):
    n = len(ws)

    def body(*refs):
        ins, outs = refs[:4 * n], refs[4 * n:]
        for k in range(n):
            d, mn, vn = _adamw_math(ins[k][...], ins[n + k][...], ins[2 * n + k][...], ins[3 * n + k][...])
            outs[k][...] = d
            outs[n + k][...] = mn
            outs[2 * n + k][...] = vn

    vm = pl.BlockSpec(memory_space=pltpu.VMEM)
    outs = pl.pallas_call(body, in_specs=[vm] * (4 * n), out_specs=[vm] * (3 * n),
                          out_shape=[SDS(w.shape, F32) for w in ws] * 3, name="adamw_small")(*ws, *gs, *ms, *vs)
    return outs[:n], outs[n:2 * n], outs[2 * n:]


ANY = pl.BlockSpec(memory_space=pl.ANY)


def _coords():
    return lax.axis_index("x"), lax.axis_index("y"), lax.axis_index("c")


def _other_chips(x, y):
    return [(1 - x, y), (x, 1 - y), (1 - x, 1 - y)]


D2D_CHUNKS = 8
ICI_CHUNKS = 4


def _row_chunks(n_rows, n_chunks, align):
    step = -(-n_rows // (n_chunks * align)) * align
    return [(r, min(step, n_rows - r)) for r in range(0, n_rows, step)]


def _ag_weights(wsh):
    rows, cols = wsh.shape
    half_rows = rows // 2

    def body(w_ref, out_ref, send_sems, recv_sems):
        x, y, c = _coords()
        s_me = 2 * x + y
        chips = _other_chips(x, y)
        sibling = (x, y, 1 - c)
        my_base = c * half_rows
        sib_base = (1 - c) * half_rows

        def piece(base, r0, nr):
            return pl.ds(pl.multiple_of(base + r0, 16), nr)

        def copy(k, shard, rows_, to, src=None):
            dst = out_ref.at[shard, rows_]
            return pltpu.make_async_remote_copy(src_ref=dst if src is None else src, dst_ref=dst, send_sem=send_sems.at[k],
                                                recv_sem=recv_sems.at[k], device_id=to, device_id_type=MESH)

        for k, (px, py) in enumerate(chips):
            for r0, nr in _row_chunks(half_rows, ICI_CHUNKS, 16):
                copy(k, s_me, piece(my_base, r0, nr), (px, py, c), src=w_ref.at[piece(my_base, r0, nr)]).start()
        for k, (px, py) in enumerate(chips):
            copy(k, 2 * px + py, piece(my_base, 0, half_rows), (px, py, c)).wait_recv()
            for r0, nr in _row_chunks(half_rows, ICI_CHUNKS, 16):
                copy(3 + k, 2 * px + py, piece(my_base, r0, nr), sibling).start()
        for k, (px, py) in enumerate(chips):
            copy(3 + k, 2 * px + py, piece(sib_base, 0, half_rows), sibling).wait_recv()
        for k in range(6):
            copy(k, s_me, piece(my_base, 0, half_rows), sibling).wait_send()

    return pl.pallas_call(
        body, in_specs=[ANY], out_specs=ANY, out_shape=SDS((4, rows, cols), wsh.dtype),
        scratch_shapes=[pltpu.SemaphoreType.DMA((6,)), pltpu.SemaphoreType.DMA((6,))],
        name="ag_weights")(wsh)


def _rs_swap_halves(gfull):
    n_sh, rows, cols = gfull.shape
    half_rows = rows // 2

    def body(g_ref, recv_ref, send_sem, recv_sem):
        x, y, c = _coords()
        sib_base = (1 - c) * half_rows
        for sh in range(n_sh):
            for r0, nr in _row_chunks(half_rows, D2D_CHUNKS, 8):
                pltpu.make_async_remote_copy(
                    src_ref=g_ref.at[sh, pl.ds(pl.multiple_of(sib_base + r0, 8), nr)], dst_ref=recv_ref.at[sh, pl.ds(r0, nr)],
                    send_sem=send_sem, recv_sem=recv_sem, device_id=(x, y, 1 - c), device_id_type=MESH).start()
        pltpu.make_async_remote_copy(src_ref=recv_ref, dst_ref=recv_ref, send_sem=send_sem, recv_sem=recv_sem,
                                     device_id=(x, y, 1 - c), device_id_type=MESH).wait()

    return pl.pallas_call(
        body, in_specs=[ANY], out_specs=ANY, out_shape=SDS((n_sh, half_rows, cols), gfull.dtype),
        scratch_shapes=[pltpu.SemaphoreType.DMA, pltpu.SemaphoreType.DMA], name="rs_swap_halves")(gfull)


def _rs_add_pair(gfull, recv, core, tr):
    n_sh, rows, cols = gfull.shape
    half_rows = rows // 2
    nblk = half_rows // tr

    def body(c_ref, g_ref, r_ref, o_ref, ob_ref):
        acc = g_ref[...] + r_ref[...]
        o_ref[...] = acc
        ob_ref[...] = acc.astype(BF16)

    out = pl.BlockSpec((None, tr, cols), lambda sh, i, c_ref: (sh, i, 0))
    gs = pltpu.PrefetchScalarGridSpec(
        num_scalar_prefetch=1, grid=(n_sh, nblk),
        in_specs=[pl.BlockSpec((None, tr, cols), lambda sh, i, c_ref: (sh, c_ref[0] * nblk + i, 0)), out],
        out_specs=[out, out])
    return pl.pallas_call(body, grid_spec=gs, out_shape=[SDS((n_sh, half_rows, cols), F32), SDS((n_sh, half_rows, cols), BF16)],
                          compiler_params=_cp(2), name="rs_add_pair")(core, gfull, recv)


def _rs_to_owner(chipsum):
    n_sh, half_rows, cols = chipsum.shape

    def body(cs_ref, recv_ref, send_sems, recv_sems):
        x, y, c = _coords()
        chips = _other_chips(x, y)
        for k, (px, py) in enumerate(chips):
            for r0, nr in _row_chunks(half_rows, ICI_CHUNKS, 16):
                pltpu.make_async_remote_copy(
                    src_ref=cs_ref.at[2 * px + py, pl.ds(r0, nr)], dst_ref=recv_ref.at[k, pl.ds(r0, nr)],
                    send_sem=send_sems.at[k], recv_sem=recv_sems.at[k], device_id=(px, py, c), device_id_type=MESH).start()
        for k, (px, py) in enumerate(chips):
            pltpu.make_async_remote_copy(src_ref=recv_ref.at[k], dst_ref=recv_ref.at[k], send_sem=send_sems.at[k],
                                         recv_sem=recv_sems.at[k], device_id=(px, py, c), device_id_type=MESH).wait()

    return pl.pallas_call(
        body, in_specs=[ANY], out_specs=ANY, out_shape=SDS((3, half_rows, cols), chipsum.dtype),
        scratch_shapes=[pltpu.SemaphoreType.DMA((3,)), pltpu.SemaphoreType.DMA((3,))], name="rs_to_owner")(chipsum)


def _rs_add_chips(chipsum, recv, shard_core, tr):
    _, half_rows, cols = chipsum.shape

    def body(s_ref, m_ref, r0_ref, r1_ref, r2_ref, o_ref):
        o_ref[...] = ((m_ref[...] + r0_ref[...].astype(F32)) + r1_ref[...].astype(F32)) + r2_ref[...].astype(F32)

    gs = pltpu.PrefetchScalarGridSpec(
        num_scalar_prefetch=1, grid=(half_rows // tr,),
        in_specs=[pl.BlockSpec((None, tr, cols), lambda i, s_ref: (s_ref[0], i, 0))]
        + [pl.BlockSpec((None, tr, cols), (lambda k: lambda i, s_ref: (k, i, 0))(k)) for k in range(3)],
        out_specs=pl.BlockSpec((None, tr, cols), lambda i, s_ref: (s_ref[1], i, 0)))
    return pl.pallas_call(body, grid_spec=gs, out_shape=SDS((2, half_rows, cols), F32),
                          compiler_params=_cp(1), name="rs_add_chips")(shard_core, chipsum, recv, recv, recv)


def _rs_join_halves(buf):
    _, half_rows, cols = buf.shape

    def body(b_ref, out_ref, send_sem, recv_sem):
        x, y, c = _coords()
        for r0, nr in _row_chunks(half_rows, D2D_CHUNKS, 8):
            pltpu.make_async_remote_copy(src_ref=out_ref.at[c, pl.ds(r0, nr)], dst_ref=out_ref.at[c, pl.ds(r0, nr)], send_sem=send_sem,
                                         recv_sem=recv_sem, device_id=(x, y, 1 - c), device_id_type=MESH).start()
        pltpu.make_async_remote_copy(src_ref=out_ref.at[c], dst_ref=out_ref.at[c], send_sem=send_sem, recv_sem=recv_sem,
                                     device_id=(x, y, 1 - c), device_id_type=MESH).wait()

    return pl.pallas_call(
        body, in_specs=[ANY], out_specs=ANY, out_shape=SDS(buf.shape, buf.dtype), input_output_aliases={0: 0},
        scratch_shapes=[pltpu.SemaphoreType.DMA, pltpu.SemaphoreType.DMA], name="rs_join_halves")(buf)


BIG = [("w_in", (1024, 1440), 1), ("w_uq", (256, 768), 1), ("w_ukv", (128, 1024), 1), ("w_out", (1024, 1024), 0),
       ("w_mem_q", (1024, 1024), 0), ("w_mem_kv", (1024, 2048), 1), ("w_mem_o", (1024, 1024), 0),
       ("w_up", (1024, 5632), 1), ("w_down", (2816, 1024), 0)]
SMALL_REP = [("mix_norm_g", 1024), ("b_conv_in", 1024), ("b_conv_dw", 512), ("conv_ln_g", 512), ("conv_ln_b", 512),
             ("q_lat_norm_g", 256), ("kv_lat_norm_g", 128), ("q_norm_g", 96), ("k_norm_g", 96), ("mem_norm_x_g", 1024),
             ("mem_norm_m_g", 1024), ("mem_q_norm_g", 256), ("mem_k_norm_g", 256), ("ffn_norm_g", 1024), ("b_ffn_dw", 5632)]
SMALL_SH = [("w_conv_dw", (31, 512)), ("w_ffn_dw", (3, 5632))]


def _shard_shape(shape, axis):
    return tuple(d // 4 if a == axis else d for a, d in enumerate(shape))


def _pack_rows(parts, rows, cols):
    flat = jnp.concatenate([p.reshape(-1) for p in parts])
    flat = jnp.pad(flat, (0, rows * cols - flat.shape[0]))
    return flat.reshape(rows, cols)


def _pack_big_shards(ws):
    parts = [ws[n].reshape(-1, PACK_COLS) for n, _, _ in BIG]
    used = sum(p.shape[0] for p in parts)
    return jnp.concatenate(parts + [jnp.zeros((PACK_ROWS - used, PACK_COLS), parts[0].dtype)], axis=0)


def _unpack_big_shards(packed):
    out, r = {}, 0
    for n, shape, axis in BIG:
        sh = _shard_shape(shape, axis)
        nr = sh[0] * sh[1] // PACK_COLS
        out[n] = packed[r:r + nr].reshape(sh)
        r += nr
    return out


def _unpack_gathered(g):
    out, r = {}, 0
    for n, shape, axis in BIG:
        sh = _shard_shape(shape, axis)
        nr = sh[0] * sh[1] // PACK_COLS
        part = g[:, r:r + nr]
        if axis == 0:
            out[n] = part.reshape(shape)
        else:
            out[n] = part.reshape((4,) + sh).transpose(1, 0, 2).reshape(shape)
        r += nr
    return out


def _pack_full_grads(gs):
    parts = []
    for n, shape, axis in BIG:
        sh = _shard_shape(shape, axis)
        nr = sh[0] * sh[1] // PACK_COLS
        if axis == 0:
            parts.append(gs[n].reshape(4, nr, PACK_COLS))
        else:
            parts.append(gs[n].reshape(shape[0], 4, sh[1]).transpose(1, 0, 2).reshape(4, nr, PACK_COLS))
    used = sum(p.shape[1] for p in parts)
    return jnp.concatenate(parts + [jnp.zeros((4, PACK_ROWS - used, PACK_COLS), F32)], axis=1)


def _rope_tables(positions):
    inv_freq = ROPE_THETA ** (-jnp.arange(0, ROPE, 2, dtype=F32) / ROPE)
    ang = positions.astype(F32)[:, None] * inv_freq
    cos, sin = jnp.cos(ang), jnp.sin(ang)
    s = positions.shape[0]
    cosf = jnp.concatenate([jnp.ones((s, NOPE), F32), cos, cos, jnp.ones((s, HEAD_PAD - HEAD_DIM), F32)], axis=-1)
    sinf = jnp.concatenate([jnp.zeros((s, NOPE), F32), -sin, sin, jnp.zeros((s, HEAD_PAD - HEAD_DIM), F32)], axis=-1)
    return cosf, sinf


def _pad_heads(w, per_head):
    k = w.shape[0]
    w3 = w.reshape(k, HEADS, per_head)
    return jnp.pad(w3, ((0, 0), (0, 0), (0, HEAD_PAD - per_head))).reshape(k, HEADS * HEAD_PAD)


def _layer_grads(x, mem, positions, target, wf, sp):
    s = x.shape[0]
    tm = _row_tile(s, 512)
    tc = _row_tile(s, 256)
    tb = 512 if s % 512 == 0 and s > 512 else s // 2
    row2 = lambda a: a.reshape(1, -1)

    w_in = wf["w_in"]
    w_in_pad = jnp.concatenate([w_in[:, :1408], jnp.zeros((D_MODEL, NOPE), BF16), w_in[:, 1408:],
                                jnp.zeros((D_MODEL, HEAD_PAD - HEAD_DIM), BF16)], axis=1)
    w_uq_pad = _pad_heads(wf["w_uq"], HEAD_DIM)
    w_ukv = wf["w_ukv"]
    w_out_u = wf["w_out"][:CONV_CH]
    w_out_o = jnp.pad(wf["w_out"][CONV_CH:].reshape(HEADS, NOPE, D_MODEL), ((0, 0), (NOPE, 0), (0, 0))).reshape(HEADS * HEAD_PAD, D_MODEL)
    w_up_g, w_up_v = wf["w_up"][:, :D_FF], wf["w_up"][:, D_FF:]
    gq_pad = jnp.pad(sp["q_norm_g"], (0, HEAD_PAD - HEAD_DIM)).reshape(1, HEAD_PAD)
    gk_pad = jnp.pad(sp["k_norm_g"], (0, HEAD_PAD - HEAD_DIM)).reshape(1, HEAD_PAD)
    w_dw32 = jnp.pad(sp["w_conv_dw"], ((0, 1), (0, 0)))
    w_ffn8 = jnp.pad(sp["w_ffn_dw"], ((0, 5), (0, 0)))
    b_ffn = row2(sp["b_ffn_dw"])
    cosf, sinf = _rope_tables(positions)

    z, h1 = _norm_linear(x, 0, D_MODEL, row2(sp["mix_norm_g"]), w_in_pad, F32, tm, IN_COLS_PAD, "in_proj")
    u, u0, u1 = _conv_fwd(z, row2(sp["b_conv_in"]), w_dw32, row2(sp["b_conv_dw"]), row2(sp["conv_ln_g"]), row2(sp["conv_ln_b"]), tc)
    q_raw, cqn = _norm_linear(z, 1024 // Q_RANK, Q_RANK, row2(sp["q_lat_norm_g"]), w_uq_pad, F32, tm, 1024, "q_up")
    kv_raw, ckvn = _norm_linear(z, 1280 // KV_RANK, KV_RANK, row2(sp["kv_lat_norm_g"]), w_ukv, F32, tm, 1024, "kv_up")
    qp, kp, vp = _mla_prep(q_raw, kv_raw, z, cosf, sinf, gq_pad, gk_pad, tm)
    o_f, o_b, lse = _attn_fwd(qp, kp, vp, tb)
    (x1,) = _linear([(u, w_out_u), (o_b, w_out_o)], False, x, [F32], tm, 1024, "out_proj")

    qm, hq = _norm_linear(x1, 0, D_MODEL, row2(sp["mem_norm_x_g"]), wf["w_mem_q"], F32, tm, 1024, "memq_proj")
    kvm, hm = _norm_linear(mem, 0, D_MODEL, row2(sp["mem_norm_m_g"]), wf["w_mem_kv"], F32, MEM_LEN, 1024, "memkv_proj")
    gmq, gmk = row2(sp["mem_q_norm_g"]), row2(sp["mem_k_norm_g"])
    o_m = _memattn_fwd(qm, kvm, gmq, gmk, tm)
    (x2,) = _linear([(o_m, wf["w_mem_o"])], False, x1, [F32], tm, 1024, "memo_proj")

    up0, h3 = _norm_linear(x2, 0, D_MODEL, row2(sp["ffn_norm_g"]), wf["w_up"], F32, tm, D_FF // 2, "up_proj")
    act = _ffn_fwd(up0, w_ffn8, b_ffn, tc, D_FF // 2)
    dy_f, dy_b, lsum = _down_loss(act, wf["w_down"], x2, target, tm)

    g = {}
    (d_act,) = _linear([(dy_b, wf["w_down"])], True, None, [F32], tm, D_FF // 2, "down_bwd")
    g["w_down"] = _dw(act, dy_b, "dw_down")
    d_upg, d_upv, dbg, dbv = _ffn_bwd_act(d_act, up0, w_ffn8, b_ffn, tc, D_FF // 2)
    d_up0g, d_up0v, dwg, dwv = _ffn_bwd_conv(d_upg, d_upv, up0, w_ffn8, tc, D_FF // 2)
    g["b_ffn_dw"] = jnp.concatenate([dbg, dbv], axis=1).reshape(-1)
    g["w_ffn_dw"] = jnp.concatenate([dwg[:3], dwv[:3]], axis=1)
    g["w_up"] = jnp.concatenate([_dw(h3, d_up0g, "dw_up_g"), _dw(h3, d_up0v, "dw_up_v")], axis=1)
    d_x2f, d_x2b, dg = _linear_normbwd([(d_up0g, w_up_g), (d_up0v, w_up_v)], x2, 0, row2(sp["ffn_norm_g"]), dy_f,
                                       [F32, BF16], tc, "up_bwd")
    g["ffn_norm_g"] = dg.reshape(-1)

    (d_om,) = _linear([(d_x2b, wf["w_mem_o"])], True, None, [BF16], tm, 1024, "memo_bwd")
    g["w_mem_o"] = _dw(o_m, d_x2b, "dw_mem_o")
    d_qm, d_km, d_vm, dgq, dgk = _memattn_bwd(qm, kvm, d_om, gmq, gmk, tm)
    g["mem_q_norm_g"], g["mem_k_norm_g"] = dgq.reshape(-1), dgk.reshape(-1)
    d_kvm = jnp.concatenate([d_km, d_vm], axis=1)
    g["w_mem_q"] = _dw(hq, d_qm, "dw_mem_q")
    g["w_mem_kv"] = _dw(hm, d_kvm, "dw_mem_kv")
    d_x1f, d_x1b, dg = _linear_normbwd([(d_qm, wf["w_mem_q"])], x1, 0, row2(sp["mem_norm_x_g"]), d_x2f, [F32, BF16], tm, "memq_bwd")
    g["mem_norm_x_g"] = dg.reshape(-1)
    _, dg = _linear_normbwd([(d_kvm, wf["w_mem_kv"])], mem, 0, row2(sp["mem_norm_m_g"]), None, [BF16], MEM_LEN, "memkv_bwd")
    g["mem_norm_m_g"] = dg.reshape(-1)

    (d_u,) = _linear([(d_x1b, w_out_u)], True, None, [F32], tm, CONV_CH, "out_bwd_u")
    d_of, d_ob = _linear([(d_x1b, w_out_o)], True, None, [F32, BF16], tm, 1024, "out_bwd_o")
    dw_out_u = _dw(u, d_x1b, "dw_out_u")
    dw_out_o = _dw(o_b, d_x1b, "dw_out_o")
    g["w_out"] = jnp.concatenate([dw_out_u, dw_out_o.reshape(HEADS, HEAD_PAD, D_MODEL)[:, NOPE:].reshape(HEADS * NOPE, D_MODEL)], axis=0)
    delta = _attn_delta(d_of, o_f, tb)
    dqp, dkp, dvp = _attn_bwd(qp, kp, vp, d_ob, lse, delta, tb)
    d_qraw, d_kvraw, d_kr, dgq, dgk = _mla_prep_bwd(dqp, dkp, dvp, q_raw, kv_raw, z, cosf, sinf, gq_pad, gk_pad, tm)
    g["q_norm_g"], g["k_norm_g"] = dgq.reshape(-1)[:HEAD_DIM], dgk.reshape(-1)[:HEAD_DIM]
    g["w_uq"] = _dw(cqn, d_qraw, "dw_uq").reshape(Q_RANK, HEADS, HEAD_PAD)[:, :, :HEAD_DIM].reshape(Q_RANK, HEADS * HEAD_DIM)
    g["w_ukv"] = _dw(ckvn, d_kvraw, "dw_ukv")
    d_cq, dg = _linear_normbwd([(d_qraw, w_uq_pad)], z, 1024 // Q_RANK, row2(sp["q_lat_norm_g"]), None, [BF16], tm, "q_up_bwd")
    g["q_lat_norm_g"] = dg.reshape(-1)
    d_ckv, dg = _linear_normbwd([(d_kvraw, w_ukv)], z, 1280 // KV_RANK, row2(sp["kv_lat_norm_g"]), None, [BF16], tm, "kv_up_bwd")
    g["kv_lat_norm_g"] = dg.reshape(-1)
    d_u1, dlg, dlb, dbdw = _conv_bwd_ln(d_u, u1, row2(sp["conv_ln_g"]), row2(sp["conv_ln_b"]), tc)
    g["conv_ln_g"], g["conv_ln_b"], g["b_conv_dw"] = dlg.reshape(-1), dlb.reshape(-1), dbdw.reshape(-1)
    d_conv, dw_dw, dbin = _conv_bwd_dw(d_u1, u0, z, row2(sp["b_conv_in"]), w_dw32, tc)
    g["w_conv_dw"], g["b_conv_in"] = dw_dw[:CONV_WIDTH], dbin.reshape(-1)
    pieces = [(d_conv, w_in_pad[:, :1024]), (d_cq, w_in_pad[:, 1024:1280]), (d_ckv, w_in_pad[:, 1280:1408]), (d_kr, w_in_pad[:, 1408:])]
    dw_in = [_dw(h1, d, "dw_in_%d" % k) for k, (d, _) in enumerate(pieces)]
    g["w_in"] = jnp.concatenate([dw_in[0], dw_in[1], dw_in[2], dw_in[3][:, NOPE:HEAD_DIM]], axis=1)
    grad_x, dg = _linear_normbwd(pieces, x, 0, row2(sp["mix_norm_g"]), d_x1f, [F32], tm, "in_bwd")
    g["mix_norm_g"] = dg.reshape(-1)
    return lsum[0, 0], grad_x, g


def kernel(x, mem, positions, mix_norm_g, w_in, b_conv_in, w_conv_dw, b_conv_dw, conv_ln_g, conv_ln_b, q_lat_norm_g, w_uq, kv_lat_norm_g, w_ukv, q_norm_g, k_norm_g, w_out, mem_norm_x_g, mem_norm_m_g, w_mem_q, w_mem_kv, mem_q_norm_g, mem_k_norm_g, w_mem_o, ffn_norm_g, w_up, w_ffn_dw, b_ffn_dw, w_down, loss_target, m_mix_norm_g, m_w_in, m_b_conv_in, m_w_conv_dw, m_b_conv_dw, m_conv_ln_g, m_conv_ln_b, m_q_lat_norm_g, m_w_uq, m_kv_lat_norm_g, m_w_ukv, m_q_norm_g, m_k_norm_g, m_w_out, m_mem_norm_x_g, m_mem_norm_m_g, m_w_mem_q, m_w_mem_kv, m_mem_q_norm_g, m_mem_k_norm_g, m_w_mem_o, m_ffn_norm_g, m_w_up, m_w_ffn_dw, m_b_ffn_dw, m_w_down, v_mix_norm_g, v_w_in, v_b_conv_in, v_w_conv_dw, v_b_conv_dw, v_conv_ln_g, v_conv_ln_b, v_q_lat_norm_g, v_w_uq, v_kv_lat_norm_g, v_w_ukv, v_q_norm_g, v_k_norm_g, v_w_out, v_mem_norm_x_g, v_mem_norm_m_g, v_w_mem_q, v_w_mem_kv, v_mem_q_norm_g, v_mem_k_norm_g, v_w_mem_o, v_ffn_norm_g, v_w_up, v_w_ffn_dw, v_b_ffn_dw, v_w_down):
    names = ["mix_norm_g", "w_in", "b_conv_in", "w_conv_dw", "b_conv_dw", "conv_ln_g", "conv_ln_b", "q_lat_norm_g", "w_uq",
             "kv_lat_norm_g", "w_ukv", "q_norm_g", "k_norm_g", "w_out", "mem_norm_x_g", "mem_norm_m_g", "w_mem_q", "w_mem_kv",
             "mem_q_norm_g", "mem_k_norm_g", "w_mem_o", "ffn_norm_g", "w_up", "w_ffn_dw", "b_ffn_dw", "w_down"]
    loc = locals()
    w = {n: loc[n] for n in names}
    m = {n: loc["m_" + n] for n in names}
    v = {n: loc["v_" + n] for n in names}
    shard_idx = 2 * lax.axis_index("x") + lax.axis_index("y")

    w_packed = _pack_big_shards({n: w[n][0] for n, _, _ in BIG})
    w_wire = w_packed.astype(BF16)
    gathered = lax.dynamic_update_index_in_dim(_ag_weights(w_wire), w_wire, shard_idx, 0)
    wf = _unpack_gathered(gathered)

    small_sh_full = {}
    gather_in = []
    for n, (r, c) in SMALL_SH:
        csh = c // 4
        slab = lax.dynamic_update_slice(jnp.zeros((r, c), F32), w[n][0], (0, shard_idx * csh))
        gather_in.append(slab.reshape(-1))
    gather_rows = 256
    gathered_small = _allreduce_small_named(_pack_rows(gather_in, gather_rows, SMALL_COLS), "gather_small") * 0.5
    off = 0
    for n, (r, c) in SMALL_SH:
        small_sh_full[n] = gathered_small.reshape(-1)[off:off + r * c].reshape(r, c)
        off += r * c
    sp = {n: w[n][0] for n, _ in SMALL_REP}
    sp.update(small_sh_full)

    lsum, grad_x, g = _layer_grads(x[0], mem[0], positions[0], loss_target[0], wf, sp)

    small_parts = [jnp.full((SMALL_COLS,), lsum, F32)] + [g[n] for n, _ in SMALL_REP] + [g[n] for n, _ in SMALL_SH]
    small_rows = 368
    small_sum = _allreduce_small_named(_pack_rows(small_parts, small_rows, SMALL_COLS), "allreduce_small").reshape(-1)
    loss = small_sum[0] * (0.5 / D_MODEL)
    gs = {}
    off = SMALL_COLS
    for n, sz in SMALL_REP:
        gs[n] = small_sum[off:off + sz].reshape(w[n].shape)
        off += sz
    for n, (r, c) in SMALL_SH:
        full = small_sum[off:off + r * c].reshape(r, c)
        gs[n] = lax.dynamic_slice(full, (0, shard_idx * (c // 4)), (r, c // 4)).reshape(w[n].shape)
        off += r * c

    gfull = _pack_full_grads(g)
    core_idx = lax.axis_index("c").astype(jnp.int32)
    chipsum, chipsum_wire = _rs_add_pair(gfull, _rs_swap_halves(gfull), core_idx.reshape(1), 240)
    red = _rs_add_chips(chipsum, _rs_to_owner(chipsum_wire), jnp.stack([shard_idx.astype(jnp.int32), core_idx]), 240)
    g_packed = _rs_join_halves(red).reshape(PACK_ROWS, PACK_COLS)
    g_big = _unpack_big_shards(g_packed)
    gs.update({n: a[None] for n, a in g_big.items()})

    delta, new_m, new_v = {}, {}, {}
    for n, _, _ in BIG:
        d_n, m_n, v_n = _adamw(w[n][0], g_big[n], m[n][0], v[n][0], "adamw_" + n)
        delta[n], new_m[n], new_v[n] = d_n[None], m_n[None], v_n[None]
    small_names = [n for n, _ in SMALL_REP] + [n for n, _ in SMALL_SH]
    as2d = lambda a: a.reshape(-1, a.shape[-1])
    d_s, m_s, v_s = _adamw_small(*[[as2d(d[n]) for n in small_names] for d in (w, gs, m, v)])
    for k, n in enumerate(small_names):
        delta[n], new_m[n], new_v[n] = d_s[k].reshape(w[n].shape), m_s[k].reshape(w[n].shape), v_s[k].reshape(w[n].shape)

    return (loss, grad_x[None], *[gs[n] for n in names], *[delta[n] for n in names], *[new_m[n] for n in names],
            *[new_v[n] for n in names])


def _allreduce_small_named(v, name):
    rows, cols = v.shape

    def body(v_ref, out_ref, buf, send_sems, recv_sems):
        x, y, c = _coords()
        me = 4 * x + 2 * y + c
        buf[me] = v_ref[...]
        cps = []
        for r in range(1, 8):
            dx, dy, dc = (r >> 2) & 1, (r >> 1) & 1, r & 1
            to = (x + dx - 2 * x * dx, y + dy - 2 * y * dy, c + dc - 2 * c * dc)
            cp = pltpu.make_async_remote_copy(src_ref=v_ref, dst_ref=buf.at[me], send_sem=send_sems.at[r - 1],
                                              recv_sem=recv_sems.at[r - 1], device_id=to, device_id_type=MESH)
            cp.start()
            cps.append(cp)
        for cp in cps:
            cp.wait()
        acc = buf[0]
        for d in range(1, 8):
            acc = acc + buf[d]
        out_ref[...] = acc

    vm = pl.BlockSpec(memory_space=pltpu.VMEM)
    return pl.pallas_call(
        body, in_specs=[vm], out_specs=vm, out_shape=SDS((rows, cols), F32),
        scratch_shapes=[pltpu.VMEM((8, rows, cols), F32), pltpu.SemaphoreType.DMA((7,)), pltpu.SemaphoreType.DMA((7,))],
        name=name)(v)
```

```python
import math

import numpy as np
import jax
import jax.numpy as jnp
from jax import lax
from jax.experimental import pallas as pl
from jax.experimental.pallas import tpu as pltpu

F32 = jnp.float32
BF16 = jnp.bfloat16
SDS = jax.ShapeDtypeStruct
MESH = pl.DeviceIdType.MESH

D_MODEL = 1024
EPS = 1e-6
CONV_CH = 512
CONV_WIDTH = 31
CONV_HALO = 32
HEADS = 8
NOPE = 64
ROPE = 32
HEAD_DIM = NOPE + ROPE
HEAD_PAD = 128
Q_RANK = 256
KV_RANK = 128
CHUNK = 64
ROPE_THETA = 10000.0
IN_COLS_PAD = 1536
MEM_HEADS = 4
MEM_HEAD_DIM = 256
MEM_LEN = 256
D_FF = 2816
FFN_HALO = 8
ATT_SCALE = 1.0 / math.sqrt(HEAD_DIM)
LOG2E = math.log2(math.e)
LN2 = math.log(2.0)

ADAM_LR = 0.001
ADAM_B1 = 0.9
ADAM_B2 = 0.999
ADAM_EPS = 1e-08
ADAM_WD = 0.01
ADAM_STEP = 10

VMEM_LIMIT_V7X = 56 * 1024 * 1024
PACK_COLS = 1024
PACK_ROWS = 3840
SMALL_COLS = 128


def _cp(n_axes):
    return pltpu.CompilerParams(dimension_semantics=("arbitrary",) * n_axes, vmem_limit_bytes=VMEM_LIMIT_V7X)


def _row_tile(s, want):
    return want if s % want == 0 else s


def _norm_linear(x, xcol, kdim, g, w, out_dtype, tm, tn, name):
    s = x.shape[0]
    n = w.shape[1]

    def body(x_ref, g_ref, w_ref, y_ref, hn_ref):
        @pl.when(pl.program_id(1) == 0)
        def _():
            xv = x_ref[...]
            r = lax.rsqrt(jnp.mean(xv * xv, axis=-1, keepdims=True) + EPS)
            hn_ref[...] = ((xv * r) * g_ref[...]).astype(BF16)

        y_ref[...] = jnp.dot(hn_ref[...], w_ref[...], preferred_element_type=F32).astype(y_ref.dtype)

    return pl.pallas_call(
        body, grid=(s // tm, n // tn),
        in_specs=[pl.BlockSpec((tm, kdim), lambda i, j: (i, xcol)), pl.BlockSpec((1, kdim), lambda i, j: (0, 0)),
                  pl.BlockSpec((kdim, tn), lambda i, j: (0, j))],
        out_specs=[pl.BlockSpec((tm, tn), lambda i, j: (i, j)), pl.BlockSpec((tm, kdim), lambda i, j: (i, 0))],
        out_shape=[SDS((s, n), out_dtype), SDS((s, kdim), BF16)],
        compiler_params=_cp(2), name=name)(x, g, w)


def _linear(pairs, nt, residual, out_dtypes, tm, tn, name):
    s = pairs[0][0].shape[0]
    n = pairs[0][1].shape[0] if nt else pairs[0][1].shape[1]
    n_pairs = len(pairs)
    has_res = residual is not None

    def body(*refs):
        a_refs = refs[:n_pairs]
        w_refs = refs[n_pairs:2 * n_pairs]
        res_ref = refs[2 * n_pairs] if has_res else None
        outs = refs[2 * n_pairs + int(has_res):]
        acc = None
        for a_ref, w_ref in zip(a_refs, w_refs):
            a = a_ref[...].astype(BF16)
            if nt:
                d = lax.dot_general(a, w_ref[...], (((1,), (1,)), ((), ())), preferred_element_type=F32)
            else:
                d = jnp.dot(a, w_ref[...], preferred_element_type=F32)
            acc = d if acc is None else acc + d
        if has_res:
            acc = res_ref[...] + acc
        for o in outs:
            o[...] = acc.astype(o.dtype)

    in_specs = [pl.BlockSpec((tm, a.shape[1]), lambda i, j: (i, 0)) for a, _ in pairs]
    if nt:
        in_specs += [pl.BlockSpec((tn, w.shape[1]), lambda i, j: (j, 0)) for _, w in pairs]
    else:
        in_specs += [pl.BlockSpec((w.shape[0], tn), lambda i, j: (0, j)) for _, w in pairs]
    args = [a for a, _ in pairs] + [w for _, w in pairs]
    if has_res:
        in_specs.append(pl.BlockSpec((tm, tn), lambda i, j: (i, j)))
        args.append(residual)
    outs = pl.pallas_call(
        body, grid=(s // tm, n // tn), in_specs=in_specs,
        out_specs=[pl.BlockSpec((tm, tn), lambda i, j: (i, j)) for _ in out_dtypes],
        out_shape=[SDS((s, n), dt) for dt in out_dtypes],
        compiler_params=_cp(2), name=name)(*args)
    return outs


def _linear_normbwd(pairs, x, xcol, g, d_res, out_dtypes, tm, name):
    s = pairs[0][0].shape[0]
    dn = pairs[0][1].shape[0]
    n_pairs = len(pairs)
    has_res = d_res is not None

    def body(*refs):
        a_refs = refs[:n_pairs]
        w_refs = refs[n_pairs:2 * n_pairs]
        x_ref, g_ref = refs[2 * n_pairs], refs[2 * n_pairs + 1]
        k = 2 * n_pairs + 2
        res_ref = refs[k] if has_res else None
        k += int(has_res)
        outs = refs[k:-1]
        dg_ref = refs[-1]
        dh = None
        for a_ref, w_ref in zip(a_refs, w_refs):
            d = lax.dot_general(a_ref[...].astype(BF16), w_ref[...], (((1,), (1,)), ((), ())), preferred_element_type=F32)
            dh = d if dh is None else dh + d
        xv = x_ref[...]
        r = lax.rsqrt(jnp.mean(xv * xv, axis=-1, keepdims=True) + EPS)
        y = xv * r

        @pl.when(pl.program_id(0) == 0)
        def _():
            dg_ref[...] = jnp.zeros_like(dg_ref)

        dg_ref[...] += jnp.sum(dh * y, axis=0, keepdims=True)
        dy = dh * g_ref[...]
        dx = r * (dy - y * jnp.mean(dy * y, axis=-1, keepdims=True))
        if has_res:
            dx = res_ref[...] + dx
        for o in outs:
            o[...] = dx.astype(o.dtype)

    in_specs = [pl.BlockSpec((tm, a.shape[1]), lambda i: (i, 0)) for a, _ in pairs]
    in_specs += [pl.BlockSpec((dn, w.shape[1]), lambda i: (0, 0)) for _, w in pairs]
    in_specs += [pl.BlockSpec((tm, dn), lambda i: (i, xcol)), pl.BlockSpec((1, dn), lambda i: (0, 0))]
    args = [a for a, _ in pairs] + [w for _, w in pairs] + [x, g]
    if has_res:
        in_specs.append(pl.BlockSpec((tm, dn), lambda i: (i, 0)))
        args.append(d_res)
    outs = pl.pallas_call(
        body, grid=(s // tm,), in_specs=in_specs,
        out_specs=[pl.BlockSpec((tm, dn), lambda i: (i, 0)) for _ in out_dtypes] + [pl.BlockSpec((1, dn), lambda i: (0, 0))],
        out_shape=[SDS((s, dn), dt) for dt in out_dtypes] + [SDS((1, dn), F32)],
        compiler_params=_cp(1), name=name)(*args)
    return outs


def _dw_matmul(a, b, tk, tn, ts, name):
    s, ka = a.shape
    n = b.shape[1]

    def body(a_ref, b_ref, o_ref):
        @pl.when(pl.program_id(2) == 0)
        def _():
            o_ref[...] = jnp.zeros_like(o_ref)

        o_ref[...] += lax.dot_general(a_ref[...].astype(BF16), b_ref[...].astype(BF16), (((0,), (0,)), ((), ())),
                                      preferred_element_type=F32)

    return pl.pallas_call(
        body, grid=(ka // tk, n // tn, s // ts),
        in_specs=[pl.BlockSpec((ts, tk), lambda k, j, t: (t, k)), pl.BlockSpec((ts, tn), lambda k, j, t: (t, j))],
        out_specs=pl.BlockSpec((tk, tn), lambda k, j, t: (k, j)),
        out_shape=SDS((ka, n), F32), compiler_params=_cp(3), name=name)(a, b)


def _dw(a, b, name):
    s, ka = a.shape
    n = b.shape[1]
    tk = ka if ka <= 1024 else ka // 2
    tn = n if n <= 1024 else (n // 2 if n == D_FF else 512)
    return _dw_matmul(a, b, tk, tn, _row_tile(s, 2048), name)


def _prev_halo(tm, halo):
    return lambda i: (jnp.maximum(i * (tm // halo) - 1, 0), 0)


def _next_halo(tm, halo, s):
    return lambda i: (jnp.minimum((i + 1) * (tm // halo), s // halo - 1), 0)


def _conv_fwd(z, b_in, w32, b_dw, ln_g, ln_b, tm):
    s = z.shape[0]
    c = CONV_CH

    def body(z_ref, zh_ref, bin_ref, w_ref, bdw_ref, lg_ref, lb_ref, u_ref, u0_ref, u1_ref, ext):
        i = pl.program_id(0)

        def glu(zz):
            zz = zz + bin_ref[...]
            return zz[:, :c] * jax.nn.sigmoid(zz[:, c:])

        u0 = glu(z_ref[...])
        u0_ref[...] = u0
        ext[0:CONV_HALO, :] = jnp.where(i > 0, glu(zh_ref[...]), 0.0)
        ext[CONV_HALO:, :] = u0
        off = CONV_HALO - (CONV_WIDTH - 1)
        for r in range(tm // 64):
            for cb in range(c // 128):
                cs = slice(cb * 128, (cb + 1) * 128)
                acc = jnp.zeros((64, 128), F32)
                for k in range(CONV_WIDTH):
                    acc = acc + ext[r * 64 + off + k: r * 64 + off + k + 64, cs] * w_ref[k:k + 1, cs]
                u1_ref[r * 64:(r + 1) * 64, cs] = acc + bdw_ref[:, cs]
        u1 = u1_ref[...]
        mu = jnp.mean(u1, axis=-1, keepdims=True)
        xc = u1 - mu
        y = xc * lax.rsqrt(jnp.mean(xc * xc, axis=-1, keepdims=True) + EPS)
        y = y * lg_ref[...] + lb_ref[...]
        u_ref[...] = (y * jax.nn.sigmoid(y)).astype(BF16)

    row = lambda i: (i, 0)
    fix = lambda i: (0, 0)
    return pl.pallas_call(
        body, grid=(s // tm,),
        in_specs=[pl.BlockSpec((tm, 2 * c), row), pl.BlockSpec((CONV_HALO, 2 * c), _prev_halo(tm, CONV_HALO)),
                  pl.BlockSpec((1, 2 * c), fix), pl.BlockSpec((32, c), fix), pl.BlockSpec((1, c), fix),
                  pl.BlockSpec((1, c), fix), pl.BlockSpec((1, c), fix)],
        out_specs=[pl.BlockSpec((tm, c), row)] * 3,
        out_shape=[SDS((s, c), BF16), SDS((s, c), F32), SDS((s, c), F32)],
        scratch_shapes=[pltpu.VMEM((tm + CONV_HALO, c), F32)],
        compiler_params=_cp(1), name="conv_fwd")(z, z, b_in, w32, b_dw, ln_g, ln_b)


def _conv_bwd_ln(d_u, u1, ln_g, ln_b, tm):
    s = d_u.shape[0]
    c = CONV_CH

    def body(du_ref, u1_ref, lg_ref, lb_ref, du1_ref, dlg_ref, dlb_ref, dbdw_ref):
        @pl.when(pl.program_id(0) == 0)
        def _():
            dlg_ref[...] = jnp.zeros_like(dlg_ref)
            dlb_ref[...] = jnp.zeros_like(dlb_ref)
            dbdw_ref[...] = jnp.zeros_like(dbdw_ref)

        u1 = u1_ref[...]
        mu = jnp.mean(u1, axis=-1, keepdims=True)
        xc = u1 - mu
        rs = lax.rsqrt(jnp.mean(xc * xc, axis=-1, keepdims=True) + EPS)
        xh = xc * rs
        y = xh * lg_ref[...] + lb_ref[...]
        sg = jax.nn.sigmoid(y)
        dy = du_ref[...] * (sg * (1.0 + y * (1.0 - sg)))
        dlg_ref[...] += jnp.sum(dy * xh, axis=0, keepdims=True)
        dlb_ref[...] += jnp.sum(dy, axis=0, keepdims=True)
        dxh = dy * lg_ref[...]
        du1 = rs * (dxh - jnp.mean(dxh, axis=-1, keepdims=True) - xh * jnp.mean(dxh * xh, axis=-1, keepdims=True))
        dbdw_ref[...] += jnp.sum(du1, axis=0, keepdims=True)
        du1_ref[...] = du1

    row = lambda i: (i, 0)
    fix = lambda i: (0, 0)
    return pl.pallas_call(
        body, grid=(s // tm,),
        in_specs=[pl.BlockSpec((tm, c), row), pl.BlockSpec((tm, c), row), pl.BlockSpec((1, c), fix), pl.BlockSpec((1, c), fix)],
        out_specs=[pl.BlockSpec((tm, c), row)] + [pl.BlockSpec((1, c), fix)] * 3,
        out_shape=[SDS((s, c), F32)] + [SDS((1, c), F32)] * 3,
        compiler_params=_cp(1), name="conv_bwd_ln")(d_u, u1, ln_g, ln_b)


def _conv_bwd_dw(d_u1, u0, z, b_in, w32, tm):
    s = d_u1.shape[0]
    c = CONV_CH

    def body(d_ref, dn_ref, u0_ref, u0p_ref, z_ref, bin_ref, w_ref, dz_ref, dw_ref, dbin_ref, extd, extu, du0):
        i = pl.program_id(0)
        last = pl.num_programs(0) - 1

        @pl.when(i == 0)
        def _():
            dw_ref[...] = jnp.zeros_like(dw_ref)
            dbin_ref[...] = jnp.zeros_like(dbin_ref)

        extd[0:tm, :] = d_ref[...]
        extd[tm:, :] = jnp.where(i < last, dn_ref[...], 0.0)
        extu[0:CONV_HALO, :] = jnp.where(i > 0, u0p_ref[...], 0.0)
        extu[CONV_HALO:, :] = u0_ref[...]
        off = CONV_HALO - (CONV_WIDTH - 1)
        for r in range(tm // 64):
            for cb in range(c // 128):
                cs = slice(cb * 128, (cb + 1) * 128)
                acc = jnp.zeros((64, 128), F32)
                for k in range(CONV_WIDTH):
                    o = r * 64 + (CONV_WIDTH - 1) - k
                    acc = acc + extd[o:o + 64, cs] * w_ref[k:k + 1, cs]
                du0[r * 64:(r + 1) * 64, cs] = acc
        for cb in range(c // 128):
            cs = slice(cb * 128, (cb + 1) * 128)
            for k in range(CONV_WIDTH):
                part = jnp.zeros((8, 128), F32)
                for r in range(tm // 64):
                    p = d_ref[r * 64:(r + 1) * 64, cs] * extu[r * 64 + off + k: r * 64 + off + k + 64, cs]
                    for q in range(8):
                        part = part + p[q * 8:(q + 1) * 8, :]
                dw_ref[k:k + 1, cs] += jnp.sum(part, axis=0, keepdims=True)
        zz = z_ref[...] + bin_ref[...]
        a = zz[:, :c]
        sg = jax.nn.sigmoid(zz[:, c:])
        d0 = du0[...]
        da = d0 * sg
        dgt = d0 * a * (sg * (1.0 - sg))
        dbin_ref[:, :c] += jnp.sum(da, axis=0, keepdims=True)
        dbin_ref[:, c:] += jnp.sum(dgt, axis=0, keepdims=True)
        dz_ref[:, :c] = da.astype(BF16)
        dz_ref[:, c:] = dgt.astype(BF16)

    row = lambda i: (i, 0)
    fix = lambda i: (0, 0)
    return pl.pallas_call(
        body, grid=(s // tm,),
        in_specs=[pl.BlockSpec((tm, c), row), pl.BlockSpec((CONV_HALO, c), _next_halo(tm, CONV_HALO, s)),
                  pl.BlockSpec((tm, c), row), pl.BlockSpec((CONV_HALO, c), _prev_halo(tm, CONV_HALO)),
                  pl.BlockSpec((tm, 2 * c), row), pl.BlockSpec((1, 2 * c), fix), pl.BlockSpec((32, c), fix)],
        out_specs=[pl.BlockSpec((tm, 2 * c), row), pl.BlockSpec((32, c), fix), pl.BlockSpec((1, 2 * c), fix)],
        out_shape=[SDS((s, 2 * c), BF16), SDS((32, c), F32), SDS((1, 2 * c), F32)],
        scratch_shapes=[pltpu.VMEM((tm + CONV_HALO, c), F32), pltpu.VMEM((tm + CONV_HALO, c), F32), pltpu.VMEM((tm, c), F32)],
        compiler_params=_cp(1), name="conv_bwd_dw")(d_u1, d_u1, u0, u0, z, b_in, w32)


def _partner(v, lane):
    up = pltpu.roll(v, HEAD_PAD - ROPE // 2, 1)
    dn = pltpu.roll(v, ROPE // 2, 1)
    lo = (lane >= NOPE) & (lane < NOPE + ROPE // 2)
    hi = (lane >= NOPE + ROPE // 2) & (lane < HEAD_DIM)
    return jnp.where(lo, up, jnp.where(hi, dn, 0.0))


def _mla_prep(q_raw, kv_raw, z, cosf, sinf, gq, gk, tm):
    s = q_raw.shape[0]

    def body(q_ref, kv_ref, kr_ref, c_ref, s_ref, gq_ref, gk_ref, qo_ref, ko_ref, vo_ref):
        lane = lax.broadcasted_iota(jnp.int32, (tm, HEAD_PAD), 1)
        cf = c_ref[...]
        sf = s_ref[...]

        def norm_rope(t, g_ref):
            r = lax.rsqrt(jnp.sum(t * t, axis=-1, keepdims=True) * (1.0 / HEAD_DIM) + EPS)
            tn = (t * r) * g_ref[...]
            return tn * cf + _partner(tn, lane) * sf

        kr = kr_ref[...]
        for h in range(HEADS):
            hs = slice(h * HEAD_PAD, (h + 1) * HEAD_PAD)
            qo_ref[:, hs] = (norm_rope(q_ref[:, hs], gq_ref) * (ATT_SCALE * LOG2E)).astype(BF16)
            kv = kv_ref[:, hs]
            ko_ref[:, hs] = norm_rope(jnp.where(lane < NOPE, kv, 0.0) + kr, gk_ref).astype(BF16)
            vo_ref[:, hs] = jnp.where(lane >= NOPE, kv, 0.0).astype(BF16)

    row = lambda i: (i, 0)
    wide = pl.BlockSpec((tm, HEADS * HEAD_PAD), row)
    one = pl.BlockSpec((tm, HEAD_PAD), row)
    return pl.pallas_call(
        body, grid=(s // tm,),
        in_specs=[wide, wide, pl.BlockSpec((tm, HEAD_PAD), lambda i: (i, IN_COLS_PAD // HEAD_PAD - 1)), one, one,
                  pl.BlockSpec((1, HEAD_PAD), lambda i: (0, 0)), pl.BlockSpec((1, HEAD_PAD), lambda i: (0, 0))],
        out_specs=[wide] * 3,
        out_shape=[SDS((s, HEADS * HEAD_PAD), BF16)] * 3,
        compiler_params=_cp(1), name="mla_prep")(q_raw, kv_raw, z, cosf, sinf, gq, gk)


def _mla_prep_bwd(dqp, dkp, dvp, q_raw, kv_raw, z, cosf, sinf, gq, gk, tm):
    s = q_raw.shape[0]

    def body(dq_ref, dk_ref, dv_ref, q_ref, kv_ref, kr_ref, c_ref, s_ref, gq_ref, gk_ref,
             dqo_ref, dkvo_ref, dkr_ref, dgq_ref, dgk_ref):
        lane = lax.broadcasted_iota(jnp.int32, (tm, HEAD_PAD), 1)
        cf = c_ref[...]
        sf = s_ref[...]

        @pl.when(pl.program_id(0) == 0)
        def _():
            dgq_ref[...] = jnp.zeros_like(dgq_ref)
            dgk_ref[...] = jnp.zeros_like(dgk_ref)

        def norm_rope_bwd(t, d_out, g_ref, dg_ref):
            r = lax.rsqrt(jnp.sum(t * t, axis=-1, keepdims=True) * (1.0 / HEAD_DIM) + EPS)
            th = t * r
            dn = d_out * cf + _partner(d_out * sf, lane)
            dg_ref[...] += jnp.sum(dn * th, axis=0, keepdims=True)
            dh = dn * g_ref[...]
            return r * (dh - th * (jnp.sum(dh * th, axis=-1, keepdims=True) * (1.0 / HEAD_DIM)))

        kr = kr_ref[...]
        dkr = None
        for h in range(HEADS):
            hs = slice(h * HEAD_PAD, (h + 1) * HEAD_PAD)
            dq = norm_rope_bwd(q_ref[:, hs], dq_ref[:, hs] * ATT_SCALE, gq_ref, dgq_ref)
            dqo_ref[:, hs] = dq.astype(BF16)
            kv = kv_ref[:, hs]
            dkpre = norm_rope_bwd(jnp.where(lane < NOPE, kv, 0.0) + kr, dk_ref[:, hs] * LN2, gk_ref, dgk_ref)
            dkvo_ref[:, hs] = jnp.where(lane < NOPE, dkpre, dv_ref[:, hs]).astype(BF16)
            dkr_h = jnp.where((lane >= NOPE) & (lane < HEAD_DIM), dkpre, 0.0)
            dkr = dkr_h if dkr is None else dkr + dkr_h
        dkr_ref[...] = dkr

    row = lambda i: (i, 0)
    fix = lambda i: (0, 0)
    wide = pl.BlockSpec((tm, HEADS * HEAD_PAD), row)
    one = pl.BlockSpec((tm, HEAD_PAD), row)
    return pl.pallas_call(
        body, grid=(s // tm,),
        in_specs=[wide, wide, wide, wide, wide, pl.BlockSpec((tm, HEAD_PAD), lambda i: (i, IN_COLS_PAD // HEAD_PAD - 1)),
                  one, one, pl.BlockSpec((1, HEAD_PAD), fix), pl.BlockSpec((1, HEAD_PAD), fix)],
        out_specs=[wide, wide, one, pl.BlockSpec((1, HEAD_PAD), fix), pl.BlockSpec((1, HEAD_PAD), fix)],
        out_shape=[SDS((s, HEADS * HEAD_PAD), BF16), SDS((s, HEADS * HEAD_PAD), BF16), SDS((s, HEAD_PAD), F32),
                   SDS((1, HEAD_PAD), F32), SDS((1, HEAD_PAD), F32)],
        compiler_params=_cp(1), name="mla_prep_bwd")(dqp, dkp, dvp, q_raw, kv_raw, z, cosf, sinf, gq, gk)


def _tri_pairs(n, row_major):
    if row_major:
        pairs = [(i, j) for i in range(n) for j in range(i + 1)]
    else:
        pairs = [(i, j) for j in range(n) for i in range(j, n)]
    ii = np.array([p[0] for p in pairs], np.int32)
    jj = np.array([p[1] for p in pairs], np.int32)
    return jnp.asarray(ii), jnp.asarray(jj)


STRIP = 64


def _strip_mask(r, tb):
    col = lax.broadcasted_iota(jnp.int32, (STRIP, tb), 1)
    return col >= r * CHUNK


def _fold8(x):
    acc = x[0:8, :]
    for g in range(1, STRIP // 8):
        acc = acc + x[g * 8:(g + 1) * 8, :]
    return acc


def _attn_fwd(qp, kp, vp, tb):
    s = qp.shape[0]
    nb = s // tb
    ii, jj = _tri_pairs(nb, True)
    n_strips = tb // STRIP

    def body(ii_ref, jj_ref, q_ref, k_ref, v_ref, of_ref, ob_ref, lse_ref, m_sc, l_sc, acc_sc, st_sc, pt_sc):
        t = pl.program_id(1)
        i = ii_ref[t]
        j = jj_ref[t]

        @pl.when(j == 0)
        def _():
            m_sc[...] = jnp.full_like(m_sc, -jnp.inf)
            l_sc[...] = jnp.zeros_like(l_sc)
            acc_sc[...] = jnp.zeros_like(acc_sc)

        def step(diag):
            st_sc[...] = lax.dot_general(k_ref[...], q_ref[...], (((1,), (1,)), ((), ())), preferred_element_type=F32)
            mx = None
            for r in range(n_strips):
                sc = st_sc[r * STRIP:(r + 1) * STRIP, :]
                if diag and r > 0:
                    sc = jnp.where(_strip_mask(r, tb), sc, -jnp.inf)
                m8 = sc[0:8, :]
                for g in range(1, STRIP // 8):
                    m8 = jnp.maximum(m8, sc[g * 8:(g + 1) * 8, :])
                mx = m8 if mx is None else jnp.maximum(mx, m8)
            m_old = m_sc[0:1, :]
            m_new = jnp.maximum(m_old, jnp.max(mx, axis=0, keepdims=True))
            alpha = jnp.exp2(m_old - m_new)
            ps = None
            for r in range(n_strips):
                p = jnp.exp2(st_sc[r * STRIP:(r + 1) * STRIP, :] - m_new)
                if diag and r > 0:
                    p = jnp.where(_strip_mask(r, tb), p, 0.0)
                p8 = _fold8(p)
                ps = p8 if ps is None else ps + p8
                pt_sc[r * STRIP:(r + 1) * STRIP, :] = p.astype(BF16)
            l_new = alpha * l_sc[0:1, :] + jnp.sum(ps, axis=0, keepdims=True)
            m_sc[...] = jnp.broadcast_to(m_new, m_sc.shape)
            l_sc[...] = jnp.broadcast_to(l_new, l_sc.shape)
            pv = lax.dot_general(v_ref[...], pt_sc[...], (((0,), (0,)), ((), ())), preferred_element_type=F32)
            acc_sc[...] = alpha * acc_sc[...] + pv

        @pl.when(j < i)
        def _():
            step(False)

        @pl.when(j == i)
        def _():
            step(True)
            l = l_sc[0:1, :]
            o = (acc_sc[...] / l).T
            of_ref[...] = o
            ob_ref[...] = o.astype(BF16)
            lse_ref[...] = m_sc[0:1, :] + jnp.log(l) * LOG2E

    qmap = lambda h, t, ii_ref, jj_ref: (ii_ref[t], h)
    kmap = lambda h, t, ii_ref, jj_ref: (jj_ref[t], h)
    blk = (tb, HEAD_PAD)
    gs = pltpu.PrefetchScalarGridSpec(
        num_scalar_prefetch=2, grid=(HEADS, int(ii.shape[0])),
        in_specs=[pl.BlockSpec(blk, qmap), pl.BlockSpec(blk, kmap), pl.BlockSpec(blk, kmap)],
        out_specs=[pl.BlockSpec(blk, qmap), pl.BlockSpec(blk, qmap),
                   pl.BlockSpec((None, 1, tb), lambda h, t, ii_ref, jj_ref: (h, 0, ii_ref[t]))],
        scratch_shapes=[pltpu.VMEM((8, tb), F32), pltpu.VMEM((8, tb), F32), pltpu.VMEM((HEAD_PAD, tb), F32),
                        pltpu.VMEM((tb, tb), F32), pltpu.VMEM((tb, tb), BF16)])
    w = HEADS * HEAD_PAD
    return pl.pallas_call(body, grid_spec=gs, out_shape=[SDS((s, w), F32), SDS((s, w), BF16), SDS((HEADS, 1, s), F32)],
                          compiler_params=_cp(2), name="attn_fwd")(ii, jj, qp, kp, vp)


def _attn_delta(do, o, tb):
    s = do.shape[0]

    def body(do_ref, o_ref, d_ref):
        for h in range(HEADS):
            hs = slice(h * HEAD_PAD, (h + 1) * HEAD_PAD)
            d_ref[h] = jnp.sum((do_ref[:, hs] * o_ref[:, hs]).T, axis=0, keepdims=True)

    blk = pl.BlockSpec((tb, HEADS * HEAD_PAD), lambda i: (i, 0))
    return pl.pallas_call(body, grid=(s // tb,), in_specs=[blk, blk],
                          out_specs=pl.BlockSpec((HEADS, 1, tb), lambda i: (0, 0, i)),
                          out_shape=SDS((HEADS, 1, s), F32), compiler_params=_cp(1), name="attn_delta")(do, o)


def _attn_bwd(qp, kp, vp, dob, lse, delta, tb):
    s = qp.shape[0]
    nb = s // tb
    ii, jj = _tri_pairs(nb, False)

    def body(ii_ref, jj_ref, q_ref, k_ref, v_ref, do_ref, lse_ref, dl_ref, dq_ref, dk_ref, dv_ref, st_sc, dpt_sc, pt_sc, dst_sc):
        t = pl.program_id(1)
        i = ii_ref[t]
        j = jj_ref[t]

        @pl.when(t == 0)
        def _():
            dq_ref[...] = jnp.zeros_like(dq_ref)

        @pl.when(i == j)
        def _():
            dk_ref[...] = jnp.zeros_like(dk_ref)
            dv_ref[...] = jnp.zeros_like(dv_ref)

        def step(diag):
            q = q_ref[...]
            k = k_ref[...]
            do = do_ref[...]
            st_sc[...] = lax.dot_general(k, q, (((1,), (1,)), ((), ())), preferred_element_type=F32)
            dpt_sc[...] = lax.dot_general(v_ref[...], do, (((1,), (1,)), ((), ())), preferred_element_type=F32)
            lse = lse_ref[...]
            dl = dl_ref[...]
            for r in range(tb // STRIP):
                rows = slice(r * STRIP, (r + 1) * STRIP)
                p = jnp.exp2(st_sc[rows, :] - lse)
                if diag and r > 0:
                    p = jnp.where(_strip_mask(r, tb), p, 0.0)
                ds = p * (dpt_sc[rows, :] - dl)
                pt_sc[rows, :] = p.astype(BF16)
                dst_sc[rows, :] = ds.astype(BF16)
            dv_ref[...] += jnp.dot(pt_sc[...], do, preferred_element_type=F32)
            dst = dst_sc[...]
            dk_ref[...] += jnp.dot(dst, q, preferred_element_type=F32)
            dq_ref[i] += lax.dot_general(k, dst, (((0,), (0,)), ((), ())), preferred_element_type=F32)

        @pl.when(j < i)
        def _():
            step(False)

        @pl.when(j == i)
        def _():
            step(True)

    qmap = lambda h, t, ii_ref, jj_ref: (ii_ref[t], h)
    kmap = lambda h, t, ii_ref, jj_ref: (jj_ref[t], h)
    rowmap = lambda h, t, ii_ref, jj_ref: (h, 0, ii_ref[t])
    blk = (tb, HEAD_PAD)
    gs = pltpu.PrefetchScalarGridSpec(
        num_scalar_prefetch=2, grid=(HEADS, int(ii.shape[0])),
        in_specs=[pl.BlockSpec(blk, qmap), pl.BlockSpec(blk, kmap), pl.BlockSpec(blk, kmap), pl.BlockSpec(blk, qmap),
                  pl.BlockSpec((None, 1, tb), rowmap), pl.BlockSpec((None, 1, tb), rowmap)],
        out_specs=[pl.BlockSpec((None, nb, HEAD_PAD, tb), lambda h, t, ii_ref, jj_ref: (h, 0, 0, 0)),
                   pl.BlockSpec(blk, kmap), pl.BlockSpec(blk, kmap)],
        scratch_shapes=[pltpu.VMEM((tb, tb), F32), pltpu.VMEM((tb, tb), F32), pltpu.VMEM((tb, tb), BF16), pltpu.VMEM((tb, tb), BF16)])
    w = HEADS * HEAD_PAD
    dqt, dk, dv = pl.pallas_call(
        body, grid_spec=gs, out_shape=[SDS((HEADS, nb, HEAD_PAD, tb), F32), SDS((s, w), F32), SDS((s, w), F32)],
        compiler_params=_cp(2), name="attn_bwd")(ii, jj, qp, kp, vp, dob, lse, delta)
    return jnp.transpose(dqt, (1, 3, 0, 2)).reshape(s, w), dk, dv


def _head_norm(t, g):
    r = lax.rsqrt(jnp.mean(t * t, axis=-1, keepdims=True) + EPS)
    th = t * r
    return r, th, th * g


def _softmax_rows(sc):
    m = jnp.max(sc, axis=-1, keepdims=True)
    e = jnp.exp(sc - m)
    return e / jnp.sum(e, axis=-1, keepdims=True)


def _memattn_fwd(qm, kvm, gq, gk, tm):
    s = qm.shape[0]
    hd = MEM_HEAD_DIM

    def body(q_ref, k_ref, v_ref, gq_ref, gk_ref, o_ref):
        _, _, qn = _head_norm(q_ref[...], gq_ref[...])
        _, _, kn = _head_norm(k_ref[...], gk_ref[...])
        sc = lax.dot_general(qn.astype(BF16), kn.astype(BF16), (((1,), (1,)), ((), ())), preferred_element_type=F32)
        p = _softmax_rows(sc * (1.0 / math.sqrt(hd)))
        o_ref[...] = jnp.dot(p.astype(BF16), v_ref[...].astype(BF16), preferred_element_type=F32).astype(BF16)

    fix = lambda i, h: (0, 0)
    return pl.pallas_call(
        body, grid=(s // tm, MEM_HEADS),
        in_specs=[pl.BlockSpec((tm, hd), lambda i, h: (i, h)), pl.BlockSpec((MEM_LEN, hd), lambda i, h: (0, h)),
                  pl.BlockSpec((MEM_LEN, hd), lambda i, h: (0, MEM_HEADS + h)), pl.BlockSpec((1, hd), fix), pl.BlockSpec((1, hd), fix)],
        out_specs=pl.BlockSpec((tm, hd), lambda i, h: (i, h)),
        out_shape=SDS((s, MEM_HEADS * hd), BF16), compiler_params=_cp(2), name="memattn_fwd")(qm, kvm, kvm, gq, gk)


def _memattn_bwd(qm, kvm, d_o, gq, gk, tm):
    s = qm.shape[0]
    hd = MEM_HEAD_DIM

    def body(q_ref, k_ref, v_ref, do_ref, gq_ref, gk_ref, dq_ref, dk_ref, dv_ref, dgq_ref, dgk_ref, dkn_sc):
        h = pl.program_id(0)
        i = pl.program_id(1)
        last = pl.num_programs(1) - 1

        @pl.when((h == 0) & (i == 0))
        def _():
            dgq_ref[...] = jnp.zeros_like(dgq_ref)
            dgk_ref[...] = jnp.zeros_like(dgk_ref)

        @pl.when(i == 0)
        def _():
            dv_ref[...] = jnp.zeros_like(dv_ref)
            dkn_sc[...] = jnp.zeros_like(dkn_sc)

        rq, qh, qn = _head_norm(q_ref[...], gq_ref[...])
        rk, kh, kn = _head_norm(k_ref[...], gk_ref[...])
        qnb = qn.astype(BF16)
        knb = kn.astype(BF16)
        scale = 1.0 / math.sqrt(hd)
        sc = lax.dot_general(qnb, knb, (((1,), (1,)), ((), ())), preferred_element_type=F32)
        p = _softmax_rows(sc * scale)
        do = do_ref[...].astype(BF16)
        dp = lax.dot_general(do, v_ref[...].astype(BF16), (((1,), (1,)), ((), ())), preferred_element_type=F32)
        dv_ref[...] += lax.dot_general(p.astype(BF16), do, (((0,), (0,)), ((), ())), preferred_element_type=F32)
        ds = ((p * (dp - jnp.sum(dp * p, axis=-1, keepdims=True))) * scale).astype(BF16)
        dqn = jnp.dot(ds, knb, preferred_element_type=F32)
        dkn_sc[...] += lax.dot_general(ds, qnb, (((0,), (0,)), ((), ())), preferred_element_type=F32)
        dgq_ref[...] += jnp.sum(dqn * qh, axis=0, keepdims=True)
        dqh = dqn * gq_ref[...]
        dq_ref[...] = (rq * (dqh - qh * jnp.mean(dqh * qh, axis=-1, keepdims=True))).astype(BF16)

        @pl.when(i == last)
        def _():
            dkn = dkn_sc[...]
            dgk_ref[...] += jnp.sum(dkn * kh, axis=0, keepdims=True)
            dkh = dkn * gk_ref[...]
            dk_ref[...] = rk * (dkh - kh * jnp.mean(dkh * kh, axis=-1, keepdims=True))

    fix = lambda h, i: (0, 0)
    qb = pl.BlockSpec((tm, hd), lambda h, i: (i, h))
    kb = pl.BlockSpec((MEM_LEN, hd), lambda h, i: (0, h))
    return pl.pallas_call(
        body, grid=(MEM_HEADS, s // tm),
        in_specs=[qb, kb, pl.BlockSpec((MEM_LEN, hd), lambda h, i: (0, MEM_HEADS + h)), qb,
                  pl.BlockSpec((1, hd), fix), pl.BlockSpec((1, hd), fix)],
        out_specs=[qb, kb, kb, pl.BlockSpec((1, hd), fix), pl.BlockSpec((1, hd), fix)],
        out_shape=[SDS((s, MEM_HEADS * hd), BF16), SDS((MEM_LEN, MEM_HEADS * hd), F32), SDS((MEM_LEN, MEM_HEADS * hd), F32),
                   SDS((1, hd), F32), SDS((1, hd), F32)],
        scratch_shapes=[pltpu.VMEM((MEM_LEN, hd), F32)],
        compiler_params=_cp(2), name="memattn_bwd")(qm, kvm, kvm, d_o, gq, gk)


def _ffn_specs(tm, tn, nbj, s, order_ji):
    if order_ji:
        ij = lambda f: (lambda j, i: f(i, j))
    else:
        ij = lambda f: f
    prev = lambda i: jnp.maximum(i * (tm // FFN_HALO) - 1, 0)
    cur_g = pl.BlockSpec((tm, tn), ij(lambda i, j: (i, j)))
    cur_v = pl.BlockSpec((tm, tn), ij(lambda i, j: (i, j + nbj)))
    halo_g = pl.BlockSpec((FFN_HALO, tn), ij(lambda i, j: (prev(i), j)))
    halo_v = pl.BlockSpec((FFN_HALO, tn), ij(lambda i, j: (prev(i), j + nbj)))
    w_g = pl.BlockSpec((8, tn), ij(lambda i, j: (0, j)))
    w_v = pl.BlockSpec((8, tn), ij(lambda i, j: (0, j + nbj)))
    b_g = pl.BlockSpec((1, tn), ij(lambda i, j: (0, j)))
    b_v = pl.BlockSpec((1, tn), ij(lambda i, j: (0, j + nbj)))
    return cur_g, cur_v, halo_g, halo_v, w_g, w_v, b_g, b_v


def _conv3(cur_ref, halo_ref, w_ref, b_ref, ext, first, tm):
    ext[0:FFN_HALO, :] = jnp.where(first, 0.0, halo_ref[...])
    ext[FFN_HALO:, :] = cur_ref[...]
    return (w_ref[0:1, :] * ext[FFN_HALO - 2:FFN_HALO - 2 + tm, :] + w_ref[1:2, :] * ext[FFN_HALO - 1:FFN_HALO - 1 + tm, :]
            + w_ref[2:3, :] * ext[FFN_HALO:FFN_HALO + tm, :] + b_ref[...])


def _ffn_fwd(up0, w8, b, tm, tn):
    s = up0.shape[0]
    nbj = D_FF // tn

    def body(g_ref, v_ref, gh_ref, vh_ref, wg_ref, wv_ref, bg_ref, bv_ref, act_ref, ext):
        first = pl.program_id(0) == 0
        ug = _conv3(g_ref, gh_ref, wg_ref, bg_ref, ext, first, tm)
        uv = _conv3(v_ref, vh_ref, wv_ref, bv_ref, ext, first, tm)
        act_ref[...] = ((ug * jax.nn.sigmoid(ug)) * uv).astype(BF16)

    specs = _ffn_specs(tm, tn, nbj, s, False)
    return pl.pallas_call(
        body, grid=(s // tm, nbj), in_specs=list(specs),
        out_specs=pl.BlockSpec((tm, tn), lambda i, j: (i, j)), out_shape=SDS((s, D_FF), BF16),
        scratch_shapes=[pltpu.VMEM((tm + FFN_HALO, tn), F32)],
        compiler_params=_cp(2), name="ffn_fwd")(up0, up0, up0, up0, w8, w8, b, b)


def _ffn_bwd(d_act, up0, w8, b, tm, tn):
    s = up0.shape[0]
    nbj = D_FF // tn
    te = tm + FFN_HALO

    def body(da_ref, dan_ref, g_ref, v_ref, gh_ref, vh_ref, gn_ref, vn_ref, wg_ref, wv_ref, bg_ref, bv_ref,
             og_ref, ov_ref, dbg_ref, dbv_ref, dwg_ref, dwv_ref, extg, extv, exta, extd):
        i = pl.program_id(1)
        first = i == 0
        last = i == pl.num_programs(1) - 1

        @pl.when(first)
        def _():
            for r in (dbg_ref, dbv_ref, dwg_ref, dwv_ref):
                r[...] = jnp.zeros_like(r)

        def conv(ext, h_ref, c_ref, n_ref, w_ref, b_ref):
            ext[0:FFN_HALO, :] = jnp.where(first, 0.0, h_ref[...])
            ext[FFN_HALO:FFN_HALO + tm, :] = c_ref[...]
            ext[FFN_HALO + tm:, :] = n_ref[...]
            return (w_ref[0:1, :] * ext[FFN_HALO - 2:FFN_HALO - 2 + te, :] + w_ref[1:2, :] * ext[FFN_HALO - 1:FFN_HALO - 1 + te, :]
                    + w_ref[2:3, :] * ext[FFN_HALO:FFN_HALO + te, :] + b_ref[...])

        ug = conv(extg, gh_ref, g_ref, gn_ref, wg_ref, bg_ref)
        uv = conv(extv, vh_ref, v_ref, vn_ref, wv_ref, bv_ref)
        exta[0:tm, :] = da_ref[...]
        exta[tm:, :] = jnp.where(last, 0.0, dan_ref[...])
        da = exta[...]
        sg = jax.nn.sigmoid(ug)

        def conv_bwd(d_up, extu, w_ref, o_ref, db_ref, dw_ref):
            extd[...] = d_up
            o_ref[...] = (w_ref[2:3, :] * extd[0:tm, :] + w_ref[1:2, :] * extd[1:1 + tm, :]
                          + w_ref[0:1, :] * extd[2:2 + tm, :]).astype(BF16)
            d = extd[0:tm, :]
            db_ref[...] += jnp.sum(d, axis=0, keepdims=True)
            for k in range(3):
                o = FFN_HALO - 2 + k
                dw_ref[k:k + 1, :] += jnp.sum(d * extu[o:o + tm, :], axis=0, keepdims=True)

        conv_bwd(da * uv * (sg * (1.0 + ug * (1.0 - sg))), extg, wg_ref, og_ref, dbg_ref, dwg_ref)
        conv_bwd(da * (ug * sg), extv, wv_ref, ov_ref, dbv_ref, dwv_ref)

    cur_g, cur_v, halo_g, halo_v, w_g, w_v, b_g, b_v = _ffn_specs(tm, tn, nbj, s, True)
    nxt_row = lambda i: jnp.minimum((i + 1) * (tm // FFN_HALO), s // FFN_HALO - 1)
    cur = pl.BlockSpec((tm, tn), lambda j, i: (i, j))
    nxt = pl.BlockSpec((FFN_HALO, tn), lambda j, i: (nxt_row(i), j))
    nxt_v = pl.BlockSpec((FFN_HALO, tn), lambda j, i: (nxt_row(i), j + nbj))
    acc1 = pl.BlockSpec((1, tn), lambda j, i: (0, j))
    acc8 = pl.BlockSpec((8, tn), lambda j, i: (0, j))
    return pl.pallas_call(
        body, grid=(nbj, s // tm), in_specs=[cur, nxt, cur_g, cur_v, halo_g, halo_v, nxt, nxt_v, w_g, w_v, b_g, b_v],
        out_specs=[cur, cur, acc1, acc1, acc8, acc8],
        out_shape=[SDS((s, D_FF), BF16), SDS((s, D_FF), BF16), SDS((1, D_FF), F32), SDS((1, D_FF), F32),
                   SDS((8, D_FF), F32), SDS((8, D_FF), F32)],
        scratch_shapes=[pltpu.VMEM((tm + 2 * FFN_HALO, tn), F32), pltpu.VMEM((tm + 2 * FFN_HALO, tn), F32),
                        pltpu.VMEM((te, tn), F32), pltpu.VMEM((te, tn), F32)],
        compiler_params=_cp(2), name="ffn_bwd")(d_act, d_act, up0, up0, up0, up0, up0, up0, w8, w8, b, b)


def _down_loss(act, w_down, x2, target, tm):
    s = act.shape[0]

    def body(a_ref, w_ref, x_ref, t_ref, dyf_ref, dyb_ref, ls_ref):
        @pl.when(pl.program_id(0) == 0)
        def _():
            ls_ref[...] = jnp.zeros_like(ls_ref)

        y = x_ref[...] + jnp.dot(a_ref[...], w_ref[...], preferred_element_type=F32)
        e = y - t_ref[...]
        ls_ref[...] += jnp.sum(e * e)
        dy = e * (1.0 / D_MODEL)
        dyf_ref[...] = dy
        dyb_ref[...] = dy.astype(BF16)

    row = lambda i: (i, 0)
    return pl.pallas_call(
        body, grid=(s // tm,),
        in_specs=[pl.BlockSpec((tm, D_FF), row), pl.BlockSpec((D_FF, D_MODEL), lambda i: (0, 0)),
                  pl.BlockSpec((tm, D_MODEL), row), pl.BlockSpec((tm, D_MODEL), row)],
        out_specs=[pl.BlockSpec((tm, D_MODEL), row), pl.BlockSpec((tm, D_MODEL), row), pl.BlockSpec((8, 128), lambda i: (0, 0))],
        out_shape=[SDS((s, D_MODEL), F32), SDS((s, D_MODEL), BF16), SDS((8, 128), F32)],
        compiler_params=_cp(1), name="down_loss")(act, w_down, x2, target)


def _adamw_math(w, g, m, v):
    mn = ADAM_B1 * m + (1.0 - ADAM_B1) * g
    vn = ADAM_B2 * v + (1.0 - ADAM_B2) * (g * g)
    m_hat = mn / (1.0 - ADAM_B1 ** ADAM_STEP)
    v_hat = vn / (1.0 - ADAM_B2 ** ADAM_STEP)
    return -ADAM_LR * (m_hat / (jnp.sqrt(v_hat) + ADAM_EPS) + ADAM_WD * w), mn, vn


def _adamw(w, g, m, v, name):
    rows, cols = w.shape
    tr = rows if rows <= 256 else (256 if rows % 256 == 0 else rows // 2)

    def body(w_ref, g_ref, m_ref, v_ref, d_ref, mo_ref, vo_ref):
        d_ref[...], mo_ref[...], vo_ref[...] = _adamw_math(w_ref[...], g_ref[...], m_ref[...], v_ref[...])

    blk = pl.BlockSpec((tr, cols), lambda i: (i, 0))
    return pl.pallas_call(body, grid=(rows // tr,), in_specs=[blk] * 4, out_specs=[blk] * 3,
                          out_shape=[SDS((rows, cols), F32)] * 3, compiler_params=_cp(1), name=name)(w, g, m, v)


def _adamw_small(ws, gs, ms, vs):
    n = len(ws)

    def body(*refs):
        ins, outs = refs[:4 * n], refs[4 * n:]
        for k in range(n):
            d, mn, vn = _adamw_math(ins[k][...], ins[n + k][...], ins[2 * n + k][...], ins[3 * n + k][...])
            outs[k][...] = d
            outs[n + k][...] = mn
            outs[2 * n + k][...] = vn

    vm = pl.BlockSpec(memory_space=pltpu.VMEM)
    outs = pl.pallas_call(body, in_specs=[vm] * (4 * n), out_specs=[vm] * (3 * n),
                          out_shape=[SDS(w.shape, F32) for w in ws] * 3, name="adamw_small")(*ws, *gs, *ms, *vs)
    return outs[:n], outs[n:2 * n], outs[2 * n:]


ANY = pl.BlockSpec(memory_space=pl.ANY)


def _coords():
    return lax.axis_index("x"), lax.axis_index("y"), lax.axis_index("c")


def _other_chips(x, y):
    return [(1 - x, y), (x, 1 - y), (1 - x, 1 - y)]


D2D_CHUNKS = 8
ICI_CHUNKS = 4


def _row_chunks(n_rows, n_chunks, align):
    step = -(-n_rows // (n_chunks * align)) * align
    return [(r, min(step, n_rows - r)) for r in range(0, n_rows, step)]


def _ag_weights(wsh):
    rows, cols = wsh.shape
    half_rows = rows // 2

    def body(w_ref, out_ref, send_sems, recv_sems):
        x, y, c = _coords()
        s_me = 2 * x + y
        chips = _other_chips(x, y)
        sibling = (x, y, 1 - c)
        my_base = c * half_rows
        sib_base = (1 - c) * half_rows

        def piece(base, r0, nr):
            return pl.ds(pl.multiple_of(base + r0, 16), nr)

        def copy(k, shard, rows_, to, src=None):
            dst = out_ref.at[shard, rows_]
            return pltpu.make_async_remote_copy(src_ref=dst if src is None else src, dst_ref=dst, send_sem=send_sems.at[k],
                                                recv_sem=recv_sems.at[k], device_id=to, device_id_type=MESH)

        for k, (px, py) in enumerate(chips):
            for r0, nr in _row_chunks(half_rows, ICI_CHUNKS, 16):
                copy(k, s_me, piece(my_base, r0, nr), (px, py, c), src=w_ref.at[piece(my_base, r0, nr)]).start()
        for k, (px, py) in enumerate(chips):
            copy(k, 2 * px + py, piece(my_base, 0, half_rows), (px, py, c)).wait_recv()
            for r0, nr in _row_chunks(half_rows, ICI_CHUNKS, 16):
                copy(3 + k, 2 * px + py, piece(my_base, r0, nr), sibling).start()
        for k, (px, py) in enumerate(chips):
            copy(3 + k, 2 * px + py, piece(sib_base, 0, half_rows), sibling).wait_recv()
        for k in range(6):
            copy(k, s_me, piece(my_base, 0, half_rows), sibling).wait_send()

    return pl.pallas_call(
        body, in_specs=[ANY], out_specs=ANY, out_shape=SDS((4, rows, cols), wsh.dtype),
        scratch_shapes=[pltpu.SemaphoreType.DMA((6,)), pltpu.SemaphoreType.DMA((6,))],
        name="ag_weights")(wsh)


def _rs_swap_halves(gfull):
    n_sh, rows, cols = gfull.shape
    half_rows = rows // 2

    def body(g_ref, recv_ref, send_sem, recv_sem):
        x, y, c = _coords()
        sib_base = (1 - c) * half_rows
        for sh in range(n_sh):
            for r0, nr in _row_chunks(half_rows, D2D_CHUNKS, 8):
                pltpu.make_async_remote_copy(
                    src_ref=g_ref.at[sh, pl.ds(pl.multiple_of(sib_base + r0, 8), nr)], dst_ref=recv_ref.at[sh, pl.ds(r0, nr)],
                    send_sem=send_sem, recv_sem=recv_sem, device_id=(x, y, 1 - c), device_id_type=MESH).start()
        pltpu.make_async_remote_copy(src_ref=recv_ref, dst_ref=recv_ref, send_sem=send_sem, recv_sem=recv_sem,
                                     device_id=(x, y, 1 - c), device_id_type=MESH).wait()

    return pl.pallas_call(
        body, in_specs=[ANY], out_specs=ANY, out_shape=SDS((n_sh, half_rows, cols), gfull.dtype),
        scratch_shapes=[pltpu.SemaphoreType.DMA, pltpu.SemaphoreType.DMA], name="rs_swap_halves")(gfull)


def _rs_add_pair(gfull, recv, core, tr):
    n_sh, rows, cols = gfull.shape
    half_rows = rows // 2
    nblk = half_rows // tr

    def body(c_ref, g_ref, r_ref, o_ref, ob_ref):
        acc = g_ref[...] + r_ref[...]
        o_ref[...] = acc
        ob_ref[...] = acc.astype(BF16)

    out = pl.BlockSpec((None, tr, cols), lambda sh, i, c_ref: (sh, i, 0))
    gs = pltpu.PrefetchScalarGridSpec(
        num_scalar_prefetch=1, grid=(n_sh, nblk),
        in_specs=[pl.BlockSpec((None, tr, cols), lambda sh, i, c_ref: (sh, c_ref[0] * nblk + i, 0)), out],
        out_specs=[out, out])
    return pl.pallas_call(body, grid_spec=gs, out_shape=[SDS((n_sh, half_rows, cols), F32), SDS((n_sh, half_rows, cols), BF16)],
                          compiler_params=_cp(2), name="rs_add_pair")(core, gfull, recv)


def _rs_to_owner(chipsum):
    n_sh, half_rows, cols = chipsum.shape

    def body(cs_ref, recv_ref, send_sems, recv_sems):
        x, y, c = _coords()
        chips = _other_chips(x, y)
        for k, (px, py) in enumerate(chips):
            for r0, nr in _row_chunks(half_rows, ICI_CHUNKS, 16):
                pltpu.make_async_remote_copy(
                    src_ref=cs_ref.at[2 * px + py, pl.ds(r0, nr)], dst_ref=recv_ref.at[k, pl.ds(r0, nr)],
                    send_sem=send_sems.at[k], recv_sem=recv_sems.at[k], device_id=(px, py, c), device_id_type=MESH).start()
        for k, (px, py) in enumerate(chips):
            pltpu.make_async_remote_copy(src_ref=recv_ref.at[k], dst_ref=recv_ref.at[k], send_sem=send_sems.at[k],
                                         recv_sem=recv_sems.at[k], device_id=(px, py, c), device_id_type=MESH).wait()

    return pl.pallas_call(
        body, in_specs=[ANY], out_specs=ANY, out_shape=SDS((3, half_rows, cols), chipsum.dtype),
        scratch_shapes=[pltpu.SemaphoreType.DMA((3,)), pltpu.SemaphoreType.DMA((3,))], name="rs_to_owner")(chipsum)


def _rs_add_chips(chipsum, recv, shard_core, tr):
    _, half_rows, cols = chipsum.shape

    def body(s_ref, m_ref, r0_ref, r1_ref, r2_ref, o_ref):
        o_ref[...] = ((m_ref[...] + r0_ref[...].astype(F32)) + r1_ref[...].astype(F32)) + r2_ref[...].astype(F32)

    gs = pltpu.PrefetchScalarGridSpec(
        num_scalar_prefetch=1, grid=(half_rows // tr,),
        in_specs=[pl.BlockSpec((None, tr, cols), lambda i, s_ref: (s_ref[0], i, 0))]
        + [pl.BlockSpec((None, tr, cols), (lambda k: lambda i, s_ref: (k, i, 0))(k)) for k in range(3)],
        out_specs=pl.BlockSpec((None, tr, cols), lambda i, s_ref: (s_ref[1], i, 0)))
    return pl.pallas_call(body, grid_spec=gs, out_shape=SDS((2, half_rows, cols), F32),
                          compiler_params=_cp(1), name="rs_add_chips")(shard_core, chipsum, recv, recv, recv)


def _rs_join_halves(buf):
    _, half_rows, cols = buf.shape

    def body(b_ref, out_ref, send_sem, recv_sem):
        x, y, c = _coords()
        for r0, nr in _row_chunks(half_rows, D2D_CHUNKS, 8):
            pltpu.make_async_remote_copy(src_ref=out_ref.at[c, pl.ds(r0, nr)], dst_ref=out_ref.at[c, pl.ds(r0, nr)], send_sem=send_sem,
                                         recv_sem=recv_sem, device_id=(x, y, 1 - c), device_id_type=MESH).start()
        pltpu.make_async_remote_copy(src_ref=out_ref.at[c], dst_ref=out_ref.at[c], send_sem=send_sem, recv_sem=recv_sem,
                                     device_id=(x, y, 1 - c), device_id_type=MESH).wait()

    return pl.pallas_call(
        body, in_specs=[ANY], out_specs=ANY, out_shape=SDS(buf.shape, buf.dtype), input_output_aliases={0: 0},
        scratch_shapes=[pltpu.SemaphoreType.DMA, pltpu.SemaphoreType.DMA], name="rs_join_halves")(buf)


BIG = [("w_in", (1024, 1440), 1), ("w_uq", (256, 768), 1), ("w_ukv", (128, 1024), 1), ("w_out", (1024, 1024), 0),
       ("w_mem_q", (1024, 1024), 0), ("w_mem_kv", (1024, 2048), 1), ("w_mem_o", (1024, 1024), 0),
       ("w_up", (1024, 5632), 1), ("w_down", (2816, 1024), 0)]
SMALL_REP = [("mix_norm_g", 1024), ("b_conv_in", 1024), ("b_conv_dw", 512), ("conv_ln_g", 512), ("conv_ln_b", 512),
             ("q_lat_norm_g", 256), ("kv_lat_norm_g", 128), ("q_norm_g", 96), ("k_norm_g", 96), ("mem_norm_x_g", 1024),
             ("mem_norm_m_g", 1024), ("mem_q_norm_g", 256), ("mem_k_norm_g", 256), ("ffn_norm_g", 1024), ("b_ffn_dw", 5632)]
SMALL_SH = [("w_conv_dw", (31, 512)), ("w_ffn_dw", (3, 5632))]


def _shard_shape(shape, axis):
    return tuple(d // 4 if a == axis else d for a, d in enumerate(shape))


def _pack_rows(parts, rows, cols):
    flat = jnp.concatenate([p.reshape(-1) for p in parts])
    flat = jnp.pad(flat, (0, rows * cols - flat.shape[0]))
    return flat.reshape(rows, cols)


def _pack_big_shards(ws):
    parts = [ws[n].reshape(-1, PACK_COLS) for n, _, _ in BIG]
    used = sum(p.shape[0] for p in parts)
    return jnp.concatenate(parts + [jnp.zeros((PACK_ROWS - used, PACK_COLS), parts[0].dtype)], axis=0)


def _unpack_big_shards(packed):
    out, r = {}, 0
    for n, shape, axis in BIG:
        sh = _shard_shape(shape, axis)
        nr = sh[0] * sh[1] // PACK_COLS
        out[n] = packed[r:r + nr].reshape(sh)
        r += nr
    return out


def _unpack_gathered(g):
    out, r = {}, 0
    for n, shape, axis in BIG:
        sh = _shard_shape(shape, axis)
        nr = sh[0] * sh[1] // PACK_COLS
        part = g[:, r:r + nr]
        if axis == 0:
            out[n] = part.reshape(shape)
        else:
            out[n] = part.reshape((4,) + sh).transpose(1, 0, 2).reshape(shape)
        r += nr
    return out


def _pack_full_grads(gs):
    parts = []
    for n, shape, axis in BIG:
        sh = _shard_shape(shape, axis)
        nr = sh[0] * sh[1] // PACK_COLS
        if axis == 0:
            parts.append(gs[n].reshape(4, nr, PACK_COLS))
        else:
            parts.append(gs[n].reshape(shape[0], 4, sh[1]).transpose(1, 0, 2).reshape(4, nr, PACK_COLS))
    used = sum(p.shape[1] for p in parts)
    return jnp.concatenate(parts + [jnp.zeros((4, PACK_ROWS - used, PACK_COLS), F32)], axis=1)


def _rope_tables(positions):
    inv_freq = ROPE_THETA ** (-jnp.arange(0, ROPE, 2, dtype=F32) / ROPE)
    ang = positions.astype(F32)[:, None] * inv_freq
    cos, sin = jnp.cos(ang), jnp.sin(ang)
    s = positions.shape[0]
    cosf = jnp.concatenate([jnp.ones((s, NOPE), F32), cos, cos, jnp.ones((s, HEAD_PAD - HEAD_DIM), F32)], axis=-1)
    sinf = jnp.concatenate([jnp.zeros((s, NOPE), F32), -sin, sin, jnp.zeros((s, HEAD_PAD - HEAD_DIM), F32)], axis=-1)
    return cosf, sinf


def _pad_heads(w, per_head):
    k = w.shape[0]
    w3 = w.reshape(k, HEADS, per_head)
    return jnp.pad(w3, ((0, 0), (0, 0), (0, HEAD_PAD - per_head))).reshape(k, HEADS * HEAD_PAD)


def _layer_grads(x, mem, positions, target, wf, sp):
    s = x.shape[0]
    tm = _row_tile(s, 512)
    tc = _row_tile(s, 256)
    tb = 512 if s % 512 == 0 and s > 512 else s // 2
    row2 = lambda a: a.reshape(1, -1)

    w_in = wf["w_in"]
    w_in_pad = jnp.concatenate([w_in[:, :1408], jnp.zeros((D_MODEL, NOPE), BF16), w_in[:, 1408:],
                                jnp.zeros((D_MODEL, HEAD_PAD - HEAD_DIM), BF16)], axis=1)
    w_uq_pad = _pad_heads(wf["w_uq"], HEAD_DIM)
    w_ukv = wf["w_ukv"]
    w_out_u = wf["w_out"][:CONV_CH]
    w_out_o = jnp.pad(wf["w_out"][CONV_CH:].reshape(HEADS, NOPE, D_MODEL), ((0, 0), (NOPE, 0), (0, 0))).reshape(HEADS * HEAD_PAD, D_MODEL)
    w_up_g, w_up_v = wf["w_up"][:, :D_FF], wf["w_up"][:, D_FF:]
    gq_pad = jnp.pad(sp["q_norm_g"], (0, HEAD_PAD - HEAD_DIM)).reshape(1, HEAD_PAD)
    gk_pad = jnp.pad(sp["k_norm_g"], (0, HEAD_PAD - HEAD_DIM)).reshape(1, HEAD_PAD)
    w_dw32 = jnp.pad(sp["w_conv_dw"], ((0, 1), (0, 0)))
    w_ffn8 = jnp.pad(sp["w_ffn_dw"], ((0, 5), (0, 0)))
    b_ffn = row2(sp["b_ffn_dw"])
    cosf, sinf = _rope_tables(positions)

    z, h1 = _norm_linear(x, 0, D_MODEL, row2(sp["mix_norm_g"]), w_in_pad, F32, tm, IN_COLS_PAD, "in_proj")
    u, u0, u1 = _conv_fwd(z, row2(sp["b_conv_in"]), w_dw32, row2(sp["b_conv_dw"]), row2(sp["conv_ln_g"]), row2(sp["conv_ln_b"]), tc)
    q_raw, cqn = _norm_linear(z, 1024 // Q_RANK, Q_RANK, row2(sp["q_lat_norm_g"]), w_uq_pad, F32, tm, 1024, "q_up")
    kv_raw, ckvn = _norm_linear(z, 1280 // KV_RANK, KV_RANK, row2(sp["kv_lat_norm_g"]), w_ukv, F32, tm, 1024, "kv_up")
    qp, kp, vp = _mla_prep(q_raw, kv_raw, z, cosf, sinf, gq_pad, gk_pad, tc)
    o_f, o_b, lse = _attn_fwd(qp, kp, vp, tb)
    (x1,) = _linear([(u, w_out_u), (o_b, w_out_o)], False, x, [F32], tm, 1024, "out_proj")

    qm, hq = _norm_linear(x1, 0, D_MODEL, row2(sp["mem_norm_x_g"]), wf["w_mem_q"], F32, tm, 1024, "memq_proj")
    kvm, hm = _norm_linear(mem, 0, D_MODEL, row2(sp["mem_norm_m_g"]), wf["w_mem_kv"], F32, MEM_LEN, 1024, "memkv_proj")
    gmq, gmk = row2(sp["mem_q_norm_g"]), row2(sp["mem_k_norm_g"])
    o_m = _memattn_fwd(qm, kvm, gmq, gmk, tm)
    (x2,) = _linear([(o_m, wf["w_mem_o"])], False, x1, [F32], tm, 1024, "memo_proj")

    up0, h3 = _norm_linear(x2, 0, D_MODEL, row2(sp["ffn_norm_g"]), wf["w_up"], F32, tm, D_FF // 2, "up_proj")
    act = _ffn_fwd(up0, w_ffn8, b_ffn, tc, D_FF // 2)
    dy_f, dy_b, lsum = _down_loss(act, wf["w_down"], x2, target, tm)

    g = {}
    (d_act,) = _linear([(dy_b, wf["w_down"])], True, None, [F32], tm, D_FF // 2, "down_bwd")
    g["w_down"] = _dw(act, dy_b, "dw_down")
    d_up0g, d_up0v, dbg, dbv, dwg, dwv = _ffn_bwd(d_act, up0, w_ffn8, b_ffn, tc, D_FF // 2)
    g["b_ffn_dw"] = jnp.concatenate([dbg, dbv], axis=1).reshape(-1)
    g["w_ffn_dw"] = jnp.concatenate([dwg[:3], dwv[:3]], axis=1)
    g["w_up"] = jnp.concatenate([_dw(h3, d_up0g, "dw_up_g"), _dw(h3, d_up0v, "dw_up_v")], axis=1)
    d_x2f, d_x2b, dg = _linear_normbwd([(d_up0g, w_up_g), (d_up0v, w_up_v)], x2, 0, row2(sp["ffn_norm_g"]), dy_f,
                                       [F32, BF16], tc, "up_bwd")
    g["ffn_norm_g"] = dg.reshape(-1)

    (d_om,) = _linear([(d_x2b, wf["w_mem_o"])], True, None, [BF16], tm, 1024, "memo_bwd")
    g["w_mem_o"] = _dw(o_m, d_x2b, "dw_mem_o")
    d_qm, d_km, d_vm, dgq, dgk = _memattn_bwd(qm, kvm, d_om, gmq, gmk, tm)
    g["mem_q_norm_g"], g["mem_k_norm_g"] = dgq.reshape(-1), dgk.reshape(-1)
    d_kvm = jnp.concatenate([d_km, d_vm], axis=1)
    g["w_mem_q"] = _dw(hq, d_qm, "dw_mem_q")
    g["w_mem_kv"] = _dw(hm, d_kvm, "dw_mem_kv")
    d_x1f, d_x1b, dg = _linear_normbwd([(d_qm, wf["w_mem_q"])], x1, 0, row2(sp["mem_norm_x_g"]), d_x2f, [F32, BF16], tm, "memq_bwd")
    g["mem_norm_x_g"] = dg.reshape(-1)
    _, dg = _linear_normbwd([(d_kvm, wf["w_mem_kv"])], mem, 0, row2(sp["mem_norm_m_g"]), None, [BF16], MEM_LEN, "memkv_bwd")
    g["mem_norm_m_g"] = dg.reshape(-1)

    (d_u,) = _linear([(d_x1b, w_out_u)], True, None, [F32], tm, CONV_CH, "out_bwd_u")
    d_of, d_ob = _linear([(d_x1b, w_out_o)], True, None, [F32, BF16], tm, 1024, "out_bwd_o")
    dw_out_u = _dw(u, d_x1b, "dw_out_u")
    dw_out_o = _dw(o_b, d_x1b, "dw_out_o")
    g["w_out"] = jnp.concatenate([dw_out_u, dw_out_o.reshape(HEADS, HEAD_PAD, D_MODEL)[:, NOPE:].reshape(HEADS * NOPE, D_MODEL)], axis=0)
    delta = _attn_delta(d_of, o_f, tb)
    dqp, dkp, dvp = _attn_bwd(qp, kp, vp, d_ob, lse, delta, tb)
    d_qraw, d_kvraw, d_kr, dgq, dgk = _mla_prep_bwd(dqp, dkp, dvp, q_raw, kv_raw, z, cosf, sinf, gq_pad, gk_pad, tc)
    g["q_norm_g"], g["k_norm_g"] = dgq.reshape(-1)[:HEAD_DIM], dgk.reshape(-1)[:HEAD_DIM]
    g["w_uq"] = _dw(cqn, d_qraw, "dw_uq").reshape(Q_RANK, HEADS, HEAD_PAD)[:, :, :HEAD_DIM].reshape(Q_RANK, HEADS * HEAD_DIM)
    g["w_ukv"] = _dw(ckvn, d_kvraw, "dw_ukv")
    d_cq, dg = _linear_normbwd([(d_qraw, w_uq_pad)], z, 1024 // Q_RANK, row2(sp["q_lat_norm_g"]), None, [BF16], tm, "q_up_bwd")
    g["q_lat_norm_g"] = dg.reshape(-1)
    d_ckv, dg = _linear_normbwd([(d_kvraw, w_ukv)], z, 1280 // KV_RANK, row2(sp["kv_lat_norm_g"]), None, [BF16], tm, "kv_up_bwd")
    g["kv_lat_norm_g"] = dg.reshape(-1)
    d_u1, dlg, dlb, dbdw = _conv_bwd_ln(d_u, u1, row2(sp["conv_ln_g"]), row2(sp["conv_ln_b"]), tc)
    g["conv_ln_g"], g["conv_ln_b"], g["b_conv_dw"] = dlg.reshape(-1), dlb.reshape(-1), dbdw.reshape(-1)
    d_conv, dw_dw, dbin = _conv_bwd_dw(d_u1, u0, z, row2(sp["b_conv_in"]), w_dw32, tc)
    g["w_conv_dw"], g["b_conv_in"] = dw_dw[:CONV_WIDTH], dbin.reshape(-1)
    pieces = [(d_conv, w_in_pad[:, :1024]), (d_cq, w_in_pad[:, 1024:1280]), (d_ckv, w_in_pad[:, 1280:1408]), (d_kr, w_in_pad[:, 1408:])]
    dw_in = [_dw(h1, d, "dw_in_%d" % k) for k, (d, _) in enumerate(pieces)]
    g["w_in"] = jnp.concatenate([dw_in[0], dw_in[1], dw_in[2], dw_in[3][:, NOPE:HEAD_DIM]], axis=1)
    grad_x, dg = _linear_normbwd(pieces, x, 0, row2(sp["mix_norm_g"]), d_x1f, [F32], tm, "in_bwd")
    g["mix_norm_g"] = dg.reshape(-1)
    return lsum[0, 0], grad_x, g


def kernel(x, mem, positions, mix_norm_g, w_in, b_conv_in, w_conv_dw, b_conv_dw, conv_ln_g, conv_ln_b, q_lat_norm_g, w_uq, kv_lat_norm_g, w_ukv, q_norm_g, k_norm_g, w_out, mem_norm_x_g, mem_norm_m_g, w_mem_q, w_mem_kv, mem_q_norm_g, mem_k_norm_g, w_mem_o, ffn_norm_g, w_up, w_ffn_dw, b_ffn_dw, w_down, loss_target, m_mix_norm_g, m_w_in, m_b_conv_in, m_w_conv_dw, m_b_conv_dw, m_conv_ln_g, m_conv_ln_b, m_q_lat_norm_g, m_w_uq, m_kv_lat_norm_g, m_w_ukv, m_q_norm_g, m_k_norm_g, m_w_out, m_mem_norm_x_g, m_mem_norm_m_g, m_w_mem_q, m_w_mem_kv, m_mem_q_norm_g, m_mem_k_norm_g, m_w_mem_o, m_ffn_norm_g, m_w_up, m_w_ffn_dw, m_b_ffn_dw, m_w_down, v_mix_norm_g, v_w_in, v_b_conv_in, v_w_conv_dw, v_b_conv_dw, v_conv_ln_g, v_conv_ln_b, v_q_lat_norm_g, v_w_uq, v_kv_lat_norm_g, v_w_ukv, v_q_norm_g, v_k_norm_g, v_w_out, v_mem_norm_x_g, v_mem_norm_m_g, v_w_mem_q, v_w_mem_kv, v_mem_q_norm_g, v_mem_k_norm_g, v_w_mem_o, v_ffn_norm_g, v_w_up, v_w_ffn_dw, v_b_ffn_dw, v_w_down):
    names = ["mix_norm_g", "w_in", "b_conv_in", "w_conv_dw", "b_conv_dw", "conv_ln_g", "conv_ln_b", "q_lat_norm_g", "w_uq",
             "kv_lat_norm_g", "w_ukv", "q_norm_g", "k_norm_g", "w_out", "mem_norm_x_g", "mem_norm_m_g", "w_mem_q", "w_mem_kv",
             "mem_q_norm_g", "mem_k_norm_g", "w_mem_o", "ffn_norm_g", "w_up", "w_ffn_dw", "b_ffn_dw", "w_down"]
    loc = locals()
    w = {n: loc[n] for n in names}
    m = {n: loc["m_" + n] for n in names}
    v = {n: loc["v_" + n] for n in names}
    shard_idx = 2 * lax.axis_index("x") + lax.axis_index("y")

    w_packed = _pack_big_shards({n: w[n][0] for n, _, _ in BIG})
    w_wire = w_packed.astype(BF16)
    gathered = lax.dynamic_update_index_in_dim(_ag_weights(w_wire), w_wire, shard_idx, 0)
    wf = _unpack_gathered(gathered)

    small_sh_full = {}
    gather_in = []
    for n, (r, c) in SMALL_SH:
        csh = c // 4
        slab = lax.dynamic_update_slice(jnp.zeros((r, c), F32), w[n][0], (0, shard_idx * csh))
        gather_in.append(slab.reshape(-1))
    gather_rows = 256
    gathered_small = _allreduce_small_named(_pack_rows(gather_in, gather_rows, SMALL_COLS), "gather_small") * 0.5
    off = 0
    for n, (r, c) in SMALL_SH:
        small_sh_full[n] = gathered_small.reshape(-1)[off:off + r * c].reshape(r, c)
        off += r * c
    sp = {n: w[n][0] for n, _ in SMALL_REP}
    sp.update(small_sh_full)

    lsum, grad_x, g = _layer_grads(x[0], mem[0], positions[0], loss_target[0], wf, sp)

    small_parts = [jnp.full((SMALL_COLS,), lsum, F32)] + [g[n] for n, _ in SMALL_REP] + [g[n] for n, _ in SMALL_SH]
    small_rows = 368
    small_sum = _allreduce_small_named(_pack_rows(small_parts, small_rows, SMALL_COLS), "allreduce_small").reshape(-1)
    loss = small_sum[0] * (0.5 / D_MODEL)
    gs = {}
    off = SMALL_COLS
    for n, sz in SMALL_REP:
        gs[n] = small_sum[off:off + sz].reshape(w[n].shape)
        off += sz
    for n, (r, c) in SMALL_SH:
        full = small_sum[off:off + r * c].reshape(r, c)
        gs[n] = lax.dynamic_slice(full, (0, shard_idx * (c // 4)), (r, c // 4)).reshape(w[n].shape)
        off += r * c

    gfull = _pack_full_grads(g)
    core_idx = lax.axis_index("c").astype(jnp.int32)
    chipsum, chipsum_wire = _rs_add_pair(gfull, _rs_swap_halves(gfull), core_idx.reshape(1), 240)
    red = _rs_add_chips(chipsum, _rs_to_owner(chipsum_wire), jnp.stack([shard_idx.astype(jnp.int32), core_idx]), 240)
    g_packed = _rs_join_halves(red).reshape(PACK_ROWS, PACK_COLS)
    g_big = _unpack_big_shards(g_packed)
    gs.update({n: a[None] for n, a in g_big.items()})

    delta, new_m, new_v = {}, {}, {}
    for n, _, _ in BIG:
        d_n, m_n, v_n = _adamw(w[n][0], g_big[n], m[n][0], v[n][0], "adamw_" + n)
        delta[n], new_m[n], new_v[n] = d_n[None], m_n[None], v_n[None]
    small_names = [n for n, _ in SMALL_REP] + [n for n, _ in SMALL_SH]
    as2d = lambda a: a.reshape(-1, a.shape[-1])
    d_s, m_s, v_s = _adamw_small(*[[as2d(d[n]) for n in small_names] for d in (w, gs, m, v)])
    for k, n in enumerate(small_names):
        delta[n], new_m[n], new_v[n] = d_s[k].reshape(w[n].shape), m_s[k].reshape(w[n].shape), v_s[k].reshape(w[n].shape)

    return (loss, grad_x[None], *[gs[n] for n in names], *[delta[n] for n in names], *[new_m[n] for n in names],
            *[new_v[n] for n in names])


def _allreduce_small_named(v, name):
    rows, cols = v.shape

    def body(v_ref, out_ref, buf, send_sems, recv_sems):
        x, y, c = _coords()
        me = 4 * x + 2 * y + c
        buf[me] = v_ref[...]
        cps = []
        for r in range(1, 8):
            dx, dy, dc = (r >> 2) & 1, (r >> 1) & 1, r & 1
            to = (x + dx - 2 * x * dx, y + dy - 2 * y * dy, c + dc - 2 * c * dc)
            cp = pltpu.make_async_remote_copy(src_ref=v_ref, dst_ref=buf.at[me], send_sem=send_sems.at[r - 1],
                                              recv_sem=recv_sems.at[r - 1], device_id=to, device_id_type=MESH)
            cp.start()
            cps.append(cp)
        for cp in cps:
            cp.wait()
        acc = buf[0]
        for d in range(1, 8):
            acc = acc + buf[d]
        out_ref[...] = acc

    vm = pl.BlockSpec(memory_space=pltpu.VMEM)
    return pl.pallas_call(
        body, in_specs=[vm], out_specs=vm, out_shape=SDS((rows, cols), F32),
        scratch_shapes=[pltpu.VMEM((8, rows, cols), F32), pltpu.SemaphoreType.DMA((7,)), pltpu.SemaphoreType.DMA((7,))],
        name=name)(v)
```

```python
import math

import numpy as np
import jax
import jax.numpy as jnp
from jax import lax
from jax.experimental import pallas as pl
from jax.experimental.pallas import tpu as pltpu

F32 = jnp.float32
BF16 = jnp.bfloat16
SDS = jax.ShapeDtypeStruct
MESH = pl.DeviceIdType.MESH

D_MODEL = 1024
EPS = 1e-6
CONV_CH = 512
CONV_WIDTH = 31
CONV_HALO = 32
HEADS = 8
NOPE = 64
ROPE = 32
HEAD_DIM = NOPE + ROPE
HEAD_PAD = 128
Q_RANK = 256
KV_RANK = 128
CHUNK = 64
ROPE_THETA = 10000.0
IN_COLS_PAD = 1536
MEM_HEADS = 4
MEM_HEAD_DIM = 256
MEM_LEN = 256
D_FF = 2816
FFN_HALO = 8
ATT_SCALE = 1.0 / math.sqrt(HEAD_DIM)
LOG2E = math.log2(math.e)
LN2 = math.log(2.0)

ADAM_LR = 0.001
ADAM_B1 = 0.9
ADAM_B2 = 0.999
ADAM_EPS = 1e-08
ADAM_WD = 0.01
ADAM_STEP = 10

VMEM_LIMIT_V7X = 56 * 1024 * 1024
PACK_COLS = 1024
PACK_ROWS = 3840
SMALL_COLS = 128


def _cp(n_axes):
    return pltpu.CompilerParams(dimension_semantics=("arbitrary",) * n_axes, vmem_limit_bytes=VMEM_LIMIT_V7X)


def _row_tile(s, want):
    return want if s % want == 0 else s


def _norm_linear(x, xcol, kdim, g, w, out_dtype, tm, tn, name):
    s = x.shape[0]
    n = w.shape[1]

    def body(x_ref, g_ref, w_ref, y_ref, hn_ref):
        @pl.when(pl.program_id(1) == 0)
        def _():
            xv = x_ref[...]
            r = lax.rsqrt(jnp.mean(xv * xv, axis=-1, keepdims=True) + EPS)
            hn_ref[...] = ((xv * r) * g_ref[...]).astype(BF16)

        y_ref[...] = jnp.dot(hn_ref[...], w_ref[...], preferred_element_type=F32).astype(y_ref.dtype)

    return pl.pallas_call(
        body, grid=(s // tm, n // tn),
        in_specs=[pl.BlockSpec((tm, kdim), lambda i, j: (i, xcol)), pl.BlockSpec((1, kdim), lambda i, j: (0, 0)),
                  pl.BlockSpec((kdim, tn), lambda i, j: (0, j))],
        out_specs=[pl.BlockSpec((tm, tn), lambda i, j: (i, j)), pl.BlockSpec((tm, kdim), lambda i, j: (i, 0))],
        out_shape=[SDS((s, n), out_dtype), SDS((s, kdim), BF16)],
        compiler_params=_cp(2), name=name)(x, g, w)


def _linear(pairs, nt, residual, out_dtypes, tm, tn, name):
    s = pairs[0][0].shape[0]
    n = pairs[0][1].shape[0] if nt else pairs[0][1].shape[1]
    n_pairs = len(pairs)
    has_res = residual is not None

    def body(*refs):
        a_refs = refs[:n_pairs]
        w_refs = refs[n_pairs:2 * n_pairs]
        res_ref = refs[2 * n_pairs] if has_res else None
        outs = refs[2 * n_pairs + int(has_res):]
        acc = None
        for a_ref, w_ref in zip(a_refs, w_refs):
            a = a_ref[...].astype(BF16)
            if nt:
                d = lax.dot_general(a, w_ref[...], (((1,), (1,)), ((), ())), preferred_element_type=F32)
            else:
                d = jnp.dot(a, w_ref[...], preferred_element_type=F32)
            acc = d if acc is None else acc + d
        if has_res:
            acc = res_ref[...] + acc
        for o in outs:
            o[...] = acc.astype(o.dtype)

    in_specs = [pl.BlockSpec((tm, a.shape[1]), lambda i, j: (i, 0)) for a, _ in pairs]
    if nt:
        in_specs += [pl.BlockSpec((tn, w.shape[1]), lambda i, j: (j, 0)) for _, w in pairs]
    else:
        in_specs += [pl.BlockSpec((w.shape[0], tn), lambda i, j: (0, j)) for _, w in pairs]
    args = [a for a, _ in pairs] + [w for _, w in pairs]
    if has_res:
        in_specs.append(pl.BlockSpec((tm, tn), lambda i, j: (i, j)))
        args.append(residual)
    outs = pl.pallas_call(
        body, grid=(s // tm, n // tn), in_specs=in_specs,
        out_specs=[pl.BlockSpec((tm, tn), lambda i, j: (i, j)) for _ in out_dtypes],
        out_shape=[SDS((s, n), dt) for dt in out_dtypes],
        compiler_params=_cp(2), name=name)(*args)
    return outs


def _linear_normbwd(pairs, x, xcol, g, d_res, out_dtypes, tm, name):
    s = pairs[0][0].shape[0]
    dn = pairs[0][1].shape[0]
    n_pairs = len(pairs)
    has_res = d_res is not None

    def body(*refs):
        a_refs = refs[:n_pairs]
        w_refs = refs[n_pairs:2 * n_pairs]
        x_ref, g_ref = refs[2 * n_pairs], refs[2 * n_pairs + 1]
        k = 2 * n_pairs + 2
        res_ref = refs[k] if has_res else None
        k += int(has_res)
        outs = refs[k:-1]
        dg_ref = refs[-1]
        dh = None
        for a_ref, w_ref in zip(a_refs, w_refs):
            d = lax.dot_general(a_ref[...].astype(BF16), w_ref[...], (((1,), (1,)), ((), ())), preferred_element_type=F32)
            dh = d if dh is None else dh + d
        xv = x_ref[...]
        r = lax.rsqrt(jnp.mean(xv * xv, axis=-1, keepdims=True) + EPS)
        y = xv * r

        @pl.when(pl.program_id(0) == 0)
        def _():
            dg_ref[...] = jnp.zeros_like(dg_ref)

        dg_ref[...] += jnp.sum(dh * y, axis=0, keepdims=True)
        dy = dh * g_ref[...]
        dx = r * (dy - y * jnp.mean(dy * y, axis=-1, keepdims=True))
        if has_res:
            dx = res_ref[...] + dx
        for o in outs:
            o[...] = dx.astype(o.dtype)

    in_specs = [pl.BlockSpec((tm, a.shape[1]), lambda i: (i, 0)) for a, _ in pairs]
    in_specs += [pl.BlockSpec((dn, w.shape[1]), lambda i: (0, 0)) for _, w in pairs]
    in_specs += [pl.BlockSpec((tm, dn), lambda i: (i, xcol)), pl.BlockSpec((1, dn), lambda i: (0, 0))]
    args = [a for a, _ in pairs] + [w for _, w in pairs] + [x, g]
    if has_res:
        in_specs.append(pl.BlockSpec((tm, dn), lambda i: (i, 0)))
        args.append(d_res)
    outs = pl.pallas_call(
        body, grid=(s // tm,), in_specs=in_specs,
        out_specs=[pl.BlockSpec((tm, dn), lambda i: (i, 0)) for _ in out_dtypes] + [pl.BlockSpec((1, dn), lambda i: (0, 0))],
        out_shape=[SDS((s, dn), dt) for dt in out_dtypes] + [SDS((1, dn), F32)],
        compiler_params=_cp(1), name=name)(*args)
    return outs


def _dw_matmul(a, b, tk, tn, ts, name):
    s, ka = a.shape
    n = b.shape[1]

    def body(a_ref, b_ref, o_ref):
        @pl.when(pl.program_id(2) == 0)
        def _():
            o_ref[...] = jnp.zeros_like(o_ref)

        o_ref[...] += lax.dot_general(a_ref[...].astype(BF16), b_ref[...].astype(BF16), (((0,), (0,)), ((), ())),
                                      preferred_element_type=F32)

    return pl.pallas_call(
        body, grid=(ka // tk, n // tn, s // ts),
        in_specs=[pl.BlockSpec((ts, tk), lambda k, j, t: (t, k)), pl.BlockSpec((ts, tn), lambda k, j, t: (t, j))],
        out_specs=pl.BlockSpec((tk, tn), lambda k, j, t: (k, j)),
        out_shape=SDS((ka, n), F32), compiler_params=_cp(3), name=name)(a, b)


def _dw(a, b, name):
    s, ka = a.shape
    n = b.shape[1]
    tk = ka if ka <= 1024 else ka // 2
    tn = n if n <= 1024 else (n // 2 if n == D_FF else 512)
    return _dw_matmul(a, b, tk, tn, _row_tile(s, 2048), name)


def _prev_halo(tm, halo):
    return lambda i: (jnp.maximum(i * (tm // halo) - 1, 0), 0)


def _next_halo(tm, halo, s):
    return lambda i: (jnp.minimum((i + 1) * (tm // halo), s // halo - 1), 0)


def _conv_fwd(z, b_in, w32, b_dw, ln_g, ln_b, tm):
    s = z.shape[0]
    c = CONV_CH

    def body(z_ref, zh_ref, bin_ref, w_ref, bdw_ref, lg_ref, lb_ref, u_ref, u0_ref, u1_ref, ext):
        i = pl.program_id(0)

        def glu(zz):
            zz = zz + bin_ref[...]
            return zz[:, :c] * jax.nn.sigmoid(zz[:, c:])

        u0 = glu(z_ref[...])
        u0_ref[...] = u0
        ext[0:CONV_HALO, :] = jnp.where(i > 0, glu(zh_ref[...]), 0.0)
        ext[CONV_HALO:, :] = u0
        off = CONV_HALO - (CONV_WIDTH - 1)
        for r in range(tm // 64):
            for cb in range(c // 128):
                cs = slice(cb * 128, (cb + 1) * 128)
                acc = jnp.zeros((64, 128), F32)
                for k in range(CONV_WIDTH):
                    acc = acc + ext[r * 64 + off + k: r * 64 + off + k + 64, cs] * w_ref[k:k + 1, cs]
                u1_ref[r * 64:(r + 1) * 64, cs] = acc + bdw_ref[:, cs]
        u1 = u1_ref[...]
        mu = jnp.mean(u1, axis=-1, keepdims=True)
        xc = u1 - mu
        y = xc * lax.rsqrt(jnp.mean(xc * xc, axis=-1, keepdims=True) + EPS)
        y = y * lg_ref[...] + lb_ref[...]
        u_ref[...] = (y * jax.nn.sigmoid(y)).astype(BF16)

    row = lambda i: (i, 0)
    fix = lambda i: (0, 0)
    return pl.pallas_call(
        body, grid=(s // tm,),
        in_specs=[pl.BlockSpec((tm, 2 * c), row), pl.BlockSpec((CONV_HALO, 2 * c), _prev_halo(tm, CONV_HALO)),
                  pl.BlockSpec((1, 2 * c), fix), pl.BlockSpec((32, c), fix), pl.BlockSpec((1, c), fix),
                  pl.BlockSpec((1, c), fix), pl.BlockSpec((1, c), fix)],
        out_specs=[pl.BlockSpec((tm, c), row)] * 3,
        out_shape=[SDS((s, c), BF16), SDS((s, c), F32), SDS((s, c), F32)],
        scratch_shapes=[pltpu.VMEM((tm + CONV_HALO, c), F32)],
        compiler_params=_cp(1), name="conv_fwd")(z, z, b_in, w32, b_dw, ln_g, ln_b)


def _conv_bwd_ln(d_u, u1, ln_g, ln_b, tm):
    s = d_u.shape[0]
    c = CONV_CH

    def body(du_ref, u1_ref, lg_ref, lb_ref, du1_ref, dlg_ref, dlb_ref, dbdw_ref):
        @pl.when(pl.program_id(0) == 0)
        def _():
            dlg_ref[...] = jnp.zeros_like(dlg_ref)
            dlb_ref[...] = jnp.zeros_like(dlb_ref)
            dbdw_ref[...] = jnp.zeros_like(dbdw_ref)

        u1 = u1_ref[...]
        mu = jnp.mean(u1, axis=-1, keepdims=True)
        xc = u1 - mu
        rs = lax.rsqrt(jnp.mean(xc * xc, axis=-1, keepdims=True) + EPS)
        xh = xc * rs
        y = xh * lg_ref[...] + lb_ref[...]
        sg = jax.nn.sigmoid(y)
        dy = du_ref[...] * (sg * (1.0 + y * (1.0 - sg)))
        dlg_ref[...] += jnp.sum(dy * xh, axis=0, keepdims=True)
        dlb_ref[...] += jnp.sum(dy, axis=0, keepdims=True)
        dxh = dy * lg_ref[...]
        du1 = rs * (dxh - jnp.mean(dxh, axis=-1, keepdims=True) - xh * jnp.mean(dxh * xh, axis=-1, keepdims=True))
        dbdw_ref[...] += jnp.sum(du1, axis=0, keepdims=True)
        du1_ref[...] = du1

    row = lambda i: (i, 0)
    fix = lambda i: (0, 0)
    return pl.pallas_call(
        body, grid=(s // tm,),
        in_specs=[pl.BlockSpec((tm, c), row), pl.BlockSpec((tm, c), row), pl.BlockSpec((1, c), fix), pl.BlockSpec((1, c), fix)],
        out_specs=[pl.BlockSpec((tm, c), row)] + [pl.BlockSpec((1, c), fix)] * 3,
        out_shape=[SDS((s, c), F32)] + [SDS((1, c), F32)] * 3,
        compiler_params=_cp(1), name="conv_bwd_ln")(d_u, u1, ln_g, ln_b)


def _conv_bwd_dw(d_u1, u0, z, b_in, w32, tm):
    s = d_u1.shape[0]
    c = CONV_CH

    def body(d_ref, dn_ref, u0_ref, u0p_ref, z_ref, bin_ref, w_ref, dz_ref, dw_ref, dbin_ref, extd, extu, du0):
        i = pl.program_id(0)
        last = pl.num_programs(0) - 1

        @pl.when(i == 0)
        def _():
            dw_ref[...] = jnp.zeros_like(dw_ref)
            dbin_ref[...] = jnp.zeros_like(dbin_ref)

        extd[0:tm, :] = d_ref[...]
        extd[tm:, :] = jnp.where(i < last, dn_ref[...], 0.0)
        extu[0:CONV_HALO, :] = jnp.where(i > 0, u0p_ref[...], 0.0)
        extu[CONV_HALO:, :] = u0_ref[...]
        off = CONV_HALO - (CONV_WIDTH - 1)
        for r in range(tm // 64):
            for cb in range(c // 128):
                cs = slice(cb * 128, (cb + 1) * 128)
                acc = jnp.zeros((64, 128), F32)
                for k in range(CONV_WIDTH):
                    o = r * 64 + (CONV_WIDTH - 1) - k
                    acc = acc + extd[o:o + 64, cs] * w_ref[k:k + 1, cs]
                du0[r * 64:(r + 1) * 64, cs] = acc
        for cb in range(c // 128):
            cs = slice(cb * 128, (cb + 1) * 128)
            for k in range(CONV_WIDTH):
                part = jnp.zeros((8, 128), F32)
                for r in range(tm // 64):
                    p = d_ref[r * 64:(r + 1) * 64, cs] * extu[r * 64 + off + k: r * 64 + off + k + 64, cs]
                    for q in range(8):
                        part = part + p[q * 8:(q + 1) * 8, :]
                dw_ref[k:k + 1, cs] += jnp.sum(part, axis=0, keepdims=True)
        zz = z_ref[...] + bin_ref[...]
        a = zz[:, :c]
        sg = jax.nn.sigmoid(zz[:, c:])
        d0 = du0[...]
        da = d0 * sg
        dgt = d0 * a * (sg * (1.0 - sg))
        dbin_ref[:, :c] += jnp.sum(da, axis=0, keepdims=True)
        dbin_ref[:, c:] += jnp.sum(dgt, axis=0, keepdims=True)
        dz_ref[:, :c] = da.astype(BF16)
        dz_ref[:, c:] = dgt.astype(BF16)

    row = lambda i: (i, 0)
    fix = lambda i: (0, 0)
    return pl.pallas_call(
        body, grid=(s // tm,),
        in_specs=[pl.BlockSpec((tm, c), row), pl.BlockSpec((CONV_HALO, c), _next_halo(tm, CONV_HALO, s)),
                  pl.BlockSpec((tm, c), row), pl.BlockSpec((CONV_HALO, c), _prev_halo(tm, CONV_HALO)),
                  pl.BlockSpec((tm, 2 * c), row), pl.BlockSpec((1, 2 * c), fix), pl.BlockSpec((32, c), fix)],
        out_specs=[pl.BlockSpec((tm, 2 * c), row), pl.BlockSpec((32, c), fix), pl.BlockSpec((1, 2 * c), fix)],
        out_shape=[SDS((s, 2 * c), BF16), SDS((32, c), F32), SDS((1, 2 * c), F32)],
        scratch_shapes=[pltpu.VMEM((tm + CONV_HALO, c), F32), pltpu.VMEM((tm + CONV_HALO, c), F32), pltpu.VMEM((tm, c), F32)],
        compiler_params=_cp(1), name="conv_bwd_dw")(d_u1, d_u1, u0, u0, z, b_in, w32)


def _partner(v, lane):
    up = pltpu.roll(v, HEAD_PAD - ROPE // 2, 1)
    dn = pltpu.roll(v, ROPE // 2, 1)
    lo = (lane >= NOPE) & (lane < NOPE + ROPE // 2)
    hi = (lane >= NOPE + ROPE // 2) & (lane < HEAD_DIM)
    return jnp.where(lo, up, jnp.where(hi, dn, 0.0))


def _mla_prep(q_raw, kv_raw, z, cosf, sinf, gq, gk, tm):
    s = q_raw.shape[0]

    def body(q_ref, kv_ref, kr_ref, c_ref, s_ref, gq_ref, gk_ref, qo_ref, ko_ref, vo_ref):
        lane = lax.broadcasted_iota(jnp.int32, (tm, HEAD_PAD), 1)
        cf = c_ref[...]
        sf = s_ref[...]

        def norm_rope(t, g_ref):
            r = lax.rsqrt(jnp.sum(t * t, axis=-1, keepdims=True) * (1.0 / HEAD_DIM) + EPS)
            tn = (t * r) * g_ref[...]
            return tn * cf + _partner(tn, lane) * sf

        kr = kr_ref[...]
        for h in range(HEADS):
            hs = slice(h * HEAD_PAD, (h + 1) * HEAD_PAD)
            qo_ref[:, hs] = (norm_rope(q_ref[:, hs], gq_ref) * (ATT_SCALE * LOG2E)).astype(BF16)
            kv = kv_ref[:, hs]
            ko_ref[:, hs] = norm_rope(jnp.where(lane < NOPE, kv, 0.0) + kr, gk_ref).astype(BF16)
            vo_ref[:, hs] = jnp.where(lane >= NOPE, kv, 0.0).astype(BF16)

    row = lambda i: (i, 0)
    wide = pl.BlockSpec((tm, HEADS * HEAD_PAD), row)
    one = pl.BlockSpec((tm, HEAD_PAD), row)
    return pl.pallas_call(
        body, grid=(s // tm,),
        in_specs=[wide, wide, pl.BlockSpec((tm, HEAD_PAD), lambda i: (i, IN_COLS_PAD // HEAD_PAD - 1)), one, one,
                  pl.BlockSpec((1, HEAD_PAD), lambda i: (0, 0)), pl.BlockSpec((1, HEAD_PAD), lambda i: (0, 0))],
        out_specs=[wide] * 3,
        out_shape=[SDS((s, HEADS * HEAD_PAD), BF16)] * 3,
        compiler_params=_cp(1), name="mla_prep")(q_raw, kv_raw, z, cosf, sinf, gq, gk)


def _mla_prep_bwd(dqp, dkp, dvp, q_raw, kv_raw, z, cosf, sinf, gq, gk, tm):
    s = q_raw.shape[0]

    def body(dq_ref, dk_ref, dv_ref, q_ref, kv_ref, kr_ref, c_ref, s_ref, gq_ref, gk_ref,
             dqo_ref, dkvo_ref, dkr_ref, dgq_ref, dgk_ref):
        lane = lax.broadcasted_iota(jnp.int32, (tm, HEAD_PAD), 1)
        cf = c_ref[...]
        sf = s_ref[...]

        @pl.when(pl.program_id(0) == 0)
        def _():
            dgq_ref[...] = jnp.zeros_like(dgq_ref)
            dgk_ref[...] = jnp.zeros_like(dgk_ref)

        def norm_rope_bwd(t, d_out, g_ref, dg_ref):
            r = lax.rsqrt(jnp.sum(t * t, axis=-1, keepdims=True) * (1.0 / HEAD_DIM) + EPS)
            th = t * r
            dn = d_out * cf + _partner(d_out * sf, lane)
            dg_ref[...] += jnp.sum(dn * th, axis=0, keepdims=True)
            dh = dn * g_ref[...]
            return r * (dh - th * (jnp.sum(dh * th, axis=-1, keepdims=True) * (1.0 / HEAD_DIM)))

        kr = kr_ref[...]
        dkr = None
        for h in range(HEADS):
            hs = slice(h * HEAD_PAD, (h + 1) * HEAD_PAD)
            dq = norm_rope_bwd(q_ref[:, hs], dq_ref[:, hs] * ATT_SCALE, gq_ref, dgq_ref)
            dqo_ref[:, hs] = dq.astype(BF16)
            kv = kv_ref[:, hs]
            dkpre = norm_rope_bwd(jnp.where(lane < NOPE, kv, 0.0) + kr, dk_ref[:, hs] * LN2, gk_ref, dgk_ref)
            dkvo_ref[:, hs] = jnp.where(lane < NOPE, dkpre, dv_ref[:, hs]).astype(BF16)
            dkr_h = jnp.where((lane >= NOPE) & (lane < HEAD_DIM), dkpre, 0.0)
            dkr = dkr_h if dkr is None else dkr + dkr_h
        dkr_ref[...] = dkr

    row = lambda i: (i, 0)
    fix = lambda i: (0, 0)
    wide = pl.BlockSpec((tm, HEADS * HEAD_PAD), row)
    one = pl.BlockSpec((tm, HEAD_PAD), row)
    return pl.pallas_call(
        body, grid=(s // tm,),
        in_specs=[wide, wide, wide, wide, wide, pl.BlockSpec((tm, HEAD_PAD), lambda i: (i, IN_COLS_PAD // HEAD_PAD - 1)),
                  one, one, pl.BlockSpec((1, HEAD_PAD), fix), pl.BlockSpec((1, HEAD_PAD), fix)],
        out_specs=[wide, wide, one, pl.BlockSpec((1, HEAD_PAD), fix), pl.BlockSpec((1, HEAD_PAD), fix)],
        out_shape=[SDS((s, HEADS * HEAD_PAD), BF16), SDS((s, HEADS * HEAD_PAD), BF16), SDS((s, HEAD_PAD), F32),
                   SDS((1, HEAD_PAD), F32), SDS((1, HEAD_PAD), F32)],
        compiler_params=_cp(1), name="mla_prep_bwd")(dqp, dkp, dvp, q_raw, kv_raw, z, cosf, sinf, gq, gk)


def _pair_schedule(nb, forward):
    one, two, case = [], [], []
    for a in range(nb):
        for b in (range(a // 2 + 1) if forward else range(a // 2, nb // 2)):
            one.append(a)
            two.append(b)
            case.append(0 if b != a // 2 else 1 + a % 2)
    return tuple(jnp.asarray(np.array(x, np.int32)) for x in (one, two, case))


STRIP = 64


def _fold8(x):
    acc = x[0:8, :]
    for g in range(1, x.shape[0] // 8):
        acc = acc + x[g * 8:(g + 1) * 8, :]
    return acc


def _attn_fwd(qp, kp, vp, tb):
    s = qp.shape[0]
    ii, jj, cc = _pair_schedule(s // tb, True)

    def body(ii_ref, jj_ref, cc_ref, q_ref, k_ref, v_ref, of_ref, ob_ref, lse_ref, m_sc, l_sc, acc_sc, st_sc, pt_sc):
        t = pl.program_id(1)
        case = cc_ref[t]

        @pl.when(jj_ref[t] == 0)
        def _():
            m_sc[...] = jnp.full_like(m_sc, -jnp.inf)
            l_sc[...] = jnp.zeros_like(l_sc)
            acc_sc[...] = jnp.zeros_like(acc_sc)

        def step(n_keys, diag_at):
            def visible(r):
                if diag_at is None or r * STRIP <= diag_at:
                    return None
                col = lax.broadcasted_iota(jnp.int32, (STRIP, tb), 1)
                return col >= r * STRIP - diag_at

            st_sc[0:n_keys, :] = lax.dot_general(k_ref[0:n_keys, :], q_ref[...], (((1,), (1,)), ((), ())), preferred_element_type=F32)
            mx = None
            for r in range(n_keys // STRIP):
                sc = st_sc[r * STRIP:(r + 1) * STRIP, :]
                if visible(r) is not None:
                    sc = jnp.where(visible(r), sc, -jnp.inf)
                m8 = sc[0:8, :]
                for g in range(1, STRIP // 8):
                    m8 = jnp.maximum(m8, sc[g * 8:(g + 1) * 8, :])
                mx = m8 if mx is None else jnp.maximum(mx, m8)
            m_old = m_sc[0:1, :]
            m_new = jnp.maximum(m_old, jnp.max(mx, axis=0, keepdims=True))
            alpha = jnp.exp2(m_old - m_new)
            ps = None
            for r in range(n_keys // STRIP):
                p = jnp.exp2(st_sc[r * STRIP:(r + 1) * STRIP, :] - m_new)
                if visible(r) is not None:
                    p = jnp.where(visible(r), p, 0.0)
                ps = _fold8(p) if ps is None else ps + _fold8(p)
                pt_sc[r * STRIP:(r + 1) * STRIP, :] = p.astype(BF16)
            l_new = alpha * l_sc[0:1, :] + jnp.sum(ps, axis=0, keepdims=True)
            m_sc[...] = jnp.broadcast_to(m_new, m_sc.shape)
            l_sc[...] = jnp.broadcast_to(l_new, l_sc.shape)
            pv = lax.dot_general(v_ref[0:n_keys, :], pt_sc[0:n_keys, :], (((0,), (0,)), ((), ())), preferred_element_type=F32)
            acc_sc[...] = alpha * acc_sc[...] + pv

        @pl.when(case == 0)
        def _():
            step(2 * tb, None)

        @pl.when(case == 1)
        def _():
            step(tb, 0)

        @pl.when(case == 2)
        def _():
            step(2 * tb, tb)

        @pl.when(case != 0)
        def _():
            l = l_sc[0:1, :]
            o = (acc_sc[...] / l).T
            of_ref[...] = o
            ob_ref[...] = o.astype(BF16)
            lse_ref[...] = m_sc[0:1, :] + jnp.log(l) * LOG2E

    qmap = lambda h, t, ii_ref, jj_ref, cc_ref: (ii_ref[t], h)
    kmap = lambda h, t, ii_ref, jj_ref, cc_ref: (jj_ref[t], h)
    gs = pltpu.PrefetchScalarGridSpec(
        num_scalar_prefetch=3, grid=(HEADS, int(ii.shape[0])),
        in_specs=[pl.BlockSpec((tb, HEAD_PAD), qmap), pl.BlockSpec((2 * tb, HEAD_PAD), kmap), pl.BlockSpec((2 * tb, HEAD_PAD), kmap)],
        out_specs=[pl.BlockSpec((tb, HEAD_PAD), qmap), pl.BlockSpec((tb, HEAD_PAD), qmap),
                   pl.BlockSpec((None, 1, tb), lambda h, t, ii_ref, jj_ref, cc_ref: (h, 0, ii_ref[t]))],
        scratch_shapes=[pltpu.VMEM((8, tb), F32), pltpu.VMEM((8, tb), F32), pltpu.VMEM((HEAD_PAD, tb), F32),
                        pltpu.VMEM((2 * tb, tb), F32), pltpu.VMEM((2 * tb, tb), BF16)])
    w = HEADS * HEAD_PAD
    return pl.pallas_call(body, grid_spec=gs, out_shape=[SDS((s, w), F32), SDS((s, w), BF16), SDS((HEADS, 1, s), F32)],
                          compiler_params=_cp(2), name="attn_fwd")(ii, jj, cc, qp, kp, vp)


def _attn_delta(do, o, tb):
    s = do.shape[0]

    def body(do_ref, o_ref, d_ref):
        for h in range(HEADS):
            hs = slice(h * HEAD_PAD, (h + 1) * HEAD_PAD)
            d_ref[h] = jnp.sum((do_ref[:, hs] * o_ref[:, hs]).T, axis=0, keepdims=True)

    blk = pl.BlockSpec((tb, HEADS * HEAD_PAD), lambda i: (i, 0))
    return pl.pallas_call(body, grid=(s // tb,), in_specs=[blk, blk],
                          out_specs=pl.BlockSpec((HEADS, 1, tb), lambda i: (0, 0, i)),
                          out_shape=SDS((HEADS, 1, s), F32), compiler_params=_cp(1), name="attn_delta")(do, o)


def _attn_bwd(qp, kp, vp, dob, lse, delta, tb):
    s = qp.shape[0]
    nb = s // tb
    jj, ii, cc = _pair_schedule(nb, False)
    rows = 32

    def body(jj_ref, ii_ref, cc_ref, q_ref, k_ref, v_ref, do_ref, lse_ref, dl_ref, dq_ref, dk_ref, dv_ref, st_sc, dpt_sc, pt_sc, dst_sc):
        t = pl.program_id(1)
        case = cc_ref[t]
        pair = ii_ref[t]

        @pl.when(t == 0)
        def _():
            dq_ref[...] = jnp.zeros_like(dq_ref)

        @pl.when(case != 0)
        def _():
            dk_ref[...] = jnp.zeros_like(dk_ref)
            dv_ref[...] = jnp.zeros_like(dv_ref)

        def step(lo, width, diag):
            q = q_ref[lo:lo + width, :]
            do = do_ref[lo:lo + width, :]
            k = k_ref[...]
            st_sc[:, 0:width] = lax.dot_general(k, q, (((1,), (1,)), ((), ())), preferred_element_type=F32)
            dpt_sc[:, 0:width] = lax.dot_general(v_ref[...], do, (((1,), (1,)), ((), ())), preferred_element_type=F32)
            lse_row = lse_ref[:, lo:lo + width]
            dl_row = dl_ref[:, lo:lo + width]
            for r in range(tb // rows):
                rs = slice(r * rows, (r + 1) * rows)
                p = jnp.exp2(st_sc[rs, 0:width] - lse_row)
                first_visible = (r * rows) // CHUNK * CHUNK
                if diag and first_visible > 0:
                    col = lax.broadcasted_iota(jnp.int32, (rows, width), 1)
                    p = jnp.where(col >= first_visible, p, 0.0)
                ds = p * (dpt_sc[rs, 0:width] - dl_row)
                pt_sc[rs, 0:width] = p.astype(BF16)
                dst_sc[rs, 0:width] = ds.astype(BF16)
            dv_ref[...] += jnp.dot(pt_sc[:, 0:width], do, preferred_element_type=F32)
            dst = dst_sc[:, 0:width]
            dk_ref[...] += jnp.dot(dst, q, preferred_element_type=F32)
            dqt = lax.dot_general(k, dst, (((0,), (0,)), ((), ())), preferred_element_type=F32)
            for h in range(width // tb):
                dq_ref[2 * pair + lo // tb + h] += dqt[:, h * tb:(h + 1) * tb]

        @pl.when(case == 0)
        def _():
            step(0, 2 * tb, False)

        @pl.when(case == 1)
        def _():
            step(0, 2 * tb, True)

        @pl.when(case == 2)
        def _():
            step(tb, tb, True)

    qmap = lambda h, t, jj_ref, ii_ref, cc_ref: (ii_ref[t], h)
    kmap = lambda h, t, jj_ref, ii_ref, cc_ref: (jj_ref[t], h)
    rowmap = lambda h, t, jj_ref, ii_ref, cc_ref: (h, 0, ii_ref[t])
    gs = pltpu.PrefetchScalarGridSpec(
        num_scalar_prefetch=3, grid=(HEADS, int(ii.shape[0])),
        in_specs=[pl.BlockSpec((2 * tb, HEAD_PAD), qmap), pl.BlockSpec((tb, HEAD_PAD), kmap), pl.BlockSpec((tb, HEAD_PAD), kmap),
                  pl.BlockSpec((2 * tb, HEAD_PAD), qmap), pl.BlockSpec((None, 1, 2 * tb), rowmap), pl.BlockSpec((None, 1, 2 * tb), rowmap)],
        out_specs=[pl.BlockSpec((None, nb, HEAD_PAD, tb), lambda h, t, jj_ref, ii_ref, cc_ref: (h, 0, 0, 0)),
                   pl.BlockSpec((tb, HEAD_PAD), kmap), pl.BlockSpec((tb, HEAD_PAD), kmap)],
        scratch_shapes=[pltpu.VMEM((tb, 2 * tb), F32), pltpu.VMEM((tb, 2 * tb), F32), pltpu.VMEM((tb, 2 * tb), BF16),
                        pltpu.VMEM((tb, 2 * tb), BF16)])
    w = HEADS * HEAD_PAD
    dqt, dk, dv = pl.pallas_call(
        body, grid_spec=gs, out_shape=[SDS((HEADS, nb, HEAD_PAD, tb), F32), SDS((s, w), F32), SDS((s, w), F32)],
        compiler_params=_cp(2), name="attn_bwd")(jj, ii, cc, qp, kp, vp, dob, lse, delta)
    return jnp.transpose(dqt, (1, 3, 0, 2)).reshape(s, w), dk, dv


def _head_norm(t, g):
    r = lax.rsqrt(jnp.mean(t * t, axis=-1, keepdims=True) + EPS)
    th = t * r
    return r, th, th * g


def _softmax_rows(sc):
    m = jnp.max(sc, axis=-1, keepdims=True)
    e = jnp.exp(sc - m)
    return e / jnp.sum(e, axis=-1, keepdims=True)


def _memattn_fwd(qm, kvm, gq, gk, tm):
    s = qm.shape[0]
    hd = MEM_HEAD_DIM

    def body(q_ref, k_ref, v_ref, gq_ref, gk_ref, o_ref):
        _, _, qn = _head_norm(q_ref[...], gq_ref[...])
        _, _, kn = _head_norm(k_ref[...], gk_ref[...])
        sc = lax.dot_general(qn.astype(BF16), kn.astype(BF16), (((1,), (1,)), ((), ())), preferred_element_type=F32)
        p = _softmax_rows(sc * (1.0 / math.sqrt(hd)))
        o_ref[...] = jnp.dot(p.astype(BF16), v_ref[...].astype(BF16), preferred_element_type=F32).astype(BF16)

    fix = lambda i, h: (0, 0)
    return pl.pallas_call(
        body, grid=(s // tm, MEM_HEADS),
        in_specs=[pl.BlockSpec((tm, hd), lambda i, h: (i, h)), pl.BlockSpec((MEM_LEN, hd), lambda i, h: (0, h)),
                  pl.BlockSpec((MEM_LEN, hd), lambda i, h: (0, MEM_HEADS + h)), pl.BlockSpec((1, hd), fix), pl.BlockSpec((1, hd), fix)],
        out_specs=pl.BlockSpec((tm, hd), lambda i, h: (i, h)),
        out_shape=SDS((s, MEM_HEADS * hd), BF16), compiler_params=_cp(2), name="memattn_fwd")(qm, kvm, kvm, gq, gk)


def _memattn_bwd(qm, kvm, d_o, gq, gk, tm):
    s = qm.shape[0]
    hd = MEM_HEAD_DIM

    def body(q_ref, k_ref, v_ref, do_ref, gq_ref, gk_ref, dq_ref, dk_ref, dv_ref, dgq_ref, dgk_ref, dkn_sc):
        h = pl.program_id(0)
        i = pl.program_id(1)
        last = pl.num_programs(1) - 1

        @pl.when((h == 0) & (i == 0))
        def _():
            dgq_ref[...] = jnp.zeros_like(dgq_ref)
            dgk_ref[...] = jnp.zeros_like(dgk_ref)

        @pl.when(i == 0)
        def _():
            dv_ref[...] = jnp.zeros_like(dv_ref)
            dkn_sc[...] = jnp.zeros_like(dkn_sc)

        rq, qh, qn = _head_norm(q_ref[...], gq_ref[...])
        rk, kh, kn = _head_norm(k_ref[...], gk_ref[...])
        qnb = qn.astype(BF16)
        knb = kn.astype(BF16)
        scale = 1.0 / math.sqrt(hd)
        sc = lax.dot_general(qnb, knb, (((1,), (1,)), ((), ())), preferred_element_type=F32)
        p = _softmax_rows(sc * scale)
        do = do_ref[...].astype(BF16)
        dp = lax.dot_general(do, v_ref[...].astype(BF16), (((1,), (1,)), ((), ())), preferred_element_type=F32)
        dv_ref[...] += lax.dot_general(p.astype(BF16), do, (((0,), (0,)), ((), ())), preferred_element_type=F32)
        ds = ((p * (dp - jnp.sum(dp * p, axis=-1, keepdims=True))) * scale).astype(BF16)
        dqn = jnp.dot(ds, knb, preferred_element_type=F32)
        dkn_sc[...] += lax.dot_general(ds, qnb, (((0,), (0,)), ((), ())), preferred_element_type=F32)
        dgq_ref[...] += jnp.sum(dqn * qh, axis=0, keepdims=True)
        dqh = dqn * gq_ref[...]
        dq_ref[...] = (rq * (dqh - qh * jnp.mean(dqh * qh, axis=-1, keepdims=True))).astype(BF16)

        @pl.when(i == last)
        def _():
            dkn = dkn_sc[...]
            dgk_ref[...] += jnp.sum(dkn * kh, axis=0, keepdims=True)
            dkh = dkn * gk_ref[...]
            dk_ref[...] = rk * (dkh - kh * jnp.mean(dkh * kh, axis=-1, keepdims=True))

    fix = lambda h, i: (0, 0)
    qb = pl.BlockSpec((tm, hd), lambda h, i: (i, h))
    kb = pl.BlockSpec((MEM_LEN, hd), lambda h, i: (0, h))
    return pl.pallas_call(
        body, grid=(MEM_HEADS, s // tm),
        in_specs=[qb, kb, pl.BlockSpec((MEM_LEN, hd), lambda h, i: (0, MEM_HEADS + h)), qb,
                  pl.BlockSpec((1, hd), fix), pl.BlockSpec((1, hd), fix)],
        out_specs=[qb, kb, kb, pl.BlockSpec((1, hd), fix), pl.BlockSpec((1, hd), fix)],
        out_shape=[SDS((s, MEM_HEADS * hd), BF16), SDS((MEM_LEN, MEM_HEADS * hd), F32), SDS((MEM_LEN, MEM_HEADS * hd), F32),
                   SDS((1, hd), F32), SDS((1, hd), F32)],
        scratch_shapes=[pltpu.VMEM((MEM_LEN, hd), F32)],
        compiler_params=_cp(2), name="memattn_bwd")(qm, kvm, kvm, d_o, gq, gk)


def _ffn_specs(tm, tn, nbj, s, order_ji):
    if order_ji:
        ij = lambda f: (lambda j, i: f(i, j))
    else:
        ij = lambda f: f
    prev = lambda i: jnp.maximum(i * (tm // FFN_HALO) - 1, 0)
    cur_g = pl.BlockSpec((tm, tn), ij(lambda i, j: (i, j)))
    cur_v = pl.BlockSpec((tm, tn), ij(lambda i, j: (i, j + nbj)))
    halo_g = pl.BlockSpec((FFN_HALO, tn), ij(lambda i, j: (prev(i), j)))
    halo_v = pl.BlockSpec((FFN_HALO, tn), ij(lambda i, j: (prev(i), j + nbj)))
    w_g = pl.BlockSpec((8, tn), ij(lambda i, j: (0, j)))
    w_v = pl.BlockSpec((8, tn), ij(lambda i, j: (0, j + nbj)))
    b_g = pl.BlockSpec((1, tn), ij(lambda i, j: (0, j)))
    b_v = pl.BlockSpec((1, tn), ij(lambda i, j: (0, j + nbj)))
    return cur_g, cur_v, halo_g, halo_v, w_g, w_v, b_g, b_v


FFN_STRIP = 16


def _conv3_rows(ext, w_ref, b_ref, o, n):
    return (w_ref[0:1, :] * ext[FFN_HALO - 2 + o:FFN_HALO - 2 + o + n, :] + w_ref[1:2, :] * ext[FFN_HALO - 1 + o:FFN_HALO - 1 + o + n, :]
            + w_ref[2:3, :] * ext[FFN_HALO + o:FFN_HALO + o + n, :] + b_ref[...])


def _ffn_fwd(up0, w8, b, tm, tn):
    s = up0.shape[0]
    nbj = D_FF // tn

    def body(g_ref, v_ref, gh_ref, vh_ref, wg_ref, wv_ref, bg_ref, bv_ref, act_ref, extg, extv):
        first = pl.program_id(0) == 0
        for ext, h_ref, c_ref in ((extg, gh_ref, g_ref), (extv, vh_ref, v_ref)):
            ext[0:FFN_HALO, :] = jnp.where(first, 0.0, h_ref[...])
            ext[FFN_HALO:, :] = c_ref[...]
        for r in range(tm // FFN_STRIP):
            o = r * FFN_STRIP
            ug = _conv3_rows(extg, wg_ref, bg_ref, o, FFN_STRIP)
            uv = _conv3_rows(extv, wv_ref, bv_ref, o, FFN_STRIP)
            act_ref[o:o + FFN_STRIP, :] = ((ug * jax.nn.sigmoid(ug)) * uv).astype(BF16)

    specs = _ffn_specs(tm, tn, nbj, s, False)
    return pl.pallas_call(
        body, grid=(s // tm, nbj), in_specs=list(specs),
        out_specs=pl.BlockSpec((tm, tn), lambda i, j: (i, j)), out_shape=SDS((s, D_FF), BF16),
        scratch_shapes=[pltpu.VMEM((tm + FFN_HALO, tn), F32), pltpu.VMEM((tm + FFN_HALO, tn), F32)],
        compiler_params=_cp(2), name="ffn_fwd")(up0, up0, up0, up0, w8, w8, b, b)


def _ffn_bwd(d_act, up0, w8, b, tm, tn):
    s = up0.shape[0]
    nbj = D_FF // tn
    te = tm + FFN_HALO

    def body(da_ref, dan_ref, g_ref, v_ref, gh_ref, vh_ref, gn_ref, vn_ref, wg_ref, wv_ref, bg_ref, bv_ref,
             og_ref, ov_ref, dbg_ref, dbv_ref, dwg_ref, dwv_ref, extg, extv, extdg, extdv, accg, accv):
        i = pl.program_id(1)
        first = i == 0
        last = i == pl.num_programs(1) - 1

        @pl.when(first)
        def _():
            for r in (dbg_ref, dbv_ref, dwg_ref, dwv_ref):
                r[...] = jnp.zeros_like(r)

        for ext, h_ref, c_ref, n_ref in ((extg, gh_ref, g_ref, gn_ref), (extv, vh_ref, v_ref, vn_ref)):
            ext[0:FFN_HALO, :] = jnp.where(first, 0.0, h_ref[...])
            ext[FFN_HALO:FFN_HALO + tm, :] = c_ref[...]
            ext[FFN_HALO + tm:, :] = n_ref[...]

        def fold8(x):
            acc = x[0:8, :]
            for q in range(1, x.shape[0] // 8):
                acc = acc + x[q * 8:(q + 1) * 8, :]
            return acc

        def taps(ext, o, n):
            return [ext[FFN_HALO - 2 + k + o:FFN_HALO - 2 + k + o + n, :] for k in range(3)]

        accg[...] = jnp.zeros_like(accg)
        accv[...] = jnp.zeros_like(accv)

        def gate_bwd(o, n, da, own_rows):
            xg, xv = taps(extg, o, n), taps(extv, o, n)
            ug = wg_ref[0:1, :] * xg[0] + wg_ref[1:2, :] * xg[1] + wg_ref[2:3, :] * xg[2] + bg_ref[...]
            uv = wv_ref[0:1, :] * xv[0] + wv_ref[1:2, :] * xv[1] + wv_ref[2:3, :] * xv[2] + bv_ref[...]
            sg = jax.nn.sigmoid(ug)
            dgt = da * uv * (sg * (1.0 + ug * (1.0 - sg)))
            dvl = da * (ug * sg)
            extdg[o:o + n, :] = dgt
            extdv[o:o + n, :] = dvl
            if own_rows:
                for acc, d, x in ((accg, dgt, xg), (accv, dvl, xv)):
                    acc[0] += fold8(d)
                    for k in range(3):
                        acc[1 + k] += fold8(d * x[k])

        for r in range(tm // FFN_STRIP):
            gate_bwd(r * FFN_STRIP, FFN_STRIP, da_ref[r * FFN_STRIP:(r + 1) * FFN_STRIP, :], True)
        gate_bwd(tm, FFN_HALO, jnp.where(last, 0.0, dan_ref[...]), False)

        for extd, w_ref, o_ref, db_ref, dw_ref, acc in ((extdg, wg_ref, og_ref, dbg_ref, dwg_ref, accg),
                                                        (extdv, wv_ref, ov_ref, dbv_ref, dwv_ref, accv)):
            for r in range(tm // FFN_STRIP):
                o = r * FFN_STRIP
                o_ref[o:o + FFN_STRIP, :] = (w_ref[2:3, :] * extd[o:o + FFN_STRIP, :] + w_ref[1:2, :] * extd[o + 1:o + 1 + FFN_STRIP, :]
                                             + w_ref[0:1, :] * extd[o + 2:o + 2 + FFN_STRIP, :]).astype(BF16)
            db_ref[...] += jnp.sum(acc[0], axis=0, keepdims=True)
            for k in range(3):
                dw_ref[k:k + 1, :] += jnp.sum(acc[1 + k], axis=0, keepdims=True)

    cur_g, cur_v, halo_g, halo_v, w_g, w_v, b_g, b_v = _ffn_specs(tm, tn, nbj, s, True)
    nxt_row = lambda i: jnp.minimum((i + 1) * (tm // FFN_HALO), s // FFN_HALO - 1)
    cur = pl.BlockSpec((tm, tn), lambda j, i: (i, j))
    nxt = pl.BlockSpec((FFN_HALO, tn), lambda j, i: (nxt_row(i), j))
    nxt_v = pl.BlockSpec((FFN_HALO, tn), lambda j, i: (nxt_row(i), j + nbj))
    acc1 = pl.BlockSpec((1, tn), lambda j, i: (0, j))
    acc8 = pl.BlockSpec((8, tn), lambda j, i: (0, j))
    return pl.pallas_call(
        body, grid=(nbj, s // tm), in_specs=[cur, nxt, cur_g, cur_v, halo_g, halo_v, nxt, nxt_v, w_g, w_v, b_g, b_v],
        out_specs=[cur, cur, acc1, acc1, acc8, acc8],
        out_shape=[SDS((s, D_FF), BF16), SDS((s, D_FF), BF16), SDS((1, D_FF), F32), SDS((1, D_FF), F32),
                   SDS((8, D_FF), F32), SDS((8, D_FF), F32)],
        scratch_shapes=[pltpu.VMEM((tm + 2 * FFN_HALO, tn), F32), pltpu.VMEM((tm + 2 * FFN_HALO, tn), F32),
                        pltpu.VMEM((te, tn), F32), pltpu.VMEM((te, tn), F32),
                        pltpu.VMEM((4, 8, tn), F32), pltpu.VMEM((4, 8, tn), F32)],
        compiler_params=_cp(2), name="ffn_bwd")(d_act, d_act, up0, up0, up0, up0, up0, up0, w8, w8, b, b)


def _down_loss(act, w_down, x2, target, tm):
    s = act.shape[0]

    def body(a_ref, w_ref, x_ref, t_ref, dyf_ref, dyb_ref, ls_ref):
        @pl.when(pl.program_id(0) == 0)
        def _():
            ls_ref[...] = jnp.zeros_like(ls_ref)

        y = x_ref[...] + jnp.dot(a_ref[...], w_ref[...], preferred_element_type=F32)
        e = y - t_ref[...]
        ls_ref[...] += jnp.sum(e * e)
        dy = e * (1.0 / D_MODEL)
        dyf_ref[...] = dy
        dyb_ref[...] = dy.astype(BF16)

    row = lambda i: (i, 0)
    return pl.pallas_call(
        body, grid=(s // tm,),
        in_specs=[pl.BlockSpec((tm, D_FF), row), pl.BlockSpec((D_FF, D_MODEL), lambda i: (0, 0)),
                  pl.BlockSpec((tm, D_MODEL), row), pl.BlockSpec((tm, D_MODEL), row)],
        out_specs=[pl.BlockSpec((tm, D_MODEL), row), pl.BlockSpec((tm, D_MODEL), row), pl.BlockSpec((8, 128), lambda i: (0, 0))],
        out_shape=[SDS((s, D_MODEL), F32), SDS((s, D_MODEL), BF16), SDS((8, 128), F32)],
        compiler_params=_cp(1), name="down_loss")(act, w_down, x2, target)


def _adamw_math(w, g, m, v):
    mn = ADAM_B1 * m + (1.0 - ADAM_B1) * g
    vn = ADAM_B2 * v + (1.0 - ADAM_B2) * (g * g)
    m_hat = mn / (1.0 - ADAM_B1 ** ADAM_STEP)
    v_hat = vn / (1.0 - ADAM_B2 ** ADAM_STEP)
    return -ADAM_LR * (m_hat / (jnp.sqrt(v_hat) + ADAM_EPS) + ADAM_WD * w), mn, vn


def _adamw(w, g, m, v, name):
    rows, cols = w.shape
    tr = rows if rows <= 256 else (256 if rows % 256 == 0 else rows // 2)

    def body(w_ref, g_ref, m_ref, v_ref, d_ref, mo_ref, vo_ref):
        d_ref[...], mo_ref[...], vo_ref[...] = _adamw_math(w_ref[...], g_ref[...], m_ref[...], v_ref[...])

    blk = pl.BlockSpec((tr, cols), lambda i: (i, 0))
    return pl.pallas_call(body, grid=(rows // tr,), in_specs=[blk] * 4, out_specs=[blk] * 3,
                          out_shape=[SDS((rows, cols), F32)] * 3, compiler_params=_cp(1), name=name)(w, g, m, v)


def _adamw_small(ws, gs, ms, vs):
    n = len(ws)

    def body(*refs):
        ins, outs = refs[:4 * n], refs[4 * n:]
        for k in range(n):
            d, mn, vn = _adamw_math(ins[k][...], ins[n + k][...], ins[2 * n + k][...], ins[3 * n + k][...])
            outs[k][...] = d
            outs[n + k][...] = mn
            outs[2 * n + k][...] = vn

    vm = pl.BlockSpec(memory_space=pltpu.VMEM)
    outs = pl.pallas_call(body, in_specs=[vm] * (4 * n), out_specs=[vm] * (3 * n),
                          out_shape=[SDS(w.shape, F32) for w in ws] * 3, name="adamw_small")(*ws, *gs, *ms, *vs)
    return outs[:n], outs[n:2 * n], outs[2 * n:]


ANY = pl.BlockSpec(memory_space=pl.ANY)


def _coords():
    return lax.axis_index("x"), lax.axis_index("y"), lax.axis_index("c")


def _other_chips(x, y):
    return [(1 - x, y), (x, 1 - y), (1 - x, 1 - y)]


D2D_CHUNKS = 8
ICI_CHUNKS = 4


def _row_chunks(n_rows, n_chunks, align):
    step = -(-n_rows // (n_chunks * align)) * align
    return [(r, min(step, n_rows - r)) for r in range(0, n_rows, step)]


def _ag_weights(wsh):
    rows, cols = wsh.shape
    half_rows = rows // 2

    def body(w_ref, out_ref, send_sems, recv_sems):
        x, y, c = _coords()
        s_me = 2 * x + y
        chips = _other_chips(x, y)
        sibling = (x, y, 1 - c)
        my_base = c * half_rows
        sib_base = (1 - c) * half_rows

        def piece(base, r0, nr):
            return pl.ds(pl.multiple_of(base + r0, 16), nr)

        def copy(k, shard, rows_, to, src=None):
            dst = out_ref.at[shard, rows_]
            return pltpu.make_async_remote_copy(src_ref=dst if src is None else src, dst_ref=dst, send_sem=send_sems.at[k],
                                                recv_sem=recv_sems.at[k], device_id=to, device_id_type=MESH)

        for k, (px, py) in enumerate(chips):
            for r0, nr in _row_chunks(half_rows, ICI_CHUNKS, 16):
                copy(k, s_me, piece(my_base, r0, nr), (px, py, c), src=w_ref.at[piece(my_base, r0, nr)]).start()
        for k, (px, py) in enumerate(chips):
            copy(k, 2 * px + py, piece(my_base, 0, half_rows), (px, py, c)).wait_recv()
            for r0, nr in _row_chunks(half_rows, ICI_CHUNKS, 16):
                copy(3 + k, 2 * px + py, piece(my_base, r0, nr), sibling).start()
        for k, (px, py) in enumerate(chips):
            copy(3 + k, 2 * px + py, piece(sib_base, 0, half_rows), sibling).wait_recv()
        for k in range(6):
            copy(k, s_me, piece(my_base, 0, half_rows), sibling).wait_send()

    return pl.pallas_call(
        body, in_specs=[ANY], out_specs=ANY, out_shape=SDS((4, rows, cols), wsh.dtype),
        scratch_shapes=[pltpu.SemaphoreType.DMA((6,)), pltpu.SemaphoreType.DMA((6,))],
        name="ag_weights")(wsh)


def _rs_swap_halves(gfull):
    n_sh, rows, cols = gfull.shape
    half_rows = rows // 2

    def body(g_ref, recv_ref, send_sem, recv_sem):
        x, y, c = _coords()
        sib_base = (1 - c) * half_rows
        for sh in range(n_sh):
            for r0, nr in _row_chunks(half_rows, D2D_CHUNKS, 8):
                pltpu.make_async_remote_copy(
                    src_ref=g_ref.at[sh, pl.ds(pl.multiple_of(sib_base + r0, 8), nr)], dst_ref=recv_ref.at[sh, pl.ds(r0, nr)],
                    send_sem=send_sem, recv_sem=recv_sem, device_id=(x, y, 1 - c), device_id_type=MESH).start()
        pltpu.make_async_remote_copy(src_ref=recv_ref, dst_ref=recv_ref, send_sem=send_sem, recv_sem=recv_sem,
                                     device_id=(x, y, 1 - c), device_id_type=MESH).wait()

    return pl.pallas_call(
        body, in_specs=[ANY], out_specs=ANY, out_shape=SDS((n_sh, half_rows, cols), gfull.dtype),
        scratch_shapes=[pltpu.SemaphoreType.DMA, pltpu.SemaphoreType.DMA], name="rs_swap_halves")(gfull)


def _rs_add_pair(gfull, recv, core, tr):
    n_sh, rows, cols = gfull.shape
    half_rows = rows // 2
    nblk = half_rows // tr

    def body(c_ref, g_ref, r_ref, o_ref, ob_ref):
        acc = g_ref[...] + r_ref[...]
        o_ref[...] = acc
        ob_ref[...] = acc.astype(BF16)

    out = pl.BlockSpec((None, tr, cols), lambda sh, i, c_ref: (sh, i, 0))
    gs = pltpu.PrefetchScalarGridSpec(
        num_scalar_prefetch=1, grid=(n_sh, nblk),
        in_specs=[pl.BlockSpec((None, tr, cols), lambda sh, i, c_ref: (sh, c_ref[0] * nblk + i, 0)), out],
        out_specs=[out, out])
    return pl.pallas_call(body, grid_spec=gs, out_shape=[SDS((n_sh, half_rows, cols), F32), SDS((n_sh, half_rows, cols), BF16)],
                          compiler_params=_cp(2), name="rs_add_pair")(core, gfull, recv)


def _rs_to_owner(chipsum):
    n_sh, half_rows, cols = chipsum.shape

    def body(cs_ref, recv_ref, send_sems, recv_sems):
        x, y, c = _coords()
        chips = _other_chips(x, y)
        for k, (px, py) in enumerate(chips):
            for r0, nr in _row_chunks(half_rows, ICI_CHUNKS, 16):
                pltpu.make_async_remote_copy(
                    src_ref=cs_ref.at[2 * px + py, pl.ds(r0, nr)], dst_ref=recv_ref.at[k, pl.ds(r0, nr)],
                    send_sem=send_sems.at[k], recv_sem=recv_sems.at[k], device_id=(px, py, c), device_id_type=MESH).start()
        for k, (px, py) in enumerate(chips):
            pltpu.make_async_remote_copy(src_ref=recv_ref.at[k], dst_ref=recv_ref.at[k], send_sem=send_sems.at[k],
                                         recv_sem=recv_sems.at[k], device_id=(px, py, c), device_id_type=MESH).wait()

    return pl.pallas_call(
        body, in_specs=[ANY], out_specs=ANY, out_shape=SDS((3, half_rows, cols), chipsum.dtype),
        scratch_shapes=[pltpu.SemaphoreType.DMA((3,)), pltpu.SemaphoreType.DMA((3,))], name="rs_to_owner")(chipsum)


def _rs_add_chips(chipsum, recv, shard_core, tr):
    _, half_rows, cols = chipsum.shape

    def body(s_ref, m_ref, r0_ref, r1_ref, r2_ref, o_ref):
        o_ref[...] = ((m_ref[...] + r0_ref[...].astype(F32)) + r1_ref[...].astype(F32)) + r2_ref[...].astype(F32)

    gs = pltpu.PrefetchScalarGridSpec(
        num_scalar_prefetch=1, grid=(half_rows // tr,),
        in_specs=[pl.BlockSpec((None, tr, cols), lambda i, s_ref: (s_ref[0], i, 0))]
        + [pl.BlockSpec((None, tr, cols), (lambda k: lambda i, s_ref: (k, i, 0))(k)) for k in range(3)],
        out_specs=pl.BlockSpec((None, tr, cols), lambda i, s_ref: (s_ref[1], i, 0)))
    return pl.pallas_call(body, grid_spec=gs, out_shape=SDS((2, half_rows, cols), F32),
                          compiler_params=_cp(1), name="rs_add_chips")(shard_core, chipsum, recv, recv, recv)


def _rs_join_halves(buf):
    _, half_rows, cols = buf.shape

    def body(b_ref, out_ref, send_sem, recv_sem):
        x, y, c = _coords()
        for r0, nr in _row_chunks(half_rows, D2D_CHUNKS, 8):
            pltpu.make_async_remote_copy(src_ref=out_ref.at[c, pl.ds(r0, nr)], dst_ref=out_ref.at[c, pl.ds(r0, nr)], send_sem=send_sem,
                                         recv_sem=recv_sem, device_id=(x, y, 1 - c), device_id_type=MESH).start()
        pltpu.make_async_remote_copy(src_ref=out_ref.at[c], dst_ref=out_ref.at[c], send_sem=send_sem, recv_sem=recv_sem,
                                     device_id=(x, y, 1 - c), device_id_type=MESH).wait()

    return pl.pallas_call(
        body, in_specs=[ANY], out_specs=ANY, out_shape=SDS(buf.shape, buf.dtype), input_output_aliases={0: 0},
        scratch_shapes=[pltpu.SemaphoreType.DMA, pltpu.SemaphoreType.DMA], name="rs_join_halves")(buf)


BIG = [("w_in", (1024, 1440), 1), ("w_uq", (256, 768), 1), ("w_ukv", (128, 1024), 1), ("w_out", (1024, 1024), 0),
       ("w_mem_q", (1024, 1024), 0), ("w_mem_kv", (1024, 2048), 1), ("w_mem_o", (1024, 1024), 0),
       ("w_up", (1024, 5632), 1), ("w_down", (2816, 1024), 0)]
SMALL_REP = [("mix_norm_g", 1024), ("b_conv_in", 1024), ("b_conv_dw", 512), ("conv_ln_g", 512), ("conv_ln_b", 512),
             ("q_lat_norm_g", 256), ("kv_lat_norm_g", 128), ("q_norm_g", 96), ("k_norm_g", 96), ("mem_norm_x_g", 1024),
             ("mem_norm_m_g", 1024), ("mem_q_norm_g", 256), ("mem_k_norm_g", 256), ("ffn_norm_g", 1024), ("b_ffn_dw", 5632)]
SMALL_SH = [("w_conv_dw", (31, 512)), ("w_ffn_dw", (3, 5632))]


def _shard_shape(shape, axis):
    return tuple(d // 4 if a == axis else d for a, d in enumerate(shape))


def _pack_rows(parts, rows, cols):
    flat = jnp.concatenate([p.reshape(-1) for p in parts])
    flat = jnp.pad(flat, (0, rows * cols - flat.shape[0]))
    return flat.reshape(rows, cols)


def _pack_big_shards(ws):
    parts = [ws[n].reshape(-1, PACK_COLS) for n, _, _ in BIG]
    used = sum(p.shape[0] for p in parts)
    return jnp.concatenate(parts + [jnp.zeros((PACK_ROWS - used, PACK_COLS), parts[0].dtype)], axis=0)


def _unpack_big_shards(packed):
    out, r = {}, 0
    for n, shape, axis in BIG:
        sh = _shard_shape(shape, axis)
        nr = sh[0] * sh[1] // PACK_COLS
        out[n] = packed[r:r + nr].reshape(sh)
        r += nr
    return out


def _unpack_gathered(g):
    out, r = {}, 0
    for n, shape, axis in BIG:
        sh = _shard_shape(shape, axis)
        nr = sh[0] * sh[1] // PACK_COLS
        part = g[:, r:r + nr]
        if axis == 0:
            out[n] = part.reshape(shape)
        else:
            out[n] = part.reshape((4,) + sh).transpose(1, 0, 2).reshape(shape)
        r += nr
    return out


def _pack_full_grads(gs):
    parts = []
    for n, shape, axis in BIG:
        sh = _shard_shape(shape, axis)
        nr = sh[0] * sh[1] // PACK_COLS
        if axis == 0:
            parts.append(gs[n].reshape(4, nr, PACK_COLS))
        else:
            parts.append(gs[n].reshape(shape[0], 4, sh[1]).transpose(1, 0, 2).reshape(4, nr, PACK_COLS))
    used = sum(p.shape[1] for p in parts)
    return jnp.concatenate(parts + [jnp.zeros((4, PACK_ROWS - used, PACK_COLS), F32)], axis=1)


def _rope_tables(positions):
    inv_freq = ROPE_THETA ** (-jnp.arange(0, ROPE, 2, dtype=F32) / ROPE)
    ang = positions.astype(F32)[:, None] * inv_freq
    cos, sin = jnp.cos(ang), jnp.sin(ang)
    s = positions.shape[0]
    cosf = jnp.concatenate([jnp.ones((s, NOPE), F32), cos, cos, jnp.ones((s, HEAD_PAD - HEAD_DIM), F32)], axis=-1)
    sinf = jnp.concatenate([jnp.zeros((s, NOPE), F32), -sin, sin, jnp.zeros((s, HEAD_PAD - HEAD_DIM), F32)], axis=-1)
    return cosf, sinf


def _pad_heads(w, per_head):
    k = w.shape[0]
    w3 = w.reshape(k, HEADS, per_head)
    return jnp.pad(w3, ((0, 0), (0, 0), (0, HEAD_PAD - per_head))).reshape(k, HEADS * HEAD_PAD)


def _layer_grads(x, mem, positions, target, wf, sp):
    s = x.shape[0]
    tm = _row_tile(s, 512)
    tc = _row_tile(s, 256)
    tb = 512 if s % 512 == 0 and s > 512 else s // 2
    row2 = lambda a: a.reshape(1, -1)

    w_in = wf["w_in"]
    w_in_pad = jnp.concatenate([w_in[:, :1408], jnp.zeros((D_MODEL, NOPE), BF16), w_in[:, 1408:],
                                jnp.zeros((D_MODEL, HEAD_PAD - HEAD_DIM), BF16)], axis=1)
    w_uq_pad = _pad_heads(wf["w_uq"], HEAD_DIM)
    w_ukv = wf["w_ukv"]
    w_out_u = wf["w_out"][:CONV_CH]
    w_out_o = jnp.pad(wf["w_out"][CONV_CH:].reshape(HEADS, NOPE, D_MODEL), ((0, 0), (NOPE, 0), (0, 0))).reshape(HEADS * HEAD_PAD, D_MODEL)
    w_up_g, w_up_v = wf["w_up"][:, :D_FF], wf["w_up"][:, D_FF:]
    gq_pad = jnp.pad(sp["q_norm_g"], (0, HEAD_PAD - HEAD_DIM)).reshape(1, HEAD_PAD)
    gk_pad = jnp.pad(sp["k_norm_g"], (0, HEAD_PAD - HEAD_DIM)).reshape(1, HEAD_PAD)
    w_dw32 = jnp.pad(sp["w_conv_dw"], ((0, 1), (0, 0)))
    w_ffn8 = jnp.pad(sp["w_ffn_dw"], ((0, 5), (0, 0)))
    b_ffn = row2(sp["b_ffn_dw"])
    cosf, sinf = _rope_tables(positions)

    z, h1 = _norm_linear(x, 0, D_MODEL, row2(sp["mix_norm_g"]), w_in_pad, F32, tm, IN_COLS_PAD, "in_proj")
    u, u0, u1 = _conv_fwd(z, row2(sp["b_conv_in"]), w_dw32, row2(sp["b_conv_dw"]), row2(sp["conv_ln_g"]), row2(sp["conv_ln_b"]), tc)
    q_raw, cqn = _norm_linear(z, 1024 // Q_RANK, Q_RANK, row2(sp["q_lat_norm_g"]), w_uq_pad, F32, tm, 1024, "q_up")
    kv_raw, ckvn = _norm_linear(z, 1280 // KV_RANK, KV_RANK, row2(sp["kv_lat_norm_g"]), w_ukv, F32, tm, 1024, "kv_up")
    qp, kp, vp = _mla_prep(q_raw, kv_raw, z, cosf, sinf, gq_pad, gk_pad, tc)
    o_f, o_b, lse = _attn_fwd(qp, kp, vp, tb)
    (x1,) = _linear([(u, w_out_u), (o_b, w_out_o)], False, x, [F32], tm, 1024, "out_proj")

    qm, hq = _norm_linear(x1, 0, D_MODEL, row2(sp["mem_norm_x_g"]), wf["w_mem_q"], F32, tm, 1024, "memq_proj")
    kvm, hm = _norm_linear(mem, 0, D_MODEL, row2(sp["mem_norm_m_g"]), wf["w_mem_kv"], F32, MEM_LEN, 1024, "memkv_proj")
    gmq, gmk = row2(sp["mem_q_norm_g"]), row2(sp["mem_k_norm_g"])
    o_m = _memattn_fwd(qm, kvm, gmq, gmk, tm)
    (x2,) = _linear([(o_m, wf["w_mem_o"])], False, x1, [F32], tm, 1024, "memo_proj")

    up0, h3 = _norm_linear(x2, 0, D_MODEL, row2(sp["ffn_norm_g"]), wf["w_up"], F32, tm, D_FF // 2, "up_proj")
    act = _ffn_fwd(up0, w_ffn8, b_ffn, tc, D_FF // 2)
    dy_f, dy_b, lsum = _down_loss(act, wf["w_down"], x2, target, tm)

    g = {}
    (d_act,) = _linear([(dy_b, wf["w_down"])], True, None, [F32], tm, D_FF // 2, "down_bwd")
    g["w_down"] = _dw(act, dy_b, "dw_down")
    d_up0g, d_up0v, dbg, dbv, dwg, dwv = _ffn_bwd(d_act, up0, w_ffn8, b_ffn, tc, D_FF // 2)
    g["b_ffn_dw"] = jnp.concatenate([dbg, dbv], axis=1).reshape(-1)
    g["w_ffn_dw"] = jnp.concatenate([dwg[:3], dwv[:3]], axis=1)
    g["w_up"] = jnp.concatenate([_dw(h3, d_up0g, "dw_up_g"), _dw(h3, d_up0v, "dw_up_v")], axis=1)
    d_x2f, d_x2b, dg = _linear_normbwd([(d_up0g, w_up_g), (d_up0v, w_up_v)], x2, 0, row2(sp["ffn_norm_g"]), dy_f,
                                       [F32, BF16], tc, "up_bwd")
    g["ffn_norm_g"] = dg.reshape(-1)

    (d_om,) = _linear([(d_x2b, wf["w_mem_o"])], True, None, [BF16], tm, 1024, "memo_bwd")
    g["w_mem_o"] = _dw(o_m, d_x2b, "dw_mem_o")
    d_qm, d_km, d_vm, dgq, dgk = _memattn_bwd(qm, kvm, d_om, gmq, gmk, tm)
    g["mem_q_norm_g"], g["mem_k_norm_g"] = dgq.reshape(-1), dgk.reshape(-1)
    d_kvm = jnp.concatenate([d_km, d_vm], axis=1)
    g["w_mem_q"] = _dw(hq, d_qm, "dw_mem_q")
    g["w_mem_kv"] = _dw(hm, d_kvm, "dw_mem_kv")
    d_x1f, d_x1b, dg = _linear_normbwd([(d_qm, wf["w_mem_q"])], x1, 0, row2(sp["mem_norm_x_g"]), d_x2f, [F32, BF16], tm, "memq_bwd")
    g["mem_norm_x_g"] = dg.reshape(-1)
    _, dg = _linear_normbwd([(d_kvm, wf["w_mem_kv"])], mem, 0, row2(sp["mem_norm_m_g"]), None, [BF16], MEM_LEN, "memkv_bwd")
    g["mem_norm_m_g"] = dg.reshape(-1)

    (d_u,) = _linear([(d_x1b, w_out_u)], True, None, [F32], tm, CONV_CH, "out_bwd_u")
    d_of, d_ob = _linear([(d_x1b, w_out_o)], True, None, [F32, BF16], tm, 1024, "out_bwd_o")
    dw_out_u = _dw(u, d_x1b, "dw_out_u")
    dw_out_o = _dw(o_b, d_x1b, "dw_out_o")
    g["w_out"] = jnp.concatenate([dw_out_u, dw_out_o.reshape(HEADS, HEAD_PAD, D_MODEL)[:, NOPE:].reshape(HEADS * NOPE, D_MODEL)], axis=0)
    delta = _attn_delta(d_of, o_f, tb)
    dqp, dkp, dvp = _attn_bwd(qp, kp, vp, d_ob, lse, delta, tb)
    d_qraw, d_kvraw, d_kr, dgq, dgk = _mla_prep_bwd(dqp, dkp, dvp, q_raw, kv_raw, z, cosf, sinf, gq_pad, gk_pad, tc)
    g["q_norm_g"], g["k_norm_g"] = dgq.reshape(-1)[:HEAD_DIM], dgk.reshape(-1)[:HEAD_DIM]
    g["w_uq"] = _dw(cqn, d_qraw, "dw_uq").reshape(Q_RANK, HEADS, HEAD_PAD)[:, :, :HEAD_DIM].reshape(Q_RANK, HEADS * HEAD_DIM)
    g["w_ukv"] = _dw(ckvn, d_kvraw, "dw_ukv")
    d_cq, dg = _linear_normbwd([(d_qraw, w_uq_pad)], z, 1024 // Q_RANK, row2(sp["q_lat_norm_g"]), None, [BF16], tm, "q_up_bwd")
    g["q_lat_norm_g"] = dg.reshape(-1)
    d_ckv, dg = _linear_normbwd([(d_kvraw, w_ukv)], z, 1280 // KV_RANK, row2(sp["kv_lat_norm_g"]), None, [BF16], tm, "kv_up_bwd")
    g["kv_lat_norm_g"] = dg.reshape(-1)
    d_u1, dlg, dlb, dbdw = _conv_bwd_ln(d_u, u1, row2(sp["conv_ln_g"]), row2(sp["conv_ln_b"]), tc)
    g["conv_ln_g"], g["conv_ln_b"], g["b_conv_dw"] = dlg.reshape(-1), dlb.reshape(-1), dbdw.reshape(-1)
    d_conv, dw_dw, dbin = _conv_bwd_dw(d_u1, u0, z, row2(sp["b_conv_in"]), w_dw32, tc)
    g["w_conv_dw"], g["b_conv_in"] = dw_dw[:CONV_WIDTH], dbin.reshape(-1)
    pieces = [(d_conv, w_in_pad[:, :1024]), (d_cq, w_in_pad[:, 1024:1280]), (d_ckv, w_in_pad[:, 1280:1408]), (d_kr, w_in_pad[:, 1408:])]
    dw_in = [_dw(h1, d, "dw_in_%d" % k) for k, (d, _) in enumerate(pieces)]
    g["w_in"] = jnp.concatenate([dw_in[0], dw_in[1], dw_in[2], dw_in[3][:, NOPE:HEAD_DIM]], axis=1)
    grad_x, dg = _linear_normbwd(pieces, x, 0, row2(sp["mix_norm_g"]), d_x1f, [F32], tm, "in_bwd")
    g["mix_norm_g"] = dg.reshape(-1)
    return lsum[0, 0], grad_x, g


def kernel(x, mem, positions, mix_norm_g, w_in, b_conv_in, w_conv_dw, b_conv_dw, conv_ln_g, conv_ln_b, q_lat_norm_g, w_uq, kv_lat_norm_g, w_ukv, q_norm_g, k_norm_g, w_out, mem_norm_x_g, mem_norm_m_g, w_mem_q, w_mem_kv, mem_q_norm_g, mem_k_norm_g, w_mem_o, ffn_norm_g, w_up, w_ffn_dw, b_ffn_dw, w_down, loss_target, m_mix_norm_g, m_w_in, m_b_conv_in, m_w_conv_dw, m_b_conv_dw, m_conv_ln_g, m_conv_ln_b, m_q_lat_norm_g, m_w_uq, m_kv_lat_norm_g, m_w_ukv, m_q_norm_g, m_k_norm_g, m_w_out, m_mem_norm_x_g, m_mem_norm_m_g, m_w_mem_q, m_w_mem_kv, m_mem_q_norm_g, m_mem_k_norm_g, m_w_mem_o, m_ffn_norm_g, m_w_up, m_w_ffn_dw, m_b_ffn_dw, m_w_down, v_mix_norm_g, v_w_in, v_b_conv_in, v_w_conv_dw, v_b_conv_dw, v_conv_ln_g, v_conv_ln_b, v_q_lat_norm_g, v_w_uq, v_kv_lat_norm_g, v_w_ukv, v_q_norm_g, v_k_norm_g, v_w_out, v_mem_norm_x_g, v_mem_norm_m_g, v_w_mem_q, v_w_mem_kv, v_mem_q_norm_g, v_mem_k_norm_g, v_w_mem_o, v_ffn_norm_g, v_w_up, v_w_ffn_dw, v_b_ffn_dw, v_w_down):
    names = ["mix_norm_g", "w_in", "b_conv_in", "w_conv_dw", "b_conv_dw", "conv_ln_g", "conv_ln_b", "q_lat_norm_g", "w_uq",
             "kv_lat_norm_g", "w_ukv", "q_norm_g", "k_norm_g", "w_out", "mem_norm_x_g", "mem_norm_m_g", "w_mem_q", "w_mem_kv",
             "mem_q_norm_g", "mem_k_norm_g", "w_mem_o", "ffn_norm_g", "w_up", "w_ffn_dw", "b_ffn_dw", "w_down"]
    loc = locals()
    w = {n: loc[n] for n in names}
    m = {n: loc["m_" + n] for n in names}
    v = {n: loc["v_" + n] for n in names}
    shard_idx = 2 * lax.axis_index("x") + lax.axis_index("y")

    w_packed = _pack_big_shards({n: w[n][0] for n, _, _ in BIG})
    w_wire = w_packed.astype(BF16)
    gathered = lax.dynamic_update_index_in_dim(_ag_weights(w_wire), w_wire, shard_idx, 0)
    wf = _unpack_gathered(gathered)

    small_sh_full = {}
    gather_in = []
    for n, (r, c) in SMALL_SH:
        csh = c // 4
        slab = lax.dynamic_update_slice(jnp.zeros((r, c), F32), w[n][0], (0, shard_idx * csh))
        gather_in.append(slab.reshape(-1))
    gather_rows = 256
    gathered_small = _allreduce_small_named(_pack_rows(gather_in, gather_rows, SMALL_COLS), "gather_small") * 0.5
    off = 0
    for n, (r, c) in SMALL_SH:
        small_sh_full[n] = gathered_small.reshape(-1)[off:off + r * c].reshape(r, c)
        off += r * c
    sp = {n: w[n][0] for n, _ in SMALL_REP}
    sp.update(small_sh_full)

    lsum, grad_x, g = _layer_grads(x[0], mem[0], positions[0], loss_target[0], wf, sp)

    small_parts = [jnp.full((SMALL_COLS,), lsum, F32)] + [g[n] for n, _ in SMALL_REP] + [g[n] for n, _ in SMALL_SH]
    small_rows = 368
    small_sum = _allreduce_small_named(_pack_rows(small_parts, small_rows, SMALL_COLS), "allreduce_small").reshape(-1)
    loss = small_sum[0] * (0.5 / D_MODEL)
    gs = {}
    off = SMALL_COLS
    for n, sz in SMALL_REP:
        gs[n] = small_sum[off:off + sz].reshape(w[n].shape)
        off += sz
    for n, (r, c) in SMALL_SH:
        full = small_sum[off:off + r * c].reshape(r, c)
        gs[n] = lax.dynamic_slice(full, (0, shard_idx * (c // 4)), (r, c // 4)).reshape(w[n].shape)
        off += r * c

    gfull = _pack_full_grads(g)
    core_idx = lax.axis_index("c").astype(jnp.int32)
    chipsum, chipsum_wire = _rs_add_pair(gfull, _rs_swap_halves(gfull), core_idx.reshape(1), 240)
    red = _rs_add_chips(chipsum, _rs_to_owner(chipsum_wire), jnp.stack([shard_idx.astype(jnp.int32), core_idx]), 240)
    g_packed = _rs_join_halves(red).reshape(PACK_ROWS, PACK_COLS)
    g_big = _unpack_big_shards(g_packed)
    gs.update({n: a[None] for n, a in g_big.items()})

    delta, new_m, new_v = {}, {}, {}
    for n, _, _ in BIG:
        d_n, m_n, v_n = _adamw(w[n][0], g_big[n], m[n][0], v[n][0], "adamw_" + n)
        delta[n], new_m[n], new_v[n] = d_n[None], m_n[None], v_n[None]
    small_names = [n for n, _ in SMALL_REP] + [n for n, _ in SMALL_SH]
    as2d = lambda a: a.reshape(-1, a.shape[-1])
    d_s, m_s, v_s = _adamw_small(*[[as2d(d[n]) for n in small_names] for d in (w, gs, m, v)])
    for k, n in enumerate(small_names):
        delta[n], new_m[n], new_v[n] = d_s[k].reshape(w[n].shape), m_s[k].reshape(w[n].shape), v_s[k].reshape(w[n].shape)

    return (loss, grad_x[None], *[gs[n] for n in names], *[delta[n] for n in names], *[new_m[n] for n in names],
            *[new_v[n] for n in names])


def _allreduce_small_named(v, name):
    rows, cols = v.shape

    def body(v_ref, out_ref, buf, send_sems, recv_sems):
        x, y, c = _coords()
        me = 4 * x + 2 * y + c
        buf[me] = v_ref[...]
        cps = []
        for r in range(1, 8):
            dx, dy, dc = (r >> 2) & 1, (r >> 1) & 1, r & 1
            to = (x + dx - 2 * x * dx, y + dy - 2 * y * dy, c + dc - 2 * c * dc)
            cp = pltpu.make_async_remote_copy(src_ref=v_ref, dst_ref=buf.at[me], send_sem=send_sems.at[r - 1],
                                              recv_sem=recv_sems.at[r - 1], device_id=to, device_id_type=MESH)
            cp.start()
            cps.append(cp)
        for cp in cps:
            cp.wait()
        acc = buf[0]
        for d in range(1, 8):
            acc = acc + buf[d]
        out_ref[...] = acc

    vm = pl.BlockSpec(memory_space=pltpu.VMEM)
    return pl.pallas_call(
        body, in_specs=[vm], out_specs=vm, out_shape=SDS((rows, cols), F32),
        scratch_shapes=[pltpu.VMEM((8, rows, cols), F32), pltpu.SemaphoreType.DMA((7,)), pltpu.SemaphoreType.DMA((7,))],
        name=name)(v)
```

```python
import math

import numpy as np
import jax
import jax.numpy as jnp
from jax import lax
from jax.experimental import pallas as pl
from jax.experimental.pallas import tpu as pltpu

F32 = jnp.float32
BF16 = jnp.bfloat16
SDS = jax.ShapeDtypeStruct
MESH = pl.DeviceIdType.MESH

D_MODEL = 1024
EPS = 1e-6
CONV_CH = 512
CONV_WIDTH = 31
CONV_HALO = 32
HEADS = 8
NOPE = 64
ROPE = 32
HEAD_DIM = NOPE + ROPE
HEAD_PAD = 128
Q_RANK = 256
KV_RANK = 128
CHUNK = 64
ROPE_THETA = 10000.0
IN_COLS_PAD = 1536
MEM_HEADS = 4
MEM_HEAD_DIM = 256
MEM_LEN = 256
D_FF = 2816
FFN_HALO = 8
ATT_SCALE = 1.0 / math.sqrt(HEAD_DIM)
LOG2E = math.log2(math.e)
LN2 = math.log(2.0)

ADAM_LR = 0.001
ADAM_B1 = 0.9
ADAM_B2 = 0.999
ADAM_EPS = 1e-08
ADAM_WD = 0.01
ADAM_STEP = 10

VMEM_LIMIT_V7X = 56 * 1024 * 1024
PACK_COLS = 1024
SMALL_COLS = 128


def _cp(n_axes):
    return pltpu.CompilerParams(dimension_semantics=("arbitrary",) * n_axes, vmem_limit_bytes=VMEM_LIMIT_V7X)


def _row_tile(s, want):
    return want if s % want == 0 else s


def _norm_linear(x, xcol, kdim, g, w, out_dtype, tm, tn, name):
    s = x.shape[0]
    n = w.shape[1]

    def body(x_ref, g_ref, w_ref, y_ref, hn_ref):
        @pl.when(pl.program_id(1) == 0)
        def _():
            xv = x_ref[...]
            r = lax.rsqrt(jnp.mean(xv * xv, axis=-1, keepdims=True) + EPS)
            hn_ref[...] = ((xv * r) * g_ref[...]).astype(BF16)

        y_ref[...] = jnp.dot(hn_ref[...], w_ref[...], preferred_element_type=F32).astype(y_ref.dtype)

    return pl.pallas_call(
        body, grid=(s // tm, n // tn),
        in_specs=[pl.BlockSpec((tm, kdim), lambda i, j: (i, xcol)), pl.BlockSpec((1, kdim), lambda i, j: (0, 0)),
                  pl.BlockSpec((kdim, tn), lambda i, j: (0, j))],
        out_specs=[pl.BlockSpec((tm, tn), lambda i, j: (i, j)), pl.BlockSpec((tm, kdim), lambda i, j: (i, 0))],
        out_shape=[SDS((s, n), out_dtype), SDS((s, kdim), BF16)],
        compiler_params=_cp(2), name=name)(x, g, w)


def _linear(pairs, nt, residual, out_dtypes, tm, tn, name):
    s = pairs[0][0].shape[0]
    n = pairs[0][1].shape[0] if nt else pairs[0][1].shape[1]
    n_pairs = len(pairs)
    has_res = residual is not None

    def body(*refs):
        a_refs = refs[:n_pairs]
        w_refs = refs[n_pairs:2 * n_pairs]
        res_ref = refs[2 * n_pairs] if has_res else None
        outs = refs[2 * n_pairs + int(has_res):]
        acc = None
        for a_ref, w_ref in zip(a_refs, w_refs):
            a = a_ref[...].astype(BF16)
            if nt:
                d = lax.dot_general(a, w_ref[...], (((1,), (1,)), ((), ())), preferred_element_type=F32)
            else:
                d = jnp.dot(a, w_ref[...], preferred_element_type=F32)
            acc = d if acc is None else acc + d
        if has_res:
            acc = res_ref[...] + acc
        for o in outs:
            o[...] = acc.astype(o.dtype)

    in_specs = [pl.BlockSpec((tm, a.shape[1]), lambda i, j: (i, 0)) for a, _ in pairs]
    if nt:
        in_specs += [pl.BlockSpec((tn, w.shape[1]), lambda i, j: (j, 0)) for _, w in pairs]
    else:
        in_specs += [pl.BlockSpec((w.shape[0], tn), lambda i, j: (0, j)) for _, w in pairs]
    args = [a for a, _ in pairs] + [w for _, w in pairs]
    if has_res:
        in_specs.append(pl.BlockSpec((tm, tn), lambda i, j: (i, j)))
        args.append(residual)
    outs = pl.pallas_call(
        body, grid=(s // tm, n // tn), in_specs=in_specs,
        out_specs=[pl.BlockSpec((tm, tn), lambda i, j: (i, j)) for _ in out_dtypes],
        out_shape=[SDS((s, n), dt) for dt in out_dtypes],
        compiler_params=_cp(2), name=name)(*args)
    return outs


def _linear_normbwd(pairs, x, xcol, g, d_res, out_dtypes, tm, name):
    s = pairs[0][0].shape[0]
    dn = pairs[0][1].shape[0]
    n_pairs = len(pairs)
    has_res = d_res is not None

    def body(*refs):
        a_refs = refs[:n_pairs]
        w_refs = refs[n_pairs:2 * n_pairs]
        x_ref, g_ref = refs[2 * n_pairs], refs[2 * n_pairs + 1]
        k = 2 * n_pairs + 2
        res_ref = refs[k] if has_res else None
        k += int(has_res)
        outs = refs[k:-1]
        dg_ref = refs[-1]
        dh = None
        for a_ref, w_ref in zip(a_refs, w_refs):
            d = lax.dot_general(a_ref[...].astype(BF16), w_ref[...], (((1,), (1,)), ((), ())), preferred_element_type=F32)
            dh = d if dh is None else dh + d
        xv = x_ref[...]
        r = lax.rsqrt(jnp.mean(xv * xv, axis=-1, keepdims=True) + EPS)
        y = xv * r

        @pl.when(pl.program_id(0) == 0)
        def _():
            dg_ref[...] = jnp.zeros_like(dg_ref)

        dg_ref[...] += jnp.sum(dh * y, axis=0, keepdims=True)
        dy = dh * g_ref[...]
        dx = r * (dy - y * jnp.mean(dy * y, axis=-1, keepdims=True))
        if has_res:
            dx = res_ref[...] + dx
        for o in outs:
            o[...] = dx.astype(o.dtype)

    in_specs = [pl.BlockSpec((tm, a.shape[1]), lambda i: (i, 0)) for a, _ in pairs]
    in_specs += [pl.BlockSpec((dn, w.shape[1]), lambda i: (0, 0)) for _, w in pairs]
    in_specs += [pl.BlockSpec((tm, dn), lambda i: (i, xcol)), pl.BlockSpec((1, dn), lambda i: (0, 0))]
    args = [a for a, _ in pairs] + [w for _, w in pairs] + [x, g]
    if has_res:
        in_specs.append(pl.BlockSpec((tm, dn), lambda i: (i, 0)))
        args.append(d_res)
    outs = pl.pallas_call(
        body, grid=(s // tm,), in_specs=in_specs,
        out_specs=[pl.BlockSpec((tm, dn), lambda i: (i, 0)) for _ in out_dtypes] + [pl.BlockSpec((1, dn), lambda i: (0, 0))],
        out_shape=[SDS((s, dn), dt) for dt in out_dtypes] + [SDS((1, dn), F32)],
        compiler_params=_cp(1), name=name)(*args)
    return outs


def _dw_matmul(a, b, tk, tn, ts, name):
    s, ka = a.shape
    n = b.shape[1]

    def body(a_ref, b_ref, o_ref):
        @pl.when(pl.program_id(2) == 0)
        def _():
            o_ref[...] = jnp.zeros_like(o_ref)

        o_ref[...] += lax.dot_general(a_ref[...].astype(BF16), b_ref[...].astype(BF16), (((0,), (0,)), ((), ())),
                                      preferred_element_type=F32)

    return pl.pallas_call(
        body, grid=(ka // tk, n // tn, s // ts),
        in_specs=[pl.BlockSpec((ts, tk), lambda k, j, t: (t, k)), pl.BlockSpec((ts, tn), lambda k, j, t: (t, j))],
        out_specs=pl.BlockSpec((tk, tn), lambda k, j, t: (k, j)),
        out_shape=SDS((ka, n), F32), compiler_params=_cp(3), name=name)(a, b)


def _dw(a, b, name):
    s, ka = a.shape
    n = b.shape[1]
    tk = ka if ka <= 1024 else ka // 2
    tn = n if n <= 1024 else (n // 2 if n == D_FF else 512)
    return _dw_matmul(a, b, tk, tn, _row_tile(s, 2048), name)


def _prev_halo(tm, halo):
    return lambda i: (jnp.maximum(i * (tm // halo) - 1, 0), 0)


def _next_halo(tm, halo, s):
    return lambda i: (jnp.minimum((i + 1) * (tm // halo), s // halo - 1), 0)


def _conv_fwd(z, b_in, w32, b_dw, ln_g, ln_b, tm):
    s = z.shape[0]
    c = CONV_CH

    def body(z_ref, zh_ref, bin_ref, w_ref, bdw_ref, lg_ref, lb_ref, u_ref, u0_ref, u1_ref, ext):
        i = pl.program_id(0)

        def glu(zz):
            zz = zz + bin_ref[...]
            return zz[:, :c] * jax.nn.sigmoid(zz[:, c:])

        u0 = glu(z_ref[...])
        u0_ref[...] = u0
        ext[0:CONV_HALO, :] = jnp.where(i > 0, glu(zh_ref[...]), 0.0)
        ext[CONV_HALO:, :] = u0
        off = CONV_HALO - (CONV_WIDTH - 1)
        for r in range(tm // 64):
            for cb in range(c // 128):
                cs = slice(cb * 128, (cb + 1) * 128)
                acc = jnp.zeros((64, 128), F32)
                for k in range(CONV_WIDTH):
                    acc = acc + ext[r * 64 + off + k: r * 64 + off + k + 64, cs] * w_ref[k:k + 1, cs]
                u1_ref[r * 64:(r + 1) * 64, cs] = acc + bdw_ref[:, cs]
        u1 = u1_ref[...]
        mu = jnp.mean(u1, axis=-1, keepdims=True)
        xc = u1 - mu
        y = xc * lax.rsqrt(jnp.mean(xc * xc, axis=-1, keepdims=True) + EPS)
        y = y * lg_ref[...] + lb_ref[...]
        u_ref[...] = (y * jax.nn.sigmoid(y)).astype(BF16)

    row = lambda i: (i, 0)
    fix = lambda i: (0, 0)
    return pl.pallas_call(
        body, grid=(s // tm,),
        in_specs=[pl.BlockSpec((tm, 2 * c), row), pl.BlockSpec((CONV_HALO, 2 * c), _prev_halo(tm, CONV_HALO)),
                  pl.BlockSpec((1, 2 * c), fix), pl.BlockSpec((32, c), fix), pl.BlockSpec((1, c), fix),
                  pl.BlockSpec((1, c), fix), pl.BlockSpec((1, c), fix)],
        out_specs=[pl.BlockSpec((tm, c), row)] * 3,
        out_shape=[SDS((s, c), BF16), SDS((s, c), F32), SDS((s, c), F32)],
        scratch_shapes=[pltpu.VMEM((tm + CONV_HALO, c), F32)],
        compiler_params=_cp(1), name="conv_fwd")(z, z, b_in, w32, b_dw, ln_g, ln_b)


def _conv_bwd_ln(d_u, u1, ln_g, ln_b, tm):
    s = d_u.shape[0]
    c = CONV_CH

    def body(du_ref, u1_ref, lg_ref, lb_ref, du1_ref, dlg_ref, dlb_ref, dbdw_ref):
        @pl.when(pl.program_id(0) == 0)
        def _():
            dlg_ref[...] = jnp.zeros_like(dlg_ref)
            dlb_ref[...] = jnp.zeros_like(dlb_ref)
            dbdw_ref[...] = jnp.zeros_like(dbdw_ref)

        u1 = u1_ref[...]
        mu = jnp.mean(u1, axis=-1, keepdims=True)
        xc = u1 - mu
        rs = lax.rsqrt(jnp.mean(xc * xc, axis=-1, keepdims=True) + EPS)
        xh = xc * rs
        y = xh * lg_ref[...] + lb_ref[...]
        sg = jax.nn.sigmoid(y)
        dy = du_ref[...] * (sg * (1.0 + y * (1.0 - sg)))
        dlg_ref[...] += jnp.sum(dy * xh, axis=0, keepdims=True)
        dlb_ref[...] += jnp.sum(dy, axis=0, keepdims=True)
        dxh = dy * lg_ref[...]
        du1 = rs * (dxh - jnp.mean(dxh, axis=-1, keepdims=True) - xh * jnp.mean(dxh * xh, axis=-1, keepdims=True))
        dbdw_ref[...] += jnp.sum(du1, axis=0, keepdims=True)
        du1_ref[...] = du1

    row = lambda i: (i, 0)
    fix = lambda i: (0, 0)
    return pl.pallas_call(
        body, grid=(s // tm,),
        in_specs=[pl.BlockSpec((tm, c), row), pl.BlockSpec((tm, c), row), pl.BlockSpec((1, c), fix), pl.BlockSpec((1, c), fix)],
        out_specs=[pl.BlockSpec((tm, c), row)] + [pl.BlockSpec((1, c), fix)] * 3,
        out_shape=[SDS((s, c), F32)] + [SDS((1, c), F32)] * 3,
        compiler_params=_cp(1), name="conv_bwd_ln")(d_u, u1, ln_g, ln_b)


def _conv_bwd_dw(d_u1, u0, z, b_in, w32, tm):
    s = d_u1.shape[0]
    c = CONV_CH

    def body(d_ref, dn_ref, u0_ref, u0p_ref, z_ref, bin_ref, w_ref, dz_ref, dw_ref, dbin_ref, extd, extu, du0):
        i = pl.program_id(0)
        last = pl.num_programs(0) - 1

        @pl.when(i == 0)
        def _():
            dw_ref[...] = jnp.zeros_like(dw_ref)
            dbin_ref[...] = jnp.zeros_like(dbin_ref)

        extd[0:tm, :] = d_ref[...]
        extd[tm:, :] = jnp.where(i < last, dn_ref[...], 0.0)
        extu[0:CONV_HALO, :] = jnp.where(i > 0, u0p_ref[...], 0.0)
        extu[CONV_HALO:, :] = u0_ref[...]
        off = CONV_HALO - (CONV_WIDTH - 1)
        for r in range(tm // 64):
            for cb in range(c // 128):
                cs = slice(cb * 128, (cb + 1) * 128)
                acc = jnp.zeros((64, 128), F32)
                for k in range(CONV_WIDTH):
                    o = r * 64 + (CONV_WIDTH - 1) - k
                    acc = acc + extd[o:o + 64, cs] * w_ref[k:k + 1, cs]
                du0[r * 64:(r + 1) * 64, cs] = acc
        for cb in range(c // 128):
            cs = slice(cb * 128, (cb + 1) * 128)
            for k in range(CONV_WIDTH):
                part = jnp.zeros((8, 128), F32)
                for r in range(tm // 64):
                    p = d_ref[r * 64:(r + 1) * 64, cs] * extu[r * 64 + off + k: r * 64 + off + k + 64, cs]
                    for q in range(8):
                        part = part + p[q * 8:(q + 1) * 8, :]
                dw_ref[k:k + 1, cs] += jnp.sum(part, axis=0, keepdims=True)
        zz = z_ref[...] + bin_ref[...]
        a = zz[:, :c]
        sg = jax.nn.sigmoid(zz[:, c:])
        d0 = du0[...]
        da = d0 * sg
        dgt = d0 * a * (sg * (1.0 - sg))
        dbin_ref[:, :c] += jnp.sum(da, axis=0, keepdims=True)
        dbin_ref[:, c:] += jnp.sum(dgt, axis=0, keepdims=True)
        dz_ref[:, :c] = da.astype(BF16)
        dz_ref[:, c:] = dgt.astype(BF16)

    row = lambda i: (i, 0)
    fix = lambda i: (0, 0)
    return pl.pallas_call(
        body, grid=(s // tm,),
        in_specs=[pl.BlockSpec((tm, c), row), pl.BlockSpec((CONV_HALO, c), _next_halo(tm, CONV_HALO, s)),
                  pl.BlockSpec((tm, c), row), pl.BlockSpec((CONV_HALO, c), _prev_halo(tm, CONV_HALO)),
                  pl.BlockSpec((tm, 2 * c), row), pl.BlockSpec((1, 2 * c), fix), pl.BlockSpec((32, c), fix)],
        out_specs=[pl.BlockSpec((tm, 2 * c), row), pl.BlockSpec((32, c), fix), pl.BlockSpec((1, 2 * c), fix)],
        out_shape=[SDS((s, 2 * c), BF16), SDS((32, c), F32), SDS((1, 2 * c), F32)],
        scratch_shapes=[pltpu.VMEM((tm + CONV_HALO, c), F32), pltpu.VMEM((tm + CONV_HALO, c), F32), pltpu.VMEM((tm, c), F32)],
        compiler_params=_cp(1), name="conv_bwd_dw")(d_u1, d_u1, u0, u0, z, b_in, w32)


def _partner(v, lane):
    up = pltpu.roll(v, HEAD_PAD - ROPE // 2, 1)
    dn = pltpu.roll(v, ROPE // 2, 1)
    lo = (lane >= NOPE) & (lane < NOPE + ROPE // 2)
    hi = (lane >= NOPE + ROPE // 2) & (lane < HEAD_DIM)
    return jnp.where(lo, up, jnp.where(hi, dn, 0.0))


def _mla_prep(q_raw, kv_raw, z, cosf, sinf, gq, gk, tm):
    s = q_raw.shape[0]

    def body(q_ref, kv_ref, kr_ref, c_ref, s_ref, gq_ref, gk_ref, qo_ref, ko_ref, vo_ref):
        lane = lax.broadcasted_iota(jnp.int32, (tm, HEAD_PAD), 1)
        cf = c_ref[...]
        sf = s_ref[...]

        def norm_rope(t, g_ref):
            r = lax.rsqrt(jnp.sum(t * t, axis=-1, keepdims=True) * (1.0 / HEAD_DIM) + EPS)
            tn = (t * r) * g_ref[...]
            return tn * cf + _partner(tn, lane) * sf

        kr = kr_ref[...]
        for h in range(HEADS):
            hs = slice(h * HEAD_PAD, (h + 1) * HEAD_PAD)
            qo_ref[:, hs] = (norm_rope(q_ref[:, hs], gq_ref) * (ATT_SCALE * LOG2E)).astype(BF16)
            kv = kv_ref[:, hs]
            ko_ref[:, hs] = norm_rope(jnp.where(lane < NOPE, kv, 0.0) + kr, gk_ref).astype(BF16)
            vo_ref[:, hs] = jnp.where(lane >= NOPE, kv, 0.0).astype(BF16)

    row = lambda i: (i, 0)
    wide = pl.BlockSpec((tm, HEADS * HEAD_PAD), row)
    one = pl.BlockSpec((tm, HEAD_PAD), row)
    return pl.pallas_call(
        body, grid=(s // tm,),
        in_specs=[wide, wide, pl.BlockSpec((tm, HEAD_PAD), lambda i: (i, IN_COLS_PAD // HEAD_PAD - 1)), one, one,
                  pl.BlockSpec((1, HEAD_PAD), lambda i: (0, 0)), pl.BlockSpec((1, HEAD_PAD), lambda i: (0, 0))],
        out_specs=[wide] * 3,
        out_shape=[SDS((s, HEADS * HEAD_PAD), BF16)] * 3,
        compiler_params=_cp(1), name="mla_prep")(q_raw, kv_raw, z, cosf, sinf, gq, gk)


def _mla_prep_bwd(dqp, dkp, dvp, q_raw, kv_raw, z, cosf, sinf, gq, gk, tm):
    s = q_raw.shape[0]

    def body(dq_ref, dk_ref, dv_ref, q_ref, kv_ref, kr_ref, c_ref, s_ref, gq_ref, gk_ref,
             dqo_ref, dkvo_ref, dkr_ref, dgq_ref, dgk_ref):
        lane = lax.broadcasted_iota(jnp.int32, (tm, HEAD_PAD), 1)
        cf = c_ref[...]
        sf = s_ref[...]

        @pl.when(pl.program_id(0) == 0)
        def _():
            dgq_ref[...] = jnp.zeros_like(dgq_ref)
            dgk_ref[...] = jnp.zeros_like(dgk_ref)

        def norm_rope_bwd(t, d_out, g_ref, dg_ref):
            r = lax.rsqrt(jnp.sum(t * t, axis=-1, keepdims=True) * (1.0 / HEAD_DIM) + EPS)
            th = t * r
            dn = d_out * cf + _partner(d_out * sf, lane)
            dg_ref[...] += jnp.sum(dn * th, axis=0, keepdims=True)
            dh = dn * g_ref[...]
            return r * (dh - th * (jnp.sum(dh * th, axis=-1, keepdims=True) * (1.0 / HEAD_DIM)))

        kr = kr_ref[...]
        dkr = None
        for h in range(HEADS):
            hs = slice(h * HEAD_PAD, (h + 1) * HEAD_PAD)
            dq = norm_rope_bwd(q_ref[:, hs], dq_ref[:, hs] * ATT_SCALE, gq_ref, dgq_ref)
            dqo_ref[:, hs] = dq.astype(BF16)
            kv = kv_ref[:, hs]
            dkpre = norm_rope_bwd(jnp.where(lane < NOPE, kv, 0.0) + kr, dk_ref[:, hs] * LN2, gk_ref, dgk_ref)
            dkvo_ref[:, hs] = jnp.where(lane < NOPE, dkpre, dv_ref[:, hs]).astype(BF16)
            dkr_h = jnp.where((lane >= NOPE) & (lane < HEAD_DIM), dkpre, 0.0)
            dkr = dkr_h if dkr is None else dkr + dkr_h
        dkr_ref[...] = dkr

    row = lambda i: (i, 0)
    fix = lambda i: (0, 0)
    wide = pl.BlockSpec((tm, HEADS * HEAD_PAD), row)
    one = pl.BlockSpec((tm, HEAD_PAD), row)
    return pl.pallas_call(
        body, grid=(s // tm,),
        in_specs=[wide, wide, wide, wide, wide, pl.BlockSpec((tm, HEAD_PAD), lambda i: (i, IN_COLS_PAD // HEAD_PAD - 1)),
                  one, one, pl.BlockSpec((1, HEAD_PAD), fix), pl.BlockSpec((1, HEAD_PAD), fix)],
        out_specs=[wide, wide, one, pl.BlockSpec((1, HEAD_PAD), fix), pl.BlockSpec((1, HEAD_PAD), fix)],
        out_shape=[SDS((s, HEADS * HEAD_PAD), BF16), SDS((s, HEADS * HEAD_PAD), BF16), SDS((s, HEAD_PAD), F32),
                   SDS((1, HEAD_PAD), F32), SDS((1, HEAD_PAD), F32)],
        compiler_params=_cp(1), name="mla_prep_bwd")(dqp, dkp, dvp, q_raw, kv_raw, z, cosf, sinf, gq, gk)


def _pair_schedule(nb, forward):
    one, two, case = [], [], []
    for a in range(nb):
        for b in (range(a // 2 + 1) if forward else range(a // 2, nb // 2)):
            one.append(a)
            two.append(b)
            case.append(0 if b != a // 2 else 1 + a % 2)
    return tuple(jnp.asarray(np.array(x, np.int32)) for x in (one, two, case))


STRIP = 64


def _fold8(x):
    acc = x[0:8, :]
    for g in range(1, x.shape[0] // 8):
        acc = acc + x[g * 8:(g + 1) * 8, :]
    return acc


def _attn_fwd(qp, kp, vp, tb, w_late):
    s = qp.shape[0]
    ii, jj, cc = _pair_schedule(s // tb, True)
    n_steps = int(ii.shape[0])

    def body(ii_ref, jj_ref, cc_ref, q_ref, k_ref, v_ref, wl_ref, of_ref, ob_ref, lse_ref, gl_ref,
             m_sc, l_sc, acc_sc, st_sc, pt_sc, send_sems, recv_sems):
        t = pl.program_id(1)
        case = cc_ref[t]

        @pl.when((pl.program_id(0) == 0) & (t == 0))
        def _():
            _ag_send(wl_ref, gl_ref, send_sems, recv_sems)

        @pl.when(jj_ref[t] == 0)
        def _():
            m_sc[...] = jnp.full_like(m_sc, -jnp.inf)
            l_sc[...] = jnp.zeros_like(l_sc)
            acc_sc[...] = jnp.zeros_like(acc_sc)

        def step(n_keys, diag_at):
            def visible(r):
                if diag_at is None or r * STRIP <= diag_at:
                    return None
                col = lax.broadcasted_iota(jnp.int32, (STRIP, tb), 1)
                return col >= r * STRIP - diag_at

            st_sc[0:n_keys, :] = lax.dot_general(k_ref[0:n_keys, :], q_ref[...], (((1,), (1,)), ((), ())), preferred_element_type=F32)
            mx = None
            for r in range(n_keys // STRIP):
                sc = st_sc[r * STRIP:(r + 1) * STRIP, :]
                if visible(r) is not None:
                    sc = jnp.where(visible(r), sc, -jnp.inf)
                m8 = sc[0:8, :]
                for g in range(1, STRIP // 8):
                    m8 = jnp.maximum(m8, sc[g * 8:(g + 1) * 8, :])
                mx = m8 if mx is None else jnp.maximum(mx, m8)
            m_old = m_sc[0:1, :]
            m_new = jnp.maximum(m_old, jnp.max(mx, axis=0, keepdims=True))
            alpha = jnp.exp2(m_old - m_new)
            ps = None
            for r in range(n_keys // STRIP):
                p = jnp.exp2(st_sc[r * STRIP:(r + 1) * STRIP, :] - m_new)
                if visible(r) is not None:
                    p = jnp.where(visible(r), p, 0.0)
                ps = _fold8(p) if ps is None else ps + _fold8(p)
                pt_sc[r * STRIP:(r + 1) * STRIP, :] = p.astype(BF16)
            l_new = alpha * l_sc[0:1, :] + jnp.sum(ps, axis=0, keepdims=True)
            m_sc[...] = jnp.broadcast_to(m_new, m_sc.shape)
            l_sc[...] = jnp.broadcast_to(l_new, l_sc.shape)
            pv = lax.dot_general(v_ref[0:n_keys, :], pt_sc[0:n_keys, :], (((0,), (0,)), ((), ())), preferred_element_type=F32)
            acc_sc[...] = alpha * acc_sc[...] + pv

        @pl.when(case == 0)
        def _():
            step(2 * tb, None)

        @pl.when(case == 1)
        def _():
            step(tb, 0)

        @pl.when(case == 2)
        def _():
            step(2 * tb, tb)

        @pl.when(case != 0)
        def _():
            l = l_sc[0:1, :]
            o = (acc_sc[...] / l).T
            of_ref[...] = o
            ob_ref[...] = o.astype(BF16)
            lse_ref[...] = m_sc[0:1, :] + jnp.log(l) * LOG2E

        @pl.when((pl.program_id(0) == HEADS - 1) & (t == n_steps - 1))
        def _():
            _ag_finish(wl_ref, gl_ref, send_sems, recv_sems)

    qmap = lambda h, t, ii_ref, jj_ref, cc_ref: (ii_ref[t], h)
    kmap = lambda h, t, ii_ref, jj_ref, cc_ref: (jj_ref[t], h)
    gs = pltpu.PrefetchScalarGridSpec(
        num_scalar_prefetch=3, grid=(HEADS, n_steps),
        in_specs=[pl.BlockSpec((tb, HEAD_PAD), qmap), pl.BlockSpec((2 * tb, HEAD_PAD), kmap), pl.BlockSpec((2 * tb, HEAD_PAD), kmap), ANY],
        out_specs=[pl.BlockSpec((tb, HEAD_PAD), qmap), pl.BlockSpec((tb, HEAD_PAD), qmap),
                   pl.BlockSpec((None, 1, tb), lambda h, t, ii_ref, jj_ref, cc_ref: (h, 0, ii_ref[t])), ANY],
        scratch_shapes=[pltpu.VMEM((8, tb), F32), pltpu.VMEM((8, tb), F32), pltpu.VMEM((HEAD_PAD, tb), F32),
                        pltpu.VMEM((2 * tb, tb), F32), pltpu.VMEM((2 * tb, tb), BF16),
                        pltpu.SemaphoreType.DMA((6,)), pltpu.SemaphoreType.DMA((6,))])
    w = HEADS * HEAD_PAD
    return pl.pallas_call(
        body, grid_spec=gs,
        out_shape=[SDS((s, w), F32), SDS((s, w), BF16), SDS((HEADS, 1, s), F32), SDS((4,) + w_late.shape, w_late.dtype)],
        compiler_params=_cp(2), name="attn_fwd")(ii, jj, cc, qp, kp, vp, w_late)


def _attn_delta(do, o, tb):
    s = do.shape[0]

    def body(do_ref, o_ref, d_ref):
        for h in range(HEADS):
            hs = slice(h * HEAD_PAD, (h + 1) * HEAD_PAD)
            d_ref[h] = jnp.sum((do_ref[:, hs] * o_ref[:, hs]).T, axis=0, keepdims=True)

    blk = pl.BlockSpec((tb, HEADS * HEAD_PAD), lambda i: (i, 0))
    return pl.pallas_call(body, grid=(s // tb,), in_specs=[blk, blk],
                          out_specs=pl.BlockSpec((HEADS, 1, tb), lambda i: (0, 0, i)),
                          out_shape=SDS((HEADS, 1, s), F32), compiler_params=_cp(1), name="attn_delta")(do, o)


def _attn_bwd(qp, kp, vp, dob, lse, delta, tb, wire):
    s = qp.shape[0]
    nb = s // tb
    jj, ii, cc = _pair_schedule(nb, False)
    rows = 32

    n_steps = int(ii.shape[0])

    def body(jj_ref, ii_ref, cc_ref, q_ref, k_ref, v_ref, do_ref, lse_ref, dl_ref, cw_ref, dq_ref, dk_ref, dv_ref, rcv_ref,
             st_sc, dpt_sc, pt_sc, dst_sc, send_sems, recv_sems):
        t = pl.program_id(1)
        case = cc_ref[t]
        pair = ii_ref[t]

        @pl.when((pl.program_id(0) == 0) & (t == 0))
        def _():
            _rs_send(cw_ref, rcv_ref, send_sems, recv_sems)

        @pl.when((pl.program_id(0) == HEADS - 1) & (t == n_steps - 1))
        def _():
            _rs_wait(rcv_ref, send_sems, recv_sems)

        @pl.when(t == 0)
        def _():
            dq_ref[...] = jnp.zeros_like(dq_ref)

        @pl.when(case != 0)
        def _():
            dk_ref[...] = jnp.zeros_like(dk_ref)
            dv_ref[...] = jnp.zeros_like(dv_ref)

        def step(lo, width, diag):
            q = q_ref[lo:lo + width, :]
            do = do_ref[lo:lo + width, :]
            k = k_ref[...]
            st_sc[:, 0:width] = lax.dot_general(k, q, (((1,), (1,)), ((), ())), preferred_element_type=F32)
            dpt_sc[:, 0:width] = lax.dot_general(v_ref[...], do, (((1,), (1,)), ((), ())), preferred_element_type=F32)
            lse_row = lse_ref[:, lo:lo + width]
            dl_row = dl_ref[:, lo:lo + width]
            for r in range(tb // rows):
                rs = slice(r * rows, (r + 1) * rows)
                p = jnp.exp2(st_sc[rs, 0:width] - lse_row)
                first_visible = (r * rows) // CHUNK * CHUNK
                if diag and first_visible > 0:
                    col = lax.broadcasted_iota(jnp.int32, (rows, width), 1)
                    p = jnp.where(col >= first_visible, p, 0.0)
                ds = p * (dpt_sc[rs, 0:width] - dl_row)
                pt_sc[rs, 0:width] = p.astype(BF16)
                dst_sc[rs, 0:width] = ds.astype(BF16)
            dv_ref[...] += jnp.dot(pt_sc[:, 0:width], do, preferred_element_type=F32)
            dst = dst_sc[:, 0:width]
            dk_ref[...] += jnp.dot(dst, q, preferred_element_type=F32)
            dqt = lax.dot_general(k, dst, (((0,), (0,)), ((), ())), preferred_element_type=F32)
            for h in range(width // tb):
                dq_ref[2 * pair + lo // tb + h] += dqt[:, h * tb:(h + 1) * tb]

        @pl.when(case == 0)
        def _():
            step(0, 2 * tb, False)

        @pl.when(case == 1)
        def _():
            step(0, 2 * tb, True)

        @pl.when(case == 2)
        def _():
            step(tb, tb, True)

    qmap = lambda h, t, jj_ref, ii_ref, cc_ref: (ii_ref[t], h)
    kmap = lambda h, t, jj_ref, ii_ref, cc_ref: (jj_ref[t], h)
    rowmap = lambda h, t, jj_ref, ii_ref, cc_ref: (h, 0, ii_ref[t])
    gs = pltpu.PrefetchScalarGridSpec(
        num_scalar_prefetch=3, grid=(HEADS, n_steps),
        in_specs=[pl.BlockSpec((2 * tb, HEAD_PAD), qmap), pl.BlockSpec((tb, HEAD_PAD), kmap), pl.BlockSpec((tb, HEAD_PAD), kmap),
                  pl.BlockSpec((2 * tb, HEAD_PAD), qmap), pl.BlockSpec((None, 1, 2 * tb), rowmap), pl.BlockSpec((None, 1, 2 * tb), rowmap), ANY],
        out_specs=[pl.BlockSpec((None, nb, HEAD_PAD, tb), lambda h, t, jj_ref, ii_ref, cc_ref: (h, 0, 0, 0)),
                   pl.BlockSpec((tb, HEAD_PAD), kmap), pl.BlockSpec((tb, HEAD_PAD), kmap), ANY],
        scratch_shapes=[pltpu.VMEM((tb, 2 * tb), F32), pltpu.VMEM((tb, 2 * tb), F32), pltpu.VMEM((tb, 2 * tb), BF16),
                        pltpu.VMEM((tb, 2 * tb), BF16), pltpu.SemaphoreType.DMA((3,)), pltpu.SemaphoreType.DMA((3,))])
    w = HEADS * HEAD_PAD
    dqt, dk, dv, recv = pl.pallas_call(
        body, grid_spec=gs,
        out_shape=[SDS((HEADS, nb, HEAD_PAD, tb), F32), SDS((s, w), F32), SDS((s, w), F32), SDS((3,) + wire.shape[1:], wire.dtype)],
        compiler_params=_cp(2), name="attn_bwd")(jj, ii, cc, qp, kp, vp, dob, lse, delta, wire)
    return jnp.transpose(dqt, (1, 3, 0, 2)).reshape(s, w), dk, dv, recv


def _head_norm(t, g):
    r = lax.rsqrt(jnp.mean(t * t, axis=-1, keepdims=True) + EPS)
    th = t * r
    return r, th, th * g


def _softmax_rows(sc):
    m = jnp.max(sc, axis=-1, keepdims=True)
    e = jnp.exp(sc - m)
    return e / jnp.sum(e, axis=-1, keepdims=True)


def _memattn_fwd(qm, kvm, gq, gk, tm):
    s = qm.shape[0]
    hd = MEM_HEAD_DIM

    def body(q_ref, k_ref, v_ref, gq_ref, gk_ref, o_ref):
        _, _, qn = _head_norm(q_ref[...], gq_ref[...])
        _, _, kn = _head_norm(k_ref[...], gk_ref[...])
        sc = lax.dot_general(qn.astype(BF16), kn.astype(BF16), (((1,), (1,)), ((), ())), preferred_element_type=F32)
        p = _softmax_rows(sc * (1.0 / math.sqrt(hd)))
        o_ref[...] = jnp.dot(p.astype(BF16), v_ref[...].astype(BF16), preferred_element_type=F32).astype(BF16)

    fix = lambda i, h: (0, 0)
    return pl.pallas_call(
        body, grid=(s // tm, MEM_HEADS),
        in_specs=[pl.BlockSpec((tm, hd), lambda i, h: (i, h)), pl.BlockSpec((MEM_LEN, hd), lambda i, h: (0, h)),
                  pl.BlockSpec((MEM_LEN, hd), lambda i, h: (0, MEM_HEADS + h)), pl.BlockSpec((1, hd), fix), pl.BlockSpec((1, hd), fix)],
        out_specs=pl.BlockSpec((tm, hd), lambda i, h: (i, h)),
        out_shape=SDS((s, MEM_HEADS * hd), BF16), compiler_params=_cp(2), name="memattn_fwd")(qm, kvm, kvm, gq, gk)


def _memattn_bwd(qm, kvm, d_o, gq, gk, tm):
    s = qm.shape[0]
    hd = MEM_HEAD_DIM

    def body(q_ref, k_ref, v_ref, do_ref, gq_ref, gk_ref, dq_ref, dk_ref, dv_ref, dgq_ref, dgk_ref, dkn_sc):
        h = pl.program_id(0)
        i = pl.program_id(1)
        last = pl.num_programs(1) - 1

        @pl.when((h == 0) & (i == 0))
        def _():
            dgq_ref[...] = jnp.zeros_like(dgq_ref)
            dgk_ref[...] = jnp.zeros_like(dgk_ref)

        @pl.when(i == 0)
        def _():
            dv_ref[...] = jnp.zeros_like(dv_ref)
            dkn_sc[...] = jnp.zeros_like(dkn_sc)

        rq, qh, qn = _head_norm(q_ref[...], gq_ref[...])
        rk, kh, kn = _head_norm(k_ref[...], gk_ref[...])
        qnb = qn.astype(BF16)
        knb = kn.astype(BF16)
        scale = 1.0 / math.sqrt(hd)
        sc = lax.dot_general(qnb, knb, (((1,), (1,)), ((), ())), preferred_element_type=F32)
        p = _softmax_rows(sc * scale)
        do = do_ref[...].astype(BF16)
        dp = lax.dot_general(do, v_ref[...].astype(BF16), (((1,), (1,)), ((), ())), preferred_element_type=F32)
        dv_ref[...] += lax.dot_general(p.astype(BF16), do, (((0,), (0,)), ((), ())), preferred_element_type=F32)
        ds = ((p * (dp - jnp.sum(dp * p, axis=-1, keepdims=True))) * scale).astype(BF16)
        dqn = jnp.dot(ds, knb, preferred_element_type=F32)
        dkn_sc[...] += lax.dot_general(ds, qnb, (((0,), (0,)), ((), ())), preferred_element_type=F32)
        dgq_ref[...] += jnp.sum(dqn * qh, axis=0, keepdims=True)
        dqh = dqn * gq_ref[...]
        dq_ref[...] = (rq * (dqh - qh * jnp.mean(dqh * qh, axis=-1, keepdims=True))).astype(BF16)

        @pl.when(i == last)
        def _():
            dkn = dkn_sc[...]
            dgk_ref[...] += jnp.sum(dkn * kh, axis=0, keepdims=True)
            dkh = dkn * gk_ref[...]
            dk_ref[...] = rk * (dkh - kh * jnp.mean(dkh * kh, axis=-1, keepdims=True))

    fix = lambda h, i: (0, 0)
    qb = pl.BlockSpec((tm, hd), lambda h, i: (i, h))
    kb = pl.BlockSpec((MEM_LEN, hd), lambda h, i: (0, h))
    return pl.pallas_call(
        body, grid=(MEM_HEADS, s // tm),
        in_specs=[qb, kb, pl.BlockSpec((MEM_LEN, hd), lambda h, i: (0, MEM_HEADS + h)), qb,
                  pl.BlockSpec((1, hd), fix), pl.BlockSpec((1, hd), fix)],
        out_specs=[qb, kb, kb, pl.BlockSpec((1, hd), fix), pl.BlockSpec((1, hd), fix)],
        out_shape=[SDS((s, MEM_HEADS * hd), BF16), SDS((MEM_LEN, MEM_HEADS * hd), F32), SDS((MEM_LEN, MEM_HEADS * hd), F32),
                   SDS((1, hd), F32), SDS((1, hd), F32)],
        scratch_shapes=[pltpu.VMEM((MEM_LEN, hd), F32)],
        compiler_params=_cp(2), name="memattn_bwd")(qm, kvm, kvm, d_o, gq, gk)


def _ffn_specs(tm, tn, nbj, s, order_ji):
    if order_ji:
        ij = lambda f: (lambda j, i: f(i, j))
    else:
        ij = lambda f: f
    prev = lambda i: jnp.maximum(i * (tm // FFN_HALO) - 1, 0)
    cur_g = pl.BlockSpec((tm, tn), ij(lambda i, j: (i, j)))
    cur_v = pl.BlockSpec((tm, tn), ij(lambda i, j: (i, j + nbj)))
    halo_g = pl.BlockSpec((FFN_HALO, tn), ij(lambda i, j: (prev(i), j)))
    halo_v = pl.BlockSpec((FFN_HALO, tn), ij(lambda i, j: (prev(i), j + nbj)))
    w_g = pl.BlockSpec((8, tn), ij(lambda i, j: (0, j)))
    w_v = pl.BlockSpec((8, tn), ij(lambda i, j: (0, j + nbj)))
    b_g = pl.BlockSpec((1, tn), ij(lambda i, j: (0, j)))
    b_v = pl.BlockSpec((1, tn), ij(lambda i, j: (0, j + nbj)))
    return cur_g, cur_v, halo_g, halo_v, w_g, w_v, b_g, b_v


FFN_STRIP = 16


def _conv3_rows(ext, w_ref, b_ref, o, n):
    return (w_ref[0:1, :] * ext[FFN_HALO - 2 + o:FFN_HALO - 2 + o + n, :] + w_ref[1:2, :] * ext[FFN_HALO - 1 + o:FFN_HALO - 1 + o + n, :]
            + w_ref[2:3, :] * ext[FFN_HALO + o:FFN_HALO + o + n, :] + b_ref[...])


def _ffn_fwd(up0, w8, b, tm, tn):
    s = up0.shape[0]
    nbj = D_FF // tn

    def body(g_ref, v_ref, gh_ref, vh_ref, wg_ref, wv_ref, bg_ref, bv_ref, act_ref, extg, extv):
        first = pl.program_id(0) == 0
        for ext, h_ref, c_ref in ((extg, gh_ref, g_ref), (extv, vh_ref, v_ref)):
            ext[0:FFN_HALO, :] = jnp.where(first, 0.0, h_ref[...])
            ext[FFN_HALO:, :] = c_ref[...]
        for r in range(tm // FFN_STRIP):
            o = r * FFN_STRIP
            ug = _conv3_rows(extg, wg_ref, bg_ref, o, FFN_STRIP)
            uv = _conv3_rows(extv, wv_ref, bv_ref, o, FFN_STRIP)
            act_ref[o:o + FFN_STRIP, :] = ((ug * jax.nn.sigmoid(ug)) * uv).astype(BF16)

    specs = _ffn_specs(tm, tn, nbj, s, False)
    return pl.pallas_call(
        body, grid=(s // tm, nbj), in_specs=list(specs),
        out_specs=pl.BlockSpec((tm, tn), lambda i, j: (i, j)), out_shape=SDS((s, D_FF), BF16),
        scratch_shapes=[pltpu.VMEM((tm + FFN_HALO, tn), F32), pltpu.VMEM((tm + FFN_HALO, tn), F32)],
        compiler_params=_cp(2), name="ffn_fwd")(up0, up0, up0, up0, w8, w8, b, b)


def _ffn_bwd(d_act, up0, w8, b, tm, tn):
    s = up0.shape[0]
    nbj = D_FF // tn
    te = tm + FFN_HALO

    def body(da_ref, dan_ref, g_ref, v_ref, gh_ref, vh_ref, gn_ref, vn_ref, wg_ref, wv_ref, bg_ref, bv_ref,
             og_ref, ov_ref, dbg_ref, dbv_ref, dwg_ref, dwv_ref, extg, extv, extdg, extdv, accg, accv):
        i = pl.program_id(1)
        first = i == 0
        last = i == pl.num_programs(1) - 1

        @pl.when(first)
        def _():
            for r in (dbg_ref, dbv_ref, dwg_ref, dwv_ref):
                r[...] = jnp.zeros_like(r)

        for ext, h_ref, c_ref, n_ref in ((extg, gh_ref, g_ref, gn_ref), (extv, vh_ref, v_ref, vn_ref)):
            ext[0:FFN_HALO, :] = jnp.where(first, 0.0, h_ref[...])
            ext[FFN_HALO:FFN_HALO + tm, :] = c_ref[...]
            ext[FFN_HALO + tm:, :] = n_ref[...]

        def fold8(x):
            acc = x[0:8, :]
            for q in range(1, x.shape[0] // 8):
                acc = acc + x[q * 8:(q + 1) * 8, :]
            return acc

        def taps(ext, o, n):
            return [ext[FFN_HALO - 2 + k + o:FFN_HALO - 2 + k + o + n, :] for k in range(3)]

        accg[...] = jnp.zeros_like(accg)
        accv[...] = jnp.zeros_like(accv)

        def gate_bwd(o, n, da, own_rows):
            xg, xv = taps(extg, o, n), taps(extv, o, n)
            ug = wg_ref[0:1, :] * xg[0] + wg_ref[1:2, :] * xg[1] + wg_ref[2:3, :] * xg[2] + bg_ref[...]
            uv = wv_ref[0:1, :] * xv[0] + wv_ref[1:2, :] * xv[1] + wv_ref[2:3, :] * xv[2] + bv_ref[...]
            sg = jax.nn.sigmoid(ug)
            dgt = da * uv * (sg * (1.0 + ug * (1.0 - sg)))
            dvl = da * (ug * sg)
            extdg[o:o + n, :] = dgt
            extdv[o:o + n, :] = dvl
            if own_rows:
                for acc, d, x in ((accg, dgt, xg), (accv, dvl, xv)):
                    acc[0] += fold8(d)
                    for k in range(3):
                        acc[1 + k] += fold8(d * x[k])

        for r in range(tm // FFN_STRIP):
            gate_bwd(r * FFN_STRIP, FFN_STRIP, da_ref[r * FFN_STRIP:(r + 1) * FFN_STRIP, :], True)
        gate_bwd(tm, FFN_HALO, jnp.where(last, 0.0, dan_ref[...]), False)

        for extd, w_ref, o_ref, db_ref, dw_ref, acc in ((extdg, wg_ref, og_ref, dbg_ref, dwg_ref, accg),
                                                        (extdv, wv_ref, ov_ref, dbv_ref, dwv_ref, accv)):
            for r in range(tm // FFN_STRIP):
                o = r * FFN_STRIP
                o_ref[o:o + FFN_STRIP, :] = (w_ref[2:3, :] * extd[o:o + FFN_STRIP, :] + w_ref[1:2, :] * extd[o + 1:o + 1 + FFN_STRIP, :]
                                             + w_ref[0:1, :] * extd[o + 2:o + 2 + FFN_STRIP, :]).astype(BF16)
            db_ref[...] += jnp.sum(acc[0], axis=0, keepdims=True)
            for k in range(3):
                dw_ref[k:k + 1, :] += jnp.sum(acc[1 + k], axis=0, keepdims=True)

    cur_g, cur_v, halo_g, halo_v, w_g, w_v, b_g, b_v = _ffn_specs(tm, tn, nbj, s, True)
    nxt_row = lambda i: jnp.minimum((i + 1) * (tm // FFN_HALO), s // FFN_HALO - 1)
    cur = pl.BlockSpec((tm, tn), lambda j, i: (i, j))
    nxt = pl.BlockSpec((FFN_HALO, tn), lambda j, i: (nxt_row(i), j))
    nxt_v = pl.BlockSpec((FFN_HALO, tn), lambda j, i: (nxt_row(i), j + nbj))
    acc1 = pl.BlockSpec((1, tn), lambda j, i: (0, j))
    acc8 = pl.BlockSpec((8, tn), lambda j, i: (0, j))
    return pl.pallas_call(
        body, grid=(nbj, s // tm), in_specs=[cur, nxt, cur_g, cur_v, halo_g, halo_v, nxt, nxt_v, w_g, w_v, b_g, b_v],
        out_specs=[cur, cur, acc1, acc1, acc8, acc8],
        out_shape=[SDS((s, D_FF), BF16), SDS((s, D_FF), BF16), SDS((1, D_FF), F32), SDS((1, D_FF), F32),
                   SDS((8, D_FF), F32), SDS((8, D_FF), F32)],
        scratch_shapes=[pltpu.VMEM((tm + 2 * FFN_HALO, tn), F32), pltpu.VMEM((tm + 2 * FFN_HALO, tn), F32),
                        pltpu.VMEM((te, tn), F32), pltpu.VMEM((te, tn), F32),
                        pltpu.VMEM((4, 8, tn), F32), pltpu.VMEM((4, 8, tn), F32)],
        compiler_params=_cp(2), name="ffn_bwd")(d_act, d_act, up0, up0, up0, up0, up0, up0, w8, w8, b, b)


def _down_loss(act, w_down, x2, target, tm):
    s = act.shape[0]

    def body(a_ref, w_ref, x_ref, t_ref, dyf_ref, dyb_ref, ls_ref):
        @pl.when(pl.program_id(0) == 0)
        def _():
            ls_ref[...] = jnp.zeros_like(ls_ref)

        y = x_ref[...] + jnp.dot(a_ref[...], w_ref[...], preferred_element_type=F32)
        e = y - t_ref[...]
        ls_ref[...] += jnp.sum(e * e)
        dy = e * (1.0 / D_MODEL)
        dyf_ref[...] = dy
        dyb_ref[...] = dy.astype(BF16)

    row = lambda i: (i, 0)
    return pl.pallas_call(
        body, grid=(s // tm,),
        in_specs=[pl.BlockSpec((tm, D_FF), row), pl.BlockSpec((D_FF, D_MODEL), lambda i: (0, 0)),
                  pl.BlockSpec((tm, D_MODEL), row), pl.BlockSpec((tm, D_MODEL), row)],
        out_specs=[pl.BlockSpec((tm, D_MODEL), row), pl.BlockSpec((tm, D_MODEL), row), pl.BlockSpec((8, 128), lambda i: (0, 0))],
        out_shape=[SDS((s, D_MODEL), F32), SDS((s, D_MODEL), BF16), SDS((8, 128), F32)],
        compiler_params=_cp(1), name="down_loss")(act, w_down, x2, target)


def _adamw_math(w, g, m, v):
    mn = ADAM_B1 * m + (1.0 - ADAM_B1) * g
    vn = ADAM_B2 * v + (1.0 - ADAM_B2) * (g * g)
    m_hat = mn / (1.0 - ADAM_B1 ** ADAM_STEP)
    v_hat = vn / (1.0 - ADAM_B2 ** ADAM_STEP)
    return -ADAM_LR * (m_hat / (jnp.sqrt(v_hat) + ADAM_EPS) + ADAM_WD * w), mn, vn


def _adamw(w, g, m, v, name):
    rows, cols = w.shape
    tr = rows if rows <= 256 else (256 if rows % 256 == 0 else rows // 2)

    def body(w_ref, g_ref, m_ref, v_ref, d_ref, mo_ref, vo_ref):
        d_ref[...], mo_ref[...], vo_ref[...] = _adamw_math(w_ref[...], g_ref[...], m_ref[...], v_ref[...])

    blk = pl.BlockSpec((tr, cols), lambda i: (i, 0))
    return pl.pallas_call(body, grid=(rows // tr,), in_specs=[blk] * 4, out_specs=[blk] * 3,
                          out_shape=[SDS((rows, cols), F32)] * 3, compiler_params=_cp(1), name=name)(w, g, m, v)


def _adamw_small(ws, gs, ms, vs):
    n = len(ws)

    def body(*refs):
        ins, outs = refs[:4 * n], refs[4 * n:]
        for k in range(n):
            d, mn, vn = _adamw_math(ins[k][...], ins[n + k][...], ins[2 * n + k][...], ins[3 * n + k][...])
            outs[k][...] = d
            outs[n + k][...] = mn
            outs[2 * n + k][...] = vn

    vm = pl.BlockSpec(memory_space=pltpu.VMEM)
    outs = pl.pallas_call(body, in_specs=[vm] * (4 * n), out_specs=[vm] * (3 * n),
                          out_shape=[SDS(w.shape, F32) for w in ws] * 3, name="adamw_small")(*ws, *gs, *ms, *vs)
    return outs[:n], outs[n:2 * n], outs[2 * n:]


ANY = pl.BlockSpec(memory_space=pl.ANY)


def _coords():
    return lax.axis_index("x"), lax.axis_index("y"), lax.axis_index("c")


def _other_chips(x, y):
    return [(1 - x, y), (x, 1 - y), (1 - x, 1 - y)]


D2D_CHUNKS = 8
ICI_CHUNKS = 4


def _row_chunks(n_rows, n_chunks, align):
    step = -(-n_rows // (n_chunks * align)) * align
    return [(r, min(step, n_rows - r)) for r in range(0, n_rows, step)]


def _ag_copy(out_ref, send_sems, recv_sems, k, shard, base, r0, nr, to, src=None):
    rows_ = pl.ds(pl.multiple_of(base + r0, 16), nr)
    dst = out_ref.at[shard, rows_]
    return pltpu.make_async_remote_copy(src_ref=dst if src is None else src.at[rows_], dst_ref=dst, send_sem=send_sems.at[k],
                                        recv_sem=recv_sems.at[k], device_id=to, device_id_type=MESH)


def _ag_send(w_ref, out_ref, send_sems, recv_sems):
    x, y, c = _coords()
    half_rows = w_ref.shape[0] // 2
    for k, (px, py) in enumerate(_other_chips(x, y)):
        for r0, nr in _row_chunks(half_rows, ICI_CHUNKS, 16):
            _ag_copy(out_ref, send_sems, recv_sems, k, 2 * x + y, c * half_rows, r0, nr, (px, py, c), src=w_ref).start()


def _ag_finish(w_ref, out_ref, send_sems, recv_sems):
    x, y, c = _coords()
    half_rows = w_ref.shape[0] // 2
    chips = _other_chips(x, y)
    sibling = (x, y, 1 - c)
    for k, (px, py) in enumerate(chips):
        _ag_copy(out_ref, send_sems, recv_sems, k, 2 * px + py, c * half_rows, 0, half_rows, (px, py, c)).wait_recv()
        for r0, nr in _row_chunks(half_rows, ICI_CHUNKS, 16):
            _ag_copy(out_ref, send_sems, recv_sems, 3 + k, 2 * px + py, c * half_rows, r0, nr, sibling).start()
    for k, (px, py) in enumerate(chips):
        _ag_copy(out_ref, send_sems, recv_sems, 3 + k, 2 * px + py, (1 - c) * half_rows, 0, half_rows, sibling).wait_recv()
    for k in range(6):
        _ag_copy(out_ref, send_sems, recv_sems, k, 2 * x + y, c * half_rows, 0, half_rows, sibling).wait_send()


def _ag_weights(wsh):
    rows, cols = wsh.shape

    def body(w_ref, out_ref, send_sems, recv_sems):
        _ag_send(w_ref, out_ref, send_sems, recv_sems)
        _ag_finish(w_ref, out_ref, send_sems, recv_sems)

    return pl.pallas_call(
        body, in_specs=[ANY], out_specs=ANY, out_shape=SDS((4, rows, cols), wsh.dtype),
        scratch_shapes=[pltpu.SemaphoreType.DMA((6,)), pltpu.SemaphoreType.DMA((6,))],
        name="ag_weights")(wsh)


def _rs_swap_halves(gfull, tag):
    n_sh, rows, cols = gfull.shape
    half_rows = rows // 2

    def body(g_ref, recv_ref, send_sem, recv_sem):
        x, y, c = _coords()
        sib_base = (1 - c) * half_rows
        for sh in range(n_sh):
            for r0, nr in _row_chunks(half_rows, D2D_CHUNKS, 8):
                pltpu.make_async_remote_copy(
                    src_ref=g_ref.at[sh, pl.ds(pl.multiple_of(sib_base + r0, 8), nr)], dst_ref=recv_ref.at[sh, pl.ds(r0, nr)],
                    send_sem=send_sem, recv_sem=recv_sem, device_id=(x, y, 1 - c), device_id_type=MESH).start()
        pltpu.make_async_remote_copy(src_ref=recv_ref, dst_ref=recv_ref, send_sem=send_sem, recv_sem=recv_sem,
                                     device_id=(x, y, 1 - c), device_id_type=MESH).wait()

    return pl.pallas_call(
        body, in_specs=[ANY], out_specs=ANY, out_shape=SDS((n_sh, half_rows, cols), gfull.dtype),
        scratch_shapes=[pltpu.SemaphoreType.DMA, pltpu.SemaphoreType.DMA], name="rs_swap_halves" + tag)(gfull)


def _rs_add_pair(gfull, recv, core, tr, tag):
    n_sh, rows, cols = gfull.shape
    half_rows = rows // 2
    nblk = half_rows // tr

    def body(c_ref, g_ref, r_ref, o_ref, ob_ref):
        acc = g_ref[...] + r_ref[...]
        o_ref[...] = acc
        ob_ref[...] = acc.astype(BF16)

    out = pl.BlockSpec((None, tr, cols), lambda sh, i, c_ref: (sh, i, 0))
    gs = pltpu.PrefetchScalarGridSpec(
        num_scalar_prefetch=1, grid=(n_sh, nblk),
        in_specs=[pl.BlockSpec((None, tr, cols), lambda sh, i, c_ref: (sh, c_ref[0] * nblk + i, 0)), out],
        out_specs=[out, out])
    return pl.pallas_call(body, grid_spec=gs, out_shape=[SDS((n_sh, half_rows, cols), F32), SDS((n_sh, half_rows, cols), BF16)],
                          compiler_params=_cp(2), name="rs_add_pair" + tag)(core, gfull, recv)


def _rs_send(cs_ref, recv_ref, send_sems, recv_sems):
    x, y, c = _coords()
    half_rows = cs_ref.shape[1]
    for k, (px, py) in enumerate(_other_chips(x, y)):
        for r0, nr in _row_chunks(half_rows, ICI_CHUNKS, 16):
            pltpu.make_async_remote_copy(
                src_ref=cs_ref.at[2 * px + py, pl.ds(r0, nr)], dst_ref=recv_ref.at[k, pl.ds(r0, nr)],
                send_sem=send_sems.at[k], recv_sem=recv_sems.at[k], device_id=(px, py, c), device_id_type=MESH).start()


def _rs_wait(recv_ref, send_sems, recv_sems):
    x, y, c = _coords()
    for k, (px, py) in enumerate(_other_chips(x, y)):
        pltpu.make_async_remote_copy(src_ref=recv_ref.at[k], dst_ref=recv_ref.at[k], send_sem=send_sems.at[k],
                                     recv_sem=recv_sems.at[k], device_id=(px, py, c), device_id_type=MESH).wait()


def _rs_to_owner(chipsum):
    n_sh, half_rows, cols = chipsum.shape

    def body(cs_ref, recv_ref, send_sems, recv_sems):
        _rs_send(cs_ref, recv_ref, send_sems, recv_sems)
        _rs_wait(recv_ref, send_sems, recv_sems)

    return pl.pallas_call(
        body, in_specs=[ANY], out_specs=ANY, out_shape=SDS((3, half_rows, cols), chipsum.dtype),
        scratch_shapes=[pltpu.SemaphoreType.DMA((3,)), pltpu.SemaphoreType.DMA((3,))], name="rs_to_owner")(chipsum)


def _rs_add_chips(chipsum, recv, shard_core, tr, tag):
    _, half_rows, cols = chipsum.shape

    def body(s_ref, m_ref, r0_ref, r1_ref, r2_ref, o_ref):
        o_ref[...] = ((m_ref[...] + r0_ref[...].astype(F32)) + r1_ref[...].astype(F32)) + r2_ref[...].astype(F32)

    gs = pltpu.PrefetchScalarGridSpec(
        num_scalar_prefetch=1, grid=(half_rows // tr,),
        in_specs=[pl.BlockSpec((None, tr, cols), lambda i, s_ref: (s_ref[0], i, 0))]
        + [pl.BlockSpec((None, tr, cols), (lambda k: lambda i, s_ref: (k, i, 0))(k)) for k in range(3)],
        out_specs=pl.BlockSpec((None, tr, cols), lambda i, s_ref: (s_ref[1], i, 0)))
    return pl.pallas_call(body, grid_spec=gs, out_shape=SDS((2, half_rows, cols), F32),
                          compiler_params=_cp(1), name="rs_add_chips" + tag)(shard_core, chipsum, recv, recv, recv)


def _rs_join_halves(buf, tag):
    _, half_rows, cols = buf.shape

    def body(b_ref, out_ref, send_sem, recv_sem):
        x, y, c = _coords()
        for r0, nr in _row_chunks(half_rows, D2D_CHUNKS, 8):
            pltpu.make_async_remote_copy(src_ref=out_ref.at[c, pl.ds(r0, nr)], dst_ref=out_ref.at[c, pl.ds(r0, nr)], send_sem=send_sem,
                                         recv_sem=recv_sem, device_id=(x, y, 1 - c), device_id_type=MESH).start()
        pltpu.make_async_remote_copy(src_ref=out_ref.at[c], dst_ref=out_ref.at[c], send_sem=send_sem, recv_sem=recv_sem,
                                     device_id=(x, y, 1 - c), device_id_type=MESH).wait()

    return pl.pallas_call(
        body, in_specs=[ANY], out_specs=ANY, out_shape=SDS(buf.shape, buf.dtype), input_output_aliases={0: 0},
        scratch_shapes=[pltpu.SemaphoreType.DMA, pltpu.SemaphoreType.DMA], name="rs_join_halves" + tag)(buf)


BIG = [("w_in", (1024, 1440), 1), ("w_uq", (256, 768), 1), ("w_ukv", (128, 1024), 1), ("w_out", (1024, 1024), 0),
       ("w_mem_q", (1024, 1024), 0), ("w_mem_kv", (1024, 2048), 1), ("w_mem_o", (1024, 1024), 0),
       ("w_up", (1024, 5632), 1), ("w_down", (2816, 1024), 0)]
SMALL_REP = [("mix_norm_g", 1024), ("b_conv_in", 1024), ("b_conv_dw", 512), ("conv_ln_g", 512), ("conv_ln_b", 512),
             ("q_lat_norm_g", 256), ("kv_lat_norm_g", 128), ("q_norm_g", 96), ("k_norm_g", 96), ("mem_norm_x_g", 1024),
             ("mem_norm_m_g", 1024), ("mem_q_norm_g", 256), ("mem_k_norm_g", 256), ("ffn_norm_g", 1024), ("b_ffn_dw", 5632)]
SMALL_SH = [("w_conv_dw", (31, 512)), ("w_ffn_dw", (3, 5632))]


def _shard_shape(shape, axis):
    return tuple(d // 4 if a == axis else d for a, d in enumerate(shape))


def _pack_rows(parts, rows, cols):
    flat = jnp.concatenate([p.reshape(-1) for p in parts])
    flat = jnp.pad(flat, (0, rows * cols - flat.shape[0]))
    return flat.reshape(rows, cols)


AG_EARLY, AG_LATE = BIG[:3], BIG[3:]
RS_REST, RS_FFN = BIG[:7], BIG[7:]


def _group_rows(group):
    used = sum(_shard_shape(shape, axis)[0] * _shard_shape(shape, axis)[1] // PACK_COLS for _, shape, axis in group)
    return -(-used // 32) * 32


def _pick_rows(n, cap=384):
    return max(r for r in range(16, cap + 1, 16) if n % r == 0)


def _pack_big_shards(ws, group):
    parts = [ws[n].reshape(-1, PACK_COLS) for n, _, _ in group]
    used = sum(p.shape[0] for p in parts)
    pad = _group_rows(group) - used
    return jnp.concatenate(parts + ([jnp.zeros((pad, PACK_COLS), parts[0].dtype)] if pad else []), axis=0)


def _unpack_big_shards(packed, group):
    out, r = {}, 0
    for n, shape, axis in group:
        sh = _shard_shape(shape, axis)
        nr = sh[0] * sh[1] // PACK_COLS
        out[n] = packed[r:r + nr].reshape(sh)
        r += nr
    return out


def _unpack_gathered(g, group):
    out, r = {}, 0
    for n, shape, axis in group:
        sh = _shard_shape(shape, axis)
        nr = sh[0] * sh[1] // PACK_COLS
        part = g[:, r:r + nr]
        if axis == 0:
            out[n] = part.reshape(shape)
        else:
            out[n] = part.reshape((4,) + sh).transpose(1, 0, 2).reshape(shape)
        r += nr
    return out


def _pack_full_grads(gs, group):
    parts = []
    for n, shape, axis in group:
        sh = _shard_shape(shape, axis)
        nr = sh[0] * sh[1] // PACK_COLS
        if axis == 0:
            parts.append(gs[n].reshape(4, nr, PACK_COLS))
        else:
            parts.append(gs[n].reshape(shape[0], 4, sh[1]).transpose(1, 0, 2).reshape(4, nr, PACK_COLS))
    pad = _group_rows(group) - sum(p.shape[1] for p in parts)
    return jnp.concatenate(parts + ([jnp.zeros((4, pad, PACK_COLS), F32)] if pad else []), axis=1)


def _rs_first(gfull, core_idx, tag):
    tr = _pick_rows(gfull.shape[1] // 2)
    return _rs_add_pair(gfull, _rs_swap_halves(gfull, tag), core_idx.reshape(1), tr, tag)


def _rs_last(chipsum, recv, shard_idx, core_idx, tag):
    tr = _pick_rows(chipsum.shape[1])
    red = _rs_add_chips(chipsum, recv, jnp.stack([shard_idx, core_idx]), tr, tag)
    return _rs_join_halves(red, tag).reshape(2 * chipsum.shape[1], chipsum.shape[2])


def _rope_tables(positions):
    inv_freq = ROPE_THETA ** (-jnp.arange(0, ROPE, 2, dtype=F32) / ROPE)
    ang = positions.astype(F32)[:, None] * inv_freq
    cos, sin = jnp.cos(ang), jnp.sin(ang)
    s = positions.shape[0]
    cosf = jnp.concatenate([jnp.ones((s, NOPE), F32), cos, cos, jnp.ones((s, HEAD_PAD - HEAD_DIM), F32)], axis=-1)
    sinf = jnp.concatenate([jnp.zeros((s, NOPE), F32), -sin, sin, jnp.zeros((s, HEAD_PAD - HEAD_DIM), F32)], axis=-1)
    return cosf, sinf


def _pad_heads(w, per_head):
    k = w.shape[0]
    w3 = w.reshape(k, HEADS, per_head)
    return jnp.pad(w3, ((0, 0), (0, 0), (0, HEAD_PAD - per_head))).reshape(k, HEADS * HEAD_PAD)


def _layer_grads(x, mem, positions, target, wf, w_late, sp, shard_idx, core_idx):
    wf = dict(wf)
    s = x.shape[0]
    tm = _row_tile(s, 512)
    tc = _row_tile(s, 256)
    tb = 512 if s % 512 == 0 and s > 512 else s // 2
    row2 = lambda a: a.reshape(1, -1)

    w_in = wf["w_in"]
    w_in_pad = jnp.concatenate([w_in[:, :1408], jnp.zeros((D_MODEL, NOPE), BF16), w_in[:, 1408:],
                                jnp.zeros((D_MODEL, HEAD_PAD - HEAD_DIM), BF16)], axis=1)
    w_uq_pad = _pad_heads(wf["w_uq"], HEAD_DIM)
    w_ukv = wf["w_ukv"]
    gq_pad = jnp.pad(sp["q_norm_g"], (0, HEAD_PAD - HEAD_DIM)).reshape(1, HEAD_PAD)
    gk_pad = jnp.pad(sp["k_norm_g"], (0, HEAD_PAD - HEAD_DIM)).reshape(1, HEAD_PAD)
    w_dw32 = jnp.pad(sp["w_conv_dw"], ((0, 1), (0, 0)))
    w_ffn8 = jnp.pad(sp["w_ffn_dw"], ((0, 5), (0, 0)))
    b_ffn = row2(sp["b_ffn_dw"])
    cosf, sinf = _rope_tables(positions)

    z, h1 = _norm_linear(x, 0, D_MODEL, row2(sp["mix_norm_g"]), w_in_pad, F32, tm, IN_COLS_PAD, "in_proj")
    u, u0, u1 = _conv_fwd(z, row2(sp["b_conv_in"]), w_dw32, row2(sp["b_conv_dw"]), row2(sp["conv_ln_g"]), row2(sp["conv_ln_b"]), tc)
    q_raw, cqn = _norm_linear(z, 1024 // Q_RANK, Q_RANK, row2(sp["q_lat_norm_g"]), w_uq_pad, F32, tm, 1024, "q_up")
    kv_raw, ckvn = _norm_linear(z, 1280 // KV_RANK, KV_RANK, row2(sp["kv_lat_norm_g"]), w_ukv, F32, tm, 1024, "kv_up")
    qp, kp, vp = _mla_prep(q_raw, kv_raw, z, cosf, sinf, gq_pad, gk_pad, tc)
    o_f, o_b, lse, gathered = _attn_fwd(qp, kp, vp, tb, w_late)
    wf.update(_unpack_gathered(lax.dynamic_update_index_in_dim(gathered, w_late, shard_idx, 0), AG_LATE))
    w_out_u = wf["w_out"][:CONV_CH]
    w_out_o = jnp.pad(wf["w_out"][CONV_CH:].reshape(HEADS, NOPE, D_MODEL), ((0, 0), (NOPE, 0), (0, 0))).reshape(HEADS * HEAD_PAD, D_MODEL)
    w_up_g, w_up_v = wf["w_up"][:, :D_FF], wf["w_up"][:, D_FF:]
    (x1,) = _linear([(u, w_out_u), (o_b, w_out_o)], False, x, [F32], tm, 1024, "out_proj")

    qm, hq = _norm_linear(x1, 0, D_MODEL, row2(sp["mem_norm_x_g"]), wf["w_mem_q"], F32, tm, 1024, "memq_proj")
    kvm, hm = _norm_linear(mem, 0, D_MODEL, row2(sp["mem_norm_m_g"]), wf["w_mem_kv"], F32, MEM_LEN, 1024, "memkv_proj")
    gmq, gmk = row2(sp["mem_q_norm_g"]), row2(sp["mem_k_norm_g"])
    o_m = _memattn_fwd(qm, kvm, gmq, gmk, tm)
    (x2,) = _linear([(o_m, wf["w_mem_o"])], False, x1, [F32], tm, 1024, "memo_proj")

    up0, h3 = _norm_linear(x2, 0, D_MODEL, row2(sp["ffn_norm_g"]), wf["w_up"], F32, tm, D_FF // 2, "up_proj")
    act = _ffn_fwd(up0, w_ffn8, b_ffn, tc, D_FF // 2)
    dy_f, dy_b, lsum = _down_loss(act, wf["w_down"], x2, target, tm)

    g = {}
    (d_act,) = _linear([(dy_b, wf["w_down"])], True, None, [F32], tm, D_FF // 2, "down_bwd")
    g["w_down"] = _dw(act, dy_b, "dw_down")
    d_up0g, d_up0v, dbg, dbv, dwg, dwv = _ffn_bwd(d_act, up0, w_ffn8, b_ffn, tc, D_FF // 2)
    g["b_ffn_dw"] = jnp.concatenate([dbg, dbv], axis=1).reshape(-1)
    g["w_ffn_dw"] = jnp.concatenate([dwg[:3], dwv[:3]], axis=1)
    g["w_up"] = jnp.concatenate([_dw(h3, d_up0g, "dw_up_g"), _dw(h3, d_up0v, "dw_up_v")], axis=1)
    chipsum_ffn, wire_ffn = _rs_first(_pack_full_grads(g, RS_FFN), core_idx, "_ffn")
    d_x2f, d_x2b, dg = _linear_normbwd([(d_up0g, w_up_g), (d_up0v, w_up_v)], x2, 0, row2(sp["ffn_norm_g"]), dy_f,
                                       [F32, BF16], tc, "up_bwd")
    g["ffn_norm_g"] = dg.reshape(-1)

    (d_om,) = _linear([(d_x2b, wf["w_mem_o"])], True, None, [BF16], tm, 1024, "memo_bwd")
    g["w_mem_o"] = _dw(o_m, d_x2b, "dw_mem_o")
    d_qm, d_km, d_vm, dgq, dgk = _memattn_bwd(qm, kvm, d_om, gmq, gmk, tm)
    g["mem_q_norm_g"], g["mem_k_norm_g"] = dgq.reshape(-1), dgk.reshape(-1)
    d_kvm = jnp.concatenate([d_km, d_vm], axis=1)
    g["w_mem_q"] = _dw(hq, d_qm, "dw_mem_q")
    g["w_mem_kv"] = _dw(hm, d_kvm, "dw_mem_kv")
    d_x1f, d_x1b, dg = _linear_normbwd([(d_qm, wf["w_mem_q"])], x1, 0, row2(sp["mem_norm_x_g"]), d_x2f, [F32, BF16], tm, "memq_bwd")
    g["mem_norm_x_g"] = dg.reshape(-1)
    _, dg = _linear_normbwd([(d_kvm, wf["w_mem_kv"])], mem, 0, row2(sp["mem_norm_m_g"]), None, [BF16], MEM_LEN, "memkv_bwd")
    g["mem_norm_m_g"] = dg.reshape(-1)

    (d_u,) = _linear([(d_x1b, w_out_u)], True, None, [F32], tm, CONV_CH, "out_bwd_u")
    d_of, d_ob = _linear([(d_x1b, w_out_o)], True, None, [F32, BF16], tm, 1024, "out_bwd_o")
    dw_out_u = _dw(u, d_x1b, "dw_out_u")
    dw_out_o = _dw(o_b, d_x1b, "dw_out_o")
    g["w_out"] = jnp.concatenate([dw_out_u, dw_out_o.reshape(HEADS, HEAD_PAD, D_MODEL)[:, NOPE:].reshape(HEADS * NOPE, D_MODEL)], axis=0)
    delta = _attn_delta(d_of, o_f, tb)
    dqp, dkp, dvp, recv_ffn = _attn_bwd(qp, kp, vp, d_ob, lse, delta, tb, wire_ffn)
    g_ffn_packed = _rs_last(chipsum_ffn, recv_ffn, shard_idx, core_idx, "_ffn")
    d_qraw, d_kvraw, d_kr, dgq, dgk = _mla_prep_bwd(dqp, dkp, dvp, q_raw, kv_raw, z, cosf, sinf, gq_pad, gk_pad, tc)
    g["q_norm_g"], g["k_norm_g"] = dgq.reshape(-1)[:HEAD_DIM], dgk.reshape(-1)[:HEAD_DIM]
    g["w_uq"] = _dw(cqn, d_qraw, "dw_uq").reshape(Q_RANK, HEADS, HEAD_PAD)[:, :, :HEAD_DIM].reshape(Q_RANK, HEADS * HEAD_DIM)
    g["w_ukv"] = _dw(ckvn, d_kvraw, "dw_ukv")
    d_cq, dg = _linear_normbwd([(d_qraw, w_uq_pad)], z, 1024 // Q_RANK, row2(sp["q_lat_norm_g"]), None, [BF16], tm, "q_up_bwd")
    g["q_lat_norm_g"] = dg.reshape(-1)
    d_ckv, dg = _linear_normbwd([(d_kvraw, w_ukv)], z, 1280 // KV_RANK, row2(sp["kv_lat_norm_g"]), None, [BF16], tm, "kv_up_bwd")
    g["kv_lat_norm_g"] = dg.reshape(-1)
    d_u1, dlg, dlb, dbdw = _conv_bwd_ln(d_u, u1, row2(sp["conv_ln_g"]), row2(sp["conv_ln_b"]), tc)
    g["conv_ln_g"], g["conv_ln_b"], g["b_conv_dw"] = dlg.reshape(-1), dlb.reshape(-1), dbdw.reshape(-1)
    d_conv, dw_dw, dbin = _conv_bwd_dw(d_u1, u0, z, row2(sp["b_conv_in"]), w_dw32, tc)
    g["w_conv_dw"], g["b_conv_in"] = dw_dw[:CONV_WIDTH], dbin.reshape(-1)
    pieces = [(d_conv, w_in_pad[:, :1024]), (d_cq, w_in_pad[:, 1024:1280]), (d_ckv, w_in_pad[:, 1280:1408]), (d_kr, w_in_pad[:, 1408:])]
    dw_in = [_dw(h1, d, "dw_in_%d" % k) for k, (d, _) in enumerate(pieces)]
    g["w_in"] = jnp.concatenate([dw_in[0], dw_in[1], dw_in[2], dw_in[3][:, NOPE:HEAD_DIM]], axis=1)
    grad_x, dg = _linear_normbwd(pieces, x, 0, row2(sp["mix_norm_g"]), d_x1f, [F32], tm, "in_bwd")
    g["mix_norm_g"] = dg.reshape(-1)
    return lsum[0, 0], grad_x, g, g_ffn_packed


def kernel(x, mem, positions, mix_norm_g, w_in, b_conv_in, w_conv_dw, b_conv_dw, conv_ln_g, conv_ln_b, q_lat_norm_g, w_uq, kv_lat_norm_g, w_ukv, q_norm_g, k_norm_g, w_out, mem_norm_x_g, mem_norm_m_g, w_mem_q, w_mem_kv, mem_q_norm_g, mem_k_norm_g, w_mem_o, ffn_norm_g, w_up, w_ffn_dw, b_ffn_dw, w_down, loss_target, m_mix_norm_g, m_w_in, m_b_conv_in, m_w_conv_dw, m_b_conv_dw, m_conv_ln_g, m_conv_ln_b, m_q_lat_norm_g, m_w_uq, m_kv_lat_norm_g, m_w_ukv, m_q_norm_g, m_k_norm_g, m_w_out, m_mem_norm_x_g, m_mem_norm_m_g, m_w_mem_q, m_w_mem_kv, m_mem_q_norm_g, m_mem_k_norm_g, m_w_mem_o, m_ffn_norm_g, m_w_up, m_w_ffn_dw, m_b_ffn_dw, m_w_down, v_mix_norm_g, v_w_in, v_b_conv_in, v_w_conv_dw, v_b_conv_dw, v_conv_ln_g, v_conv_ln_b, v_q_lat_norm_g, v_w_uq, v_kv_lat_norm_g, v_w_ukv, v_q_norm_g, v_k_norm_g, v_w_out, v_mem_norm_x_g, v_mem_norm_m_g, v_w_mem_q, v_w_mem_kv, v_mem_q_norm_g, v_mem_k_norm_g, v_w_mem_o, v_ffn_norm_g, v_w_up, v_w_ffn_dw, v_b_ffn_dw, v_w_down):
    names = ["mix_norm_g", "w_in", "b_conv_in", "w_conv_dw", "b_conv_dw", "conv_ln_g", "conv_ln_b", "q_lat_norm_g", "w_uq",
             "kv_lat_norm_g", "w_ukv", "q_norm_g", "k_norm_g", "w_out", "mem_norm_x_g", "mem_norm_m_g", "w_mem_q", "w_mem_kv",
             "mem_q_norm_g", "mem_k_norm_g", "w_mem_o", "ffn_norm_g", "w_up", "w_ffn_dw", "b_ffn_dw", "w_down"]
    loc = locals()
    w = {n: loc[n] for n in names}
    m = {n: loc["m_" + n] for n in names}
    v = {n: loc["v_" + n] for n in names}
    shard_idx = 2 * lax.axis_index("x") + lax.axis_index("y")

    shard_idx = shard_idx.astype(jnp.int32)
    core_idx = lax.axis_index("c").astype(jnp.int32)

    w_local = {n: w[n][0] for n, _, _ in BIG}
    w_early = _pack_big_shards(w_local, AG_EARLY).astype(BF16)
    w_late = _pack_big_shards(w_local, AG_LATE).astype(BF16)
    wf = _unpack_gathered(lax.dynamic_update_index_in_dim(_ag_weights(w_early), w_early, shard_idx, 0), AG_EARLY)

    small_sh_full = {}
    gather_in = []
    for n, (r, c) in SMALL_SH:
        csh = c // 4
        slab = lax.dynamic_update_slice(jnp.zeros((r, c), F32), w[n][0], (0, shard_idx * csh))
        gather_in.append(slab.reshape(-1))
    gather_rows = 256
    gathered_small = _allreduce_small_named(_pack_rows(gather_in, gather_rows, SMALL_COLS), "gather_small") * 0.5
    off = 0
    for n, (r, c) in SMALL_SH:
        small_sh_full[n] = gathered_small.reshape(-1)[off:off + r * c].reshape(r, c)
        off += r * c
    sp = {n: w[n][0] for n, _ in SMALL_REP}
    sp.update(small_sh_full)

    lsum, grad_x, g, g_ffn_packed = _layer_grads(x[0], mem[0], positions[0], loss_target[0], wf, w_late, sp, shard_idx, core_idx)

    small_parts = [jnp.full((SMALL_COLS,), lsum, F32)] + [g[n] for n, _ in SMALL_REP] + [g[n] for n, _ in SMALL_SH]
    small_rows = 368
    small_sum = _allreduce_small_named(_pack_rows(small_parts, small_rows, SMALL_COLS), "allreduce_small").reshape(-1)
    loss = small_sum[0] * (0.5 / D_MODEL)
    gs = {}
    off = SMALL_COLS
    for n, sz in SMALL_REP:
        gs[n] = small_sum[off:off + sz].reshape(w[n].shape)
        off += sz
    for n, (r, c) in SMALL_SH:
        full = small_sum[off:off + r * c].reshape(r, c)
        gs[n] = lax.dynamic_slice(full, (0, shard_idx * (c // 4)), (r, c // 4)).reshape(w[n].shape)
        off += r * c

    chipsum, chipsum_wire = _rs_first(_pack_full_grads(g, RS_REST), core_idx, "_rest")
    g_rest_packed = _rs_last(chipsum, _rs_to_owner(chipsum_wire), shard_idx, core_idx, "_rest")
    g_big = {**_unpack_big_shards(g_rest_packed, RS_REST), **_unpack_big_shards(g_ffn_packed, RS_FFN)}
    gs.update({n: a[None] for n, a in g_big.items()})

    delta, new_m, new_v = {}, {}, {}
    for n, _, _ in BIG:
        d_n, m_n, v_n = _adamw(w[n][0], g_big[n], m[n][0], v[n][0], "adamw_" + n)
        delta[n], new_m[n], new_v[n] = d_n[None], m_n[None], v_n[None]
    small_names = [n for n, _ in SMALL_REP] + [n for n, _ in SMALL_SH]
    as2d = lambda a: a.reshape(-1, a.shape[-1])
    d_s, m_s, v_s = _adamw_small(*[[as2d(d[n]) for n in small_names] for d in (w, gs, m, v)])
    for k, n in enumerate(small_names):
        delta[n], new_m[n], new_v[n] = d_s[k].reshape(w[n].shape), m_s[k].reshape(w[n].shape), v_s[k].reshape(w[n].shape)

    return (loss, grad_x[None], *[gs[n] for n in names], *[delta[n] for n in names], *[new_m[n] for n in names],
            *[new_v[n] for n in names])


def _allreduce_small_named(v, name):
    rows, cols = v.shape

    def body(v_ref, out_ref, buf, send_sems, recv_sems):
        x, y, c = _coords()
        me = 4 * x + 2 * y + c
        buf[me] = v_ref[...]
        cps = []
        for r in range(1, 8):
            dx, dy, dc = (r >> 2) & 1, (r >> 1) & 1, r & 1
            to = (x + dx - 2 * x * dx, y + dy - 2 * y * dy, c + dc - 2 * c * dc)
            cp = pltpu.make_async_remote_copy(src_ref=v_ref, dst_ref=buf.at[me], send_sem=send_sems.at[r - 1],
                                              recv_sem=recv_sems.at[r - 1], device_id=to, device_id_type=MESH)
            cp.start()
            cps.append(cp)
        for cp in cps:
            cp.wait()
        acc = buf[0]
        for d in range(1, 8):
            acc = acc + buf[d]
        out_ref[...] = acc

    vm = pl.BlockSpec(memory_space=pltpu.VMEM)
    return pl.pallas_call(
        body, in_specs=[vm], out_specs=vm, out_shape=SDS((rows, cols), F32),
        scratch_shapes=[pltpu.VMEM((8, rows, cols), F32), pltpu.SemaphoreType.DMA((7,)), pltpu.SemaphoreType.DMA((7,))],
        name=name)(v)
```

```python
import math

import numpy as np
import jax
import jax.numpy as jnp
from jax import lax
from jax.experimental import pallas as pl
from jax.experimental.pallas import tpu as pltpu

F32 = jnp.float32
BF16 = jnp.bfloat16
SDS = jax.ShapeDtypeStruct
MESH = pl.DeviceIdType.MESH

D_MODEL = 1024
EPS = 1e-6
CONV_CH = 512
CONV_WIDTH = 31
CONV_HALO = 32
HEADS = 8
NOPE = 64
ROPE = 32
HEAD_DIM = NOPE + ROPE
HEAD_PAD = 128
Q_RANK = 256
KV_RANK = 128
CHUNK = 64
ROPE_THETA = 10000.0
IN_COLS_PAD = 1536
MEM_HEADS = 4
MEM_HEAD_DIM = 256
MEM_LEN = 256
D_FF = 2816
FFN_HALO = 8
ATT_SCALE = 1.0 / math.sqrt(HEAD_DIM)
LOG2E = math.log2(math.e)
LN2 = math.log(2.0)

ADAM_LR = 0.001
ADAM_B1 = 0.9
ADAM_B2 = 0.999
ADAM_EPS = 1e-08
ADAM_WD = 0.01
ADAM_STEP = 10

VMEM_LIMIT_V7X = 56 * 1024 * 1024
PACK_COLS = 1024
SMALL_COLS = 128


def _cp(n_axes):
    return pltpu.CompilerParams(dimension_semantics=("arbitrary",) * n_axes, vmem_limit_bytes=VMEM_LIMIT_V7X)


def _row_tile(s, want):
    return want if s % want == 0 else s


def _norm_linear(x, xcol, kdim, g, w, out_dtype, tm, tn, name):
    s = x.shape[0]
    n = w.shape[1]

    def body(x_ref, g_ref, w_ref, y_ref, hn_ref):
        @pl.when(pl.program_id(1) == 0)
        def _():
            xv = x_ref[...]
            r = lax.rsqrt(jnp.mean(xv * xv, axis=-1, keepdims=True) + EPS)
            hn_ref[...] = ((xv * r) * g_ref[...]).astype(BF16)

        y_ref[...] = jnp.dot(hn_ref[...], w_ref[...], preferred_element_type=F32).astype(y_ref.dtype)

    return pl.pallas_call(
        body, grid=(s // tm, n // tn),
        in_specs=[pl.BlockSpec((tm, kdim), lambda i, j: (i, xcol)), pl.BlockSpec((1, kdim), lambda i, j: (0, 0)),
                  pl.BlockSpec((kdim, tn), lambda i, j: (0, j))],
        out_specs=[pl.BlockSpec((tm, tn), lambda i, j: (i, j)), pl.BlockSpec((tm, kdim), lambda i, j: (i, 0))],
        out_shape=[SDS((s, n), out_dtype), SDS((s, kdim), BF16)],
        compiler_params=_cp(2), name=name)(x, g, w)


def _linear(pairs, nt, residual, out_dtypes, tm, tn, name):
    s = pairs[0][0].shape[0]
    n = pairs[0][1].shape[0] if nt else pairs[0][1].shape[1]
    n_pairs = len(pairs)
    has_res = residual is not None

    def body(*refs):
        a_refs = refs[:n_pairs]
        w_refs = refs[n_pairs:2 * n_pairs]
        res_ref = refs[2 * n_pairs] if has_res else None
        outs = refs[2 * n_pairs + int(has_res):]
        acc = None
        for a_ref, w_ref in zip(a_refs, w_refs):
            a = a_ref[...].astype(BF16)
            if nt:
                d = lax.dot_general(a, w_ref[...], (((1,), (1,)), ((), ())), preferred_element_type=F32)
            else:
                d = jnp.dot(a, w_ref[...], preferred_element_type=F32)
            acc = d if acc is None else acc + d
        if has_res:
            acc = res_ref[...] + acc
        for o in outs:
            o[...] = acc.astype(o.dtype)

    in_specs = [pl.BlockSpec((tm, a.shape[1]), lambda i, j: (i, 0)) for a, _ in pairs]
    if nt:
        in_specs += [pl.BlockSpec((tn, w.shape[1]), lambda i, j: (j, 0)) for _, w in pairs]
    else:
        in_specs += [pl.BlockSpec((w.shape[0], tn), lambda i, j: (0, j)) for _, w in pairs]
    args = [a for a, _ in pairs] + [w for _, w in pairs]
    if has_res:
        in_specs.append(pl.BlockSpec((tm, tn), lambda i, j: (i, j)))
        args.append(residual)
    outs = pl.pallas_call(
        body, grid=(s // tm, n // tn), in_specs=in_specs,
        out_specs=[pl.BlockSpec((tm, tn), lambda i, j: (i, j)) for _ in out_dtypes],
        out_shape=[SDS((s, n), dt) for dt in out_dtypes],
        compiler_params=_cp(2), name=name)(*args)
    return outs


def _linear_normbwd(pairs, x, xcol, g, d_res, out_dtypes, tm, name):
    s = pairs[0][0].shape[0]
    dn = pairs[0][1].shape[0]
    n_pairs = len(pairs)
    has_res = d_res is not None

    def body(*refs):
        a_refs = refs[:n_pairs]
        w_refs = refs[n_pairs:2 * n_pairs]
        x_ref, g_ref = refs[2 * n_pairs], refs[2 * n_pairs + 1]
        k = 2 * n_pairs + 2
        res_ref = refs[k] if has_res else None
        k += int(has_res)
        outs = refs[k:-1]
        dg_ref = refs[-1]
        dh = None
        for a_ref, w_ref in zip(a_refs, w_refs):
            d = lax.dot_general(a_ref[...].astype(BF16), w_ref[...], (((1,), (1,)), ((), ())), preferred_element_type=F32)
            dh = d if dh is None else dh + d
        xv = x_ref[...]
        r = lax.rsqrt(jnp.mean(xv * xv, axis=-1, keepdims=True) + EPS)
        y = xv * r

        @pl.when(pl.program_id(0) == 0)
        def _():
            dg_ref[...] = jnp.zeros_like(dg_ref)

        dg_ref[...] += jnp.sum(dh * y, axis=0, keepdims=True)
        dy = dh * g_ref[...]
        dx = r * (dy - y * jnp.mean(dy * y, axis=-1, keepdims=True))
        if has_res:
            dx = res_ref[...] + dx
        for o in outs:
            o[...] = dx.astype(o.dtype)

    in_specs = [pl.BlockSpec((tm, a.shape[1]), lambda i: (i, 0)) for a, _ in pairs]
    in_specs += [pl.BlockSpec((dn, w.shape[1]), lambda i: (0, 0)) for _, w in pairs]
    in_specs += [pl.BlockSpec((tm, dn), lambda i: (i, xcol)), pl.BlockSpec((1, dn), lambda i: (0, 0))]
    args = [a for a, _ in pairs] + [w for _, w in pairs] + [x, g]
    if has_res:
        in_specs.append(pl.BlockSpec((tm, dn), lambda i: (i, 0)))
        args.append(d_res)
    outs = pl.pallas_call(
        body, grid=(s // tm,), in_specs=in_specs,
        out_specs=[pl.BlockSpec((tm, dn), lambda i: (i, 0)) for _ in out_dtypes] + [pl.BlockSpec((1, dn), lambda i: (0, 0))],
        out_shape=[SDS((s, dn), dt) for dt in out_dtypes] + [SDS((1, dn), F32)],
        compiler_params=_cp(1), name=name)(*args)
    return outs


def _dw_matmul(a, b, tk, tn, ts, name):
    s, ka = a.shape
    n = b.shape[1]

    def body(a_ref, b_ref, o_ref):
        @pl.when(pl.program_id(2) == 0)
        def _():
            o_ref[...] = jnp.zeros_like(o_ref)

        o_ref[...] += lax.dot_general(a_ref[...].astype(BF16), b_ref[...].astype(BF16), (((0,), (0,)), ((), ())),
                                      preferred_element_type=F32)

    return pl.pallas_call(
        body, grid=(ka // tk, n // tn, s // ts),
        in_specs=[pl.BlockSpec((ts, tk), lambda k, j, t: (t, k)), pl.BlockSpec((ts, tn), lambda k, j, t: (t, j))],
        out_specs=pl.BlockSpec((tk, tn), lambda k, j, t: (k, j)),
        out_shape=SDS((ka, n), F32), compiler_params=_cp(3), name=name)(a, b)


def _dw(a, b, name):
    s, ka = a.shape
    n = b.shape[1]
    tk = ka if ka <= 1024 else ka // 2
    tn = n if n <= 1024 else (n // 2 if n == D_FF else 512)
    return _dw_matmul(a, b, tk, tn, _row_tile(s, 2048), name)


def _prev_halo(tm, halo):
    return lambda i: (jnp.maximum(i * (tm // halo) - 1, 0), 0)


def _next_halo(tm, halo, s):
    return lambda i: (jnp.minimum((i + 1) * (tm // halo), s // halo - 1), 0)


def _shifted_copies(ext, tm):
    n = tm + CONV_HALO - 8
    for s in range(1, 8):
        ext[s, 0:n, :] = ext[0, s:s + n, :]


def _tap_rows(ext, o, n, cs):
    return ext[o % 8, o - o % 8:o - o % 8 + n, cs]


def _conv_fwd(z, b_in, w32, b_dw, ln_g, ln_b, tm):
    s = z.shape[0]
    c = CONV_CH

    def body(z_ref, zh_ref, bin_ref, w_ref, bdw_ref, lg_ref, lb_ref, u_ref, u0_ref, u1_ref, ext):
        i = pl.program_id(0)

        def glu(zz):
            zz = zz + bin_ref[...]
            return zz[:, :c] * jax.nn.sigmoid(zz[:, c:])

        u0 = glu(z_ref[...])
        u0_ref[...] = u0
        ext[0, 0:CONV_HALO, :] = jnp.where(i > 0, glu(zh_ref[...]), 0.0)
        ext[0, CONV_HALO:, :] = u0
        _shifted_copies(ext, tm)
        off = CONV_HALO - (CONV_WIDTH - 1)
        for r in range(tm // 64):
            for cb in range(c // 128):
                cs = slice(cb * 128, (cb + 1) * 128)
                acc = jnp.zeros((64, 128), F32)
                for k in range(CONV_WIDTH):
                    acc = acc + _tap_rows(ext, r * 64 + off + k, 64, cs) * w_ref[k:k + 1, cs]
                u1_ref[r * 64:(r + 1) * 64, cs] = acc + bdw_ref[:, cs]
        u1 = u1_ref[...]
        mu = jnp.mean(u1, axis=-1, keepdims=True)
        xc = u1 - mu
        y = xc * lax.rsqrt(jnp.mean(xc * xc, axis=-1, keepdims=True) + EPS)
        y = y * lg_ref[...] + lb_ref[...]
        u_ref[...] = (y * jax.nn.sigmoid(y)).astype(BF16)

    row = lambda i: (i, 0)
    fix = lambda i: (0, 0)
    return pl.pallas_call(
        body, grid=(s // tm,),
        in_specs=[pl.BlockSpec((tm, 2 * c), row), pl.BlockSpec((CONV_HALO, 2 * c), _prev_halo(tm, CONV_HALO)),
                  pl.BlockSpec((1, 2 * c), fix), pl.BlockSpec((32, c), fix), pl.BlockSpec((1, c), fix),
                  pl.BlockSpec((1, c), fix), pl.BlockSpec((1, c), fix)],
        out_specs=[pl.BlockSpec((tm, c), row)] * 3,
        out_shape=[SDS((s, c), BF16), SDS((s, c), F32), SDS((s, c), F32)],
        scratch_shapes=[pltpu.VMEM((8, tm + CONV_HALO, c), F32)],
        compiler_params=_cp(1), name="conv_fwd")(z, z, b_in, w32, b_dw, ln_g, ln_b)


def _conv_bwd_ln(d_u, u1, ln_g, ln_b, tm):
    s = d_u.shape[0]
    c = CONV_CH

    def body(du_ref, u1_ref, lg_ref, lb_ref, du1_ref, dlg_ref, dlb_ref, dbdw_ref):
        @pl.when(pl.program_id(0) == 0)
        def _():
            dlg_ref[...] = jnp.zeros_like(dlg_ref)
            dlb_ref[...] = jnp.zeros_like(dlb_ref)
            dbdw_ref[...] = jnp.zeros_like(dbdw_ref)

        u1 = u1_ref[...]
        mu = jnp.mean(u1, axis=-1, keepdims=True)
        xc = u1 - mu
        rs = lax.rsqrt(jnp.mean(xc * xc, axis=-1, keepdims=True) + EPS)
        xh = xc * rs
        y = xh * lg_ref[...] + lb_ref[...]
        sg = jax.nn.sigmoid(y)
        dy = du_ref[...] * (sg * (1.0 + y * (1.0 - sg)))
        dlg_ref[...] += jnp.sum(dy * xh, axis=0, keepdims=True)
        dlb_ref[...] += jnp.sum(dy, axis=0, keepdims=True)
        dxh = dy * lg_ref[...]
        du1 = rs * (dxh - jnp.mean(dxh, axis=-1, keepdims=True) - xh * jnp.mean(dxh * xh, axis=-1, keepdims=True))
        dbdw_ref[...] += jnp.sum(du1, axis=0, keepdims=True)
        du1_ref[...] = du1

    row = lambda i: (i, 0)
    fix = lambda i: (0, 0)
    return pl.pallas_call(
        body, grid=(s // tm,),
        in_specs=[pl.BlockSpec((tm, c), row), pl.BlockSpec((tm, c), row), pl.BlockSpec((1, c), fix), pl.BlockSpec((1, c), fix)],
        out_specs=[pl.BlockSpec((tm, c), row)] + [pl.BlockSpec((1, c), fix)] * 3,
        out_shape=[SDS((s, c), F32)] + [SDS((1, c), F32)] * 3,
        compiler_params=_cp(1), name="conv_bwd_ln")(d_u, u1, ln_g, ln_b)


def _conv_bwd_dw(d_u1, u0, z, b_in, w32, tm):
    s = d_u1.shape[0]
    c = CONV_CH

    def body(d_ref, dn_ref, u0_ref, u0p_ref, z_ref, bin_ref, w_ref, dz_ref, dw_ref, dbin_ref, extd, extu, du0):
        i = pl.program_id(0)
        last = pl.num_programs(0) - 1

        @pl.when(i == 0)
        def _():
            dw_ref[...] = jnp.zeros_like(dw_ref)
            dbin_ref[...] = jnp.zeros_like(dbin_ref)

        extd[0, 0:tm, :] = d_ref[...]
        extd[0, tm:, :] = jnp.where(i < last, dn_ref[...], 0.0)
        extu[0, 0:CONV_HALO, :] = jnp.where(i > 0, u0p_ref[...], 0.0)
        extu[0, CONV_HALO:, :] = u0_ref[...]
        _shifted_copies(extd, tm)
        _shifted_copies(extu, tm)
        off = CONV_HALO - (CONV_WIDTH - 1)
        for r in range(tm // 64):
            for cb in range(c // 128):
                cs = slice(cb * 128, (cb + 1) * 128)
                acc = jnp.zeros((64, 128), F32)
                for k in range(CONV_WIDTH):
                    acc = acc + _tap_rows(extd, r * 64 + (CONV_WIDTH - 1) - k, 64, cs) * w_ref[k:k + 1, cs]
                du0[r * 64:(r + 1) * 64, cs] = acc
        for cb in range(c // 128):
            cs = slice(cb * 128, (cb + 1) * 128)
            for k in range(CONV_WIDTH):
                part = jnp.zeros((8, 128), F32)
                for r in range(tm // 64):
                    p = d_ref[r * 64:(r + 1) * 64, cs] * _tap_rows(extu, r * 64 + off + k, 64, cs)
                    for q in range(8):
                        part = part + p[q * 8:(q + 1) * 8, :]
                dw_ref[k:k + 1, cs] += jnp.sum(part, axis=0, keepdims=True)
        zz = z_ref[...] + bin_ref[...]
        a = zz[:, :c]
        sg = jax.nn.sigmoid(zz[:, c:])
        d0 = du0[...]
        da = d0 * sg
        dgt = d0 * a * (sg * (1.0 - sg))
        dbin_ref[:, :c] += jnp.sum(da, axis=0, keepdims=True)
        dbin_ref[:, c:] += jnp.sum(dgt, axis=0, keepdims=True)
        dz_ref[:, :c] = da.astype(BF16)
        dz_ref[:, c:] = dgt.astype(BF16)

    row = lambda i: (i, 0)
    fix = lambda i: (0, 0)
    return pl.pallas_call(
        body, grid=(s // tm,),
        in_specs=[pl.BlockSpec((tm, c), row), pl.BlockSpec((CONV_HALO, c), _next_halo(tm, CONV_HALO, s)),
                  pl.BlockSpec((tm, c), row), pl.BlockSpec((CONV_HALO, c), _prev_halo(tm, CONV_HALO)),
                  pl.BlockSpec((tm, 2 * c), row), pl.BlockSpec((1, 2 * c), fix), pl.BlockSpec((32, c), fix)],
        out_specs=[pl.BlockSpec((tm, 2 * c), row), pl.BlockSpec((32, c), fix), pl.BlockSpec((1, 2 * c), fix)],
        out_shape=[SDS((s, 2 * c), BF16), SDS((32, c), F32), SDS((1, 2 * c), F32)],
        scratch_shapes=[pltpu.VMEM((8, tm + CONV_HALO, c), F32), pltpu.VMEM((8, tm + CONV_HALO, c), F32), pltpu.VMEM((tm, c), F32)],
        compiler_params=_cp(1), name="conv_bwd_dw")(d_u1, d_u1, u0, u0, z, b_in, w32)


def _partner(v, lane):
    up = pltpu.roll(v, HEAD_PAD - ROPE // 2, 1)
    dn = pltpu.roll(v, ROPE // 2, 1)
    lo = (lane >= NOPE) & (lane < NOPE + ROPE // 2)
    hi = (lane >= NOPE + ROPE // 2) & (lane < HEAD_DIM)
    return jnp.where(lo, up, jnp.where(hi, dn, 0.0))


def _mla_prep(q_raw, kv_raw, z, cosf, sinf, gq, gk, tm):
    s = q_raw.shape[0]

    def body(q_ref, kv_ref, kr_ref, c_ref, s_ref, gq_ref, gk_ref, qo_ref, ko_ref, vo_ref):
        lane = lax.broadcasted_iota(jnp.int32, (tm, HEAD_PAD), 1)
        cf = c_ref[...]
        sf = s_ref[...]

        def norm_rope(t, g_ref):
            r = lax.rsqrt(jnp.sum(t * t, axis=-1, keepdims=True) * (1.0 / HEAD_DIM) + EPS)
            tn = (t * r) * g_ref[...]
            return tn * cf + _partner(tn, lane) * sf

        kr = kr_ref[...]
        for h in range(HEADS):
            hs = slice(h * HEAD_PAD, (h + 1) * HEAD_PAD)
            qo_ref[:, hs] = (norm_rope(q_ref[:, hs], gq_ref) * (ATT_SCALE * LOG2E)).astype(BF16)
            kv = kv_ref[:, hs]
            ko_ref[:, hs] = norm_rope(jnp.where(lane < NOPE, kv, 0.0) + kr, gk_ref).astype(BF16)
            vo_ref[:, hs] = jnp.where(lane >= NOPE, kv, 0.0).astype(BF16)

    row = lambda i: (i, 0)
    wide = pl.BlockSpec((tm, HEADS * HEAD_PAD), row)
    one = pl.BlockSpec((tm, HEAD_PAD), row)
    return pl.pallas_call(
        body, grid=(s // tm,),
        in_specs=[wide, wide, pl.BlockSpec((tm, HEAD_PAD), lambda i: (i, IN_COLS_PAD // HEAD_PAD - 1)), one, one,
                  pl.BlockSpec((1, HEAD_PAD), lambda i: (0, 0)), pl.BlockSpec((1, HEAD_PAD), lambda i: (0, 0))],
        out_specs=[wide] * 3,
        out_shape=[SDS((s, HEADS * HEAD_PAD), BF16)] * 3,
        compiler_params=_cp(1), name="mla_prep")(q_raw, kv_raw, z, cosf, sinf, gq, gk)


def _mla_prep_bwd(dqp, dkp, dvp, q_raw, kv_raw, z, cosf, sinf, gq, gk, tm):
    s = q_raw.shape[0]

    def body(dq_ref, dk_ref, dv_ref, q_ref, kv_ref, kr_ref, c_ref, s_ref, gq_ref, gk_ref,
             dqo_ref, dkvo_ref, dkr_ref, dgq_ref, dgk_ref):
        lane = lax.broadcasted_iota(jnp.int32, (tm, HEAD_PAD), 1)
        cf = c_ref[...]
        sf = s_ref[...]

        @pl.when(pl.program_id(0) == 0)
        def _():
            dgq_ref[...] = jnp.zeros_like(dgq_ref)
            dgk_ref[...] = jnp.zeros_like(dgk_ref)

        def norm_rope_bwd(t, d_out, g_ref, dg_ref):
            r = lax.rsqrt(jnp.sum(t * t, axis=-1, keepdims=True) * (1.0 / HEAD_DIM) + EPS)
            th = t * r
            dn = d_out * cf + _partner(d_out * sf, lane)
            dg_ref[...] += jnp.sum(dn * th, axis=0, keepdims=True)
            dh = dn * g_ref[...]
            return r * (dh - th * (jnp.sum(dh * th, axis=-1, keepdims=True) * (1.0 / HEAD_DIM)))

        kr = kr_ref[...]
        dkr = None
        for h in range(HEADS):
            hs = slice(h * HEAD_PAD, (h + 1) * HEAD_PAD)
            dq = norm_rope_bwd(q_ref[:, hs], dq_ref[:, hs] * ATT_SCALE, gq_ref, dgq_ref)
            dqo_ref[:, hs] = dq.astype(BF16)
            kv = kv_ref[:, hs]
            dkpre = norm_rope_bwd(jnp.where(lane < NOPE, kv, 0.0) + kr, dk_ref[:, hs] * LN2, gk_ref, dgk_ref)
            dkvo_ref[:, hs] = jnp.where(lane < NOPE, dkpre, dv_ref[:, hs]).astype(BF16)
            dkr_h = jnp.where((lane >= NOPE) & (lane < HEAD_DIM), dkpre, 0.0)
            dkr = dkr_h if dkr is None else dkr + dkr_h
        dkr_ref[...] = dkr

    row = lambda i: (i, 0)
    fix = lambda i: (0, 0)
    wide = pl.BlockSpec((tm, HEADS * HEAD_PAD), row)
    one = pl.BlockSpec((tm, HEAD_PAD), row)
    return pl.pallas_call(
        body, grid=(s // tm,),
        in_specs=[wide, wide, wide, wide, wide, pl.BlockSpec((tm, HEAD_PAD), lambda i: (i, IN_COLS_PAD // HEAD_PAD - 1)),
                  one, one, pl.BlockSpec((1, HEAD_PAD), fix), pl.BlockSpec((1, HEAD_PAD), fix)],
        out_specs=[wide, wide, one, pl.BlockSpec((1, HEAD_PAD), fix), pl.BlockSpec((1, HEAD_PAD), fix)],
        out_shape=[SDS((s, HEADS * HEAD_PAD), BF16), SDS((s, HEADS * HEAD_PAD), BF16), SDS((s, HEAD_PAD), F32),
                   SDS((1, HEAD_PAD), F32), SDS((1, HEAD_PAD), F32)],
        compiler_params=_cp(1), name="mla_prep_bwd")(dqp, dkp, dvp, q_raw, kv_raw, z, cosf, sinf, gq, gk)


def _pair_schedule(nb, forward):
    one, two, case = [], [], []
    for a in range(nb):
        for b in (range(a // 2 + 1) if forward else range(a // 2, nb // 2)):
            one.append(a)
            two.append(b)
            case.append(0 if b != a // 2 else 1 + a % 2)
    return tuple(jnp.asarray(np.array(x, np.int32)) for x in (one, two, case))


STRIP = 64


def _fold8(x):
    acc = x[0:8, :]
    for g in range(1, x.shape[0] // 8):
        acc = acc + x[g * 8:(g + 1) * 8, :]
    return acc


def _attn_fwd(qp, kp, vp, tb, w_late):
    s = qp.shape[0]
    ii, jj, cc = _pair_schedule(s // tb, True)
    n_steps = int(ii.shape[0])

    def body(ii_ref, jj_ref, cc_ref, q_ref, k_ref, v_ref, wl_ref, of_ref, ob_ref, lse_ref, gl_ref,
             m_sc, l_sc, acc_sc, st_sc, pt_sc, send_sems, recv_sems):
        t = pl.program_id(1)
        case = cc_ref[t]

        @pl.when((pl.program_id(0) == 0) & (t == 0))
        def _():
            _ag_send(wl_ref, gl_ref, send_sems, recv_sems)

        @pl.when(jj_ref[t] == 0)
        def _():
            m_sc[...] = jnp.full_like(m_sc, -jnp.inf)
            l_sc[...] = jnp.zeros_like(l_sc)
            acc_sc[...] = jnp.zeros_like(acc_sc)

        def step(n_keys, diag_at):
            def visible(r):
                if diag_at is None or r * STRIP <= diag_at:
                    return None
                col = lax.broadcasted_iota(jnp.int32, (STRIP, tb), 1)
                return col >= r * STRIP - diag_at

            st_sc[0:n_keys, :] = lax.dot_general(k_ref[0:n_keys, :], q_ref[...], (((1,), (1,)), ((), ())), preferred_element_type=F32)
            mx = None
            for r in range(n_keys // STRIP):
                sc = st_sc[r * STRIP:(r + 1) * STRIP, :]
                if visible(r) is not None:
                    sc = jnp.where(visible(r), sc, -jnp.inf)
                m8 = sc[0:8, :]
                for g in range(1, STRIP // 8):
                    m8 = jnp.maximum(m8, sc[g * 8:(g + 1) * 8, :])
                mx = m8 if mx is None else jnp.maximum(mx, m8)
            m_old = m_sc[0:1, :]
            m_new = jnp.maximum(m_old, jnp.max(mx, axis=0, keepdims=True))
            alpha = jnp.exp2(m_old - m_new)
            ps = None
            for r in range(n_keys // STRIP):
                p = jnp.exp2(st_sc[r * STRIP:(r + 1) * STRIP, :] - m_new)
                if visible(r) is not None:
                    p = jnp.where(visible(r), p, 0.0)
                ps = _fold8(p) if ps is None else ps + _fold8(p)
                pt_sc[r * STRIP:(r + 1) * STRIP, :] = p.astype(BF16)
            l_new = alpha * l_sc[0:1, :] + jnp.sum(ps, axis=0, keepdims=True)
            m_sc[...] = jnp.broadcast_to(m_new, m_sc.shape)
            l_sc[...] = jnp.broadcast_to(l_new, l_sc.shape)
            pv = lax.dot_general(v_ref[0:n_keys, :], pt_sc[0:n_keys, :], (((0,), (0,)), ((), ())), preferred_element_type=F32)
            acc_sc[...] = alpha * acc_sc[...] + pv

        @pl.when(case == 0)
        def _():
            step(2 * tb, None)

        @pl.when(case == 1)
        def _():
            step(tb, 0)

        @pl.when(case == 2)
        def _():
            step(2 * tb, tb)

        @pl.when(case != 0)
        def _():
            l = l_sc[0:1, :]
            o = (acc_sc[...] / l).T
            of_ref[...] = o
            ob_ref[...] = o.astype(BF16)
            lse_ref[...] = m_sc[0:1, :] + jnp.log(l) * LOG2E

        @pl.when((pl.program_id(0) == HEADS - 1) & (t == n_steps - 1))
        def _():
            _ag_finish(wl_ref, gl_ref, send_sems, recv_sems)

    qmap = lambda h, t, ii_ref, jj_ref, cc_ref: (ii_ref[t], h)
    kmap = lambda h, t, ii_ref, jj_ref, cc_ref: (jj_ref[t], h)
    gs = pltpu.PrefetchScalarGridSpec(
        num_scalar_prefetch=3, grid=(HEADS, n_steps),
        in_specs=[pl.BlockSpec((tb, HEAD_PAD), qmap), pl.BlockSpec((2 * tb, HEAD_PAD), kmap), pl.BlockSpec((2 * tb, HEAD_PAD), kmap), ANY],
        out_specs=[pl.BlockSpec((tb, HEAD_PAD), qmap), pl.BlockSpec((tb, HEAD_PAD), qmap),
                   pl.BlockSpec((None, 1, tb), lambda h, t, ii_ref, jj_ref, cc_ref: (h, 0, ii_ref[t])), ANY],
        scratch_shapes=[pltpu.VMEM((8, tb), F32), pltpu.VMEM((8, tb), F32), pltpu.VMEM((HEAD_PAD, tb), F32),
                        pltpu.VMEM((2 * tb, tb), F32), pltpu.VMEM((2 * tb, tb), BF16),
                        pltpu.SemaphoreType.DMA((6,)), pltpu.SemaphoreType.DMA((6,))])
    w = HEADS * HEAD_PAD
    return pl.pallas_call(
        body, grid_spec=gs,
        out_shape=[SDS((s, w), F32), SDS((s, w), BF16), SDS((HEADS, 1, s), F32), SDS((4,) + w_late.shape, w_late.dtype)],
        compiler_params=_cp(2), name="attn_fwd")(ii, jj, cc, qp, kp, vp, w_late)


def _attn_delta(do, o, tb):
    s = do.shape[0]

    def body(do_ref, o_ref, d_ref):
        for h in range(HEADS):
            hs = slice(h * HEAD_PAD, (h + 1) * HEAD_PAD)
            d_ref[h] = jnp.sum((do_ref[:, hs] * o_ref[:, hs]).T, axis=0, keepdims=True)

    blk = pl.BlockSpec((tb, HEADS * HEAD_PAD), lambda i: (i, 0))
    return pl.pallas_call(body, grid=(s // tb,), in_specs=[blk, blk],
                          out_specs=pl.BlockSpec((HEADS, 1, tb), lambda i: (0, 0, i)),
                          out_shape=SDS((HEADS, 1, s), F32), compiler_params=_cp(1), name="attn_delta")(do, o)


def _attn_bwd(qp, kp, vp, dob, lse, delta, tb, wire):
    s = qp.shape[0]
    nb = s // tb
    jj, ii, cc = _pair_schedule(nb, False)
    rows = 32

    n_steps = int(ii.shape[0])

    def body(jj_ref, ii_ref, cc_ref, q_ref, k_ref, v_ref, do_ref, lse_ref, dl_ref, cw_ref, dq_ref, dk_ref, dv_ref, rcv_ref,
             st_sc, dpt_sc, pt_sc, dst_sc, send_sems, recv_sems):
        t = pl.program_id(1)
        case = cc_ref[t]
        pair = ii_ref[t]

        @pl.when((pl.program_id(0) == 0) & (t == 0))
        def _():
            _rs_send(cw_ref, rcv_ref, send_sems, recv_sems)

        @pl.when((pl.program_id(0) == HEADS - 1) & (t == n_steps - 1))
        def _():
            _rs_wait(rcv_ref, send_sems, recv_sems)

        @pl.when(t == 0)
        def _():
            dq_ref[...] = jnp.zeros_like(dq_ref)

        @pl.when(case != 0)
        def _():
            dk_ref[...] = jnp.zeros_like(dk_ref)
            dv_ref[...] = jnp.zeros_like(dv_ref)

        def step(lo, width, diag):
            q = q_ref[lo:lo + width, :]
            do = do_ref[lo:lo + width, :]
            k = k_ref[...]
            st_sc[:, 0:width] = lax.dot_general(k, q, (((1,), (1,)), ((), ())), preferred_element_type=F32)
            dpt_sc[:, 0:width] = lax.dot_general(v_ref[...], do, (((1,), (1,)), ((), ())), preferred_element_type=F32)
            lse_row = lse_ref[:, lo:lo + width]
            dl_row = dl_ref[:, lo:lo + width]
            for r in range(tb // rows):
                rs = slice(r * rows, (r + 1) * rows)
                p = jnp.exp2(st_sc[rs, 0:width] - lse_row)
                first_visible = (r * rows) // CHUNK * CHUNK
                if diag and first_visible > 0:
                    col = lax.broadcasted_iota(jnp.int32, (rows, width), 1)
                    p = jnp.where(col >= first_visible, p, 0.0)
                ds = p * (dpt_sc[rs, 0:width] - dl_row)
                pt_sc[rs, 0:width] = p.astype(BF16)
                dst_sc[rs, 0:width] = ds.astype(BF16)
            dv_ref[...] += jnp.dot(pt_sc[:, 0:width], do, preferred_element_type=F32)
            dst = dst_sc[:, 0:width]
            dk_ref[...] += jnp.dot(dst, q, preferred_element_type=F32)
            dqt = lax.dot_general(k, dst, (((0,), (0,)), ((), ())), preferred_element_type=F32)
            for h in range(width // tb):
                dq_ref[2 * pair + lo // tb + h] += dqt[:, h * tb:(h + 1) * tb]

        @pl.when(case == 0)
        def _():
            step(0, 2 * tb, False)

        @pl.when(case == 1)
        def _():
            step(0, 2 * tb, True)

        @pl.when(case == 2)
        def _():
            step(tb, tb, True)

    qmap = lambda h, t, jj_ref, ii_ref, cc_ref: (ii_ref[t], h)
    kmap = lambda h, t, jj_ref, ii_ref, cc_ref: (jj_ref[t], h)
    rowmap = lambda h, t, jj_ref, ii_ref, cc_ref: (h, 0, ii_ref[t])
    gs = pltpu.PrefetchScalarGridSpec(
        num_scalar_prefetch=3, grid=(HEADS, n_steps),
        in_specs=[pl.BlockSpec((2 * tb, HEAD_PAD), qmap), pl.BlockSpec((tb, HEAD_PAD), kmap), pl.BlockSpec((tb, HEAD_PAD), kmap),
                  pl.BlockSpec((2 * tb, HEAD_PAD), qmap), pl.BlockSpec((None, 1, 2 * tb), rowmap), pl.BlockSpec((None, 1, 2 * tb), rowmap), ANY],
        out_specs=[pl.BlockSpec((None, nb, HEAD_PAD, tb), lambda h, t, jj_ref, ii_ref, cc_ref: (h, 0, 0, 0)),
                   pl.BlockSpec((tb, HEAD_PAD), kmap), pl.BlockSpec((tb, HEAD_PAD), kmap), ANY],
        scratch_shapes=[pltpu.VMEM((tb, 2 * tb), F32), pltpu.VMEM((tb, 2 * tb), F32), pltpu.VMEM((tb, 2 * tb), BF16),
                        pltpu.VMEM((tb, 2 * tb), BF16), pltpu.SemaphoreType.DMA((3,)), pltpu.SemaphoreType.DMA((3,))])
    w = HEADS * HEAD_PAD
    dqt, dk, dv, recv = pl.pallas_call(
        body, grid_spec=gs,
        out_shape=[SDS((HEADS, nb, HEAD_PAD, tb), F32), SDS((s, w), F32), SDS((s, w), F32), SDS((3,) + wire.shape[1:], wire.dtype)],
        compiler_params=_cp(2), name="attn_bwd")(jj, ii, cc, qp, kp, vp, dob, lse, delta, wire)
    return jnp.transpose(dqt, (1, 3, 0, 2)).reshape(s, w), dk, dv, recv


def _head_norm(t, g):
    r = lax.rsqrt(jnp.mean(t * t, axis=-1, keepdims=True) + EPS)
    th = t * r
    return r, th, th * g


def _softmax_rows(sc):
    m = jnp.max(sc, axis=-1, keepdims=True)
    e = jnp.exp(sc - m)
    return e / jnp.sum(e, axis=-1, keepdims=True)


def _memattn_fwd(qm, kvm, gq, gk, tm):
    s = qm.shape[0]
    hd = MEM_HEAD_DIM

    def body(q_ref, k_ref, v_ref, gq_ref, gk_ref, o_ref):
        _, _, qn = _head_norm(q_ref[...], gq_ref[...])
        _, _, kn = _head_norm(k_ref[...], gk_ref[...])
        sc = lax.dot_general(qn.astype(BF16), kn.astype(BF16), (((1,), (1,)), ((), ())), preferred_element_type=F32)
        p = _softmax_rows(sc * (1.0 / math.sqrt(hd)))
        o_ref[...] = jnp.dot(p.astype(BF16), v_ref[...].astype(BF16), preferred_element_type=F32).astype(BF16)

    fix = lambda i, h: (0, 0)
    return pl.pallas_call(
        body, grid=(s // tm, MEM_HEADS),
        in_specs=[pl.BlockSpec((tm, hd), lambda i, h: (i, h)), pl.BlockSpec((MEM_LEN, hd), lambda i, h: (0, h)),
                  pl.BlockSpec((MEM_LEN, hd), lambda i, h: (0, MEM_HEADS + h)), pl.BlockSpec((1, hd), fix), pl.BlockSpec((1, hd), fix)],
        out_specs=pl.BlockSpec((tm, hd), lambda i, h: (i, h)),
        out_shape=SDS((s, MEM_HEADS * hd), BF16), compiler_params=_cp(2), name="memattn_fwd")(qm, kvm, kvm, gq, gk)


def _memattn_bwd(qm, kvm, d_o, gq, gk, tm):
    s = qm.shape[0]
    hd = MEM_HEAD_DIM

    def body(q_ref, k_ref, v_ref, do_ref, gq_ref, gk_ref, dq_ref, dk_ref, dv_ref, dgq_ref, dgk_ref, dkn_sc):
        h = pl.program_id(0)
        i = pl.program_id(1)
        last = pl.num_programs(1) - 1

        @pl.when((h == 0) & (i == 0))
        def _():
            dgq_ref[...] = jnp.zeros_like(dgq_ref)
            dgk_ref[...] = jnp.zeros_like(dgk_ref)

        @pl.when(i == 0)
        def _():
            dv_ref[...] = jnp.zeros_like(dv_ref)
            dkn_sc[...] = jnp.zeros_like(dkn_sc)

        rq, qh, qn = _head_norm(q_ref[...], gq_ref[...])
        rk, kh, kn = _head_norm(k_ref[...], gk_ref[...])
        qnb = qn.astype(BF16)
        knb = kn.astype(BF16)
        scale = 1.0 / math.sqrt(hd)
        sc = lax.dot_general(qnb, knb, (((1,), (1,)), ((), ())), preferred_element_type=F32)
        p = _softmax_rows(sc * scale)
        do = do_ref[...].astype(BF16)
        dp = lax.dot_general(do, v_ref[...].astype(BF16), (((1,), (1,)), ((), ())), preferred_element_type=F32)
        dv_ref[...] += lax.dot_general(p.astype(BF16), do, (((0,), (0,)), ((), ())), preferred_element_type=F32)
        ds = ((p * (dp - jnp.sum(dp * p, axis=-1, keepdims=True))) * scale).astype(BF16)
        dqn = jnp.dot(ds, knb, preferred_element_type=F32)
        dkn_sc[...] += lax.dot_general(ds, qnb, (((0,), (0,)), ((), ())), preferred_element_type=F32)
        dgq_ref[...] += jnp.sum(dqn * qh, axis=0, keepdims=True)
        dqh = dqn * gq_ref[...]
        dq_ref[...] = (rq * (dqh - qh * jnp.mean(dqh * qh, axis=-1, keepdims=True))).astype(BF16)

        @pl.when(i == last)
        def _():
            dkn = dkn_sc[...]
            dgk_ref[...] += jnp.sum(dkn * kh, axis=0, keepdims=True)
            dkh = dkn * gk_ref[...]
            dk_ref[...] = rk * (dkh - kh * jnp.mean(dkh * kh, axis=-1, keepdims=True))

    fix = lambda h, i: (0, 0)
    qb = pl.BlockSpec((tm, hd), lambda h, i: (i, h))
    kb = pl.BlockSpec((MEM_LEN, hd), lambda h, i: (0, h))
    return pl.pallas_call(
        body, grid=(MEM_HEADS, s // tm),
        in_specs=[qb, kb, pl.BlockSpec((MEM_LEN, hd), lambda h, i: (0, MEM_HEADS + h)), qb,
                  pl.BlockSpec((1, hd), fix), pl.BlockSpec((1, hd), fix)],
        out_specs=[qb, kb, kb, pl.BlockSpec((1, hd), fix), pl.BlockSpec((1, hd), fix)],
        out_shape=[SDS((s, MEM_HEADS * hd), BF16), SDS((MEM_LEN, MEM_HEADS * hd), F32), SDS((MEM_LEN, MEM_HEADS * hd), F32),
                   SDS((1, hd), F32), SDS((1, hd), F32)],
        scratch_shapes=[pltpu.VMEM((MEM_LEN, hd), F32)],
        compiler_params=_cp(2), name="memattn_bwd")(qm, kvm, kvm, d_o, gq, gk)


def _ffn_specs(tm, tn, nbj, s, order_ji):
    if order_ji:
        ij = lambda f: (lambda j, i: f(i, j))
    else:
        ij = lambda f: f
    prev = lambda i: jnp.maximum(i * (tm // FFN_HALO) - 1, 0)
    cur_g = pl.BlockSpec((tm, tn), ij(lambda i, j: (i, j)))
    cur_v = pl.BlockSpec((tm, tn), ij(lambda i, j: (i, j + nbj)))
    halo_g = pl.BlockSpec((FFN_HALO, tn), ij(lambda i, j: (prev(i), j)))
    halo_v = pl.BlockSpec((FFN_HALO, tn), ij(lambda i, j: (prev(i), j + nbj)))
    w_g = pl.BlockSpec((8, tn), ij(lambda i, j: (0, j)))
    w_v = pl.BlockSpec((8, tn), ij(lambda i, j: (0, j + nbj)))
    b_g = pl.BlockSpec((1, tn), ij(lambda i, j: (0, j)))
    b_v = pl.BlockSpec((1, tn), ij(lambda i, j: (0, j + nbj)))
    return cur_g, cur_v, halo_g, halo_v, w_g, w_v, b_g, b_v


FFN_STRIP = 16


def _conv3_rows(ext, w_ref, b_ref, o, n):
    return (w_ref[0:1, :] * ext[FFN_HALO - 2 + o:FFN_HALO - 2 + o + n, :] + w_ref[1:2, :] * ext[FFN_HALO - 1 + o:FFN_HALO - 1 + o + n, :]
            + w_ref[2:3, :] * ext[FFN_HALO + o:FFN_HALO + o + n, :] + b_ref[...])


def _ffn_fwd(up0, w8, b, tm, tn):
    s = up0.shape[0]
    nbj = D_FF // tn

    def body(g_ref, v_ref, gh_ref, vh_ref, wg_ref, wv_ref, bg_ref, bv_ref, act_ref, extg, extv):
        first = pl.program_id(0) == 0
        for ext, h_ref, c_ref in ((extg, gh_ref, g_ref), (extv, vh_ref, v_ref)):
            ext[0:FFN_HALO, :] = jnp.where(first, 0.0, h_ref[...])
            ext[FFN_HALO:, :] = c_ref[...]
        for r in range(tm // FFN_STRIP):
            o = r * FFN_STRIP
            ug = _conv3_rows(extg, wg_ref, bg_ref, o, FFN_STRIP)
            uv = _conv3_rows(extv, wv_ref, bv_ref, o, FFN_STRIP)
            act_ref[o:o + FFN_STRIP, :] = ((ug * jax.nn.sigmoid(ug)) * uv).astype(BF16)

    specs = _ffn_specs(tm, tn, nbj, s, False)
    return pl.pallas_call(
        body, grid=(s // tm, nbj), in_specs=list(specs),
        out_specs=pl.BlockSpec((tm, tn), lambda i, j: (i, j)), out_shape=SDS((s, D_FF), BF16),
        scratch_shapes=[pltpu.VMEM((tm + FFN_HALO, tn), F32), pltpu.VMEM((tm + FFN_HALO, tn), F32)],
        compiler_params=_cp(2), name="ffn_fwd")(up0, up0, up0, up0, w8, w8, b, b)


def _ffn_bwd(d_act, up0, w8, b, tm, tn):
    s = up0.shape[0]
    nbj = D_FF // tn
    te = tm + FFN_HALO

    def body(da_ref, dan_ref, g_ref, v_ref, gh_ref, vh_ref, gn_ref, vn_ref, wg_ref, wv_ref, bg_ref, bv_ref,
             og_ref, ov_ref, dbg_ref, dbv_ref, dwg_ref, dwv_ref, extg, extv, extdg, extdv, accg, accv):
        i = pl.program_id(1)
        first = i == 0
        last = i == pl.num_programs(1) - 1

        @pl.when(first)
        def _():
            for r in (dbg_ref, dbv_ref, dwg_ref, dwv_ref):
                r[...] = jnp.zeros_like(r)

        for ext, h_ref, c_ref, n_ref in ((extg, gh_ref, g_ref, gn_ref), (extv, vh_ref, v_ref, vn_ref)):
            ext[0:FFN_HALO, :] = jnp.where(first, 0.0, h_ref[...])
            ext[FFN_HALO:FFN_HALO + tm, :] = c_ref[...]
            ext[FFN_HALO + tm:, :] = n_ref[...]

        def fold8(x):
            acc = x[0:8, :]
            for q in range(1, x.shape[0] // 8):
                acc = acc + x[q * 8:(q + 1) * 8, :]
            return acc

        def taps(ext, o, n):
            return [ext[FFN_HALO - 2 + k + o:FFN_HALO - 2 + k + o + n, :] for k in range(3)]

        accg[...] = jnp.zeros_like(accg)
        accv[...] = jnp.zeros_like(accv)

        def gate_bwd(o, n, da, own_rows):
            xg, xv = taps(extg, o, n), taps(extv, o, n)
            ug = wg_ref[0:1, :] * xg[0] + wg_ref[1:2, :] * xg[1] + wg_ref[2:3, :] * xg[2] + bg_ref[...]
            uv = wv_ref[0:1, :] * xv[0] + wv_ref[1:2, :] * xv[1] + wv_ref[2:3, :] * xv[2] + bv_ref[...]
            sg = jax.nn.sigmoid(ug)
            dgt = da * uv * (sg * (1.0 + ug * (1.0 - sg)))
            dvl = da * (ug * sg)
            extdg[o:o + n, :] = dgt
            extdv[o:o + n, :] = dvl
            if own_rows:
                for acc, d, x in ((accg, dgt, xg), (accv, dvl, xv)):
                    acc[0] += fold8(d)
                    for k in range(3):
                        acc[1 + k] += fold8(d * x[k])

        for r in range(tm // FFN_STRIP):
            gate_bwd(r * FFN_STRIP, FFN_STRIP, da_ref[r * FFN_STRIP:(r + 1) * FFN_STRIP, :], True)
        gate_bwd(tm, FFN_HALO, jnp.where(last, 0.0, dan_ref[...]), False)

        for extd, w_ref, o_ref, db_ref, dw_ref, acc in ((extdg, wg_ref, og_ref, dbg_ref, dwg_ref, accg),
                                                        (extdv, wv_ref, ov_ref, dbv_ref, dwv_ref, accv)):
            for r in range(tm // FFN_STRIP):
                o = r * FFN_STRIP
                o_ref[o:o + FFN_STRIP, :] = (w_ref[2:3, :] * extd[o:o + FFN_STRIP, :] + w_ref[1:2, :] * extd[o + 1:o + 1 + FFN_STRIP, :]
                                             + w_ref[0:1, :] * extd[o + 2:o + 2 + FFN_STRIP, :]).astype(BF16)
            db_ref[...] += jnp.sum(acc[0], axis=0, keepdims=True)
            for k in range(3):
                dw_ref[k:k + 1, :] += jnp.sum(acc[1 + k], axis=0, keepdims=True)

    cur_g, cur_v, halo_g, halo_v, w_g, w_v, b_g, b_v = _ffn_specs(tm, tn, nbj, s, True)
    nxt_row = lambda i: jnp.minimum((i + 1) * (tm // FFN_HALO), s // FFN_HALO - 1)
    cur = pl.BlockSpec((tm, tn), lambda j, i: (i, j))
    nxt = pl.BlockSpec((FFN_HALO, tn), lambda j, i: (nxt_row(i), j))
    nxt_v = pl.BlockSpec((FFN_HALO, tn), lambda j, i: (nxt_row(i), j + nbj))
    acc1 = pl.BlockSpec((1, tn), lambda j, i: (0, j))
    acc8 = pl.BlockSpec((8, tn), lambda j, i: (0, j))
    return pl.pallas_call(
        body, grid=(nbj, s // tm), in_specs=[cur, nxt, cur_g, cur_v, halo_g, halo_v, nxt, nxt_v, w_g, w_v, b_g, b_v],
        out_specs=[cur, cur, acc1, acc1, acc8, acc8],
        out_shape=[SDS((s, D_FF), BF16), SDS((s, D_FF), BF16), SDS((1, D_FF), F32), SDS((1, D_FF), F32),
                   SDS((8, D_FF), F32), SDS((8, D_FF), F32)],
        scratch_shapes=[pltpu.VMEM((tm + 2 * FFN_HALO, tn), F32), pltpu.VMEM((tm + 2 * FFN_HALO, tn), F32),
                        pltpu.VMEM((te, tn), F32), pltpu.VMEM((te, tn), F32),
                        pltpu.VMEM((4, 8, tn), F32), pltpu.VMEM((4, 8, tn), F32)],
        compiler_params=_cp(2), name="ffn_bwd")(d_act, d_act, up0, up0, up0, up0, up0, up0, w8, w8, b, b)


def _down_loss(act, w_down, x2, target, tm):
    s = act.shape[0]

    def body(a_ref, w_ref, x_ref, t_ref, dyf_ref, dyb_ref, ls_ref):
        @pl.when(pl.program_id(0) == 0)
        def _():
            ls_ref[...] = jnp.zeros_like(ls_ref)

        y = x_ref[...] + jnp.dot(a_ref[...], w_ref[...], preferred_element_type=F32)
        e = y - t_ref[...]
        ls_ref[...] += jnp.sum(e * e)
        dy = e * (1.0 / D_MODEL)
        dyf_ref[...] = dy
        dyb_ref[...] = dy.astype(BF16)

    row = lambda i: (i, 0)
    return pl.pallas_call(
        body, grid=(s // tm,),
        in_specs=[pl.BlockSpec((tm, D_FF), row), pl.BlockSpec((D_FF, D_MODEL), lambda i: (0, 0)),
                  pl.BlockSpec((tm, D_MODEL), row), pl.BlockSpec((tm, D_MODEL), row)],
        out_specs=[pl.BlockSpec((tm, D_MODEL), row), pl.BlockSpec((tm, D_MODEL), row), pl.BlockSpec((8, 128), lambda i: (0, 0))],
        out_shape=[SDS((s, D_MODEL), F32), SDS((s, D_MODEL), BF16), SDS((8, 128), F32)],
        compiler_params=_cp(1), name="down_loss")(act, w_down, x2, target)


def _adamw_math(w, g, m, v):
    mn = ADAM_B1 * m + (1.0 - ADAM_B1) * g
    vn = ADAM_B2 * v + (1.0 - ADAM_B2) * (g * g)
    m_hat = mn / (1.0 - ADAM_B1 ** ADAM_STEP)
    v_hat = vn / (1.0 - ADAM_B2 ** ADAM_STEP)
    return -ADAM_LR * (m_hat / (jnp.sqrt(v_hat) + ADAM_EPS) + ADAM_WD * w), mn, vn


def _adamw(w, g, m, v, name):
    rows, cols = w.shape
    tr = rows if rows <= 256 else (256 if rows % 256 == 0 else rows // 2)

    def body(w_ref, g_ref, m_ref, v_ref, d_ref, mo_ref, vo_ref):
        d_ref[...], mo_ref[...], vo_ref[...] = _adamw_math(w_ref[...], g_ref[...], m_ref[...], v_ref[...])

    blk = pl.BlockSpec((tr, cols), lambda i: (i, 0))
    return pl.pallas_call(body, grid=(rows // tr,), in_specs=[blk] * 4, out_specs=[blk] * 3,
                          out_shape=[SDS((rows, cols), F32)] * 3, compiler_params=_cp(1), name=name)(w, g, m, v)


def _adamw_small(ws, gs, ms, vs):
    n = len(ws)

    def body(*refs):
        ins, outs = refs[:4 * n], refs[4 * n:]
        for k in range(n):
            d, mn, vn = _adamw_math(ins[k][...], ins[n + k][...], ins[2 * n + k][...], ins[3 * n + k][...])
            outs[k][...] = d
            outs[n + k][...] = mn
            outs[2 * n + k][...] = vn

    vm = pl.BlockSpec(memory_space=pltpu.VMEM)
    outs = pl.pallas_call(body, in_specs=[vm] * (4 * n), out_specs=[vm] * (3 * n),
                          out_shape=[SDS(w.shape, F32) for w in ws] * 3, name="adamw_small")(*ws, *gs, *ms, *vs)
    return outs[:n], outs[n:2 * n], outs[2 * n:]


ANY = pl.BlockSpec(memory_space=pl.ANY)


def _coords():
    return lax.axis_index("x"), lax.axis_index("y"), lax.axis_index("c")


def _other_chips(x, y):
    return [(1 - x, y), (x, 1 - y), (1 - x, 1 - y)]


D2D_CHUNKS = 8
ICI_CHUNKS = 4


def _row_chunks(n_rows, n_chunks, align):
    step = -(-n_rows // (n_chunks * align)) * align
    return [(r, min(step, n_rows - r)) for r in range(0, n_rows, step)]


def _ag_copy(out_ref, send_sems, recv_sems, k, shard, base, r0, nr, to, src=None):
    rows_ = pl.ds(pl.multiple_of(base + r0, 16), nr)
    dst = out_ref.at[shard, rows_]
    return pltpu.make_async_remote_copy(src_ref=dst if src is None else src.at[rows_], dst_ref=dst, send_sem=send_sems.at[k],
                                        recv_sem=recv_sems.at[k], device_id=to, device_id_type=MESH)


def _ag_send(w_ref, out_ref, send_sems, recv_sems):
    x, y, c = _coords()
    half_rows = w_ref.shape[0] // 2
    for k, (px, py) in enumerate(_other_chips(x, y)):
        for r0, nr in _row_chunks(half_rows, ICI_CHUNKS, 16):
            _ag_copy(out_ref, send_sems, recv_sems, k, 2 * x + y, c * half_rows, r0, nr, (px, py, c), src=w_ref).start()


def _ag_finish(w_ref, out_ref, send_sems, recv_sems):
    x, y, c = _coords()
    half_rows = w_ref.shape[0] // 2
    chips = _other_chips(x, y)
    sibling = (x, y, 1 - c)
    for k, (px, py) in enumerate(chips):
        _ag_copy(out_ref, send_sems, recv_sems, k, 2 * px + py, c * half_rows, 0, half_rows, (px, py, c)).wait_recv()
        for r0, nr in _row_chunks(half_rows, ICI_CHUNKS, 16):
            _ag_copy(out_ref, send_sems, recv_sems, 3 + k, 2 * px + py, c * half_rows, r0, nr, sibling).start()
    for k, (px, py) in enumerate(chips):
        _ag_copy(out_ref, send_sems, recv_sems, 3 + k, 2 * px + py, (1 - c) * half_rows, 0, half_rows, sibling).wait_recv()
    for k in range(6):
        _ag_copy(out_ref, send_sems, recv_sems, k, 2 * x + y, c * half_rows, 0, half_rows, sibling).wait_send()


def _ag_weights(wsh):
    rows, cols = wsh.shape

    def body(w_ref, out_ref, send_sems, recv_sems):
        _ag_send(w_ref, out_ref, send_sems, recv_sems)
        _ag_finish(w_ref, out_ref, send_sems, recv_sems)

    return pl.pallas_call(
        body, in_specs=[ANY], out_specs=ANY, out_shape=SDS((4, rows, cols), wsh.dtype),
        scratch_shapes=[pltpu.SemaphoreType.DMA((6,)), pltpu.SemaphoreType.DMA((6,))],
        name="ag_weights")(wsh)


def _rs_swap_halves(gfull, tag):
    n_sh, rows, cols = gfull.shape
    half_rows = rows // 2

    def body(g_ref, recv_ref, send_sem, recv_sem):
        x, y, c = _coords()
        sib_base = (1 - c) * half_rows
        for sh in range(n_sh):
            for r0, nr in _row_chunks(half_rows, D2D_CHUNKS, 8):
                pltpu.make_async_remote_copy(
                    src_ref=g_ref.at[sh, pl.ds(pl.multiple_of(sib_base + r0, 8), nr)], dst_ref=recv_ref.at[sh, pl.ds(r0, nr)],
                    send_sem=send_sem, recv_sem=recv_sem, device_id=(x, y, 1 - c), device_id_type=MESH).start()
        pltpu.make_async_remote_copy(src_ref=recv_ref, dst_ref=recv_ref, send_sem=send_sem, recv_sem=recv_sem,
                                     device_id=(x, y, 1 - c), device_id_type=MESH).wait()

    return pl.pallas_call(
        body, in_specs=[ANY], out_specs=ANY, out_shape=SDS((n_sh, half_rows, cols), gfull.dtype),
        scratch_shapes=[pltpu.SemaphoreType.DMA, pltpu.SemaphoreType.DMA], name="rs_swap_halves" + tag)(gfull)


def _rs_add_pair(gfull, recv, core, tr, tag):
    n_sh, rows, cols = gfull.shape
    half_rows = rows // 2
    nblk = half_rows // tr

    def body(c_ref, g_ref, r_ref, o_ref, ob_ref):
        acc = g_ref[...] + r_ref[...]
        o_ref[...] = acc
        ob_ref[...] = acc.astype(BF16)

    out = pl.BlockSpec((None, tr, cols), lambda sh, i, c_ref: (sh, i, 0))
    gs = pltpu.PrefetchScalarGridSpec(
        num_scalar_prefetch=1, grid=(n_sh, nblk),
        in_specs=[pl.BlockSpec((None, tr, cols), lambda sh, i, c_ref: (sh, c_ref[0] * nblk + i, 0)), out],
        out_specs=[out, out])
    return pl.pallas_call(body, grid_spec=gs, out_shape=[SDS((n_sh, half_rows, cols), F32), SDS((n_sh, half_rows, cols), BF16)],
                          compiler_params=_cp(2), name="rs_add_pair" + tag)(core, gfull, recv)


def _rs_send(cs_ref, recv_ref, send_sems, recv_sems):
    x, y, c = _coords()
    half_rows = cs_ref.shape[1]
    for k, (px, py) in enumerate(_other_chips(x, y)):
        for r0, nr in _row_chunks(half_rows, ICI_CHUNKS, 16):
            pltpu.make_async_remote_copy(
                src_ref=cs_ref.at[2 * px + py, pl.ds(r0, nr)], dst_ref=recv_ref.at[k, pl.ds(r0, nr)],
                send_sem=send_sems.at[k], recv_sem=recv_sems.at[k], device_id=(px, py, c), device_id_type=MESH).start()


def _rs_wait(recv_ref, send_sems, recv_sems):
    x, y, c = _coords()
    for k, (px, py) in enumerate(_other_chips(x, y)):
        pltpu.make_async_remote_copy(src_ref=recv_ref.at[k], dst_ref=recv_ref.at[k], send_sem=send_sems.at[k],
                                     recv_sem=recv_sems.at[k], device_id=(px, py, c), device_id_type=MESH).wait()


def _rs_to_owner(chipsum):
    n_sh, half_rows, cols = chipsum.shape

    def body(cs_ref, recv_ref, send_sems, recv_sems):
        _rs_send(cs_ref, recv_ref, send_sems, recv_sems)
        _rs_wait(recv_ref, send_sems, recv_sems)

    return pl.pallas_call(
        body, in_specs=[ANY], out_specs=ANY, out_shape=SDS((3, half_rows, cols), chipsum.dtype),
        scratch_shapes=[pltpu.SemaphoreType.DMA((3,)), pltpu.SemaphoreType.DMA((3,))], name="rs_to_owner")(chipsum)


def _rs_add_chips(chipsum, recv, shard_core, tr, tag):
    _, half_rows, cols = chipsum.shape

    def body(s_ref, m_ref, r0_ref, r1_ref, r2_ref, o_ref):
        o_ref[...] = ((m_ref[...] + r0_ref[...].astype(F32)) + r1_ref[...].astype(F32)) + r2_ref[...].astype(F32)

    gs = pltpu.PrefetchScalarGridSpec(
        num_scalar_prefetch=1, grid=(half_rows // tr,),
        in_specs=[pl.BlockSpec((None, tr, cols), lambda i, s_ref: (s_ref[0], i, 0))]
        + [pl.BlockSpec((None, tr, cols), (lambda k: lambda i, s_ref: (k, i, 0))(k)) for k in range(3)],
        out_specs=pl.BlockSpec((None, tr, cols), lambda i, s_ref: (s_ref[1], i, 0)))
    return pl.pallas_call(body, grid_spec=gs, out_shape=SDS((2, half_rows, cols), F32),
                          compiler_params=_cp(1), name="rs_add_chips" + tag)(shard_core, chipsum, recv, recv, recv)


def _rs_join_halves(buf, tag):
    _, half_rows, cols = buf.shape

    def body(b_ref, out_ref, send_sem, recv_sem):
        x, y, c = _coords()
        for r0, nr in _row_chunks(half_rows, D2D_CHUNKS, 8):
            pltpu.make_async_remote_copy(src_ref=out_ref.at[c, pl.ds(r0, nr)], dst_ref=out_ref.at[c, pl.ds(r0, nr)], send_sem=send_sem,
                                         recv_sem=recv_sem, device_id=(x, y, 1 - c), device_id_type=MESH).start()
        pltpu.make_async_remote_copy(src_ref=out_ref.at[c], dst_ref=out_ref.at[c], send_sem=send_sem, recv_sem=recv_sem,
                                     device_id=(x, y, 1 - c), device_id_type=MESH).wait()

    return pl.pallas_call(
        body, in_specs=[ANY], out_specs=ANY, out_shape=SDS(buf.shape, buf.dtype), input_output_aliases={0: 0},
        scratch_shapes=[pltpu.SemaphoreType.DMA, pltpu.SemaphoreType.DMA], name="rs_join_halves" + tag)(buf)


BIG = [("w_in", (1024, 1440), 1), ("w_uq", (256, 768), 1), ("w_ukv", (128, 1024), 1), ("w_out", (1024, 1024), 0),
       ("w_mem_q", (1024, 1024), 0), ("w_mem_kv", (1024, 2048), 1), ("w_mem_o", (1024, 1024), 0),
       ("w_up", (1024, 5632), 1), ("w_down", (2816, 1024), 0)]
SMALL_REP = [("mix_norm_g", 1024), ("b_conv_in", 1024), ("b_conv_dw", 512), ("conv_ln_g", 512), ("conv_ln_b", 512),
             ("q_lat_norm_g", 256), ("kv_lat_norm_g", 128), ("q_norm_g", 96), ("k_norm_g", 96), ("mem_norm_x_g", 1024),
             ("mem_norm_m_g", 1024), ("mem_q_norm_g", 256), ("mem_k_norm_g", 256), ("ffn_norm_g", 1024), ("b_ffn_dw", 5632)]
SMALL_SH = [("w_conv_dw", (31, 512)), ("w_ffn_dw", (3, 5632))]


def _shard_shape(shape, axis):
    return tuple(d // 4 if a == axis else d for a, d in enumerate(shape))


def _pack_rows(parts, rows, cols):
    flat = jnp.concatenate([p.reshape(-1) for p in parts])
    flat = jnp.pad(flat, (0, rows * cols - flat.shape[0]))
    return flat.reshape(rows, cols)


AG_EARLY, AG_LATE = BIG[:3], BIG[3:]
RS_REST, RS_FFN = AG_EARLY, AG_LATE


def _group_rows(group):
    used = sum(_shard_shape(shape, axis)[0] * _shard_shape(shape, axis)[1] // PACK_COLS for _, shape, axis in group)
    return -(-used // 512) * 512


def _pick_rows(n, cap=384):
    return max(r for r in range(16, cap + 1, 16) if n % r == 0)


def _pack_big_shards(ws, group):
    parts = [ws[n].reshape(-1, PACK_COLS) for n, _, _ in group]
    used = sum(p.shape[0] for p in parts)
    pad = _group_rows(group) - used
    return jnp.concatenate(parts + ([jnp.zeros((pad, PACK_COLS), parts[0].dtype)] if pad else []), axis=0)


def _unpack_big_shards(packed, group):
    out, r = {}, 0
    for n, shape, axis in group:
        sh = _shard_shape(shape, axis)
        nr = sh[0] * sh[1] // PACK_COLS
        out[n] = packed[r:r + nr].reshape(sh)
        r += nr
    return out


def _unpack_gathered(g, group):
    out, r = {}, 0
    for n, shape, axis in group:
        sh = _shard_shape(shape, axis)
        nr = sh[0] * sh[1] // PACK_COLS
        part = g[:, r:r + nr]
        if axis == 0:
            out[n] = part.reshape(shape)
        else:
            out[n] = part.reshape((4,) + sh).transpose(1, 0, 2).reshape(shape)
        r += nr
    return out


def _pack_full_grads(gs, group):
    parts = []
    for n, shape, axis in group:
        sh = _shard_shape(shape, axis)
        nr = sh[0] * sh[1] // PACK_COLS
        if axis == 0:
            parts.append(gs[n].reshape(4, nr, PACK_COLS))
        else:
            parts.append(gs[n].reshape(shape[0], 4, sh[1]).transpose(1, 0, 2).reshape(4, nr, PACK_COLS))
    pad = _group_rows(group) - sum(p.shape[1] for p in parts)
    return jnp.concatenate(parts + ([jnp.zeros((4, pad, PACK_COLS), F32)] if pad else []), axis=1)


def _rs_first(gfull, core_idx, tag):
    tr = _pick_rows(gfull.shape[1] // 2)
    return _rs_add_pair(gfull, _rs_swap_halves(gfull, tag), core_idx.reshape(1), tr, tag)


def _rs_last(chipsum, recv, shard_idx, core_idx, tag):
    tr = _pick_rows(chipsum.shape[1])
    red = _rs_add_chips(chipsum, recv, jnp.stack([shard_idx, core_idx]), tr, tag)
    return _rs_join_halves(red, tag).reshape(2 * chipsum.shape[1], chipsum.shape[2])


def _rope_tables(positions):
    inv_freq = ROPE_THETA ** (-jnp.arange(0, ROPE, 2, dtype=F32) / ROPE)
    ang = positions.astype(F32)[:, None] * inv_freq
    cos, sin = jnp.cos(ang), jnp.sin(ang)
    s = positions.shape[0]
    cosf = jnp.concatenate([jnp.ones((s, NOPE), F32), cos, cos, jnp.ones((s, HEAD_PAD - HEAD_DIM), F32)], axis=-1)
    sinf = jnp.concatenate([jnp.zeros((s, NOPE), F32), -sin, sin, jnp.zeros((s, HEAD_PAD - HEAD_DIM), F32)], axis=-1)
    return cosf, sinf


def _pad_heads(w, per_head):
    k = w.shape[0]
    w3 = w.reshape(k, HEADS, per_head)
    return jnp.pad(w3, ((0, 0), (0, 0), (0, HEAD_PAD - per_head))).reshape(k, HEADS * HEAD_PAD)


def _layer_grads(x, mem, positions, target, wf, w_late, sp, shard_idx, core_idx):
    wf = dict(wf)
    s = x.shape[0]
    tm = _row_tile(s, 512)
    tc = _row_tile(s, 256)
    tb = 512 if s % 512 == 0 and s > 512 else s // 2
    row2 = lambda a: a.reshape(1, -1)

    w_in = wf["w_in"]
    w_in_pad = jnp.concatenate([w_in[:, :1408], jnp.zeros((D_MODEL, NOPE), BF16), w_in[:, 1408:],
                                jnp.zeros((D_MODEL, HEAD_PAD - HEAD_DIM), BF16)], axis=1)
    w_uq_pad = _pad_heads(wf["w_uq"], HEAD_DIM)
    w_ukv = wf["w_ukv"]
    gq_pad = jnp.pad(sp["q_norm_g"], (0, HEAD_PAD - HEAD_DIM)).reshape(1, HEAD_PAD)
    gk_pad = jnp.pad(sp["k_norm_g"], (0, HEAD_PAD - HEAD_DIM)).reshape(1, HEAD_PAD)
    w_dw32 = jnp.pad(sp["w_conv_dw"], ((0, 1), (0, 0)))
    w_ffn8 = jnp.pad(sp["w_ffn_dw"], ((0, 5), (0, 0)))
    b_ffn = row2(sp["b_ffn_dw"])
    cosf, sinf = _rope_tables(positions)

    z, h1 = _norm_linear(x, 0, D_MODEL, row2(sp["mix_norm_g"]), w_in_pad, F32, tm, IN_COLS_PAD, "in_proj")
    u, u0, u1 = _conv_fwd(z, row2(sp["b_conv_in"]), w_dw32, row2(sp["b_conv_dw"]), row2(sp["conv_ln_g"]), row2(sp["conv_ln_b"]), tc)
    q_raw, cqn = _norm_linear(z, 1024 // Q_RANK, Q_RANK, row2(sp["q_lat_norm_g"]), w_uq_pad, F32, tm, 1024, "q_up")
    kv_raw, ckvn = _norm_linear(z, 1280 // KV_RANK, KV_RANK, row2(sp["kv_lat_norm_g"]), w_ukv, F32, tm, 1024, "kv_up")
    qp, kp, vp = _mla_prep(q_raw, kv_raw, z, cosf, sinf, gq_pad, gk_pad, tc)
    o_f, o_b, lse, gathered = _attn_fwd(qp, kp, vp, tb, w_late)
    wf.update(_unpack_gathered(lax.dynamic_update_index_in_dim(gathered, w_late, shard_idx, 0), AG_LATE))
    w_out_u = wf["w_out"][:CONV_CH]
    w_out_o = jnp.pad(wf["w_out"][CONV_CH:].reshape(HEADS, NOPE, D_MODEL), ((0, 0), (NOPE, 0), (0, 0))).reshape(HEADS * HEAD_PAD, D_MODEL)
    w_up_g, w_up_v = wf["w_up"][:, :D_FF], wf["w_up"][:, D_FF:]
    (x1,) = _linear([(u, w_out_u), (o_b, w_out_o)], False, x, [F32], tm, 1024, "out_proj")

    qm, hq = _norm_linear(x1, 0, D_MODEL, row2(sp["mem_norm_x_g"]), wf["w_mem_q"], F32, tm, 1024, "memq_proj")
    kvm, hm = _norm_linear(mem, 0, D_MODEL, row2(sp["mem_norm_m_g"]), wf["w_mem_kv"], F32, MEM_LEN, 1024, "memkv_proj")
    gmq, gmk = row2(sp["mem_q_norm_g"]), row2(sp["mem_k_norm_g"])
    o_m = _memattn_fwd(qm, kvm, gmq, gmk, tm)
    (x2,) = _linear([(o_m, wf["w_mem_o"])], False, x1, [F32], tm, 1024, "memo_proj")

    up0, h3 = _norm_linear(x2, 0, D_MODEL, row2(sp["ffn_norm_g"]), wf["w_up"], F32, tm, D_FF // 2, "up_proj")
    act = _ffn_fwd(up0, w_ffn8, b_ffn, tc, D_FF // 2)
    dy_f, dy_b, lsum = _down_loss(act, wf["w_down"], x2, target, tm)

    g = {}
    (d_act,) = _linear([(dy_b, wf["w_down"])], True, None, [F32], tm, D_FF // 2, "down_bwd")
    g["w_down"] = _dw(act, dy_b, "dw_down")
    d_up0g, d_up0v, dbg, dbv, dwg, dwv = _ffn_bwd(d_act, up0, w_ffn8, b_ffn, tc, D_FF // 2)
    g["b_ffn_dw"] = jnp.concatenate([dbg, dbv], axis=1).reshape(-1)
    g["w_ffn_dw"] = jnp.concatenate([dwg[:3], dwv[:3]], axis=1)
    g["w_up"] = jnp.concatenate([_dw(h3, d_up0g, "dw_up_g"), _dw(h3, d_up0v, "dw_up_v")], axis=1)
    d_x2f, d_x2b, dg = _linear_normbwd([(d_up0g, w_up_g), (d_up0v, w_up_v)], x2, 0, row2(sp["ffn_norm_g"]), dy_f,
                                       [F32, BF16], tc, "up_bwd")
    g["ffn_norm_g"] = dg.reshape(-1)

    (d_om,) = _linear([(d_x2b, wf["w_mem_o"])], True, None, [BF16], tm, 1024, "memo_bwd")
    g["w_mem_o"] = _dw(o_m, d_x2b, "dw_mem_o")
    d_qm, d_km, d_vm, dgq, dgk = _memattn_bwd(qm, kvm, d_om, gmq, gmk, tm)
    g["mem_q_norm_g"], g["mem_k_norm_g"] = dgq.reshape(-1), dgk.reshape(-1)
    d_kvm = jnp.concatenate([d_km, d_vm], axis=1)
    g["w_mem_q"] = _dw(hq, d_qm, "dw_mem_q")
    g["w_mem_kv"] = _dw(hm, d_kvm, "dw_mem_kv")
    d_x1f, d_x1b, dg = _linear_normbwd([(d_qm, wf["w_mem_q"])], x1, 0, row2(sp["mem_norm_x_g"]), d_x2f, [F32, BF16], tm, "memq_bwd")
    g["mem_norm_x_g"] = dg.reshape(-1)
    _, dg = _linear_normbwd([(d_kvm, wf["w_mem_kv"])], mem, 0, row2(sp["mem_norm_m_g"]), None, [BF16], MEM_LEN, "memkv_bwd")
    g["mem_norm_m_g"] = dg.reshape(-1)

    (d_u,) = _linear([(d_x1b, w_out_u)], True, None, [F32], tm, CONV_CH, "out_bwd_u")
    d_of, d_ob = _linear([(d_x1b, w_out_o)], True, None, [F32, BF16], tm, 1024, "out_bwd_o")
    dw_out_u = _dw(u, d_x1b, "dw_out_u")
    dw_out_o = _dw(o_b, d_x1b, "dw_out_o")
    g["w_out"] = jnp.concatenate([dw_out_u, dw_out_o.reshape(HEADS, HEAD_PAD, D_MODEL)[:, NOPE:].reshape(HEADS * NOPE, D_MODEL)], axis=0)
    chipsum_ffn, wire_ffn = _rs_first(_pack_full_grads(g, RS_FFN), core_idx, "_ffn")
    delta = _attn_delta(d_of, o_f, tb)
    dqp, dkp, dvp, recv_ffn = _attn_bwd(qp, kp, vp, d_ob, lse, delta, tb, wire_ffn)
    g_ffn_packed = _rs_last(chipsum_ffn, recv_ffn, shard_idx, core_idx, "_ffn")
    d_qraw, d_kvraw, d_kr, dgq, dgk = _mla_prep_bwd(dqp, dkp, dvp, q_raw, kv_raw, z, cosf, sinf, gq_pad, gk_pad, tc)
    g["q_norm_g"], g["k_norm_g"] = dgq.reshape(-1)[:HEAD_DIM], dgk.reshape(-1)[:HEAD_DIM]
    g["w_uq"] = _dw(cqn, d_qraw, "dw_uq").reshape(Q_RANK, HEADS, HEAD_PAD)[:, :, :HEAD_DIM].reshape(Q_RANK, HEADS * HEAD_DIM)
    g["w_ukv"] = _dw(ckvn, d_kvraw, "dw_ukv")
    d_cq, dg = _linear_normbwd([(d_qraw, w_uq_pad)], z, 1024 // Q_RANK, row2(sp["q_lat_norm_g"]), None, [BF16], tm, "q_up_bwd")
    g["q_lat_norm_g"] = dg.reshape(-1)
    d_ckv, dg = _linear_normbwd([(d_kvraw, w_ukv)], z, 1280 // KV_RANK, row2(sp["kv_lat_norm_g"]), None, [BF16], tm, "kv_up_bwd")
    g["kv_lat_norm_g"] = dg.reshape(-1)
    d_u1, dlg, dlb, dbdw = _conv_bwd_ln(d_u, u1, row2(sp["conv_ln_g"]), row2(sp["conv_ln_b"]), tc)
    g["conv_ln_g"], g["conv_ln_b"], g["b_conv_dw"] = dlg.reshape(-1), dlb.reshape(-1), dbdw.reshape(-1)
    d_conv, dw_dw, dbin = _conv_bwd_dw(d_u1, u0, z, row2(sp["b_conv_in"]), w_dw32, tc)
    g["w_conv_dw"], g["b_conv_in"] = dw_dw[:CONV_WIDTH], dbin.reshape(-1)
    pieces = [(d_conv, w_in_pad[:, :1024]), (d_cq, w_in_pad[:, 1024:1280]), (d_ckv, w_in_pad[:, 1280:1408]), (d_kr, w_in_pad[:, 1408:])]
    dw_in = [_dw(h1, d, "dw_in_%d" % k) for k, (d, _) in enumerate(pieces)]
    g["w_in"] = jnp.concatenate([dw_in[0], dw_in[1], dw_in[2], dw_in[3][:, NOPE:HEAD_DIM]], axis=1)
    grad_x, dg = _linear_normbwd(pieces, x, 0, row2(sp["mix_norm_g"]), d_x1f, [F32], tm, "in_bwd")
    g["mix_norm_g"] = dg.reshape(-1)
    return lsum[0, 0], grad_x, g, g_ffn_packed


def kernel(x, mem, positions, mix_norm_g, w_in, b_conv_in, w_conv_dw, b_conv_dw, conv_ln_g, conv_ln_b, q_lat_norm_g, w_uq, kv_lat_norm_g, w_ukv, q_norm_g, k_norm_g, w_out, mem_norm_x_g, mem_norm_m_g, w_mem_q, w_mem_kv, mem_q_norm_g, mem_k_norm_g, w_mem_o, ffn_norm_g, w_up, w_ffn_dw, b_ffn_dw, w_down, loss_target, m_mix_norm_g, m_w_in, m_b_conv_in, m_w_conv_dw, m_b_conv_dw, m_conv_ln_g, m_conv_ln_b, m_q_lat_norm_g, m_w_uq, m_kv_lat_norm_g, m_w_ukv, m_q_norm_g, m_k_norm_g, m_w_out, m_mem_norm_x_g, m_mem_norm_m_g, m_w_mem_q, m_w_mem_kv, m_mem_q_norm_g, m_mem_k_norm_g, m_w_mem_o, m_ffn_norm_g, m_w_up, m_w_ffn_dw, m_b_ffn_dw, m_w_down, v_mix_norm_g, v_w_in, v_b_conv_in, v_w_conv_dw, v_b_conv_dw, v_conv_ln_g, v_conv_ln_b, v_q_lat_norm_g, v_w_uq, v_kv_lat_norm_g, v_w_ukv, v_q_norm_g, v_k_norm_g, v_w_out, v_mem_norm_x_g, v_mem_norm_m_g, v_w_mem_q, v_w_mem_kv, v_mem_q_norm_g, v_mem_k_norm_g, v_w_mem_o, v_ffn_norm_g, v_w_up, v_w_ffn_dw, v_b_ffn_dw, v_w_down):
    names = ["mix_norm_g", "w_in", "b_conv_in", "w_conv_dw", "b_conv_dw", "conv_ln_g", "conv_ln_b", "q_lat_norm_g", "w_uq",
             "kv_lat_norm_g", "w_ukv", "q_norm_g", "k_norm_g", "w_out", "mem_norm_x_g", "mem_norm_m_g", "w_mem_q", "w_mem_kv",
             "mem_q_norm_g", "mem_k_norm_g", "w_mem_o", "ffn_norm_g", "w_up", "w_ffn_dw", "b_ffn_dw", "w_down"]
    loc = locals()
    w = {n: loc[n] for n in names}
    m = {n: loc["m_" + n] for n in names}
    v = {n: loc["v_" + n] for n in names}
    shard_idx = 2 * lax.axis_index("x") + lax.axis_index("y")

    shard_idx = shard_idx.astype(jnp.int32)
    core_idx = lax.axis_index("c").astype(jnp.int32)

    w_local = {n: w[n][0] for n, _, _ in BIG}
    w_early = _pack_big_shards(w_local, AG_EARLY).astype(BF16)
    w_late = _pack_big_shards(w_local, AG_LATE).astype(BF16)
    wf = _unpack_gathered(lax.dynamic_update_index_in_dim(_ag_weights(w_early), w_early, shard_idx, 0), AG_EARLY)

    small_sh_full = {}
    gather_in = []
    for n, (r, c) in SMALL_SH:
        csh = c // 4
        slab = lax.dynamic_update_slice(jnp.zeros((r, c), F32), w[n][0], (0, shard_idx * csh))
        gather_in.append(slab.reshape(-1))
    gather_rows = 256
    gathered_small = _allreduce_small_named(_pack_rows(gather_in, gather_rows, SMALL_COLS), "gather_small") * 0.5
    off = 0
    for n, (r, c) in SMALL_SH:
        small_sh_full[n] = gathered_small.reshape(-1)[off:off + r * c].reshape(r, c)
        off += r * c
    sp = {n: w[n][0] for n, _ in SMALL_REP}
    sp.update(small_sh_full)

    lsum, grad_x, g, g_ffn_packed = _layer_grads(x[0], mem[0], positions[0], loss_target[0], wf, w_late, sp, shard_idx, core_idx)

    small_parts = [jnp.full((SMALL_COLS,), lsum, F32)] + [g[n] for n, _ in SMALL_REP] + [g[n] for n, _ in SMALL_SH]
    small_rows = 368
    small_sum = _allreduce_small_named(_pack_rows(small_parts, small_rows, SMALL_COLS), "allreduce_small").reshape(-1)
    loss = small_sum[0] * (0.5 / D_MODEL)
    gs = {}
    off = SMALL_COLS
    for n, sz in SMALL_REP:
        gs[n] = small_sum[off:off + sz].reshape(w[n].shape)
        off += sz
    for n, (r, c) in SMALL_SH:
        full = small_sum[off:off + r * c].reshape(r, c)
        gs[n] = lax.dynamic_slice(full, (0, shard_idx * (c // 4)), (r, c // 4)).reshape(w[n].shape)
        off += r * c

    chipsum, chipsum_wire = _rs_first(_pack_full_grads(g, RS_REST), core_idx, "_rest")
    g_rest_packed = _rs_last(chipsum, _rs_to_owner(chipsum_wire), shard_idx, core_idx, "_rest")
    g_big = {**_unpack_big_shards(g_rest_packed, RS_REST), **_unpack_big_shards(g_ffn_packed, RS_FFN)}
    gs.update({n: a[None] for n, a in g_big.items()})

    delta, new_m, new_v = {}, {}, {}
    for n, _, _ in BIG:
        d_n, m_n, v_n = _adamw(w[n][0], g_big[n], m[n][0], v[n][0], "adamw_" + n)
        delta[n], new_m[n], new_v[n] = d_n[None], m_n[None], v_n[None]
    small_names = [n for n, _ in SMALL_REP] + [n for n, _ in SMALL_SH]
    as2d = lambda a: a.reshape(-1, a.shape[-1])
    d_s, m_s, v_s = _adamw_small(*[[as2d(d[n]) for n in small_names] for d in (w, gs, m, v)])
    for k, n in enumerate(small_names):
        delta[n], new_m[n], new_v[n] = d_s[k].reshape(w[n].shape), m_s[k].reshape(w[n].shape), v_s[k].reshape(w[n].shape)

    return (loss, grad_x[None], *[gs[n] for n in names], *[delta[n] for n in names], *[new_m[n] for n in names],
            *[new_v[n] for n in names])


def _allreduce_small_named(v, name):
    rows, cols = v.shape

    def body(v_ref, out_ref, buf, send_sems, recv_sems):
        x, y, c = _coords()
        me = 4 * x + 2 * y + c
        buf[me] = v_ref[...]
        cps = []
        for r in range(1, 8):
            dx, dy, dc = (r >> 2) & 1, (r >> 1) & 1, r & 1
            to = (x + dx - 2 * x * dx, y + dy - 2 * y * dy, c + dc - 2 * c * dc)
            cp = pltpu.make_async_remote_copy(src_ref=v_ref, dst_ref=buf.at[me], send_sem=send_sems.at[r - 1],
                                              recv_sem=recv_sems.at[r - 1], device_id=to, device_id_type=MESH)
            cp.start()
            cps.append(cp)
        for cp in cps:
            cp.wait()
        acc = buf[0]
        for d in range(1, 8):
            acc = acc + buf[d]
        out_ref[...] = acc

    vm = pl.BlockSpec(memory_space=pltpu.VMEM)
    return pl.pallas_call(
        body, in_specs=[vm], out_specs=vm, out_shape=SDS((rows, cols), F32),
        scratch_shapes=[pltpu.VMEM((8, rows, cols), F32), pltpu.SemaphoreType.DMA((7,)), pltpu.SemaphoreType.DMA((7,))],
        name=name)(v)
```

```python
import math

import numpy as np
import jax
import jax.numpy as jnp
from jax import lax
from jax.experimental import pallas as pl
from jax.experimental.pallas import tpu as pltpu

F32 = jnp.float32
BF16 = jnp.bfloat16
SDS = jax.ShapeDtypeStruct
MESH = pl.DeviceIdType.MESH

D_MODEL = 1024
EPS = 1e-6
CONV_CH = 512
CONV_WIDTH = 31
CONV_HALO = 32
HEADS = 8
NOPE = 64
ROPE = 32
HEAD_DIM = NOPE + ROPE
HEAD_PAD = 128
Q_RANK = 256
KV_RANK = 128
CHUNK = 64
ROPE_THETA = 10000.0
IN_COLS_PAD = 1536
MEM_HEADS = 4
MEM_HEAD_DIM = 256
MEM_LEN = 256
D_FF = 2816
FFN_HALO = 8
ATT_SCALE = 1.0 / math.sqrt(HEAD_DIM)
LOG2E = math.log2(math.e)
LN2 = math.log(2.0)

ADAM_LR = 0.001
ADAM_B1 = 0.9
ADAM_B2 = 0.999
ADAM_EPS = 1e-08
ADAM_WD = 0.01
ADAM_STEP = 10

VMEM_LIMIT_V7X = 56 * 1024 * 1024
PACK_COLS = 1024
SMALL_COLS = 128


def _cp(n_axes):
    return pltpu.CompilerParams(dimension_semantics=("arbitrary",) * n_axes, vmem_limit_bytes=VMEM_LIMIT_V7X)


def _row_tile(s, want):
    return want if s % want == 0 else s


def _norm_linear(x, xcol, kdim, g, w, out_dtype, tm, tn, name):
    s = x.shape[0]
    n = w.shape[1]

    def body(x_ref, g_ref, w_ref, y_ref, hn_ref):
        @pl.when(pl.program_id(1) == 0)
        def _():
            xv = x_ref[...]
            r = lax.rsqrt(jnp.mean(xv * xv, axis=-1, keepdims=True) + EPS)
            hn_ref[...] = ((xv * r) * g_ref[...]).astype(BF16)

        y_ref[...] = jnp.dot(hn_ref[...], w_ref[...], preferred_element_type=F32).astype(y_ref.dtype)

    return pl.pallas_call(
        body, grid=(s // tm, n // tn),
        in_specs=[pl.BlockSpec((tm, kdim), lambda i, j: (i, xcol)), pl.BlockSpec((1, kdim), lambda i, j: (0, 0)),
                  pl.BlockSpec((kdim, tn), lambda i, j: (0, j))],
        out_specs=[pl.BlockSpec((tm, tn), lambda i, j: (i, j)), pl.BlockSpec((tm, kdim), lambda i, j: (i, 0))],
        out_shape=[SDS((s, n), out_dtype), SDS((s, kdim), BF16)],
        compiler_params=_cp(2), name=name)(x, g, w)


def _linear(pairs, nt, residual, out_dtypes, tm, tn, name):
    s = pairs[0][0].shape[0]
    n = pairs[0][1].shape[0] if nt else pairs[0][1].shape[1]
    n_pairs = len(pairs)
    has_res = residual is not None

    def body(*refs):
        a_refs = refs[:n_pairs]
        w_refs = refs[n_pairs:2 * n_pairs]
        res_ref = refs[2 * n_pairs] if has_res else None
        outs = refs[2 * n_pairs + int(has_res):]
        acc = None
        for a_ref, w_ref in zip(a_refs, w_refs):
            a = a_ref[...].astype(BF16)
            if nt:
                d = lax.dot_general(a, w_ref[...], (((1,), (1,)), ((), ())), preferred_element_type=F32)
            else:
                d = jnp.dot(a, w_ref[...], preferred_element_type=F32)
            acc = d if acc is None else acc + d
        if has_res:
            acc = res_ref[...] + acc
        for o in outs:
            o[...] = acc.astype(o.dtype)

    in_specs = [pl.BlockSpec((tm, a.shape[1]), lambda i, j: (i, 0)) for a, _ in pairs]
    if nt:
        in_specs += [pl.BlockSpec((tn, w.shape[1]), lambda i, j: (j, 0)) for _, w in pairs]
    else:
        in_specs += [pl.BlockSpec((w.shape[0], tn), lambda i, j: (0, j)) for _, w in pairs]
    args = [a for a, _ in pairs] + [w for _, w in pairs]
    if has_res:
        in_specs.append(pl.BlockSpec((tm, tn), lambda i, j: (i, j)))
        args.append(residual)
    outs = pl.pallas_call(
        body, grid=(s // tm, n // tn), in_specs=in_specs,
        out_specs=[pl.BlockSpec((tm, tn), lambda i, j: (i, j)) for _ in out_dtypes],
        out_shape=[SDS((s, n), dt) for dt in out_dtypes],
        compiler_params=_cp(2), name=name)(*args)
    return outs


def _linear_normbwd(pairs, x, xcol, g, d_res, out_dtypes, tm, name):
    s = pairs[0][0].shape[0]
    dn = pairs[0][1].shape[0]
    n_pairs = len(pairs)
    has_res = d_res is not None

    def body(*refs):
        a_refs = refs[:n_pairs]
        w_refs = refs[n_pairs:2 * n_pairs]
        x_ref, g_ref = refs[2 * n_pairs], refs[2 * n_pairs + 1]
        k = 2 * n_pairs + 2
        res_ref = refs[k] if has_res else None
        k += int(has_res)
        outs = refs[k:-1]
        dg_ref = refs[-1]
        dh = None
        for a_ref, w_ref in zip(a_refs, w_refs):
            d = lax.dot_general(a_ref[...].astype(BF16), w_ref[...], (((1,), (1,)), ((), ())), preferred_element_type=F32)
            dh = d if dh is None else dh + d
        xv = x_ref[...]
        r = lax.rsqrt(jnp.mean(xv * xv, axis=-1, keepdims=True) + EPS)
        y = xv * r

        @pl.when(pl.program_id(0) == 0)
        def _():
            dg_ref[...] = jnp.zeros_like(dg_ref)

        dg_ref[...] += jnp.sum(dh * y, axis=0, keepdims=True)
        dy = dh * g_ref[...]
        dx = r * (dy - y * jnp.mean(dy * y, axis=-1, keepdims=True))
        if has_res:
            dx = res_ref[...] + dx
        for o in outs:
            o[...] = dx.astype(o.dtype)

    in_specs = [pl.BlockSpec((tm, a.shape[1]), lambda i: (i, 0)) for a, _ in pairs]
    in_specs += [pl.BlockSpec((dn, w.shape[1]), lambda i: (0, 0)) for _, w in pairs]
    in_specs += [pl.BlockSpec((tm, dn), lambda i: (i, xcol)), pl.BlockSpec((1, dn), lambda i: (0, 0))]
    args = [a for a, _ in pairs] + [w for _, w in pairs] + [x, g]
    if has_res:
        in_specs.append(pl.BlockSpec((tm, dn), lambda i: (i, 0)))
        args.append(d_res)
    outs = pl.pallas_call(
        body, grid=(s // tm,), in_specs=in_specs,
        out_specs=[pl.BlockSpec((tm, dn), lambda i: (i, 0)) for _ in out_dtypes] + [pl.BlockSpec((1, dn), lambda i: (0, 0))],
        out_shape=[SDS((s, dn), dt) for dt in out_dtypes] + [SDS((1, dn), F32)],
        compiler_params=_cp(1), name=name)(*args)
    return outs


def _dw_matmul(a, b, tk, tn, ts, name):
    s, ka = a.shape
    n = b.shape[1]

    def body(a_ref, b_ref, o_ref):
        @pl.when(pl.program_id(2) == 0)
        def _():
            o_ref[...] = jnp.zeros_like(o_ref)

        o_ref[...] += lax.dot_general(a_ref[...].astype(BF16), b_ref[...].astype(BF16), (((0,), (0,)), ((), ())),
                                      preferred_element_type=F32)

    return pl.pallas_call(
        body, grid=(ka // tk, n // tn, s // ts),
        in_specs=[pl.BlockSpec((ts, tk), lambda k, j, t: (t, k)), pl.BlockSpec((ts, tn), lambda k, j, t: (t, j))],
        out_specs=pl.BlockSpec((tk, tn), lambda k, j, t: (k, j)),
        out_shape=SDS((ka, n), F32), compiler_params=_cp(3), name=name)(a, b)


def _dw(a, b, name):
    s, ka = a.shape
    n = b.shape[1]
    tk = ka if ka <= 1024 else ka // 2
    tn = n if n <= 1024 else (n // 2 if n == D_FF else 512)
    return _dw_matmul(a, b, tk, tn, _row_tile(s, 2048), name)


def _prev_halo(tm, halo):
    return lambda i: (jnp.maximum(i * (tm // halo) - 1, 0), 0)


def _next_halo(tm, halo, s):
    return lambda i: (jnp.minimum((i + 1) * (tm // halo), s // halo - 1), 0)


def _shifted_copies(ext, tm):
    n = tm + CONV_HALO - 8
    for s in range(1, 8):
        ext[s, 0:n, :] = ext[0, s:s + n, :]


def _sum_taps(terms, ways=4):
    accs = []
    for i, t in enumerate(terms):
        if i < ways:
            accs.append(t)
        else:
            accs[i % ways] = accs[i % ways] + t
    while len(accs) > 1:
        accs = [accs[i] + accs[i + 1] if i + 1 < len(accs) else accs[i] for i in range(0, len(accs), 2)]
    return accs[0]


def _tap_rows(ext, o, n, cs):
    return ext[o % 8, o - o % 8:o - o % 8 + n, cs]


def _conv_fwd(z, b_in, w32, b_dw, ln_g, ln_b, tm):
    s = z.shape[0]
    c = CONV_CH

    def body(z_ref, zh_ref, bin_ref, w_ref, bdw_ref, lg_ref, lb_ref, u_ref, u0_ref, u1_ref, ext):
        i = pl.program_id(0)

        def glu(zz):
            zz = zz + bin_ref[...]
            return zz[:, :c] * jax.nn.sigmoid(zz[:, c:])

        u0 = glu(z_ref[...])
        u0_ref[...] = u0
        ext[0, 0:CONV_HALO, :] = jnp.where(i > 0, glu(zh_ref[...]), 0.0)
        ext[0, CONV_HALO:, :] = u0
        _shifted_copies(ext, tm)
        off = CONV_HALO - (CONV_WIDTH - 1)
        for r in range(tm // 64):
            for cb in range(c // 128):
                cs = slice(cb * 128, (cb + 1) * 128)
                u1_ref[r * 64:(r + 1) * 64, cs] = _sum_taps(
                    _tap_rows(ext, r * 64 + off + k, 64, cs) * w_ref[k:k + 1, cs] for k in range(CONV_WIDTH)) + bdw_ref[:, cs]
        u1 = u1_ref[...]
        mu = jnp.mean(u1, axis=-1, keepdims=True)
        xc = u1 - mu
        y = xc * lax.rsqrt(jnp.mean(xc * xc, axis=-1, keepdims=True) + EPS)
        y = y * lg_ref[...] + lb_ref[...]
        u_ref[...] = (y * jax.nn.sigmoid(y)).astype(BF16)

    row = lambda i: (i, 0)
    fix = lambda i: (0, 0)
    return pl.pallas_call(
        body, grid=(s // tm,),
        in_specs=[pl.BlockSpec((tm, 2 * c), row), pl.BlockSpec((CONV_HALO, 2 * c), _prev_halo(tm, CONV_HALO)),
                  pl.BlockSpec((1, 2 * c), fix), pl.BlockSpec((32, c), fix), pl.BlockSpec((1, c), fix),
                  pl.BlockSpec((1, c), fix), pl.BlockSpec((1, c), fix)],
        out_specs=[pl.BlockSpec((tm, c), row)] * 3,
        out_shape=[SDS((s, c), BF16), SDS((s, c), F32), SDS((s, c), F32)],
        scratch_shapes=[pltpu.VMEM((8, tm + CONV_HALO, c), F32)],
        compiler_params=_cp(1), name="conv_fwd")(z, z, b_in, w32, b_dw, ln_g, ln_b)


def _conv_bwd_ln(d_u, u1, ln_g, ln_b, tm):
    s = d_u.shape[0]
    c = CONV_CH

    def body(du_ref, u1_ref, lg_ref, lb_ref, du1_ref, dlg_ref, dlb_ref, dbdw_ref):
        @pl.when(pl.program_id(0) == 0)
        def _():
            dlg_ref[...] = jnp.zeros_like(dlg_ref)
            dlb_ref[...] = jnp.zeros_like(dlb_ref)
            dbdw_ref[...] = jnp.zeros_like(dbdw_ref)

        u1 = u1_ref[...]
        mu = jnp.mean(u1, axis=-1, keepdims=True)
        xc = u1 - mu
        rs = lax.rsqrt(jnp.mean(xc * xc, axis=-1, keepdims=True) + EPS)
        xh = xc * rs
        y = xh * lg_ref[...] + lb_ref[...]
        sg = jax.nn.sigmoid(y)
        dy = du_ref[...] * (sg * (1.0 + y * (1.0 - sg)))
        dlg_ref[...] += jnp.sum(dy * xh, axis=0, keepdims=True)
        dlb_ref[...] += jnp.sum(dy, axis=0, keepdims=True)
        dxh = dy * lg_ref[...]
        du1 = rs * (dxh - jnp.mean(dxh, axis=-1, keepdims=True) - xh * jnp.mean(dxh * xh, axis=-1, keepdims=True))
        dbdw_ref[...] += jnp.sum(du1, axis=0, keepdims=True)
        du1_ref[...] = du1

    row = lambda i: (i, 0)
    fix = lambda i: (0, 0)
    return pl.pallas_call(
        body, grid=(s // tm,),
        in_specs=[pl.BlockSpec((tm, c), row), pl.BlockSpec((tm, c), row), pl.BlockSpec((1, c), fix), pl.BlockSpec((1, c), fix)],
        out_specs=[pl.BlockSpec((tm, c), row)] + [pl.BlockSpec((1, c), fix)] * 3,
        out_shape=[SDS((s, c), F32)] + [SDS((1, c), F32)] * 3,
        compiler_params=_cp(1), name="conv_bwd_ln")(d_u, u1, ln_g, ln_b)


def _conv_bwd_dw(d_u1, u0, z, b_in, w32, tm):
    s = d_u1.shape[0]
    c = CONV_CH

    def body(d_ref, dn_ref, u0_ref, u0p_ref, z_ref, bin_ref, w_ref, dz_ref, dw_ref, dbin_ref, extd, extu, du0):
        i = pl.program_id(0)
        last = pl.num_programs(0) - 1

        @pl.when(i == 0)
        def _():
            dw_ref[...] = jnp.zeros_like(dw_ref)
            dbin_ref[...] = jnp.zeros_like(dbin_ref)

        extd[0, 0:tm, :] = d_ref[...]
        extd[0, tm:, :] = jnp.where(i < last, dn_ref[...], 0.0)
        extu[0, 0:CONV_HALO, :] = jnp.where(i > 0, u0p_ref[...], 0.0)
        extu[0, CONV_HALO:, :] = u0_ref[...]
        _shifted_copies(extd, tm)
        _shifted_copies(extu, tm)
        off = CONV_HALO - (CONV_WIDTH - 1)
        for r in range(tm // 64):
            for cb in range(c // 128):
                cs = slice(cb * 128, (cb + 1) * 128)
                du0[r * 64:(r + 1) * 64, cs] = _sum_taps(
                    _tap_rows(extd, r * 64 + (CONV_WIDTH - 1) - k, 64, cs) * w_ref[k:k + 1, cs] for k in range(CONV_WIDTH))
        for cb in range(c // 128):
            cs = slice(cb * 128, (cb + 1) * 128)
            for k in range(CONV_WIDTH):
                parts = []
                for r in range(tm // 64):
                    p = d_ref[r * 64:(r + 1) * 64, cs] * _tap_rows(extu, r * 64 + off + k, 64, cs)
                    parts.append(_sum_taps(p[q * 8:(q + 1) * 8, :] for q in range(8)))
                dw_ref[k:k + 1, cs] += jnp.sum(_sum_taps(parts), axis=0, keepdims=True)
        zz = z_ref[...] + bin_ref[...]
        a = zz[:, :c]
        sg = jax.nn.sigmoid(zz[:, c:])
        d0 = du0[...]
        da = d0 * sg
        dgt = d0 * a * (sg * (1.0 - sg))
        dbin_ref[:, :c] += jnp.sum(da, axis=0, keepdims=True)
        dbin_ref[:, c:] += jnp.sum(dgt, axis=0, keepdims=True)
        dz_ref[:, :c] = da.astype(BF16)
        dz_ref[:, c:] = dgt.astype(BF16)

    row = lambda i: (i, 0)
    fix = lambda i: (0, 0)
    return pl.pallas_call(
        body, grid=(s // tm,),
        in_specs=[pl.BlockSpec((tm, c), row), pl.BlockSpec((CONV_HALO, c), _next_halo(tm, CONV_HALO, s)),
                  pl.BlockSpec((tm, c), row), pl.BlockSpec((CONV_HALO, c), _prev_halo(tm, CONV_HALO)),
                  pl.BlockSpec((tm, 2 * c), row), pl.BlockSpec((1, 2 * c), fix), pl.BlockSpec((32, c), fix)],
        out_specs=[pl.BlockSpec((tm, 2 * c), row), pl.BlockSpec((32, c), fix), pl.BlockSpec((1, 2 * c), fix)],
        out_shape=[SDS((s, 2 * c), BF16), SDS((32, c), F32), SDS((1, 2 * c), F32)],
        scratch_shapes=[pltpu.VMEM((8, tm + CONV_HALO, c), F32), pltpu.VMEM((8, tm + CONV_HALO, c), F32), pltpu.VMEM((tm, c), F32)],
        compiler_params=_cp(1), name="conv_bwd_dw")(d_u1, d_u1, u0, u0, z, b_in, w32)


def _partner(v, lane):
    up = pltpu.roll(v, HEAD_PAD - ROPE // 2, 1)
    dn = pltpu.roll(v, ROPE // 2, 1)
    lo = (lane >= NOPE) & (lane < NOPE + ROPE // 2)
    hi = (lane >= NOPE + ROPE // 2) & (lane < HEAD_DIM)
    return jnp.where(lo, up, jnp.where(hi, dn, 0.0))


def _mla_prep(q_raw, kv_raw, z, cosf, sinf, gq, gk, tm):
    s = q_raw.shape[0]

    def body(q_ref, kv_ref, kr_ref, c_ref, s_ref, gq_ref, gk_ref, qo_ref, ko_ref, vo_ref):
        lane = lax.broadcasted_iota(jnp.int32, (tm, HEAD_PAD), 1)
        cf = c_ref[...]
        sf = s_ref[...]

        def norm_rope(t, g_ref):
            r = lax.rsqrt(jnp.sum(t * t, axis=-1, keepdims=True) * (1.0 / HEAD_DIM) + EPS)
            tn = (t * r) * g_ref[...]
            return tn * cf + _partner(tn, lane) * sf

        kr = kr_ref[...]
        for h in range(HEADS):
            hs = slice(h * HEAD_PAD, (h + 1) * HEAD_PAD)
            qo_ref[:, hs] = (norm_rope(q_ref[:, hs], gq_ref) * (ATT_SCALE * LOG2E)).astype(BF16)
            kv = kv_ref[:, hs]
            ko_ref[:, hs] = norm_rope(jnp.where(lane < NOPE, kv, 0.0) + kr, gk_ref).astype(BF16)
            vo_ref[:, hs] = jnp.where(lane >= NOPE, kv, 0.0).astype(BF16)

    row = lambda i: (i, 0)
    wide = pl.BlockSpec((tm, HEADS * HEAD_PAD), row)
    one = pl.BlockSpec((tm, HEAD_PAD), row)
    return pl.pallas_call(
        body, grid=(s // tm,),
        in_specs=[wide, wide, pl.BlockSpec((tm, HEAD_PAD), lambda i: (i, IN_COLS_PAD // HEAD_PAD - 1)), one, one,
                  pl.BlockSpec((1, HEAD_PAD), lambda i: (0, 0)), pl.BlockSpec((1, HEAD_PAD), lambda i: (0, 0))],
        out_specs=[wide] * 3,
        out_shape=[SDS((s, HEADS * HEAD_PAD), BF16)] * 3,
        compiler_params=_cp(1), name="mla_prep")(q_raw, kv_raw, z, cosf, sinf, gq, gk)


def _mla_prep_bwd(dqp, dkp, dvp, q_raw, kv_raw, z, cosf, sinf, gq, gk, tm):
    s = q_raw.shape[0]

    def body(dq_ref, dk_ref, dv_ref, q_ref, kv_ref, kr_ref, c_ref, s_ref, gq_ref, gk_ref,
             dqo_ref, dkvo_ref, dkr_ref, dgq_ref, dgk_ref):
        lane = lax.broadcasted_iota(jnp.int32, (tm, HEAD_PAD), 1)
        cf = c_ref[...]
        sf = s_ref[...]

        @pl.when(pl.program_id(0) == 0)
        def _():
            dgq_ref[...] = jnp.zeros_like(dgq_ref)
            dgk_ref[...] = jnp.zeros_like(dgk_ref)

        def norm_rope_bwd(t, d_out, g_ref, dg_ref):
            r = lax.rsqrt(jnp.sum(t * t, axis=-1, keepdims=True) * (1.0 / HEAD_DIM) + EPS)
            th = t * r
            dn = d_out * cf + _partner(d_out * sf, lane)
            dg_ref[...] += jnp.sum(dn * th, axis=0, keepdims=True)
            dh = dn * g_ref[...]
            return r * (dh - th * (jnp.sum(dh * th, axis=-1, keepdims=True) * (1.0 / HEAD_DIM)))

        kr = kr_ref[...]
        dkr = None
        for h in range(HEADS):
            hs = slice(h * HEAD_PAD, (h + 1) * HEAD_PAD)
            dq = norm_rope_bwd(q_ref[:, hs], dq_ref[:, hs] * ATT_SCALE, gq_ref, dgq_ref)
            dqo_ref[:, hs] = dq.astype(BF16)
            kv = kv_ref[:, hs]
            dkpre = norm_rope_bwd(jnp.where(lane < NOPE, kv, 0.0) + kr, dk_ref[:, hs] * LN2, gk_ref, dgk_ref)
            dkvo_ref[:, hs] = jnp.where(lane < NOPE, dkpre, dv_ref[:, hs]).astype(BF16)
            dkr_h = jnp.where((lane >= NOPE) & (lane < HEAD_DIM), dkpre, 0.0)
            dkr = dkr_h if dkr is None else dkr + dkr_h
        dkr_ref[...] = dkr

    row = lambda i: (i, 0)
    fix = lambda i: (0, 0)
    wide = pl.BlockSpec((tm, HEADS * HEAD_PAD), row)
    one = pl.BlockSpec((tm, HEAD_PAD), row)
    return pl.pallas_call(
        body, grid=(s // tm,),
        in_specs=[wide, wide, wide, wide, wide, pl.BlockSpec((tm, HEAD_PAD), lambda i: (i, IN_COLS_PAD // HEAD_PAD - 1)),
                  one, one, pl.BlockSpec((1, HEAD_PAD), fix), pl.BlockSpec((1, HEAD_PAD), fix)],
        out_specs=[wide, wide, one, pl.BlockSpec((1, HEAD_PAD), fix), pl.BlockSpec((1, HEAD_PAD), fix)],
        out_shape=[SDS((s, HEADS * HEAD_PAD), BF16), SDS((s, HEADS * HEAD_PAD), BF16), SDS((s, HEAD_PAD), F32),
                   SDS((1, HEAD_PAD), F32), SDS((1, HEAD_PAD), F32)],
        compiler_params=_cp(1), name="mla_prep_bwd")(dqp, dkp, dvp, q_raw, kv_raw, z, cosf, sinf, gq, gk)


def _pair_schedule(nb, forward):
    one, two, case = [], [], []
    for a in range(nb):
        for b in (range(a // 2 + 1) if forward else range(a // 2, nb // 2)):
            one.append(a)
            two.append(b)
            case.append(0 if b != a // 2 else 1 + a % 2)
    return tuple(jnp.asarray(np.array(x, np.int32)) for x in (one, two, case))


STRIP = 64


def _fold8(x):
    acc = x[0:8, :]
    for g in range(1, x.shape[0] // 8):
        acc = acc + x[g * 8:(g + 1) * 8, :]
    return acc


def _attn_fwd(qp, kp, vp, tb, w_late):
    s = qp.shape[0]
    ii, jj, cc = _pair_schedule(s // tb, True)
    n_steps = int(ii.shape[0])

    def body(ii_ref, jj_ref, cc_ref, q_ref, k_ref, v_ref, wl_ref, of_ref, ob_ref, lse_ref, gl_ref,
             m_sc, l_sc, acc_sc, st_sc, pt_sc, send_sems, recv_sems):
        t = pl.program_id(1)
        case = cc_ref[t]

        @pl.when((pl.program_id(0) == 0) & (t == 0))
        def _():
            _ag_send(wl_ref, gl_ref, send_sems, recv_sems)

        @pl.when(jj_ref[t] == 0)
        def _():
            m_sc[...] = jnp.full_like(m_sc, -jnp.inf)
            l_sc[...] = jnp.zeros_like(l_sc)
            acc_sc[...] = jnp.zeros_like(acc_sc)

        def step(n_keys, diag_at):
            def visible(r):
                if diag_at is None or r * STRIP <= diag_at:
                    return None
                col = lax.broadcasted_iota(jnp.int32, (STRIP, tb), 1)
                return col >= r * STRIP - diag_at

            st_sc[0:n_keys, :] = lax.dot_general(k_ref[0:n_keys, :], q_ref[...], (((1,), (1,)), ((), ())), preferred_element_type=F32)
            mx = None
            for r in range(n_keys // STRIP):
                sc = st_sc[r * STRIP:(r + 1) * STRIP, :]
                if visible(r) is not None:
                    sc = jnp.where(visible(r), sc, -jnp.inf)
                m8 = sc[0:8, :]
                for g in range(1, STRIP // 8):
                    m8 = jnp.maximum(m8, sc[g * 8:(g + 1) * 8, :])
                mx = m8 if mx is None else jnp.maximum(mx, m8)
            m_old = m_sc[0:1, :]
            m_new = jnp.maximum(m_old, jnp.max(mx, axis=0, keepdims=True))
            alpha = jnp.exp2(m_old - m_new)
            ps = None
            pv = None
            for b in range(n_keys // tb):
                for r in range(b * tb // STRIP, (b + 1) * tb // STRIP):
                    p = jnp.exp2(st_sc[r * STRIP:(r + 1) * STRIP, :] - m_new)
                    if visible(r) is not None:
                        p = jnp.where(visible(r), p, 0.0)
                    ps = _fold8(p) if ps is None else ps + _fold8(p)
                    pt_sc[r * STRIP:(r + 1) * STRIP, :] = p.astype(BF16)
                pvb = lax.dot_general(v_ref[b * tb:(b + 1) * tb, :], pt_sc[b * tb:(b + 1) * tb, :], (((0,), (0,)), ((), ())),
                                      preferred_element_type=F32)
                pv = pvb if pv is None else pv + pvb
            l_new = alpha * l_sc[0:1, :] + jnp.sum(ps, axis=0, keepdims=True)
            m_sc[...] = jnp.broadcast_to(m_new, m_sc.shape)
            l_sc[...] = jnp.broadcast_to(l_new, l_sc.shape)
            acc_sc[...] = alpha * acc_sc[...] + pv

        @pl.when(case == 0)
        def _():
            step(2 * tb, None)

        @pl.when(case == 1)
        def _():
            step(tb, 0)

        @pl.when(case == 2)
        def _():
            step(2 * tb, tb)

        @pl.when(case != 0)
        def _():
            l = l_sc[0:1, :]
            o = (acc_sc[...] / l).T
            of_ref[...] = o
            ob_ref[...] = o.astype(BF16)
            lse_ref[...] = m_sc[0:1, :] + jnp.log(l) * LOG2E

        @pl.when((pl.program_id(0) == HEADS - 1) & (t == n_steps - 1))
        def _():
            _ag_finish(wl_ref, gl_ref, send_sems, recv_sems)

    qmap = lambda h, t, ii_ref, jj_ref, cc_ref: (ii_ref[t], h)
    kmap = lambda h, t, ii_ref, jj_ref, cc_ref: (jj_ref[t], h)
    gs = pltpu.PrefetchScalarGridSpec(
        num_scalar_prefetch=3, grid=(HEADS, n_steps),
        in_specs=[pl.BlockSpec((tb, HEAD_PAD), qmap), pl.BlockSpec((2 * tb, HEAD_PAD), kmap), pl.BlockSpec((2 * tb, HEAD_PAD), kmap), ANY],
        out_specs=[pl.BlockSpec((tb, HEAD_PAD), qmap), pl.BlockSpec((tb, HEAD_PAD), qmap),
                   pl.BlockSpec((None, 1, tb), lambda h, t, ii_ref, jj_ref, cc_ref: (h, 0, ii_ref[t])), ANY],
        scratch_shapes=[pltpu.VMEM((8, tb), F32), pltpu.VMEM((8, tb), F32), pltpu.VMEM((HEAD_PAD, tb), F32),
                        pltpu.VMEM((2 * tb, tb), F32), pltpu.VMEM((2 * tb, tb), BF16),
                        pltpu.SemaphoreType.DMA((6,)), pltpu.SemaphoreType.DMA((6,))])
    w = HEADS * HEAD_PAD
    return pl.pallas_call(
        body, grid_spec=gs,
        out_shape=[SDS((s, w), F32), SDS((s, w), BF16), SDS((HEADS, 1, s), F32), SDS((4,) + w_late.shape, w_late.dtype)],
        compiler_params=_cp(2), name="attn_fwd")(ii, jj, cc, qp, kp, vp, w_late)


def _attn_delta(do, o, tb):
    s = do.shape[0]

    def body(do_ref, o_ref, d_ref):
        for h in range(HEADS):
            hs = slice(h * HEAD_PAD, (h + 1) * HEAD_PAD)
            d_ref[h] = jnp.sum((do_ref[:, hs] * o_ref[:, hs]).T, axis=0, keepdims=True)

    blk = pl.BlockSpec((tb, HEADS * HEAD_PAD), lambda i: (i, 0))
    return pl.pallas_call(body, grid=(s // tb,), in_specs=[blk, blk],
                          out_specs=pl.BlockSpec((HEADS, 1, tb), lambda i: (0, 0, i)),
                          out_shape=SDS((HEADS, 1, s), F32), compiler_params=_cp(1), name="attn_delta")(do, o)


def _attn_bwd(qp, kp, vp, dob, lse, delta, tb, wire):
    s = qp.shape[0]
    nb = s // tb
    jj, ii, cc = _pair_schedule(nb, False)
    rows = 32

    n_steps = int(ii.shape[0])

    def body(jj_ref, ii_ref, cc_ref, q_ref, k_ref, v_ref, do_ref, lse_ref, dl_ref, cw_ref, dq_ref, dk_ref, dv_ref, rcv_ref,
             st_sc, dpt_sc, pt_sc, dst_sc, send_sems, recv_sems):
        t = pl.program_id(1)
        case = cc_ref[t]
        pair = ii_ref[t]

        @pl.when((pl.program_id(0) == 0) & (t == 0))
        def _():
            _rs_send(cw_ref, rcv_ref, send_sems, recv_sems)

        @pl.when((pl.program_id(0) == HEADS - 1) & (t == n_steps - 1))
        def _():
            _rs_wait(rcv_ref, send_sems, recv_sems)

        @pl.when(t == 0)
        def _():
            dq_ref[...] = jnp.zeros_like(dq_ref)

        @pl.when(case != 0)
        def _():
            dk_ref[...] = jnp.zeros_like(dk_ref)
            dv_ref[...] = jnp.zeros_like(dv_ref)

        def step(lo, width, diag):
            q = q_ref[lo:lo + width, :]
            do = do_ref[lo:lo + width, :]
            k = k_ref[...]
            st_sc[:, 0:width] = lax.dot_general(k, q, (((1,), (1,)), ((), ())), preferred_element_type=F32)
            dpt_sc[:, 0:width] = lax.dot_general(v_ref[...], do, (((1,), (1,)), ((), ())), preferred_element_type=F32)
            dv = dk = None
            for h in range(width // tb):
                ls = slice(h * tb, (h + 1) * tb)
                lse_row = lse_ref[:, lo + h * tb:lo + (h + 1) * tb]
                dl_row = dl_ref[:, lo + h * tb:lo + (h + 1) * tb]
                for r in range(tb // rows):
                    rs = slice(r * rows, (r + 1) * rows)
                    p = jnp.exp2(st_sc[rs, ls] - lse_row)
                    first_visible = (r * rows) // CHUNK * CHUNK
                    if diag and h == 0 and first_visible > 0:
                        col = lax.broadcasted_iota(jnp.int32, (rows, tb), 1)
                        p = jnp.where(col >= first_visible, p, 0.0)
                    ds = p * (dpt_sc[rs, ls] - dl_row)
                    pt_sc[rs, ls] = p.astype(BF16)
                    dst_sc[rs, ls] = ds.astype(BF16)
                dst = dst_sc[:, ls]
                dvh = jnp.dot(pt_sc[:, ls], do[ls, :], preferred_element_type=F32)
                dkh = jnp.dot(dst, q[ls, :], preferred_element_type=F32)
                dv = dvh if dv is None else dv + dvh
                dk = dkh if dk is None else dk + dkh
                dq_ref[2 * pair + lo // tb + h] += lax.dot_general(k, dst, (((0,), (0,)), ((), ())), preferred_element_type=F32)
            dv_ref[...] += dv
            dk_ref[...] += dk

        @pl.when(case == 0)
        def _():
            step(0, 2 * tb, False)

        @pl.when(case == 1)
        def _():
            step(0, 2 * tb, True)

        @pl.when(case == 2)
        def _():
            step(tb, tb, True)

    qmap = lambda h, t, jj_ref, ii_ref, cc_ref: (ii_ref[t], h)
    kmap = lambda h, t, jj_ref, ii_ref, cc_ref: (jj_ref[t], h)
    rowmap = lambda h, t, jj_ref, ii_ref, cc_ref: (h, 0, ii_ref[t])
    gs = pltpu.PrefetchScalarGridSpec(
        num_scalar_prefetch=3, grid=(HEADS, n_steps),
        in_specs=[pl.BlockSpec((2 * tb, HEAD_PAD), qmap), pl.BlockSpec((tb, HEAD_PAD), kmap), pl.BlockSpec((tb, HEAD_PAD), kmap),
                  pl.BlockSpec((2 * tb, HEAD_PAD), qmap), pl.BlockSpec((None, 1, 2 * tb), rowmap), pl.BlockSpec((None, 1, 2 * tb), rowmap), ANY],
        out_specs=[pl.BlockSpec((None, nb, HEAD_PAD, tb), lambda h, t, jj_ref, ii_ref, cc_ref: (h, 0, 0, 0)),
                   pl.BlockSpec((tb, HEAD_PAD), kmap), pl.BlockSpec((tb, HEAD_PAD), kmap), ANY],
        scratch_shapes=[pltpu.VMEM((tb, 2 * tb), F32), pltpu.VMEM((tb, 2 * tb), F32), pltpu.VMEM((tb, 2 * tb), BF16),
                        pltpu.VMEM((tb, 2 * tb), BF16), pltpu.SemaphoreType.DMA((3,)), pltpu.SemaphoreType.DMA((3,))])
    w = HEADS * HEAD_PAD
    dqt, dk, dv, recv = pl.pallas_call(
        body, grid_spec=gs,
        out_shape=[SDS((HEADS, nb, HEAD_PAD, tb), F32), SDS((s, w), F32), SDS((s, w), F32), SDS((3,) + wire.shape[1:], wire.dtype)],
        compiler_params=_cp(2), name="attn_bwd")(jj, ii, cc, qp, kp, vp, dob, lse, delta, wire)
    return jnp.transpose(dqt, (1, 3, 0, 2)).reshape(s, w), dk, dv, recv


def _head_norm(t, g):
    r = lax.rsqrt(jnp.mean(t * t, axis=-1, keepdims=True) + EPS)
    th = t * r
    return r, th, th * g


def _softmax_rows(sc):
    m = jnp.max(sc, axis=-1, keepdims=True)
    e = jnp.exp(sc - m)
    return e / jnp.sum(e, axis=-1, keepdims=True)


def _memattn_fwd(qm, kvm, gq, gk, tm):
    s = qm.shape[0]
    hd = MEM_HEAD_DIM

    def body(q_ref, k_ref, v_ref, gq_ref, gk_ref, o_ref):
        _, _, qn = _head_norm(q_ref[...], gq_ref[...])
        _, _, kn = _head_norm(k_ref[...], gk_ref[...])
        sc = lax.dot_general(qn.astype(BF16), kn.astype(BF16), (((1,), (1,)), ((), ())), preferred_element_type=F32)
        p = _softmax_rows(sc * (1.0 / math.sqrt(hd)))
        o_ref[...] = jnp.dot(p.astype(BF16), v_ref[...].astype(BF16), preferred_element_type=F32).astype(BF16)

    fix = lambda i, h: (0, 0)
    return pl.pallas_call(
        body, grid=(s // tm, MEM_HEADS),
        in_specs=[pl.BlockSpec((tm, hd), lambda i, h: (i, h)), pl.BlockSpec((MEM_LEN, hd), lambda i, h: (0, h)),
                  pl.BlockSpec((MEM_LEN, hd), lambda i, h: (0, MEM_HEADS + h)), pl.BlockSpec((1, hd), fix), pl.BlockSpec((1, hd), fix)],
        out_specs=pl.BlockSpec((tm, hd), lambda i, h: (i, h)),
        out_shape=SDS((s, MEM_HEADS * hd), BF16), compiler_params=_cp(2), name="memattn_fwd")(qm, kvm, kvm, gq, gk)


def _memattn_bwd(qm, kvm, d_o, gq, gk, tm):
    s = qm.shape[0]
    hd = MEM_HEAD_DIM

    def body(q_ref, k_ref, v_ref, do_ref, gq_ref, gk_ref, dq_ref, dk_ref, dv_ref, dgq_ref, dgk_ref, dkn_sc):
        h = pl.program_id(0)
        i = pl.program_id(1)
        last = pl.num_programs(1) - 1

        @pl.when((h == 0) & (i == 0))
        def _():
            dgq_ref[...] = jnp.zeros_like(dgq_ref)
            dgk_ref[...] = jnp.zeros_like(dgk_ref)

        @pl.when(i == 0)
        def _():
            dv_ref[...] = jnp.zeros_like(dv_ref)
            dkn_sc[...] = jnp.zeros_like(dkn_sc)

        rq, qh, qn = _head_norm(q_ref[...], gq_ref[...])
        rk, kh, kn = _head_norm(k_ref[...], gk_ref[...])
        qnb = qn.astype(BF16)
        knb = kn.astype(BF16)
        scale = 1.0 / math.sqrt(hd)
        sc = lax.dot_general(qnb, knb, (((1,), (1,)), ((), ())), preferred_element_type=F32)
        p = _softmax_rows(sc * scale)
        do = do_ref[...].astype(BF16)
        dp = lax.dot_general(do, v_ref[...].astype(BF16), (((1,), (1,)), ((), ())), preferred_element_type=F32)
        dv_ref[...] += lax.dot_general(p.astype(BF16), do, (((0,), (0,)), ((), ())), preferred_element_type=F32)
        ds = ((p * (dp - jnp.sum(dp * p, axis=-1, keepdims=True))) * scale).astype(BF16)
        dqn = jnp.dot(ds, knb, preferred_element_type=F32)
        dkn_sc[...] += lax.dot_general(ds, qnb, (((0,), (0,)), ((), ())), preferred_element_type=F32)
        dgq_ref[...] += jnp.sum(dqn * qh, axis=0, keepdims=True)
        dqh = dqn * gq_ref[...]
        dq_ref[...] = (rq * (dqh - qh * jnp.mean(dqh * qh, axis=-1, keepdims=True))).astype(BF16)

        @pl.when(i == last)
        def _():
            dkn = dkn_sc[...]
            dgk_ref[...] += jnp.sum(dkn * kh, axis=0, keepdims=True)
            dkh = dkn * gk_ref[...]
            dk_ref[...] = rk * (dkh - kh * jnp.mean(dkh * kh, axis=-1, keepdims=True))

    fix = lambda h, i: (0, 0)
    qb = pl.BlockSpec((tm, hd), lambda h, i: (i, h))
    kb = pl.BlockSpec((MEM_LEN, hd), lambda h, i: (0, h))
    return pl.pallas_call(
        body, grid=(MEM_HEADS, s // tm),
        in_specs=[qb, kb, pl.BlockSpec((MEM_LEN, hd), lambda h, i: (0, MEM_HEADS + h)), qb,
                  pl.BlockSpec((1, hd), fix), pl.BlockSpec((1, hd), fix)],
        out_specs=[qb, kb, kb, pl.BlockSpec((1, hd), fix), pl.BlockSpec((1, hd), fix)],
        out_shape=[SDS((s, MEM_HEADS * hd), BF16), SDS((MEM_LEN, MEM_HEADS * hd), F32), SDS((MEM_LEN, MEM_HEADS * hd), F32),
                   SDS((1, hd), F32), SDS((1, hd), F32)],
        scratch_shapes=[pltpu.VMEM((MEM_LEN, hd), F32)],
        compiler_params=_cp(2), name="memattn_bwd")(qm, kvm, kvm, d_o, gq, gk)


def _ffn_specs(tm, tn, nbj, s, order_ji):
    if order_ji:
        ij = lambda f: (lambda j, i: f(i, j))
    else:
        ij = lambda f: f
    prev = lambda i: jnp.maximum(i * (tm // FFN_HALO) - 1, 0)
    cur_g = pl.BlockSpec((tm, tn), ij(lambda i, j: (i, j)))
    cur_v = pl.BlockSpec((tm, tn), ij(lambda i, j: (i, j + nbj)))
    halo_g = pl.BlockSpec((FFN_HALO, tn), ij(lambda i, j: (prev(i), j)))
    halo_v = pl.BlockSpec((FFN_HALO, tn), ij(lambda i, j: (prev(i), j + nbj)))
    w_g = pl.BlockSpec((8, tn), ij(lambda i, j: (0, j)))
    w_v = pl.BlockSpec((8, tn), ij(lambda i, j: (0, j + nbj)))
    b_g = pl.BlockSpec((1, tn), ij(lambda i, j: (0, j)))
    b_v = pl.BlockSpec((1, tn), ij(lambda i, j: (0, j + nbj)))
    return cur_g, cur_v, halo_g, halo_v, w_g, w_v, b_g, b_v


FFN_STRIP = 16


def _conv3_rows(ext, w_ref, b_ref, o, n):
    return (w_ref[0:1, :] * ext[FFN_HALO - 2 + o:FFN_HALO - 2 + o + n, :] + w_ref[1:2, :] * ext[FFN_HALO - 1 + o:FFN_HALO - 1 + o + n, :]
            + w_ref[2:3, :] * ext[FFN_HALO + o:FFN_HALO + o + n, :] + b_ref[...])


def _ffn_fwd(up0, w8, b, tm, tn):
    s = up0.shape[0]
    nbj = D_FF // tn

    def body(g_ref, v_ref, gh_ref, vh_ref, wg_ref, wv_ref, bg_ref, bv_ref, act_ref, extg, extv):
        first = pl.program_id(0) == 0
        for ext, h_ref, c_ref in ((extg, gh_ref, g_ref), (extv, vh_ref, v_ref)):
            ext[0:FFN_HALO, :] = jnp.where(first, 0.0, h_ref[...])
            ext[FFN_HALO:, :] = c_ref[...]
        for r in range(tm // FFN_STRIP):
            o = r * FFN_STRIP
            ug = _conv3_rows(extg, wg_ref, bg_ref, o, FFN_STRIP)
            uv = _conv3_rows(extv, wv_ref, bv_ref, o, FFN_STRIP)
            act_ref[o:o + FFN_STRIP, :] = ((ug * jax.nn.sigmoid(ug)) * uv).astype(BF16)

    specs = _ffn_specs(tm, tn, nbj, s, False)
    return pl.pallas_call(
        body, grid=(s // tm, nbj), in_specs=list(specs),
        out_specs=pl.BlockSpec((tm, tn), lambda i, j: (i, j)), out_shape=SDS((s, D_FF), BF16),
        scratch_shapes=[pltpu.VMEM((tm + FFN_HALO, tn), F32), pltpu.VMEM((tm + FFN_HALO, tn), F32)],
        compiler_params=_cp(2), name="ffn_fwd")(up0, up0, up0, up0, w8, w8, b, b)


def _ffn_bwd(d_act, up0, w8, b, tm, tn):
    s = up0.shape[0]
    nbj = D_FF // tn
    te = tm + FFN_HALO

    def body(da_ref, dan_ref, g_ref, v_ref, gh_ref, vh_ref, gn_ref, vn_ref, wg_ref, wv_ref, bg_ref, bv_ref,
             og_ref, ov_ref, dbg_ref, dbv_ref, dwg_ref, dwv_ref, extg, extv, extdg, extdv, accg, accv):
        i = pl.program_id(1)
        first = i == 0
        last = i == pl.num_programs(1) - 1

        @pl.when(first)
        def _():
            for r in (dbg_ref, dbv_ref, dwg_ref, dwv_ref):
                r[...] = jnp.zeros_like(r)

        for ext, h_ref, c_ref, n_ref in ((extg, gh_ref, g_ref, gn_ref), (extv, vh_ref, v_ref, vn_ref)):
            ext[0:FFN_HALO, :] = jnp.where(first, 0.0, h_ref[...])
            ext[FFN_HALO:FFN_HALO + tm, :] = c_ref[...]
            ext[FFN_HALO + tm:, :] = n_ref[...]

        def fold8(x):
            acc = x[0:8, :]
            for q in range(1, x.shape[0] // 8):
                acc = acc + x[q * 8:(q + 1) * 8, :]
            return acc

        def taps(ext, o, n):
            return [ext[FFN_HALO - 2 + k + o:FFN_HALO - 2 + k + o + n, :] for k in range(3)]

        accg[...] = jnp.zeros_like(accg)
        accv[...] = jnp.zeros_like(accv)

        def gate_bwd(o, n, da, own_rows):
            xg, xv = taps(extg, o, n), taps(extv, o, n)
            ug = wg_ref[0:1, :] * xg[0] + wg_ref[1:2, :] * xg[1] + wg_ref[2:3, :] * xg[2] + bg_ref[...]
            uv = wv_ref[0:1, :] * xv[0] + wv_ref[1:2, :] * xv[1] + wv_ref[2:3, :] * xv[2] + bv_ref[...]
            sg = jax.nn.sigmoid(ug)
            dgt = da * uv * (sg * (1.0 + ug * (1.0 - sg)))
            dvl = da * (ug * sg)
            extdg[o:o + n, :] = dgt
            extdv[o:o + n, :] = dvl
            if own_rows:
                for acc, d, x in ((accg, dgt, xg), (accv, dvl, xv)):
                    acc[0] += fold8(d)
                    for k in range(3):
                        acc[1 + k] += fold8(d * x[k])

        for r in range(tm // FFN_STRIP):
            gate_bwd(r * FFN_STRIP, FFN_STRIP, da_ref[r * FFN_STRIP:(r + 1) * FFN_STRIP, :], True)
        gate_bwd(tm, FFN_HALO, jnp.where(last, 0.0, dan_ref[...]), False)

        for extd, w_ref, o_ref, db_ref, dw_ref, acc in ((extdg, wg_ref, og_ref, dbg_ref, dwg_ref, accg),
                                                        (extdv, wv_ref, ov_ref, dbv_ref, dwv_ref, accv)):
            for r in range(tm // FFN_STRIP):
                o = r * FFN_STRIP
                o_ref[o:o + FFN_STRIP, :] = (w_ref[2:3, :] * extd[o:o + FFN_STRIP, :] + w_ref[1:2, :] * extd[o + 1:o + 1 + FFN_STRIP, :]
                                             + w_ref[0:1, :] * extd[o + 2:o + 2 + FFN_STRIP, :]).astype(BF16)
            db_ref[...] += jnp.sum(acc[0], axis=0, keepdims=True)
            for k in range(3):
                dw_ref[k:k + 1, :] += jnp.sum(acc[1 + k], axis=0, keepdims=True)

    cur_g, cur_v, halo_g, halo_v, w_g, w_v, b_g, b_v = _ffn_specs(tm, tn, nbj, s, True)
    nxt_row = lambda i: jnp.minimum((i + 1) * (tm // FFN_HALO), s // FFN_HALO - 1)
    cur = pl.BlockSpec((tm, tn), lambda j, i: (i, j))
    nxt = pl.BlockSpec((FFN_HALO, tn), lambda j, i: (nxt_row(i), j))
    nxt_v = pl.BlockSpec((FFN_HALO, tn), lambda j, i: (nxt_row(i), j + nbj))
    acc1 = pl.BlockSpec((1, tn), lambda j, i: (0, j))
    acc8 = pl.BlockSpec((8, tn), lambda j, i: (0, j))
    return pl.pallas_call(
        body, grid=(nbj, s // tm), in_specs=[cur, nxt, cur_g, cur_v, halo_g, halo_v, nxt, nxt_v, w_g, w_v, b_g, b_v],
        out_specs=[cur, cur, acc1, acc1, acc8, acc8],
        out_shape=[SDS((s, D_FF), BF16), SDS((s, D_FF), BF16), SDS((1, D_FF), F32), SDS((1, D_FF), F32),
                   SDS((8, D_FF), F32), SDS((8, D_FF), F32)],
        scratch_shapes=[pltpu.VMEM((tm + 2 * FFN_HALO, tn), F32), pltpu.VMEM((tm + 2 * FFN_HALO, tn), F32),
                        pltpu.VMEM((te, tn), F32), pltpu.VMEM((te, tn), F32),
                        pltpu.VMEM((4, 8, tn), F32), pltpu.VMEM((4, 8, tn), F32)],
        compiler_params=_cp(2), name="ffn_bwd")(d_act, d_act, up0, up0, up0, up0, up0, up0, w8, w8, b, b)


def _down_loss(act, w_down, x2, target, tm):
    s = act.shape[0]

    def body(a_ref, w_ref, x_ref, t_ref, dyf_ref, dyb_ref, ls_ref):
        @pl.when(pl.program_id(0) == 0)
        def _():
            ls_ref[...] = jnp.zeros_like(ls_ref)

        y = x_ref[...] + jnp.dot(a_ref[...], w_ref[...], preferred_element_type=F32)
        e = y - t_ref[...]
        ls_ref[...] += jnp.sum(e * e)
        dy = e * (1.0 / D_MODEL)
        dyf_ref[...] = dy
        dyb_ref[...] = dy.astype(BF16)

    row = lambda i: (i, 0)
    return pl.pallas_call(
        body, grid=(s // tm,),
        in_specs=[pl.BlockSpec((tm, D_FF), row), pl.BlockSpec((D_FF, D_MODEL), lambda i: (0, 0)),
                  pl.BlockSpec((tm, D_MODEL), row), pl.BlockSpec((tm, D_MODEL), row)],
        out_specs=[pl.BlockSpec((tm, D_MODEL), row), pl.BlockSpec((tm, D_MODEL), row), pl.BlockSpec((8, 128), lambda i: (0, 0))],
        out_shape=[SDS((s, D_MODEL), F32), SDS((s, D_MODEL), BF16), SDS((8, 128), F32)],
        compiler_params=_cp(1), name="down_loss")(act, w_down, x2, target)


def _adamw_math(w, g, m, v):
    mn = ADAM_B1 * m + (1.0 - ADAM_B1) * g
    vn = ADAM_B2 * v + (1.0 - ADAM_B2) * (g * g)
    m_hat = mn / (1.0 - ADAM_B1 ** ADAM_STEP)
    v_hat = vn / (1.0 - ADAM_B2 ** ADAM_STEP)
    return -ADAM_LR * (m_hat / (jnp.sqrt(v_hat) + ADAM_EPS) + ADAM_WD * w), mn, vn


def _adamw(w, g, m, v, name):
    rows, cols = w.shape
    tr = rows if rows <= 256 else (256 if rows % 256 == 0 else rows // 2)

    def body(w_ref, g_ref, m_ref, v_ref, d_ref, mo_ref, vo_ref):
        d_ref[...], mo_ref[...], vo_ref[...] = _adamw_math(w_ref[...], g_ref[...], m_ref[...], v_ref[...])

    blk = pl.BlockSpec((tr, cols), lambda i: (i, 0))
    return pl.pallas_call(body, grid=(rows // tr,), in_specs=[blk] * 4, out_specs=[blk] * 3,
                          out_shape=[SDS((rows, cols), F32)] * 3, compiler_params=_cp(1), name=name)(w, g, m, v)


def _adamw_small(ws, gs, ms, vs):
    n = len(ws)

    def body(*refs):
        ins, outs = refs[:4 * n], refs[4 * n:]
        for k in range(n):
            d, mn, vn = _adamw_math(ins[k][...], ins[n + k][...], ins[2 * n + k][...], ins[3 * n + k][...])
            outs[k][...] = d
            outs[n + k][...] = mn
            outs[2 * n + k][...] = vn

    vm = pl.BlockSpec(memory_space=pltpu.VMEM)
    outs = pl.pallas_call(body, in_specs=[vm] * (4 * n), out_specs=[vm] * (3 * n),
                          out_shape=[SDS(w.shape, F32) for w in ws] * 3, name="adamw_small")(*ws, *gs, *ms, *vs)
    return outs[:n], outs[n:2 * n], outs[2 * n:]


ANY = pl.BlockSpec(memory_space=pl.ANY)


def _coords():
    return lax.axis_index("x"), lax.axis_index("y"), lax.axis_index("c")


def _other_chips(x, y):
    return [(1 - x, y), (x, 1 - y), (1 - x, 1 - y)]


D2D_CHUNKS = 8
ICI_CHUNKS = 4


def _row_chunks(n_rows, n_chunks, align):
    step = -(-n_rows // (n_chunks * align)) * align
    return [(r, min(step, n_rows - r)) for r in range(0, n_rows, step)]


def _ag_copy(out_ref, send_sems, recv_sems, k, shard, base, r0, nr, to, src=None):
    rows_ = pl.ds(pl.multiple_of(base + r0, 16), nr)
    dst = out_ref.at[shard, rows_]
    return pltpu.make_async_remote_copy(src_ref=dst if src is None else src.at[rows_], dst_ref=dst, send_sem=send_sems.at[k],
                                        recv_sem=recv_sems.at[k], device_id=to, device_id_type=MESH)


def _ag_send(w_ref, out_ref, send_sems, recv_sems):
    x, y, c = _coords()
    half_rows = w_ref.shape[0] // 2
    for k, (px, py) in enumerate(_other_chips(x, y)):
        for r0, nr in _row_chunks(half_rows, ICI_CHUNKS, 16):
            _ag_copy(out_ref, send_sems, recv_sems, k, 2 * x + y, c * half_rows, r0, nr, (px, py, c), src=w_ref).start()


def _ag_finish(w_ref, out_ref, send_sems, recv_sems):
    x, y, c = _coords()
    half_rows = w_ref.shape[0] // 2
    chips = _other_chips(x, y)
    sibling = (x, y, 1 - c)
    for k, (px, py) in enumerate(chips):
        _ag_copy(out_ref, send_sems, recv_sems, k, 2 * px + py, c * half_rows, 0, half_rows, (px, py, c)).wait_recv()
        for r0, nr in _row_chunks(half_rows, ICI_CHUNKS, 16):
            _ag_copy(out_ref, send_sems, recv_sems, 3 + k, 2 * px + py, c * half_rows, r0, nr, sibling).start()
    for k, (px, py) in enumerate(chips):
        _ag_copy(out_ref, send_sems, recv_sems, 3 + k, 2 * px + py, (1 - c) * half_rows, 0, half_rows, sibling).wait_recv()
    for k in range(6):
        _ag_copy(out_ref, send_sems, recv_sems, k, 2 * x + y, c * half_rows, 0, half_rows, sibling).wait_send()


def _ag_weights(wsh):
    rows, cols = wsh.shape

    def body(w_ref, out_ref, send_sems, recv_sems):
        _ag_send(w_ref, out_ref, send_sems, recv_sems)
        _ag_finish(w_ref, out_ref, send_sems, recv_sems)

    return pl.pallas_call(
        body, in_specs=[ANY], out_specs=ANY, out_shape=SDS((4, rows, cols), wsh.dtype),
        scratch_shapes=[pltpu.SemaphoreType.DMA((6,)), pltpu.SemaphoreType.DMA((6,))],
        name="ag_weights")(wsh)


def _rs_swap_halves(gfull, tag):
    n_sh, rows, cols = gfull.shape
    half_rows = rows // 2

    def body(g_ref, recv_ref, send_sem, recv_sem):
        x, y, c = _coords()
        sib_base = (1 - c) * half_rows
        for sh in range(n_sh):
            for r0, nr in _row_chunks(half_rows, D2D_CHUNKS, 8):
                pltpu.make_async_remote_copy(
                    src_ref=g_ref.at[sh, pl.ds(pl.multiple_of(sib_base + r0, 8), nr)], dst_ref=recv_ref.at[sh, pl.ds(r0, nr)],
                    send_sem=send_sem, recv_sem=recv_sem, device_id=(x, y, 1 - c), device_id_type=MESH).start()
        pltpu.make_async_remote_copy(src_ref=recv_ref, dst_ref=recv_ref, send_sem=send_sem, recv_sem=recv_sem,
                                     device_id=(x, y, 1 - c), device_id_type=MESH).wait()

    return pl.pallas_call(
        body, in_specs=[ANY], out_specs=ANY, out_shape=SDS((n_sh, half_rows, cols), gfull.dtype),
        scratch_shapes=[pltpu.SemaphoreType.DMA, pltpu.SemaphoreType.DMA], name="rs_swap_halves" + tag)(gfull)


def _rs_add_pair(gfull, recv, core, tr, tag):
    n_sh, rows, cols = gfull.shape
    half_rows = rows // 2
    nblk = half_rows // tr

    def body(c_ref, g_ref, r_ref, o_ref, ob_ref):
        acc = g_ref[...] + r_ref[...]
        o_ref[...] = acc
        ob_ref[...] = acc.astype(BF16)

    out = pl.BlockSpec((None, tr, cols), lambda sh, i, c_ref: (sh, i, 0))
    gs = pltpu.PrefetchScalarGridSpec(
        num_scalar_prefetch=1, grid=(n_sh, nblk),
        in_specs=[pl.BlockSpec((None, tr, cols), lambda sh, i, c_ref: (sh, c_ref[0] * nblk + i, 0)), out],
        out_specs=[out, out])
    return pl.pallas_call(body, grid_spec=gs, out_shape=[SDS((n_sh, half_rows, cols), F32), SDS((n_sh, half_rows, cols), BF16)],
                          compiler_params=_cp(2), name="rs_add_pair" + tag)(core, gfull, recv)


def _rs_send(cs_ref, recv_ref, send_sems, recv_sems):
    x, y, c = _coords()
    half_rows = cs_ref.shape[1]
    for k, (px, py) in enumerate(_other_chips(x, y)):
        for r0, nr in _row_chunks(half_rows, ICI_CHUNKS, 16):
            pltpu.make_async_remote_copy(
                src_ref=cs_ref.at[2 * px + py, pl.ds(r0, nr)], dst_ref=recv_ref.at[k, pl.ds(r0, nr)],
                send_sem=send_sems.at[k], recv_sem=recv_sems.at[k], device_id=(px, py, c), device_id_type=MESH).start()


def _rs_wait(recv_ref, send_sems, recv_sems):
    x, y, c = _coords()
    for k, (px, py) in enumerate(_other_chips(x, y)):
        pltpu.make_async_remote_copy(src_ref=recv_ref.at[k], dst_ref=recv_ref.at[k], send_sem=send_sems.at[k],
                                     recv_sem=recv_sems.at[k], device_id=(px, py, c), device_id_type=MESH).wait()


def _rs_to_owner(chipsum):
    n_sh, half_rows, cols = chipsum.shape

    def body(cs_ref, recv_ref, send_sems, recv_sems):
        _rs_send(cs_ref, recv_ref, send_sems, recv_sems)
        _rs_wait(recv_ref, send_sems, recv_sems)

    return pl.pallas_call(
        body, in_specs=[ANY], out_specs=ANY, out_shape=SDS((3, half_rows, cols), chipsum.dtype),
        scratch_shapes=[pltpu.SemaphoreType.DMA((3,)), pltpu.SemaphoreType.DMA((3,))], name="rs_to_owner")(chipsum)


def _rs_add_chips(chipsum, recv, shard_core, tr, tag):
    _, half_rows, cols = chipsum.shape

    def body(s_ref, m_ref, r0_ref, r1_ref, r2_ref, o_ref):
        o_ref[...] = ((m_ref[...] + r0_ref[...].astype(F32)) + r1_ref[...].astype(F32)) + r2_ref[...].astype(F32)

    gs = pltpu.PrefetchScalarGridSpec(
        num_scalar_prefetch=1, grid=(half_rows // tr,),
        in_specs=[pl.BlockSpec((None, tr, cols), lambda i, s_ref: (s_ref[0], i, 0))]
        + [pl.BlockSpec((None, tr, cols), (lambda k: lambda i, s_ref: (k, i, 0))(k)) for k in range(3)],
        out_specs=pl.BlockSpec((None, tr, cols), lambda i, s_ref: (s_ref[1], i, 0)))
    return pl.pallas_call(body, grid_spec=gs, out_shape=SDS((2, half_rows, cols), F32),
                          compiler_params=_cp(1), name="rs_add_chips" + tag)(shard_core, chipsum, recv, recv, recv)


def _rs_join_halves(buf, tag):
    _, half_rows, cols = buf.shape

    def body(b_ref, out_ref, send_sem, recv_sem):
        x, y, c = _coords()
        for r0, nr in _row_chunks(half_rows, D2D_CHUNKS, 8):
            pltpu.make_async_remote_copy(src_ref=out_ref.at[c, pl.ds(r0, nr)], dst_ref=out_ref.at[c, pl.ds(r0, nr)], send_sem=send_sem,
                                         recv_sem=recv_sem, device_id=(x, y, 1 - c), device_id_type=MESH).start()
        pltpu.make_async_remote_copy(src_ref=out_ref.at[c], dst_ref=out_ref.at[c], send_sem=send_sem, recv_sem=recv_sem,
                                     device_id=(x, y, 1 - c), device_id_type=MESH).wait()

    return pl.pallas_call(
        body, in_specs=[ANY], out_specs=ANY, out_shape=SDS(buf.shape, buf.dtype), input_output_aliases={0: 0},
        scratch_shapes=[pltpu.SemaphoreType.DMA, pltpu.SemaphoreType.DMA], name="rs_join_halves" + tag)(buf)


BIG = [("w_in", (1024, 1440), 1), ("w_uq", (256, 768), 1), ("w_ukv", (128, 1024), 1), ("w_out", (1024, 1024), 0),
       ("w_mem_q", (1024, 1024), 0), ("w_mem_kv", (1024, 2048), 1), ("w_mem_o", (1024, 1024), 0),
       ("w_up", (1024, 5632), 1), ("w_down", (2816, 1024), 0)]
SMALL_REP = [("mix_norm_g", 1024), ("b_conv_in", 1024), ("b_conv_dw", 512), ("conv_ln_g", 512), ("conv_ln_b", 512),
             ("q_lat_norm_g", 256), ("kv_lat_norm_g", 128), ("q_norm_g", 96), ("k_norm_g", 96), ("mem_norm_x_g", 1024),
             ("mem_norm_m_g", 1024), ("mem_q_norm_g", 256), ("mem_k_norm_g", 256), ("ffn_norm_g", 1024), ("b_ffn_dw", 5632)]
SMALL_SH = [("w_conv_dw", (31, 512)), ("w_ffn_dw", (3, 5632))]


def _shard_shape(shape, axis):
    return tuple(d // 4 if a == axis else d for a, d in enumerate(shape))


def _pack_rows(parts, rows, cols):
    flat = jnp.concatenate([p.reshape(-1) for p in parts])
    flat = jnp.pad(flat, (0, rows * cols - flat.shape[0]))
    return flat.reshape(rows, cols)


AG_EARLY, AG_LATE = BIG[:3], BIG[3:]
RS_REST, RS_FFN = AG_EARLY, AG_LATE


def _group_rows(group):
    used = sum(_shard_shape(shape, axis)[0] * _shard_shape(shape, axis)[1] // PACK_COLS for _, shape, axis in group)
    return -(-used // 512) * 512


def _pick_rows(n, cap=384):
    return max(r for r in range(16, cap + 1, 16) if n % r == 0)


def _pack_big_shards(ws, group):
    parts = [ws[n].reshape(-1, PACK_COLS) for n, _, _ in group]
    used = sum(p.shape[0] for p in parts)
    pad = _group_rows(group) - used
    return jnp.concatenate(parts + ([jnp.zeros((pad, PACK_COLS), parts[0].dtype)] if pad else []), axis=0)


def _unpack_big_shards(packed, group):
    out, r = {}, 0
    for n, shape, axis in group:
        sh = _shard_shape(shape, axis)
        nr = sh[0] * sh[1] // PACK_COLS
        out[n] = packed[r:r + nr].reshape(sh)
        r += nr
    return out


def _unpack_gathered(g, group):
    out, r = {}, 0
    for n, shape, axis in group:
        sh = _shard_shape(shape, axis)
        nr = sh[0] * sh[1] // PACK_COLS
        part = g[:, r:r + nr]
        if axis == 0:
            out[n] = part.reshape(shape)
        else:
            out[n] = part.reshape((4,) + sh).transpose(1, 0, 2).reshape(shape)
        r += nr
    return out


def _pack_full_grads(gs, group):
    parts = []
    for n, shape, axis in group:
        sh = _shard_shape(shape, axis)
        nr = sh[0] * sh[1] // PACK_COLS
        if axis == 0:
            parts.append(gs[n].reshape(4, nr, PACK_COLS))
        else:
            parts.append(gs[n].reshape(shape[0], 4, sh[1]).transpose(1, 0, 2).reshape(4, nr, PACK_COLS))
    pad = _group_rows(group) - sum(p.shape[1] for p in parts)
    return jnp.concatenate(parts + ([jnp.zeros((4, pad, PACK_COLS), F32)] if pad else []), axis=1)


def _rs_first(gfull, core_idx, tag):
    tr = _pick_rows(gfull.shape[1] // 2)
    return _rs_add_pair(gfull, _rs_swap_halves(gfull, tag), core_idx.reshape(1), tr, tag)


def _rs_last(chipsum, recv, shard_idx, core_idx, tag):
    tr = _pick_rows(chipsum.shape[1])
    red = _rs_add_chips(chipsum, recv, jnp.stack([shard_idx, core_idx]), tr, tag)
    return _rs_join_halves(red, tag).reshape(2 * chipsum.shape[1], chipsum.shape[2])


def _rope_tables(positions):
    inv_freq = ROPE_THETA ** (-jnp.arange(0, ROPE, 2, dtype=F32) / ROPE)
    ang = positions.astype(F32)[:, None] * inv_freq
    cos, sin = jnp.cos(ang), jnp.sin(ang)
    s = positions.shape[0]
    cosf = jnp.concatenate([jnp.ones((s, NOPE), F32), cos, cos, jnp.ones((s, HEAD_PAD - HEAD_DIM), F32)], axis=-1)
    sinf = jnp.concatenate([jnp.zeros((s, NOPE), F32), -sin, sin, jnp.zeros((s, HEAD_PAD - HEAD_DIM), F32)], axis=-1)
    return cosf, sinf


def _pad_heads(w, per_head):
    k = w.shape[0]
    w3 = w.reshape(k, HEADS, per_head)
    return jnp.pad(w3, ((0, 0), (0, 0), (0, HEAD_PAD - per_head))).reshape(k, HEADS * HEAD_PAD)


def _layer_grads(x, mem, positions, target, wf, w_late, sp, shard_idx, core_idx):
    wf = dict(wf)
    s = x.shape[0]
    tm = _row_tile(s, 512)
    tc = _row_tile(s, 256)
    tb = 512 if s % 512 == 0 and s > 512 else s // 2
    row2 = lambda a: a.reshape(1, -1)

    w_in = wf["w_in"]
    w_in_pad = jnp.concatenate([w_in[:, :1408], jnp.zeros((D_MODEL, NOPE), BF16), w_in[:, 1408:],
                                jnp.zeros((D_MODEL, HEAD_PAD - HEAD_DIM), BF16)], axis=1)
    w_uq_pad = _pad_heads(wf["w_uq"], HEAD_DIM)
    w_ukv = wf["w_ukv"]
    gq_pad = jnp.pad(sp["q_norm_g"], (0, HEAD_PAD - HEAD_DIM)).reshape(1, HEAD_PAD)
    gk_pad = jnp.pad(sp["k_norm_g"], (0, HEAD_PAD - HEAD_DIM)).reshape(1, HEAD_PAD)
    w_dw32 = jnp.pad(sp["w_conv_dw"], ((0, 1), (0, 0)))
    w_ffn8 = jnp.pad(sp["w_ffn_dw"], ((0, 5), (0, 0)))
    b_ffn = row2(sp["b_ffn_dw"])
    cosf, sinf = _rope_tables(positions)

    z, h1 = _norm_linear(x, 0, D_MODEL, row2(sp["mix_norm_g"]), w_in_pad, F32, tm, IN_COLS_PAD, "in_proj")
    u, u0, u1 = _conv_fwd(z, row2(sp["b_conv_in"]), w_dw32, row2(sp["b_conv_dw"]), row2(sp["conv_ln_g"]), row2(sp["conv_ln_b"]), tc)
    q_raw, cqn = _norm_linear(z, 1024 // Q_RANK, Q_RANK, row2(sp["q_lat_norm_g"]), w_uq_pad, F32, tm, 1024, "q_up")
    kv_raw, ckvn = _norm_linear(z, 1280 // KV_RANK, KV_RANK, row2(sp["kv_lat_norm_g"]), w_ukv, F32, tm, 1024, "kv_up")
    qp, kp, vp = _mla_prep(q_raw, kv_raw, z, cosf, sinf, gq_pad, gk_pad, tc)
    o_f, o_b, lse, gathered = _attn_fwd(qp, kp, vp, tb, w_late)
    wf.update(_unpack_gathered(lax.dynamic_update_index_in_dim(gathered, w_late, shard_idx, 0), AG_LATE))
    w_out_u = wf["w_out"][:CONV_CH]
    w_out_o = jnp.pad(wf["w_out"][CONV_CH:].reshape(HEADS, NOPE, D_MODEL), ((0, 0), (NOPE, 0), (0, 0))).reshape(HEADS * HEAD_PAD, D_MODEL)
    w_up_g, w_up_v = wf["w_up"][:, :D_FF], wf["w_up"][:, D_FF:]
    (x1,) = _linear([(u, w_out_u), (o_b, w_out_o)], False, x, [F32], tm, 1024, "out_proj")

    qm, hq = _norm_linear(x1, 0, D_MODEL, row2(sp["mem_norm_x_g"]), wf["w_mem_q"], F32, tm, 1024, "memq_proj")
    kvm, hm = _norm_linear(mem, 0, D_MODEL, row2(sp["mem_norm_m_g"]), wf["w_mem_kv"], F32, MEM_LEN, 1024, "memkv_proj")
    gmq, gmk = row2(sp["mem_q_norm_g"]), row2(sp["mem_k_norm_g"])
    o_m = _memattn_fwd(qm, kvm, gmq, gmk, tm)
    (x2,) = _linear([(o_m, wf["w_mem_o"])], False, x1, [F32], tm, 1024, "memo_proj")

    up0, h3 = _norm_linear(x2, 0, D_MODEL, row2(sp["ffn_norm_g"]), wf["w_up"], F32, _row_tile(s, 1024), D_FF // 2, "up_proj")
    act = _ffn_fwd(up0, w_ffn8, b_ffn, tc, D_FF // 2)
    dy_f, dy_b, lsum = _down_loss(act, wf["w_down"], x2, target, tm)

    g = {}
    (d_act,) = _linear([(dy_b, wf["w_down"])], True, None, [F32], tm, D_FF // 2, "down_bwd")
    g["w_down"] = _dw(act, dy_b, "dw_down")
    d_up0g, d_up0v, dbg, dbv, dwg, dwv = _ffn_bwd(d_act, up0, w_ffn8, b_ffn, tc, D_FF // 2)
    g["b_ffn_dw"] = jnp.concatenate([dbg, dbv], axis=1).reshape(-1)
    g["w_ffn_dw"] = jnp.concatenate([dwg[:3], dwv[:3]], axis=1)
    g["w_up"] = jnp.concatenate([_dw(h3, d_up0g, "dw_up_g"), _dw(h3, d_up0v, "dw_up_v")], axis=1)
    d_x2f, d_x2b, dg = _linear_normbwd([(d_up0g, w_up_g), (d_up0v, w_up_v)], x2, 0, row2(sp["ffn_norm_g"]), dy_f,
                                       [F32, BF16], tc, "up_bwd")
    g["ffn_norm_g"] = dg.reshape(-1)

    (d_om,) = _linear([(d_x2b, wf["w_mem_o"])], True, None, [BF16], tm, 1024, "memo_bwd")
    g["w_mem_o"] = _dw(o_m, d_x2b, "dw_mem_o")
    d_qm, d_km, d_vm, dgq, dgk = _memattn_bwd(qm, kvm, d_om, gmq, gmk, tm)
    g["mem_q_norm_g"], g["mem_k_norm_g"] = dgq.reshape(-1), dgk.reshape(-1)
    d_kvm = jnp.concatenate([d_km, d_vm], axis=1)
    g["w_mem_q"] = _dw(hq, d_qm, "dw_mem_q")
    g["w_mem_kv"] = _dw(hm, d_kvm, "dw_mem_kv")
    d_x1f, d_x1b, dg = _linear_normbwd([(d_qm, wf["w_mem_q"])], x1, 0, row2(sp["mem_norm_x_g"]), d_x2f, [F32, BF16], tm, "memq_bwd")
    g["mem_norm_x_g"] = dg.reshape(-1)
    _, dg = _linear_normbwd([(d_kvm, wf["w_mem_kv"])], mem, 0, row2(sp["mem_norm_m_g"]), None, [BF16], MEM_LEN, "memkv_bwd")
    g["mem_norm_m_g"] = dg.reshape(-1)

    (d_u,) = _linear([(d_x1b, w_out_u)], True, None, [F32], tm, CONV_CH, "out_bwd_u")
    d_of, d_ob = _linear([(d_x1b, w_out_o)], True, None, [F32, BF16], tm, 1024, "out_bwd_o")
    dw_out_u = _dw(u, d_x1b, "dw_out_u")
    dw_out_o = _dw(o_b, d_x1b, "dw_out_o")
    g["w_out"] = jnp.concatenate([dw_out_u, dw_out_o.reshape(HEADS, HEAD_PAD, D_MODEL)[:, NOPE:].reshape(HEADS * NOPE, D_MODEL)], axis=0)
    chipsum_ffn, wire_ffn = _rs_first(_pack_full_grads(g, RS_FFN), core_idx, "_ffn")
    delta = _attn_delta(d_of, o_f, tb)
    dqp, dkp, dvp, recv_ffn = _attn_bwd(qp, kp, vp, d_ob, lse, delta, tb, wire_ffn)
    g_ffn_packed = _rs_last(chipsum_ffn, recv_ffn, shard_idx, core_idx, "_ffn")
    d_qraw, d_kvraw, d_kr, dgq, dgk = _mla_prep_bwd(dqp, dkp, dvp, q_raw, kv_raw, z, cosf, sinf, gq_pad, gk_pad, tc)
    g["q_norm_g"], g["k_norm_g"] = dgq.reshape(-1)[:HEAD_DIM], dgk.reshape(-1)[:HEAD_DIM]
    g["w_uq"] = _dw(cqn, d_qraw, "dw_uq").reshape(Q_RANK, HEADS, HEAD_PAD)[:, :, :HEAD_DIM].reshape(Q_RANK, HEADS * HEAD_DIM)
    g["w_ukv"] = _dw(ckvn, d_kvraw, "dw_ukv")
    d_cq, dg = _linear_normbwd([(d_qraw, w_uq_pad)], z, 1024 // Q_RANK, row2(sp["q_lat_norm_g"]), None, [BF16], tm, "q_up_bwd")
    g["q_lat_norm_g"] = dg.reshape(-1)
    d_ckv, dg = _linear_normbwd([(d_kvraw, w_ukv)], z, 1280 // KV_RANK, row2(sp["kv_lat_norm_g"]), None, [BF16], tm, "kv_up_bwd")
    g["kv_lat_norm_g"] = dg.reshape(-1)
    d_u1, dlg, dlb, dbdw = _conv_bwd_ln(d_u, u1, row2(sp["conv_ln_g"]), row2(sp["conv_ln_b"]), tc)
    g["conv_ln_g"], g["conv_ln_b"], g["b_conv_dw"] = dlg.reshape(-1), dlb.reshape(-1), dbdw.reshape(-1)
    d_conv, dw_dw, dbin = _conv_bwd_dw(d_u1, u0, z, row2(sp["b_conv_in"]), w_dw32, tc)
    g["w_conv_dw"], g["b_conv_in"] = dw_dw[:CONV_WIDTH], dbin.reshape(-1)
    pieces = [(d_conv, w_in_pad[:, :1024]), (d_cq, w_in_pad[:, 1024:1280]), (d_ckv, w_in_pad[:, 1280:1408]), (d_kr, w_in_pad[:, 1408:])]
    dw_in = [_dw(h1, d, "dw_in_%d" % k) for k, (d, _) in enumerate(pieces)]
    g["w_in"] = jnp.concatenate([dw_in[0], dw_in[1], dw_in[2], dw_in[3][:, NOPE:HEAD_DIM]], axis=1)
    grad_x, dg = _linear_normbwd(pieces, x, 0, row2(sp["mix_norm_g"]), d_x1f, [F32], tm, "in_bwd")
    g["mix_norm_g"] = dg.reshape(-1)
    return lsum[0, 0], grad_x, g, g_ffn_packed


def kernel(x, mem, positions, mix_norm_g, w_in, b_conv_in, w_conv_dw, b_conv_dw, conv_ln_g, conv_ln_b, q_lat_norm_g, w_uq, kv_lat_norm_g, w_ukv, q_norm_g, k_norm_g, w_out, mem_norm_x_g, mem_norm_m_g, w_mem_q, w_mem_kv, mem_q_norm_g, mem_k_norm_g, w_mem_o, ffn_norm_g, w_up, w_ffn_dw, b_ffn_dw, w_down, loss_target, m_mix_norm_g, m_w_in, m_b_conv_in, m_w_conv_dw, m_b_conv_dw, m_conv_ln_g, m_conv_ln_b, m_q_lat_norm_g, m_w_uq, m_kv_lat_norm_g, m_w_ukv, m_q_norm_g, m_k_norm_g, m_w_out, m_mem_norm_x_g, m_mem_norm_m_g, m_w_mem_q, m_w_mem_kv, m_mem_q_norm_g, m_mem_k_norm_g, m_w_mem_o, m_ffn_norm_g, m_w_up, m_w_ffn_dw, m_b_ffn_dw, m_w_down, v_mix_norm_g, v_w_in, v_b_conv_in, v_w_conv_dw, v_b_conv_dw, v_conv_ln_g, v_conv_ln_b, v_q_lat_norm_g, v_w_uq, v_kv_lat_norm_g, v_w_ukv, v_q_norm_g, v_k_norm_g, v_w_out, v_mem_norm_x_g, v_mem_norm_m_g, v_w_mem_q, v_w_mem_kv, v_mem_q_norm_g, v_mem_k_norm_g, v_w_mem_o, v_ffn_norm_g, v_w_up, v_w_ffn_dw, v_b_ffn_dw, v_w_down):
    names = ["mix_norm_g", "w_in", "b_conv_in", "w_conv_dw", "b_conv_dw", "conv_ln_g", "conv_ln_b", "q_lat_norm_g", "w_uq",
             "kv_lat_norm_g", "w_ukv", "q_norm_g", "k_norm_g", "w_out", "mem_norm_x_g", "mem_norm_m_g", "w_mem_q", "w_mem_kv",
             "mem_q_norm_g", "mem_k_norm_g", "w_mem_o", "ffn_norm_g", "w_up", "w_ffn_dw", "b_ffn_dw", "w_down"]
    loc = locals()
    w = {n: loc[n] for n in names}
    m = {n: loc["m_" + n] for n in names}
    v = {n: loc["v_" + n] for n in names}
    shard_idx = 2 * lax.axis_index("x") + lax.axis_index("y")

    shard_idx = shard_idx.astype(jnp.int32)
    core_idx = lax.axis_index("c").astype(jnp.int32)

    w_local = {n: w[n][0] for n, _, _ in BIG}
    w_early = _pack_big_shards(w_local, AG_EARLY).astype(BF16)
    w_late = _pack_big_shards(w_local, AG_LATE).astype(BF16)
    wf = _unpack_gathered(lax.dynamic_update_index_in_dim(_ag_weights(w_early), w_early, shard_idx, 0), AG_EARLY)

    small_sh_full = {}
    gather_in = []
    for n, (r, c) in SMALL_SH:
        csh = c // 4
        slab = lax.dynamic_update_slice(jnp.zeros((r, c), F32), w[n][0], (0, shard_idx * csh))
        gather_in.append(slab.reshape(-1))
    gather_rows = 256
    gathered_small = _allreduce_small_named(_pack_rows(gather_in, gather_rows, SMALL_COLS), "gather_small") * 0.5
    off = 0
    for n, (r, c) in SMALL_SH:
        small_sh_full[n] = gathered_small.reshape(-1)[off:off + r * c].reshape(r, c)
        off += r * c
    sp = {n: w[n][0] for n, _ in SMALL_REP}
    sp.update(small_sh_full)

    lsum, grad_x, g, g_ffn_packed = _layer_grads(x[0], mem[0], positions[0], loss_target[0], wf, w_late, sp, shard_idx, core_idx)

    small_parts = [jnp.full((SMALL_COLS,), lsum, F32)] + [g[n] for n, _ in SMALL_REP] + [g[n] for n, _ in SMALL_SH]
    small_rows = 368
    small_sum = _allreduce_small_named(_pack_rows(small_parts, small_rows, SMALL_COLS), "allreduce_small").reshape(-1)
    loss = small_sum[0] * (0.5 / D_MODEL)
    gs = {}
    off = SMALL_COLS
    for n, sz in SMALL_REP:
        gs[n] = small_sum[off:off + sz].reshape(w[n].shape)
        off += sz
    for n, (r, c) in SMALL_SH:
        full = small_sum[off:off + r * c].reshape(r, c)
        gs[n] = lax.dynamic_slice(full, (0, shard_idx * (c // 4)), (r, c // 4)).reshape(w[n].shape)
        off += r * c

    chipsum, chipsum_wire = _rs_first(_pack_full_grads(g, RS_REST), core_idx, "_rest")
    g_rest_packed = _rs_last(chipsum, _rs_to_owner(chipsum_wire), shard_idx, core_idx, "_rest")
    g_big = {**_unpack_big_shards(g_rest_packed, RS_REST), **_unpack_big_shards(g_ffn_packed, RS_FFN)}
    gs.update({n: a[None] for n, a in g_big.items()})

    delta, new_m, new_v = {}, {}, {}
    for n, _, _ in BIG:
        d_n, m_n, v_n = _adamw(w[n][0], g_big[n], m[n][0], v[n][0], "adamw_" + n)
        delta[n], new_m[n], new_v[n] = d_n[None], m_n[None], v_n[None]
    small_names = [n for n, _ in SMALL_REP] + [n for n, _ in SMALL_SH]
    as2d = lambda a: a.reshape(-1, a.shape[-1])
    d_s, m_s, v_s = _adamw_small(*[[as2d(d[n]) for n in small_names] for d in (w, gs, m, v)])
    for k, n in enumerate(small_names):
        delta[n], new_m[n], new_v[n] = d_s[k].reshape(w[n].shape), m_s[k].reshape(w[n].shape), v_s[k].reshape(w[n].shape)

    return (loss, grad_x[None], *[gs[n] for n in names], *[delta[n] for n in names], *[new_m[n] for n in names],
            *[new_v[n] for n in names])


def _allreduce_small_named(v, name):
    rows, cols = v.shape

    def body(v_ref, out_ref, buf, send_sems, recv_sems):
        x, y, c = _coords()
        me = 4 * x + 2 * y + c
        buf[me] = v_ref[...]
        cps = []
        for r in range(1, 8):
            dx, dy, dc = (r >> 2) & 1, (r >> 1) & 1, r & 1
            to = (x + dx - 2 * x * dx, y + dy - 2 * y * dy, c + dc - 2 * c * dc)
            cp = pltpu.make_async_remote_copy(src_ref=v_ref, dst_ref=buf.at[me], send_sem=send_sems.at[r - 1],
                                              recv_sem=recv_sems.at[r - 1], device_id=to, device_id_type=MESH)
            cp.start()
            cps.append(cp)
        for cp in cps:
            cp.wait()
        acc = buf[0]
        for d in range(1, 8):
            acc = acc + buf[d]
        out_ref[...] = acc

    vm = pl.BlockSpec(memory_space=pltpu.VMEM)
    return pl.pallas_call(
        body, in_specs=[vm], out_specs=vm, out_shape=SDS((rows, cols), F32),
        scratch_shapes=[pltpu.VMEM((8, rows, cols), F32), pltpu.SemaphoreType.DMA((7,)), pltpu.SemaphoreType.DMA((7,))],
        name=name)(v)
```

```python
import math

import numpy as np
import jax
import jax.numpy as jnp
from jax import lax
from jax.experimental import pallas as pl
from jax.experimental.pallas import tpu as pltpu

F32 = jnp.float32
BF16 = jnp.bfloat16
SDS = jax.ShapeDtypeStruct
MESH = pl.DeviceIdType.MESH

D_MODEL = 1024
EPS = 1e-6
CONV_CH = 512
CONV_WIDTH = 31
CONV_HALO = 32
HEADS = 8
NOPE = 64
ROPE = 32
HEAD_DIM = NOPE + ROPE
HEAD_PAD = 128
Q_RANK = 256
KV_RANK = 128
CHUNK = 64
ROPE_THETA = 10000.0
IN_COLS_PAD = 1536
MEM_HEADS = 4
MEM_HEAD_DIM = 256
MEM_LEN = 256
D_FF = 2816
FFN_HALO = 8
ATT_SCALE = 1.0 / math.sqrt(HEAD_DIM)
LOG2E = math.log2(math.e)
LN2 = math.log(2.0)

ADAM_LR = 0.001
ADAM_B1 = 0.9
ADAM_B2 = 0.999
ADAM_EPS = 1e-08
ADAM_WD = 0.01
ADAM_STEP = 10

VMEM_LIMIT_V7X = 56 * 1024 * 1024
PACK_COLS = 1024
SMALL_COLS = 128


def _cp(n_axes):
    return pltpu.CompilerParams(dimension_semantics=("arbitrary",) * n_axes, vmem_limit_bytes=VMEM_LIMIT_V7X)


def _row_tile(s, want):
    return want if s % want == 0 else s


def _norm_linear(x, xcol, kdim, g, w, out_dtype, tm, tn, name):
    s = x.shape[0]
    n = w.shape[1]

    def body(x_ref, g_ref, w_ref, y_ref, hn_ref):
        @pl.when(pl.program_id(1) == 0)
        def _():
            xv = x_ref[...]
            r = lax.rsqrt(jnp.mean(xv * xv, axis=-1, keepdims=True) + EPS)
            hn_ref[...] = ((xv * r) * g_ref[...]).astype(BF16)

        y_ref[...] = jnp.dot(hn_ref[...], w_ref[...], preferred_element_type=F32).astype(y_ref.dtype)

    return pl.pallas_call(
        body, grid=(s // tm, n // tn),
        in_specs=[pl.BlockSpec((tm, kdim), lambda i, j: (i, xcol)), pl.BlockSpec((1, kdim), lambda i, j: (0, 0)),
                  pl.BlockSpec((kdim, tn), lambda i, j: (0, j))],
        out_specs=[pl.BlockSpec((tm, tn), lambda i, j: (i, j)), pl.BlockSpec((tm, kdim), lambda i, j: (i, 0))],
        out_shape=[SDS((s, n), out_dtype), SDS((s, kdim), BF16)],
        compiler_params=_cp(2), name=name)(x, g, w)


def _linear(pairs, nt, residual, out_dtypes, tm, tn, name):
    s = pairs[0][0].shape[0]
    n = pairs[0][1].shape[0] if nt else pairs[0][1].shape[1]
    n_pairs = len(pairs)
    has_res = residual is not None

    def body(*refs):
        a_refs = refs[:n_pairs]
        w_refs = refs[n_pairs:2 * n_pairs]
        res_ref = refs[2 * n_pairs] if has_res else None
        outs = refs[2 * n_pairs + int(has_res):]
        acc = None
        for a_ref, w_ref in zip(a_refs, w_refs):
            a = a_ref[...].astype(BF16)
            if nt:
                d = lax.dot_general(a, w_ref[...], (((1,), (1,)), ((), ())), preferred_element_type=F32)
            else:
                d = jnp.dot(a, w_ref[...], preferred_element_type=F32)
            acc = d if acc is None else acc + d
        if has_res:
            acc = res_ref[...] + acc
        for o in outs:
            o[...] = acc.astype(o.dtype)

    in_specs = [pl.BlockSpec((tm, a.shape[1]), lambda i, j: (i, 0)) for a, _ in pairs]
    if nt:
        in_specs += [pl.BlockSpec((tn, w.shape[1]), lambda i, j: (j, 0)) for _, w in pairs]
    else:
        in_specs += [pl.BlockSpec((w.shape[0], tn), lambda i, j: (0, j)) for _, w in pairs]
    args = [a for a, _ in pairs] + [w for _, w in pairs]
    if has_res:
        in_specs.append(pl.BlockSpec((tm, tn), lambda i, j: (i, j)))
        args.append(residual)
    outs = pl.pallas_call(
        body, grid=(s // tm, n // tn), in_specs=in_specs,
        out_specs=[pl.BlockSpec((tm, tn), lambda i, j: (i, j)) for _ in out_dtypes],
        out_shape=[SDS((s, n), dt) for dt in out_dtypes],
        compiler_params=_cp(2), name=name)(*args)
    return outs


def _linear_normbwd(pairs, x, xcol, g, d_res, out_dtypes, tm, name):
    s = pairs[0][0].shape[0]
    dn = pairs[0][1].shape[0]
    n_pairs = len(pairs)
    has_res = d_res is not None

    def body(*refs):
        a_refs = refs[:n_pairs]
        w_refs = refs[n_pairs:2 * n_pairs]
        x_ref, g_ref = refs[2 * n_pairs], refs[2 * n_pairs + 1]
        k = 2 * n_pairs + 2
        res_ref = refs[k] if has_res else None
        k += int(has_res)
        outs = refs[k:-1]
        dg_ref = refs[-1]
        dh = None
        for a_ref, w_ref in zip(a_refs, w_refs):
            d = lax.dot_general(a_ref[...].astype(BF16), w_ref[...], (((1,), (1,)), ((), ())), preferred_element_type=F32)
            dh = d if dh is None else dh + d
        xv = x_ref[...]
        r = lax.rsqrt(jnp.mean(xv * xv, axis=-1, keepdims=True) + EPS)
        y = xv * r

        @pl.when(pl.program_id(0) == 0)
        def _():
            dg_ref[...] = jnp.zeros_like(dg_ref)

        dg_ref[...] += jnp.sum(dh * y, axis=0, keepdims=True)
        dy = dh * g_ref[...]
        dx = r * (dy - y * jnp.mean(dy * y, axis=-1, keepdims=True))
        if has_res:
            dx = res_ref[...] + dx
        for o in outs:
            o[...] = dx.astype(o.dtype)

    in_specs = [pl.BlockSpec((tm, a.shape[1]), lambda i: (i, 0)) for a, _ in pairs]
    in_specs += [pl.BlockSpec((dn, w.shape[1]), lambda i: (0, 0)) for _, w in pairs]
    in_specs += [pl.BlockSpec((tm, dn), lambda i: (i, xcol)), pl.BlockSpec((1, dn), lambda i: (0, 0))]
    args = [a for a, _ in pairs] + [w for _, w in pairs] + [x, g]
    if has_res:
        in_specs.append(pl.BlockSpec((tm, dn), lambda i: (i, 0)))
        args.append(d_res)
    outs = pl.pallas_call(
        body, grid=(s // tm,), in_specs=in_specs,
        out_specs=[pl.BlockSpec((tm, dn), lambda i: (i, 0)) for _ in out_dtypes] + [pl.BlockSpec((1, dn), lambda i: (0, 0))],
        out_shape=[SDS((s, dn), dt) for dt in out_dtypes] + [SDS((1, dn), F32)],
        compiler_params=_cp(1), name=name)(*args)
    return outs


def _dw_matmul(a, b, tk, tn, ts, name):
    s, ka = a.shape
    n = b.shape[1]

    def body(a_ref, b_ref, o_ref):
        @pl.when(pl.program_id(2) == 0)
        def _():
            o_ref[...] = jnp.zeros_like(o_ref)

        o_ref[...] += lax.dot_general(a_ref[...].astype(BF16), b_ref[...].astype(BF16), (((0,), (0,)), ((), ())),
                                      preferred_element_type=F32)

    return pl.pallas_call(
        body, grid=(ka // tk, n // tn, s // ts),
        in_specs=[pl.BlockSpec((ts, tk), lambda k, j, t: (t, k)), pl.BlockSpec((ts, tn), lambda k, j, t: (t, j))],
        out_specs=pl.BlockSpec((tk, tn), lambda k, j, t: (k, j)),
        out_shape=SDS((ka, n), F32), compiler_params=_cp(3), name=name)(a, b)


def _dw(a, b, name):
    s, ka = a.shape
    n = b.shape[1]
    tk = ka if ka <= 1024 else ka // 2
    tn = n if n <= 1024 else (n // 2 if n == D_FF else 512)
    return _dw_matmul(a, b, tk, tn, _row_tile(s, 2048), name)


def _prev_halo(tm, halo):
    return lambda i: (jnp.maximum(i * (tm // halo) - 1, 0), 0)


def _next_halo(tm, halo, s):
    return lambda i: (jnp.minimum((i + 1) * (tm // halo), s // halo - 1), 0)


def _shifted_copies(ext, tm):
    n = tm + CONV_HALO - 8
    for s in range(1, 8):
        ext[s, 0:n, :] = ext[0, s:s + n, :]


def _sum_taps(terms, ways=4):
    accs = []
    for i, t in enumerate(terms):
        if i < ways:
            accs.append(t)
        else:
            accs[i % ways] = accs[i % ways] + t
    while len(accs) > 1:
        accs = [accs[i] + accs[i + 1] if i + 1 < len(accs) else accs[i] for i in range(0, len(accs), 2)]
    return accs[0]


def _tap_rows(ext, o, n, cs):
    return ext[o % 8, o - o % 8:o - o % 8 + n, cs]


def _conv_fwd(z, b_in, w32, b_dw, ln_g, ln_b, tm):
    s = z.shape[0]
    c = CONV_CH

    def body(z_ref, zh_ref, bin_ref, w_ref, bdw_ref, lg_ref, lb_ref, u_ref, u0_ref, u1_ref, ext):
        i = pl.program_id(0)

        def glu(zz):
            zz = zz + bin_ref[...]
            return zz[:, :c] * jax.nn.sigmoid(zz[:, c:])

        u0 = glu(z_ref[...])
        u0_ref[...] = u0
        ext[0, 0:CONV_HALO, :] = jnp.where(i > 0, glu(zh_ref[...]), 0.0)
        ext[0, CONV_HALO:, :] = u0
        _shifted_copies(ext, tm)
        off = CONV_HALO - (CONV_WIDTH - 1)
        for r in range(tm // 64):
            for cb in range(c // 128):
                cs = slice(cb * 128, (cb + 1) * 128)
                u1_ref[r * 64:(r + 1) * 64, cs] = _sum_taps(
                    _tap_rows(ext, r * 64 + off + k, 64, cs) * w_ref[k:k + 1, cs] for k in range(CONV_WIDTH)) + bdw_ref[:, cs]
        u1 = u1_ref[...]
        mu = jnp.mean(u1, axis=-1, keepdims=True)
        xc = u1 - mu
        y = xc * lax.rsqrt(jnp.mean(xc * xc, axis=-1, keepdims=True) + EPS)
        y = y * lg_ref[...] + lb_ref[...]
        u_ref[...] = (y * jax.nn.sigmoid(y)).astype(BF16)

    row = lambda i: (i, 0)
    fix = lambda i: (0, 0)
    return pl.pallas_call(
        body, grid=(s // tm,),
        in_specs=[pl.BlockSpec((tm, 2 * c), row), pl.BlockSpec((CONV_HALO, 2 * c), _prev_halo(tm, CONV_HALO)),
                  pl.BlockSpec((1, 2 * c), fix), pl.BlockSpec((32, c), fix), pl.BlockSpec((1, c), fix),
                  pl.BlockSpec((1, c), fix), pl.BlockSpec((1, c), fix)],
        out_specs=[pl.BlockSpec((tm, c), row)] * 3,
        out_shape=[SDS((s, c), BF16), SDS((s, c), F32), SDS((s, c), F32)],
        scratch_shapes=[pltpu.VMEM((8, tm + CONV_HALO, c), F32)],
        compiler_params=_cp(1), name="conv_fwd")(z, z, b_in, w32, b_dw, ln_g, ln_b)


def _conv_bwd_ln(d_u, u1, ln_g, ln_b, tm):
    s = d_u.shape[0]
    c = CONV_CH

    def body(du_ref, u1_ref, lg_ref, lb_ref, du1_ref, dlg_ref, dlb_ref, dbdw_ref):
        @pl.when(pl.program_id(0) == 0)
        def _():
            dlg_ref[...] = jnp.zeros_like(dlg_ref)
            dlb_ref[...] = jnp.zeros_like(dlb_ref)
            dbdw_ref[...] = jnp.zeros_like(dbdw_ref)

        u1 = u1_ref[...]
        mu = jnp.mean(u1, axis=-1, keepdims=True)
        xc = u1 - mu
        rs = lax.rsqrt(jnp.mean(xc * xc, axis=-1, keepdims=True) + EPS)
        xh = xc * rs
        y = xh * lg_ref[...] + lb_ref[...]
        sg = jax.nn.sigmoid(y)
        dy = du_ref[...] * (sg * (1.0 + y * (1.0 - sg)))
        dlg_ref[...] += jnp.sum(dy * xh, axis=0, keepdims=True)
        dlb_ref[...] += jnp.sum(dy, axis=0, keepdims=True)
        dxh = dy * lg_ref[...]
        du1 = rs * (dxh - jnp.mean(dxh, axis=-1, keepdims=True) - xh * jnp.mean(dxh * xh, axis=-1, keepdims=True))
        dbdw_ref[...] += jnp.sum(du1, axis=0, keepdims=True)
        du1_ref[...] = du1

    row = lambda i: (i, 0)
    fix = lambda i: (0, 0)
    return pl.pallas_call(
        body, grid=(s // tm,),
        in_specs=[pl.BlockSpec((tm, c), row), pl.BlockSpec((tm, c), row), pl.BlockSpec((1, c), fix), pl.BlockSpec((1, c), fix)],
        out_specs=[pl.BlockSpec((tm, c), row)] + [pl.BlockSpec((1, c), fix)] * 3,
        out_shape=[SDS((s, c), F32)] + [SDS((1, c), F32)] * 3,
        compiler_params=_cp(1), name="conv_bwd_ln")(d_u, u1, ln_g, ln_b)


def _conv_bwd_dw(d_u1, u0, z, b_in, w32, tm):
    s = d_u1.shape[0]
    c = CONV_CH

    def body(d_ref, dn_ref, u0_ref, u0p_ref, z_ref, bin_ref, w_ref, dz_ref, dw_ref, dbin_ref, extd, extu, du0):
        i = pl.program_id(0)
        last = pl.num_programs(0) - 1

        @pl.when(i == 0)
        def _():
            dw_ref[...] = jnp.zeros_like(dw_ref)
            dbin_ref[...] = jnp.zeros_like(dbin_ref)

        extd[0, 0:tm, :] = d_ref[...]
        extd[0, tm:, :] = jnp.where(i < last, dn_ref[...], 0.0)
        extu[0, 0:CONV_HALO, :] = jnp.where(i > 0, u0p_ref[...], 0.0)
        extu[0, CONV_HALO:, :] = u0_ref[...]
        _shifted_copies(extd, tm)
        _shifted_copies(extu, tm)
        off = CONV_HALO - (CONV_WIDTH - 1)
        for r in range(tm // 64):
            for cb in range(c // 128):
                cs = slice(cb * 128, (cb + 1) * 128)
                du0[r * 64:(r + 1) * 64, cs] = _sum_taps(
                    _tap_rows(extd, r * 64 + (CONV_WIDTH - 1) - k, 64, cs) * w_ref[k:k + 1, cs] for k in range(CONV_WIDTH))
        for cb in range(c // 128):
            cs = slice(cb * 128, (cb + 1) * 128)
            for k in range(CONV_WIDTH):
                parts = []
                for r in range(tm // 64):
                    p = d_ref[r * 64:(r + 1) * 64, cs] * _tap_rows(extu, r * 64 + off + k, 64, cs)
                    parts.append(_sum_taps(p[q * 8:(q + 1) * 8, :] for q in range(8)))
                dw_ref[k:k + 1, cs] += jnp.sum(_sum_taps(parts), axis=0, keepdims=True)
        zz = z_ref[...] + bin_ref[...]
        a = zz[:, :c]
        sg = jax.nn.sigmoid(zz[:, c:])
        d0 = du0[...]
        da = d0 * sg
        dgt = d0 * a * (sg * (1.0 - sg))
        dbin_ref[:, :c] += jnp.sum(da, axis=0, keepdims=True)
        dbin_ref[:, c:] += jnp.sum(dgt, axis=0, keepdims=True)
        dz_ref[:, :c] = da.astype(BF16)
        dz_ref[:, c:] = dgt.astype(BF16)

    row = lambda i: (i, 0)
    fix = lambda i: (0, 0)
    return pl.pallas_call(
        body, grid=(s // tm,),
        in_specs=[pl.BlockSpec((tm, c), row), pl.BlockSpec((CONV_HALO, c), _next_halo(tm, CONV_HALO, s)),
                  pl.BlockSpec((tm, c), row), pl.BlockSpec((CONV_HALO, c), _prev_halo(tm, CONV_HALO)),
                  pl.BlockSpec((tm, 2 * c), row), pl.BlockSpec((1, 2 * c), fix), pl.BlockSpec((32, c), fix)],
        out_specs=[pl.BlockSpec((tm, 2 * c), row), pl.BlockSpec((32, c), fix), pl.BlockSpec((1, 2 * c), fix)],
        out_shape=[SDS((s, 2 * c), BF16), SDS((32, c), F32), SDS((1, 2 * c), F32)],
        scratch_shapes=[pltpu.VMEM((8, tm + CONV_HALO, c), F32), pltpu.VMEM((8, tm + CONV_HALO, c), F32), pltpu.VMEM((tm, c), F32)],
        compiler_params=_cp(1), name="conv_bwd_dw")(d_u1, d_u1, u0, u0, z, b_in, w32)


def _partner(v, lane):
    up = pltpu.roll(v, HEAD_PAD - ROPE // 2, 1)
    dn = pltpu.roll(v, ROPE // 2, 1)
    lo = (lane >= NOPE) & (lane < NOPE + ROPE // 2)
    hi = (lane >= NOPE + ROPE // 2) & (lane < HEAD_DIM)
    return jnp.where(lo, up, jnp.where(hi, dn, 0.0))


def _mla_prep(q_raw, kv_raw, z, cosf, sinf, gq, gk, tm):
    s = q_raw.shape[0]

    def body(q_ref, kv_ref, kr_ref, c_ref, s_ref, gq_ref, gk_ref, qo_ref, ko_ref, vo_ref):
        lane = lax.broadcasted_iota(jnp.int32, (tm, HEAD_PAD), 1)
        cf = c_ref[...]
        sf = s_ref[...]

        def norm_rope(t, g_ref):
            r = lax.rsqrt(jnp.sum(t * t, axis=-1, keepdims=True) * (1.0 / HEAD_DIM) + EPS)
            tn = (t * r) * g_ref[...]
            return tn * cf + _partner(tn, lane) * sf

        kr = kr_ref[...]
        for h in range(HEADS):
            hs = slice(h * HEAD_PAD, (h + 1) * HEAD_PAD)
            qo_ref[:, hs] = (norm_rope(q_ref[:, hs], gq_ref) * (ATT_SCALE * LOG2E)).astype(BF16)
            kv = kv_ref[:, hs]
            ko_ref[:, hs] = norm_rope(jnp.where(lane < NOPE, kv, 0.0) + kr, gk_ref).astype(BF16)
            vo_ref[:, hs] = jnp.where(lane >= NOPE, kv, 0.0).astype(BF16)

    row = lambda i: (i, 0)
    wide = pl.BlockSpec((tm, HEADS * HEAD_PAD), row)
    one = pl.BlockSpec((tm, HEAD_PAD), row)
    return pl.pallas_call(
        body, grid=(s // tm,),
        in_specs=[wide, wide, pl.BlockSpec((tm, HEAD_PAD), lambda i: (i, IN_COLS_PAD // HEAD_PAD - 1)), one, one,
                  pl.BlockSpec((1, HEAD_PAD), lambda i: (0, 0)), pl.BlockSpec((1, HEAD_PAD), lambda i: (0, 0))],
        out_specs=[wide] * 3,
        out_shape=[SDS((s, HEADS * HEAD_PAD), BF16)] * 3,
        compiler_params=_cp(1), name="mla_prep")(q_raw, kv_raw, z, cosf, sinf, gq, gk)


def _mla_prep_bwd(dqp, dkp, dvp, q_raw, kv_raw, z, cosf, sinf, gq, gk, tm):
    s = q_raw.shape[0]

    def body(dq_ref, dk_ref, dv_ref, q_ref, kv_ref, kr_ref, c_ref, s_ref, gq_ref, gk_ref,
             dqo_ref, dkvo_ref, dkr_ref, dgq_ref, dgk_ref):
        lane = lax.broadcasted_iota(jnp.int32, (tm, HEAD_PAD), 1)
        cf = c_ref[...]
        sf = s_ref[...]

        @pl.when(pl.program_id(0) == 0)
        def _():
            dgq_ref[...] = jnp.zeros_like(dgq_ref)
            dgk_ref[...] = jnp.zeros_like(dgk_ref)

        def norm_rope_bwd(t, d_out, g_ref, dg_ref):
            r = lax.rsqrt(jnp.sum(t * t, axis=-1, keepdims=True) * (1.0 / HEAD_DIM) + EPS)
            th = t * r
            dn = d_out * cf + _partner(d_out * sf, lane)
            dg_ref[...] += jnp.sum(dn * th, axis=0, keepdims=True)
            dh = dn * g_ref[...]
            return r * (dh - th * (jnp.sum(dh * th, axis=-1, keepdims=True) * (1.0 / HEAD_DIM)))

        kr = kr_ref[...]
        dkr = None
        for h in range(HEADS):
            hs = slice(h * HEAD_PAD, (h + 1) * HEAD_PAD)
            dq = norm_rope_bwd(q_ref[:, hs], dq_ref[:, hs] * ATT_SCALE, gq_ref, dgq_ref)
            dqo_ref[:, hs] = dq.astype(BF16)
            kv = kv_ref[:, hs]
            dkpre = norm_rope_bwd(jnp.where(lane < NOPE, kv, 0.0) + kr, dk_ref[:, hs] * LN2, gk_ref, dgk_ref)
            dkvo_ref[:, hs] = jnp.where(lane < NOPE, dkpre, dv_ref[:, hs]).astype(BF16)
            dkr_h = jnp.where((lane >= NOPE) & (lane < HEAD_DIM), dkpre, 0.0)
            dkr = dkr_h if dkr is None else dkr + dkr_h
        dkr_ref[...] = dkr

    row = lambda i: (i, 0)
    fix = lambda i: (0, 0)
    wide = pl.BlockSpec((tm, HEADS * HEAD_PAD), row)
    one = pl.BlockSpec((tm, HEAD_PAD), row)
    return pl.pallas_call(
        body, grid=(s // tm,),
        in_specs=[wide, wide, wide, wide, wide, pl.BlockSpec((tm, HEAD_PAD), lambda i: (i, IN_COLS_PAD // HEAD_PAD - 1)),
                  one, one, pl.BlockSpec((1, HEAD_PAD), fix), pl.BlockSpec((1, HEAD_PAD), fix)],
        out_specs=[wide, wide, one, pl.BlockSpec((1, HEAD_PAD), fix), pl.BlockSpec((1, HEAD_PAD), fix)],
        out_shape=[SDS((s, HEADS * HEAD_PAD), BF16), SDS((s, HEADS * HEAD_PAD), BF16), SDS((s, HEAD_PAD), F32),
                   SDS((1, HEAD_PAD), F32), SDS((1, HEAD_PAD), F32)],
        compiler_params=_cp(1), name="mla_prep_bwd")(dqp, dkp, dvp, q_raw, kv_raw, z, cosf, sinf, gq, gk)


ATT_GROUP = 4


def _group_schedule(nb, forward):
    g = ATT_GROUP
    one, two, case = [], [], []
    for a in range(nb):
        for b in (range(a // g + 1) if forward else range(a // g, nb // g)):
            one.append(a)
            two.append(b)
            case.append(0 if b != a // g else 1 + a % g)
    return tuple(jnp.asarray(np.array(x, np.int32)) for x in (one, two, case))


STRIP = 64


def _fold8(x):
    acc = x[0:8, :]
    for g in range(1, x.shape[0] // 8):
        acc = acc + x[g * 8:(g + 1) * 8, :]
    return acc


def _attn_fwd(qp, kp, vp, tb, w_late):
    s = qp.shape[0]
    ii, jj, cc = _group_schedule(s // tb, True)
    n_steps = int(ii.shape[0])

    def body(ii_ref, jj_ref, cc_ref, q_ref, k_ref, v_ref, wl_ref, of_ref, ob_ref, lse_ref, gl_ref,
             m_sc, l_sc, acc_sc, st_sc, pt_sc, send_sems, recv_sems):
        t = pl.program_id(1)
        case = cc_ref[t]

        @pl.when((pl.program_id(0) == 0) & (t == 0))
        def _():
            _ag_send(wl_ref, gl_ref, send_sems, recv_sems)

        @pl.when(jj_ref[t] == 0)
        def _():
            m_sc[...] = jnp.full_like(m_sc, -jnp.inf)
            l_sc[...] = jnp.zeros_like(l_sc)
            acc_sc[...] = jnp.zeros_like(acc_sc)

        def step(n_keys, diag_at):
            def visible(r):
                if diag_at is None or r * STRIP <= diag_at:
                    return None
                col = lax.broadcasted_iota(jnp.int32, (STRIP, tb), 1)
                return col >= r * STRIP - diag_at

            st_sc[0:n_keys, :] = lax.dot_general(k_ref[0:n_keys, :], q_ref[...], (((1,), (1,)), ((), ())), preferred_element_type=F32)
            mx = None
            for r in range(n_keys // STRIP):
                sc = st_sc[r * STRIP:(r + 1) * STRIP, :]
                if visible(r) is not None:
                    sc = jnp.where(visible(r), sc, -jnp.inf)
                m8 = sc[0:8, :]
                for g in range(1, STRIP // 8):
                    m8 = jnp.maximum(m8, sc[g * 8:(g + 1) * 8, :])
                mx = m8 if mx is None else jnp.maximum(mx, m8)
            m_old = m_sc[0:1, :]
            m_new = jnp.maximum(m_old, jnp.max(mx, axis=0, keepdims=True))
            alpha = jnp.exp2(m_old - m_new)
            ps = None
            pv = None
            for b in range(n_keys // tb):
                for r in range(b * tb // STRIP, (b + 1) * tb // STRIP):
                    p = jnp.exp2(st_sc[r * STRIP:(r + 1) * STRIP, :] - m_new)
                    if visible(r) is not None:
                        p = jnp.where(visible(r), p, 0.0)
                    ps = _fold8(p) if ps is None else ps + _fold8(p)
                    pt_sc[r * STRIP:(r + 1) * STRIP, :] = p.astype(BF16)
                pvb = lax.dot_general(v_ref[b * tb:(b + 1) * tb, :], pt_sc[b * tb:(b + 1) * tb, :], (((0,), (0,)), ((), ())),
                                      preferred_element_type=F32)
                pv = pvb if pv is None else pv + pvb
            l_new = alpha * l_sc[0:1, :] + jnp.sum(ps, axis=0, keepdims=True)
            m_sc[...] = jnp.broadcast_to(m_new, m_sc.shape)
            l_sc[...] = jnp.broadcast_to(l_new, l_sc.shape)
            acc_sc[...] = alpha * acc_sc[...] + pv

        @pl.when(case == 0)
        def _():
            step(ATT_GROUP * tb, None)

        for d in range(ATT_GROUP):
            @pl.when(case == 1 + d)
            def _(d=d):
                step((d + 1) * tb, d * tb)

        @pl.when(case != 0)
        def _():
            l = l_sc[0:1, :]
            o = (acc_sc[...] / l).T
            of_ref[...] = o
            ob_ref[...] = o.astype(BF16)
            lse_ref[...] = m_sc[0:1, :] + jnp.log(l) * LOG2E

        @pl.when((pl.program_id(0) == HEADS - 1) & (t == n_steps - 1))
        def _():
            _ag_finish(wl_ref, gl_ref, send_sems, recv_sems)

    qmap = lambda h, t, ii_ref, jj_ref, cc_ref: (ii_ref[t], h)
    kmap = lambda h, t, ii_ref, jj_ref, cc_ref: (jj_ref[t], h)
    gs = pltpu.PrefetchScalarGridSpec(
        num_scalar_prefetch=3, grid=(HEADS, n_steps),
        in_specs=[pl.BlockSpec((tb, HEAD_PAD), qmap), pl.BlockSpec((ATT_GROUP * tb, HEAD_PAD), kmap),
                  pl.BlockSpec((ATT_GROUP * tb, HEAD_PAD), kmap), ANY],
        out_specs=[pl.BlockSpec((tb, HEAD_PAD), qmap), pl.BlockSpec((tb, HEAD_PAD), qmap),
                   pl.BlockSpec((None, 1, tb), lambda h, t, ii_ref, jj_ref, cc_ref: (h, 0, ii_ref[t])), ANY],
        scratch_shapes=[pltpu.VMEM((8, tb), F32), pltpu.VMEM((8, tb), F32), pltpu.VMEM((HEAD_PAD, tb), F32),
                        pltpu.VMEM((ATT_GROUP * tb, tb), F32), pltpu.VMEM((ATT_GROUP * tb, tb), BF16),
                        pltpu.SemaphoreType.DMA((6,)), pltpu.SemaphoreType.DMA((6,))])
    w = HEADS * HEAD_PAD
    return pl.pallas_call(
        body, grid_spec=gs,
        out_shape=[SDS((s, w), F32), SDS((s, w), BF16), SDS((HEADS, 1, s), F32), SDS((4,) + w_late.shape, w_late.dtype)],
        compiler_params=_cp(2), name="attn_fwd")(ii, jj, cc, qp, kp, vp, w_late)


def _attn_delta(do, o, tb):
    s = do.shape[0]

    def body(do_ref, o_ref, d_ref):
        for h in range(HEADS):
            hs = slice(h * HEAD_PAD, (h + 1) * HEAD_PAD)
            d_ref[h] = jnp.sum((do_ref[:, hs] * o_ref[:, hs]).T, axis=0, keepdims=True)

    blk = pl.BlockSpec((tb, HEADS * HEAD_PAD), lambda i: (i, 0))
    return pl.pallas_call(body, grid=(s // tb,), in_specs=[blk, blk],
                          out_specs=pl.BlockSpec((HEADS, 1, tb), lambda i: (0, 0, i)),
                          out_shape=SDS((HEADS, 1, s), F32), compiler_params=_cp(1), name="attn_delta")(do, o)


def _attn_bwd(qp, kp, vp, dob, lse, delta, tb, wire):
    s = qp.shape[0]
    nb = s // tb
    jj, ii, cc = _group_schedule(nb, False)
    gw = ATT_GROUP * tb
    rows = 32

    n_steps = int(ii.shape[0])

    def body(jj_ref, ii_ref, cc_ref, q_ref, k_ref, v_ref, do_ref, lse_ref, dl_ref, cw_ref, dq_ref, dk_ref, dv_ref, rcv_ref,
             st_sc, dpt_sc, pt_sc, dst_sc, send_sems, recv_sems):
        t = pl.program_id(1)
        case = cc_ref[t]
        pair = ii_ref[t]

        @pl.when((pl.program_id(0) == 0) & (t == 0))
        def _():
            _rs_send(cw_ref, rcv_ref, send_sems, recv_sems)

        @pl.when((pl.program_id(0) == HEADS - 1) & (t == n_steps - 1))
        def _():
            _rs_wait(rcv_ref, send_sems, recv_sems)

        @pl.when(t == 0)
        def _():
            dq_ref[...] = jnp.zeros_like(dq_ref)

        @pl.when(case != 0)
        def _():
            dk_ref[...] = jnp.zeros_like(dk_ref)
            dv_ref[...] = jnp.zeros_like(dv_ref)

        def step(lo, width, diag):
            q = q_ref[lo:lo + width, :]
            do = do_ref[lo:lo + width, :]
            k = k_ref[...]
            st_sc[:, 0:width] = lax.dot_general(k, q, (((1,), (1,)), ((), ())), preferred_element_type=F32)
            dpt_sc[:, 0:width] = lax.dot_general(v_ref[...], do, (((1,), (1,)), ((), ())), preferred_element_type=F32)
            dv = dk = None
            for h in range(width // tb):
                ls = slice(h * tb, (h + 1) * tb)
                lse_row = lse_ref[:, lo + h * tb:lo + (h + 1) * tb]
                dl_row = dl_ref[:, lo + h * tb:lo + (h + 1) * tb]
                for r in range(tb // rows):
                    rs = slice(r * rows, (r + 1) * rows)
                    p = jnp.exp2(st_sc[rs, ls] - lse_row)
                    first_visible = (r * rows) // CHUNK * CHUNK
                    if diag and h == 0 and first_visible > 0:
                        col = lax.broadcasted_iota(jnp.int32, (rows, tb), 1)
                        p = jnp.where(col >= first_visible, p, 0.0)
                    ds = p * (dpt_sc[rs, ls] - dl_row)
                    pt_sc[rs, ls] = p.astype(BF16)
                    dst_sc[rs, ls] = ds.astype(BF16)
                dst = dst_sc[:, ls]
                dvh = jnp.dot(pt_sc[:, ls], do[ls, :], preferred_element_type=F32)
                dkh = jnp.dot(dst, q[ls, :], preferred_element_type=F32)
                dv = dvh if dv is None else dv + dvh
                dk = dkh if dk is None else dk + dkh
                dq_ref[ATT_GROUP * pair + lo // tb + h] += lax.dot_general(k, dst, (((0,), (0,)), ((), ())), preferred_element_type=F32)
            dv_ref[...] += dv
            dk_ref[...] += dk

        @pl.when(case == 0)
        def _():
            step(0, ATT_GROUP * tb, False)

        for d in range(ATT_GROUP):
            @pl.when(case == 1 + d)
            def _(d=d):
                step(d * tb, (ATT_GROUP - d) * tb, True)

    qmap = lambda h, t, jj_ref, ii_ref, cc_ref: (ii_ref[t], h)
    kmap = lambda h, t, jj_ref, ii_ref, cc_ref: (jj_ref[t], h)
    rowmap = lambda h, t, jj_ref, ii_ref, cc_ref: (h, 0, ii_ref[t])
    gs = pltpu.PrefetchScalarGridSpec(
        num_scalar_prefetch=3, grid=(HEADS, n_steps),
        in_specs=[pl.BlockSpec((gw, HEAD_PAD), qmap), pl.BlockSpec((tb, HEAD_PAD), kmap), pl.BlockSpec((tb, HEAD_PAD), kmap),
                  pl.BlockSpec((gw, HEAD_PAD), qmap), pl.BlockSpec((None, 1, gw), rowmap), pl.BlockSpec((None, 1, gw), rowmap), ANY],
        out_specs=[pl.BlockSpec((None, nb, HEAD_PAD, tb), lambda h, t, jj_ref, ii_ref, cc_ref: (h, 0, 0, 0)),
                   pl.BlockSpec((tb, HEAD_PAD), kmap), pl.BlockSpec((tb, HEAD_PAD), kmap), ANY],
        scratch_shapes=[pltpu.VMEM((tb, gw), F32), pltpu.VMEM((tb, gw), F32), pltpu.VMEM((tb, gw), BF16),
                        pltpu.VMEM((tb, gw), BF16), pltpu.SemaphoreType.DMA((3,)), pltpu.SemaphoreType.DMA((3,))])
    w = HEADS * HEAD_PAD
    dqt, dk, dv, recv = pl.pallas_call(
        body, grid_spec=gs,
        out_shape=[SDS((HEADS, nb, HEAD_PAD, tb), F32), SDS((s, w), F32), SDS((s, w), F32), SDS((3,) + wire.shape[1:], wire.dtype)],
        compiler_params=_cp(2), name="attn_bwd")(jj, ii, cc, qp, kp, vp, dob, lse, delta, wire)
    return jnp.transpose(dqt, (1, 3, 0, 2)).reshape(s, w), dk, dv, recv


def _head_norm(t, g):
    r = lax.rsqrt(jnp.mean(t * t, axis=-1, keepdims=True) + EPS)
    th = t * r
    return r, th, th * g


def _softmax_rows(sc):
    m = jnp.max(sc, axis=-1, keepdims=True)
    e = jnp.exp(sc - m)
    return e / jnp.sum(e, axis=-1, keepdims=True)


def _memattn_fwd(qm, kvm, gq, gk, tm):
    s = qm.shape[0]
    hd = MEM_HEAD_DIM

    def body(q_ref, k_ref, v_ref, gq_ref, gk_ref, o_ref):
        _, _, qn = _head_norm(q_ref[...], gq_ref[...])
        _, _, kn = _head_norm(k_ref[...], gk_ref[...])
        sc = lax.dot_general(qn.astype(BF16), kn.astype(BF16), (((1,), (1,)), ((), ())), preferred_element_type=F32)
        p = _softmax_rows(sc * (1.0 / math.sqrt(hd)))
        o_ref[...] = jnp.dot(p.astype(BF16), v_ref[...].astype(BF16), preferred_element_type=F32).astype(BF16)

    fix = lambda i, h: (0, 0)
    return pl.pallas_call(
        body, grid=(s // tm, MEM_HEADS),
        in_specs=[pl.BlockSpec((tm, hd), lambda i, h: (i, h)), pl.BlockSpec((MEM_LEN, hd), lambda i, h: (0, h)),
                  pl.BlockSpec((MEM_LEN, hd), lambda i, h: (0, MEM_HEADS + h)), pl.BlockSpec((1, hd), fix), pl.BlockSpec((1, hd), fix)],
        out_specs=pl.BlockSpec((tm, hd), lambda i, h: (i, h)),
        out_shape=SDS((s, MEM_HEADS * hd), BF16), compiler_params=_cp(2), name="memattn_fwd")(qm, kvm, kvm, gq, gk)


def _memattn_bwd(qm, kvm, d_o, gq, gk, tm):
    s = qm.shape[0]
    hd = MEM_HEAD_DIM

    def body(q_ref, k_ref, v_ref, do_ref, gq_ref, gk_ref, dq_ref, dk_ref, dv_ref, dgq_ref, dgk_ref, dkn_sc):
        h = pl.program_id(0)
        i = pl.program_id(1)
        last = pl.num_programs(1) - 1

        @pl.when((h == 0) & (i == 0))
        def _():
            dgq_ref[...] = jnp.zeros_like(dgq_ref)
            dgk_ref[...] = jnp.zeros_like(dgk_ref)

        @pl.when(i == 0)
        def _():
            dv_ref[...] = jnp.zeros_like(dv_ref)
            dkn_sc[...] = jnp.zeros_like(dkn_sc)

        rq, qh, qn = _head_norm(q_ref[...], gq_ref[...])
        rk, kh, kn = _head_norm(k_ref[...], gk_ref[...])
        qnb = qn.astype(BF16)
        knb = kn.astype(BF16)
        scale = 1.0 / math.sqrt(hd)
        sc = lax.dot_general(qnb, knb, (((1,), (1,)), ((), ())), preferred_element_type=F32)
        p = _softmax_rows(sc * scale)
        do = do_ref[...].astype(BF16)
        dp = lax.dot_general(do, v_ref[...].astype(BF16), (((1,), (1,)), ((), ())), preferred_element_type=F32)
        dv_ref[...] += lax.dot_general(p.astype(BF16), do, (((0,), (0,)), ((), ())), preferred_element_type=F32)
        ds = ((p * (dp - jnp.sum(dp * p, axis=-1, keepdims=True))) * scale).astype(BF16)
        dqn = jnp.dot(ds, knb, preferred_element_type=F32)
        dkn_sc[...] += lax.dot_general(ds, qnb, (((0,), (0,)), ((), ())), preferred_element_type=F32)
        dgq_ref[...] += jnp.sum(dqn * qh, axis=0, keepdims=True)
        dqh = dqn * gq_ref[...]
        dq_ref[...] = (rq * (dqh - qh * jnp.mean(dqh * qh, axis=-1, keepdims=True))).astype(BF16)

        @pl.when(i == last)
        def _():
            dkn = dkn_sc[...]
            dgk_ref[...] += jnp.sum(dkn * kh, axis=0, keepdims=True)
            dkh = dkn * gk_ref[...]
            dk_ref[...] = rk * (dkh - kh * jnp.mean(dkh * kh, axis=-1, keepdims=True))

    fix = lambda h, i: (0, 0)
    qb = pl.BlockSpec((tm, hd), lambda h, i: (i, h))
    kb = pl.BlockSpec((MEM_LEN, hd), lambda h, i: (0, h))
    return pl.pallas_call(
        body, grid=(MEM_HEADS, s // tm),
        in_specs=[qb, kb, pl.BlockSpec((MEM_LEN, hd), lambda h, i: (0, MEM_HEADS + h)), qb,
                  pl.BlockSpec((1, hd), fix), pl.BlockSpec((1, hd), fix)],
        out_specs=[qb, kb, kb, pl.BlockSpec((1, hd), fix), pl.BlockSpec((1, hd), fix)],
        out_shape=[SDS((s, MEM_HEADS * hd), BF16), SDS((MEM_LEN, MEM_HEADS * hd), F32), SDS((MEM_LEN, MEM_HEADS * hd), F32),
                   SDS((1, hd), F32), SDS((1, hd), F32)],
        scratch_shapes=[pltpu.VMEM((MEM_LEN, hd), F32)],
        compiler_params=_cp(2), name="memattn_bwd")(qm, kvm, kvm, d_o, gq, gk)


def _ffn_specs(tm, tn, nbj, s, order_ji):
    if order_ji:
        ij = lambda f: (lambda j, i: f(i, j))
    else:
        ij = lambda f: f
    prev = lambda i: jnp.maximum(i * (tm // FFN_HALO) - 1, 0)
    cur_g = pl.BlockSpec((tm, tn), ij(lambda i, j: (i, j)))
    cur_v = pl.BlockSpec((tm, tn), ij(lambda i, j: (i, j + nbj)))
    halo_g = pl.BlockSpec((FFN_HALO, tn), ij(lambda i, j: (prev(i), j)))
    halo_v = pl.BlockSpec((FFN_HALO, tn), ij(lambda i, j: (prev(i), j + nbj)))
    w_g = pl.BlockSpec((8, tn), ij(lambda i, j: (0, j)))
    w_v = pl.BlockSpec((8, tn), ij(lambda i, j: (0, j + nbj)))
    b_g = pl.BlockSpec((1, tn), ij(lambda i, j: (0, j)))
    b_v = pl.BlockSpec((1, tn), ij(lambda i, j: (0, j + nbj)))
    return cur_g, cur_v, halo_g, halo_v, w_g, w_v, b_g, b_v


FFN_STRIP = 16


def _conv3_rows(ext, w_ref, b_ref, o, n):
    return (w_ref[0:1, :] * ext[FFN_HALO - 2 + o:FFN_HALO - 2 + o + n, :] + w_ref[1:2, :] * ext[FFN_HALO - 1 + o:FFN_HALO - 1 + o + n, :]
            + w_ref[2:3, :] * ext[FFN_HALO + o:FFN_HALO + o + n, :] + b_ref[...])


def _ffn_fwd(up0, w8, b, tm, tn):
    s = up0.shape[0]
    nbj = D_FF // tn

    def body(g_ref, v_ref, gh_ref, vh_ref, wg_ref, wv_ref, bg_ref, bv_ref, act_ref, extg, extv):
        first = pl.program_id(0) == 0
        for ext, h_ref, c_ref in ((extg, gh_ref, g_ref), (extv, vh_ref, v_ref)):
            ext[0:FFN_HALO, :] = jnp.where(first, 0.0, h_ref[...])
            ext[FFN_HALO:, :] = c_ref[...]
        for r in range(tm // FFN_STRIP):
            o = r * FFN_STRIP
            ug = _conv3_rows(extg, wg_ref, bg_ref, o, FFN_STRIP)
            uv = _conv3_rows(extv, wv_ref, bv_ref, o, FFN_STRIP)
            act_ref[o:o + FFN_STRIP, :] = ((ug * jax.nn.sigmoid(ug)) * uv).astype(BF16)

    specs = _ffn_specs(tm, tn, nbj, s, False)
    return pl.pallas_call(
        body, grid=(s // tm, nbj), in_specs=list(specs),
        out_specs=pl.BlockSpec((tm, tn), lambda i, j: (i, j)), out_shape=SDS((s, D_FF), BF16),
        scratch_shapes=[pltpu.VMEM((tm + FFN_HALO, tn), F32), pltpu.VMEM((tm + FFN_HALO, tn), F32)],
        compiler_params=_cp(2), name="ffn_fwd")(up0, up0, up0, up0, w8, w8, b, b)


def _ffn_bwd(d_act, up0, w8, b, tm, tn):
    s = up0.shape[0]
    nbj = D_FF // tn
    te = tm + FFN_HALO

    def body(da_ref, dan_ref, g_ref, v_ref, gh_ref, vh_ref, gn_ref, vn_ref, wg_ref, wv_ref, bg_ref, bv_ref,
             og_ref, ov_ref, dbg_ref, dbv_ref, dwg_ref, dwv_ref, extg, extv, extdg, extdv, accg, accv):
        i = pl.program_id(1)
        first = i == 0
        last = i == pl.num_programs(1) - 1

        @pl.when(first)
        def _():
            for r in (dbg_ref, dbv_ref, dwg_ref, dwv_ref):
                r[...] = jnp.zeros_like(r)

        for ext, h_ref, c_ref, n_ref in ((extg, gh_ref, g_ref, gn_ref), (extv, vh_ref, v_ref, vn_ref)):
            ext[0:FFN_HALO, :] = jnp.where(first, 0.0, h_ref[...])
            ext[FFN_HALO:FFN_HALO + tm, :] = c_ref[...]
            ext[FFN_HALO + tm:, :] = n_ref[...]

        def fold8(x):
            acc = x[0:8, :]
            for q in range(1, x.shape[0] // 8):
                acc = acc + x[q * 8:(q + 1) * 8, :]
            return acc

        def taps(ext, o, n):
            return [ext[FFN_HALO - 2 + k + o:FFN_HALO - 2 + k + o + n, :] for k in range(3)]

        accg[...] = jnp.zeros_like(accg)
        accv[...] = jnp.zeros_like(accv)

        def gate_bwd(o, n, da, own_rows):
            xg, xv = taps(extg, o, n), taps(extv, o, n)
            ug = wg_ref[0:1, :] * xg[0] + wg_ref[1:2, :] * xg[1] + wg_ref[2:3, :] * xg[2] + bg_ref[...]
            uv = wv_ref[0:1, :] * xv[0] + wv_ref[1:2, :] * xv[1] + wv_ref[2:3, :] * xv[2] + bv_ref[...]
            sg = jax.nn.sigmoid(ug)
            dgt = da * uv * (sg * (1.0 + ug * (1.0 - sg)))
            dvl = da * (ug * sg)
            extdg[o:o + n, :] = dgt
            extdv[o:o + n, :] = dvl
            if own_rows:
                for acc, d, x in ((accg, dgt, xg), (accv, dvl, xv)):
                    acc[0] += fold8(d)
                    for k in range(3):
                        acc[1 + k] += fold8(d * x[k])

        for r in range(tm // FFN_STRIP):
            gate_bwd(r * FFN_STRIP, FFN_STRIP, da_ref[r * FFN_STRIP:(r + 1) * FFN_STRIP, :], True)
        gate_bwd(tm, FFN_HALO, jnp.where(last, 0.0, dan_ref[...]), False)

        for extd, w_ref, o_ref, db_ref, dw_ref, acc in ((extdg, wg_ref, og_ref, dbg_ref, dwg_ref, accg),
                                                        (extdv, wv_ref, ov_ref, dbv_ref, dwv_ref, accv)):
            for r in range(tm // FFN_STRIP):
                o = r * FFN_STRIP
                o_ref[o:o + FFN_STRIP, :] = (w_ref[2:3, :] * extd[o:o + FFN_STRIP, :] + w_ref[1:2, :] * extd[o + 1:o + 1 + FFN_STRIP, :]
                                             + w_ref[0:1, :] * extd[o + 2:o + 2 + FFN_STRIP, :]).astype(BF16)
            db_ref[...] += jnp.sum(acc[0], axis=0, keepdims=True)
            for k in range(3):
                dw_ref[k:k + 1, :] += jnp.sum(acc[1 + k], axis=0, keepdims=True)

    cur_g, cur_v, halo_g, halo_v, w_g, w_v, b_g, b_v = _ffn_specs(tm, tn, nbj, s, True)
    nxt_row = lambda i: jnp.minimum((i + 1) * (tm // FFN_HALO), s // FFN_HALO - 1)
    cur = pl.BlockSpec((tm, tn), lambda j, i: (i, j))
    nxt = pl.BlockSpec((FFN_HALO, tn), lambda j, i: (nxt_row(i), j))
    nxt_v = pl.BlockSpec((FFN_HALO, tn), lambda j, i: (nxt_row(i), j + nbj))
    acc1 = pl.BlockSpec((1, tn), lambda j, i: (0, j))
    acc8 = pl.BlockSpec((8, tn), lambda j, i: (0, j))
    return pl.pallas_call(
        body, grid=(nbj, s // tm), in_specs=[cur, nxt, cur_g, cur_v, halo_g, halo_v, nxt, nxt_v, w_g, w_v, b_g, b_v],
        out_specs=[cur, cur, acc1, acc1, acc8, acc8],
        out_shape=[SDS((s, D_FF), BF16), SDS((s, D_FF), BF16), SDS((1, D_FF), F32), SDS((1, D_FF), F32),
                   SDS((8, D_FF), F32), SDS((8, D_FF), F32)],
        scratch_shapes=[pltpu.VMEM((tm + 2 * FFN_HALO, tn), F32), pltpu.VMEM((tm + 2 * FFN_HALO, tn), F32),
                        pltpu.VMEM((te, tn), F32), pltpu.VMEM((te, tn), F32),
                        pltpu.VMEM((4, 8, tn), F32), pltpu.VMEM((4, 8, tn), F32)],
        compiler_params=_cp(2), name="ffn_bwd")(d_act, d_act, up0, up0, up0, up0, up0, up0, w8, w8, b, b)


def _down_loss(act, w_down, x2, target, tm):
    s = act.shape[0]

    def body(a_ref, w_ref, x_ref, t_ref, dyf_ref, dyb_ref, ls_ref):
        @pl.when(pl.program_id(0) == 0)
        def _():
            ls_ref[...] = jnp.zeros_like(ls_ref)

        y = x_ref[...] + jnp.dot(a_ref[...], w_ref[...], preferred_element_type=F32)
        e = y - t_ref[...]
        ls_ref[...] += jnp.sum(e * e)
        dy = e * (1.0 / D_MODEL)
        dyf_ref[...] = dy
        dyb_ref[...] = dy.astype(BF16)

    row = lambda i: (i, 0)
    return pl.pallas_call(
        body, grid=(s // tm,),
        in_specs=[pl.BlockSpec((tm, D_FF), row), pl.BlockSpec((D_FF, D_MODEL), lambda i: (0, 0)),
                  pl.BlockSpec((tm, D_MODEL), row), pl.BlockSpec((tm, D_MODEL), row)],
        out_specs=[pl.BlockSpec((tm, D_MODEL), row), pl.BlockSpec((tm, D_MODEL), row), pl.BlockSpec((8, 128), lambda i: (0, 0))],
        out_shape=[SDS((s, D_MODEL), F32), SDS((s, D_MODEL), BF16), SDS((8, 128), F32)],
        compiler_params=_cp(1), name="down_loss")(act, w_down, x2, target)


def _adamw_math(w, g, m, v):
    mn = ADAM_B1 * m + (1.0 - ADAM_B1) * g
    vn = ADAM_B2 * v + (1.0 - ADAM_B2) * (g * g)
    m_hat = mn / (1.0 - ADAM_B1 ** ADAM_STEP)
    v_hat = vn / (1.0 - ADAM_B2 ** ADAM_STEP)
    return -ADAM_LR * (m_hat / (jnp.sqrt(v_hat) + ADAM_EPS) + ADAM_WD * w), mn, vn


def _adamw(w, g, m, v, name):
    rows, cols = w.shape
    tr = rows if rows <= 256 else (256 if rows % 256 == 0 else rows // 2)

    def body(w_ref, g_ref, m_ref, v_ref, d_ref, mo_ref, vo_ref):
        d_ref[...], mo_ref[...], vo_ref[...] = _adamw_math(w_ref[...], g_ref[...], m_ref[...], v_ref[...])

    blk = pl.BlockSpec((tr, cols), lambda i: (i, 0))
    return pl.pallas_call(body, grid=(rows // tr,), in_specs=[blk] * 4, out_specs=[blk] * 3,
                          out_shape=[SDS((rows, cols), F32)] * 3, compiler_params=_cp(1), name=name)(w, g, m, v)


def _adamw_small(ws, gs, ms, vs):
    n = len(ws)

    def body(*refs):
        ins, outs = refs[:4 * n], refs[4 * n:]
        for k in range(n):
            d, mn, vn = _adamw_math(ins[k][...], ins[n + k][...], ins[2 * n + k][...], ins[3 * n + k][...])
            outs[k][...] = d
            outs[n + k][...] = mn
            outs[2 * n + k][...] = vn

    vm = pl.BlockSpec(memory_space=pltpu.VMEM)
    outs = pl.pallas_call(body, in_specs=[vm] * (4 * n), out_specs=[vm] * (3 * n),
                          out_shape=[SDS(w.shape, F32) for w in ws] * 3, name="adamw_small")(*ws, *gs, *ms, *vs)
    return outs[:n], outs[n:2 * n], outs[2 * n:]


ANY = pl.BlockSpec(memory_space=pl.ANY)


def _coords():
    return lax.axis_index("x"), lax.axis_index("y"), lax.axis_index("c")


def _other_chips(x, y):
    return [(1 - x, y), (x, 1 - y), (1 - x, 1 - y)]


D2D_CHUNKS = 8
ICI_CHUNKS = 4


def _row_chunks(n_rows, n_chunks, align):
    step = -(-n_rows // (n_chunks * align)) * align
    return [(r, min(step, n_rows - r)) for r in range(0, n_rows, step)]


def _ag_copy(out_ref, send_sems, recv_sems, k, shard, base, r0, nr, to, src=None):
    rows_ = pl.ds(pl.multiple_of(base + r0, 16), nr)
    dst = out_ref.at[shard, rows_]
    return pltpu.make_async_remote_copy(src_ref=dst if src is None else src.at[rows_], dst_ref=dst, send_sem=send_sems.at[k],
                                        recv_sem=recv_sems.at[k], device_id=to, device_id_type=MESH)


def _ag_send(w_ref, out_ref, send_sems, recv_sems):
    x, y, c = _coords()
    half_rows = w_ref.shape[0] // 2
    for k, (px, py) in enumerate(_other_chips(x, y)):
        for r0, nr in _row_chunks(half_rows, ICI_CHUNKS, 16):
            _ag_copy(out_ref, send_sems, recv_sems, k, 2 * x + y, c * half_rows, r0, nr, (px, py, c), src=w_ref).start()


def _ag_finish(w_ref, out_ref, send_sems, recv_sems):
    x, y, c = _coords()
    half_rows = w_ref.shape[0] // 2
    chips = _other_chips(x, y)
    sibling = (x, y, 1 - c)
    for k, (px, py) in enumerate(chips):
        _ag_copy(out_ref, send_sems, recv_sems, k, 2 * px + py, c * half_rows, 0, half_rows, (px, py, c)).wait_recv()
        for r0, nr in _row_chunks(half_rows, ICI_CHUNKS, 16):
            _ag_copy(out_ref, send_sems, recv_sems, 3 + k, 2 * px + py, c * half_rows, r0, nr, sibling).start()
    for k, (px, py) in enumerate(chips):
        _ag_copy(out_ref, send_sems, recv_sems, 3 + k, 2 * px + py, (1 - c) * half_rows, 0, half_rows, sibling).wait_recv()
    for k in range(6):
        _ag_copy(out_ref, send_sems, recv_sems, k, 2 * x + y, c * half_rows, 0, half_rows, sibling).wait_send()


def _ag_weights(wsh):
    rows, cols = wsh.shape

    def body(w_ref, out_ref, send_sems, recv_sems):
        _ag_send(w_ref, out_ref, send_sems, recv_sems)
        _ag_finish(w_ref, out_ref, send_sems, recv_sems)

    return pl.pallas_call(
        body, in_specs=[ANY], out_specs=ANY, out_shape=SDS((4, rows, cols), wsh.dtype),
        scratch_shapes=[pltpu.SemaphoreType.DMA((6,)), pltpu.SemaphoreType.DMA((6,))],
        name="ag_weights")(wsh)


def _rs_swap_halves(gfull, tag):
    n_sh, rows, cols = gfull.shape
    half_rows = rows // 2

    def body(g_ref, recv_ref, send_sem, recv_sem):
        x, y, c = _coords()
        sib_base = (1 - c) * half_rows
        for sh in range(n_sh):
            for r0, nr in _row_chunks(half_rows, D2D_CHUNKS, 8):
                pltpu.make_async_remote_copy(
                    src_ref=g_ref.at[sh, pl.ds(pl.multiple_of(sib_base + r0, 8), nr)], dst_ref=recv_ref.at[sh, pl.ds(r0, nr)],
                    send_sem=send_sem, recv_sem=recv_sem, device_id=(x, y, 1 - c), device_id_type=MESH).start()
        pltpu.make_async_remote_copy(src_ref=recv_ref, dst_ref=recv_ref, send_sem=send_sem, recv_sem=recv_sem,
                                     device_id=(x, y, 1 - c), device_id_type=MESH).wait()

    return pl.pallas_call(
        body, in_specs=[ANY], out_specs=ANY, out_shape=SDS((n_sh, half_rows, cols), gfull.dtype),
        scratch_shapes=[pltpu.SemaphoreType.DMA, pltpu.SemaphoreType.DMA], name="rs_swap_halves" + tag)(gfull)


def _rs_add_pair(gfull, recv, core, tr, tag):
    n_sh, rows, cols = gfull.shape
    half_rows = rows // 2
    nblk = half_rows // tr

    def body(c_ref, g_ref, r_ref, o_ref, ob_ref):
        acc = g_ref[...] + r_ref[...]
        o_ref[...] = acc
        ob_ref[...] = acc.astype(BF16)

    out = pl.BlockSpec((None, tr, cols), lambda sh, i, c_ref: (sh, i, 0))
    gs = pltpu.PrefetchScalarGridSpec(
        num_scalar_prefetch=1, grid=(n_sh, nblk),
        in_specs=[pl.BlockSpec((None, tr, cols), lambda sh, i, c_ref: (sh, c_ref[0] * nblk + i, 0)), out],
        out_specs=[out, out])
    return pl.pallas_call(body, grid_spec=gs, out_shape=[SDS((n_sh, half_rows, cols), F32), SDS((n_sh, half_rows, cols), BF16)],
                          compiler_params=_cp(2), name="rs_add_pair" + tag)(core, gfull, recv)


def _rs_send(cs_ref, recv_ref, send_sems, recv_sems):
    x, y, c = _coords()
    half_rows = cs_ref.shape[1]
    for k, (px, py) in enumerate(_other_chips(x, y)):
        for r0, nr in _row_chunks(half_rows, ICI_CHUNKS, 16):
            pltpu.make_async_remote_copy(
                src_ref=cs_ref.at[2 * px + py, pl.ds(r0, nr)], dst_ref=recv_ref.at[k, pl.ds(r0, nr)],
                send_sem=send_sems.at[k], recv_sem=recv_sems.at[k], device_id=(px, py, c), device_id_type=MESH).start()


def _rs_wait(recv_ref, send_sems, recv_sems):
    x, y, c = _coords()
    for k, (px, py) in enumerate(_other_chips(x, y)):
        pltpu.make_async_remote_copy(src_ref=recv_ref.at[k], dst_ref=recv_ref.at[k], send_sem=send_sems.at[k],
                                     recv_sem=recv_sems.at[k], device_id=(px, py, c), device_id_type=MESH).wait()


def _rs_to_owner(chipsum):
    n_sh, half_rows, cols = chipsum.shape

    def body(cs_ref, recv_ref, send_sems, recv_sems):
        _rs_send(cs_ref, recv_ref, send_sems, recv_sems)
        _rs_wait(recv_ref, send_sems, recv_sems)

    return pl.pallas_call(
        body, in_specs=[ANY], out_specs=ANY, out_shape=SDS((3, half_rows, cols), chipsum.dtype),
        scratch_shapes=[pltpu.SemaphoreType.DMA((3,)), pltpu.SemaphoreType.DMA((3,))], name="rs_to_owner")(chipsum)


def _rs_add_chips(chipsum, recv, shard_core, tr, tag):
    _, half_rows, cols = chipsum.shape

    def body(s_ref, m_ref, r0_ref, r1_ref, r2_ref, o_ref):
        o_ref[...] = ((m_ref[...] + r0_ref[...].astype(F32)) + r1_ref[...].astype(F32)) + r2_ref[...].astype(F32)

    gs = pltpu.PrefetchScalarGridSpec(
        num_scalar_prefetch=1, grid=(half_rows // tr,),
        in_specs=[pl.BlockSpec((None, tr, cols), lambda i, s_ref: (s_ref[0], i, 0))]
        + [pl.BlockSpec((None, tr, cols), (lambda k: lambda i, s_ref: (k, i, 0))(k)) for k in range(3)],
        out_specs=pl.BlockSpec((None, tr, cols), lambda i, s_ref: (s_ref[1], i, 0)))
    return pl.pallas_call(body, grid_spec=gs, out_shape=SDS((2, half_rows, cols), F32),
                          compiler_params=_cp(1), name="rs_add_chips" + tag)(shard_core, chipsum, recv, recv, recv)


def _rs_join_halves(buf, tag):
    _, half_rows, cols = buf.shape

    def body(b_ref, out_ref, send_sem, recv_sem):
        x, y, c = _coords()
        for r0, nr in _row_chunks(half_rows, D2D_CHUNKS, 8):
            pltpu.make_async_remote_copy(src_ref=out_ref.at[c, pl.ds(r0, nr)], dst_ref=out_ref.at[c, pl.ds(r0, nr)], send_sem=send_sem,
                                         recv_sem=recv_sem, device_id=(x, y, 1 - c), device_id_type=MESH).start()
        pltpu.make_async_remote_copy(src_ref=out_ref.at[c], dst_ref=out_ref.at[c], send_sem=send_sem, recv_sem=recv_sem,
                                     device_id=(x, y, 1 - c), device_id_type=MESH).wait()

    return pl.pallas_call(
        body, in_specs=[ANY], out_specs=ANY, out_shape=SDS(buf.shape, buf.dtype), input_output_aliases={0: 0},
        scratch_shapes=[pltpu.SemaphoreType.DMA, pltpu.SemaphoreType.DMA], name="rs_join_halves" + tag)(buf)


BIG = [("w_in", (1024, 1440), 1), ("w_uq", (256, 768), 1), ("w_ukv", (128, 1024), 1), ("w_out", (1024, 1024), 0),
       ("w_mem_q", (1024, 1024), 0), ("w_mem_kv", (1024, 2048), 1), ("w_mem_o", (1024, 1024), 0),
       ("w_up", (1024, 5632), 1), ("w_down", (2816, 1024), 0)]
SMALL_REP = [("mix_norm_g", 1024), ("b_conv_in", 1024), ("b_conv_dw", 512), ("conv_ln_g", 512), ("conv_ln_b", 512),
             ("q_lat_norm_g", 256), ("kv_lat_norm_g", 128), ("q_norm_g", 96), ("k_norm_g", 96), ("mem_norm_x_g", 1024),
             ("mem_norm_m_g", 1024), ("mem_q_norm_g", 256), ("mem_k_norm_g", 256), ("ffn_norm_g", 1024), ("b_ffn_dw", 5632)]
SMALL_SH = [("w_conv_dw", (31, 512)), ("w_ffn_dw", (3, 5632))]


def _shard_shape(shape, axis):
    return tuple(d // 4 if a == axis else d for a, d in enumerate(shape))


def _pack_rows(parts, rows, cols):
    flat = jnp.concatenate([p.reshape(-1) for p in parts])
    flat = jnp.pad(flat, (0, rows * cols - flat.shape[0]))
    return flat.reshape(rows, cols)


AG_EARLY, AG_LATE = BIG[:3], BIG[3:]
RS_REST, RS_FFN = AG_EARLY, AG_LATE


def _group_rows(group):
    used = sum(_shard_shape(shape, axis)[0] * _shard_shape(shape, axis)[1] // PACK_COLS for _, shape, axis in group)
    return -(-used // 512) * 512


def _pick_rows(n, cap=384):
    return max(r for r in range(16, cap + 1, 16) if n % r == 0)


def _pack_big_shards(ws, group):
    parts = [ws[n].reshape(-1, PACK_COLS) for n, _, _ in group]
    used = sum(p.shape[0] for p in parts)
    pad = _group_rows(group) - used
    return jnp.concatenate(parts + ([jnp.zeros((pad, PACK_COLS), parts[0].dtype)] if pad else []), axis=0)


def _unpack_big_shards(packed, group):
    out, r = {}, 0
    for n, shape, axis in group:
        sh = _shard_shape(shape, axis)
        nr = sh[0] * sh[1] // PACK_COLS
        out[n] = packed[r:r + nr].reshape(sh)
        r += nr
    return out


def _unpack_gathered(g, group):
    out, r = {}, 0
    for n, shape, axis in group:
        sh = _shard_shape(shape, axis)
        nr = sh[0] * sh[1] // PACK_COLS
        part = g[:, r:r + nr]
        if axis == 0:
            out[n] = part.reshape(shape)
        else:
            out[n] = part.reshape((4,) + sh).transpose(1, 0, 2).reshape(shape)
        r += nr
    return out


def _pack_full_grads(gs, group):
    parts = []
    for n, shape, axis in group:
        sh = _shard_shape(shape, axis)
        nr = sh[0] * sh[1] // PACK_COLS
        if axis == 0:
            parts.append(gs[n].reshape(4, nr, PACK_COLS))
        else:
            parts.append(gs[n].reshape(shape[0], 4, sh[1]).transpose(1, 0, 2).reshape(4, nr, PACK_COLS))
    pad = _group_rows(group) - sum(p.shape[1] for p in parts)
    return jnp.concatenate(parts + ([jnp.zeros((4, pad, PACK_COLS), F32)] if pad else []), axis=1)


def _rs_first(gfull, core_idx, tag):
    tr = _pick_rows(gfull.shape[1] // 2)
    return _rs_add_pair(gfull, _rs_swap_halves(gfull, tag), core_idx.reshape(1), tr, tag)


def _rs_last(chipsum, recv, shard_idx, core_idx, tag):
    tr = _pick_rows(chipsum.shape[1])
    red = _rs_add_chips(chipsum, recv, jnp.stack([shard_idx, core_idx]), tr, tag)
    return _rs_join_halves(red, tag).reshape(2 * chipsum.shape[1], chipsum.shape[2])


def _rope_tables(positions):
    inv_freq = ROPE_THETA ** (-jnp.arange(0, ROPE, 2, dtype=F32) / ROPE)
    ang = positions.astype(F32)[:, None] * inv_freq
    cos, sin = jnp.cos(ang), jnp.sin(ang)
    s = positions.shape[0]
    cosf = jnp.concatenate([jnp.ones((s, NOPE), F32), cos, cos, jnp.ones((s, HEAD_PAD - HEAD_DIM), F32)], axis=-1)
    sinf = jnp.concatenate([jnp.zeros((s, NOPE), F32), -sin, sin, jnp.zeros((s, HEAD_PAD - HEAD_DIM), F32)], axis=-1)
    return cosf, sinf


def _pad_heads(w, per_head):
    k = w.shape[0]
    w3 = w.reshape(k, HEADS, per_head)
    return jnp.pad(w3, ((0, 0), (0, 0), (0, HEAD_PAD - per_head))).reshape(k, HEADS * HEAD_PAD)


def _layer_grads(x, mem, positions, target, wf, w_late, sp, shard_idx, core_idx):
    wf = dict(wf)
    s = x.shape[0]
    tm = _row_tile(s, 512)
    tc = _row_tile(s, 256)
    tb = 512 if s % (512 * ATT_GROUP) == 0 else 128
    row2 = lambda a: a.reshape(1, -1)

    w_in = wf["w_in"]
    w_in_pad = jnp.concatenate([w_in[:, :1408], jnp.zeros((D_MODEL, NOPE), BF16), w_in[:, 1408:],
                                jnp.zeros((D_MODEL, HEAD_PAD - HEAD_DIM), BF16)], axis=1)
    w_uq_pad = _pad_heads(wf["w_uq"], HEAD_DIM)
    w_ukv = wf["w_ukv"]
    gq_pad = jnp.pad(sp["q_norm_g"], (0, HEAD_PAD - HEAD_DIM)).reshape(1, HEAD_PAD)
    gk_pad = jnp.pad(sp["k_norm_g"], (0, HEAD_PAD - HEAD_DIM)).reshape(1, HEAD_PAD)
    w_dw32 = jnp.pad(sp["w_conv_dw"], ((0, 1), (0, 0)))
    w_ffn8 = jnp.pad(sp["w_ffn_dw"], ((0, 5), (0, 0)))
    b_ffn = row2(sp["b_ffn_dw"])
    cosf, sinf = _rope_tables(positions)

    z, h1 = _norm_linear(x, 0, D_MODEL, row2(sp["mix_norm_g"]), w_in_pad, F32, tm, IN_COLS_PAD, "in_proj")
    u, u0, u1 = _conv_fwd(z, row2(sp["b_conv_in"]), w_dw32, row2(sp["b_conv_dw"]), row2(sp["conv_ln_g"]), row2(sp["conv_ln_b"]), tc)
    q_raw, cqn = _norm_linear(z, 1024 // Q_RANK, Q_RANK, row2(sp["q_lat_norm_g"]), w_uq_pad, F32, tm, 1024, "q_up")
    kv_raw, ckvn = _norm_linear(z, 1280 // KV_RANK, KV_RANK, row2(sp["kv_lat_norm_g"]), w_ukv, F32, tm, 1024, "kv_up")
    qp, kp, vp = _mla_prep(q_raw, kv_raw, z, cosf, sinf, gq_pad, gk_pad, tc)
    o_f, o_b, lse, gathered = _attn_fwd(qp, kp, vp, tb, w_late)
    wf.update(_unpack_gathered(lax.dynamic_update_index_in_dim(gathered, w_late, shard_idx, 0), AG_LATE))
    w_out_u = wf["w_out"][:CONV_CH]
    w_out_o = jnp.pad(wf["w_out"][CONV_CH:].reshape(HEADS, NOPE, D_MODEL), ((0, 0), (NOPE, 0), (0, 0))).reshape(HEADS * HEAD_PAD, D_MODEL)
    w_up_g, w_up_v = wf["w_up"][:, :D_FF], wf["w_up"][:, D_FF:]
    (x1,) = _linear([(u, w_out_u), (o_b, w_out_o)], False, x, [F32], tm, 1024, "out_proj")

    qm, hq = _norm_linear(x1, 0, D_MODEL, row2(sp["mem_norm_x_g"]), wf["w_mem_q"], F32, tm, 1024, "memq_proj")
    kvm, hm = _norm_linear(mem, 0, D_MODEL, row2(sp["mem_norm_m_g"]), wf["w_mem_kv"], F32, MEM_LEN, 1024, "memkv_proj")
    gmq, gmk = row2(sp["mem_q_norm_g"]), row2(sp["mem_k_norm_g"])
    o_m = _memattn_fwd(qm, kvm, gmq, gmk, tm)
    (x2,) = _linear([(o_m, wf["w_mem_o"])], False, x1, [F32], tm, 1024, "memo_proj")

    up0, h3 = _norm_linear(x2, 0, D_MODEL, row2(sp["ffn_norm_g"]), wf["w_up"], F32, _row_tile(s, 1024), D_FF // 2, "up_proj")
    act = _ffn_fwd(up0, w_ffn8, b_ffn, tc, D_FF // 2)
    dy_f, dy_b, lsum = _down_loss(act, wf["w_down"], x2, target, tm)

    g = {}
    (d_act,) = _linear([(dy_b, wf["w_down"])], True, None, [F32], tm, D_FF // 2, "down_bwd")
    g["w_down"] = _dw(act, dy_b, "dw_down")
    d_up0g, d_up0v, dbg, dbv, dwg, dwv = _ffn_bwd(d_act, up0, w_ffn8, b_ffn, tc, D_FF // 2)
    g["b_ffn_dw"] = jnp.concatenate([dbg, dbv], axis=1).reshape(-1)
    g["w_ffn_dw"] = jnp.concatenate([dwg[:3], dwv[:3]], axis=1)
    g["w_up"] = jnp.concatenate([_dw(h3, d_up0g, "dw_up_g"), _dw(h3, d_up0v, "dw_up_v")], axis=1)
    d_x2f, d_x2b, dg = _linear_normbwd([(d_up0g, w_up_g), (d_up0v, w_up_v)], x2, 0, row2(sp["ffn_norm_g"]), dy_f,
                                       [F32, BF16], tc, "up_bwd")
    g["ffn_norm_g"] = dg.reshape(-1)

    (d_om,) = _linear([(d_x2b, wf["w_mem_o"])], True, None, [BF16], tm, 1024, "memo_bwd")
    g["w_mem_o"] = _dw(o_m, d_x2b, "dw_mem_o")
    d_qm, d_km, d_vm, dgq, dgk = _memattn_bwd(qm, kvm, d_om, gmq, gmk, tm)
    g["mem_q_norm_g"], g["mem_k_norm_g"] = dgq.reshape(-1), dgk.reshape(-1)
    d_kvm = jnp.concatenate([d_km, d_vm], axis=1)
    g["w_mem_q"] = _dw(hq, d_qm, "dw_mem_q")
    g["w_mem_kv"] = _dw(hm, d_kvm, "dw_mem_kv")
    d_x1f, d_x1b, dg = _linear_normbwd([(d_qm, wf["w_mem_q"])], x1, 0, row2(sp["mem_norm_x_g"]), d_x2f, [F32, BF16], tm, "memq_bwd")
    g["mem_norm_x_g"] = dg.reshape(-1)
    _, dg = _linear_normbwd([(d_kvm, wf["w_mem_kv"])], mem, 0, row2(sp["mem_norm_m_g"]), None, [BF16], MEM_LEN, "memkv_bwd")
    g["mem_norm_m_g"] = dg.reshape(-1)

    (d_u,) = _linear([(d_x1b, w_out_u)], True, None, [F32], tm, CONV_CH, "out_bwd_u")
    d_of, d_ob = _linear([(d_x1b, w_out_o)], True, None, [F32, BF16], tm, 1024, "out_bwd_o")
    dw_out_u = _dw(u, d_x1b, "dw_out_u")
    dw_out_o = _dw(o_b, d_x1b, "dw_out_o")
    g["w_out"] = jnp.concatenate([dw_out_u, dw_out_o.reshape(HEADS, HEAD_PAD, D_MODEL)[:, NOPE:].reshape(HEADS * NOPE, D_MODEL)], axis=0)
    chipsum_ffn, wire_ffn = _rs_first(_pack_full_grads(g, RS_FFN), core_idx, "_ffn")
    delta = _attn_delta(d_of, o_f, tb)
    dqp, dkp, dvp, recv_ffn = _attn_bwd(qp, kp, vp, d_ob, lse, delta, tb, wire_ffn)
    g_ffn_packed = _rs_last(chipsum_ffn, recv_ffn, shard_idx, core_idx, "_ffn")
    d_qraw, d_kvraw, d_kr, dgq, dgk = _mla_prep_bwd(dqp, dkp, dvp, q_raw, kv_raw, z, cosf, sinf, gq_pad, gk_pad, tc)
    g["q_norm_g"], g["k_norm_g"] = dgq.reshape(-1)[:HEAD_DIM], dgk.reshape(-1)[:HEAD_DIM]
    g["w_uq"] = _dw(cqn, d_qraw, "dw_uq").reshape(Q_RANK, HEADS, HEAD_PAD)[:, :, :HEAD_DIM].reshape(Q_RANK, HEADS * HEAD_DIM)
    g["w_ukv"] = _dw(ckvn, d_kvraw, "dw_ukv")
    d_cq, dg = _linear_normbwd([(d_qraw, w_uq_pad)], z, 1024 // Q_RANK, row2(sp["q_lat_norm_g"]), None, [BF16], tm, "q_up_bwd")
    g["q_lat_norm_g"] = dg.reshape(-1)
    d_ckv, dg = _linear_normbwd([(d_kvraw, w_ukv)], z, 1280 // KV_RANK, row2(sp["kv_lat_norm_g"]), None, [BF16], tm, "kv_up_bwd")
    g["kv_lat_norm_g"] = dg.reshape(-1)
    d_u1, dlg, dlb, dbdw = _conv_bwd_ln(d_u, u1, row2(sp["conv_ln_g"]), row2(sp["conv_ln_b"]), tc)
    g["conv_ln_g"], g["conv_ln_b"], g["b_conv_dw"] = dlg.reshape(-1), dlb.reshape(-1), dbdw.reshape(-1)
    d_conv, dw_dw, dbin = _conv_bwd_dw(d_u1, u0, z, row2(sp["b_conv_in"]), w_dw32, tc)
    g["w_conv_dw"], g["b_conv_in"] = dw_dw[:CONV_WIDTH], dbin.reshape(-1)
    pieces = [(d_conv, w_in_pad[:, :1024]), (d_cq, w_in_pad[:, 1024:1280]), (d_ckv, w_in_pad[:, 1280:1408]), (d_kr, w_in_pad[:, 1408:])]
    dw_in = [_dw(h1, d, "dw_in_%d" % k) for k, (d, _) in enumerate(pieces)]
    g["w_in"] = jnp.concatenate([dw_in[0], dw_in[1], dw_in[2], dw_in[3][:, NOPE:HEAD_DIM]], axis=1)
    grad_x, dg = _linear_normbwd(pieces, x, 0, row2(sp["mix_norm_g"]), d_x1f, [F32], tm, "in_bwd")
    g["mix_norm_g"] = dg.reshape(-1)
    return lsum[0, 0], grad_x, g, g_ffn_packed


def kernel(x, mem, positions, mix_norm_g, w_in, b_conv_in, w_conv_dw, b_conv_dw, conv_ln_g, conv_ln_b, q_lat_norm_g, w_uq, kv_lat_norm_g, w_ukv, q_norm_g, k_norm_g, w_out, mem_norm_x_g, mem_norm_m_g, w_mem_q, w_mem_kv, mem_q_norm_g, mem_k_norm_g, w_mem_o, ffn_norm_g, w_up, w_ffn_dw, b_ffn_dw, w_down, loss_target, m_mix_norm_g, m_w_in, m_b_conv_in, m_w_conv_dw, m_b_conv_dw, m_conv_ln_g, m_conv_ln_b, m_q_lat_norm_g, m_w_uq, m_kv_lat_norm_g, m_w_ukv, m_q_norm_g, m_k_norm_g, m_w_out, m_mem_norm_x_g, m_mem_norm_m_g, m_w_mem_q, m_w_mem_kv, m_mem_q_norm_g, m_mem_k_norm_g, m_w_mem_o, m_ffn_norm_g, m_w_up, m_w_ffn_dw, m_b_ffn_dw, m_w_down, v_mix_norm_g, v_w_in, v_b_conv_in, v_w_conv_dw, v_b_conv_dw, v_conv_ln_g, v_conv_ln_b, v_q_lat_norm_g, v_w_uq, v_kv_lat_norm_g, v_w_ukv, v_q_norm_g, v_k_norm_g, v_w_out, v_mem_norm_x_g, v_mem_norm_m_g, v_w_mem_q, v_w_mem_kv, v_mem_q_norm_g, v_mem_k_norm_g, v_w_mem_o, v_ffn_norm_g, v_w_up, v_w_ffn_dw, v_b_ffn_dw, v_w_down):
    names = ["mix_norm_g", "w_in", "b_conv_in", "w_conv_dw", "b_conv_dw", "conv_ln_g", "conv_ln_b", "q_lat_norm_g", "w_uq",
             "kv_lat_norm_g", "w_ukv", "q_norm_g", "k_norm_g", "w_out", "mem_norm_x_g", "mem_norm_m_g", "w_mem_q", "w_mem_kv",
             "mem_q_norm_g", "mem_k_norm_g", "w_mem_o", "ffn_norm_g", "w_up", "w_ffn_dw", "b_ffn_dw", "w_down"]
    loc = locals()
    w = {n: loc[n] for n in names}
    m = {n: loc["m_" + n] for n in names}
    v = {n: loc["v_" + n] for n in names}
    shard_idx = 2 * lax.axis_index("x") + lax.axis_index("y")

    shard_idx = shard_idx.astype(jnp.int32)
    core_idx = lax.axis_index("c").astype(jnp.int32)

    w_local = {n: w[n][0] for n, _, _ in BIG}
    w_early = _pack_big_shards(w_local, AG_EARLY).astype(BF16)
    w_late = _pack_big_shards(w_local, AG_LATE).astype(BF16)
    wf = _unpack_gathered(lax.dynamic_update_index_in_dim(_ag_weights(w_early), w_early, shard_idx, 0), AG_EARLY)

    small_sh_full = {}
    gather_in = []
    for n, (r, c) in SMALL_SH:
        csh = c // 4
        slab = lax.dynamic_update_slice(jnp.zeros((r, c), F32), w[n][0], (0, shard_idx * csh))
        gather_in.append(slab.reshape(-1))
    gather_rows = 256
    gathered_small = _allreduce_small_named(_pack_rows(gather_in, gather_rows, SMALL_COLS), "gather_small") * 0.5
    off = 0
    for n, (r, c) in SMALL_SH:
        small_sh_full[n] = gathered_small.reshape(-1)[off:off + r * c].reshape(r, c)
        off += r * c
    sp = {n: w[n][0] for n, _ in SMALL_REP}
    sp.update(small_sh_full)

    lsum, grad_x, g, g_ffn_packed = _layer_grads(x[0], mem[0], positions[0], loss_target[0], wf, w_late, sp, shard_idx, core_idx)

    small_parts = [jnp.full((SMALL_COLS,), lsum, F32)] + [g[n] for n, _ in SMALL_REP] + [g[n] for n, _ in SMALL_SH]
    small_rows = 368
    small_sum = _allreduce_small_named(_pack_rows(small_parts, small_rows, SMALL_COLS), "allreduce_small").reshape(-1)
    loss = small_sum[0] * (0.5 / D_MODEL)
    gs = {}
    off = SMALL_COLS
    for n, sz in SMALL_REP:
        gs[n] = small_sum[off:off + sz].reshape(w[n].shape)
        off += sz
    for n, (r, c) in SMALL_SH:
        full = small_sum[off:off + r * c].reshape(r, c)
        gs[n] = lax.dynamic_slice(full, (0, shard_idx * (c // 4)), (r, c // 4)).reshape(w[n].shape)
        off += r * c

    chipsum, chipsum_wire = _rs_first(_pack_full_grads(g, RS_REST), core_idx, "_rest")
    g_rest_packed = _rs_last(chipsum, _rs_to_owner(chipsum_wire), shard_idx, core_idx, "_rest")
    g_big = {**_unpack_big_shards(g_rest_packed, RS_REST), **_unpack_big_shards(g_ffn_packed, RS_FFN)}
    gs.update({n: a[None] for n, a in g_big.items()})

    delta, new_m, new_v = {}, {}, {}
    for n, _, _ in BIG:
        d_n, m_n, v_n = _adamw(w[n][0], g_big[n], m[n][0], v[n][0], "adamw_" + n)
        delta[n], new_m[n], new_v[n] = d_n[None], m_n[None], v_n[None]
    small_names = [n for n, _ in SMALL_REP] + [n for n, _ in SMALL_SH]
    as2d = lambda a: a.reshape(-1, a.shape[-1])
    d_s, m_s, v_s = _adamw_small(*[[as2d(d[n]) for n in small_names] for d in (w, gs, m, v)])
    for k, n in enumerate(small_names):
        delta[n], new_m[n], new_v[n] = d_s[k].reshape(w[n].shape), m_s[k].reshape(w[n].shape), v_s[k].reshape(w[n].shape)

    return (loss, grad_x[None], *[gs[n] for n in names], *[delta[n] for n in names], *[new_m[n] for n in names],
            *[new_v[n] for n in names])


def _allreduce_small_named(v, name):
    rows, cols = v.shape

    def body(v_ref, out_ref, buf, send_sems, recv_sems):
        x, y, c = _coords()
        me = 4 * x + 2 * y + c
        buf[me] = v_ref[...]
        cps = []
        for r in range(1, 8):
            dx, dy, dc = (r >> 2) & 1, (r >> 1) & 1, r & 1
            to = (x + dx - 2 * x * dx, y + dy - 2 * y * dy, c + dc - 2 * c * dc)
            cp = pltpu.make_async_remote_copy(src_ref=v_ref, dst_ref=buf.at[me], send_sem=send_sems.at[r - 1],
                                              recv_sem=recv_sems.at[r - 1], device_id=to, device_id_type=MESH)
            cp.start()
            cps.append(cp)
        for cp in cps:
            cp.wait()
        acc = buf[0]
        for d in range(1, 8):
            acc = acc + buf[d]
        out_ref[...] = acc

    vm = pl.BlockSpec(memory_space=pltpu.VMEM)
    return pl.pallas_call(
        body, in_specs=[vm], out_specs=vm, out_shape=SDS((rows, cols), F32),
        scratch_shapes=[pltpu.VMEM((8, rows, cols), F32), pltpu.SemaphoreType.DMA((7,)), pltpu.SemaphoreType.DMA((7,))],
        name=name)(v)
```

```python
import math

import numpy as np
import jax
import jax.numpy as jnp
from jax import lax
from jax.experimental import pallas as pl
from jax.experimental.pallas import tpu as pltpu

F32 = jnp.float32
BF16 = jnp.bfloat16
SDS = jax.ShapeDtypeStruct
MESH = pl.DeviceIdType.MESH

D_MODEL = 1024
EPS = 1e-6
CONV_CH = 512
CONV_WIDTH = 31
CONV_HALO = 32
HEADS = 8
NOPE = 64
ROPE = 32
HEAD_DIM = NOPE + ROPE
HEAD_PAD = 128
Q_RANK = 256
KV_RANK = 128
CHUNK = 64
ROPE_THETA = 10000.0
IN_COLS_PAD = 1536
MEM_HEADS = 4
MEM_HEAD_DIM = 256
MEM_LEN = 256
D_FF = 2816
FFN_HALO = 8
ATT_SCALE = 1.0 / math.sqrt(HEAD_DIM)
LOG2E = math.log2(math.e)
LN2 = math.log(2.0)

ADAM_LR = 0.001
ADAM_B1 = 0.9
ADAM_B2 = 0.999
ADAM_EPS = 1e-08
ADAM_WD = 0.01
ADAM_STEP = 10

VMEM_LIMIT_V7X = 56 * 1024 * 1024
PACK_COLS = 1024
SMALL_COLS = 128


def _cp(n_axes):
    return pltpu.CompilerParams(dimension_semantics=("arbitrary",) * n_axes, vmem_limit_bytes=VMEM_LIMIT_V7X)


def _row_tile(s, want):
    return want if s % want == 0 else s


def _norm_linear(x, xcol, kdim, g, w, out_dtype, tm, tn, name):
    s = x.shape[0]
    n = w.shape[1]

    def body(x_ref, g_ref, w_ref, y_ref, hn_ref):
        @pl.when(pl.program_id(1) == 0)
        def _():
            xv = x_ref[...]
            r = lax.rsqrt(jnp.mean(xv * xv, axis=-1, keepdims=True) + EPS)
            hn_ref[...] = ((xv * r) * g_ref[...]).astype(BF16)

        y_ref[...] = jnp.dot(hn_ref[...], w_ref[...], preferred_element_type=F32).astype(y_ref.dtype)

    return pl.pallas_call(
        body, grid=(s // tm, n // tn),
        in_specs=[pl.BlockSpec((tm, kdim), lambda i, j: (i, xcol)), pl.BlockSpec((1, kdim), lambda i, j: (0, 0)),
                  pl.BlockSpec((kdim, tn), lambda i, j: (0, j))],
        out_specs=[pl.BlockSpec((tm, tn), lambda i, j: (i, j)), pl.BlockSpec((tm, kdim), lambda i, j: (i, 0))],
        out_shape=[SDS((s, n), out_dtype), SDS((s, kdim), BF16)],
        compiler_params=_cp(2), name=name)(x, g, w)


def _linear(pairs, nt, residual, out_dtypes, tm, tn, name):
    s = pairs[0][0].shape[0]
    n = pairs[0][1].shape[0] if nt else pairs[0][1].shape[1]
    n_pairs = len(pairs)
    has_res = residual is not None

    def body(*refs):
        a_refs = refs[:n_pairs]
        w_refs = refs[n_pairs:2 * n_pairs]
        res_ref = refs[2 * n_pairs] if has_res else None
        outs = refs[2 * n_pairs + int(has_res):]
        acc = None
        for a_ref, w_ref in zip(a_refs, w_refs):
            a = a_ref[...].astype(BF16)
            if nt:
                d = lax.dot_general(a, w_ref[...], (((1,), (1,)), ((), ())), preferred_element_type=F32)
            else:
                d = jnp.dot(a, w_ref[...], preferred_element_type=F32)
            acc = d if acc is None else acc + d
        if has_res:
            acc = res_ref[...] + acc
        for o in outs:
            o[...] = acc.astype(o.dtype)

    in_specs = [pl.BlockSpec((tm, a.shape[1]), lambda i, j: (i, 0)) for a, _ in pairs]
    if nt:
        in_specs += [pl.BlockSpec((tn, w.shape[1]), lambda i, j: (j, 0)) for _, w in pairs]
    else:
        in_specs += [pl.BlockSpec((w.shape[0], tn), lambda i, j: (0, j)) for _, w in pairs]
    args = [a for a, _ in pairs] + [w for _, w in pairs]
    if has_res:
        in_specs.append(pl.BlockSpec((tm, tn), lambda i, j: (i, j)))
        args.append(residual)
    outs = pl.pallas_call(
        body, grid=(s // tm, n // tn), in_specs=in_specs,
        out_specs=[pl.BlockSpec((tm, tn), lambda i, j: (i, j)) for _ in out_dtypes],
        out_shape=[SDS((s, n), dt) for dt in out_dtypes],
        compiler_params=_cp(2), name=name)(*args)
    return outs


def _linear_normbwd(pairs, x, xcol, g, d_res, out_dtypes, tm, name):
    s = pairs[0][0].shape[0]
    dn = pairs[0][1].shape[0]
    n_pairs = len(pairs)
    has_res = d_res is not None

    def body(*refs):
        a_refs = refs[:n_pairs]
        w_refs = refs[n_pairs:2 * n_pairs]
        x_ref, g_ref = refs[2 * n_pairs], refs[2 * n_pairs + 1]
        k = 2 * n_pairs + 2
        res_ref = refs[k] if has_res else None
        k += int(has_res)
        outs = refs[k:-1]
        dg_ref = refs[-1]
        dh = None
        for a_ref, w_ref in zip(a_refs, w_refs):
            d = lax.dot_general(a_ref[...].astype(BF16), w_ref[...], (((1,), (1,)), ((), ())), preferred_element_type=F32)
            dh = d if dh is None else dh + d
        xv = x_ref[...]
        r = lax.rsqrt(jnp.mean(xv * xv, axis=-1, keepdims=True) + EPS)
        y = xv * r

        @pl.when(pl.program_id(0) == 0)
        def _():
            dg_ref[...] = jnp.zeros_like(dg_ref)

        dg_ref[...] += jnp.sum(dh * y, axis=0, keepdims=True)
        dy = dh * g_ref[...]
        dx = r * (dy - y * jnp.mean(dy * y, axis=-1, keepdims=True))
        if has_res:
            dx = res_ref[...] + dx
        for o in outs:
            o[...] = dx.astype(o.dtype)

    in_specs = [pl.BlockSpec((tm, a.shape[1]), lambda i: (i, 0)) for a, _ in pairs]
    in_specs += [pl.BlockSpec((dn, w.shape[1]), lambda i: (0, 0)) for _, w in pairs]
    in_specs += [pl.BlockSpec((tm, dn), lambda i: (i, xcol)), pl.BlockSpec((1, dn), lambda i: (0, 0))]
    args = [a for a, _ in pairs] + [w for _, w in pairs] + [x, g]
    if has_res:
        in_specs.append(pl.BlockSpec((tm, dn), lambda i: (i, 0)))
        args.append(d_res)
    outs = pl.pallas_call(
        body, grid=(s // tm,), in_specs=in_specs,
        out_specs=[pl.BlockSpec((tm, dn), lambda i: (i, 0)) for _ in out_dtypes] + [pl.BlockSpec((1, dn), lambda i: (0, 0))],
        out_shape=[SDS((s, dn), dt) for dt in out_dtypes] + [SDS((1, dn), F32)],
        compiler_params=_cp(1), name=name)(*args)
    return outs


def _dw_matmul(a, b, tk, tn, ts, name):
    s, ka = a.shape
    n = b.shape[1]

    def body(a_ref, b_ref, o_ref):
        @pl.when(pl.program_id(2) == 0)
        def _():
            o_ref[...] = jnp.zeros_like(o_ref)

        o_ref[...] += lax.dot_general(a_ref[...].astype(BF16), b_ref[...].astype(BF16), (((0,), (0,)), ((), ())),
                                      preferred_element_type=F32)

    return pl.pallas_call(
        body, grid=(ka // tk, n // tn, s // ts),
        in_specs=[pl.BlockSpec((ts, tk), lambda k, j, t: (t, k)), pl.BlockSpec((ts, tn), lambda k, j, t: (t, j))],
        out_specs=pl.BlockSpec((tk, tn), lambda k, j, t: (k, j)),
        out_shape=SDS((ka, n), F32), compiler_params=_cp(3), name=name)(a, b)


def _dw(a, b, name):
    s, ka = a.shape
    n = b.shape[1]
    tk = ka if ka <= 1024 else ka // 2
    tn = n if n <= 1024 else (n // 2 if n == D_FF else 512)
    return _dw_matmul(a, b, tk, tn, _row_tile(s, 2048), name)


def _prev_halo(tm, halo):
    return lambda i: (jnp.maximum(i * (tm // halo) - 1, 0), 0)


def _next_halo(tm, halo, s):
    return lambda i: (jnp.minimum((i + 1) * (tm // halo), s // halo - 1), 0)


def _shifted_copies(ext, tm):
    n = tm + CONV_HALO - 8
    for s in range(1, 8):
        ext[s, 0:n, :] = ext[0, s:s + n, :]


def _sum_taps(terms, ways=4):
    accs = []
    for i, t in enumerate(terms):
        if i < ways:
            accs.append(t)
        else:
            accs[i % ways] = accs[i % ways] + t
    while len(accs) > 1:
        accs = [accs[i] + accs[i + 1] if i + 1 < len(accs) else accs[i] for i in range(0, len(accs), 2)]
    return accs[0]


def _tap_rows(ext, o, n, cs):
    return ext[o % 8, o - o % 8:o - o % 8 + n, cs]


def _conv_fwd(z, b_in, w32, b_dw, ln_g, ln_b, tm):
    s = z.shape[0]
    c = CONV_CH

    def body(z_ref, zh_ref, bin_ref, w_ref, bdw_ref, lg_ref, lb_ref, u_ref, u0_ref, u1_ref, ext):
        i = pl.program_id(0)

        def glu(zz):
            zz = zz + bin_ref[...]
            return zz[:, :c] * jax.nn.sigmoid(zz[:, c:])

        u0 = glu(z_ref[...])
        u0_ref[...] = u0
        ext[0, 0:CONV_HALO, :] = jnp.where(i > 0, glu(zh_ref[...]), 0.0)
        ext[0, CONV_HALO:, :] = u0
        _shifted_copies(ext, tm)
        off = CONV_HALO - (CONV_WIDTH - 1)
        for r in range(tm // 64):
            for cb in range(c // 128):
                cs = slice(cb * 128, (cb + 1) * 128)
                u1_ref[r * 64:(r + 1) * 64, cs] = _sum_taps(
                    _tap_rows(ext, r * 64 + off + k, 64, cs) * w_ref[k:k + 1, cs] for k in range(CONV_WIDTH)) + bdw_ref[:, cs]
        u1 = u1_ref[...]
        mu = jnp.mean(u1, axis=-1, keepdims=True)
        xc = u1 - mu
        y = xc * lax.rsqrt(jnp.mean(xc * xc, axis=-1, keepdims=True) + EPS)
        y = y * lg_ref[...] + lb_ref[...]
        u_ref[...] = (y * jax.nn.sigmoid(y)).astype(BF16)

    row = lambda i: (i, 0)
    fix = lambda i: (0, 0)
    return pl.pallas_call(
        body, grid=(s // tm,),
        in_specs=[pl.BlockSpec((tm, 2 * c), row), pl.BlockSpec((CONV_HALO, 2 * c), _prev_halo(tm, CONV_HALO)),
                  pl.BlockSpec((1, 2 * c), fix), pl.BlockSpec((32, c), fix), pl.BlockSpec((1, c), fix),
                  pl.BlockSpec((1, c), fix), pl.BlockSpec((1, c), fix)],
        out_specs=[pl.BlockSpec((tm, c), row)] * 3,
        out_shape=[SDS((s, c), BF16), SDS((s, c), F32), SDS((s, c), F32)],
        scratch_shapes=[pltpu.VMEM((8, tm + CONV_HALO, c), F32)],
        compiler_params=_cp(1), name="conv_fwd")(z, z, b_in, w32, b_dw, ln_g, ln_b)


def _conv_bwd_ln(d_u, u1, ln_g, ln_b, tm):
    s = d_u.shape[0]
    c = CONV_CH

    def body(du_ref, u1_ref, lg_ref, lb_ref, du1_ref, dlg_ref, dlb_ref, dbdw_ref):
        @pl.when(pl.program_id(0) == 0)
        def _():
            dlg_ref[...] = jnp.zeros_like(dlg_ref)
            dlb_ref[...] = jnp.zeros_like(dlb_ref)
            dbdw_ref[...] = jnp.zeros_like(dbdw_ref)

        u1 = u1_ref[...]
        mu = jnp.mean(u1, axis=-1, keepdims=True)
        xc = u1 - mu
        rs = lax.rsqrt(jnp.mean(xc * xc, axis=-1, keepdims=True) + EPS)
        xh = xc * rs
        y = xh * lg_ref[...] + lb_ref[...]
        sg = jax.nn.sigmoid(y)
        dy = du_ref[...] * (sg * (1.0 + y * (1.0 - sg)))
        dlg_ref[...] += jnp.sum(dy * xh, axis=0, keepdims=True)
        dlb_ref[...] += jnp.sum(dy, axis=0, keepdims=True)
        dxh = dy * lg_ref[...]
        du1 = rs * (dxh - jnp.mean(dxh, axis=-1, keepdims=True) - xh * jnp.mean(dxh * xh, axis=-1, keepdims=True))
        dbdw_ref[...] += jnp.sum(du1, axis=0, keepdims=True)
        du1_ref[...] = du1

    row = lambda i: (i, 0)
    fix = lambda i: (0, 0)
    return pl.pallas_call(
        body, grid=(s // tm,),
        in_specs=[pl.BlockSpec((tm, c), row), pl.BlockSpec((tm, c), row), pl.BlockSpec((1, c), fix), pl.BlockSpec((1, c), fix)],
        out_specs=[pl.BlockSpec((tm, c), row)] + [pl.BlockSpec((1, c), fix)] * 3,
        out_shape=[SDS((s, c), F32)] + [SDS((1, c), F32)] * 3,
        compiler_params=_cp(1), name="conv_bwd_ln")(d_u, u1, ln_g, ln_b)


def _conv_bwd_dw(d_u1, u0, z, b_in, w32, tm):
    s = d_u1.shape[0]
    c = CONV_CH

    def body(d_ref, dn_ref, u0_ref, u0p_ref, z_ref, bin_ref, w_ref, dz_ref, dw_ref, dbin_ref, extd, extu, du0):
        i = pl.program_id(0)
        last = pl.num_programs(0) - 1

        @pl.when(i == 0)
        def _():
            dw_ref[...] = jnp.zeros_like(dw_ref)
            dbin_ref[...] = jnp.zeros_like(dbin_ref)

        extd[0, 0:tm, :] = d_ref[...]
        extd[0, tm:, :] = jnp.where(i < last, dn_ref[...], 0.0)
        extu[0, 0:CONV_HALO, :] = jnp.where(i > 0, u0p_ref[...], 0.0)
        extu[0, CONV_HALO:, :] = u0_ref[...]
        _shifted_copies(extd, tm)
        _shifted_copies(extu, tm)
        off = CONV_HALO - (CONV_WIDTH - 1)
        for r in range(tm // 64):
            for cb in range(c // 128):
                cs = slice(cb * 128, (cb + 1) * 128)
                du0[r * 64:(r + 1) * 64, cs] = _sum_taps(
                    _tap_rows(extd, r * 64 + (CONV_WIDTH - 1) - k, 64, cs) * w_ref[k:k + 1, cs] for k in range(CONV_WIDTH))
        for cb in range(c // 128):
            cs = slice(cb * 128, (cb + 1) * 128)
            for k in range(CONV_WIDTH):
                parts = []
                for r in range(tm // 64):
                    p = d_ref[r * 64:(r + 1) * 64, cs] * _tap_rows(extu, r * 64 + off + k, 64, cs)
                    parts.append(_sum_taps(p[q * 8:(q + 1) * 8, :] for q in range(8)))
                dw_ref[k:k + 1, cs] += jnp.sum(_sum_taps(parts), axis=0, keepdims=True)
        zz = z_ref[...] + bin_ref[...]
        a = zz[:, :c]
        sg = jax.nn.sigmoid(zz[:, c:])
        d0 = du0[...]
        da = d0 * sg
        dgt = d0 * a * (sg * (1.0 - sg))
        dbin_ref[:, :c] += jnp.sum(da, axis=0, keepdims=True)
        dbin_ref[:, c:] += jnp.sum(dgt, axis=0, keepdims=True)
        dz_ref[:, :c] = da.astype(BF16)
        dz_ref[:, c:] = dgt.astype(BF16)

    row = lambda i: (i, 0)
    fix = lambda i: (0, 0)
    return pl.pallas_call(
        body, grid=(s // tm,),
        in_specs=[pl.BlockSpec((tm, c), row), pl.BlockSpec((CONV_HALO, c), _next_halo(tm, CONV_HALO, s)),
                  pl.BlockSpec((tm, c), row), pl.BlockSpec((CONV_HALO, c), _prev_halo(tm, CONV_HALO)),
                  pl.BlockSpec((tm, 2 * c), row), pl.BlockSpec((1, 2 * c), fix), pl.BlockSpec((32, c), fix)],
        out_specs=[pl.BlockSpec((tm, 2 * c), row), pl.BlockSpec((32, c), fix), pl.BlockSpec((1, 2 * c), fix)],
        out_shape=[SDS((s, 2 * c), BF16), SDS((32, c), F32), SDS((1, 2 * c), F32)],
        scratch_shapes=[pltpu.VMEM((8, tm + CONV_HALO, c), F32), pltpu.VMEM((8, tm + CONV_HALO, c), F32), pltpu.VMEM((tm, c), F32)],
        compiler_params=_cp(1), name="conv_bwd_dw")(d_u1, d_u1, u0, u0, z, b_in, w32)


def _partner(v, lane):
    up = pltpu.roll(v, HEAD_PAD - ROPE // 2, 1)
    dn = pltpu.roll(v, ROPE // 2, 1)
    lo = (lane >= NOPE) & (lane < NOPE + ROPE // 2)
    hi = (lane >= NOPE + ROPE // 2) & (lane < HEAD_DIM)
    return jnp.where(lo, up, jnp.where(hi, dn, 0.0))


def _mla_prep(q_raw, kv_raw, z, cosf, sinf, gq, gk, tm):
    s = q_raw.shape[0]

    def body(q_ref, kv_ref, kr_ref, c_ref, s_ref, gq_ref, gk_ref, qo_ref, ko_ref, vo_ref):
        lane = lax.broadcasted_iota(jnp.int32, (tm, HEAD_PAD), 1)
        cf = c_ref[...]
        sf = s_ref[...]

        def norm_rope(t, g_ref):
            r = lax.rsqrt(jnp.sum(t * t, axis=-1, keepdims=True) * (1.0 / HEAD_DIM) + EPS)
            tn = (t * r) * g_ref[...]
            return tn * cf + _partner(tn, lane) * sf

        kr = kr_ref[...]
        for h in range(HEADS):
            hs = slice(h * HEAD_PAD, (h + 1) * HEAD_PAD)
            qo_ref[:, hs] = (norm_rope(q_ref[:, hs], gq_ref) * (ATT_SCALE * LOG2E)).astype(BF16)
            kv = kv_ref[:, hs]
            ko_ref[:, hs] = norm_rope(jnp.where(lane < NOPE, kv, 0.0) + kr, gk_ref).astype(BF16)
            vo_ref[:, hs] = jnp.where(lane >= NOPE, kv, 0.0).astype(BF16)

    row = lambda i: (i, 0)
    wide = pl.BlockSpec((tm, HEADS * HEAD_PAD), row)
    one = pl.BlockSpec((tm, HEAD_PAD), row)
    return pl.pallas_call(
        body, grid=(s // tm,),
        in_specs=[wide, wide, pl.BlockSpec((tm, HEAD_PAD), lambda i: (i, IN_COLS_PAD // HEAD_PAD - 1)), one, one,
                  pl.BlockSpec((1, HEAD_PAD), lambda i: (0, 0)), pl.BlockSpec((1, HEAD_PAD), lambda i: (0, 0))],
        out_specs=[wide] * 3,
        out_shape=[SDS((s, HEADS * HEAD_PAD), BF16)] * 3,
        compiler_params=_cp(1), name="mla_prep")(q_raw, kv_raw, z, cosf, sinf, gq, gk)


def _mla_prep_bwd(dqp, dkp, dvp, q_raw, kv_raw, z, cosf, sinf, gq, gk, tm):
    s = q_raw.shape[0]

    def body(dq_ref, dk_ref, dv_ref, q_ref, kv_ref, kr_ref, c_ref, s_ref, gq_ref, gk_ref,
             dqo_ref, dkvo_ref, dkr_ref, dgq_ref, dgk_ref):
        lane = lax.broadcasted_iota(jnp.int32, (tm, HEAD_PAD), 1)
        cf = c_ref[...]
        sf = s_ref[...]

        @pl.when(pl.program_id(0) == 0)
        def _():
            dgq_ref[...] = jnp.zeros_like(dgq_ref)
            dgk_ref[...] = jnp.zeros_like(dgk_ref)

        def norm_rope_bwd(t, d_out, g_ref, dg_ref):
            r = lax.rsqrt(jnp.sum(t * t, axis=-1, keepdims=True) * (1.0 / HEAD_DIM) + EPS)
            th = t * r
            dn = d_out * cf + _partner(d_out * sf, lane)
            dg_ref[...] += jnp.sum(dn * th, axis=0, keepdims=True)
            dh = dn * g_ref[...]
            return r * (dh - th * (jnp.sum(dh * th, axis=-1, keepdims=True) * (1.0 / HEAD_DIM)))

        kr = kr_ref[...]
        dkr = None
        for h in range(HEADS):
            hs = slice(h * HEAD_PAD, (h + 1) * HEAD_PAD)
            dq = norm_rope_bwd(q_ref[:, hs], dq_ref[:, hs] * ATT_SCALE, gq_ref, dgq_ref)
            dqo_ref[:, hs] = dq.astype(BF16)
            kv = kv_ref[:, hs]
            dkpre = norm_rope_bwd(jnp.where(lane < NOPE, kv, 0.0) + kr, dk_ref[:, hs] * LN2, gk_ref, dgk_ref)
            dkvo_ref[:, hs] = jnp.where(lane < NOPE, dkpre, dv_ref[:, hs]).astype(BF16)
            dkr_h = jnp.where((lane >= NOPE) & (lane < HEAD_DIM), dkpre, 0.0)
            dkr = dkr_h if dkr is None else dkr + dkr_h
        dkr_ref[...] = dkr

    row = lambda i: (i, 0)
    fix = lambda i: (0, 0)
    wide = pl.BlockSpec((tm, HEADS * HEAD_PAD), row)
    one = pl.BlockSpec((tm, HEAD_PAD), row)
    return pl.pallas_call(
        body, grid=(s // tm,),
        in_specs=[wide, wide, wide, wide, wide, pl.BlockSpec((tm, HEAD_PAD), lambda i: (i, IN_COLS_PAD // HEAD_PAD - 1)),
                  one, one, pl.BlockSpec((1, HEAD_PAD), fix), pl.BlockSpec((1, HEAD_PAD), fix)],
        out_specs=[wide, wide, one, pl.BlockSpec((1, HEAD_PAD), fix), pl.BlockSpec((1, HEAD_PAD), fix)],
        out_shape=[SDS((s, HEADS * HEAD_PAD), BF16), SDS((s, HEADS * HEAD_PAD), BF16), SDS((s, HEAD_PAD), F32),
                   SDS((1, HEAD_PAD), F32), SDS((1, HEAD_PAD), F32)],
        compiler_params=_cp(1), name="mla_prep_bwd")(dqp, dkp, dvp, q_raw, kv_raw, z, cosf, sinf, gq, gk)


ATT_GROUP = 8


def _group_schedule(nb, forward):
    g = ATT_GROUP
    one, two, case = [], [], []
    for a in range(nb):
        for b in (range(a // g + 1) if forward else range(a // g, nb // g)):
            one.append(a)
            two.append(b)
            case.append(0 if b != a // g else 1 + a % g)
    return tuple(jnp.asarray(np.array(x, np.int32)) for x in (one, two, case))


STRIP = 64


def _fold8(x):
    acc = x[0:8, :]
    for g in range(1, x.shape[0] // 8):
        acc = acc + x[g * 8:(g + 1) * 8, :]
    return acc


def _attn_fwd(qp, kp, vp, tb, w_late):
    s = qp.shape[0]
    ii, jj, cc = _group_schedule(s // tb, True)
    n_steps = int(ii.shape[0])

    def body(ii_ref, jj_ref, cc_ref, q_ref, k_ref, v_ref, wl_ref, of_ref, ob_ref, lse_ref, gl_ref,
             m_sc, l_sc, acc_sc, st_sc, pt_sc, send_sems, recv_sems):
        t = pl.program_id(1)
        case = cc_ref[t]

        @pl.when((pl.program_id(0) == 0) & (t == 0))
        def _():
            _ag_send(wl_ref, gl_ref, send_sems, recv_sems)

        @pl.when(jj_ref[t] == 0)
        def _():
            m_sc[...] = jnp.full_like(m_sc, -jnp.inf)
            l_sc[...] = jnp.zeros_like(l_sc)
            acc_sc[...] = jnp.zeros_like(acc_sc)

        def step(n_keys, diag_at):
            def visible(r):
                if diag_at is None or r * STRIP <= diag_at:
                    return None
                col = lax.broadcasted_iota(jnp.int32, (STRIP, tb), 1)
                return col >= r * STRIP - diag_at

            st_sc[0:n_keys, :] = lax.dot_general(k_ref[0:n_keys, :], q_ref[...], (((1,), (1,)), ((), ())), preferred_element_type=F32)
            mx = None
            for r in range(n_keys // STRIP):
                sc = st_sc[r * STRIP:(r + 1) * STRIP, :]
                if visible(r) is not None:
                    sc = jnp.where(visible(r), sc, -jnp.inf)
                m8 = sc[0:8, :]
                for g in range(1, STRIP // 8):
                    m8 = jnp.maximum(m8, sc[g * 8:(g + 1) * 8, :])
                mx = m8 if mx is None else jnp.maximum(mx, m8)
            m_old = m_sc[0:1, :]
            m_new = jnp.maximum(m_old, jnp.max(mx, axis=0, keepdims=True))
            alpha = jnp.exp2(m_old - m_new)
            ps = None
            pv = None
            for b in range(n_keys // tb):
                for r in range(b * tb // STRIP, (b + 1) * tb // STRIP):
                    p = jnp.exp2(st_sc[r * STRIP:(r + 1) * STRIP, :] - m_new)
                    if visible(r) is not None:
                        p = jnp.where(visible(r), p, 0.0)
                    ps = _fold8(p) if ps is None else ps + _fold8(p)
                    pt_sc[r * STRIP:(r + 1) * STRIP, :] = p.astype(BF16)
                pvb = lax.dot_general(v_ref[b * tb:(b + 1) * tb, :], pt_sc[b * tb:(b + 1) * tb, :], (((0,), (0,)), ((), ())),
                                      preferred_element_type=F32)
                pv = pvb if pv is None else pv + pvb
            l_new = alpha * l_sc[0:1, :] + jnp.sum(ps, axis=0, keepdims=True)
            m_sc[...] = jnp.broadcast_to(m_new, m_sc.shape)
            l_sc[...] = jnp.broadcast_to(l_new, l_sc.shape)
            acc_sc[...] = alpha * acc_sc[...] + pv

        @pl.when(case == 0)
        def _():
            step(ATT_GROUP * tb, None)

        for d in range(ATT_GROUP):
            @pl.when(case == 1 + d)
            def _(d=d):
                step((d + 1) * tb, d * tb)

        @pl.when(case != 0)
        def _():
            l = l_sc[0:1, :]
            o = (acc_sc[...] / l).T
            of_ref[...] = o
            ob_ref[...] = o.astype(BF16)
            lse_ref[...] = m_sc[0:1, :] + jnp.log(l) * LOG2E

        @pl.when((pl.program_id(0) == HEADS - 1) & (t == n_steps - 1))
        def _():
            _ag_finish(wl_ref, gl_ref, send_sems, recv_sems)

    qmap = lambda h, t, ii_ref, jj_ref, cc_ref: (ii_ref[t], h)
    kmap = lambda h, t, ii_ref, jj_ref, cc_ref: (jj_ref[t], h)
    gs = pltpu.PrefetchScalarGridSpec(
        num_scalar_prefetch=3, grid=(HEADS, n_steps),
        in_specs=[pl.BlockSpec((tb, HEAD_PAD), qmap), pl.BlockSpec((ATT_GROUP * tb, HEAD_PAD), kmap),
                  pl.BlockSpec((ATT_GROUP * tb, HEAD_PAD), kmap), ANY],
        out_specs=[pl.BlockSpec((tb, HEAD_PAD), qmap), pl.BlockSpec((tb, HEAD_PAD), qmap),
                   pl.BlockSpec((None, 1, tb), lambda h, t, ii_ref, jj_ref, cc_ref: (h, 0, ii_ref[t])), ANY],
        scratch_shapes=[pltpu.VMEM((8, tb), F32), pltpu.VMEM((8, tb), F32), pltpu.VMEM((HEAD_PAD, tb), F32),
                        pltpu.VMEM((ATT_GROUP * tb, tb), F32), pltpu.VMEM((ATT_GROUP * tb, tb), BF16),
                        pltpu.SemaphoreType.DMA((6,)), pltpu.SemaphoreType.DMA((6,))])
    w = HEADS * HEAD_PAD
    return pl.pallas_call(
        body, grid_spec=gs,
        out_shape=[SDS((s, w), F32), SDS((s, w), BF16), SDS((HEADS, 1, s), F32), SDS((4,) + w_late.shape, w_late.dtype)],
        compiler_params=_cp(2), name="attn_fwd")(ii, jj, cc, qp, kp, vp, w_late)


def _attn_delta(do, o, tb):
    s = do.shape[0]

    def body(do_ref, o_ref, d_ref):
        for h in range(HEADS):
            hs = slice(h * HEAD_PAD, (h + 1) * HEAD_PAD)
            d_ref[h] = jnp.sum((do_ref[:, hs] * o_ref[:, hs]).T, axis=0, keepdims=True)

    blk = pl.BlockSpec((tb, HEADS * HEAD_PAD), lambda i: (i, 0))
    return pl.pallas_call(body, grid=(s // tb,), in_specs=[blk, blk],
                          out_specs=pl.BlockSpec((HEADS, 1, tb), lambda i: (0, 0, i)),
                          out_shape=SDS((HEADS, 1, s), F32), compiler_params=_cp(1), name="attn_delta")(do, o)


def _attn_bwd(qp, kp, vp, dob, lse, delta, tb, wire):
    s = qp.shape[0]
    nb = s // tb
    jj, ii, cc = _group_schedule(nb, False)
    gw = ATT_GROUP * tb
    rows = 32

    n_steps = int(ii.shape[0])

    def body(jj_ref, ii_ref, cc_ref, q_ref, k_ref, v_ref, do_ref, lse_ref, dl_ref, cw_ref, dq_ref, dk_ref, dv_ref, rcv_ref,
             st_sc, dpt_sc, pt_sc, dst_sc, send_sems, recv_sems):
        t = pl.program_id(1)
        case = cc_ref[t]
        pair = ii_ref[t]

        @pl.when((pl.program_id(0) == 0) & (t == 0))
        def _():
            _rs_send(cw_ref, rcv_ref, send_sems, recv_sems)

        @pl.when((pl.program_id(0) == HEADS - 1) & (t == n_steps - 1))
        def _():
            _rs_wait(rcv_ref, send_sems, recv_sems)

        @pl.when(t == 0)
        def _():
            dq_ref[...] = jnp.zeros_like(dq_ref)

        @pl.when(case != 0)
        def _():
            dk_ref[...] = jnp.zeros_like(dk_ref)
            dv_ref[...] = jnp.zeros_like(dv_ref)

        def step(lo, width, diag):
            q = q_ref[lo:lo + width, :]
            do = do_ref[lo:lo + width, :]
            k = k_ref[...]
            st_sc[:, 0:width] = lax.dot_general(k, q, (((1,), (1,)), ((), ())), preferred_element_type=F32)
            dpt_sc[:, 0:width] = lax.dot_general(v_ref[...], do, (((1,), (1,)), ((), ())), preferred_element_type=F32)
            dv = dk = None
            for h in range(width // tb):
                ls = slice(h * tb, (h + 1) * tb)
                lse_row = lse_ref[:, lo + h * tb:lo + (h + 1) * tb]
                dl_row = dl_ref[:, lo + h * tb:lo + (h + 1) * tb]
                for r in range(tb // rows):
                    rs = slice(r * rows, (r + 1) * rows)
                    p = jnp.exp2(st_sc[rs, ls] - lse_row)
                    first_visible = (r * rows) // CHUNK * CHUNK
                    if diag and h == 0 and first_visible > 0:
                        col = lax.broadcasted_iota(jnp.int32, (rows, tb), 1)
                        p = jnp.where(col >= first_visible, p, 0.0)
                    ds = p * (dpt_sc[rs, ls] - dl_row)
                    pt_sc[rs, ls] = p.astype(BF16)
                    dst_sc[rs, ls] = ds.astype(BF16)
                dst = dst_sc[:, ls]
                dvh = jnp.dot(pt_sc[:, ls], do[ls, :], preferred_element_type=F32)
                dkh = jnp.dot(dst, q[ls, :], preferred_element_type=F32)
                dv = dvh if dv is None else dv + dvh
                dk = dkh if dk is None else dk + dkh
                dq_ref[ATT_GROUP * pair + lo // tb + h] += lax.dot_general(k, dst, (((0,), (0,)), ((), ())), preferred_element_type=F32)
            dv_ref[...] += dv
            dk_ref[...] += dk

        @pl.when(case == 0)
        def _():
            step(0, ATT_GROUP * tb, False)

        for d in range(ATT_GROUP):
            @pl.when(case == 1 + d)
            def _(d=d):
                step(d * tb, (ATT_GROUP - d) * tb, True)

    qmap = lambda h, t, jj_ref, ii_ref, cc_ref: (ii_ref[t], h)
    kmap = lambda h, t, jj_ref, ii_ref, cc_ref: (jj_ref[t], h)
    rowmap = lambda h, t, jj_ref, ii_ref, cc_ref: (h, 0, ii_ref[t])
    gs = pltpu.PrefetchScalarGridSpec(
        num_scalar_prefetch=3, grid=(HEADS, n_steps),
        in_specs=[pl.BlockSpec((gw, HEAD_PAD), qmap), pl.BlockSpec((tb, HEAD_PAD), kmap), pl.BlockSpec((tb, HEAD_PAD), kmap),
                  pl.BlockSpec((gw, HEAD_PAD), qmap), pl.BlockSpec((None, 1, gw), rowmap), pl.BlockSpec((None, 1, gw), rowmap), ANY],
        out_specs=[pl.BlockSpec((None, nb, HEAD_PAD, tb), lambda h, t, jj_ref, ii_ref, cc_ref: (h, 0, 0, 0)),
                   pl.BlockSpec((tb, HEAD_PAD), kmap), pl.BlockSpec((tb, HEAD_PAD), kmap), ANY],
        scratch_shapes=[pltpu.VMEM((tb, gw), F32), pltpu.VMEM((tb, gw), F32), pltpu.VMEM((tb, gw), BF16),
                        pltpu.VMEM((tb, gw), BF16), pltpu.SemaphoreType.DMA((3,)), pltpu.SemaphoreType.DMA((3,))])
    w = HEADS * HEAD_PAD
    dqt, dk, dv, recv = pl.pallas_call(
        body, grid_spec=gs,
        out_shape=[SDS((HEADS, nb, HEAD_PAD, tb), F32), SDS((s, w), F32), SDS((s, w), F32), SDS((3,) + wire.shape[1:], wire.dtype)],
        compiler_params=_cp(2), name="attn_bwd")(jj, ii, cc, qp, kp, vp, dob, lse, delta, wire)
    return jnp.transpose(dqt, (1, 3, 0, 2)).reshape(s, w), dk, dv, recv


def _head_norm(t, g):
    r = lax.rsqrt(jnp.mean(t * t, axis=-1, keepdims=True) + EPS)
    th = t * r
    return r, th, th * g


def _softmax_rows(sc):
    m = jnp.max(sc, axis=-1, keepdims=True)
    e = jnp.exp(sc - m)
    return e / jnp.sum(e, axis=-1, keepdims=True)


def _memattn_fwd(qm, kvm, gq, gk, tm):
    s = qm.shape[0]
    hd = MEM_HEAD_DIM

    def body(q_ref, k_ref, v_ref, gq_ref, gk_ref, o_ref):
        _, _, qn = _head_norm(q_ref[...], gq_ref[...])
        _, _, kn = _head_norm(k_ref[...], gk_ref[...])
        sc = lax.dot_general(qn.astype(BF16), kn.astype(BF16), (((1,), (1,)), ((), ())), preferred_element_type=F32)
        p = _softmax_rows(sc * (1.0 / math.sqrt(hd)))
        o_ref[...] = jnp.dot(p.astype(BF16), v_ref[...].astype(BF16), preferred_element_type=F32).astype(BF16)

    fix = lambda i, h: (0, 0)
    return pl.pallas_call(
        body, grid=(s // tm, MEM_HEADS),
        in_specs=[pl.BlockSpec((tm, hd), lambda i, h: (i, h)), pl.BlockSpec((MEM_LEN, hd), lambda i, h: (0, h)),
                  pl.BlockSpec((MEM_LEN, hd), lambda i, h: (0, MEM_HEADS + h)), pl.BlockSpec((1, hd), fix), pl.BlockSpec((1, hd), fix)],
        out_specs=pl.BlockSpec((tm, hd), lambda i, h: (i, h)),
        out_shape=SDS((s, MEM_HEADS * hd), BF16), compiler_params=_cp(2), name="memattn_fwd")(qm, kvm, kvm, gq, gk)


def _memattn_bwd(qm, kvm, d_o, gq, gk, tm):
    s = qm.shape[0]
    hd = MEM_HEAD_DIM

    def body(q_ref, k_ref, v_ref, do_ref, gq_ref, gk_ref, dq_ref, dk_ref, dv_ref, dgq_ref, dgk_ref, dkn_sc):
        h = pl.program_id(0)
        i = pl.program_id(1)
        last = pl.num_programs(1) - 1

        @pl.when((h == 0) & (i == 0))
        def _():
            dgq_ref[...] = jnp.zeros_like(dgq_ref)
            dgk_ref[...] = jnp.zeros_like(dgk_ref)

        @pl.when(i == 0)
        def _():
            dv_ref[...] = jnp.zeros_like(dv_ref)
            dkn_sc[...] = jnp.zeros_like(dkn_sc)

        rq, qh, qn = _head_norm(q_ref[...], gq_ref[...])
        rk, kh, kn = _head_norm(k_ref[...], gk_ref[...])
        qnb = qn.astype(BF16)
        knb = kn.astype(BF16)
        scale = 1.0 / math.sqrt(hd)
        sc = lax.dot_general(qnb, knb, (((1,), (1,)), ((), ())), preferred_element_type=F32)
        p = _softmax_rows(sc * scale)
        do = do_ref[...].astype(BF16)
        dp = lax.dot_general(do, v_ref[...].astype(BF16), (((1,), (1,)), ((), ())), preferred_element_type=F32)
        dv_ref[...] += lax.dot_general(p.astype(BF16), do, (((0,), (0,)), ((), ())), preferred_element_type=F32)
        ds = ((p * (dp - jnp.sum(dp * p, axis=-1, keepdims=True))) * scale).astype(BF16)
        dqn = jnp.dot(ds, knb, preferred_element_type=F32)
        dkn_sc[...] += lax.dot_general(ds, qnb, (((0,), (0,)), ((), ())), preferred_element_type=F32)
        dgq_ref[...] += jnp.sum(dqn * qh, axis=0, keepdims=True)
        dqh = dqn * gq_ref[...]
        dq_ref[...] = (rq * (dqh - qh * jnp.mean(dqh * qh, axis=-1, keepdims=True))).astype(BF16)

        @pl.when(i == last)
        def _():
            dkn = dkn_sc[...]
            dgk_ref[...] += jnp.sum(dkn * kh, axis=0, keepdims=True)
            dkh = dkn * gk_ref[...]
            dk_ref[...] = rk * (dkh - kh * jnp.mean(dkh * kh, axis=-1, keepdims=True))

    fix = lambda h, i: (0, 0)
    qb = pl.BlockSpec((tm, hd), lambda h, i: (i, h))
    kb = pl.BlockSpec((MEM_LEN, hd), lambda h, i: (0, h))
    return pl.pallas_call(
        body, grid=(MEM_HEADS, s // tm),
        in_specs=[qb, kb, pl.BlockSpec((MEM_LEN, hd), lambda h, i: (0, MEM_HEADS + h)), qb,
                  pl.BlockSpec((1, hd), fix), pl.BlockSpec((1, hd), fix)],
        out_specs=[qb, kb, kb, pl.BlockSpec((1, hd), fix), pl.BlockSpec((1, hd), fix)],
        out_shape=[SDS((s, MEM_HEADS * hd), BF16), SDS((MEM_LEN, MEM_HEADS * hd), F32), SDS((MEM_LEN, MEM_HEADS * hd), F32),
                   SDS((1, hd), F32), SDS((1, hd), F32)],
        scratch_shapes=[pltpu.VMEM((MEM_LEN, hd), F32)],
        compiler_params=_cp(2), name="memattn_bwd")(qm, kvm, kvm, d_o, gq, gk)


def _ffn_specs(tm, tn, nbj, s, order_ji):
    if order_ji:
        ij = lambda f: (lambda j, i: f(i, j))
    else:
        ij = lambda f: f
    prev = lambda i: jnp.maximum(i * (tm // FFN_HALO) - 1, 0)
    cur_g = pl.BlockSpec((tm, tn), ij(lambda i, j: (i, j)))
    cur_v = pl.BlockSpec((tm, tn), ij(lambda i, j: (i, j + nbj)))
    halo_g = pl.BlockSpec((FFN_HALO, tn), ij(lambda i, j: (prev(i), j)))
    halo_v = pl.BlockSpec((FFN_HALO, tn), ij(lambda i, j: (prev(i), j + nbj)))
    w_g = pl.BlockSpec((8, tn), ij(lambda i, j: (0, j)))
    w_v = pl.BlockSpec((8, tn), ij(lambda i, j: (0, j + nbj)))
    b_g = pl.BlockSpec((1, tn), ij(lambda i, j: (0, j)))
    b_v = pl.BlockSpec((1, tn), ij(lambda i, j: (0, j + nbj)))
    return cur_g, cur_v, halo_g, halo_v, w_g, w_v, b_g, b_v


FFN_STRIP = 16


def _conv3_rows(ext, w_ref, b_ref, o, n):
    return (w_ref[0:1, :] * ext[FFN_HALO - 2 + o:FFN_HALO - 2 + o + n, :] + w_ref[1:2, :] * ext[FFN_HALO - 1 + o:FFN_HALO - 1 + o + n, :]
            + w_ref[2:3, :] * ext[FFN_HALO + o:FFN_HALO + o + n, :] + b_ref[...])


def _ffn_fwd(up0, w8, b, tm, tn):
    s = up0.shape[0]
    nbj = D_FF // tn

    def body(g_ref, v_ref, gh_ref, vh_ref, wg_ref, wv_ref, bg_ref, bv_ref, act_ref, extg, extv):
        first = pl.program_id(0) == 0
        for ext, h_ref, c_ref in ((extg, gh_ref, g_ref), (extv, vh_ref, v_ref)):
            ext[0:FFN_HALO, :] = jnp.where(first, 0.0, h_ref[...])
            ext[FFN_HALO:, :] = c_ref[...]
        for r in range(tm // FFN_STRIP):
            o = r * FFN_STRIP
            ug = _conv3_rows(extg, wg_ref, bg_ref, o, FFN_STRIP)
            uv = _conv3_rows(extv, wv_ref, bv_ref, o, FFN_STRIP)
            act_ref[o:o + FFN_STRIP, :] = ((ug * jax.nn.sigmoid(ug)) * uv).astype(BF16)

    specs = _ffn_specs(tm, tn, nbj, s, False)
    return pl.pallas_call(
        body, grid=(s // tm, nbj), in_specs=list(specs),
        out_specs=pl.BlockSpec((tm, tn), lambda i, j: (i, j)), out_shape=SDS((s, D_FF), BF16),
        scratch_shapes=[pltpu.VMEM((tm + FFN_HALO, tn), F32), pltpu.VMEM((tm + FFN_HALO, tn), F32)],
        compiler_params=_cp(2), name="ffn_fwd")(up0, up0, up0, up0, w8, w8, b, b)


def _ffn_bwd(d_act, up0, w8, b, tm, tn):
    s = up0.shape[0]
    nbj = D_FF // tn
    te = tm + FFN_HALO

    def body(da_ref, dan_ref, g_ref, v_ref, gh_ref, vh_ref, gn_ref, vn_ref, wg_ref, wv_ref, bg_ref, bv_ref,
             og_ref, ov_ref, dbg_ref, dbv_ref, dwg_ref, dwv_ref, extg, extv, extdg, extdv, accg, accv):
        i = pl.program_id(1)
        first = i == 0
        last = i == pl.num_programs(1) - 1

        @pl.when(first)
        def _():
            for r in (dbg_ref, dbv_ref, dwg_ref, dwv_ref):
                r[...] = jnp.zeros_like(r)

        for ext, h_ref, c_ref, n_ref in ((extg, gh_ref, g_ref, gn_ref), (extv, vh_ref, v_ref, vn_ref)):
            ext[0:FFN_HALO, :] = jnp.where(first, 0.0, h_ref[...])
            ext[FFN_HALO:FFN_HALO + tm, :] = c_ref[...]
            ext[FFN_HALO + tm:, :] = n_ref[...]

        def fold8(x):
            acc = x[0:8, :]
            for q in range(1, x.shape[0] // 8):
                acc = acc + x[q * 8:(q + 1) * 8, :]
            return acc

        def taps(ext, o, n):
            return [ext[FFN_HALO - 2 + k + o:FFN_HALO - 2 + k + o + n, :] for k in range(3)]

        accg[...] = jnp.zeros_like(accg)
        accv[...] = jnp.zeros_like(accv)

        def gate_bwd(o, n, da, own_rows):
            xg, xv = taps(extg, o, n), taps(extv, o, n)
            ug = wg_ref[0:1, :] * xg[0] + wg_ref[1:2, :] * xg[1] + wg_ref[2:3, :] * xg[2] + bg_ref[...]
            uv = wv_ref[0:1, :] * xv[0] + wv_ref[1:2, :] * xv[1] + wv_ref[2:3, :] * xv[2] + bv_ref[...]
            sg = jax.nn.sigmoid(ug)
            dgt = da * uv * (sg * (1.0 + ug * (1.0 - sg)))
            dvl = da * (ug * sg)
            extdg[o:o + n, :] = dgt
            extdv[o:o + n, :] = dvl
            if own_rows:
                for acc, d, x in ((accg, dgt, xg), (accv, dvl, xv)):
                    acc[0] += fold8(d)
                    for k in range(3):
                        acc[1 + k] += fold8(d * x[k])

        for r in range(tm // FFN_STRIP):
            gate_bwd(r * FFN_STRIP, FFN_STRIP, da_ref[r * FFN_STRIP:(r + 1) * FFN_STRIP, :], True)
        gate_bwd(tm, FFN_HALO, jnp.where(last, 0.0, dan_ref[...]), False)

        for extd, w_ref, o_ref, db_ref, dw_ref, acc in ((extdg, wg_ref, og_ref, dbg_ref, dwg_ref, accg),
                                                        (extdv, wv_ref, ov_ref, dbv_ref, dwv_ref, accv)):
            for r in range(tm // FFN_STRIP):
                o = r * FFN_STRIP
                o_ref[o:o + FFN_STRIP, :] = (w_ref[2:3, :] * extd[o:o + FFN_STRIP, :] + w_ref[1:2, :] * extd[o + 1:o + 1 + FFN_STRIP, :]
                                             + w_ref[0:1, :] * extd[o + 2:o + 2 + FFN_STRIP, :]).astype(BF16)
            db_ref[...] += jnp.sum(acc[0], axis=0, keepdims=True)
            for k in range(3):
                dw_ref[k:k + 1, :] += jnp.sum(acc[1 + k], axis=0, keepdims=True)

    cur_g, cur_v, halo_g, halo_v, w_g, w_v, b_g, b_v = _ffn_specs(tm, tn, nbj, s, True)
    nxt_row = lambda i: jnp.minimum((i + 1) * (tm // FFN_HALO), s // FFN_HALO - 1)
    cur = pl.BlockSpec((tm, tn), lambda j, i: (i, j))
    nxt = pl.BlockSpec((FFN_HALO, tn), lambda j, i: (nxt_row(i), j))
    nxt_v = pl.BlockSpec((FFN_HALO, tn), lambda j, i: (nxt_row(i), j + nbj))
    acc1 = pl.BlockSpec((1, tn), lambda j, i: (0, j))
    acc8 = pl.BlockSpec((8, tn), lambda j, i: (0, j))
    return pl.pallas_call(
        body, grid=(nbj, s // tm), in_specs=[cur, nxt, cur_g, cur_v, halo_g, halo_v, nxt, nxt_v, w_g, w_v, b_g, b_v],
        out_specs=[cur, cur, acc1, acc1, acc8, acc8],
        out_shape=[SDS((s, D_FF), BF16), SDS((s, D_FF), BF16), SDS((1, D_FF), F32), SDS((1, D_FF), F32),
                   SDS((8, D_FF), F32), SDS((8, D_FF), F32)],
        scratch_shapes=[pltpu.VMEM((tm + 2 * FFN_HALO, tn), F32), pltpu.VMEM((tm + 2 * FFN_HALO, tn), F32),
                        pltpu.VMEM((te, tn), F32), pltpu.VMEM((te, tn), F32),
                        pltpu.VMEM((4, 8, tn), F32), pltpu.VMEM((4, 8, tn), F32)],
        compiler_params=_cp(2), name="ffn_bwd")(d_act, d_act, up0, up0, up0, up0, up0, up0, w8, w8, b, b)


def _down_loss(act, w_down, x2, target, tm):
    s = act.shape[0]

    def body(a_ref, w_ref, x_ref, t_ref, dyf_ref, dyb_ref, ls_ref):
        @pl.when(pl.program_id(0) == 0)
        def _():
            ls_ref[...] = jnp.zeros_like(ls_ref)

        y = x_ref[...] + jnp.dot(a_ref[...], w_ref[...], preferred_element_type=F32)
        e = y - t_ref[...]
        ls_ref[...] += jnp.sum(e * e)
        dy = e * (1.0 / D_MODEL)
        dyf_ref[...] = dy
        dyb_ref[...] = dy.astype(BF16)

    row = lambda i: (i, 0)
    return pl.pallas_call(
        body, grid=(s // tm,),
        in_specs=[pl.BlockSpec((tm, D_FF), row), pl.BlockSpec((D_FF, D_MODEL), lambda i: (0, 0)),
                  pl.BlockSpec((tm, D_MODEL), row), pl.BlockSpec((tm, D_MODEL), row)],
        out_specs=[pl.BlockSpec((tm, D_MODEL), row), pl.BlockSpec((tm, D_MODEL), row), pl.BlockSpec((8, 128), lambda i: (0, 0))],
        out_shape=[SDS((s, D_MODEL), F32), SDS((s, D_MODEL), BF16), SDS((8, 128), F32)],
        compiler_params=_cp(1), name="down_loss")(act, w_down, x2, target)


def _adamw_math(w, g, m, v):
    mn = ADAM_B1 * m + (1.0 - ADAM_B1) * g
    vn = ADAM_B2 * v + (1.0 - ADAM_B2) * (g * g)
    m_hat = mn / (1.0 - ADAM_B1 ** ADAM_STEP)
    v_hat = vn / (1.0 - ADAM_B2 ** ADAM_STEP)
    return -ADAM_LR * (m_hat / (jnp.sqrt(v_hat) + ADAM_EPS) + ADAM_WD * w), mn, vn


def _adamw(w, g, m, v, name):
    rows, cols = w.shape
    tr = rows if rows <= 256 else (256 if rows % 256 == 0 else rows // 2)

    def body(w_ref, g_ref, m_ref, v_ref, d_ref, mo_ref, vo_ref):
        d_ref[...], mo_ref[...], vo_ref[...] = _adamw_math(w_ref[...], g_ref[...], m_ref[...], v_ref[...])

    blk = pl.BlockSpec((tr, cols), lambda i: (i, 0))
    return pl.pallas_call(body, grid=(rows // tr,), in_specs=[blk] * 4, out_specs=[blk] * 3,
                          out_shape=[SDS((rows, cols), F32)] * 3, compiler_params=_cp(1), name=name)(w, g, m, v)


def _adamw_small(ws, gs, ms, vs):
    n = len(ws)

    def body(*refs):
        ins, outs = refs[:4 * n], refs[4 * n:]
        for k in range(n):
            d, mn, vn = _adamw_math(ins[k][...], ins[n + k][...], ins[2 * n + k][...], ins[3 * n + k][...])
            outs[k][...] = d
            outs[n + k][...] = mn
            outs[2 * n + k][...] = vn

    vm = pl.BlockSpec(memory_space=pltpu.VMEM)
    outs = pl.pallas_call(body, in_specs=[vm] * (4 * n), out_specs=[vm] * (3 * n),
                          out_shape=[SDS(w.shape, F32) for w in ws] * 3, name="adamw_small")(*ws, *gs, *ms, *vs)
    return outs[:n], outs[n:2 * n], outs[2 * n:]


ANY = pl.BlockSpec(memory_space=pl.ANY)


def _coords():
    return lax.axis_index("x"), lax.axis_index("y"), lax.axis_index("c")


def _other_chips(x, y):
    return [(1 - x, y), (x, 1 - y), (1 - x, 1 - y)]


D2D_CHUNKS = 8
ICI_CHUNKS = 4


def _row_chunks(n_rows, n_chunks, align):
    step = -(-n_rows // (n_chunks * align)) * align
    return [(r, min(step, n_rows - r)) for r in range(0, n_rows, step)]


def _ag_copy(out_ref, send_sems, recv_sems, k, shard, base, r0, nr, to, src=None):
    rows_ = pl.ds(pl.multiple_of(base + r0, 16), nr)
    dst = out_ref.at[shard, rows_]
    return pltpu.make_async_remote_copy(src_ref=dst if src is None else src.at[rows_], dst_ref=dst, send_sem=send_sems.at[k],
                                        recv_sem=recv_sems.at[k], device_id=to, device_id_type=MESH)


def _ag_send(w_ref, out_ref, send_sems, recv_sems):
    x, y, c = _coords()
    half_rows = w_ref.shape[0] // 2
    for k, (px, py) in enumerate(_other_chips(x, y)):
        for r0, nr in _row_chunks(half_rows, ICI_CHUNKS, 16):
            _ag_copy(out_ref, send_sems, recv_sems, k, 2 * x + y, c * half_rows, r0, nr, (px, py, c), src=w_ref).start()


def _ag_finish(w_ref, out_ref, send_sems, recv_sems):
    x, y, c = _coords()
    half_rows = w_ref.shape[0] // 2
    chips = _other_chips(x, y)
    sibling = (x, y, 1 - c)
    for k, (px, py) in enumerate(chips):
        _ag_copy(out_ref, send_sems, recv_sems, k, 2 * px + py, c * half_rows, 0, half_rows, (px, py, c)).wait_recv()
        for r0, nr in _row_chunks(half_rows, ICI_CHUNKS, 16):
            _ag_copy(out_ref, send_sems, recv_sems, 3 + k, 2 * px + py, c * half_rows, r0, nr, sibling).start()
    for k, (px, py) in enumerate(chips):
        _ag_copy(out_ref, send_sems, recv_sems, 3 + k, 2 * px + py, (1 - c) * half_rows, 0, half_rows, sibling).wait_recv()
    for k in range(6):
        _ag_copy(out_ref, send_sems, recv_sems, k, 2 * x + y, c * half_rows, 0, half_rows, sibling).wait_send()


def _ag_weights(wsh):
    rows, cols = wsh.shape

    def body(w_ref, out_ref, send_sems, recv_sems):
        _ag_send(w_ref, out_ref, send_sems, recv_sems)
        _ag_finish(w_ref, out_ref, send_sems, recv_sems)

    return pl.pallas_call(
        body, in_specs=[ANY], out_specs=ANY, out_shape=SDS((4, rows, cols), wsh.dtype),
        scratch_shapes=[pltpu.SemaphoreType.DMA((6,)), pltpu.SemaphoreType.DMA((6,))],
        name="ag_weights")(wsh)


def _rs_swap_halves(gfull, tag):
    n_sh, rows, cols = gfull.shape
    half_rows = rows // 2

    def body(g_ref, recv_ref, send_sem, recv_sem):
        x, y, c = _coords()
        sib_base = (1 - c) * half_rows
        for sh in range(n_sh):
            for r0, nr in _row_chunks(half_rows, D2D_CHUNKS, 8):
                pltpu.make_async_remote_copy(
                    src_ref=g_ref.at[sh, pl.ds(pl.multiple_of(sib_base + r0, 8), nr)], dst_ref=recv_ref.at[sh, pl.ds(r0, nr)],
                    send_sem=send_sem, recv_sem=recv_sem, device_id=(x, y, 1 - c), device_id_type=MESH).start()
        pltpu.make_async_remote_copy(src_ref=recv_ref, dst_ref=recv_ref, send_sem=send_sem, recv_sem=recv_sem,
                                     device_id=(x, y, 1 - c), device_id_type=MESH).wait()

    return pl.pallas_call(
        body, in_specs=[ANY], out_specs=ANY, out_shape=SDS((n_sh, half_rows, cols), gfull.dtype),
        scratch_shapes=[pltpu.SemaphoreType.DMA, pltpu.SemaphoreType.DMA], name="rs_swap_halves" + tag)(gfull)


def _rs_add_pair(gfull, recv, core, tr, tag):
    n_sh, rows, cols = gfull.shape
    half_rows = rows // 2
    nblk = half_rows // tr

    def body(c_ref, g_ref, r_ref, o_ref, ob_ref):
        acc = g_ref[...] + r_ref[...]
        o_ref[...] = acc
        ob_ref[...] = acc.astype(BF16)

    out = pl.BlockSpec((None, tr, cols), lambda sh, i, c_ref: (sh, i, 0))
    gs = pltpu.PrefetchScalarGridSpec(
        num_scalar_prefetch=1, grid=(n_sh, nblk),
        in_specs=[pl.BlockSpec((None, tr, cols), lambda sh, i, c_ref: (sh, c_ref[0] * nblk + i, 0)), out],
        out_specs=[out, out])
    return pl.pallas_call(body, grid_spec=gs, out_shape=[SDS((n_sh, half_rows, cols), F32), SDS((n_sh, half_rows, cols), BF16)],
                          compiler_params=_cp(2), name="rs_add_pair" + tag)(core, gfull, recv)


def _rs_send(cs_ref, recv_ref, send_sems, recv_sems):
    x, y, c = _coords()
    half_rows = cs_ref.shape[1]
    for k, (px, py) in enumerate(_other_chips(x, y)):
        for r0, nr in _row_chunks(half_rows, ICI_CHUNKS, 16):
            pltpu.make_async_remote_copy(
                src_ref=cs_ref.at[2 * px + py, pl.ds(r0, nr)], dst_ref=recv_ref.at[k, pl.ds(r0, nr)],
                send_sem=send_sems.at[k], recv_sem=recv_sems.at[k], device_id=(px, py, c), device_id_type=MESH).start()


def _rs_wait(recv_ref, send_sems, recv_sems):
    x, y, c = _coords()
    for k, (px, py) in enumerate(_other_chips(x, y)):
        pltpu.make_async_remote_copy(src_ref=recv_ref.at[k], dst_ref=recv_ref.at[k], send_sem=send_sems.at[k],
                                     recv_sem=recv_sems.at[k], device_id=(px, py, c), device_id_type=MESH).wait()


def _rs_to_owner(chipsum):
    n_sh, half_rows, cols = chipsum.shape

    def body(cs_ref, recv_ref, send_sems, recv_sems):
        _rs_send(cs_ref, recv_ref, send_sems, recv_sems)
        _rs_wait(recv_ref, send_sems, recv_sems)

    return pl.pallas_call(
        body, in_specs=[ANY], out_specs=ANY, out_shape=SDS((3, half_rows, cols), chipsum.dtype),
        scratch_shapes=[pltpu.SemaphoreType.DMA((3,)), pltpu.SemaphoreType.DMA((3,))], name="rs_to_owner")(chipsum)


def _rs_add_chips(chipsum, recv, shard_core, tr, tag):
    _, half_rows, cols = chipsum.shape

    def body(s_ref, m_ref, r0_ref, r1_ref, r2_ref, o_ref):
        o_ref[...] = ((m_ref[...] + r0_ref[...].astype(F32)) + r1_ref[...].astype(F32)) + r2_ref[...].astype(F32)

    gs = pltpu.PrefetchScalarGridSpec(
        num_scalar_prefetch=1, grid=(half_rows // tr,),
        in_specs=[pl.BlockSpec((None, tr, cols), lambda i, s_ref: (s_ref[0], i, 0))]
        + [pl.BlockSpec((None, tr, cols), (lambda k: lambda i, s_ref: (k, i, 0))(k)) for k in range(3)],
        out_specs=pl.BlockSpec((None, tr, cols), lambda i, s_ref: (s_ref[1], i, 0)))
    return pl.pallas_call(body, grid_spec=gs, out_shape=SDS((2, half_rows, cols), F32),
                          compiler_params=_cp(1), name="rs_add_chips" + tag)(shard_core, chipsum, recv, recv, recv)


def _rs_join_halves(buf, tag):
    _, half_rows, cols = buf.shape

    def body(b_ref, out_ref, send_sem, recv_sem):
        x, y, c = _coords()
        for r0, nr in _row_chunks(half_rows, D2D_CHUNKS, 8):
            pltpu.make_async_remote_copy(src_ref=out_ref.at[c, pl.ds(r0, nr)], dst_ref=out_ref.at[c, pl.ds(r0, nr)], send_sem=send_sem,
                                         recv_sem=recv_sem, device_id=(x, y, 1 - c), device_id_type=MESH).start()
        pltpu.make_async_remote_copy(src_ref=out_ref.at[c], dst_ref=out_ref.at[c], send_sem=send_sem, recv_sem=recv_sem,
                                     device_id=(x, y, 1 - c), device_id_type=MESH).wait()

    return pl.pallas_call(
        body, in_specs=[ANY], out_specs=ANY, out_shape=SDS(buf.shape, buf.dtype), input_output_aliases={0: 0},
        scratch_shapes=[pltpu.SemaphoreType.DMA, pltpu.SemaphoreType.DMA], name="rs_join_halves" + tag)(buf)


BIG = [("w_in", (1024, 1440), 1), ("w_uq", (256, 768), 1), ("w_ukv", (128, 1024), 1), ("w_out", (1024, 1024), 0),
       ("w_mem_q", (1024, 1024), 0), ("w_mem_kv", (1024, 2048), 1), ("w_mem_o", (1024, 1024), 0),
       ("w_up", (1024, 5632), 1), ("w_down", (2816, 1024), 0)]
SMALL_REP = [("mix_norm_g", 1024), ("b_conv_in", 1024), ("b_conv_dw", 512), ("conv_ln_g", 512), ("conv_ln_b", 512),
             ("q_lat_norm_g", 256), ("kv_lat_norm_g", 128), ("q_norm_g", 96), ("k_norm_g", 96), ("mem_norm_x_g", 1024),
             ("mem_norm_m_g", 1024), ("mem_q_norm_g", 256), ("mem_k_norm_g", 256), ("ffn_norm_g", 1024), ("b_ffn_dw", 5632)]
SMALL_SH = [("w_conv_dw", (31, 512)), ("w_ffn_dw", (3, 5632))]


def _shard_shape(shape, axis):
    return tuple(d // 4 if a == axis else d for a, d in enumerate(shape))


def _pack_rows(parts, rows, cols):
    flat = jnp.concatenate([p.reshape(-1) for p in parts])
    flat = jnp.pad(flat, (0, rows * cols - flat.shape[0]))
    return flat.reshape(rows, cols)


AG_EARLY, AG_LATE = BIG[:3], BIG[3:]
RS_REST, RS_FFN = AG_EARLY, AG_LATE


def _group_rows(group):
    used = sum(_shard_shape(shape, axis)[0] * _shard_shape(shape, axis)[1] // PACK_COLS for _, shape, axis in group)
    return -(-used // 512) * 512


def _pick_rows(n, cap=384):
    return max(r for r in range(16, cap + 1, 16) if n % r == 0)


def _pack_big_shards(ws, group):
    parts = [ws[n].reshape(-1, PACK_COLS) for n, _, _ in group]
    used = sum(p.shape[0] for p in parts)
    pad = _group_rows(group) - used
    return jnp.concatenate(parts + ([jnp.zeros((pad, PACK_COLS), parts[0].dtype)] if pad else []), axis=0)


def _unpack_big_shards(packed, group):
    out, r = {}, 0
    for n, shape, axis in group:
        sh = _shard_shape(shape, axis)
        nr = sh[0] * sh[1] // PACK_COLS
        out[n] = packed[r:r + nr].reshape(sh)
        r += nr
    return out


def _unpack_gathered(g, group):
    out, r = {}, 0
    for n, shape, axis in group:
        sh = _shard_shape(shape, axis)
        nr = sh[0] * sh[1] // PACK_COLS
        part = g[:, r:r + nr]
        if axis == 0:
            out[n] = part.reshape(shape)
        else:
            out[n] = part.reshape((4,) + sh).transpose(1, 0, 2).reshape(shape)
        r += nr
    return out


def _pack_full_grads(gs, group):
    parts = []
    for n, shape, axis in group:
        sh = _shard_shape(shape, axis)
        nr = sh[0] * sh[1] // PACK_COLS
        if axis == 0:
            parts.append(gs[n].reshape(4, nr, PACK_COLS))
        else:
            parts.append(gs[n].reshape(shape[0], 4, sh[1]).transpose(1, 0, 2).reshape(4, nr, PACK_COLS))
    pad = _group_rows(group) - sum(p.shape[1] for p in parts)
    return jnp.concatenate(parts + ([jnp.zeros((4, pad, PACK_COLS), F32)] if pad else []), axis=1)


def _rs_first(gfull, core_idx, tag):
    tr = _pick_rows(gfull.shape[1] // 2)
    return _rs_add_pair(gfull, _rs_swap_halves(gfull, tag), core_idx.reshape(1), tr, tag)


def _rs_last(chipsum, recv, shard_idx, core_idx, tag):
    tr = _pick_rows(chipsum.shape[1])
    red = _rs_add_chips(chipsum, recv, jnp.stack([shard_idx, core_idx]), tr, tag)
    return _rs_join_halves(red, tag).reshape(2 * chipsum.shape[1], chipsum.shape[2])


def _rope_tables(positions):
    inv_freq = ROPE_THETA ** (-jnp.arange(0, ROPE, 2, dtype=F32) / ROPE)
    ang = positions.astype(F32)[:, None] * inv_freq
    cos, sin = jnp.cos(ang), jnp.sin(ang)
    s = positions.shape[0]
    cosf = jnp.concatenate([jnp.ones((s, NOPE), F32), cos, cos, jnp.ones((s, HEAD_PAD - HEAD_DIM), F32)], axis=-1)
    sinf = jnp.concatenate([jnp.zeros((s, NOPE), F32), -sin, sin, jnp.zeros((s, HEAD_PAD - HEAD_DIM), F32)], axis=-1)
    return cosf, sinf


def _pad_heads(w, per_head):
    k = w.shape[0]
    w3 = w.reshape(k, HEADS, per_head)
    return jnp.pad(w3, ((0, 0), (0, 0), (0, HEAD_PAD - per_head))).reshape(k, HEADS * HEAD_PAD)


def _layer_grads(x, mem, positions, target, wf, w_late, sp, shard_idx, core_idx):
    wf = dict(wf)
    s = x.shape[0]
    tm = _row_tile(s, 512)
    tc = _row_tile(s, 256)
    tb = 512 if s % (512 * ATT_GROUP) == 0 else 128
    row2 = lambda a: a.reshape(1, -1)

    w_in = wf["w_in"]
    w_in_pad = jnp.concatenate([w_in[:, :1408], jnp.zeros((D_MODEL, NOPE), BF16), w_in[:, 1408:],
                                jnp.zeros((D_MODEL, HEAD_PAD - HEAD_DIM), BF16)], axis=1)
    w_uq_pad = _pad_heads(wf["w_uq"], HEAD_DIM)
    w_ukv = wf["w_ukv"]
    gq_pad = jnp.pad(sp["q_norm_g"], (0, HEAD_PAD - HEAD_DIM)).reshape(1, HEAD_PAD)
    gk_pad = jnp.pad(sp["k_norm_g"], (0, HEAD_PAD - HEAD_DIM)).reshape(1, HEAD_PAD)
    w_dw32 = jnp.pad(sp["w_conv_dw"], ((0, 1), (0, 0)))
    w_ffn8 = jnp.pad(sp["w_ffn_dw"], ((0, 5), (0, 0)))
    b_ffn = row2(sp["b_ffn_dw"])
    cosf, sinf = _rope_tables(positions)

    z, h1 = _norm_linear(x, 0, D_MODEL, row2(sp["mix_norm_g"]), w_in_pad, F32, tm, IN_COLS_PAD, "in_proj")
    u, u0, u1 = _conv_fwd(z, row2(sp["b_conv_in"]), w_dw32, row2(sp["b_conv_dw"]), row2(sp["conv_ln_g"]), row2(sp["conv_ln_b"]), tc)
    q_raw, cqn = _norm_linear(z, 1024 // Q_RANK, Q_RANK, row2(sp["q_lat_norm_g"]), w_uq_pad, F32, tm, 1024, "q_up")
    kv_raw, ckvn = _norm_linear(z, 1280 // KV_RANK, KV_RANK, row2(sp["kv_lat_norm_g"]), w_ukv, F32, tm, 1024, "kv_up")
    qp, kp, vp = _mla_prep(q_raw, kv_raw, z, cosf, sinf, gq_pad, gk_pad, tc)
    o_f, o_b, lse, gathered = _attn_fwd(qp, kp, vp, tb, w_late)
    wf.update(_unpack_gathered(lax.dynamic_update_index_in_dim(gathered, w_late, shard_idx, 0), AG_LATE))
    w_out_u = wf["w_out"][:CONV_CH]
    w_out_o = jnp.pad(wf["w_out"][CONV_CH:].reshape(HEADS, NOPE, D_MODEL), ((0, 0), (NOPE, 0), (0, 0))).reshape(HEADS * HEAD_PAD, D_MODEL)
    w_up_g, w_up_v = wf["w_up"][:, :D_FF], wf["w_up"][:, D_FF:]
    (x1,) = _linear([(u, w_out_u), (o_b, w_out_o)], False, x, [F32], tm, 1024, "out_proj")

    qm, hq = _norm_linear(x1, 0, D_MODEL, row2(sp["mem_norm_x_g"]), wf["w_mem_q"], F32, tm, 1024, "memq_proj")
    kvm, hm = _norm_linear(mem, 0, D_MODEL, row2(sp["mem_norm_m_g"]), wf["w_mem_kv"], F32, MEM_LEN, 1024, "memkv_proj")
    gmq, gmk = row2(sp["mem_q_norm_g"]), row2(sp["mem_k_norm_g"])
    o_m = _memattn_fwd(qm, kvm, gmq, gmk, tm)
    (x2,) = _linear([(o_m, wf["w_mem_o"])], False, x1, [F32], tm, 1024, "memo_proj")

    up0, h3 = _norm_linear(x2, 0, D_MODEL, row2(sp["ffn_norm_g"]), wf["w_up"], F32, _row_tile(s, 1024), D_FF // 2, "up_proj")
    act = _ffn_fwd(up0, w_ffn8, b_ffn, tc, D_FF // 2)
    dy_f, dy_b, lsum = _down_loss(act, wf["w_down"], x2, target, tm)

    g = {}
    (d_act,) = _linear([(dy_b, wf["w_down"])], True, None, [F32], tm, D_FF // 2, "down_bwd")
    g["w_down"] = _dw(act, dy_b, "dw_down")
    d_up0g, d_up0v, dbg, dbv, dwg, dwv = _ffn_bwd(d_act, up0, w_ffn8, b_ffn, tc, D_FF // 2)
    g["b_ffn_dw"] = jnp.concatenate([dbg, dbv], axis=1).reshape(-1)
    g["w_ffn_dw"] = jnp.concatenate([dwg[:3], dwv[:3]], axis=1)
    g["w_up"] = jnp.concatenate([_dw(h3, d_up0g, "dw_up_g"), _dw(h3, d_up0v, "dw_up_v")], axis=1)
    d_x2f, d_x2b, dg = _linear_normbwd([(d_up0g, w_up_g), (d_up0v, w_up_v)], x2, 0, row2(sp["ffn_norm_g"]), dy_f,
                                       [F32, BF16], tc, "up_bwd")
    g["ffn_norm_g"] = dg.reshape(-1)

    (d_om,) = _linear([(d_x2b, wf["w_mem_o"])], True, None, [BF16], tm, 1024, "memo_bwd")
    g["w_mem_o"] = _dw(o_m, d_x2b, "dw_mem_o")
    d_qm, d_km, d_vm, dgq, dgk = _memattn_bwd(qm, kvm, d_om, gmq, gmk, tm)
    g["mem_q_norm_g"], g["mem_k_norm_g"] = dgq.reshape(-1), dgk.reshape(-1)
    d_kvm = jnp.concatenate([d_km, d_vm], axis=1)
    g["w_mem_q"] = _dw(hq, d_qm, "dw_mem_q")
    g["w_mem_kv"] = _dw(hm, d_kvm, "dw_mem_kv")
    d_x1f, d_x1b, dg = _linear_normbwd([(d_qm, wf["w_mem_q"])], x1, 0, row2(sp["mem_norm_x_g"]), d_x2f, [F32, BF16], tm, "memq_bwd")
    g["mem_norm_x_g"] = dg.reshape(-1)
    _, dg = _linear_normbwd([(d_kvm, wf["w_mem_kv"])], mem, 0, row2(sp["mem_norm_m_g"]), None, [BF16], MEM_LEN, "memkv_bwd")
    g["mem_norm_m_g"] = dg.reshape(-1)

    (d_u,) = _linear([(d_x1b, w_out_u)], True, None, [F32], tm, CONV_CH, "out_bwd_u")
    d_of, d_ob = _linear([(d_x1b, w_out_o)], True, None, [F32, BF16], tm, 1024, "out_bwd_o")
    dw_out_u = _dw(u, d_x1b, "dw_out_u")
    dw_out_o = _dw(o_b, d_x1b, "dw_out_o")
    g["w_out"] = jnp.concatenate([dw_out_u, dw_out_o.reshape(HEADS, HEAD_PAD, D_MODEL)[:, NOPE:].reshape(HEADS * NOPE, D_MODEL)], axis=0)
    chipsum_ffn, wire_ffn = _rs_first(_pack_full_grads(g, RS_FFN), core_idx, "_ffn")
    delta = _attn_delta(d_of, o_f, tb)
    dqp, dkp, dvp, recv_ffn = _attn_bwd(qp, kp, vp, d_ob, lse, delta, tb, wire_ffn)
    g_ffn_packed = _rs_last(chipsum_ffn, recv_ffn, shard_idx, core_idx, "_ffn")
    d_qraw, d_kvraw, d_kr, dgq, dgk = _mla_prep_bwd(dqp, dkp, dvp, q_raw, kv_raw, z, cosf, sinf, gq_pad, gk_pad, tc)
    g["q_norm_g"], g["k_norm_g"] = dgq.reshape(-1)[:HEAD_DIM], dgk.reshape(-1)[:HEAD_DIM]
    g["w_uq"] = _dw(cqn, d_qraw, "dw_uq").reshape(Q_RANK, HEADS, HEAD_PAD)[:, :, :HEAD_DIM].reshape(Q_RANK, HEADS * HEAD_DIM)
    g["w_ukv"] = _dw(ckvn, d_kvraw, "dw_ukv")
    d_cq, dg = _linear_normbwd([(d_qraw, w_uq_pad)], z, 1024 // Q_RANK, row2(sp["q_lat_norm_g"]), None, [BF16], tm, "q_up_bwd")
    g["q_lat_norm_g"] = dg.reshape(-1)
    d_ckv, dg = _linear_normbwd([(d_kvraw, w_ukv)], z, 1280 // KV_RANK, row2(sp["kv_lat_norm_g"]), None, [BF16], tm, "kv_up_bwd")
    g["kv_lat_norm_g"] = dg.reshape(-1)
    d_u1, dlg, dlb, dbdw = _conv_bwd_ln(d_u, u1, row2(sp["conv_ln_g"]), row2(sp["conv_ln_b"]), tc)
    g["conv_ln_g"], g["conv_ln_b"], g["b_conv_dw"] = dlg.reshape(-1), dlb.reshape(-1), dbdw.reshape(-1)
    d_conv, dw_dw, dbin = _conv_bwd_dw(d_u1, u0, z, row2(sp["b_conv_in"]), w_dw32, tc)
    g["w_conv_dw"], g["b_conv_in"] = dw_dw[:CONV_WIDTH], dbin.reshape(-1)
    pieces = [(d_conv, w_in_pad[:, :1024]), (d_cq, w_in_pad[:, 1024:1280]), (d_ckv, w_in_pad[:, 1280:1408]), (d_kr, w_in_pad[:, 1408:])]
    dw_in = [_dw(h1, d, "dw_in_%d" % k) for k, (d, _) in enumerate(pieces)]
    g["w_in"] = jnp.concatenate([dw_in[0], dw_in[1], dw_in[2], dw_in[3][:, NOPE:HEAD_DIM]], axis=1)
    grad_x, dg = _linear_normbwd(pieces, x, 0, row2(sp["mix_norm_g"]), d_x1f, [F32], tm, "in_bwd")
    g["mix_norm_g"] = dg.reshape(-1)
    return lsum[0, 0], grad_x, g, g_ffn_packed


def kernel(x, mem, positions, mix_norm_g, w_in, b_conv_in, w_conv_dw, b_conv_dw, conv_ln_g, conv_ln_b, q_lat_norm_g, w_uq, kv_lat_norm_g, w_ukv, q_norm_g, k_norm_g, w_out, mem_norm_x_g, mem_norm_m_g, w_mem_q, w_mem_kv, mem_q_norm_g, mem_k_norm_g, w_mem_o, ffn_norm_g, w_up, w_ffn_dw, b_ffn_dw, w_down, loss_target, m_mix_norm_g, m_w_in, m_b_conv_in, m_w_conv_dw, m_b_conv_dw, m_conv_ln_g, m_conv_ln_b, m_q_lat_norm_g, m_w_uq, m_kv_lat_norm_g, m_w_ukv, m_q_norm_g, m_k_norm_g, m_w_out, m_mem_norm_x_g, m_mem_norm_m_g, m_w_mem_q, m_w_mem_kv, m_mem_q_norm_g, m_mem_k_norm_g, m_w_mem_o, m_ffn_norm_g, m_w_up, m_w_ffn_dw, m_b_ffn_dw, m_w_down, v_mix_norm_g, v_w_in, v_b_conv_in, v_w_conv_dw, v_b_conv_dw, v_conv_ln_g, v_conv_ln_b, v_q_lat_norm_g, v_w_uq, v_kv_lat_norm_g, v_w_ukv, v_q_norm_g, v_k_norm_g, v_w_out, v_mem_norm_x_g, v_mem_norm_m_g, v_w_mem_q, v_w_mem_kv, v_mem_q_norm_g, v_mem_k_norm_g, v_w_mem_o, v_ffn_norm_g, v_w_up, v_w_ffn_dw, v_b_ffn_dw, v_w_down):
    names = ["mix_norm_g", "w_in", "b_conv_in", "w_conv_dw", "b_conv_dw", "conv_ln_g", "conv_ln_b", "q_lat_norm_g", "w_uq",
             "kv_lat_norm_g", "w_ukv", "q_norm_g", "k_norm_g", "w_out", "mem_norm_x_g", "mem_norm_m_g", "w_mem_q", "w_mem_kv",
             "mem_q_norm_g", "mem_k_norm_g", "w_mem_o", "ffn_norm_g", "w_up", "w_ffn_dw", "b_ffn_dw", "w_down"]
    loc = locals()
    w = {n: loc[n] for n in names}
    m = {n: loc["m_" + n] for n in names}
    v = {n: loc["v_" + n] for n in names}
    shard_idx = 2 * lax.axis_index("x") + lax.axis_index("y")

    shard_idx = shard_idx.astype(jnp.int32)
    core_idx = lax.axis_index("c").astype(jnp.int32)

    w_local = {n: w[n][0] for n, _, _ in BIG}
    w_early = _pack_big_shards(w_local, AG_EARLY).astype(BF16)
    w_late = _pack_big_shards(w_local, AG_LATE).astype(BF16)
    wf = _unpack_gathered(lax.dynamic_update_index_in_dim(_ag_weights(w_early), w_early, shard_idx, 0), AG_EARLY)

    small_sh_full = {}
    gather_in = []
    for n, (r, c) in SMALL_SH:
        csh = c // 4
        slab = lax.dynamic_update_slice(jnp.zeros((r, c), F32), w[n][0], (0, shard_idx * csh))
        gather_in.append(slab.reshape(-1))
    gather_rows = 256
    gathered_small = _allreduce_small_named(_pack_rows(gather_in, gather_rows, SMALL_COLS), "gather_small") * 0.5
    off = 0
    for n, (r, c) in SMALL_SH:
        small_sh_full[n] = gathered_small.reshape(-1)[off:off + r * c].reshape(r, c)
        off += r * c
    sp = {n: w[n][0] for n, _ in SMALL_REP}
    sp.update(small_sh_full)

    lsum, grad_x, g, g_ffn_packed = _layer_grads(x[0], mem[0], positions[0], loss_target[0], wf, w_late, sp, shard_idx, core_idx)

    small_parts = [jnp.full((SMALL_COLS,), lsum, F32)] + [g[n] for n, _ in SMALL_REP] + [g[n] for n, _ in SMALL_SH]
    small_rows = 368
    small_sum = _allreduce_small_named(_pack_rows(small_parts, small_rows, SMALL_COLS), "allreduce_small").reshape(-1)
    loss = small_sum[0] * (0.5 / D_MODEL)
    gs = {}
    off = SMALL_COLS
    for n, sz in SMALL_REP:
        gs[n] = small_sum[off:off + sz].reshape(w[n].shape)
        off += sz
    for n, (r, c) in SMALL_SH:
        full = small_sum[off:off + r * c].reshape(r, c)
        gs[n] = lax.dynamic_slice(full, (0, shard_idx * (c // 4)), (r, c // 4)).reshape(w[n].shape)
        off += r * c

    chipsum, chipsum_wire = _rs_first(_pack_full_grads(g, RS_REST), core_idx, "_rest")
    g_rest_packed = _rs_last(chipsum, _rs_to_owner(chipsum_wire), shard_idx, core_idx, "_rest")
    g_big = {**_unpack_big_shards(g_rest_packed, RS_REST), **_unpack_big_shards(g_ffn_packed, RS_FFN)}
    gs.update({n: a[None] for n, a in g_big.items()})

    delta, new_m, new_v = {}, {}, {}
    for n, _, _ in BIG:
        d_n, m_n, v_n = _adamw(w[n][0], g_big[n], m[n][0], v[n][0], "adamw_" + n)
        delta[n], new_m[n], new_v[n] = d_n[None], m_n[None], v_n[None]
    small_names = [n for n, _ in SMALL_REP] + [n for n, _ in SMALL_SH]
    as2d = lambda a: a.reshape(-1, a.shape[-1])
    d_s, m_s, v_s = _adamw_small(*[[as2d(d[n]) for n in small_names] for d in (w, gs, m, v)])
    for k, n in enumerate(small_names):
        delta[n], new_m[n], new_v[n] = d_s[k].reshape(w[n].shape), m_s[k].reshape(w[n].shape), v_s[k].reshape(w[n].shape)

    return (loss, grad_x[None], *[gs[n] for n in names], *[delta[n] for n in names], *[new_m[n] for n in names],
            *[new_v[n] for n in names])


def _allreduce_small_named(v, name):
    rows, cols = v.shape

    def body(v_ref, out_ref, buf, send_sems, recv_sems):
        x, y, c = _coords()
        me = 4 * x + 2 * y + c
        buf[me] = v_ref[...]
        cps = []
        for r in range(1, 8):
            dx, dy, dc = (r >> 2) & 1, (r >> 1) & 1, r & 1
            to = (x + dx - 2 * x * dx, y + dy - 2 * y * dy, c + dc - 2 * c * dc)
            cp = pltpu.make_async_remote_copy(src_ref=v_ref, dst_ref=buf.at[me], send_sem=send_sems.at[r - 1],
                                              recv_sem=recv_sems.at[r - 1], device_id=to, device_id_type=MESH)
            cp.start()
            cps.append(cp)
        for cp in cps:
            cp.wait()
        acc = buf[0]
        for d in range(1, 8):
            acc = acc + buf[d]
        out_ref[...] = acc

    vm = pl.BlockSpec(memory_space=pltpu.VMEM)
    return pl.pallas_call(
        body, in_specs=[vm], out_specs=vm, out_shape=SDS((rows, cols), F32),
        scratch_shapes=[pltpu.VMEM((8, rows, cols), F32), pltpu.SemaphoreType.DMA((7,)), pltpu.SemaphoreType.DMA((7,))],
        name=name)(v)
```

```python
import math

import numpy as np
import jax
import jax.numpy as jnp
from jax import lax
from jax.experimental import pallas as pl
from jax.experimental.pallas import tpu as pltpu

F32 = jnp.float32
BF16 = jnp.bfloat16
SDS = jax.ShapeDtypeStruct
MESH = pl.DeviceIdType.MESH

D_MODEL = 1024
EPS = 1e-6
CONV_CH = 512
CONV_WIDTH = 31
CONV_HALO = 32
HEADS = 8
NOPE = 64
ROPE = 32
HEAD_DIM = NOPE + ROPE
HEAD_PAD = 128
Q_RANK = 256
KV_RANK = 128
CHUNK = 64
ROPE_THETA = 10000.0
IN_COLS_PAD = 1536
MEM_HEADS = 4
MEM_HEAD_DIM = 256
MEM_LEN = 256
D_FF = 2816
FFN_HALO = 8
ATT_SCALE = 1.0 / math.sqrt(HEAD_DIM)
LOG2E = math.log2(math.e)
LN2 = math.log(2.0)

ADAM_LR = 0.001
ADAM_B1 = 0.9
ADAM_B2 = 0.999
ADAM_EPS = 1e-08
ADAM_WD = 0.01
ADAM_STEP = 10

VMEM_LIMIT_V7X = 56 * 1024 * 1024
PACK_COLS = 1024
SMALL_COLS = 128


def _cp(n_axes):
    return pltpu.CompilerParams(dimension_semantics=("arbitrary",) * n_axes, vmem_limit_bytes=VMEM_LIMIT_V7X)


def _row_tile(s, want):
    return want if s % want == 0 else s


def _norm_linear(x, xcol, kdim, g, w, out_dtype, tm, tn, name):
    s = x.shape[0]
    n = w.shape[1]

    def body(x_ref, g_ref, w_ref, y_ref, hn_ref):
        @pl.when(pl.program_id(1) == 0)
        def _():
            xv = x_ref[...]
            r = lax.rsqrt(jnp.mean(xv * xv, axis=-1, keepdims=True) + EPS)
            hn_ref[...] = ((xv * r) * g_ref[...]).astype(BF16)

        y_ref[...] = jnp.dot(hn_ref[...], w_ref[...], preferred_element_type=F32).astype(y_ref.dtype)

    return pl.pallas_call(
        body, grid=(s // tm, n // tn),
        in_specs=[pl.BlockSpec((tm, kdim), lambda i, j: (i, xcol)), pl.BlockSpec((1, kdim), lambda i, j: (0, 0)),
                  pl.BlockSpec((kdim, tn), lambda i, j: (0, j))],
        out_specs=[pl.BlockSpec((tm, tn), lambda i, j: (i, j)), pl.BlockSpec((tm, kdim), lambda i, j: (i, 0))],
        out_shape=[SDS((s, n), out_dtype), SDS((s, kdim), BF16)],
        compiler_params=_cp(2), name=name)(x, g, w)


def _linear(pairs, nt, residual, out_dtypes, tm, tn, name):
    s = pairs[0][0].shape[0]
    n = pairs[0][1].shape[0] if nt else pairs[0][1].shape[1]
    n_pairs = len(pairs)
    has_res = residual is not None

    def body(*refs):
        a_refs = refs[:n_pairs]
        w_refs = refs[n_pairs:2 * n_pairs]
        res_ref = refs[2 * n_pairs] if has_res else None
        outs = refs[2 * n_pairs + int(has_res):]
        acc = None
        for a_ref, w_ref in zip(a_refs, w_refs):
            a = a_ref[...].astype(BF16)
            if nt:
                d = lax.dot_general(a, w_ref[...], (((1,), (1,)), ((), ())), preferred_element_type=F32)
            else:
                d = jnp.dot(a, w_ref[...], preferred_element_type=F32)
            acc = d if acc is None else acc + d
        if has_res:
            acc = res_ref[...] + acc
        for o in outs:
            o[...] = acc.astype(o.dtype)

    in_specs = [pl.BlockSpec((tm, a.shape[1]), lambda i, j: (i, 0)) for a, _ in pairs]
    if nt:
        in_specs += [pl.BlockSpec((tn, w.shape[1]), lambda i, j: (j, 0)) for _, w in pairs]
    else:
        in_specs += [pl.BlockSpec((w.shape[0], tn), lambda i, j: (0, j)) for _, w in pairs]
    args = [a for a, _ in pairs] + [w for _, w in pairs]
    if has_res:
        in_specs.append(pl.BlockSpec((tm, tn), lambda i, j: (i, j)))
        args.append(residual)
    outs = pl.pallas_call(
        body, grid=(s // tm, n // tn), in_specs=in_specs,
        out_specs=[pl.BlockSpec((tm, tn), lambda i, j: (i, j)) for _ in out_dtypes],
        out_shape=[SDS((s, n), dt) for dt in out_dtypes],
        compiler_params=_cp(2), name=name)(*args)
    return outs


def _linear_normbwd(pairs, x, xcol, g, d_res, out_dtypes, tm, name):
    s = pairs[0][0].shape[0]
    dn = pairs[0][1].shape[0]
    n_pairs = len(pairs)
    has_res = d_res is not None

    def body(*refs):
        a_refs = refs[:n_pairs]
        w_refs = refs[n_pairs:2 * n_pairs]
        x_ref, g_ref = refs[2 * n_pairs], refs[2 * n_pairs + 1]
        k = 2 * n_pairs + 2
        res_ref = refs[k] if has_res else None
        k += int(has_res)
        outs = refs[k:-1]
        dg_ref = refs[-1]
        dh = None
        for a_ref, w_ref in zip(a_refs, w_refs):
            d = lax.dot_general(a_ref[...].astype(BF16), w_ref[...], (((1,), (1,)), ((), ())), preferred_element_type=F32)
            dh = d if dh is None else dh + d
        xv = x_ref[...]
        r = lax.rsqrt(jnp.mean(xv * xv, axis=-1, keepdims=True) + EPS)
        y = xv * r

        @pl.when(pl.program_id(0) == 0)
        def _():
            dg_ref[...] = jnp.zeros_like(dg_ref)

        dg_ref[...] += jnp.sum(dh * y, axis=0, keepdims=True)
        dy = dh * g_ref[...]
        dx = r * (dy - y * jnp.mean(dy * y, axis=-1, keepdims=True))
        if has_res:
            dx = res_ref[...] + dx
        for o in outs:
            o[...] = dx.astype(o.dtype)

    in_specs = [pl.BlockSpec((tm, a.shape[1]), lambda i: (i, 0)) for a, _ in pairs]
    in_specs += [pl.BlockSpec((dn, w.shape[1]), lambda i: (0, 0)) for _, w in pairs]
    in_specs += [pl.BlockSpec((tm, dn), lambda i: (i, xcol)), pl.BlockSpec((1, dn), lambda i: (0, 0))]
    args = [a for a, _ in pairs] + [w for _, w in pairs] + [x, g]
    if has_res:
        in_specs.append(pl.BlockSpec((tm, dn), lambda i: (i, 0)))
        args.append(d_res)
    outs = pl.pallas_call(
        body, grid=(s // tm,), in_specs=in_specs,
        out_specs=[pl.BlockSpec((tm, dn), lambda i: (i, 0)) for _ in out_dtypes] + [pl.BlockSpec((1, dn), lambda i: (0, 0))],
        out_shape=[SDS((s, dn), dt) for dt in out_dtypes] + [SDS((1, dn), F32)],
        compiler_params=_cp(1), name=name)(*args)
    return outs


def _dw_matmul(a, b, tk, tn, ts, name):
    s, ka = a.shape
    n = b.shape[1]

    def body(a_ref, b_ref, o_ref):
        @pl.when(pl.program_id(2) == 0)
        def _():
            o_ref[...] = jnp.zeros_like(o_ref)

        o_ref[...] += lax.dot_general(a_ref[...].astype(BF16), b_ref[...].astype(BF16), (((0,), (0,)), ((), ())),
                                      preferred_element_type=F32)

    return pl.pallas_call(
        body, grid=(ka // tk, n // tn, s // ts),
        in_specs=[pl.BlockSpec((ts, tk), lambda k, j, t: (t, k)), pl.BlockSpec((ts, tn), lambda k, j, t: (t, j))],
        out_specs=pl.BlockSpec((tk, tn), lambda k, j, t: (k, j)),
        out_shape=SDS((ka, n), F32), compiler_params=_cp(3), name=name)(a, b)


def _dw(a, b, name):
    s, ka = a.shape
    n = b.shape[1]
    tk = ka if ka <= 1024 else ka // 2
    tn = n if n <= 1024 else (n // 2 if n == D_FF else 512)
    return _dw_matmul(a, b, tk, tn, _row_tile(s, 2048), name)


def _prev_halo(tm, halo):
    return lambda i: (jnp.maximum(i * (tm // halo) - 1, 0), 0)


def _next_halo(tm, halo, s):
    return lambda i: (jnp.minimum((i + 1) * (tm // halo), s // halo - 1), 0)


def _shifted_copies(ext, tm):
    n = tm + CONV_HALO - 8
    for s in range(1, 8):
        ext[s, 0:n, :] = ext[0, s:s + n, :]


def _sum_taps(terms, ways=4):
    accs = []
    for i, t in enumerate(terms):
        if i < ways:
            accs.append(t)
        else:
            accs[i % ways] = accs[i % ways] + t
    while len(accs) > 1:
        accs = [accs[i] + accs[i + 1] if i + 1 < len(accs) else accs[i] for i in range(0, len(accs), 2)]
    return accs[0]


def _tap_rows(ext, o, n, cs):
    return ext[o % 8, o - o % 8:o - o % 8 + n, cs]


def _conv_fwd(z, b_in, w32, b_dw, ln_g, ln_b, tm):
    s = z.shape[0]
    c = CONV_CH

    def body(z_ref, zh_ref, bin_ref, w_ref, bdw_ref, lg_ref, lb_ref, u_ref, u0_ref, u1_ref, ext):
        i = pl.program_id(0)

        def glu(zz):
            zz = zz + bin_ref[...]
            return zz[:, :c] * jax.nn.sigmoid(zz[:, c:])

        u0 = glu(z_ref[...])
        u0_ref[...] = u0
        ext[0, 0:CONV_HALO, :] = jnp.where(i > 0, glu(zh_ref[...]), 0.0)
        ext[0, CONV_HALO:, :] = u0
        _shifted_copies(ext, tm)
        off = CONV_HALO - (CONV_WIDTH - 1)
        for r in range(tm // 64):
            for cb in range(c // 128):
                cs = slice(cb * 128, (cb + 1) * 128)
                u1_ref[r * 64:(r + 1) * 64, cs] = _sum_taps(
                    _tap_rows(ext, r * 64 + off + k, 64, cs) * w_ref[k:k + 1, cs] for k in range(CONV_WIDTH)) + bdw_ref[:, cs]
        u1 = u1_ref[...]
        mu = jnp.mean(u1, axis=-1, keepdims=True)
        xc = u1 - mu
        y = xc * lax.rsqrt(jnp.mean(xc * xc, axis=-1, keepdims=True) + EPS)
        y = y * lg_ref[...] + lb_ref[...]
        u_ref[...] = (y * jax.nn.sigmoid(y)).astype(BF16)

    row = lambda i: (i, 0)
    fix = lambda i: (0, 0)
    return pl.pallas_call(
        body, grid=(s // tm,),
        in_specs=[pl.BlockSpec((tm, 2 * c), row), pl.BlockSpec((CONV_HALO, 2 * c), _prev_halo(tm, CONV_HALO)),
                  pl.BlockSpec((1, 2 * c), fix), pl.BlockSpec((32, c), fix), pl.BlockSpec((1, c), fix),
                  pl.BlockSpec((1, c), fix), pl.BlockSpec((1, c), fix)],
        out_specs=[pl.BlockSpec((tm, c), row)] * 3,
        out_shape=[SDS((s, c), BF16), SDS((s, c), F32), SDS((s, c), F32)],
        scratch_shapes=[pltpu.VMEM((8, tm + CONV_HALO, c), F32)],
        compiler_params=_cp(1), name="conv_fwd")(z, z, b_in, w32, b_dw, ln_g, ln_b)


def _conv_bwd_ln(d_u, u1, ln_g, ln_b, tm):
    s = d_u.shape[0]
    c = CONV_CH

    def body(du_ref, u1_ref, lg_ref, lb_ref, du1_ref, dlg_ref, dlb_ref, dbdw_ref):
        @pl.when(pl.program_id(0) == 0)
        def _():
            dlg_ref[...] = jnp.zeros_like(dlg_ref)
            dlb_ref[...] = jnp.zeros_like(dlb_ref)
            dbdw_ref[...] = jnp.zeros_like(dbdw_ref)

        u1 = u1_ref[...]
        mu = jnp.mean(u1, axis=-1, keepdims=True)
        xc = u1 - mu
        rs = lax.rsqrt(jnp.mean(xc * xc, axis=-1, keepdims=True) + EPS)
        xh = xc * rs
        y = xh * lg_ref[...] + lb_ref[...]
        sg = jax.nn.sigmoid(y)
        dy = du_ref[...] * (sg * (1.0 + y * (1.0 - sg)))
        dlg_ref[...] += jnp.sum(dy * xh, axis=0, keepdims=True)
        dlb_ref[...] += jnp.sum(dy, axis=0, keepdims=True)
        dxh = dy * lg_ref[...]
        du1 = rs * (dxh - jnp.mean(dxh, axis=-1, keepdims=True) - xh * jnp.mean(dxh * xh, axis=-1, keepdims=True))
        dbdw_ref[...] += jnp.sum(du1, axis=0, keepdims=True)
        du1_ref[...] = du1

    row = lambda i: (i, 0)
    fix = lambda i: (0, 0)
    return pl.pallas_call(
        body, grid=(s // tm,),
        in_specs=[pl.BlockSpec((tm, c), row), pl.BlockSpec((tm, c), row), pl.BlockSpec((1, c), fix), pl.BlockSpec((1, c), fix)],
        out_specs=[pl.BlockSpec((tm, c), row)] + [pl.BlockSpec((1, c), fix)] * 3,
        out_shape=[SDS((s, c), F32)] + [SDS((1, c), F32)] * 3,
        compiler_params=_cp(1), name="conv_bwd_ln")(d_u, u1, ln_g, ln_b)


def _conv_bwd_dw(d_u1, u0, z, b_in, w32, tm):
    s = d_u1.shape[0]
    c = CONV_CH

    def body(d_ref, dn_ref, u0_ref, u0p_ref, z_ref, bin_ref, w_ref, dz_ref, dw_ref, dbin_ref, extd, extu, du0):
        i = pl.program_id(0)
        last = pl.num_programs(0) - 1

        @pl.when(i == 0)
        def _():
            dw_ref[...] = jnp.zeros_like(dw_ref)
            dbin_ref[...] = jnp.zeros_like(dbin_ref)

        extd[0, 0:tm, :] = d_ref[...]
        extd[0, tm:, :] = jnp.where(i < last, dn_ref[...], 0.0)
        extu[0, 0:CONV_HALO, :] = jnp.where(i > 0, u0p_ref[...], 0.0)
        extu[0, CONV_HALO:, :] = u0_ref[...]
        _shifted_copies(extd, tm)
        _shifted_copies(extu, tm)
        off = CONV_HALO - (CONV_WIDTH - 1)
        for r in range(tm // 64):
            for cb in range(c // 128):
                cs = slice(cb * 128, (cb + 1) * 128)
                du0[r * 64:(r + 1) * 64, cs] = _sum_taps(
                    _tap_rows(extd, r * 64 + (CONV_WIDTH - 1) - k, 64, cs) * w_ref[k:k + 1, cs] for k in range(CONV_WIDTH))
        for cb in range(c // 128):
            cs = slice(cb * 128, (cb + 1) * 128)
            accs = [None] * CONV_WIDTH
            for r in range(tm // 64):
                d = d_ref[r * 64:(r + 1) * 64, cs]
                for k in range(CONV_WIDTH):
                    p = d * _tap_rows(extu, r * 64 + off + k, 64, cs)
                    part = _sum_taps(p[q * 8:(q + 1) * 8, :] for q in range(8))
                    accs[k] = part if accs[k] is None else accs[k] + part
            for k in range(CONV_WIDTH):
                dw_ref[k:k + 1, cs] += jnp.sum(accs[k], axis=0, keepdims=True)
        zz = z_ref[...] + bin_ref[...]
        a = zz[:, :c]
        sg = jax.nn.sigmoid(zz[:, c:])
        d0 = du0[...]
        da = d0 * sg
        dgt = d0 * a * (sg * (1.0 - sg))
        dbin_ref[:, :c] += jnp.sum(da, axis=0, keepdims=True)
        dbin_ref[:, c:] += jnp.sum(dgt, axis=0, keepdims=True)
        dz_ref[:, :c] = da.astype(BF16)
        dz_ref[:, c:] = dgt.astype(BF16)

    row = lambda i: (i, 0)
    fix = lambda i: (0, 0)
    return pl.pallas_call(
        body, grid=(s // tm,),
        in_specs=[pl.BlockSpec((tm, c), row), pl.BlockSpec((CONV_HALO, c), _next_halo(tm, CONV_HALO, s)),
                  pl.BlockSpec((tm, c), row), pl.BlockSpec((CONV_HALO, c), _prev_halo(tm, CONV_HALO)),
                  pl.BlockSpec((tm, 2 * c), row), pl.BlockSpec((1, 2 * c), fix), pl.BlockSpec((32, c), fix)],
        out_specs=[pl.BlockSpec((tm, 2 * c), row), pl.BlockSpec((32, c), fix), pl.BlockSpec((1, 2 * c), fix)],
        out_shape=[SDS((s, 2 * c), BF16), SDS((32, c), F32), SDS((1, 2 * c), F32)],
        scratch_shapes=[pltpu.VMEM((8, tm + CONV_HALO, c), F32), pltpu.VMEM((8, tm + CONV_HALO, c), F32), pltpu.VMEM((tm, c), F32)],
        compiler_params=_cp(1), name="conv_bwd_dw")(d_u1, d_u1, u0, u0, z, b_in, w32)


def _partner(v, lane):
    up = pltpu.roll(v, HEAD_PAD - ROPE // 2, 1)
    dn = pltpu.roll(v, ROPE // 2, 1)
    lo = (lane >= NOPE) & (lane < NOPE + ROPE // 2)
    hi = (lane >= NOPE + ROPE // 2) & (lane < HEAD_DIM)
    return jnp.where(lo, up, jnp.where(hi, dn, 0.0))


def _mla_prep(q_raw, kv_raw, z, cosf, sinf, gq, gk, tm):
    s = q_raw.shape[0]

    def body(q_ref, kv_ref, kr_ref, c_ref, s_ref, gq_ref, gk_ref, qo_ref, ko_ref, vo_ref):
        lane = lax.broadcasted_iota(jnp.int32, (tm, HEAD_PAD), 1)
        cf = c_ref[...]
        sf = s_ref[...]

        def norm_rope(t, g_ref):
            r = lax.rsqrt(jnp.sum(t * t, axis=-1, keepdims=True) * (1.0 / HEAD_DIM) + EPS)
            tn = (t * r) * g_ref[...]
            return tn * cf + _partner(tn, lane) * sf

        kr = kr_ref[...]
        for h in range(HEADS):
            hs = slice(h * HEAD_PAD, (h + 1) * HEAD_PAD)
            qo_ref[:, hs] = (norm_rope(q_ref[:, hs], gq_ref) * (ATT_SCALE * LOG2E)).astype(BF16)
            kv = kv_ref[:, hs]
            ko_ref[:, hs] = norm_rope(jnp.where(lane < NOPE, kv, 0.0) + kr, gk_ref).astype(BF16)
            vo_ref[:, hs] = jnp.where(lane >= NOPE, kv, 0.0).astype(BF16)

    row = lambda i: (i, 0)
    wide = pl.BlockSpec((tm, HEADS * HEAD_PAD), row)
    one = pl.BlockSpec((tm, HEAD_PAD), row)
    return pl.pallas_call(
        body, grid=(s // tm,),
        in_specs=[wide, wide, pl.BlockSpec((tm, HEAD_PAD), lambda i: (i, IN_COLS_PAD // HEAD_PAD - 1)), one, one,
                  pl.BlockSpec((1, HEAD_PAD), lambda i: (0, 0)), pl.BlockSpec((1, HEAD_PAD), lambda i: (0, 0))],
        out_specs=[wide] * 3,
        out_shape=[SDS((s, HEADS * HEAD_PAD), BF16)] * 3,
        compiler_params=_cp(1), name="mla_prep")(q_raw, kv_raw, z, cosf, sinf, gq, gk)


def _mla_prep_bwd(dqp, dkp, dvp, q_raw, kv_raw, z, cosf, sinf, gq, gk, tm):
    s = q_raw.shape[0]

    def body(dq_ref, dk_ref, dv_ref, q_ref, kv_ref, kr_ref, c_ref, s_ref, gq_ref, gk_ref,
             dqo_ref, dkvo_ref, dkr_ref, dgq_ref, dgk_ref):
        lane = lax.broadcasted_iota(jnp.int32, (tm, HEAD_PAD), 1)
        cf = c_ref[...]
        sf = s_ref[...]

        @pl.when(pl.program_id(0) == 0)
        def _():
            dgq_ref[...] = jnp.zeros_like(dgq_ref)
            dgk_ref[...] = jnp.zeros_like(dgk_ref)

        def norm_rope_bwd(t, d_out, g_ref, dg_ref):
            r = lax.rsqrt(jnp.sum(t * t, axis=-1, keepdims=True) * (1.0 / HEAD_DIM) + EPS)
            th = t * r
            dn = d_out * cf + _partner(d_out * sf, lane)
            dg_ref[...] += jnp.sum(dn * th, axis=0, keepdims=True)
            dh = dn * g_ref[...]
            return r * (dh - th * (jnp.sum(dh * th, axis=-1, keepdims=True) * (1.0 / HEAD_DIM)))

        kr = kr_ref[...]
        dkr = None
        for h in range(HEADS):
            hs = slice(h * HEAD_PAD, (h + 1) * HEAD_PAD)
            dq = norm_rope_bwd(q_ref[:, hs], dq_ref[:, hs] * ATT_SCALE, gq_ref, dgq_ref)
            dqo_ref[:, hs] = dq.astype(BF16)
            kv = kv_ref[:, hs]
            dkpre = norm_rope_bwd(jnp.where(lane < NOPE, kv, 0.0) + kr, dk_ref[:, hs] * LN2, gk_ref, dgk_ref)
            dkvo_ref[:, hs] = jnp.where(lane < NOPE, dkpre, dv_ref[:, hs]).astype(BF16)
            dkr_h = jnp.where((lane >= NOPE) & (lane < HEAD_DIM), dkpre, 0.0)
            dkr = dkr_h if dkr is None else dkr + dkr_h
        dkr_ref[...] = dkr

    row = lambda i: (i, 0)
    fix = lambda i: (0, 0)
    wide = pl.BlockSpec((tm, HEADS * HEAD_PAD), row)
    one = pl.BlockSpec((tm, HEAD_PAD), row)
    return pl.pallas_call(
        body, grid=(s // tm,),
        in_specs=[wide, wide, wide, wide, wide, pl.BlockSpec((tm, HEAD_PAD), lambda i: (i, IN_COLS_PAD // HEAD_PAD - 1)),
                  one, one, pl.BlockSpec((1, HEAD_PAD), fix), pl.BlockSpec((1, HEAD_PAD), fix)],
        out_specs=[wide, wide, one, pl.BlockSpec((1, HEAD_PAD), fix), pl.BlockSpec((1, HEAD_PAD), fix)],
        out_shape=[SDS((s, HEADS * HEAD_PAD), BF16), SDS((s, HEADS * HEAD_PAD), BF16), SDS((s, HEAD_PAD), F32),
                   SDS((1, HEAD_PAD), F32), SDS((1, HEAD_PAD), F32)],
        compiler_params=_cp(1), name="mla_prep_bwd")(dqp, dkp, dvp, q_raw, kv_raw, z, cosf, sinf, gq, gk)


ATT_GROUP = 8


def _group_schedule(nb, forward):
    g = ATT_GROUP
    one, two, case = [], [], []
    for a in range(nb):
        for b in (range(a // g + 1) if forward else range(a // g, nb // g)):
            one.append(a)
            two.append(b)
            case.append(0 if b != a // g else 1 + a % g)
    return tuple(jnp.asarray(np.array(x, np.int32)) for x in (one, two, case))


STRIP = 64


def _fold8(x):
    acc = x[0:8, :]
    for g in range(1, x.shape[0] // 8):
        acc = acc + x[g * 8:(g + 1) * 8, :]
    return acc


def _attn_fwd(qp, kp, vp, tb, w_late):
    s = qp.shape[0]
    ii, jj, cc = _group_schedule(s // tb, True)
    n_steps = int(ii.shape[0])

    def body(ii_ref, jj_ref, cc_ref, q_ref, k_ref, v_ref, wl_ref, of_ref, ob_ref, lse_ref, gl_ref,
             m_sc, l_sc, acc_sc, st_sc, pt_sc, send_sems, recv_sems):
        t = pl.program_id(1)
        case = cc_ref[t]

        @pl.when((pl.program_id(0) == 0) & (t == 0))
        def _():
            _ag_send(wl_ref, gl_ref, send_sems, recv_sems)

        @pl.when(jj_ref[t] == 0)
        def _():
            m_sc[...] = jnp.full_like(m_sc, -jnp.inf)
            l_sc[...] = jnp.zeros_like(l_sc)
            acc_sc[...] = jnp.zeros_like(acc_sc)

        def step(n_keys, diag_at):
            def visible(r):
                if diag_at is None or r * STRIP <= diag_at:
                    return None
                col = lax.broadcasted_iota(jnp.int32, (STRIP, tb), 1)
                return col >= r * STRIP - diag_at

            st_sc[0:n_keys, :] = lax.dot_general(k_ref[0:n_keys, :], q_ref[...], (((1,), (1,)), ((), ())), preferred_element_type=F32)
            mx = None
            for r in range(n_keys // STRIP):
                sc = st_sc[r * STRIP:(r + 1) * STRIP, :]
                if visible(r) is not None:
                    sc = jnp.where(visible(r), sc, -jnp.inf)
                m8 = sc[0:8, :]
                for g in range(1, STRIP // 8):
                    m8 = jnp.maximum(m8, sc[g * 8:(g + 1) * 8, :])
                mx = m8 if mx is None else jnp.maximum(mx, m8)
            m_old = m_sc[0:1, :]
            m_new = jnp.maximum(m_old, jnp.max(mx, axis=0, keepdims=True))
            alpha = jnp.exp2(m_old - m_new)
            ps = None
            pv = None
            for b in range(n_keys // tb):
                for r in range(b * tb // STRIP, (b + 1) * tb // STRIP):
                    p = jnp.exp2(st_sc[r * STRIP:(r + 1) * STRIP, :] - m_new)
                    if visible(r) is not None:
                        p = jnp.where(visible(r), p, 0.0)
                    ps = _fold8(p) if ps is None else ps + _fold8(p)
                    pt_sc[r * STRIP:(r + 1) * STRIP, :] = p.astype(BF16)
                pvb = lax.dot_general(v_ref[b * tb:(b + 1) * tb, :], pt_sc[b * tb:(b + 1) * tb, :], (((0,), (0,)), ((), ())),
                                      preferred_element_type=F32)
                pv = pvb if pv is None else pv + pvb
            l_new = alpha * l_sc[0:1, :] + jnp.sum(ps, axis=0, keepdims=True)
            m_sc[...] = jnp.broadcast_to(m_new, m_sc.shape)
            l_sc[...] = jnp.broadcast_to(l_new, l_sc.shape)
            acc_sc[...] = alpha * acc_sc[...] + pv

        @pl.when(case == 0)
        def _():
            step(ATT_GROUP * tb, None)

        for d in range(ATT_GROUP):
            @pl.when(case == 1 + d)
            def _(d=d):
                step((d + 1) * tb, d * tb)

        @pl.when(case != 0)
        def _():
            l = l_sc[0:1, :]
            o = (acc_sc[...] / l).T
            of_ref[...] = o
            ob_ref[...] = o.astype(BF16)
            lse_ref[...] = m_sc[0:1, :] + jnp.log(l) * LOG2E

        @pl.when((pl.program_id(0) == HEADS - 1) & (t == n_steps - 1))
        def _():
            _ag_finish(wl_ref, gl_ref, send_sems, recv_sems)

    qmap = lambda h, t, ii_ref, jj_ref, cc_ref: (ii_ref[t], h)
    kmap = lambda h, t, ii_ref, jj_ref, cc_ref: (jj_ref[t], h)
    gs = pltpu.PrefetchScalarGridSpec(
        num_scalar_prefetch=3, grid=(HEADS, n_steps),
        in_specs=[pl.BlockSpec((tb, HEAD_PAD), qmap), pl.BlockSpec((ATT_GROUP * tb, HEAD_PAD), kmap),
                  pl.BlockSpec((ATT_GROUP * tb, HEAD_PAD), kmap), ANY],
        out_specs=[pl.BlockSpec((tb, HEAD_PAD), qmap), pl.BlockSpec((tb, HEAD_PAD), qmap),
                   pl.BlockSpec((None, 1, tb), lambda h, t, ii_ref, jj_ref, cc_ref: (h, 0, ii_ref[t])), ANY],
        scratch_shapes=[pltpu.VMEM((8, tb), F32), pltpu.VMEM((8, tb), F32), pltpu.VMEM((HEAD_PAD, tb), F32),
                        pltpu.VMEM((ATT_GROUP * tb, tb), F32), pltpu.VMEM((ATT_GROUP * tb, tb), BF16),
                        pltpu.SemaphoreType.DMA((6,)), pltpu.SemaphoreType.DMA((6,))])
    w = HEADS * HEAD_PAD
    return pl.pallas_call(
        body, grid_spec=gs,
        out_shape=[SDS((s, w), F32), SDS((s, w), BF16), SDS((HEADS, 1, s), F32), SDS((4,) + w_late.shape, w_late.dtype)],
        compiler_params=_cp(2), name="attn_fwd")(ii, jj, cc, qp, kp, vp, w_late)


def _attn_delta(do, o, tb):
    s = do.shape[0]

    def body(do_ref, o_ref, d_ref):
        for h in range(HEADS):
            hs = slice(h * HEAD_PAD, (h + 1) * HEAD_PAD)
            d_ref[h] = jnp.sum((do_ref[:, hs] * o_ref[:, hs]).T, axis=0, keepdims=True)

    blk = pl.BlockSpec((tb, HEADS * HEAD_PAD), lambda i: (i, 0))
    return pl.pallas_call(body, grid=(s // tb,), in_specs=[blk, blk],
                          out_specs=pl.BlockSpec((HEADS, 1, tb), lambda i: (0, 0, i)),
                          out_shape=SDS((HEADS, 1, s), F32), compiler_params=_cp(1), name="attn_delta")(do, o)


def _attn_bwd(qp, kp, vp, dob, lse, delta, tb, wire):
    s = qp.shape[0]
    nb = s // tb
    jj, ii, cc = _group_schedule(nb, False)
    gw = ATT_GROUP * tb
    rows = 32

    n_steps = int(ii.shape[0])

    def body(jj_ref, ii_ref, cc_ref, q_ref, k_ref, v_ref, do_ref, lse_ref, dl_ref, cw_ref, dq_ref, dk_ref, dv_ref, rcv_ref,
             st_sc, dpt_sc, pt_sc, dst_sc, send_sems, recv_sems):
        t = pl.program_id(1)
        case = cc_ref[t]
        pair = ii_ref[t]

        @pl.when((pl.program_id(0) == 0) & (t == 0))
        def _():
            _rs_send(cw_ref, rcv_ref, send_sems, recv_sems)

        @pl.when((pl.program_id(0) == HEADS - 1) & (t == n_steps - 1))
        def _():
            _rs_wait(rcv_ref, send_sems, recv_sems)

        @pl.when(t == 0)
        def _():
            dq_ref[...] = jnp.zeros_like(dq_ref)

        @pl.when(case != 0)
        def _():
            dk_ref[...] = jnp.zeros_like(dk_ref)
            dv_ref[...] = jnp.zeros_like(dv_ref)

        def step(lo, width, diag):
            q = q_ref[lo:lo + width, :]
            do = do_ref[lo:lo + width, :]
            k = k_ref[...]
            st_sc[:, 0:width] = lax.dot_general(k, q, (((1,), (1,)), ((), ())), preferred_element_type=F32)
            dpt_sc[:, 0:width] = lax.dot_general(v_ref[...], do, (((1,), (1,)), ((), ())), preferred_element_type=F32)
            dv = dk = None
            for h in range(width // tb):
                ls = slice(h * tb, (h + 1) * tb)
                lse_row = lse_ref[:, lo + h * tb:lo + (h + 1) * tb]
                dl_row = dl_ref[:, lo + h * tb:lo + (h + 1) * tb]
                for r in range(tb // rows):
                    rs = slice(r * rows, (r + 1) * rows)
                    p = jnp.exp2(st_sc[rs, ls] - lse_row)
                    first_visible = (r * rows) // CHUNK * CHUNK
                    if diag and h == 0 and first_visible > 0:
                        col = lax.broadcasted_iota(jnp.int32, (rows, tb), 1)
                        p = jnp.where(col >= first_visible, p, 0.0)
                    ds = p * (dpt_sc[rs, ls] - dl_row)
                    pt_sc[rs, ls] = p.astype(BF16)
                    dst_sc[rs, ls] = ds.astype(BF16)
                dst = dst_sc[:, ls]
                dvh = jnp.dot(pt_sc[:, ls], do[ls, :], preferred_element_type=F32)
                dkh = jnp.dot(dst, q[ls, :], preferred_element_type=F32)
                dv = dvh if dv is None else dv + dvh
                dk = dkh if dk is None else dk + dkh
                dq_ref[ATT_GROUP * pair + lo // tb + h] += lax.dot_general(k, dst, (((0,), (0,)), ((), ())), preferred_element_type=F32)
            dv_ref[...] += dv
            dk_ref[...] += dk

        @pl.when(case == 0)
        def _():
            step(0, ATT_GROUP * tb, False)

        for d in range(ATT_GROUP):
            @pl.when(case == 1 + d)
            def _(d=d):
                step(d * tb, (ATT_GROUP - d) * tb, True)

    qmap = lambda h, t, jj_ref, ii_ref, cc_ref: (ii_ref[t], h)
    kmap = lambda h, t, jj_ref, ii_ref, cc_ref: (jj_ref[t], h)
    rowmap = lambda h, t, jj_ref, ii_ref, cc_ref: (h, 0, ii_ref[t])
    gs = pltpu.PrefetchScalarGridSpec(
        num_scalar_prefetch=3, grid=(HEADS, n_steps),
        in_specs=[pl.BlockSpec((gw, HEAD_PAD), qmap), pl.BlockSpec((tb, HEAD_PAD), kmap), pl.BlockSpec((tb, HEAD_PAD), kmap),
                  pl.BlockSpec((gw, HEAD_PAD), qmap), pl.BlockSpec((None, 1, gw), rowmap), pl.BlockSpec((None, 1, gw), rowmap), ANY],
        out_specs=[pl.BlockSpec((None, nb, HEAD_PAD, tb), lambda h, t, jj_ref, ii_ref, cc_ref: (h, 0, 0, 0)),
                   pl.BlockSpec((tb, HEAD_PAD), kmap), pl.BlockSpec((tb, HEAD_PAD), kmap), ANY],
        scratch_shapes=[pltpu.VMEM((tb, gw), F32), pltpu.VMEM((tb, gw), F32), pltpu.VMEM((tb, gw), BF16),
                        pltpu.VMEM((tb, gw), BF16), pltpu.SemaphoreType.DMA((3,)), pltpu.SemaphoreType.DMA((3,))])
    w = HEADS * HEAD_PAD
    dqt, dk, dv, recv = pl.pallas_call(
        body, grid_spec=gs,
        out_shape=[SDS((HEADS, nb, HEAD_PAD, tb), F32), SDS((s, w), F32), SDS((s, w), F32), SDS((3,) + wire.shape[1:], wire.dtype)],
        compiler_params=_cp(2), name="attn_bwd")(jj, ii, cc, qp, kp, vp, dob, lse, delta, wire)
    return jnp.transpose(dqt, (1, 3, 0, 2)).reshape(s, w), dk, dv, recv


def _head_norm(t, g):
    r = lax.rsqrt(jnp.mean(t * t, axis=-1, keepdims=True) + EPS)
    th = t * r
    return r, th, th * g


def _softmax_rows(sc):
    m = jnp.max(sc, axis=-1, keepdims=True)
    e = jnp.exp(sc - m)
    return e / jnp.sum(e, axis=-1, keepdims=True)


def _memattn_fwd(qm, kvm, gq, gk, tm):
    s = qm.shape[0]
    hd = MEM_HEAD_DIM

    def body(q_ref, k_ref, v_ref, gq_ref, gk_ref, o_ref):
        _, _, qn = _head_norm(q_ref[...], gq_ref[...])
        _, _, kn = _head_norm(k_ref[...], gk_ref[...])
        sc = lax.dot_general(qn.astype(BF16), kn.astype(BF16), (((1,), (1,)), ((), ())), preferred_element_type=F32)
        p = _softmax_rows(sc * (1.0 / math.sqrt(hd)))
        o_ref[...] = jnp.dot(p.astype(BF16), v_ref[...].astype(BF16), preferred_element_type=F32).astype(BF16)

    fix = lambda i, h: (0, 0)
    return pl.pallas_call(
        body, grid=(s // tm, MEM_HEADS),
        in_specs=[pl.BlockSpec((tm, hd), lambda i, h: (i, h)), pl.BlockSpec((MEM_LEN, hd), lambda i, h: (0, h)),
                  pl.BlockSpec((MEM_LEN, hd), lambda i, h: (0, MEM_HEADS + h)), pl.BlockSpec((1, hd), fix), pl.BlockSpec((1, hd), fix)],
        out_specs=pl.BlockSpec((tm, hd), lambda i, h: (i, h)),
        out_shape=SDS((s, MEM_HEADS * hd), BF16), compiler_params=_cp(2), name="memattn_fwd")(qm, kvm, kvm, gq, gk)


def _memattn_bwd(qm, kvm, d_o, gq, gk, tm):
    s = qm.shape[0]
    hd = MEM_HEAD_DIM

    def body(q_ref, k_ref, v_ref, do_ref, gq_ref, gk_ref, dq_ref, dk_ref, dv_ref, dgq_ref, dgk_ref, dkn_sc):
        h = pl.program_id(0)
        i = pl.program_id(1)
        last = pl.num_programs(1) - 1

        @pl.when((h == 0) & (i == 0))
        def _():
            dgq_ref[...] = jnp.zeros_like(dgq_ref)
            dgk_ref[...] = jnp.zeros_like(dgk_ref)

        @pl.when(i == 0)
        def _():
            dv_ref[...] = jnp.zeros_like(dv_ref)
            dkn_sc[...] = jnp.zeros_like(dkn_sc)

        rq, qh, qn = _head_norm(q_ref[...], gq_ref[...])
        rk, kh, kn = _head_norm(k_ref[...], gk_ref[...])
        qnb = qn.astype(BF16)
        knb = kn.astype(BF16)
        scale = 1.0 / math.sqrt(hd)
        sc = lax.dot_general(qnb, knb, (((1,), (1,)), ((), ())), preferred_element_type=F32)
        p = _softmax_rows(sc * scale)
        do = do_ref[...].astype(BF16)
        dp = lax.dot_general(do, v_ref[...].astype(BF16), (((1,), (1,)), ((), ())), preferred_element_type=F32)
        dv_ref[...] += lax.dot_general(p.astype(BF16), do, (((0,), (0,)), ((), ())), preferred_element_type=F32)
        ds = ((p * (dp - jnp.sum(dp * p, axis=-1, keepdims=True))) * scale).astype(BF16)
        dqn = jnp.dot(ds, knb, preferred_element_type=F32)
        dkn_sc[...] += lax.dot_general(ds, qnb, (((0,), (0,)), ((), ())), preferred_element_type=F32)
        dgq_ref[...] += jnp.sum(dqn * qh, axis=0, keepdims=True)
        dqh = dqn * gq_ref[...]
        dq_ref[...] = (rq * (dqh - qh * jnp.mean(dqh * qh, axis=-1, keepdims=True))).astype(BF16)

        @pl.when(i == last)
        def _():
            dkn = dkn_sc[...]
            dgk_ref[...] += jnp.sum(dkn * kh, axis=0, keepdims=True)
            dkh = dkn * gk_ref[...]
            dk_ref[...] = rk * (dkh - kh * jnp.mean(dkh * kh, axis=-1, keepdims=True))

    fix = lambda h, i: (0, 0)
    qb = pl.BlockSpec((tm, hd), lambda h, i: (i, h))
    kb = pl.BlockSpec((MEM_LEN, hd), lambda h, i: (0, h))
    return pl.pallas_call(
        body, grid=(MEM_HEADS, s // tm),
        in_specs=[qb, kb, pl.BlockSpec((MEM_LEN, hd), lambda h, i: (0, MEM_HEADS + h)), qb,
                  pl.BlockSpec((1, hd), fix), pl.BlockSpec((1, hd), fix)],
        out_specs=[qb, kb, kb, pl.BlockSpec((1, hd), fix), pl.BlockSpec((1, hd), fix)],
        out_shape=[SDS((s, MEM_HEADS * hd), BF16), SDS((MEM_LEN, MEM_HEADS * hd), F32), SDS((MEM_LEN, MEM_HEADS * hd), F32),
                   SDS((1, hd), F32), SDS((1, hd), F32)],
        scratch_shapes=[pltpu.VMEM((MEM_LEN, hd), F32)],
        compiler_params=_cp(2), name="memattn_bwd")(qm, kvm, kvm, d_o, gq, gk)


def _ffn_specs(tm, tn, nbj, s, order_ji):
    if order_ji:
        ij = lambda f: (lambda j, i: f(i, j))
    else:
        ij = lambda f: f
    prev = lambda i: jnp.maximum(i * (tm // FFN_HALO) - 1, 0)
    cur_g = pl.BlockSpec((tm, tn), ij(lambda i, j: (i, j)))
    cur_v = pl.BlockSpec((tm, tn), ij(lambda i, j: (i, j + nbj)))
    halo_g = pl.BlockSpec((FFN_HALO, tn), ij(lambda i, j: (prev(i), j)))
    halo_v = pl.BlockSpec((FFN_HALO, tn), ij(lambda i, j: (prev(i), j + nbj)))
    w_g = pl.BlockSpec((8, tn), ij(lambda i, j: (0, j)))
    w_v = pl.BlockSpec((8, tn), ij(lambda i, j: (0, j + nbj)))
    b_g = pl.BlockSpec((1, tn), ij(lambda i, j: (0, j)))
    b_v = pl.BlockSpec((1, tn), ij(lambda i, j: (0, j + nbj)))
    return cur_g, cur_v, halo_g, halo_v, w_g, w_v, b_g, b_v


FFN_STRIP = 16


def _conv3_rows(ext, w_ref, b_ref, o, n):
    return (w_ref[0:1, :] * ext[FFN_HALO - 2 + o:FFN_HALO - 2 + o + n, :] + w_ref[1:2, :] * ext[FFN_HALO - 1 + o:FFN_HALO - 1 + o + n, :]
            + w_ref[2:3, :] * ext[FFN_HALO + o:FFN_HALO + o + n, :] + b_ref[...])


def _ffn_fwd(up0, w8, b, tm, tn):
    s = up0.shape[0]
    nbj = D_FF // tn

    def body(g_ref, v_ref, gh_ref, vh_ref, wg_ref, wv_ref, bg_ref, bv_ref, act_ref, extg, extv):
        first = pl.program_id(0) == 0
        for ext, h_ref, c_ref in ((extg, gh_ref, g_ref), (extv, vh_ref, v_ref)):
            ext[0:FFN_HALO, :] = jnp.where(first, 0.0, h_ref[...])
            ext[FFN_HALO:, :] = c_ref[...]
        for r in range(tm // FFN_STRIP):
            o = r * FFN_STRIP
            ug = _conv3_rows(extg, wg_ref, bg_ref, o, FFN_STRIP)
            uv = _conv3_rows(extv, wv_ref, bv_ref, o, FFN_STRIP)
            act_ref[o:o + FFN_STRIP, :] = ((ug * jax.nn.sigmoid(ug)) * uv).astype(BF16)

    specs = _ffn_specs(tm, tn, nbj, s, False)
    return pl.pallas_call(
        body, grid=(s // tm, nbj), in_specs=list(specs),
        out_specs=pl.BlockSpec((tm, tn), lambda i, j: (i, j)), out_shape=SDS((s, D_FF), BF16),
        scratch_shapes=[pltpu.VMEM((tm + FFN_HALO, tn), F32), pltpu.VMEM((tm + FFN_HALO, tn), F32)],
        compiler_params=_cp(2), name="ffn_fwd")(up0, up0, up0, up0, w8, w8, b, b)


def _ffn_bwd(d_act, up0, w8, b, tm, tn):
    s = up0.shape[0]
    nbj = D_FF // tn
    te = tm + FFN_HALO

    def body(da_ref, dan_ref, g_ref, v_ref, gh_ref, vh_ref, gn_ref, vn_ref, wg_ref, wv_ref, bg_ref, bv_ref,
             og_ref, ov_ref, dbg_ref, dbv_ref, dwg_ref, dwv_ref, extg, extv, extdg, extdv, accg, accv):
        i = pl.program_id(1)
        first = i == 0
        last = i == pl.num_programs(1) - 1

        @pl.when(first)
        def _():
            for r in (dbg_ref, dbv_ref, dwg_ref, dwv_ref):
                r[...] = jnp.zeros_like(r)

        for ext, h_ref, c_ref, n_ref in ((extg, gh_ref, g_ref, gn_ref), (extv, vh_ref, v_ref, vn_ref)):
            ext[0:FFN_HALO, :] = jnp.where(first, 0.0, h_ref[...])
            ext[FFN_HALO:FFN_HALO + tm, :] = c_ref[...]
            ext[FFN_HALO + tm:, :] = n_ref[...]

        def fold8(x):
            acc = x[0:8, :]
            for q in range(1, x.shape[0] // 8):
                acc = acc + x[q * 8:(q + 1) * 8, :]
            return acc

        def taps(ext, o, n):
            return [ext[FFN_HALO - 2 + k + o:FFN_HALO - 2 + k + o + n, :] for k in range(3)]

        accg[...] = jnp.zeros_like(accg)
        accv[...] = jnp.zeros_like(accv)

        def gate_bwd(o, n, da, own_rows):
            xg, xv = taps(extg, o, n), taps(extv, o, n)
            ug = wg_ref[0:1, :] * xg[0] + wg_ref[1:2, :] * xg[1] + wg_ref[2:3, :] * xg[2] + bg_ref[...]
            uv = wv_ref[0:1, :] * xv[0] + wv_ref[1:2, :] * xv[1] + wv_ref[2:3, :] * xv[2] + bv_ref[...]
            sg = jax.nn.sigmoid(ug)
            dgt = da * uv * (sg * (1.0 + ug * (1.0 - sg)))
            dvl = da * (ug * sg)
            extdg[o:o + n, :] = dgt
            extdv[o:o + n, :] = dvl
            if own_rows:
                for acc, d, x in ((accg, dgt, xg), (accv, dvl, xv)):
                    acc[0] += fold8(d)
                    for k in range(3):
                        acc[1 + k] += fold8(d * x[k])

        for r in range(tm // FFN_STRIP):
            gate_bwd(r * FFN_STRIP, FFN_STRIP, da_ref[r * FFN_STRIP:(r + 1) * FFN_STRIP, :], True)
        gate_bwd(tm, FFN_HALO, jnp.where(last, 0.0, dan_ref[...]), False)

        for extd, w_ref, o_ref, db_ref, dw_ref, acc in ((extdg, wg_ref, og_ref, dbg_ref, dwg_ref, accg),
                                                        (extdv, wv_ref, ov_ref, dbv_ref, dwv_ref, accv)):
            for r in range(tm // FFN_STRIP):
                o = r * FFN_STRIP
                o_ref[o:o + FFN_STRIP, :] = (w_ref[2:3, :] * extd[o:o + FFN_STRIP, :] + w_ref[1:2, :] * extd[o + 1:o + 1 + FFN_STRIP, :]
                                             + w_ref[0:1, :] * extd[o + 2:o + 2 + FFN_STRIP, :]).astype(BF16)
            db_ref[...] += jnp.sum(acc[0], axis=0, keepdims=True)
            for k in range(3):
                dw_ref[k:k + 1, :] += jnp.sum(acc[1 + k], axis=0, keepdims=True)

    cur_g, cur_v, halo_g, halo_v, w_g, w_v, b_g, b_v = _ffn_specs(tm, tn, nbj, s, True)
    nxt_row = lambda i: jnp.minimum((i + 1) * (tm // FFN_HALO), s // FFN_HALO - 1)
    cur = pl.BlockSpec((tm, tn), lambda j, i: (i, j))
    nxt = pl.BlockSpec((FFN_HALO, tn), lambda j, i: (nxt_row(i), j))
    nxt_v = pl.BlockSpec((FFN_HALO, tn), lambda j, i: (nxt_row(i), j + nbj))
    acc1 = pl.BlockSpec((1, tn), lambda j, i: (0, j))
    acc8 = pl.BlockSpec((8, tn), lambda j, i: (0, j))
    return pl.pallas_call(
        body, grid=(nbj, s // tm), in_specs=[cur, nxt, cur_g, cur_v, halo_g, halo_v, nxt, nxt_v, w_g, w_v, b_g, b_v],
        out_specs=[cur, cur, acc1, acc1, acc8, acc8],
        out_shape=[SDS((s, D_FF), BF16), SDS((s, D_FF), BF16), SDS((1, D_FF), F32), SDS((1, D_FF), F32),
                   SDS((8, D_FF), F32), SDS((8, D_FF), F32)],
        scratch_shapes=[pltpu.VMEM((tm + 2 * FFN_HALO, tn), F32), pltpu.VMEM((tm + 2 * FFN_HALO, tn), F32),
                        pltpu.VMEM((te, tn), F32), pltpu.VMEM((te, tn), F32),
                        pltpu.VMEM((4, 8, tn), F32), pltpu.VMEM((4, 8, tn), F32)],
        compiler_params=_cp(2), name="ffn_bwd")(d_act, d_act, up0, up0, up0, up0, up0, up0, w8, w8, b, b)


def _down_loss(act, w_down, x2, target, tm):
    s = act.shape[0]

    def body(a_ref, w_ref, x_ref, t_ref, dyf_ref, dyb_ref, ls_ref):
        @pl.when(pl.program_id(0) == 0)
        def _():
            ls_ref[...] = jnp.zeros_like(ls_ref)

        y = x_ref[...] + jnp.dot(a_ref[...], w_ref[...], preferred_element_type=F32)
        e = y - t_ref[...]
        ls_ref[...] += jnp.sum(e * e)
        dy = e * (1.0 / D_MODEL)
        dyf_ref[...] = dy
        dyb_ref[...] = dy.astype(BF16)

    row = lambda i: (i, 0)
    return pl.pallas_call(
        body, grid=(s // tm,),
        in_specs=[pl.BlockSpec((tm, D_FF), row), pl.BlockSpec((D_FF, D_MODEL), lambda i: (0, 0)),
                  pl.BlockSpec((tm, D_MODEL), row), pl.BlockSpec((tm, D_MODEL), row)],
        out_specs=[pl.BlockSpec((tm, D_MODEL), row), pl.BlockSpec((tm, D_MODEL), row), pl.BlockSpec((8, 128), lambda i: (0, 0))],
        out_shape=[SDS((s, D_MODEL), F32), SDS((s, D_MODEL), BF16), SDS((8, 128), F32)],
        compiler_params=_cp(1), name="down_loss")(act, w_down, x2, target)


def _adamw_math(w, g, m, v):
    mn = ADAM_B1 * m + (1.0 - ADAM_B1) * g
    vn = ADAM_B2 * v + (1.0 - ADAM_B2) * (g * g)
    m_hat = mn / (1.0 - ADAM_B1 ** ADAM_STEP)
    v_hat = vn / (1.0 - ADAM_B2 ** ADAM_STEP)
    return -ADAM_LR * (m_hat / (jnp.sqrt(v_hat) + ADAM_EPS) + ADAM_WD * w), mn, vn


def _adamw(w, g, m, v, name):
    rows, cols = w.shape
    tr = rows if rows <= 256 else (256 if rows % 256 == 0 else rows // 2)

    def body(w_ref, g_ref, m_ref, v_ref, d_ref, mo_ref, vo_ref):
        d_ref[...], mo_ref[...], vo_ref[...] = _adamw_math(w_ref[...], g_ref[...], m_ref[...], v_ref[...])

    blk = pl.BlockSpec((tr, cols), lambda i: (i, 0))
    return pl.pallas_call(body, grid=(rows // tr,), in_specs=[blk] * 4, out_specs=[blk] * 3,
                          out_shape=[SDS((rows, cols), F32)] * 3, compiler_params=_cp(1), name=name)(w, g, m, v)


def _adamw_small(ws, gs, ms, vs):
    n = len(ws)

    def body(*refs):
        ins, outs = refs[:4 * n], refs[4 * n:]
        for k in range(n):
            d, mn, vn = _adamw_math(ins[k][...], ins[n + k][...], ins[2 * n + k][...], ins[3 * n + k][...])
            outs[k][...] = d
            outs[n + k][...] = mn
            outs[2 * n + k][...] = vn

    vm = pl.BlockSpec(memory_space=pltpu.VMEM)
    outs = pl.pallas_call(body, in_specs=[vm] * (4 * n), out_specs=[vm] * (3 * n),
                          out_shape=[SDS(w.shape, F32) for w in ws] * 3, name="adamw_small")(*ws, *gs, *ms, *vs)
    return outs[:n], outs[n:2 * n], outs[2 * n:]


ANY = pl.BlockSpec(memory_space=pl.ANY)


def _coords():
    return lax.axis_index("x"), lax.axis_index("y"), lax.axis_index("c")


def _other_chips(x, y):
    return [(1 - x, y), (x, 1 - y), (1 - x, 1 - y)]


D2D_CHUNKS = 8
ICI_CHUNKS = 4


def _row_chunks(n_rows, n_chunks, align):
    step = -(-n_rows // (n_chunks * align)) * align
    return [(r, min(step, n_rows - r)) for r in range(0, n_rows, step)]


def _ag_copy(out_ref, send_sems, recv_sems, k, shard, base, r0, nr, to, src=None):
    rows_ = pl.ds(pl.multiple_of(base + r0, 16), nr)
    dst = out_ref.at[shard, rows_]
    return pltpu.make_async_remote_copy(src_ref=dst if src is None else src.at[rows_], dst_ref=dst, send_sem=send_sems.at[k],
                                        recv_sem=recv_sems.at[k], device_id=to, device_id_type=MESH)


def _ag_send(w_ref, out_ref, send_sems, recv_sems):
    x, y, c = _coords()
    half_rows = w_ref.shape[0] // 2
    for k, (px, py) in enumerate(_other_chips(x, y)):
        for r0, nr in _row_chunks(half_rows, ICI_CHUNKS, 16):
            _ag_copy(out_ref, send_sems, recv_sems, k, 2 * x + y, c * half_rows, r0, nr, (px, py, c), src=w_ref).start()


def _ag_finish(w_ref, out_ref, send_sems, recv_sems):
    x, y, c = _coords()
    half_rows = w_ref.shape[0] // 2
    chips = _other_chips(x, y)
    sibling = (x, y, 1 - c)
    for k, (px, py) in enumerate(chips):
        _ag_copy(out_ref, send_sems, recv_sems, k, 2 * px + py, c * half_rows, 0, half_rows, (px, py, c)).wait_recv()
        for r0, nr in _row_chunks(half_rows, ICI_CHUNKS, 16):
            _ag_copy(out_ref, send_sems, recv_sems, 3 + k, 2 * px + py, c * half_rows, r0, nr, sibling).start()
    for k, (px, py) in enumerate(chips):
        _ag_copy(out_ref, send_sems, recv_sems, 3 + k, 2 * px + py, (1 - c) * half_rows, 0, half_rows, sibling).wait_recv()
    for k in range(6):
        _ag_copy(out_ref, send_sems, recv_sems, k, 2 * x + y, c * half_rows, 0, half_rows, sibling).wait_send()


def _ag_weights(wsh):
    rows, cols = wsh.shape

    def body(w_ref, out_ref, send_sems, recv_sems):
        _ag_send(w_ref, out_ref, send_sems, recv_sems)
        _ag_finish(w_ref, out_ref, send_sems, recv_sems)

    return pl.pallas_call(
        body, in_specs=[ANY], out_specs=ANY, out_shape=SDS((4, rows, cols), wsh.dtype),
        scratch_shapes=[pltpu.SemaphoreType.DMA((6,)), pltpu.SemaphoreType.DMA((6,))],
        name="ag_weights")(wsh)


def _rs_swap_halves(gfull, tag):
    n_sh, rows, cols = gfull.shape
    half_rows = rows // 2

    def body(g_ref, recv_ref, send_sem, recv_sem):
        x, y, c = _coords()
        sib_base = (1 - c) * half_rows
        for sh in range(n_sh):
            for r0, nr in _row_chunks(half_rows, D2D_CHUNKS, 8):
                pltpu.make_async_remote_copy(
                    src_ref=g_ref.at[sh, pl.ds(pl.multiple_of(sib_base + r0, 8), nr)], dst_ref=recv_ref.at[sh, pl.ds(r0, nr)],
                    send_sem=send_sem, recv_sem=recv_sem, device_id=(x, y, 1 - c), device_id_type=MESH).start()
        pltpu.make_async_remote_copy(src_ref=recv_ref, dst_ref=recv_ref, send_sem=send_sem, recv_sem=recv_sem,
                                     device_id=(x, y, 1 - c), device_id_type=MESH).wait()

    return pl.pallas_call(
        body, in_specs=[ANY], out_specs=ANY, out_shape=SDS((n_sh, half_rows, cols), gfull.dtype),
        scratch_shapes=[pltpu.SemaphoreType.DMA, pltpu.SemaphoreType.DMA], name="rs_swap_halves" + tag)(gfull)


def _rs_add_pair(gfull, recv, core, tr, tag):
    n_sh, rows, cols = gfull.shape
    half_rows = rows // 2
    nblk = half_rows // tr

    def body(c_ref, g_ref, r_ref, o_ref, ob_ref):
        acc = g_ref[...] + r_ref[...]
        o_ref[...] = acc
        ob_ref[...] = acc.astype(BF16)

    out = pl.BlockSpec((None, tr, cols), lambda sh, i, c_ref: (sh, i, 0))
    gs = pltpu.PrefetchScalarGridSpec(
        num_scalar_prefetch=1, grid=(n_sh, nblk),
        in_specs=[pl.BlockSpec((None, tr, cols), lambda sh, i, c_ref: (sh, c_ref[0] * nblk + i, 0)), out],
        out_specs=[out, out])
    return pl.pallas_call(body, grid_spec=gs, out_shape=[SDS((n_sh, half_rows, cols), F32), SDS((n_sh, half_rows, cols), BF16)],
                          compiler_params=_cp(2), name="rs_add_pair" + tag)(core, gfull, recv)


def _rs_send(cs_ref, recv_ref, send_sems, recv_sems):
    x, y, c = _coords()
    half_rows = cs_ref.shape[1]
    for k, (px, py) in enumerate(_other_chips(x, y)):
        for r0, nr in _row_chunks(half_rows, ICI_CHUNKS, 16):
            pltpu.make_async_remote_copy(
                src_ref=cs_ref.at[2 * px + py, pl.ds(r0, nr)], dst_ref=recv_ref.at[k, pl.ds(r0, nr)],
                send_sem=send_sems.at[k], recv_sem=recv_sems.at[k], device_id=(px, py, c), device_id_type=MESH).start()


def _rs_wait(recv_ref, send_sems, recv_sems):
    x, y, c = _coords()
    for k, (px, py) in enumerate(_other_chips(x, y)):
        pltpu.make_async_remote_copy(src_ref=recv_ref.at[k], dst_ref=recv_ref.at[k], send_sem=send_sems.at[k],
                                     recv_sem=recv_sems.at[k], device_id=(px, py, c), device_id_type=MESH).wait()


def _rs_to_owner(chipsum):
    n_sh, half_rows, cols = chipsum.shape

    def body(cs_ref, recv_ref, send_sems, recv_sems):
        _rs_send(cs_ref, recv_ref, send_sems, recv_sems)
        _rs_wait(recv_ref, send_sems, recv_sems)

    return pl.pallas_call(
        body, in_specs=[ANY], out_specs=ANY, out_shape=SDS((3, half_rows, cols), chipsum.dtype),
        scratch_shapes=[pltpu.SemaphoreType.DMA((3,)), pltpu.SemaphoreType.DMA((3,))], name="rs_to_owner")(chipsum)


def _rs_add_chips(chipsum, recv, shard_core, tr, tag):
    _, half_rows, cols = chipsum.shape

    def body(s_ref, m_ref, r0_ref, r1_ref, r2_ref, o_ref):
        o_ref[...] = ((m_ref[...] + r0_ref[...].astype(F32)) + r1_ref[...].astype(F32)) + r2_ref[...].astype(F32)

    gs = pltpu.PrefetchScalarGridSpec(
        num_scalar_prefetch=1, grid=(half_rows // tr,),
        in_specs=[pl.BlockSpec((None, tr, cols), lambda i, s_ref: (s_ref[0], i, 0))]
        + [pl.BlockSpec((None, tr, cols), (lambda k: lambda i, s_ref: (k, i, 0))(k)) for k in range(3)],
        out_specs=pl.BlockSpec((None, tr, cols), lambda i, s_ref: (s_ref[1], i, 0)))
    return pl.pallas_call(body, grid_spec=gs, out_shape=SDS((2, half_rows, cols), F32),
                          compiler_params=_cp(1), name="rs_add_chips" + tag)(shard_core, chipsum, recv, recv, recv)


def _rs_join_halves(buf, tag):
    _, half_rows, cols = buf.shape

    def body(b_ref, out_ref, send_sem, recv_sem):
        x, y, c = _coords()
        for r0, nr in _row_chunks(half_rows, D2D_CHUNKS, 8):
            pltpu.make_async_remote_copy(src_ref=out_ref.at[c, pl.ds(r0, nr)], dst_ref=out_ref.at[c, pl.ds(r0, nr)], send_sem=send_sem,
                                         recv_sem=recv_sem, device_id=(x, y, 1 - c), device_id_type=MESH).start()
        pltpu.make_async_remote_copy(src_ref=out_ref.at[c], dst_ref=out_ref.at[c], send_sem=send_sem, recv_sem=recv_sem,
                                     device_id=(x, y, 1 - c), device_id_type=MESH).wait()

    return pl.pallas_call(
        body, in_specs=[ANY], out_specs=ANY, out_shape=SDS(buf.shape, buf.dtype), input_output_aliases={0: 0},
        scratch_shapes=[pltpu.SemaphoreType.DMA, pltpu.SemaphoreType.DMA], name="rs_join_halves" + tag)(buf)


BIG = [("w_in", (1024, 1440), 1), ("w_uq", (256, 768), 1), ("w_ukv", (128, 1024), 1), ("w_out", (1024, 1024), 0),
       ("w_mem_q", (1024, 1024), 0), ("w_mem_kv", (1024, 2048), 1), ("w_mem_o", (1024, 1024), 0),
       ("w_up", (1024, 5632), 1), ("w_down", (2816, 1024), 0)]
SMALL_REP = [("mix_norm_g", 1024), ("b_conv_in", 1024), ("b_conv_dw", 512), ("conv_ln_g", 512), ("conv_ln_b", 512),
             ("q_lat_norm_g", 256), ("kv_lat_norm_g", 128), ("q_norm_g", 96), ("k_norm_g", 96), ("mem_norm_x_g", 1024),
             ("mem_norm_m_g", 1024), ("mem_q_norm_g", 256), ("mem_k_norm_g", 256), ("ffn_norm_g", 1024), ("b_ffn_dw", 5632)]
SMALL_SH = [("w_conv_dw", (31, 512)), ("w_ffn_dw", (3, 5632))]


def _shard_shape(shape, axis):
    return tuple(d // 4 if a == axis else d for a, d in enumerate(shape))


def _pack_rows(parts, rows, cols):
    flat = jnp.concatenate([p.reshape(-1) for p in parts])
    flat = jnp.pad(flat, (0, rows * cols - flat.shape[0]))
    return flat.reshape(rows, cols)


AG_EARLY, AG_LATE = BIG[:3], BIG[3:]
RS_REST, RS_FFN = AG_EARLY, AG_LATE


def _group_rows(group):
    used = sum(_shard_shape(shape, axis)[0] * _shard_shape(shape, axis)[1] // PACK_COLS for _, shape, axis in group)
    return -(-used // 512) * 512


def _pick_rows(n, cap=384):
    return max(r for r in range(16, cap + 1, 16) if n % r == 0)


def _pack_big_shards(ws, group):
    parts = [ws[n].reshape(-1, PACK_COLS) for n, _, _ in group]
    used = sum(p.shape[0] for p in parts)
    pad = _group_rows(group) - used
    return jnp.concatenate(parts + ([jnp.zeros((pad, PACK_COLS), parts[0].dtype)] if pad else []), axis=0)


def _unpack_big_shards(packed, group):
    out, r = {}, 0
    for n, shape, axis in group:
        sh = _shard_shape(shape, axis)
        nr = sh[0] * sh[1] // PACK_COLS
        out[n] = packed[r:r + nr].reshape(sh)
        r += nr
    return out


def _unpack_gathered(g, group):
    out, r = {}, 0
    for n, shape, axis in group:
        sh = _shard_shape(shape, axis)
        nr = sh[0] * sh[1] // PACK_COLS
        part = g[:, r:r + nr]
        if axis == 0:
            out[n] = part.reshape(shape)
        else:
            out[n] = part.reshape((4,) + sh).transpose(1, 0, 2).reshape(shape)
        r += nr
    return out


def _pack_full_grads(gs, group):
    parts = []
    for n, shape, axis in group:
        sh = _shard_shape(shape, axis)
        nr = sh[0] * sh[1] // PACK_COLS
        if axis == 0:
            parts.append(gs[n].reshape(4, nr, PACK_COLS))
        else:
            parts.append(gs[n].reshape(shape[0], 4, sh[1]).transpose(1, 0, 2).reshape(4, nr, PACK_COLS))
    pad = _group_rows(group) - sum(p.shape[1] for p in parts)
    return jnp.concatenate(parts + ([jnp.zeros((4, pad, PACK_COLS), F32)] if pad else []), axis=1)


def _rs_first(gfull, core_idx, tag):
    tr = _pick_rows(gfull.shape[1] // 2)
    return _rs_add_pair(gfull, _rs_swap_halves(gfull, tag), core_idx.reshape(1), tr, tag)


def _rs_last(chipsum, recv, shard_idx, core_idx, tag):
    tr = _pick_rows(chipsum.shape[1])
    red = _rs_add_chips(chipsum, recv, jnp.stack([shard_idx, core_idx]), tr, tag)
    return _rs_join_halves(red, tag).reshape(2 * chipsum.shape[1], chipsum.shape[2])


def _rope_tables(positions):
    inv_freq = ROPE_THETA ** (-jnp.arange(0, ROPE, 2, dtype=F32) / ROPE)
    ang = positions.astype(F32)[:, None] * inv_freq
    cos, sin = jnp.cos(ang), jnp.sin(ang)
    s = positions.shape[0]
    cosf = jnp.concatenate([jnp.ones((s, NOPE), F32), cos, cos, jnp.ones((s, HEAD_PAD - HEAD_DIM), F32)], axis=-1)
    sinf = jnp.concatenate([jnp.zeros((s, NOPE), F32), -sin, sin, jnp.zeros((s, HEAD_PAD - HEAD_DIM), F32)], axis=-1)
    return cosf, sinf


def _pad_heads(w, per_head):
    k = w.shape[0]
    w3 = w.reshape(k, HEADS, per_head)
    return jnp.pad(w3, ((0, 0), (0, 0), (0, HEAD_PAD - per_head))).reshape(k, HEADS * HEAD_PAD)


def _layer_grads(x, mem, positions, target, wf, w_late, sp, shard_idx, core_idx):
    wf = dict(wf)
    s = x.shape[0]
    tm = _row_tile(s, 512)
    tc = _row_tile(s, 256)
    tb = 512 if s % (512 * ATT_GROUP) == 0 else 128
    row2 = lambda a: a.reshape(1, -1)

    w_in = wf["w_in"]
    w_in_pad = jnp.concatenate([w_in[:, :1408], jnp.zeros((D_MODEL, NOPE), BF16), w_in[:, 1408:],
                                jnp.zeros((D_MODEL, HEAD_PAD - HEAD_DIM), BF16)], axis=1)
    w_uq_pad = _pad_heads(wf["w_uq"], HEAD_DIM)
    w_ukv = wf["w_ukv"]
    gq_pad = jnp.pad(sp["q_norm_g"], (0, HEAD_PAD - HEAD_DIM)).reshape(1, HEAD_PAD)
    gk_pad = jnp.pad(sp["k_norm_g"], (0, HEAD_PAD - HEAD_DIM)).reshape(1, HEAD_PAD)
    w_dw32 = jnp.pad(sp["w_conv_dw"], ((0, 1), (0, 0)))
    w_ffn8 = jnp.pad(sp["w_ffn_dw"], ((0, 5), (0, 0)))
    b_ffn = row2(sp["b_ffn_dw"])
    cosf, sinf = _rope_tables(positions)

    z, h1 = _norm_linear(x, 0, D_MODEL, row2(sp["mix_norm_g"]), w_in_pad, F32, tm, IN_COLS_PAD, "in_proj")
    u, u0, u1 = _conv_fwd(z, row2(sp["b_conv_in"]), w_dw32, row2(sp["b_conv_dw"]), row2(sp["conv_ln_g"]), row2(sp["conv_ln_b"]), tc)
    q_raw, cqn = _norm_linear(z, 1024 // Q_RANK, Q_RANK, row2(sp["q_lat_norm_g"]), w_uq_pad, F32, tm, 1024, "q_up")
    kv_raw, ckvn = _norm_linear(z, 1280 // KV_RANK, KV_RANK, row2(sp["kv_lat_norm_g"]), w_ukv, F32, tm, 1024, "kv_up")
    qp, kp, vp = _mla_prep(q_raw, kv_raw, z, cosf, sinf, gq_pad, gk_pad, tc)
    o_f, o_b, lse, gathered = _attn_fwd(qp, kp, vp, tb, w_late)
    wf.update(_unpack_gathered(lax.dynamic_update_index_in_dim(gathered, w_late, shard_idx, 0), AG_LATE))
    w_out_u = wf["w_out"][:CONV_CH]
    w_out_o = jnp.pad(wf["w_out"][CONV_CH:].reshape(HEADS, NOPE, D_MODEL), ((0, 0), (NOPE, 0), (0, 0))).reshape(HEADS * HEAD_PAD, D_MODEL)
    w_up_g, w_up_v = wf["w_up"][:, :D_FF], wf["w_up"][:, D_FF:]
    (x1,) = _linear([(u, w_out_u), (o_b, w_out_o)], False, x, [F32], tm, 1024, "out_proj")

    qm, hq = _norm_linear(x1, 0, D_MODEL, row2(sp["mem_norm_x_g"]), wf["w_mem_q"], F32, tm, 1024, "memq_proj")
    kvm, hm = _norm_linear(mem, 0, D_MODEL, row2(sp["mem_norm_m_g"]), wf["w_mem_kv"], F32, MEM_LEN, 1024, "memkv_proj")
    gmq, gmk = row2(sp["mem_q_norm_g"]), row2(sp["mem_k_norm_g"])
    o_m = _memattn_fwd(qm, kvm, gmq, gmk, tm)
    (x2,) = _linear([(o_m, wf["w_mem_o"])], False, x1, [F32], tm, 1024, "memo_proj")

    up0, h3 = _norm_linear(x2, 0, D_MODEL, row2(sp["ffn_norm_g"]), wf["w_up"], F32, _row_tile(s, 1024), D_FF // 2, "up_proj")
    act = _ffn_fwd(up0, w_ffn8, b_ffn, tc, D_FF // 2)
    dy_f, dy_b, lsum = _down_loss(act, wf["w_down"], x2, target, tm)

    g = {}
    (d_act,) = _linear([(dy_b, wf["w_down"])], True, None, [F32], _row_tile(s, 1024), D_FF // 2, "down_bwd")
    g["w_down"] = _dw(act, dy_b, "dw_down")
    d_up0g, d_up0v, dbg, dbv, dwg, dwv = _ffn_bwd(d_act, up0, w_ffn8, b_ffn, tc, D_FF // 2)
    g["b_ffn_dw"] = jnp.concatenate([dbg, dbv], axis=1).reshape(-1)
    g["w_ffn_dw"] = jnp.concatenate([dwg[:3], dwv[:3]], axis=1)
    g["w_up"] = jnp.concatenate([_dw(h3, d_up0g, "dw_up_g"), _dw(h3, d_up0v, "dw_up_v")], axis=1)
    d_x2f, d_x2b, dg = _linear_normbwd([(d_up0g, w_up_g), (d_up0v, w_up_v)], x2, 0, row2(sp["ffn_norm_g"]), dy_f,
                                       [F32, BF16], tc, "up_bwd")
    g["ffn_norm_g"] = dg.reshape(-1)

    (d_om,) = _linear([(d_x2b, wf["w_mem_o"])], True, None, [BF16], tm, 1024, "memo_bwd")
    g["w_mem_o"] = _dw(o_m, d_x2b, "dw_mem_o")
    d_qm, d_km, d_vm, dgq, dgk = _memattn_bwd(qm, kvm, d_om, gmq, gmk, tm)
    g["mem_q_norm_g"], g["mem_k_norm_g"] = dgq.reshape(-1), dgk.reshape(-1)
    d_kvm = jnp.concatenate([d_km, d_vm], axis=1)
    g["w_mem_q"] = _dw(hq, d_qm, "dw_mem_q")
    g["w_mem_kv"] = _dw(hm, d_kvm, "dw_mem_kv")
    d_x1f, d_x1b, dg = _linear_normbwd([(d_qm, wf["w_mem_q"])], x1, 0, row2(sp["mem_norm_x_g"]), d_x2f, [F32, BF16], tm, "memq_bwd")
    g["mem_norm_x_g"] = dg.reshape(-1)
    _, dg = _linear_normbwd([(d_kvm, wf["w_mem_kv"])], mem, 0, row2(sp["mem_norm_m_g"]), None, [BF16], MEM_LEN, "memkv_bwd")
    g["mem_norm_m_g"] = dg.reshape(-1)

    (d_u,) = _linear([(d_x1b, w_out_u)], True, None, [F32], tm, CONV_CH, "out_bwd_u")
    d_of, d_ob = _linear([(d_x1b, w_out_o)], True, None, [F32, BF16], tm, 1024, "out_bwd_o")
    dw_out_u = _dw(u, d_x1b, "dw_out_u")
    dw_out_o = _dw(o_b, d_x1b, "dw_out_o")
    g["w_out"] = jnp.concatenate([dw_out_u, dw_out_o.reshape(HEADS, HEAD_PAD, D_MODEL)[:, NOPE:].reshape(HEADS * NOPE, D_MODEL)], axis=0)
    chipsum_ffn, wire_ffn = _rs_first(_pack_full_grads(g, RS_FFN), core_idx, "_ffn")
    delta = _attn_delta(d_of, o_f, tb)
    dqp, dkp, dvp, recv_ffn = _attn_bwd(qp, kp, vp, d_ob, lse, delta, tb, wire_ffn)
    g_ffn_packed = _rs_last(chipsum_ffn, recv_ffn, shard_idx, core_idx, "_ffn")
    d_qraw, d_kvraw, d_kr, dgq, dgk = _mla_prep_bwd(dqp, dkp, dvp, q_raw, kv_raw, z, cosf, sinf, gq_pad, gk_pad, tc)
    g["q_norm_g"], g["k_norm_g"] = dgq.reshape(-1)[:HEAD_DIM], dgk.reshape(-1)[:HEAD_DIM]
    g["w_uq"] = _dw(cqn, d_qraw, "dw_uq").reshape(Q_RANK, HEADS, HEAD_PAD)[:, :, :HEAD_DIM].reshape(Q_RANK, HEADS * HEAD_DIM)
    g["w_ukv"] = _dw(ckvn, d_kvraw, "dw_ukv")
    d_cq, dg = _linear_normbwd([(d_qraw, w_uq_pad)], z, 1024 // Q_RANK, row2(sp["q_lat_norm_g"]), None, [BF16], tm, "q_up_bwd")
    g["q_lat_norm_g"] = dg.reshape(-1)
    d_ckv, dg = _linear_normbwd([(d_kvraw, w_ukv)], z, 1280 // KV_RANK, row2(sp["kv_lat_norm_g"]), None, [BF16], tm, "kv_up_bwd")
    g["kv_lat_norm_g"] = dg.reshape(-1)
    d_u1, dlg, dlb, dbdw = _conv_bwd_ln(d_u, u1, row2(sp["conv_ln_g"]), row2(sp["conv_ln_b"]), tc)
    g["conv_ln_g"], g["conv_ln_b"], g["b_conv_dw"] = dlg.reshape(-1), dlb.reshape(-1), dbdw.reshape(-1)
    d_conv, dw_dw, dbin = _conv_bwd_dw(d_u1, u0, z, row2(sp["b_conv_in"]), w_dw32, tc)
    g["w_conv_dw"], g["b_conv_in"] = dw_dw[:CONV_WIDTH], dbin.reshape(-1)
    pieces = [(d_conv, w_in_pad[:, :1024]), (d_cq, w_in_pad[:, 1024:1280]), (d_ckv, w_in_pad[:, 1280:1408]), (d_kr, w_in_pad[:, 1408:])]
    dw_in = [_dw(h1, d, "dw_in_%d" % k) for k, (d, _) in enumerate(pieces)]
    g["w_in"] = jnp.concatenate([dw_in[0], dw_in[1], dw_in[2], dw_in[3][:, NOPE:HEAD_DIM]], axis=1)
    grad_x, dg = _linear_normbwd(pieces, x, 0, row2(sp["mix_norm_g"]), d_x1f, [F32], tm, "in_bwd")
    g["mix_norm_g"] = dg.reshape(-1)
    return lsum[0, 0], grad_x, g, g_ffn_packed


def kernel(x, mem, positions, mix_norm_g, w_in, b_conv_in, w_conv_dw, b_conv_dw, conv_ln_g, conv_ln_b, q_lat_norm_g, w_uq, kv_lat_norm_g, w_ukv, q_norm_g, k_norm_g, w_out, mem_norm_x_g, mem_norm_m_g, w_mem_q, w_mem_kv, mem_q_norm_g, mem_k_norm_g, w_mem_o, ffn_norm_g, w_up, w_ffn_dw, b_ffn_dw, w_down, loss_target, m_mix_norm_g, m_w_in, m_b_conv_in, m_w_conv_dw, m_b_conv_dw, m_conv_ln_g, m_conv_ln_b, m_q_lat_norm_g, m_w_uq, m_kv_lat_norm_g, m_w_ukv, m_q_norm_g, m_k_norm_g, m_w_out, m_mem_norm_x_g, m_mem_norm_m_g, m_w_mem_q, m_w_mem_kv, m_mem_q_norm_g, m_mem_k_norm_g, m_w_mem_o, m_ffn_norm_g, m_w_up, m_w_ffn_dw, m_b_ffn_dw, m_w_down, v_mix_norm_g, v_w_in, v_b_conv_in, v_w_conv_dw, v_b_conv_dw, v_conv_ln_g, v_conv_ln_b, v_q_lat_norm_g, v_w_uq, v_kv_lat_norm_g, v_w_ukv, v_q_norm_g, v_k_norm_g, v_w_out, v_mem_norm_x_g, v_mem_norm_m_g, v_w_mem_q, v_w_mem_kv, v_mem_q_norm_g, v_mem_k_norm_g, v_w_mem_o, v_ffn_norm_g, v_w_up, v_w_ffn_dw, v_b_ffn_dw, v_w_down):
    names = ["mix_norm_g", "w_in", "b_conv_in", "w_conv_dw", "b_conv_dw", "conv_ln_g", "conv_ln_b", "q_lat_norm_g", "w_uq",
             "kv_lat_norm_g", "w_ukv", "q_norm_g", "k_norm_g", "w_out", "mem_norm_x_g", "mem_norm_m_g", "w_mem_q", "w_mem_kv",
             "mem_q_norm_g", "mem_k_norm_g", "w_mem_o", "ffn_norm_g", "w_up", "w_ffn_dw", "b_ffn_dw", "w_down"]
    loc = locals()
    w = {n: loc[n] for n in names}
    m = {n: loc["m_" + n] for n in names}
    v = {n: loc["v_" + n] for n in names}
    shard_idx = 2 * lax.axis_index("x") + lax.axis_index("y")

    shard_idx = shard_idx.astype(jnp.int32)
    core_idx = lax.axis_index("c").astype(jnp.int32)

    w_local = {n: w[n][0] for n, _, _ in BIG}
    w_early = _pack_big_shards(w_local, AG_EARLY).astype(BF16)
    w_late = _pack_big_shards(w_local, AG_LATE).astype(BF16)
    wf = _unpack_gathered(lax.dynamic_update_index_in_dim(_ag_weights(w_early), w_early, shard_idx, 0), AG_EARLY)

    small_sh_full = {}
    gather_in = []
    for n, (r, c) in SMALL_SH:
        csh = c // 4
        slab = lax.dynamic_update_slice(jnp.zeros((r, c), F32), w[n][0], (0, shard_idx * csh))
        gather_in.append(slab.reshape(-1))
    gather_rows = 256
    gathered_small = _allreduce_small_named(_pack_rows(gather_in, gather_rows, SMALL_COLS), "gather_small") * 0.5
    off = 0
    for n, (r, c) in SMALL_SH:
        small_sh_full[n] = gathered_small.reshape(-1)[off:off + r * c].reshape(r, c)
        off += r * c
    sp = {n: w[n][0] for n, _ in SMALL_REP}
    sp.update(small_sh_full)

    lsum, grad_x, g, g_ffn_packed = _layer_grads(x[0], mem[0], positions[0], loss_target[0], wf, w_late, sp, shard_idx, core_idx)

    small_parts = [jnp.full((SMALL_COLS,), lsum, F32)] + [g[n] for n, _ in SMALL_REP] + [g[n] for n, _ in SMALL_SH]
    small_rows = 368
    small_sum = _allreduce_small_named(_pack_rows(small_parts, small_rows, SMALL_COLS), "allreduce_small").reshape(-1)
    loss = small_sum[0] * (0.5 / D_MODEL)
    gs = {}
    off = SMALL_COLS
    for n, sz in SMALL_REP:
        gs[n] = small_sum[off:off + sz].reshape(w[n].shape)
        off += sz
    for n, (r, c) in SMALL_SH:
        full = small_sum[off:off + r * c].reshape(r, c)
        gs[n] = lax.dynamic_slice(full, (0, shard_idx * (c // 4)), (r, c // 4)).reshape(w[n].shape)
        off += r * c

    chipsum, chipsum_wire = _rs_first(_pack_full_grads(g, RS_REST), core_idx, "_rest")
    g_rest_packed = _rs_last(chipsum, _rs_to_owner(chipsum_wire), shard_idx, core_idx, "_rest")
    g_big = {**_unpack_big_shards(g_rest_packed, RS_REST), **_unpack_big_shards(g_ffn_packed, RS_FFN)}
    gs.update({n: a[None] for n, a in g_big.items()})

    delta, new_m, new_v = {}, {}, {}
    for n, _, _ in BIG:
        d_n, m_n, v_n = _adamw(w[n][0], g_big[n], m[n][0], v[n][0], "adamw_" + n)
        delta[n], new_m[n], new_v[n] = d_n[None], m_n[None], v_n[None]
    small_names = [n for n, _ in SMALL_REP] + [n for n, _ in SMALL_SH]
    as2d = lambda a: a.reshape(-1, a.shape[-1])
    d_s, m_s, v_s = _adamw_small(*[[as2d(d[n]) for n in small_names] for d in (w, gs, m, v)])
    for k, n in enumerate(small_names):
        delta[n], new_m[n], new_v[n] = d_s[k].reshape(w[n].shape), m_s[k].reshape(w[n].shape), v_s[k].reshape(w[n].shape)

    return (loss, grad_x[None], *[gs[n] for n in names], *[delta[n] for n in names], *[new_m[n] for n in names],
            *[new_v[n] for n in names])


def _allreduce_small_named(v, name):
    rows, cols = v.shape

    def body(v_ref, out_ref, buf, send_sems, recv_sems):
        x, y, c = _coords()
        me = 4 * x + 2 * y + c
        buf[me] = v_ref[...]
        cps = []
        for r in range(1, 8):
            dx, dy, dc = (r >> 2) & 1, (r >> 1) & 1, r & 1
            to = (x + dx - 2 * x * dx, y + dy - 2 * y * dy, c + dc - 2 * c * dc)
            cp = pltpu.make_async_remote_copy(src_ref=v_ref, dst_ref=buf.at[me], send_sem=send_sems.at[r - 1],
                                              recv_sem=recv_sems.at[r - 1], device_id=to, device_id_type=MESH)
            cp.start()
            cps.append(cp)
        for cp in cps:
            cp.wait()
        acc = buf[0]
        for d in range(1, 8):
            acc = acc + buf[d]
        out_ref[...] = acc

    vm = pl.BlockSpec(memory_space=pltpu.VMEM)
    return pl.pallas_call(
        body, in_specs=[vm], out_specs=vm, out_shape=SDS((rows, cols), F32),
        scratch_shapes=[pltpu.VMEM((8, rows, cols), F32), pltpu.SemaphoreType.DMA((7,)), pltpu.SemaphoreType.DMA((7,))],
        name=name)(v)
```

```python
import math

import numpy as np
import jax
import jax.numpy as jnp
from jax import lax
from jax.experimental import pallas as pl
from jax.experimental.pallas import tpu as pltpu

F32 = jnp.float32
BF16 = jnp.bfloat16
SDS = jax.ShapeDtypeStruct
MESH = pl.DeviceIdType.MESH

D_MODEL = 1024
EPS = 1e-6
CONV_CH = 512
CONV_WIDTH = 31
CONV_HALO = 32
HEADS = 8
NOPE = 64
ROPE = 32
HEAD_DIM = NOPE + ROPE
HEAD_PAD = 128
Q_RANK = 256
KV_RANK = 128
CHUNK = 64
ROPE_THETA = 10000.0
IN_COLS_PAD = 1536
MEM_HEADS = 4
MEM_HEAD_DIM = 256
MEM_LEN = 256
D_FF = 2816
FFN_HALO = 8
ATT_SCALE = 1.0 / math.sqrt(HEAD_DIM)
LOG2E = math.log2(math.e)
LN2 = math.log(2.0)

ADAM_LR = 0.001
ADAM_B1 = 0.9
ADAM_B2 = 0.999
ADAM_EPS = 1e-08
ADAM_WD = 0.01
ADAM_STEP = 10

VMEM_LIMIT_V7X = 56 * 1024 * 1024
PACK_COLS = 1024
SMALL_COLS = 128


def _cp(n_axes):
    return pltpu.CompilerParams(dimension_semantics=("arbitrary",) * n_axes, vmem_limit_bytes=VMEM_LIMIT_V7X)


def _row_tile(s, want):
    return want if s % want == 0 else s


def _norm_linear(x, xcol, kdim, g, w, out_dtype, tm, tn, name):
    s = x.shape[0]
    n = w.shape[1]

    def body(x_ref, g_ref, w_ref, y_ref, hn_ref):
        @pl.when(pl.program_id(1) == 0)
        def _():
            xv = x_ref[...]
            r = lax.rsqrt(jnp.mean(xv * xv, axis=-1, keepdims=True) + EPS)
            hn_ref[...] = ((xv * r) * g_ref[...]).astype(BF16)

        y_ref[...] = jnp.dot(hn_ref[...], w_ref[...], preferred_element_type=F32).astype(y_ref.dtype)

    return pl.pallas_call(
        body, grid=(s // tm, n // tn),
        in_specs=[pl.BlockSpec((tm, kdim), lambda i, j: (i, xcol)), pl.BlockSpec((1, kdim), lambda i, j: (0, 0)),
                  pl.BlockSpec((kdim, tn), lambda i, j: (0, j))],
        out_specs=[pl.BlockSpec((tm, tn), lambda i, j: (i, j)), pl.BlockSpec((tm, kdim), lambda i, j: (i, 0))],
        out_shape=[SDS((s, n), out_dtype), SDS((s, kdim), BF16)],
        compiler_params=_cp(2), name=name)(x, g, w)


def _linear(pairs, nt, residual, out_dtypes, tm, tn, name):
    s = pairs[0][0].shape[0]
    n = pairs[0][1].shape[0] if nt else pairs[0][1].shape[1]
    n_pairs = len(pairs)
    has_res = residual is not None

    def body(*refs):
        a_refs = refs[:n_pairs]
        w_refs = refs[n_pairs:2 * n_pairs]
        res_ref = refs[2 * n_pairs] if has_res else None
        outs = refs[2 * n_pairs + int(has_res):]
        acc = None
        for a_ref, w_ref in zip(a_refs, w_refs):
            a = a_ref[...].astype(BF16)
            if nt:
                d = lax.dot_general(a, w_ref[...], (((1,), (1,)), ((), ())), preferred_element_type=F32)
            else:
                d = jnp.dot(a, w_ref[...], preferred_element_type=F32)
            acc = d if acc is None else acc + d
        if has_res:
            acc = res_ref[...] + acc
        for o in outs:
            o[...] = acc.astype(o.dtype)

    in_specs = [pl.BlockSpec((tm, a.shape[1]), lambda i, j: (i, 0)) for a, _ in pairs]
    if nt:
        in_specs += [pl.BlockSpec((tn, w.shape[1]), lambda i, j: (j, 0)) for _, w in pairs]
    else:
        in_specs += [pl.BlockSpec((w.shape[0], tn), lambda i, j: (0, j)) for _, w in pairs]
    args = [a for a, _ in pairs] + [w for _, w in pairs]
    if has_res:
        in_specs.append(pl.BlockSpec((tm, tn), lambda i, j: (i, j)))
        args.append(residual)
    outs = pl.pallas_call(
        body, grid=(s // tm, n // tn), in_specs=in_specs,
        out_specs=[pl.BlockSpec((tm, tn), lambda i, j: (i, j)) for _ in out_dtypes],
        out_shape=[SDS((s, n), dt) for dt in out_dtypes],
        compiler_params=_cp(2), name=name)(*args)
    return outs


def _linear_normbwd(pairs, x, xcol, g, d_res, out_dtypes, tm, name):
    s = pairs[0][0].shape[0]
    dn = pairs[0][1].shape[0]
    n_pairs = len(pairs)
    has_res = d_res is not None

    def body(*refs):
        a_refs = refs[:n_pairs]
        w_refs = refs[n_pairs:2 * n_pairs]
        x_ref, g_ref = refs[2 * n_pairs], refs[2 * n_pairs + 1]
        k = 2 * n_pairs + 2
        res_ref = refs[k] if has_res else None
        k += int(has_res)
        outs = refs[k:-1]
        dg_ref = refs[-1]
        dh = None
        for a_ref, w_ref in zip(a_refs, w_refs):
            d = lax.dot_general(a_ref[...].astype(BF16), w_ref[...], (((1,), (1,)), ((), ())), preferred_element_type=F32)
            dh = d if dh is None else dh + d
        xv = x_ref[...]
        r = lax.rsqrt(jnp.mean(xv * xv, axis=-1, keepdims=True) + EPS)
        y = xv * r

        @pl.when(pl.program_id(0) == 0)
        def _():
            dg_ref[...] = jnp.zeros_like(dg_ref)

        dg_ref[...] += jnp.sum(dh * y, axis=0, keepdims=True)
        dy = dh * g_ref[...]
        dx = r * (dy - y * jnp.mean(dy * y, axis=-1, keepdims=True))
        if has_res:
            dx = res_ref[...] + dx
        for o in outs:
            o[...] = dx.astype(o.dtype)

    in_specs = [pl.BlockSpec((tm, a.shape[1]), lambda i: (i, 0)) for a, _ in pairs]
    in_specs += [pl.BlockSpec((dn, w.shape[1]), lambda i: (0, 0)) for _, w in pairs]
    in_specs += [pl.BlockSpec((tm, dn), lambda i: (i, xcol)), pl.BlockSpec((1, dn), lambda i: (0, 0))]
    args = [a for a, _ in pairs] + [w for _, w in pairs] + [x, g]
    if has_res:
        in_specs.append(pl.BlockSpec((tm, dn), lambda i: (i, 0)))
        args.append(d_res)
    outs = pl.pallas_call(
        body, grid=(s // tm,), in_specs=in_specs,
        out_specs=[pl.BlockSpec((tm, dn), lambda i: (i, 0)) for _ in out_dtypes] + [pl.BlockSpec((1, dn), lambda i: (0, 0))],
        out_shape=[SDS((s, dn), dt) for dt in out_dtypes] + [SDS((1, dn), F32)],
        compiler_params=_cp(1), name=name)(*args)
    return outs


def _dw_matmul(a, b, tk, tn, ts, name):
    s, ka = a.shape
    n = b.shape[1]

    def body(a_ref, b_ref, o_ref):
        @pl.when(pl.program_id(2) == 0)
        def _():
            o_ref[...] = jnp.zeros_like(o_ref)

        o_ref[...] += lax.dot_general(a_ref[...].astype(BF16), b_ref[...].astype(BF16), (((0,), (0,)), ((), ())),
                                      preferred_element_type=F32)

    return pl.pallas_call(
        body, grid=(ka // tk, n // tn, s // ts),
        in_specs=[pl.BlockSpec((ts, tk), lambda k, j, t: (t, k)), pl.BlockSpec((ts, tn), lambda k, j, t: (t, j))],
        out_specs=pl.BlockSpec((tk, tn), lambda k, j, t: (k, j)),
        out_shape=SDS((ka, n), F32), compiler_params=_cp(3), name=name)(a, b)


def _dw(a, b, name):
    s, ka = a.shape
    n = b.shape[1]
    tk = ka if ka <= 1024 else ka // 2
    tn = n if n <= 1024 else (n // 2 if n == D_FF else 512)
    return _dw_matmul(a, b, tk, tn, _row_tile(s, 2048), name)


def _prev_halo(tm, halo):
    return lambda i: (jnp.maximum(i * (tm // halo) - 1, 0), 0)


def _next_halo(tm, halo, s):
    return lambda i: (jnp.minimum((i + 1) * (tm // halo), s // halo - 1), 0)


def _shifted_copies(ext, tm):
    n = tm + CONV_HALO - 8
    for s in range(1, 8):
        ext[s, 0:n, :] = ext[0, s:s + n, :]


def _sum_taps(terms, ways=4):
    accs = []
    for i, t in enumerate(terms):
        if i < ways:
            accs.append(t)
        else:
            accs[i % ways] = accs[i % ways] + t
    while len(accs) > 1:
        accs = [accs[i] + accs[i + 1] if i + 1 < len(accs) else accs[i] for i in range(0, len(accs), 2)]
    return accs[0]


def _tap_rows(ext, o, n, cs):
    return ext[o % 8, o - o % 8:o - o % 8 + n, cs]


def _conv_fwd(z, b_in, w32, b_dw, ln_g, ln_b, tm):
    s = z.shape[0]
    c = CONV_CH

    def body(z_ref, zh_ref, bin_ref, w_ref, bdw_ref, lg_ref, lb_ref, u_ref, u0_ref, u1_ref, ext):
        i = pl.program_id(0)

        def glu(zz):
            zz = zz + bin_ref[...]
            return zz[:, :c] * jax.nn.sigmoid(zz[:, c:])

        u0 = glu(z_ref[...])
        u0_ref[...] = u0
        ext[0, 0:CONV_HALO, :] = jnp.where(i > 0, glu(zh_ref[...]), 0.0)
        ext[0, CONV_HALO:, :] = u0
        _shifted_copies(ext, tm)
        off = CONV_HALO - (CONV_WIDTH - 1)
        for r in range(tm // 64):
            for cb in range(c // 128):
                cs = slice(cb * 128, (cb + 1) * 128)
                u1_ref[r * 64:(r + 1) * 64, cs] = _sum_taps(
                    _tap_rows(ext, r * 64 + off + k, 64, cs) * w_ref[k:k + 1, cs] for k in range(CONV_WIDTH)) + bdw_ref[:, cs]
        u1 = u1_ref[...]
        mu = jnp.mean(u1, axis=-1, keepdims=True)
        xc = u1 - mu
        y = xc * lax.rsqrt(jnp.mean(xc * xc, axis=-1, keepdims=True) + EPS)
        y = y * lg_ref[...] + lb_ref[...]
        u_ref[...] = (y * jax.nn.sigmoid(y)).astype(BF16)

    row = lambda i: (i, 0)
    fix = lambda i: (0, 0)
    return pl.pallas_call(
        body, grid=(s // tm,),
        in_specs=[pl.BlockSpec((tm, 2 * c), row), pl.BlockSpec((CONV_HALO, 2 * c), _prev_halo(tm, CONV_HALO)),
                  pl.BlockSpec((1, 2 * c), fix), pl.BlockSpec((32, c), fix), pl.BlockSpec((1, c), fix),
                  pl.BlockSpec((1, c), fix), pl.BlockSpec((1, c), fix)],
        out_specs=[pl.BlockSpec((tm, c), row)] * 3,
        out_shape=[SDS((s, c), BF16), SDS((s, c), F32), SDS((s, c), F32)],
        scratch_shapes=[pltpu.VMEM((8, tm + CONV_HALO, c), F32)],
        compiler_params=_cp(1), name="conv_fwd")(z, z, b_in, w32, b_dw, ln_g, ln_b)


def _conv_bwd_ln(d_u, u1, ln_g, ln_b, tm):
    s = d_u.shape[0]
    c = CONV_CH

    def body(du_ref, u1_ref, lg_ref, lb_ref, du1_ref, dlg_ref, dlb_ref, dbdw_ref):
        @pl.when(pl.program_id(0) == 0)
        def _():
            dlg_ref[...] = jnp.zeros_like(dlg_ref)
            dlb_ref[...] = jnp.zeros_like(dlb_ref)
            dbdw_ref[...] = jnp.zeros_like(dbdw_ref)

        u1 = u1_ref[...]
        mu = jnp.mean(u1, axis=-1, keepdims=True)
        xc = u1 - mu
        rs = lax.rsqrt(jnp.mean(xc * xc, axis=-1, keepdims=True) + EPS)
        xh = xc * rs
        y = xh * lg_ref[...] + lb_ref[...]
        sg = jax.nn.sigmoid(y)
        dy = du_ref[...] * (sg * (1.0 + y * (1.0 - sg)))
        dlg_ref[...] += jnp.sum(dy * xh, axis=0, keepdims=True)
        dlb_ref[...] += jnp.sum(dy, axis=0, keepdims=True)
        dxh = dy * lg_ref[...]
        du1 = rs * (dxh - jnp.mean(dxh, axis=-1, keepdims=True) - xh * jnp.mean(dxh * xh, axis=-1, keepdims=True))
        dbdw_ref[...] += jnp.sum(du1, axis=0, keepdims=True)
        du1_ref[...] = du1

    row = lambda i: (i, 0)
    fix = lambda i: (0, 0)
    return pl.pallas_call(
        body, grid=(s // tm,),
        in_specs=[pl.BlockSpec((tm, c), row), pl.BlockSpec((tm, c), row), pl.BlockSpec((1, c), fix), pl.BlockSpec((1, c), fix)],
        out_specs=[pl.BlockSpec((tm, c), row)] + [pl.BlockSpec((1, c), fix)] * 3,
        out_shape=[SDS((s, c), F32)] + [SDS((1, c), F32)] * 3,
        compiler_params=_cp(1), name="conv_bwd_ln")(d_u, u1, ln_g, ln_b)


def _conv_bwd_dw(d_u1, u0, z, b_in, w32, tm):
    s = d_u1.shape[0]
    c = CONV_CH

    def body(d_ref, dn_ref, u0_ref, u0p_ref, z_ref, bin_ref, w_ref, dz_ref, dw_ref, dbin_ref, extd, extu, du0):
        i = pl.program_id(0)
        last = pl.num_programs(0) - 1

        @pl.when(i == 0)
        def _():
            dw_ref[...] = jnp.zeros_like(dw_ref)
            dbin_ref[...] = jnp.zeros_like(dbin_ref)

        extd[0, 0:tm, :] = d_ref[...]
        extd[0, tm:, :] = jnp.where(i < last, dn_ref[...], 0.0)
        extu[0, 0:CONV_HALO, :] = jnp.where(i > 0, u0p_ref[...], 0.0)
        extu[0, CONV_HALO:, :] = u0_ref[...]
        _shifted_copies(extd, tm)
        _shifted_copies(extu, tm)
        off = CONV_HALO - (CONV_WIDTH - 1)
        for r in range(tm // 64):
            for cb in range(c // 128):
                cs = slice(cb * 128, (cb + 1) * 128)
                du0[r * 64:(r + 1) * 64, cs] = _sum_taps(
                    _tap_rows(extd, r * 64 + (CONV_WIDTH - 1) - k, 64, cs) * w_ref[k:k + 1, cs] for k in range(CONV_WIDTH))
        for cb in range(c // 128):
            cs = slice(cb * 128, (cb + 1) * 128)
            accs = [None] * CONV_WIDTH
            for r in range(tm // 64):
                d = d_ref[r * 64:(r + 1) * 64, cs]
                for k in range(CONV_WIDTH):
                    p = d * _tap_rows(extu, r * 64 + off + k, 64, cs)
                    part = _sum_taps(p[q * 8:(q + 1) * 8, :] for q in range(8))
                    accs[k] = part if accs[k] is None else accs[k] + part
            for k in range(CONV_WIDTH):
                dw_ref[k:k + 1, cs] += jnp.sum(accs[k], axis=0, keepdims=True)
        zz = z_ref[...] + bin_ref[...]
        a = zz[:, :c]
        sg = jax.nn.sigmoid(zz[:, c:])
        d0 = du0[...]
        da = d0 * sg
        dgt = d0 * a * (sg * (1.0 - sg))
        dbin_ref[:, :c] += jnp.sum(da, axis=0, keepdims=True)
        dbin_ref[:, c:] += jnp.sum(dgt, axis=0, keepdims=True)
        dz_ref[:, :c] = da.astype(BF16)
        dz_ref[:, c:] = dgt.astype(BF16)

    row = lambda i: (i, 0)
    fix = lambda i: (0, 0)
    return pl.pallas_call(
        body, grid=(s // tm,),
        in_specs=[pl.BlockSpec((tm, c), row), pl.BlockSpec((CONV_HALO, c), _next_halo(tm, CONV_HALO, s)),
                  pl.BlockSpec((tm, c), row), pl.BlockSpec((CONV_HALO, c), _prev_halo(tm, CONV_HALO)),
                  pl.BlockSpec((tm, 2 * c), row), pl.BlockSpec((1, 2 * c), fix), pl.BlockSpec((32, c), fix)],
        out_specs=[pl.BlockSpec((tm, 2 * c), row), pl.BlockSpec((32, c), fix), pl.BlockSpec((1, 2 * c), fix)],
        out_shape=[SDS((s, 2 * c), BF16), SDS((32, c), F32), SDS((1, 2 * c), F32)],
        scratch_shapes=[pltpu.VMEM((8, tm + CONV_HALO, c), F32), pltpu.VMEM((8, tm + CONV_HALO, c), F32), pltpu.VMEM((tm, c), F32)],
        compiler_params=_cp(1), name="conv_bwd_dw")(d_u1, d_u1, u0, u0, z, b_in, w32)


def _partner(v, lane):
    up = pltpu.roll(v, HEAD_PAD - ROPE // 2, 1)
    dn = pltpu.roll(v, ROPE // 2, 1)
    lo = (lane >= NOPE) & (lane < NOPE + ROPE // 2)
    hi = (lane >= NOPE + ROPE // 2) & (lane < HEAD_DIM)
    return jnp.where(lo, up, jnp.where(hi, dn, 0.0))


def _mla_prep(q_raw, kv_raw, z, cosf, sinf, gq, gk, tm):
    s = q_raw.shape[0]

    def body(q_ref, kv_ref, kr_ref, c_ref, s_ref, gq_ref, gk_ref, qo_ref, ko_ref, vo_ref):
        lane = lax.broadcasted_iota(jnp.int32, (tm, HEAD_PAD), 1)
        cf = c_ref[...]
        sf = s_ref[...]

        def norm_rope(t, g_ref):
            r = lax.rsqrt(jnp.sum(t * t, axis=-1, keepdims=True) * (1.0 / HEAD_DIM) + EPS)
            tn = (t * r) * g_ref[...]
            return tn * cf + _partner(tn, lane) * sf

        kr = kr_ref[...]
        for h in range(HEADS):
            hs = slice(h * HEAD_PAD, (h + 1) * HEAD_PAD)
            qo_ref[:, hs] = (norm_rope(q_ref[:, hs], gq_ref) * (ATT_SCALE * LOG2E)).astype(BF16)
            kv = kv_ref[:, hs]
            ko_ref[:, hs] = norm_rope(jnp.where(lane < NOPE, kv, 0.0) + kr, gk_ref).astype(BF16)
            vo_ref[:, hs] = jnp.where(lane >= NOPE, kv, 0.0).astype(BF16)

    row = lambda i: (i, 0)
    wide = pl.BlockSpec((tm, HEADS * HEAD_PAD), row)
    one = pl.BlockSpec((tm, HEAD_PAD), row)
    return pl.pallas_call(
        body, grid=(s // tm,),
        in_specs=[wide, wide, pl.BlockSpec((tm, HEAD_PAD), lambda i: (i, IN_COLS_PAD // HEAD_PAD - 1)), one, one,
                  pl.BlockSpec((1, HEAD_PAD), lambda i: (0, 0)), pl.BlockSpec((1, HEAD_PAD), lambda i: (0, 0))],
        out_specs=[wide] * 3,
        out_shape=[SDS((s, HEADS * HEAD_PAD), BF16)] * 3,
        compiler_params=_cp(1), name="mla_prep")(q_raw, kv_raw, z, cosf, sinf, gq, gk)


def _mla_prep_bwd(dqp, dkp, dvp, q_raw, kv_raw, z, cosf, sinf, gq, gk, tm):
    s = q_raw.shape[0]

    def body(dq_ref, dk_ref, dv_ref, q_ref, kv_ref, kr_ref, c_ref, s_ref, gq_ref, gk_ref,
             dqo_ref, dkvo_ref, dkr_ref, dgq_ref, dgk_ref):
        lane = lax.broadcasted_iota(jnp.int32, (tm, HEAD_PAD), 1)
        cf = c_ref[...]
        sf = s_ref[...]

        @pl.when(pl.program_id(0) == 0)
        def _():
            dgq_ref[...] = jnp.zeros_like(dgq_ref)
            dgk_ref[...] = jnp.zeros_like(dgk_ref)

        def norm_rope_bwd(t, d_out, g_ref, dg_ref):
            r = lax.rsqrt(jnp.sum(t * t, axis=-1, keepdims=True) * (1.0 / HEAD_DIM) + EPS)
            th = t * r
            dn = d_out * cf + _partner(d_out * sf, lane)
            dg_ref[...] += jnp.sum(dn * th, axis=0, keepdims=True)
            dh = dn * g_ref[...]
            return r * (dh - th * (jnp.sum(dh * th, axis=-1, keepdims=True) * (1.0 / HEAD_DIM)))

        kr = kr_ref[...]
        dkr = None
        for h in range(HEADS):
            hs = slice(h * HEAD_PAD, (h + 1) * HEAD_PAD)
            dq = norm_rope_bwd(q_ref[:, hs], dq_ref[:, hs] * ATT_SCALE, gq_ref, dgq_ref)
            dqo_ref[:, hs] = dq.astype(BF16)
            kv = kv_ref[:, hs]
            dkpre = norm_rope_bwd(jnp.where(lane < NOPE, kv, 0.0) + kr, dk_ref[:, hs] * LN2, gk_ref, dgk_ref)
            dkvo_ref[:, hs] = jnp.where(lane < NOPE, dkpre, dv_ref[:, hs]).astype(BF16)
            dkr_h = jnp.where((lane >= NOPE) & (lane < HEAD_DIM), dkpre, 0.0)
            dkr = dkr_h if dkr is None else dkr + dkr_h
        dkr_ref[...] = dkr

    row = lambda i: (i, 0)
    fix = lambda i: (0, 0)
    wide = pl.BlockSpec((tm, HEADS * HEAD_PAD), row)
    one = pl.BlockSpec((tm, HEAD_PAD), row)
    return pl.pallas_call(
        body, grid=(s // tm,),
        in_specs=[wide, wide, wide, wide, wide, pl.BlockSpec((tm, HEAD_PAD), lambda i: (i, IN_COLS_PAD // HEAD_PAD - 1)),
                  one, one, pl.BlockSpec((1, HEAD_PAD), fix), pl.BlockSpec((1, HEAD_PAD), fix)],
        out_specs=[wide, wide, one, pl.BlockSpec((1, HEAD_PAD), fix), pl.BlockSpec((1, HEAD_PAD), fix)],
        out_shape=[SDS((s, HEADS * HEAD_PAD), BF16), SDS((s, HEADS * HEAD_PAD), BF16), SDS((s, HEAD_PAD), F32),
                   SDS((1, HEAD_PAD), F32), SDS((1, HEAD_PAD), F32)],
        compiler_params=_cp(1), name="mla_prep_bwd")(dqp, dkp, dvp, q_raw, kv_raw, z, cosf, sinf, gq, gk)


ATT_GROUP = 8


def _group_schedule(nb, forward):
    g = ATT_GROUP
    one, two, case = [], [], []
    for a in range(nb):
        for b in (range(a // g + 1) if forward else range(a // g, nb // g)):
            one.append(a)
            two.append(b)
            case.append(0 if b != a // g else 1 + a % g)
    return tuple(jnp.asarray(np.array(x, np.int32)) for x in (one, two, case))


STRIP = 64


def _fold8(x):
    acc = x[0:8, :]
    for g in range(1, x.shape[0] // 8):
        acc = acc + x[g * 8:(g + 1) * 8, :]
    return acc


def _attn_fwd(qp, kp, vp, tb, w_late):
    s = qp.shape[0]
    ii, jj, cc = _group_schedule(s // tb, True)
    n_steps = int(ii.shape[0])

    def body(ii_ref, jj_ref, cc_ref, q_ref, k_ref, v_ref, wl_ref, of_ref, ob_ref, lse_ref, gl_ref,
             m_sc, l_sc, acc_sc, st_sc, pt_sc, send_sems, recv_sems):
        t = pl.program_id(1)
        case = cc_ref[t]

        @pl.when((pl.program_id(0) == 0) & (t == 0))
        def _():
            _ag_send(wl_ref, gl_ref, send_sems, recv_sems)

        @pl.when(jj_ref[t] == 0)
        def _():
            m_sc[...] = jnp.full_like(m_sc, -jnp.inf)
            l_sc[...] = jnp.zeros_like(l_sc)
            acc_sc[...] = jnp.zeros_like(acc_sc)

        def step(n_keys, diag_at):
            def visible(r):
                if diag_at is None or r * STRIP <= diag_at:
                    return None
                col = lax.broadcasted_iota(jnp.int32, (STRIP, tb), 1)
                return col >= r * STRIP - diag_at

            st_sc[0:n_keys, :] = lax.dot_general(k_ref[0:n_keys, :], q_ref[...], (((1,), (1,)), ((), ())), preferred_element_type=F32)
            mx = None
            for r in range(n_keys // STRIP):
                sc = st_sc[r * STRIP:(r + 1) * STRIP, :]
                if visible(r) is not None:
                    sc = jnp.where(visible(r), sc, -jnp.inf)
                m8 = sc[0:8, :]
                for g in range(1, STRIP // 8):
                    m8 = jnp.maximum(m8, sc[g * 8:(g + 1) * 8, :])
                mx = m8 if mx is None else jnp.maximum(mx, m8)
            m_old = m_sc[0:1, :]
            m_new = jnp.maximum(m_old, jnp.max(mx, axis=0, keepdims=True))
            alpha = jnp.exp2(m_old - m_new)
            ps = None
            pv = None
            for b in range(n_keys // tb):
                for r in range(b * tb // STRIP, (b + 1) * tb // STRIP):
                    p = jnp.exp2(st_sc[r * STRIP:(r + 1) * STRIP, :] - m_new)
                    if visible(r) is not None:
                        p = jnp.where(visible(r), p, 0.0)
                    ps = _fold8(p) if ps is None else ps + _fold8(p)
                    pt_sc[r * STRIP:(r + 1) * STRIP, :] = p.astype(BF16)
                pvb = lax.dot_general(v_ref[b * tb:(b + 1) * tb, :], pt_sc[b * tb:(b + 1) * tb, :], (((0,), (0,)), ((), ())),
                                      preferred_element_type=F32)
                pv = pvb if pv is None else pv + pvb
            l_new = alpha * l_sc[0:1, :] + jnp.sum(ps, axis=0, keepdims=True)
            m_sc[...] = jnp.broadcast_to(m_new, m_sc.shape)
            l_sc[...] = jnp.broadcast_to(l_new, l_sc.shape)
            acc_sc[...] = alpha * acc_sc[...] + pv

        @pl.when(case == 0)
        def _():
            step(ATT_GROUP * tb, None)

        for d in range(ATT_GROUP):
            @pl.when(case == 1 + d)
            def _(d=d):
                step((d + 1) * tb, d * tb)

        @pl.when(case != 0)
        def _():
            l = l_sc[0:1, :]
            o = (acc_sc[...] / l).T
            of_ref[...] = o
            ob_ref[...] = o.astype(BF16)
            lse_ref[...] = m_sc[0:1, :] + jnp.log(l) * LOG2E

        @pl.when((pl.program_id(0) == HEADS - 1) & (t == n_steps - 1))
        def _():
            _ag_finish(wl_ref, gl_ref, send_sems, recv_sems)

    qmap = lambda h, t, ii_ref, jj_ref, cc_ref: (ii_ref[t], h)
    kmap = lambda h, t, ii_ref, jj_ref, cc_ref: (jj_ref[t], h)
    gs = pltpu.PrefetchScalarGridSpec(
        num_scalar_prefetch=3, grid=(HEADS, n_steps),
        in_specs=[pl.BlockSpec((tb, HEAD_PAD), qmap), pl.BlockSpec((ATT_GROUP * tb, HEAD_PAD), kmap),
                  pl.BlockSpec((ATT_GROUP * tb, HEAD_PAD), kmap), ANY],
        out_specs=[pl.BlockSpec((tb, HEAD_PAD), qmap), pl.BlockSpec((tb, HEAD_PAD), qmap),
                   pl.BlockSpec((None, 1, tb), lambda h, t, ii_ref, jj_ref, cc_ref: (h, 0, ii_ref[t])), ANY],
        scratch_shapes=[pltpu.VMEM((8, tb), F32), pltpu.VMEM((8, tb), F32), pltpu.VMEM((HEAD_PAD, tb), F32),
                        pltpu.VMEM((ATT_GROUP * tb, tb), F32), pltpu.VMEM((ATT_GROUP * tb, tb), BF16),
                        pltpu.SemaphoreType.DMA((6,)), pltpu.SemaphoreType.DMA((6,))])
    w = HEADS * HEAD_PAD
    return pl.pallas_call(
        body, grid_spec=gs,
        out_shape=[SDS((s, w), F32), SDS((s, w), BF16), SDS((HEADS, 1, s), F32), SDS((4,) + w_late.shape, w_late.dtype)],
        compiler_params=_cp(2), name="attn_fwd")(ii, jj, cc, qp, kp, vp, w_late)


def _attn_delta(do, o, tb):
    s = do.shape[0]

    def body(do_ref, o_ref, d_ref):
        for h in range(HEADS):
            hs = slice(h * HEAD_PAD, (h + 1) * HEAD_PAD)
            d_ref[h] = jnp.sum((do_ref[:, hs] * o_ref[:, hs]).T, axis=0, keepdims=True)

    blk = pl.BlockSpec((tb, HEADS * HEAD_PAD), lambda i: (i, 0))
    return pl.pallas_call(body, grid=(s // tb,), in_specs=[blk, blk],
                          out_specs=pl.BlockSpec((HEADS, 1, tb), lambda i: (0, 0, i)),
                          out_shape=SDS((HEADS, 1, s), F32), compiler_params=_cp(1), name="attn_delta")(do, o)


def _attn_bwd(qp, kp, vp, dob, lse, delta, tb, wire):
    s = qp.shape[0]
    nb = s // tb
    jj, ii, cc = _group_schedule(nb, False)
    gw = ATT_GROUP * tb
    rows = 32

    n_steps = int(ii.shape[0])

    def body(jj_ref, ii_ref, cc_ref, q_ref, k_ref, v_ref, do_ref, lse_ref, dl_ref, cw_ref, dqo_ref, dk_ref, dv_ref, rcv_ref,
             st_sc, dpt_sc, pt_sc, dst_sc, dq_ref, send_sems, recv_sems):
        t = pl.program_id(1)
        case = cc_ref[t]
        pair = ii_ref[t]

        @pl.when((pl.program_id(0) == 0) & (t == 0))
        def _():
            _rs_send(cw_ref, rcv_ref, send_sems, recv_sems)

        @pl.when((pl.program_id(0) == HEADS - 1) & (t == n_steps - 1))
        def _():
            _rs_wait(rcv_ref, send_sems, recv_sems)

        @pl.when(t == 0)
        def _():
            dq_ref[...] = jnp.zeros_like(dq_ref)

        @pl.when(case != 0)
        def _():
            dk_ref[...] = jnp.zeros_like(dk_ref)
            dv_ref[...] = jnp.zeros_like(dv_ref)

        def step(lo, width, diag):
            q = q_ref[lo:lo + width, :]
            do = do_ref[lo:lo + width, :]
            k = k_ref[...]
            st_sc[:, 0:width] = lax.dot_general(k, q, (((1,), (1,)), ((), ())), preferred_element_type=F32)
            dpt_sc[:, 0:width] = lax.dot_general(v_ref[...], do, (((1,), (1,)), ((), ())), preferred_element_type=F32)
            dv = dk = None
            for h in range(width // tb):
                ls = slice(h * tb, (h + 1) * tb)
                lse_row = lse_ref[:, lo + h * tb:lo + (h + 1) * tb]
                dl_row = dl_ref[:, lo + h * tb:lo + (h + 1) * tb]
                for r in range(tb // rows):
                    rs = slice(r * rows, (r + 1) * rows)
                    p = jnp.exp2(st_sc[rs, ls] - lse_row)
                    first_visible = (r * rows) // CHUNK * CHUNK
                    if diag and h == 0 and first_visible > 0:
                        col = lax.broadcasted_iota(jnp.int32, (rows, tb), 1)
                        p = jnp.where(col >= first_visible, p, 0.0)
                    ds = p * (dpt_sc[rs, ls] - dl_row)
                    pt_sc[rs, ls] = p.astype(BF16)
                    dst_sc[rs, ls] = ds.astype(BF16)
                dst = dst_sc[:, ls]
                dvh = jnp.dot(pt_sc[:, ls], do[ls, :], preferred_element_type=F32)
                dkh = jnp.dot(dst, q[ls, :], preferred_element_type=F32)
                dv = dvh if dv is None else dv + dvh
                dk = dkh if dk is None else dk + dkh
                dq_ref[ATT_GROUP * pair + lo // tb + h] += lax.dot_general(k, dst, (((0,), (0,)), ((), ())), preferred_element_type=F32)
            dv_ref[...] += dv
            dk_ref[...] += dk

        @pl.when(case == 0)
        def _():
            step(0, ATT_GROUP * tb, False)

        for d in range(ATT_GROUP):
            @pl.when(case == 1 + d)
            def _(d=d):
                step(d * tb, (ATT_GROUP - d) * tb, True)

        @pl.when(t == n_steps - 1)
        def _():
            for i in range(nb):
                dqo_ref[i * tb:(i + 1) * tb, :] = dq_ref[i].T

    qmap = lambda h, t, jj_ref, ii_ref, cc_ref: (ii_ref[t], h)
    kmap = lambda h, t, jj_ref, ii_ref, cc_ref: (jj_ref[t], h)
    rowmap = lambda h, t, jj_ref, ii_ref, cc_ref: (h, 0, ii_ref[t])
    gs = pltpu.PrefetchScalarGridSpec(
        num_scalar_prefetch=3, grid=(HEADS, n_steps),
        in_specs=[pl.BlockSpec((gw, HEAD_PAD), qmap), pl.BlockSpec((tb, HEAD_PAD), kmap), pl.BlockSpec((tb, HEAD_PAD), kmap),
                  pl.BlockSpec((gw, HEAD_PAD), qmap), pl.BlockSpec((None, 1, gw), rowmap), pl.BlockSpec((None, 1, gw), rowmap), ANY],
        out_specs=[pl.BlockSpec((s, HEAD_PAD), lambda h, t, jj_ref, ii_ref, cc_ref: (0, h)),
                   pl.BlockSpec((tb, HEAD_PAD), kmap), pl.BlockSpec((tb, HEAD_PAD), kmap), ANY],
        scratch_shapes=[pltpu.VMEM((tb, gw), F32), pltpu.VMEM((tb, gw), F32), pltpu.VMEM((tb, gw), BF16),
                        pltpu.VMEM((tb, gw), BF16), pltpu.VMEM((nb, HEAD_PAD, tb), F32),
                        pltpu.SemaphoreType.DMA((3,)), pltpu.SemaphoreType.DMA((3,))])
    w = HEADS * HEAD_PAD
    return pl.pallas_call(
        body, grid_spec=gs,
        out_shape=[SDS((s, w), F32), SDS((s, w), F32), SDS((s, w), F32), SDS((3,) + wire.shape[1:], wire.dtype)],
        compiler_params=_cp(2), name="attn_bwd")(jj, ii, cc, qp, kp, vp, dob, lse, delta, wire)


def _head_norm(t, g):
    r = lax.rsqrt(jnp.mean(t * t, axis=-1, keepdims=True) + EPS)
    th = t * r
    return r, th, th * g


def _softmax_rows(sc):
    m = jnp.max(sc, axis=-1, keepdims=True)
    e = jnp.exp(sc - m)
    return e / jnp.sum(e, axis=-1, keepdims=True)


def _memattn_fwd(qm, kvm, gq, gk, tm):
    s = qm.shape[0]
    hd = MEM_HEAD_DIM

    def body(q_ref, k_ref, v_ref, gq_ref, gk_ref, o_ref):
        _, _, qn = _head_norm(q_ref[...], gq_ref[...])
        _, _, kn = _head_norm(k_ref[...], gk_ref[...])
        sc = lax.dot_general(qn.astype(BF16), kn.astype(BF16), (((1,), (1,)), ((), ())), preferred_element_type=F32)
        p = _softmax_rows(sc * (1.0 / math.sqrt(hd)))
        o_ref[...] = jnp.dot(p.astype(BF16), v_ref[...].astype(BF16), preferred_element_type=F32).astype(BF16)

    fix = lambda i, h: (0, 0)
    return pl.pallas_call(
        body, grid=(s // tm, MEM_HEADS),
        in_specs=[pl.BlockSpec((tm, hd), lambda i, h: (i, h)), pl.BlockSpec((MEM_LEN, hd), lambda i, h: (0, h)),
                  pl.BlockSpec((MEM_LEN, hd), lambda i, h: (0, MEM_HEADS + h)), pl.BlockSpec((1, hd), fix), pl.BlockSpec((1, hd), fix)],
        out_specs=pl.BlockSpec((tm, hd), lambda i, h: (i, h)),
        out_shape=SDS((s, MEM_HEADS * hd), BF16), compiler_params=_cp(2), name="memattn_fwd")(qm, kvm, kvm, gq, gk)


def _memattn_bwd(qm, kvm, d_o, gq, gk, tm):
    s = qm.shape[0]
    hd = MEM_HEAD_DIM

    def body(q_ref, k_ref, v_ref, do_ref, gq_ref, gk_ref, dq_ref, dk_ref, dv_ref, dgq_ref, dgk_ref, dkn_sc):
        h = pl.program_id(0)
        i = pl.program_id(1)
        last = pl.num_programs(1) - 1

        @pl.when((h == 0) & (i == 0))
        def _():
            dgq_ref[...] = jnp.zeros_like(dgq_ref)
            dgk_ref[...] = jnp.zeros_like(dgk_ref)

        @pl.when(i == 0)
        def _():
            dv_ref[...] = jnp.zeros_like(dv_ref)
            dkn_sc[...] = jnp.zeros_like(dkn_sc)

        rq, qh, qn = _head_norm(q_ref[...], gq_ref[...])
        rk, kh, kn = _head_norm(k_ref[...], gk_ref[...])
        qnb = qn.astype(BF16)
        knb = kn.astype(BF16)
        scale = 1.0 / math.sqrt(hd)
        sc = lax.dot_general(qnb, knb, (((1,), (1,)), ((), ())), preferred_element_type=F32)
        p = _softmax_rows(sc * scale)
        do = do_ref[...].astype(BF16)
        dp = lax.dot_general(do, v_ref[...].astype(BF16), (((1,), (1,)), ((), ())), preferred_element_type=F32)
        dv_ref[...] += lax.dot_general(p.astype(BF16), do, (((0,), (0,)), ((), ())), preferred_element_type=F32)
        ds = ((p * (dp - jnp.sum(dp * p, axis=-1, keepdims=True))) * scale).astype(BF16)
        dqn = jnp.dot(ds, knb, preferred_element_type=F32)
        dkn_sc[...] += lax.dot_general(ds, qnb, (((0,), (0,)), ((), ())), preferred_element_type=F32)
        dgq_ref[...] += jnp.sum(dqn * qh, axis=0, keepdims=True)
        dqh = dqn * gq_ref[...]
        dq_ref[...] = (rq * (dqh - qh * jnp.mean(dqh * qh, axis=-1, keepdims=True))).astype(BF16)

        @pl.when(i == last)
        def _():
            dkn = dkn_sc[...]
            dgk_ref[...] += jnp.sum(dkn * kh, axis=0, keepdims=True)
            dkh = dkn * gk_ref[...]
            dk_ref[...] = rk * (dkh - kh * jnp.mean(dkh * kh, axis=-1, keepdims=True))

    fix = lambda h, i: (0, 0)
    qb = pl.BlockSpec((tm, hd), lambda h, i: (i, h))
    kb = pl.BlockSpec((MEM_LEN, hd), lambda h, i: (0, h))
    return pl.pallas_call(
        body, grid=(MEM_HEADS, s // tm),
        in_specs=[qb, kb, pl.BlockSpec((MEM_LEN, hd), lambda h, i: (0, MEM_HEADS + h)), qb,
                  pl.BlockSpec((1, hd), fix), pl.BlockSpec((1, hd), fix)],
        out_specs=[qb, kb, kb, pl.BlockSpec((1, hd), fix), pl.BlockSpec((1, hd), fix)],
        out_shape=[SDS((s, MEM_HEADS * hd), BF16), SDS((MEM_LEN, MEM_HEADS * hd), F32), SDS((MEM_LEN, MEM_HEADS * hd), F32),
                   SDS((1, hd), F32), SDS((1, hd), F32)],
        scratch_shapes=[pltpu.VMEM((MEM_LEN, hd), F32)],
        compiler_params=_cp(2), name="memattn_bwd")(qm, kvm, kvm, d_o, gq, gk)


def _ffn_specs(tm, tn, nbj, s, order_ji):
    if order_ji:
        ij = lambda f: (lambda j, i: f(i, j))
    else:
        ij = lambda f: f
    prev = lambda i: jnp.maximum(i * (tm // FFN_HALO) - 1, 0)
    cur_g = pl.BlockSpec((tm, tn), ij(lambda i, j: (i, j)))
    cur_v = pl.BlockSpec((tm, tn), ij(lambda i, j: (i, j + nbj)))
    halo_g = pl.BlockSpec((FFN_HALO, tn), ij(lambda i, j: (prev(i), j)))
    halo_v = pl.BlockSpec((FFN_HALO, tn), ij(lambda i, j: (prev(i), j + nbj)))
    w_g = pl.BlockSpec((8, tn), ij(lambda i, j: (0, j)))
    w_v = pl.BlockSpec((8, tn), ij(lambda i, j: (0, j + nbj)))
    b_g = pl.BlockSpec((1, tn), ij(lambda i, j: (0, j)))
    b_v = pl.BlockSpec((1, tn), ij(lambda i, j: (0, j + nbj)))
    return cur_g, cur_v, halo_g, halo_v, w_g, w_v, b_g, b_v


FFN_STRIP = 16


def _conv3_rows(ext, w_ref, b_ref, o, n):
    return (w_ref[0:1, :] * ext[FFN_HALO - 2 + o:FFN_HALO - 2 + o + n, :] + w_ref[1:2, :] * ext[FFN_HALO - 1 + o:FFN_HALO - 1 + o + n, :]
            + w_ref[2:3, :] * ext[FFN_HALO + o:FFN_HALO + o + n, :] + b_ref[...])


def _ffn_fwd(up0, w8, b, tm, tn):
    s = up0.shape[0]
    nbj = D_FF // tn

    def body(g_ref, v_ref, gh_ref, vh_ref, wg_ref, wv_ref, bg_ref, bv_ref, act_ref, extg, extv):
        first = pl.program_id(0) == 0
        for ext, h_ref, c_ref in ((extg, gh_ref, g_ref), (extv, vh_ref, v_ref)):
            ext[0:FFN_HALO, :] = jnp.where(first, 0.0, h_ref[...])
            ext[FFN_HALO:, :] = c_ref[...]
        for r in range(tm // FFN_STRIP):
            o = r * FFN_STRIP
            ug = _conv3_rows(extg, wg_ref, bg_ref, o, FFN_STRIP)
            uv = _conv3_rows(extv, wv_ref, bv_ref, o, FFN_STRIP)
            act_ref[o:o + FFN_STRIP, :] = ((ug * jax.nn.sigmoid(ug)) * uv).astype(BF16)

    specs = _ffn_specs(tm, tn, nbj, s, False)
    return pl.pallas_call(
        body, grid=(s // tm, nbj), in_specs=list(specs),
        out_specs=pl.BlockSpec((tm, tn), lambda i, j: (i, j)), out_shape=SDS((s, D_FF), BF16),
        scratch_shapes=[pltpu.VMEM((tm + FFN_HALO, tn), F32), pltpu.VMEM((tm + FFN_HALO, tn), F32)],
        compiler_params=_cp(2), name="ffn_fwd")(up0, up0, up0, up0, w8, w8, b, b)


def _ffn_bwd(d_act, up0, w8, b, tm, tn):
    s = up0.shape[0]
    nbj = D_FF // tn
    te = tm + FFN_HALO

    def body(da_ref, dan_ref, g_ref, v_ref, gh_ref, vh_ref, gn_ref, vn_ref, wg_ref, wv_ref, bg_ref, bv_ref,
             og_ref, ov_ref, dbg_ref, dbv_ref, dwg_ref, dwv_ref, extg, extv, extdg, extdv, accg, accv):
        i = pl.program_id(1)
        first = i == 0
        last = i == pl.num_programs(1) - 1

        @pl.when(first)
        def _():
            for r in (dbg_ref, dbv_ref, dwg_ref, dwv_ref):
                r[...] = jnp.zeros_like(r)

        for ext, h_ref, c_ref, n_ref in ((extg, gh_ref, g_ref, gn_ref), (extv, vh_ref, v_ref, vn_ref)):
            ext[0:FFN_HALO, :] = jnp.where(first, 0.0, h_ref[...])
            ext[FFN_HALO:FFN_HALO + tm, :] = c_ref[...]
            ext[FFN_HALO + tm:, :] = n_ref[...]

        def fold8(x):
            acc = x[0:8, :]
            for q in range(1, x.shape[0] // 8):
                acc = acc + x[q * 8:(q + 1) * 8, :]
            return acc

        def taps(ext, o, n):
            return [ext[FFN_HALO - 2 + k + o:FFN_HALO - 2 + k + o + n, :] for k in range(3)]

        accg[...] = jnp.zeros_like(accg)
        accv[...] = jnp.zeros_like(accv)

        def gate_bwd(o, n, da, own_rows):
            xg, xv = taps(extg, o, n), taps(extv, o, n)
            ug = wg_ref[0:1, :] * xg[0] + wg_ref[1:2, :] * xg[1] + wg_ref[2:3, :] * xg[2] + bg_ref[...]
            uv = wv_ref[0:1, :] * xv[0] + wv_ref[1:2, :] * xv[1] + wv_ref[2:3, :] * xv[2] + bv_ref[...]
            sg = jax.nn.sigmoid(ug)
            dgt = da * uv * (sg * (1.0 + ug * (1.0 - sg)))
            dvl = da * (ug * sg)
            extdg[o:o + n, :] = dgt
            extdv[o:o + n, :] = dvl
            if own_rows:
                for acc, d, x in ((accg, dgt, xg), (accv, dvl, xv)):
                    acc[0] += fold8(d)
                    for k in range(3):
                        acc[1 + k] += fold8(d * x[k])

        for r in range(tm // FFN_STRIP):
            gate_bwd(r * FFN_STRIP, FFN_STRIP, da_ref[r * FFN_STRIP:(r + 1) * FFN_STRIP, :], True)
        gate_bwd(tm, FFN_HALO, jnp.where(last, 0.0, dan_ref[...]), False)

        for extd, w_ref, o_ref, db_ref, dw_ref, acc in ((extdg, wg_ref, og_ref, dbg_ref, dwg_ref, accg),
                                                        (extdv, wv_ref, ov_ref, dbv_ref, dwv_ref, accv)):
            for r in range(tm // FFN_STRIP):
                o = r * FFN_STRIP
                o_ref[o:o + FFN_STRIP, :] = (w_ref[2:3, :] * extd[o:o + FFN_STRIP, :] + w_ref[1:2, :] * extd[o + 1:o + 1 + FFN_STRIP, :]
                                             + w_ref[0:1, :] * extd[o + 2:o + 2 + FFN_STRIP, :]).astype(BF16)
            db_ref[...] += jnp.sum(acc[0], axis=0, keepdims=True)
            for k in range(3):
                dw_ref[k:k + 1, :] += jnp.sum(acc[1 + k], axis=0, keepdims=True)

    cur_g, cur_v, halo_g, halo_v, w_g, w_v, b_g, b_v = _ffn_specs(tm, tn, nbj, s, True)
    nxt_row = lambda i: jnp.minimum((i + 1) * (tm // FFN_HALO), s // FFN_HALO - 1)
    cur = pl.BlockSpec((tm, tn), lambda j, i: (i, j))
    nxt = pl.BlockSpec((FFN_HALO, tn), lambda j, i: (nxt_row(i), j))
    nxt_v = pl.BlockSpec((FFN_HALO, tn), lambda j, i: (nxt_row(i), j + nbj))
    acc1 = pl.BlockSpec((1, tn), lambda j, i: (0, j))
    acc8 = pl.BlockSpec((8, tn), lambda j, i: (0, j))
    return pl.pallas_call(
        body, grid=(nbj, s // tm), in_specs=[cur, nxt, cur_g, cur_v, halo_g, halo_v, nxt, nxt_v, w_g, w_v, b_g, b_v],
        out_specs=[cur, cur, acc1, acc1, acc8, acc8],
        out_shape=[SDS((s, D_FF), BF16), SDS((s, D_FF), BF16), SDS((1, D_FF), F32), SDS((1, D_FF), F32),
                   SDS((8, D_FF), F32), SDS((8, D_FF), F32)],
        scratch_shapes=[pltpu.VMEM((tm + 2 * FFN_HALO, tn), F32), pltpu.VMEM((tm + 2 * FFN_HALO, tn), F32),
                        pltpu.VMEM((te, tn), F32), pltpu.VMEM((te, tn), F32),
                        pltpu.VMEM((4, 8, tn), F32), pltpu.VMEM((4, 8, tn), F32)],
        compiler_params=_cp(2), name="ffn_bwd")(d_act, d_act, up0, up0, up0, up0, up0, up0, w8, w8, b, b)


def _down_loss(act, w_down, x2, target, tm):
    s = act.shape[0]

    def body(a_ref, w_ref, x_ref, t_ref, dyf_ref, dyb_ref, ls_ref):
        @pl.when(pl.program_id(0) == 0)
        def _():
            ls_ref[...] = jnp.zeros_like(ls_ref)

        y = x_ref[...] + jnp.dot(a_ref[...], w_ref[...], preferred_element_type=F32)
        e = y - t_ref[...]
        ls_ref[...] += jnp.sum(e * e)
        dy = e * (1.0 / D_MODEL)
        dyf_ref[...] = dy
        dyb_ref[...] = dy.astype(BF16)

    row = lambda i: (i, 0)
    return pl.pallas_call(
        body, grid=(s // tm,),
        in_specs=[pl.BlockSpec((tm, D_FF), row), pl.BlockSpec((D_FF, D_MODEL), lambda i: (0, 0)),
                  pl.BlockSpec((tm, D_MODEL), row), pl.BlockSpec((tm, D_MODEL), row)],
        out_specs=[pl.BlockSpec((tm, D_MODEL), row), pl.BlockSpec((tm, D_MODEL), row), pl.BlockSpec((8, 128), lambda i: (0, 0))],
        out_shape=[SDS((s, D_MODEL), F32), SDS((s, D_MODEL), BF16), SDS((8, 128), F32)],
        compiler_params=_cp(1), name="down_loss")(act, w_down, x2, target)


def _adamw_math(w, g, m, v):
    mn = ADAM_B1 * m + (1.0 - ADAM_B1) * g
    vn = ADAM_B2 * v + (1.0 - ADAM_B2) * (g * g)
    m_hat = mn / (1.0 - ADAM_B1 ** ADAM_STEP)
    v_hat = vn / (1.0 - ADAM_B2 ** ADAM_STEP)
    return -ADAM_LR * (m_hat / (jnp.sqrt(v_hat) + ADAM_EPS) + ADAM_WD * w), mn, vn


def _adamw(w, g, m, v, name):
    rows, cols = w.shape
    tr = rows if rows <= 256 else (256 if rows % 256 == 0 else rows // 2)

    def body(w_ref, g_ref, m_ref, v_ref, d_ref, mo_ref, vo_ref):
        d_ref[...], mo_ref[...], vo_ref[...] = _adamw_math(w_ref[...], g_ref[...], m_ref[...], v_ref[...])

    blk = pl.BlockSpec((tr, cols), lambda i: (i, 0))
    return pl.pallas_call(body, grid=(rows // tr,), in_specs=[blk] * 4, out_specs=[blk] * 3,
                          out_shape=[SDS((rows, cols), F32)] * 3, compiler_params=_cp(1), name=name)(w, g, m, v)


def _adamw_small(ws, gs, ms, vs):
    n = len(ws)

    def body(*refs):
        ins, outs = refs[:4 * n], refs[4 * n:]
        for k in range(n):
            d, mn, vn = _adamw_math(ins[k][...], ins[n + k][...], ins[2 * n + k][...], ins[3 * n + k][...])
            outs[k][...] = d
            outs[n + k][...] = mn
            outs[2 * n + k][...] = vn

    vm = pl.BlockSpec(memory_space=pltpu.VMEM)
    outs = pl.pallas_call(body, in_specs=[vm] * (4 * n), out_specs=[vm] * (3 * n),
                          out_shape=[SDS(w.shape, F32) for w in ws] * 3, name="adamw_small")(*ws, *gs, *ms, *vs)
    return outs[:n], outs[n:2 * n], outs[2 * n:]


ANY = pl.BlockSpec(memory_space=pl.ANY)


def _coords():
    return lax.axis_index("x"), lax.axis_index("y"), lax.axis_index("c")


def _other_chips(x, y):
    return [(1 - x, y), (x, 1 - y), (1 - x, 1 - y)]


D2D_CHUNKS = 8
ICI_CHUNKS = 4


def _row_chunks(n_rows, n_chunks, align):
    step = -(-n_rows // (n_chunks * align)) * align
    return [(r, min(step, n_rows - r)) for r in range(0, n_rows, step)]


def _ag_copy(out_ref, send_sems, recv_sems, k, shard, base, r0, nr, to, src=None):
    rows_ = pl.ds(pl.multiple_of(base + r0, 16), nr)
    dst = out_ref.at[shard, rows_]
    return pltpu.make_async_remote_copy(src_ref=dst if src is None else src.at[rows_], dst_ref=dst, send_sem=send_sems.at[k],
                                        recv_sem=recv_sems.at[k], device_id=to, device_id_type=MESH)


def _ag_send(w_ref, out_ref, send_sems, recv_sems):
    x, y, c = _coords()
    half_rows = w_ref.shape[0] // 2
    for k, (px, py) in enumerate(_other_chips(x, y)):
        for r0, nr in _row_chunks(half_rows, ICI_CHUNKS, 16):
            _ag_copy(out_ref, send_sems, recv_sems, k, 2 * x + y, c * half_rows, r0, nr, (px, py, c), src=w_ref).start()


def _ag_finish(w_ref, out_ref, send_sems, recv_sems):
    x, y, c = _coords()
    half_rows = w_ref.shape[0] // 2
    chips = _other_chips(x, y)
    sibling = (x, y, 1 - c)
    for k, (px, py) in enumerate(chips):
        _ag_copy(out_ref, send_sems, recv_sems, k, 2 * px + py, c * half_rows, 0, half_rows, (px, py, c)).wait_recv()
        for r0, nr in _row_chunks(half_rows, ICI_CHUNKS, 16):
            _ag_copy(out_ref, send_sems, recv_sems, 3 + k, 2 * px + py, c * half_rows, r0, nr, sibling).start()
    for k, (px, py) in enumerate(chips):
        _ag_copy(out_ref, send_sems, recv_sems, 3 + k, 2 * px + py, (1 - c) * half_rows, 0, half_rows, sibling).wait_recv()
    for k in range(6):
        _ag_copy(out_ref, send_sems, recv_sems, k, 2 * x + y, c * half_rows, 0, half_rows, sibling).wait_send()


def _ag_weights(wsh):
    rows, cols = wsh.shape

    def body(w_ref, out_ref, send_sems, recv_sems):
        _ag_send(w_ref, out_ref, send_sems, recv_sems)
        _ag_finish(w_ref, out_ref, send_sems, recv_sems)

    return pl.pallas_call(
        body, in_specs=[ANY], out_specs=ANY, out_shape=SDS((4, rows, cols), wsh.dtype),
        scratch_shapes=[pltpu.SemaphoreType.DMA((6,)), pltpu.SemaphoreType.DMA((6,))],
        name="ag_weights")(wsh)


def _rs_swap_halves(gfull, tag):
    n_sh, rows, cols = gfull.shape
    half_rows = rows // 2

    def body(g_ref, recv_ref, send_sem, recv_sem):
        x, y, c = _coords()
        sib_base = (1 - c) * half_rows
        for sh in range(n_sh):
            for r0, nr in _row_chunks(half_rows, D2D_CHUNKS, 8):
                pltpu.make_async_remote_copy(
                    src_ref=g_ref.at[sh, pl.ds(pl.multiple_of(sib_base + r0, 8), nr)], dst_ref=recv_ref.at[sh, pl.ds(r0, nr)],
                    send_sem=send_sem, recv_sem=recv_sem, device_id=(x, y, 1 - c), device_id_type=MESH).start()
        pltpu.make_async_remote_copy(src_ref=recv_ref, dst_ref=recv_ref, send_sem=send_sem, recv_sem=recv_sem,
                                     device_id=(x, y, 1 - c), device_id_type=MESH).wait()

    return pl.pallas_call(
        body, in_specs=[ANY], out_specs=ANY, out_shape=SDS((n_sh, half_rows, cols), gfull.dtype),
        scratch_shapes=[pltpu.SemaphoreType.DMA, pltpu.SemaphoreType.DMA], name="rs_swap_halves" + tag)(gfull)


def _rs_add_pair(gfull, recv, core, tr, tag):
    n_sh, rows, cols = gfull.shape
    half_rows = rows // 2
    nblk = half_rows // tr

    def body(c_ref, g_ref, r_ref, o_ref, ob_ref):
        acc = g_ref[...] + r_ref[...]
        o_ref[...] = acc
        ob_ref[...] = acc.astype(BF16)

    out = pl.BlockSpec((None, tr, cols), lambda sh, i, c_ref: (sh, i, 0))
    gs = pltpu.PrefetchScalarGridSpec(
        num_scalar_prefetch=1, grid=(n_sh, nblk),
        in_specs=[pl.BlockSpec((None, tr, cols), lambda sh, i, c_ref: (sh, c_ref[0] * nblk + i, 0)), out],
        out_specs=[out, out])
    return pl.pallas_call(body, grid_spec=gs, out_shape=[SDS((n_sh, half_rows, cols), F32), SDS((n_sh, half_rows, cols), BF16)],
                          compiler_params=_cp(2), name="rs_add_pair" + tag)(core, gfull, recv)


def _rs_send(cs_ref, recv_ref, send_sems, recv_sems):
    x, y, c = _coords()
    half_rows = cs_ref.shape[1]
    for k, (px, py) in enumerate(_other_chips(x, y)):
        for r0, nr in _row_chunks(half_rows, ICI_CHUNKS, 16):
            pltpu.make_async_remote_copy(
                src_ref=cs_ref.at[2 * px + py, pl.ds(r0, nr)], dst_ref=recv_ref.at[k, pl.ds(r0, nr)],
                send_sem=send_sems.at[k], recv_sem=recv_sems.at[k], device_id=(px, py, c), device_id_type=MESH).start()


def _rs_wait(recv_ref, send_sems, recv_sems):
    x, y, c = _coords()
    for k, (px, py) in enumerate(_other_chips(x, y)):
        pltpu.make_async_remote_copy(src_ref=recv_ref.at[k], dst_ref=recv_ref.at[k], send_sem=send_sems.at[k],
                                     recv_sem=recv_sems.at[k], device_id=(px, py, c), device_id_type=MESH).wait()


def _rs_to_owner(chipsum):
    n_sh, half_rows, cols = chipsum.shape

    def body(cs_ref, recv_ref, send_sems, recv_sems):
        _rs_send(cs_ref, recv_ref, send_sems, recv_sems)
        _rs_wait(recv_ref, send_sems, recv_sems)

    return pl.pallas_call(
        body, in_specs=[ANY], out_specs=ANY, out_shape=SDS((3, half_rows, cols), chipsum.dtype),
        scratch_shapes=[pltpu.SemaphoreType.DMA((3,)), pltpu.SemaphoreType.DMA((3,))], name="rs_to_owner")(chipsum)


def _rs_add_chips(chipsum, recv, shard_core, tr, tag):
    _, half_rows, cols = chipsum.shape

    def body(s_ref, m_ref, r0_ref, r1_ref, r2_ref, o_ref):
        o_ref[...] = ((m_ref[...] + r0_ref[...].astype(F32)) + r1_ref[...].astype(F32)) + r2_ref[...].astype(F32)

    gs = pltpu.PrefetchScalarGridSpec(
        num_scalar_prefetch=1, grid=(half_rows // tr,),
        in_specs=[pl.BlockSpec((None, tr, cols), lambda i, s_ref: (s_ref[0], i, 0))]
        + [pl.BlockSpec((None, tr, cols), (lambda k: lambda i, s_ref: (k, i, 0))(k)) for k in range(3)],
        out_specs=pl.BlockSpec((None, tr, cols), lambda i, s_ref: (s_ref[1], i, 0)))
    return pl.pallas_call(body, grid_spec=gs, out_shape=SDS((2, half_rows, cols), F32),
                          compiler_params=_cp(1), name="rs_add_chips" + tag)(shard_core, chipsum, recv, recv, recv)


def _rs_join_halves(buf, tag):
    _, half_rows, cols = buf.shape

    def body(b_ref, out_ref, send_sem, recv_sem):
        x, y, c = _coords()
        for r0, nr in _row_chunks(half_rows, D2D_CHUNKS, 8):
            pltpu.make_async_remote_copy(src_ref=out_ref.at[c, pl.ds(r0, nr)], dst_ref=out_ref.at[c, pl.ds(r0, nr)], send_sem=send_sem,
                                         recv_sem=recv_sem, device_id=(x, y, 1 - c), device_id_type=MESH).start()
        pltpu.make_async_remote_copy(src_ref=out_ref.at[c], dst_ref=out_ref.at[c], send_sem=send_sem, recv_sem=recv_sem,
                                     device_id=(x, y, 1 - c), device_id_type=MESH).wait()

    return pl.pallas_call(
        body, in_specs=[ANY], out_specs=ANY, out_shape=SDS(buf.shape, buf.dtype), input_output_aliases={0: 0},
        scratch_shapes=[pltpu.SemaphoreType.DMA, pltpu.SemaphoreType.DMA], name="rs_join_halves" + tag)(buf)


BIG = [("w_in", (1024, 1440), 1), ("w_uq", (256, 768), 1), ("w_ukv", (128, 1024), 1), ("w_out", (1024, 1024), 0),
       ("w_mem_q", (1024, 1024), 0), ("w_mem_kv", (1024, 2048), 1), ("w_mem_o", (1024, 1024), 0),
       ("w_up", (1024, 5632), 1), ("w_down", (2816, 1024), 0)]
SMALL_REP = [("mix_norm_g", 1024), ("b_conv_in", 1024), ("b_conv_dw", 512), ("conv_ln_g", 512), ("conv_ln_b", 512),
             ("q_lat_norm_g", 256), ("kv_lat_norm_g", 128), ("q_norm_g", 96), ("k_norm_g", 96), ("mem_norm_x_g", 1024),
             ("mem_norm_m_g", 1024), ("mem_q_norm_g", 256), ("mem_k_norm_g", 256), ("ffn_norm_g", 1024), ("b_ffn_dw", 5632)]
SMALL_SH = [("w_conv_dw", (31, 512)), ("w_ffn_dw", (3, 5632))]


def _shard_shape(shape, axis):
    return tuple(d // 4 if a == axis else d for a, d in enumerate(shape))


def _pack_rows(parts, rows, cols):
    flat = jnp.concatenate([p.reshape(-1) for p in parts])
    flat = jnp.pad(flat, (0, rows * cols - flat.shape[0]))
    return flat.reshape(rows, cols)


AG_EARLY, AG_LATE = BIG[:3], BIG[3:]
RS_REST, RS_FFN = AG_EARLY, AG_LATE


def _group_rows(group):
    used = sum(_shard_shape(shape, axis)[0] * _shard_shape(shape, axis)[1] // PACK_COLS for _, shape, axis in group)
    return -(-used // 512) * 512


def _pick_rows(n, cap=384):
    return max(r for r in range(16, cap + 1, 16) if n % r == 0)


def _pack_big_shards(ws, group):
    parts = [ws[n].reshape(-1, PACK_COLS) for n, _, _ in group]
    used = sum(p.shape[0] for p in parts)
    pad = _group_rows(group) - used
    return jnp.concatenate(parts + ([jnp.zeros((pad, PACK_COLS), parts[0].dtype)] if pad else []), axis=0)


def _unpack_big_shards(packed, group):
    out, r = {}, 0
    for n, shape, axis in group:
        sh = _shard_shape(shape, axis)
        nr = sh[0] * sh[1] // PACK_COLS
        out[n] = packed[r:r + nr].reshape(sh)
        r += nr
    return out


def _unpack_gathered(g, group):
    out, r = {}, 0
    for n, shape, axis in group:
        sh = _shard_shape(shape, axis)
        nr = sh[0] * sh[1] // PACK_COLS
        part = g[:, r:r + nr]
        if axis == 0:
            out[n] = part.reshape(shape)
        else:
            out[n] = part.reshape((4,) + sh).transpose(1, 0, 2).reshape(shape)
        r += nr
    return out


def _pack_full_grads(gs, group):
    parts = []
    for n, shape, axis in group:
        sh = _shard_shape(shape, axis)
        nr = sh[0] * sh[1] // PACK_COLS
        if axis == 0:
            parts.append(gs[n].reshape(4, nr, PACK_COLS))
        else:
            parts.append(gs[n].reshape(shape[0], 4, sh[1]).transpose(1, 0, 2).reshape(4, nr, PACK_COLS))
    pad = _group_rows(group) - sum(p.shape[1] for p in parts)
    return jnp.concatenate(parts + ([jnp.zeros((4, pad, PACK_COLS), F32)] if pad else []), axis=1)


def _rs_first(gfull, core_idx, tag):
    tr = _pick_rows(gfull.shape[1] // 2)
    return _rs_add_pair(gfull, _rs_swap_halves(gfull, tag), core_idx.reshape(1), tr, tag)


def _rs_last(chipsum, recv, shard_idx, core_idx, tag):
    tr = _pick_rows(chipsum.shape[1])
    red = _rs_add_chips(chipsum, recv, jnp.stack([shard_idx, core_idx]), tr, tag)
    return _rs_join_halves(red, tag).reshape(2 * chipsum.shape[1], chipsum.shape[2])


def _rope_tables(positions):
    inv_freq = ROPE_THETA ** (-jnp.arange(0, ROPE, 2, dtype=F32) / ROPE)
    ang = positions.astype(F32)[:, None] * inv_freq
    cos, sin = jnp.cos(ang), jnp.sin(ang)
    s = positions.shape[0]
    cosf = jnp.concatenate([jnp.ones((s, NOPE), F32), cos, cos, jnp.ones((s, HEAD_PAD - HEAD_DIM), F32)], axis=-1)
    sinf = jnp.concatenate([jnp.zeros((s, NOPE), F32), -sin, sin, jnp.zeros((s, HEAD_PAD - HEAD_DIM), F32)], axis=-1)
    return cosf, sinf


def _pad_heads(w, per_head):
    k = w.shape[0]
    w3 = w.reshape(k, HEADS, per_head)
    return jnp.pad(w3, ((0, 0), (0, 0), (0, HEAD_PAD - per_head))).reshape(k, HEADS * HEAD_PAD)


def _layer_grads(x, mem, positions, target, wf, w_late, sp, shard_idx, core_idx):
    wf = dict(wf)
    s = x.shape[0]
    tm = _row_tile(s, 512)
    tc = _row_tile(s, 256)
    tb = 512 if s % (512 * ATT_GROUP) == 0 else 128
    row2 = lambda a: a.reshape(1, -1)

    w_in = wf["w_in"]
    w_in_pad = jnp.concatenate([w_in[:, :1408], jnp.zeros((D_MODEL, NOPE), BF16), w_in[:, 1408:],
                                jnp.zeros((D_MODEL, HEAD_PAD - HEAD_DIM), BF16)], axis=1)
    w_uq_pad = _pad_heads(wf["w_uq"], HEAD_DIM)
    w_ukv = wf["w_ukv"]
    gq_pad = jnp.pad(sp["q_norm_g"], (0, HEAD_PAD - HEAD_DIM)).reshape(1, HEAD_PAD)
    gk_pad = jnp.pad(sp["k_norm_g"], (0, HEAD_PAD - HEAD_DIM)).reshape(1, HEAD_PAD)
    w_dw32 = jnp.pad(sp["w_conv_dw"], ((0, 1), (0, 0)))
    w_ffn8 = jnp.pad(sp["w_ffn_dw"], ((0, 5), (0, 0)))
    b_ffn = row2(sp["b_ffn_dw"])
    cosf, sinf = _rope_tables(positions)

    z, h1 = _norm_linear(x, 0, D_MODEL, row2(sp["mix_norm_g"]), w_in_pad, F32, tm, IN_COLS_PAD, "in_proj")
    u, u0, u1 = _conv_fwd(z, row2(sp["b_conv_in"]), w_dw32, row2(sp["b_conv_dw"]), row2(sp["conv_ln_g"]), row2(sp["conv_ln_b"]), tc)
    q_raw, cqn = _norm_linear(z, 1024 // Q_RANK, Q_RANK, row2(sp["q_lat_norm_g"]), w_uq_pad, F32, tm, 1024, "q_up")
    kv_raw, ckvn = _norm_linear(z, 1280 // KV_RANK, KV_RANK, row2(sp["kv_lat_norm_g"]), w_ukv, F32, tm, 1024, "kv_up")
    qp, kp, vp = _mla_prep(q_raw, kv_raw, z, cosf, sinf, gq_pad, gk_pad, tc)
    o_f, o_b, lse, gathered = _attn_fwd(qp, kp, vp, tb, w_late)
    wf.update(_unpack_gathered(lax.dynamic_update_index_in_dim(gathered, w_late, shard_idx, 0), AG_LATE))
    w_out_u = wf["w_out"][:CONV_CH]
    w_out_o = jnp.pad(wf["w_out"][CONV_CH:].reshape(HEADS, NOPE, D_MODEL), ((0, 0), (NOPE, 0), (0, 0))).reshape(HEADS * HEAD_PAD, D_MODEL)
    w_up_g, w_up_v = wf["w_up"][:, :D_FF], wf["w_up"][:, D_FF:]
    (x1,) = _linear([(u, w_out_u), (o_b, w_out_o)], False, x, [F32], tm, 1024, "out_proj")

    qm, hq = _norm_linear(x1, 0, D_MODEL, row2(sp["mem_norm_x_g"]), wf["w_mem_q"], F32, tm, 1024, "memq_proj")
    kvm, hm = _norm_linear(mem, 0, D_MODEL, row2(sp["mem_norm_m_g"]), wf["w_mem_kv"], F32, MEM_LEN, 1024, "memkv_proj")
    gmq, gmk = row2(sp["mem_q_norm_g"]), row2(sp["mem_k_norm_g"])
    o_m = _memattn_fwd(qm, kvm, gmq, gmk, tm)
    (x2,) = _linear([(o_m, wf["w_mem_o"])], False, x1, [F32], tm, 1024, "memo_proj")

    up0, h3 = _norm_linear(x2, 0, D_MODEL, row2(sp["ffn_norm_g"]), wf["w_up"], F32, _row_tile(s, 1024), D_FF // 2, "up_proj")
    act = _ffn_fwd(up0, w_ffn8, b_ffn, tc, D_FF // 2)
    dy_f, dy_b, lsum = _down_loss(act, wf["w_down"], x2, target, tm)

    g = {}
    (d_act,) = _linear([(dy_b, wf["w_down"])], True, None, [F32], _row_tile(s, 1024), D_FF // 2, "down_bwd")
    g["w_down"] = _dw(act, dy_b, "dw_down")
    d_up0g, d_up0v, dbg, dbv, dwg, dwv = _ffn_bwd(d_act, up0, w_ffn8, b_ffn, tc, D_FF // 2)
    g["b_ffn_dw"] = jnp.concatenate([dbg, dbv], axis=1).reshape(-1)
    g["w_ffn_dw"] = jnp.concatenate([dwg[:3], dwv[:3]], axis=1)
    g["w_up"] = jnp.concatenate([_dw(h3, d_up0g, "dw_up_g"), _dw(h3, d_up0v, "dw_up_v")], axis=1)
    d_x2f, d_x2b, dg = _linear_normbwd([(d_up0g, w_up_g), (d_up0v, w_up_v)], x2, 0, row2(sp["ffn_norm_g"]), dy_f,
                                       [F32, BF16], tc, "up_bwd")
    g["ffn_norm_g"] = dg.reshape(-1)

    (d_om,) = _linear([(d_x2b, wf["w_mem_o"])], True, None, [BF16], tm, 1024, "memo_bwd")
    g["w_mem_o"] = _dw(o_m, d_x2b, "dw_mem_o")
    d_qm, d_km, d_vm, dgq, dgk = _memattn_bwd(qm, kvm, d_om, gmq, gmk, tm)
    g["mem_q_norm_g"], g["mem_k_norm_g"] = dgq.reshape(-1), dgk.reshape(-1)
    d_kvm = jnp.concatenate([d_km, d_vm], axis=1)
    g["w_mem_q"] = _dw(hq, d_qm, "dw_mem_q")
    g["w_mem_kv"] = _dw(hm, d_kvm, "dw_mem_kv")
    d_x1f, d_x1b, dg = _linear_normbwd([(d_qm, wf["w_mem_q"])], x1, 0, row2(sp["mem_norm_x_g"]), d_x2f, [F32, BF16], tm, "memq_bwd")
    g["mem_norm_x_g"] = dg.reshape(-1)
    _, dg = _linear_normbwd([(d_kvm, wf["w_mem_kv"])], mem, 0, row2(sp["mem_norm_m_g"]), None, [BF16], MEM_LEN, "memkv_bwd")
    g["mem_norm_m_g"] = dg.reshape(-1)

    (d_u,) = _linear([(d_x1b, w_out_u)], True, None, [F32], tm, CONV_CH, "out_bwd_u")
    d_of, d_ob = _linear([(d_x1b, w_out_o)], True, None, [F32, BF16], tm, 1024, "out_bwd_o")
    dw_out_u = _dw(u, d_x1b, "dw_out_u")
    dw_out_o = _dw(o_b, d_x1b, "dw_out_o")
    g["w_out"] = jnp.concatenate([dw_out_u, dw_out_o.reshape(HEADS, HEAD_PAD, D_MODEL)[:, NOPE:].reshape(HEADS * NOPE, D_MODEL)], axis=0)
    chipsum_ffn, wire_ffn = _rs_first(_pack_full_grads(g, RS_FFN), core_idx, "_ffn")
    delta = _attn_delta(d_of, o_f, tb)
    dqp, dkp, dvp, recv_ffn = _attn_bwd(qp, kp, vp, d_ob, lse, delta, tb, wire_ffn)
    g_ffn_packed = _rs_last(chipsum_ffn, recv_ffn, shard_idx, core_idx, "_ffn")
    d_qraw, d_kvraw, d_kr, dgq, dgk = _mla_prep_bwd(dqp, dkp, dvp, q_raw, kv_raw, z, cosf, sinf, gq_pad, gk_pad, tc)
    g["q_norm_g"], g["k_norm_g"] = dgq.reshape(-1)[:HEAD_DIM], dgk.reshape(-1)[:HEAD_DIM]
    g["w_uq"] = _dw(cqn, d_qraw, "dw_uq").reshape(Q_RANK, HEADS, HEAD_PAD)[:, :, :HEAD_DIM].reshape(Q_RANK, HEADS * HEAD_DIM)
    g["w_ukv"] = _dw(ckvn, d_kvraw, "dw_ukv")
    d_cq, dg = _linear_normbwd([(d_qraw, w_uq_pad)], z, 1024 // Q_RANK, row2(sp["q_lat_norm_g"]), None, [BF16], tm, "q_up_bwd")
    g["q_lat_norm_g"] = dg.reshape(-1)
    d_ckv, dg = _linear_normbwd([(d_kvraw, w_ukv)], z, 1280 // KV_RANK, row2(sp["kv_lat_norm_g"]), None, [BF16], tm, "kv_up_bwd")
    g["kv_lat_norm_g"] = dg.reshape(-1)
    d_u1, dlg, dlb, dbdw = _conv_bwd_ln(d_u, u1, row2(sp["conv_ln_g"]), row2(sp["conv_ln_b"]), tc)
    g["conv_ln_g"], g["conv_ln_b"], g["b_conv_dw"] = dlg.reshape(-1), dlb.reshape(-1), dbdw.reshape(-1)
    d_conv, dw_dw, dbin = _conv_bwd_dw(d_u1, u0, z, row2(sp["b_conv_in"]), w_dw32, tc)
    g["w_conv_dw"], g["b_conv_in"] = dw_dw[:CONV_WIDTH], dbin.reshape(-1)
    pieces = [(d_conv, w_in_pad[:, :1024]), (d_cq, w_in_pad[:, 1024:1280]), (d_ckv, w_in_pad[:, 1280:1408]), (d_kr, w_in_pad[:, 1408:])]
    dw_in = [_dw(h1, d, "dw_in_%d" % k) for k, (d, _) in enumerate(pieces)]
    g["w_in"] = jnp.concatenate([dw_in[0], dw_in[1], dw_in[2], dw_in[3][:, NOPE:HEAD_DIM]], axis=1)
    grad_x, dg = _linear_normbwd(pieces, x, 0, row2(sp["mix_norm_g"]), d_x1f, [F32], tm, "in_bwd")
    g["mix_norm_g"] = dg.reshape(-1)
    return lsum[0, 0], grad_x, g, g_ffn_packed


def kernel(x, mem, positions, mix_norm_g, w_in, b_conv_in, w_conv_dw, b_conv_dw, conv_ln_g, conv_ln_b, q_lat_norm_g, w_uq, kv_lat_norm_g, w_ukv, q_norm_g, k_norm_g, w_out, mem_norm_x_g, mem_norm_m_g, w_mem_q, w_mem_kv, mem_q_norm_g, mem_k_norm_g, w_mem_o, ffn_norm_g, w_up, w_ffn_dw, b_ffn_dw, w_down, loss_target, m_mix_norm_g, m_w_in, m_b_conv_in, m_w_conv_dw, m_b_conv_dw, m_conv_ln_g, m_conv_ln_b, m_q_lat_norm_g, m_w_uq, m_kv_lat_norm_g, m_w_ukv, m_q_norm_g, m_k_norm_g, m_w_out, m_mem_norm_x_g, m_mem_norm_m_g, m_w_mem_q, m_w_mem_kv, m_mem_q_norm_g, m_mem_k_norm_g, m_w_mem_o, m_ffn_norm_g, m_w_up, m_w_ffn_dw, m_b_ffn_dw, m_w_down, v_mix_norm_g, v_w_in, v_b_conv_in, v_w_conv_dw, v_b_conv_dw, v_conv_ln_g, v_conv_ln_b, v_q_lat_norm_g, v_w_uq, v_kv_lat_norm_g, v_w_ukv, v_q_norm_g, v_k_norm_g, v_w_out, v_mem_norm_x_g, v_mem_norm_m_g, v_w_mem_q, v_w_mem_kv, v_mem_q_norm_g, v_mem_k_norm_g, v_w_mem_o, v_ffn_norm_g, v_w_up, v_w_ffn_dw, v_b_ffn_dw, v_w_down):
    names = ["mix_norm_g", "w_in", "b_conv_in", "w_conv_dw", "b_conv_dw", "conv_ln_g", "conv_ln_b", "q_lat_norm_g", "w_uq",
             "kv_lat_norm_g", "w_ukv", "q_norm_g", "k_norm_g", "w_out", "mem_norm_x_g", "mem_norm_m_g", "w_mem_q", "w_mem_kv",
             "mem_q_norm_g", "mem_k_norm_g", "w_mem_o", "ffn_norm_g", "w_up", "w_ffn_dw", "b_ffn_dw", "w_down"]
    loc = locals()
    w = {n: loc[n] for n in names}
    m = {n: loc["m_" + n] for n in names}
    v = {n: loc["v_" + n] for n in names}
    shard_idx = 2 * lax.axis_index("x") + lax.axis_index("y")

    shard_idx = shard_idx.astype(jnp.int32)
    core_idx = lax.axis_index("c").astype(jnp.int32)

    w_local = {n: w[n][0] for n, _, _ in BIG}
    w_early = _pack_big_shards(w_local, AG_EARLY).astype(BF16)
    w_late = _pack_big_shards(w_local, AG_LATE).astype(BF16)
    wf = _unpack_gathered(lax.dynamic_update_index_in_dim(_ag_weights(w_early), w_early, shard_idx, 0), AG_EARLY)

    small_sh_full = {}
    gather_in = []
    for n, (r, c) in SMALL_SH:
        csh = c // 4
        slab = lax.dynamic_update_slice(jnp.zeros((r, c), F32), w[n][0], (0, shard_idx * csh))
        gather_in.append(slab.reshape(-1))
    gather_rows = 256
    gathered_small = _allreduce_small_named(_pack_rows(gather_in, gather_rows, SMALL_COLS), "gather_small") * 0.5
    off = 0
    for n, (r, c) in SMALL_SH:
        small_sh_full[n] = gathered_small.reshape(-1)[off:off + r * c].reshape(r, c)
        off += r * c
    sp = {n: w[n][0] for n, _ in SMALL_REP}
    sp.update(small_sh_full)

    lsum, grad_x, g, g_ffn_packed = _layer_grads(x[0], mem[0], positions[0], loss_target[0], wf, w_late, sp, shard_idx, core_idx)

    small_parts = [jnp.full((SMALL_COLS,), lsum, F32)] + [g[n] for n, _ in SMALL_REP] + [g[n] for n, _ in SMALL_SH]
    small_rows = 368
    small_sum = _allreduce_small_named(_pack_rows(small_parts, small_rows, SMALL_COLS), "allreduce_small").reshape(-1)
    loss = small_sum[0] * (0.5 / D_MODEL)
    gs = {}
    off = SMALL_COLS
    for n, sz in SMALL_REP:
        gs[n] = small_sum[off:off + sz].reshape(w[n].shape)
        off += sz
    for n, (r, c) in SMALL_SH:
        full = small_sum[off:off + r * c].reshape(r, c)
        gs[n] = lax.dynamic_slice(full, (0, shard_idx * (c // 4)), (r, c // 4)).reshape(w[n].shape)
        off += r * c

    chipsum, chipsum_wire = _rs_first(_pack_full_grads(g, RS_REST), core_idx, "_rest")
    g_rest_packed = _rs_last(chipsum, _rs_to_owner(chipsum_wire), shard_idx, core_idx, "_rest")
    g_big = {**_unpack_big_shards(g_rest_packed, RS_REST), **_unpack_big_shards(g_ffn_packed, RS_FFN)}
    gs.update({n: a[None] for n, a in g_big.items()})

    delta, new_m, new_v = {}, {}, {}
    for n, _, _ in BIG:
        d_n, m_n, v_n = _adamw(w[n][0], g_big[n], m[n][0], v[n][0], "adamw_" + n)
        delta[n], new_m[n], new_v[n] = d_n[None], m_n[None], v_n[None]
    small_names = [n for n, _ in SMALL_REP] + [n for n, _ in SMALL_SH]
    as2d = lambda a: a.reshape(-1, a.shape[-1])
    d_s, m_s, v_s = _adamw_small(*[[as2d(d[n]) for n in small_names] for d in (w, gs, m, v)])
    for k, n in enumerate(small_names):
        delta[n], new_m[n], new_v[n] = d_s[k].reshape(w[n].shape), m_s[k].reshape(w[n].shape), v_s[k].reshape(w[n].shape)

    return (loss, grad_x[None], *[gs[n] for n in names], *[delta[n] for n in names], *[new_m[n] for n in names],
            *[new_v[n] for n in names])


def _allreduce_small_named(v, name):
    rows, cols = v.shape

    def body(v_ref, out_ref, buf, send_sems, recv_sems):
        x, y, c = _coords()
        me = 4 * x + 2 * y + c
        buf[me] = v_ref[...]
        cps = []
        for r in range(1, 8):
            dx, dy, dc = (r >> 2) & 1, (r >> 1) & 1, r & 1
            to = (x + dx - 2 * x * dx, y + dy - 2 * y * dy, c + dc - 2 * c * dc)
            cp = pltpu.make_async_remote_copy(src_ref=v_ref, dst_ref=buf.at[me], send_sem=send_sems.at[r - 1],
                                              recv_sem=recv_sems.at[r - 1], device_id=to, device_id_type=MESH)
            cp.start()
            cps.append(cp)
        for cp in cps:
            cp.wait()
        acc = buf[0]
        for d in range(1, 8):
            acc = acc + buf[d]
        out_ref[...] = acc

    vm = pl.BlockSpec(memory_space=pltpu.VMEM)
    return pl.pallas_call(
        body, in_specs=[vm], out_specs=vm, out_shape=SDS((rows, cols), F32),
        scratch_shapes=[pltpu.VMEM((8, rows, cols), F32), pltpu.SemaphoreType.DMA((7,)), pltpu.SemaphoreType.DMA((7,))],
        name=name)(v)
```

```python
import math

import numpy as np
import jax
import jax.numpy as jnp
from jax import lax
from jax.experimental import pallas as pl
from jax.experimental.pallas import tpu as pltpu

F32 = jnp.float32
BF16 = jnp.bfloat16
SDS = jax.ShapeDtypeStruct
MESH = pl.DeviceIdType.MESH

D_MODEL = 1024
EPS = 1e-6
CONV_CH = 512
CONV_WIDTH = 31
CONV_HALO = 32
HEADS = 8
NOPE = 64
ROPE = 32
HEAD_DIM = NOPE + ROPE
HEAD_PAD = 128
Q_RANK = 256
KV_RANK = 128
CHUNK = 64
ROPE_THETA = 10000.0
IN_COLS_PAD = 1536
MEM_HEADS = 4
MEM_HEAD_DIM = 256
MEM_LEN = 256
D_FF = 2816
FFN_HALO = 8
ATT_SCALE = 1.0 / math.sqrt(HEAD_DIM)
LOG2E = math.log2(math.e)
LN2 = math.log(2.0)

ADAM_LR = 0.001
ADAM_B1 = 0.9
ADAM_B2 = 0.999
ADAM_EPS = 1e-08
ADAM_WD = 0.01
ADAM_STEP = 10

VMEM_LIMIT_V7X = 56 * 1024 * 1024
PACK_COLS = 1024
SMALL_COLS = 128


def _cp(n_axes):
    return pltpu.CompilerParams(dimension_semantics=("arbitrary",) * n_axes, vmem_limit_bytes=VMEM_LIMIT_V7X)


def _row_tile(s, want):
    return want if s % want == 0 else s


def _norm_linear(x, xcol, kdim, g, w, out_dtype, tm, tn, name):
    s = x.shape[0]
    n = w.shape[1]

    def body(x_ref, g_ref, w_ref, y_ref, hn_ref):
        @pl.when(pl.program_id(1) == 0)
        def _():
            xv = x_ref[...]
            r = lax.rsqrt(jnp.mean(xv * xv, axis=-1, keepdims=True) + EPS)
            hn_ref[...] = ((xv * r) * g_ref[...]).astype(BF16)

        y_ref[...] = jnp.dot(hn_ref[...], w_ref[...], preferred_element_type=F32).astype(y_ref.dtype)

    return pl.pallas_call(
        body, grid=(s // tm, n // tn),
        in_specs=[pl.BlockSpec((tm, kdim), lambda i, j: (i, xcol)), pl.BlockSpec((1, kdim), lambda i, j: (0, 0)),
                  pl.BlockSpec((kdim, tn), lambda i, j: (0, j))],
        out_specs=[pl.BlockSpec((tm, tn), lambda i, j: (i, j)), pl.BlockSpec((tm, kdim), lambda i, j: (i, 0))],
        out_shape=[SDS((s, n), out_dtype), SDS((s, kdim), BF16)],
        compiler_params=_cp(2), name=name)(x, g, w)


def _linear(pairs, nt, residual, out_dtypes, tm, tn, name):
    s = pairs[0][0].shape[0]
    n = pairs[0][1].shape[0] if nt else pairs[0][1].shape[1]
    n_pairs = len(pairs)
    has_res = residual is not None

    def body(*refs):
        a_refs = refs[:n_pairs]
        w_refs = refs[n_pairs:2 * n_pairs]
        res_ref = refs[2 * n_pairs] if has_res else None
        outs = refs[2 * n_pairs + int(has_res):]
        acc = None
        for a_ref, w_ref in zip(a_refs, w_refs):
            a = a_ref[...].astype(BF16)
            if nt:
                d = lax.dot_general(a, w_ref[...], (((1,), (1,)), ((), ())), preferred_element_type=F32)
            else:
                d = jnp.dot(a, w_ref[...], preferred_element_type=F32)
            acc = d if acc is None else acc + d
        if has_res:
            acc = res_ref[...] + acc
        for o in outs:
            o[...] = acc.astype(o.dtype)

    in_specs = [pl.BlockSpec((tm, a.shape[1]), lambda i, j: (i, 0)) for a, _ in pairs]
    if nt:
        in_specs += [pl.BlockSpec((tn, w.shape[1]), lambda i, j: (j, 0)) for _, w in pairs]
    else:
        in_specs += [pl.BlockSpec((w.shape[0], tn), lambda i, j: (0, j)) for _, w in pairs]
    args = [a for a, _ in pairs] + [w for _, w in pairs]
    if has_res:
        in_specs.append(pl.BlockSpec((tm, tn), lambda i, j: (i, j)))
        args.append(residual)
    outs = pl.pallas_call(
        body, grid=(s // tm, n // tn), in_specs=in_specs,
        out_specs=[pl.BlockSpec((tm, tn), lambda i, j: (i, j)) for _ in out_dtypes],
        out_shape=[SDS((s, n), dt) for dt in out_dtypes],
        compiler_params=_cp(2), name=name)(*args)
    return outs


def _linear_normbwd(pairs, x, xcol, g, d_res, out_dtypes, tm, name):
    s = pairs[0][0].shape[0]
    dn = pairs[0][1].shape[0]
    n_pairs = len(pairs)
    has_res = d_res is not None

    def body(*refs):
        a_refs = refs[:n_pairs]
        w_refs = refs[n_pairs:2 * n_pairs]
        x_ref, g_ref = refs[2 * n_pairs], refs[2 * n_pairs + 1]
        k = 2 * n_pairs + 2
        res_ref = refs[k] if has_res else None
        k += int(has_res)
        outs = refs[k:-1]
        dg_ref = refs[-1]
        dh = None
        for a_ref, w_ref in zip(a_refs, w_refs):
            d = lax.dot_general(a_ref[...].astype(BF16), w_ref[...], (((1,), (1,)), ((), ())), preferred_element_type=F32)
            dh = d if dh is None else dh + d
        xv = x_ref[...]
        r = lax.rsqrt(jnp.mean(xv * xv, axis=-1, keepdims=True) + EPS)
        y = xv * r

        @pl.when(pl.program_id(0) == 0)
        def _():
            dg_ref[...] = jnp.zeros_like(dg_ref)

        dg_ref[...] += jnp.sum(dh * y, axis=0, keepdims=True)
        dy = dh * g_ref[...]
        dx = r * (dy - y * jnp.mean(dy * y, axis=-1, keepdims=True))
        if has_res:
            dx = res_ref[...] + dx
        for o in outs:
            o[...] = dx.astype(o.dtype)

    in_specs = [pl.BlockSpec((tm, a.shape[1]), lambda i: (i, 0)) for a, _ in pairs]
    in_specs += [pl.BlockSpec((dn, w.shape[1]), lambda i: (0, 0)) for _, w in pairs]
    in_specs += [pl.BlockSpec((tm, dn), lambda i: (i, xcol)), pl.BlockSpec((1, dn), lambda i: (0, 0))]
    args = [a for a, _ in pairs] + [w for _, w in pairs] + [x, g]
    if has_res:
        in_specs.append(pl.BlockSpec((tm, dn), lambda i: (i, 0)))
        args.append(d_res)
    outs = pl.pallas_call(
        body, grid=(s // tm,), in_specs=in_specs,
        out_specs=[pl.BlockSpec((tm, dn), lambda i: (i, 0)) for _ in out_dtypes] + [pl.BlockSpec((1, dn), lambda i: (0, 0))],
        out_shape=[SDS((s, dn), dt) for dt in out_dtypes] + [SDS((1, dn), F32)],
        compiler_params=_cp(1), name=name)(*args)
    return outs


def _dw_matmul(a, b, tk, tn, ts, name):
    s, ka = a.shape
    n = b.shape[1]

    def body(a_ref, b_ref, o_ref):
        @pl.when(pl.program_id(2) == 0)
        def _():
            o_ref[...] = jnp.zeros_like(o_ref)

        o_ref[...] += lax.dot_general(a_ref[...].astype(BF16), b_ref[...].astype(BF16), (((0,), (0,)), ((), ())),
                                      preferred_element_type=F32)

    return pl.pallas_call(
        body, grid=(ka // tk, n // tn, s // ts),
        in_specs=[pl.BlockSpec((ts, tk), lambda k, j, t: (t, k)), pl.BlockSpec((ts, tn), lambda k, j, t: (t, j))],
        out_specs=pl.BlockSpec((tk, tn), lambda k, j, t: (k, j)),
        out_shape=SDS((ka, n), F32), compiler_params=_cp(3), name=name)(a, b)


def _dw(a, b, name):
    s, ka = a.shape
    n = b.shape[1]
    tk = ka if ka <= 1024 else ka // 2
    tn = n if n <= 1024 else (n // 2 if n == D_FF else 512)
    return _dw_matmul(a, b, tk, tn, _row_tile(s, 2048), name)


def _prev_halo(tm, halo):
    return lambda i: (jnp.maximum(i * (tm // halo) - 1, 0), 0)


def _next_halo(tm, halo, s):
    return lambda i: (jnp.minimum((i + 1) * (tm // halo), s // halo - 1), 0)


def _shifted_copies(ext, tm):
    n = tm + CONV_HALO - 8
    for s in range(1, 8):
        ext[s, 0:n, :] = ext[0, s:s + n, :]


def _sum_taps(terms, ways=4):
    accs = []
    for i, t in enumerate(terms):
        if i < ways:
            accs.append(t)
        else:
            accs[i % ways] = accs[i % ways] + t
    while len(accs) > 1:
        accs = [accs[i] + accs[i + 1] if i + 1 < len(accs) else accs[i] for i in range(0, len(accs), 2)]
    return accs[0]


def _tap_rows(ext, o, n, cs):
    return ext[o % 8, o - o % 8:o - o % 8 + n, cs]


def _conv_fwd(z, b_in, w32, b_dw, ln_g, ln_b, tm):
    s = z.shape[0]
    c = CONV_CH

    def body(z_ref, zh_ref, bin_ref, w_ref, bdw_ref, lg_ref, lb_ref, u_ref, u0_ref, u1_ref, ext):
        i = pl.program_id(0)

        def glu(zz):
            zz = zz + bin_ref[...]
            return zz[:, :c] * jax.nn.sigmoid(zz[:, c:])

        u0 = glu(z_ref[...])
        u0_ref[...] = u0
        ext[0, 0:CONV_HALO, :] = jnp.where(i > 0, glu(zh_ref[...]), 0.0)
        ext[0, CONV_HALO:, :] = u0
        _shifted_copies(ext, tm)
        off = CONV_HALO - (CONV_WIDTH - 1)
        for r in range(tm // 64):
            for cb in range(c // 128):
                cs = slice(cb * 128, (cb + 1) * 128)
                u1_ref[r * 64:(r + 1) * 64, cs] = _sum_taps(
                    _tap_rows(ext, r * 64 + off + k, 64, cs) * w_ref[k:k + 1, cs] for k in range(CONV_WIDTH)) + bdw_ref[:, cs]
        u1 = u1_ref[...]
        mu = jnp.mean(u1, axis=-1, keepdims=True)
        xc = u1 - mu
        y = xc * lax.rsqrt(jnp.mean(xc * xc, axis=-1, keepdims=True) + EPS)
        y = y * lg_ref[...] + lb_ref[...]
        u_ref[...] = (y * jax.nn.sigmoid(y)).astype(BF16)

    row = lambda i: (i, 0)
    fix = lambda i: (0, 0)
    return pl.pallas_call(
        body, grid=(s // tm,),
        in_specs=[pl.BlockSpec((tm, 2 * c), row), pl.BlockSpec((CONV_HALO, 2 * c), _prev_halo(tm, CONV_HALO)),
                  pl.BlockSpec((1, 2 * c), fix), pl.BlockSpec((32, c), fix), pl.BlockSpec((1, c), fix),
                  pl.BlockSpec((1, c), fix), pl.BlockSpec((1, c), fix)],
        out_specs=[pl.BlockSpec((tm, c), row)] * 3,
        out_shape=[SDS((s, c), BF16), SDS((s, c), F32), SDS((s, c), F32)],
        scratch_shapes=[pltpu.VMEM((8, tm + CONV_HALO, c), F32)],
        compiler_params=_cp(1), name="conv_fwd")(z, z, b_in, w32, b_dw, ln_g, ln_b)


def _conv_bwd_ln(d_u, u1, ln_g, ln_b, tm):
    s = d_u.shape[0]
    c = CONV_CH

    def body(du_ref, u1_ref, lg_ref, lb_ref, du1_ref, dlg_ref, dlb_ref, dbdw_ref):
        @pl.when(pl.program_id(0) == 0)
        def _():
            dlg_ref[...] = jnp.zeros_like(dlg_ref)
            dlb_ref[...] = jnp.zeros_like(dlb_ref)
            dbdw_ref[...] = jnp.zeros_like(dbdw_ref)

        u1 = u1_ref[...]
        mu = jnp.mean(u1, axis=-1, keepdims=True)
        xc = u1 - mu
        rs = lax.rsqrt(jnp.mean(xc * xc, axis=-1, keepdims=True) + EPS)
        xh = xc * rs
        y = xh * lg_ref[...] + lb_ref[...]
        sg = jax.nn.sigmoid(y)
        dy = du_ref[...] * (sg * (1.0 + y * (1.0 - sg)))
        dlg_ref[...] += jnp.sum(dy * xh, axis=0, keepdims=True)
        dlb_ref[...] += jnp.sum(dy, axis=0, keepdims=True)
        dxh = dy * lg_ref[...]
        du1 = rs * (dxh - jnp.mean(dxh, axis=-1, keepdims=True) - xh * jnp.mean(dxh * xh, axis=-1, keepdims=True))
        dbdw_ref[...] += jnp.sum(du1, axis=0, keepdims=True)
        du1_ref[...] = du1

    row = lambda i: (i, 0)
    fix = lambda i: (0, 0)
    return pl.pallas_call(
        body, grid=(s // tm,),
        in_specs=[pl.BlockSpec((tm, c), row), pl.BlockSpec((tm, c), row), pl.BlockSpec((1, c), fix), pl.BlockSpec((1, c), fix)],
        out_specs=[pl.BlockSpec((tm, c), row)] + [pl.BlockSpec((1, c), fix)] * 3,
        out_shape=[SDS((s, c), F32)] + [SDS((1, c), F32)] * 3,
        compiler_params=_cp(1), name="conv_bwd_ln")(d_u, u1, ln_g, ln_b)


def _conv_bwd_dw(d_u1, u0, z, b_in, w32, tm):
    s = d_u1.shape[0]
    c = CONV_CH

    def body(d_ref, dn_ref, u0_ref, u0p_ref, z_ref, bin_ref, w_ref, dz_ref, dw_ref, dbin_ref, extd, extu, du0):
        i = pl.program_id(0)
        last = pl.num_programs(0) - 1

        @pl.when(i == 0)
        def _():
            dw_ref[...] = jnp.zeros_like(dw_ref)
            dbin_ref[...] = jnp.zeros_like(dbin_ref)

        extd[0, 0:tm, :] = d_ref[...]
        extd[0, tm:, :] = jnp.where(i < last, dn_ref[...], 0.0)
        extu[0, 0:CONV_HALO, :] = jnp.where(i > 0, u0p_ref[...], 0.0)
        extu[0, CONV_HALO:, :] = u0_ref[...]
        _shifted_copies(extd, tm)
        _shifted_copies(extu, tm)
        off = CONV_HALO - (CONV_WIDTH - 1)
        for r in range(tm // 64):
            for cb in range(c // 128):
                cs = slice(cb * 128, (cb + 1) * 128)
                du0[r * 64:(r + 1) * 64, cs] = _sum_taps(
                    _tap_rows(extd, r * 64 + (CONV_WIDTH - 1) - k, 64, cs) * w_ref[k:k + 1, cs] for k in range(CONV_WIDTH))
        for cb in range(c // 128):
            cs = slice(cb * 128, (cb + 1) * 128)
            accs = [None] * CONV_WIDTH
            for r in range(tm // 64):
                d = d_ref[r * 64:(r + 1) * 64, cs]
                for k in range(CONV_WIDTH):
                    p = d * _tap_rows(extu, r * 64 + off + k, 64, cs)
                    part = _sum_taps(p[q * 8:(q + 1) * 8, :] for q in range(8))
                    accs[k] = part if accs[k] is None else accs[k] + part
            for k in range(CONV_WIDTH):
                dw_ref[k:k + 1, cs] += jnp.sum(accs[k], axis=0, keepdims=True)
        zz = z_ref[...] + bin_ref[...]
        a = zz[:, :c]
        sg = jax.nn.sigmoid(zz[:, c:])
        d0 = du0[...]
        da = d0 * sg
        dgt = d0 * a * (sg * (1.0 - sg))
        dbin_ref[:, :c] += jnp.sum(da, axis=0, keepdims=True)
        dbin_ref[:, c:] += jnp.sum(dgt, axis=0, keepdims=True)
        dz_ref[:, :c] = da.astype(BF16)
        dz_ref[:, c:] = dgt.astype(BF16)

    row = lambda i: (i, 0)
    fix = lambda i: (0, 0)
    return pl.pallas_call(
        body, grid=(s // tm,),
        in_specs=[pl.BlockSpec((tm, c), row), pl.BlockSpec((CONV_HALO, c), _next_halo(tm, CONV_HALO, s)),
                  pl.BlockSpec((tm, c), row), pl.BlockSpec((CONV_HALO, c), _prev_halo(tm, CONV_HALO)),
                  pl.BlockSpec((tm, 2 * c), row), pl.BlockSpec((1, 2 * c), fix), pl.BlockSpec((32, c), fix)],
        out_specs=[pl.BlockSpec((tm, 2 * c), row), pl.BlockSpec((32, c), fix), pl.BlockSpec((1, 2 * c), fix)],
        out_shape=[SDS((s, 2 * c), BF16), SDS((32, c), F32), SDS((1, 2 * c), F32)],
        scratch_shapes=[pltpu.VMEM((8, tm + CONV_HALO, c), F32), pltpu.VMEM((8, tm + CONV_HALO, c), F32), pltpu.VMEM((tm, c), F32)],
        compiler_params=_cp(1), name="conv_bwd_dw")(d_u1, d_u1, u0, u0, z, b_in, w32)


def _partner(v, lane):
    up = pltpu.roll(v, HEAD_PAD - ROPE // 2, 1)
    dn = pltpu.roll(v, ROPE // 2, 1)
    lo = (lane >= NOPE) & (lane < NOPE + ROPE // 2)
    hi = (lane >= NOPE + ROPE // 2) & (lane < HEAD_DIM)
    return jnp.where(lo, up, jnp.where(hi, dn, 0.0))


def _mla_prep(q_raw, kv_raw, z, cosf, sinf, gq, gk, tm):
    s = q_raw.shape[0]

    def body(q_ref, kv_ref, kr_ref, c_ref, s_ref, gq_ref, gk_ref, qo_ref, ko_ref, vo_ref):
        lane = lax.broadcasted_iota(jnp.int32, (tm, HEAD_PAD), 1)
        cf = c_ref[...]
        sf = s_ref[...]

        def norm_rope(t, g_ref):
            r = lax.rsqrt(jnp.sum(t * t, axis=-1, keepdims=True) * (1.0 / HEAD_DIM) + EPS)
            tn = (t * r) * g_ref[...]
            return tn * cf + _partner(tn, lane) * sf

        kr = kr_ref[...]
        for h in range(HEADS):
            hs = slice(h * HEAD_PAD, (h + 1) * HEAD_PAD)
            qo_ref[:, hs] = (norm_rope(q_ref[:, hs], gq_ref) * (ATT_SCALE * LOG2E)).astype(BF16)
            kv = kv_ref[:, hs]
            ko_ref[:, hs] = norm_rope(jnp.where(lane < NOPE, kv, 0.0) + kr, gk_ref).astype(BF16)
            vo_ref[:, hs] = jnp.where(lane >= NOPE, kv, 0.0).astype(BF16)

    row = lambda i: (i, 0)
    wide = pl.BlockSpec((tm, HEADS * HEAD_PAD), row)
    one = pl.BlockSpec((tm, HEAD_PAD), row)
    return pl.pallas_call(
        body, grid=(s // tm,),
        in_specs=[wide, wide, pl.BlockSpec((tm, HEAD_PAD), lambda i: (i, IN_COLS_PAD // HEAD_PAD - 1)), one, one,
                  pl.BlockSpec((1, HEAD_PAD), lambda i: (0, 0)), pl.BlockSpec((1, HEAD_PAD), lambda i: (0, 0))],
        out_specs=[wide] * 3,
        out_shape=[SDS((s, HEADS * HEAD_PAD), BF16)] * 3,
        compiler_params=_cp(1), name="mla_prep")(q_raw, kv_raw, z, cosf, sinf, gq, gk)


def _mla_prep_bwd(dqp, dkp, dvp, q_raw, kv_raw, z, cosf, sinf, gq, gk, tm):
    s = q_raw.shape[0]

    def body(dq_ref, dk_ref, dv_ref, q_ref, kv_ref, kr_ref, c_ref, s_ref, gq_ref, gk_ref,
             dqo_ref, dkvo_ref, dkr_ref, dgq_ref, dgk_ref):
        lane = lax.broadcasted_iota(jnp.int32, (tm, HEAD_PAD), 1)
        cf = c_ref[...]
        sf = s_ref[...]

        @pl.when(pl.program_id(0) == 0)
        def _():
            dgq_ref[...] = jnp.zeros_like(dgq_ref)
            dgk_ref[...] = jnp.zeros_like(dgk_ref)

        def norm_rope_bwd(t, d_out, g_ref, dg_ref):
            r = lax.rsqrt(jnp.sum(t * t, axis=-1, keepdims=True) * (1.0 / HEAD_DIM) + EPS)
            th = t * r
            dn = d_out * cf + _partner(d_out * sf, lane)
            dg_ref[...] += jnp.sum(dn * th, axis=0, keepdims=True)
            dh = dn * g_ref[...]
            return r * (dh - th * (jnp.sum(dh * th, axis=-1, keepdims=True) * (1.0 / HEAD_DIM)))

        kr = kr_ref[...]
        dkr = None
        for h in range(HEADS):
            hs = slice(h * HEAD_PAD, (h + 1) * HEAD_PAD)
            dq = norm_rope_bwd(q_ref[:, hs], dq_ref[:, hs] * ATT_SCALE, gq_ref, dgq_ref)
            dqo_ref[:, hs] = dq.astype(BF16)
            kv = kv_ref[:, hs]
            dkpre = norm_rope_bwd(jnp.where(lane < NOPE, kv, 0.0) + kr, dk_ref[:, hs] * LN2, gk_ref, dgk_ref)
            dkvo_ref[:, hs] = jnp.where(lane < NOPE, dkpre, dv_ref[:, hs]).astype(BF16)
            dkr_h = jnp.where((lane >= NOPE) & (lane < HEAD_DIM), dkpre, 0.0)
            dkr = dkr_h if dkr is None else dkr + dkr_h
        dkr_ref[...] = dkr

    row = lambda i: (i, 0)
    fix = lambda i: (0, 0)
    wide = pl.BlockSpec((tm, HEADS * HEAD_PAD), row)
    one = pl.BlockSpec((tm, HEAD_PAD), row)
    return pl.pallas_call(
        body, grid=(s // tm,),
        in_specs=[wide, wide, wide, wide, wide, pl.BlockSpec((tm, HEAD_PAD), lambda i: (i, IN_COLS_PAD // HEAD_PAD - 1)),
                  one, one, pl.BlockSpec((1, HEAD_PAD), fix), pl.BlockSpec((1, HEAD_PAD), fix)],
        out_specs=[wide, wide, one, pl.BlockSpec((1, HEAD_PAD), fix), pl.BlockSpec((1, HEAD_PAD), fix)],
        out_shape=[SDS((s, HEADS * HEAD_PAD), BF16), SDS((s, HEADS * HEAD_PAD), BF16), SDS((s, HEAD_PAD), F32),
                   SDS((1, HEAD_PAD), F32), SDS((1, HEAD_PAD), F32)],
        compiler_params=_cp(1), name="mla_prep_bwd")(dqp, dkp, dvp, q_raw, kv_raw, z, cosf, sinf, gq, gk)


ATT_GROUP = 8


def _group_schedule(nb, forward):
    g = ATT_GROUP
    one, two, case = [], [], []
    for a in range(nb):
        for b in (range(a // g + 1) if forward else range(a // g, nb // g)):
            one.append(a)
            two.append(b)
            case.append(0 if b != a // g else 1 + a % g)
    return tuple(jnp.asarray(np.array(x, np.int32)) for x in (one, two, case))


STRIP = 64


def _fold8(x):
    acc = x[0:8, :]
    for g in range(1, x.shape[0] // 8):
        acc = acc + x[g * 8:(g + 1) * 8, :]
    return acc


def _attn_fwd(qp, kp, vp, tb, w_late):
    s = qp.shape[0]
    ii, jj, cc = _group_schedule(s // tb, True)
    n_steps = int(ii.shape[0])

    def body(ii_ref, jj_ref, cc_ref, q_ref, k_ref, v_ref, wl_ref, of_ref, ob_ref, lse_ref, gl_ref,
             m_sc, l_sc, acc_sc, st_sc, pt_sc, send_sems, recv_sems):
        t = pl.program_id(1)
        case = cc_ref[t]

        @pl.when((pl.program_id(0) == 0) & (t == 0))
        def _():
            _ag_send(wl_ref, gl_ref, send_sems, recv_sems)

        @pl.when(jj_ref[t] == 0)
        def _():
            m_sc[...] = jnp.full_like(m_sc, -jnp.inf)
            l_sc[...] = jnp.zeros_like(l_sc)
            acc_sc[...] = jnp.zeros_like(acc_sc)

        def step(n_keys, diag_at):
            def visible(r):
                if diag_at is None or r * STRIP <= diag_at:
                    return None
                col = lax.broadcasted_iota(jnp.int32, (STRIP, tb), 1)
                return col >= r * STRIP - diag_at

            st_sc[0:n_keys, :] = lax.dot_general(k_ref[0:n_keys, :], q_ref[...], (((1,), (1,)), ((), ())), preferred_element_type=F32)
            mx = None
            for r in range(n_keys // STRIP):
                sc = st_sc[r * STRIP:(r + 1) * STRIP, :]
                if visible(r) is not None:
                    sc = jnp.where(visible(r), sc, -jnp.inf)
                m8 = sc[0:8, :]
                for g in range(1, STRIP // 8):
                    m8 = jnp.maximum(m8, sc[g * 8:(g + 1) * 8, :])
                mx = m8 if mx is None else jnp.maximum(mx, m8)
            m_old = m_sc[0:1, :]
            m_new = jnp.maximum(m_old, jnp.max(mx, axis=0, keepdims=True))
            alpha = jnp.exp2(m_old - m_new)
            ps = None
            pv = None
            for b in range(n_keys // tb):
                for r in range(b * tb // STRIP, (b + 1) * tb // STRIP):
                    p = jnp.exp2(st_sc[r * STRIP:(r + 1) * STRIP, :] - m_new)
                    if visible(r) is not None:
                        p = jnp.where(visible(r), p, 0.0)
                    ps = _fold8(p) if ps is None else ps + _fold8(p)
                    pt_sc[r * STRIP:(r + 1) * STRIP, :] = p.astype(BF16)
                pvb = lax.dot_general(v_ref[b * tb:(b + 1) * tb, :], pt_sc[b * tb:(b + 1) * tb, :], (((0,), (0,)), ((), ())),
                                      preferred_element_type=F32)
                pv = pvb if pv is None else pv + pvb
            l_new = alpha * l_sc[0:1, :] + jnp.sum(ps, axis=0, keepdims=True)
            m_sc[...] = jnp.broadcast_to(m_new, m_sc.shape)
            l_sc[...] = jnp.broadcast_to(l_new, l_sc.shape)
            acc_sc[...] = alpha * acc_sc[...] + pv

        @pl.when(case == 0)
        def _():
            step(ATT_GROUP * tb, None)

        for d in range(ATT_GROUP):
            @pl.when(case == 1 + d)
            def _(d=d):
                step((d + 1) * tb, d * tb)

        @pl.when(case != 0)
        def _():
            l = l_sc[0:1, :]
            o = (acc_sc[...] / l).T
            of_ref[...] = o
            ob_ref[...] = o.astype(BF16)
            lse_ref[...] = m_sc[0:1, :] + jnp.log(l) * LOG2E

        @pl.when((pl.program_id(0) == HEADS - 1) & (t == n_steps - 1))
        def _():
            _ag_finish(wl_ref, gl_ref, send_sems, recv_sems)

    qmap = lambda h, t, ii_ref, jj_ref, cc_ref: (ii_ref[t], h)
    kmap = lambda h, t, ii_ref, jj_ref, cc_ref: (jj_ref[t], h)
    gs = pltpu.PrefetchScalarGridSpec(
        num_scalar_prefetch=3, grid=(HEADS, n_steps),
        in_specs=[pl.BlockSpec((tb, HEAD_PAD), qmap), pl.BlockSpec((ATT_GROUP * tb, HEAD_PAD), kmap),
                  pl.BlockSpec((ATT_GROUP * tb, HEAD_PAD), kmap), ANY],
        out_specs=[pl.BlockSpec((tb, HEAD_PAD), qmap), pl.BlockSpec((tb, HEAD_PAD), qmap),
                   pl.BlockSpec((None, 1, tb), lambda h, t, ii_ref, jj_ref, cc_ref: (h, 0, ii_ref[t])), ANY],
        scratch_shapes=[pltpu.VMEM((8, tb), F32), pltpu.VMEM((8, tb), F32), pltpu.VMEM((HEAD_PAD, tb), F32),
                        pltpu.VMEM((ATT_GROUP * tb, tb), F32), pltpu.VMEM((ATT_GROUP * tb, tb), BF16),
                        pltpu.SemaphoreType.DMA((6,)), pltpu.SemaphoreType.DMA((6,))])
    w = HEADS * HEAD_PAD
    return pl.pallas_call(
        body, grid_spec=gs,
        out_shape=[SDS((s, w), F32), SDS((s, w), BF16), SDS((HEADS, 1, s), F32), SDS((4,) + w_late.shape, w_late.dtype)],
        compiler_params=_cp(2), name="attn_fwd")(ii, jj, cc, qp, kp, vp, w_late)


def _attn_delta(do, o, tb):
    s = do.shape[0]

    def body(do_ref, o_ref, d_ref):
        for h in range(HEADS):
            hs = slice(h * HEAD_PAD, (h + 1) * HEAD_PAD)
            d_ref[h] = jnp.sum((do_ref[:, hs] * o_ref[:, hs]).T, axis=0, keepdims=True)

    blk = pl.BlockSpec((tb, HEADS * HEAD_PAD), lambda i: (i, 0))
    return pl.pallas_call(body, grid=(s // tb,), in_specs=[blk, blk],
                          out_specs=pl.BlockSpec((HEADS, 1, tb), lambda i: (0, 0, i)),
                          out_shape=SDS((HEADS, 1, s), F32), compiler_params=_cp(1), name="attn_delta")(do, o)


def _attn_bwd(qp, kp, vp, dob, lse, delta, tb, wire):
    s = qp.shape[0]
    nb = s // tb
    jj, ii, cc = _group_schedule(nb, False)
    gw = ATT_GROUP * tb
    rows = 32

    n_steps = int(ii.shape[0])

    def body(jj_ref, ii_ref, cc_ref, q_ref, k_ref, v_ref, do_ref, lse_ref, dl_ref, cw_ref, dqo_ref, dk_ref, dv_ref, rcv_ref,
             st_sc, dpt_sc, pt_sc, dst_sc, dq_ref, send_sems, recv_sems):
        t = pl.program_id(1)
        case = cc_ref[t]
        pair = ii_ref[t]

        @pl.when((pl.program_id(0) == 0) & (t == 0))
        def _():
            _rs_send(cw_ref, rcv_ref, send_sems, recv_sems)

        @pl.when((pl.program_id(0) == HEADS - 1) & (t == n_steps - 1))
        def _():
            _rs_wait(rcv_ref, send_sems, recv_sems)

        @pl.when(t == 0)
        def _():
            dq_ref[...] = jnp.zeros_like(dq_ref)

        @pl.when(case != 0)
        def _():
            dk_ref[...] = jnp.zeros_like(dk_ref)
            dv_ref[...] = jnp.zeros_like(dv_ref)

        def step(lo, width, diag):
            q = q_ref[lo:lo + width, :]
            do = do_ref[lo:lo + width, :]
            k = k_ref[...]
            st_sc[:, 0:width] = lax.dot_general(k, q, (((1,), (1,)), ((), ())), preferred_element_type=F32)
            dpt_sc[:, 0:width] = lax.dot_general(v_ref[...], do, (((1,), (1,)), ((), ())), preferred_element_type=F32)
            dv = dk = None
            for h in range(width // tb):
                ls = slice(h * tb, (h + 1) * tb)
                lse_row = lse_ref[:, lo + h * tb:lo + (h + 1) * tb]
                dl_row = dl_ref[:, lo + h * tb:lo + (h + 1) * tb]
                for r in range(tb // rows):
                    rs = slice(r * rows, (r + 1) * rows)
                    p = jnp.exp2(st_sc[rs, ls] - lse_row)
                    first_visible = (r * rows) // CHUNK * CHUNK
                    if diag and h == 0 and first_visible > 0:
                        col = lax.broadcasted_iota(jnp.int32, (rows, tb), 1)
                        p = jnp.where(col >= first_visible, p, 0.0)
                    ds = p * (dpt_sc[rs, ls] - dl_row)
                    pt_sc[rs, ls] = p.astype(BF16)
                    dst_sc[rs, ls] = ds.astype(BF16)
                dst = dst_sc[:, ls]
                dvh = jnp.dot(pt_sc[:, ls], do[ls, :], preferred_element_type=F32)
                dkh = jnp.dot(dst, q[ls, :], preferred_element_type=F32)
                dv = dvh if dv is None else dv + dvh
                dk = dkh if dk is None else dk + dkh
                dq_ref[ATT_GROUP * pair + lo // tb + h] += lax.dot_general(k, dst, (((0,), (0,)), ((), ())), preferred_element_type=F32)
            dv_ref[...] += dv
            dk_ref[...] += dk

        @pl.when(case == 0)
        def _():
            step(0, ATT_GROUP * tb, False)

        for d in range(ATT_GROUP):
            @pl.when(case == 1 + d)
            def _(d=d):
                step(d * tb, (ATT_GROUP - d) * tb, True)

        @pl.when(t == n_steps - 1)
        def _():
            for i in range(nb):
                dqo_ref[i * tb:(i + 1) * tb, :] = dq_ref[i].T

    qmap = lambda h, t, jj_ref, ii_ref, cc_ref: (ii_ref[t], h)
    kmap = lambda h, t, jj_ref, ii_ref, cc_ref: (jj_ref[t], h)
    rowmap = lambda h, t, jj_ref, ii_ref, cc_ref: (h, 0, ii_ref[t])
    gs = pltpu.PrefetchScalarGridSpec(
        num_scalar_prefetch=3, grid=(HEADS, n_steps),
        in_specs=[pl.BlockSpec((gw, HEAD_PAD), qmap), pl.BlockSpec((tb, HEAD_PAD), kmap), pl.BlockSpec((tb, HEAD_PAD), kmap),
                  pl.BlockSpec((gw, HEAD_PAD), qmap), pl.BlockSpec((None, 1, gw), rowmap), pl.BlockSpec((None, 1, gw), rowmap), ANY],
        out_specs=[pl.BlockSpec((s, HEAD_PAD), lambda h, t, jj_ref, ii_ref, cc_ref: (0, h)),
                   pl.BlockSpec((tb, HEAD_PAD), kmap), pl.BlockSpec((tb, HEAD_PAD), kmap), ANY],
        scratch_shapes=[pltpu.VMEM((tb, gw), F32), pltpu.VMEM((tb, gw), F32), pltpu.VMEM((tb, gw), BF16),
                        pltpu.VMEM((tb, gw), BF16), pltpu.VMEM((nb, HEAD_PAD, tb), F32),
                        pltpu.SemaphoreType.DMA((3,)), pltpu.SemaphoreType.DMA((3,))])
    w = HEADS * HEAD_PAD
    return pl.pallas_call(
        body, grid_spec=gs,
        out_shape=[SDS((s, w), F32), SDS((s, w), F32), SDS((s, w), F32), SDS((3,) + wire.shape[1:], wire.dtype)],
        compiler_params=_cp(2), name="attn_bwd")(jj, ii, cc, qp, kp, vp, dob, lse, delta, wire)


def _head_norm(t, g):
    r = lax.rsqrt(jnp.mean(t * t, axis=-1, keepdims=True) + EPS)
    th = t * r
    return r, th, th * g


def _softmax_rows(sc):
    m = jnp.max(sc, axis=-1, keepdims=True)
    e = jnp.exp(sc - m)
    return e / jnp.sum(e, axis=-1, keepdims=True)


def _memattn_fwd(qm, kvm, gq, gk, tm):
    s = qm.shape[0]
    hd = MEM_HEAD_DIM

    def body(q_ref, k_ref, v_ref, gq_ref, gk_ref, o_ref):
        _, _, qn = _head_norm(q_ref[...], gq_ref[...])
        _, _, kn = _head_norm(k_ref[...], gk_ref[...])
        sc = lax.dot_general(qn.astype(BF16), kn.astype(BF16), (((1,), (1,)), ((), ())), preferred_element_type=F32)
        p = _softmax_rows(sc * (1.0 / math.sqrt(hd)))
        o_ref[...] = jnp.dot(p.astype(BF16), v_ref[...].astype(BF16), preferred_element_type=F32).astype(BF16)

    fix = lambda i, h: (0, 0)
    return pl.pallas_call(
        body, grid=(s // tm, MEM_HEADS),
        in_specs=[pl.BlockSpec((tm, hd), lambda i, h: (i, h)), pl.BlockSpec((MEM_LEN, hd), lambda i, h: (0, h)),
                  pl.BlockSpec((MEM_LEN, hd), lambda i, h: (0, MEM_HEADS + h)), pl.BlockSpec((1, hd), fix), pl.BlockSpec((1, hd), fix)],
        out_specs=pl.BlockSpec((tm, hd), lambda i, h: (i, h)),
        out_shape=SDS((s, MEM_HEADS * hd), BF16), compiler_params=_cp(2), name="memattn_fwd")(qm, kvm, kvm, gq, gk)


def _memattn_bwd(qm, kvm, d_o, gq, gk, tm):
    s = qm.shape[0]
    hd = MEM_HEAD_DIM

    def body(q_ref, k_ref, v_ref, do_ref, gq_ref, gk_ref, dq_ref, dk_ref, dv_ref, dgq_ref, dgk_ref, dkn_sc):
        h = pl.program_id(0)
        i = pl.program_id(1)
        last = pl.num_programs(1) - 1

        @pl.when((h == 0) & (i == 0))
        def _():
            dgq_ref[...] = jnp.zeros_like(dgq_ref)
            dgk_ref[...] = jnp.zeros_like(dgk_ref)

        @pl.when(i == 0)
        def _():
            dv_ref[...] = jnp.zeros_like(dv_ref)
            dkn_sc[...] = jnp.zeros_like(dkn_sc)

        rq, qh, qn = _head_norm(q_ref[...], gq_ref[...])
        rk, kh, kn = _head_norm(k_ref[...], gk_ref[...])
        qnb = qn.astype(BF16)
        knb = kn.astype(BF16)
        scale = 1.0 / math.sqrt(hd)
        sc = lax.dot_general(qnb, knb, (((1,), (1,)), ((), ())), preferred_element_type=F32)
        p = _softmax_rows(sc * scale)
        do = do_ref[...].astype(BF16)
        dp = lax.dot_general(do, v_ref[...].astype(BF16), (((1,), (1,)), ((), ())), preferred_element_type=F32)
        dv_ref[...] += lax.dot_general(p.astype(BF16), do, (((0,), (0,)), ((), ())), preferred_element_type=F32)
        ds = ((p * (dp - jnp.sum(dp * p, axis=-1, keepdims=True))) * scale).astype(BF16)
        dqn = jnp.dot(ds, knb, preferred_element_type=F32)
        dkn_sc[...] += lax.dot_general(ds, qnb, (((0,), (0,)), ((), ())), preferred_element_type=F32)
        dgq_ref[...] += jnp.sum(dqn * qh, axis=0, keepdims=True)
        dqh = dqn * gq_ref[...]
        dq_ref[...] = (rq * (dqh - qh * jnp.mean(dqh * qh, axis=-1, keepdims=True))).astype(BF16)

        @pl.when(i == last)
        def _():
            dkn = dkn_sc[...]
            dgk_ref[...] += jnp.sum(dkn * kh, axis=0, keepdims=True)
            dkh = dkn * gk_ref[...]
            dk_ref[...] = rk * (dkh - kh * jnp.mean(dkh * kh, axis=-1, keepdims=True))

    fix = lambda h, i: (0, 0)
    qb = pl.BlockSpec((tm, hd), lambda h, i: (i, h))
    kb = pl.BlockSpec((MEM_LEN, hd), lambda h, i: (0, h))
    return pl.pallas_call(
        body, grid=(MEM_HEADS, s // tm),
        in_specs=[qb, kb, pl.BlockSpec((MEM_LEN, hd), lambda h, i: (0, MEM_HEADS + h)), qb,
                  pl.BlockSpec((1, hd), fix), pl.BlockSpec((1, hd), fix)],
        out_specs=[qb, kb, kb, pl.BlockSpec((1, hd), fix), pl.BlockSpec((1, hd), fix)],
        out_shape=[SDS((s, MEM_HEADS * hd), BF16), SDS((MEM_LEN, MEM_HEADS * hd), F32), SDS((MEM_LEN, MEM_HEADS * hd), F32),
                   SDS((1, hd), F32), SDS((1, hd), F32)],
        scratch_shapes=[pltpu.VMEM((MEM_LEN, hd), F32)],
        compiler_params=_cp(2), name="memattn_bwd")(qm, kvm, kvm, d_o, gq, gk)


def _ffn_specs(tm, tn, nbj, s, order_ji):
    if order_ji:
        ij = lambda f: (lambda j, i: f(i, j))
    else:
        ij = lambda f: f
    prev = lambda i: jnp.maximum(i * (tm // FFN_HALO) - 1, 0)
    cur_g = pl.BlockSpec((tm, tn), ij(lambda i, j: (i, j)))
    cur_v = pl.BlockSpec((tm, tn), ij(lambda i, j: (i, j + nbj)))
    halo_g = pl.BlockSpec((FFN_HALO, tn), ij(lambda i, j: (prev(i), j)))
    halo_v = pl.BlockSpec((FFN_HALO, tn), ij(lambda i, j: (prev(i), j + nbj)))
    w_g = pl.BlockSpec((8, tn), ij(lambda i, j: (0, j)))
    w_v = pl.BlockSpec((8, tn), ij(lambda i, j: (0, j + nbj)))
    b_g = pl.BlockSpec((1, tn), ij(lambda i, j: (0, j)))
    b_v = pl.BlockSpec((1, tn), ij(lambda i, j: (0, j + nbj)))
    return cur_g, cur_v, halo_g, halo_v, w_g, w_v, b_g, b_v


FFN_STRIP = 16


def _conv3_rows(ext, w_ref, b_ref, o, n):
    return (w_ref[0:1, :] * ext[FFN_HALO - 2 + o:FFN_HALO - 2 + o + n, :] + w_ref[1:2, :] * ext[FFN_HALO - 1 + o:FFN_HALO - 1 + o + n, :]
            + w_ref[2:3, :] * ext[FFN_HALO + o:FFN_HALO + o + n, :] + b_ref[...])


def _ffn_fwd(up0, w8, b, tm, tn):
    s = up0.shape[0]
    nbj = D_FF // tn

    def body(g_ref, v_ref, gh_ref, vh_ref, wg_ref, wv_ref, bg_ref, bv_ref, act_ref, extg, extv):
        first = pl.program_id(0) == 0
        for ext, h_ref, c_ref in ((extg, gh_ref, g_ref), (extv, vh_ref, v_ref)):
            ext[0:FFN_HALO, :] = jnp.where(first, 0.0, h_ref[...])
            ext[FFN_HALO:, :] = c_ref[...]
        for r in range(tm // FFN_STRIP):
            o = r * FFN_STRIP
            ug = _conv3_rows(extg, wg_ref, bg_ref, o, FFN_STRIP)
            uv = _conv3_rows(extv, wv_ref, bv_ref, o, FFN_STRIP)
            act_ref[o:o + FFN_STRIP, :] = ((ug * jax.nn.sigmoid(ug)) * uv).astype(BF16)

    specs = _ffn_specs(tm, tn, nbj, s, False)
    return pl.pallas_call(
        body, grid=(s // tm, nbj), in_specs=list(specs),
        out_specs=pl.BlockSpec((tm, tn), lambda i, j: (i, j)), out_shape=SDS((s, D_FF), BF16),
        scratch_shapes=[pltpu.VMEM((tm + FFN_HALO, tn), F32), pltpu.VMEM((tm + FFN_HALO, tn), F32)],
        compiler_params=_cp(2), name="ffn_fwd")(up0, up0, up0, up0, w8, w8, b, b)


def _ffn_bwd(d_act, up0, w8, b, tm, tn):
    s = up0.shape[0]
    nbj = D_FF // tn
    te = tm + FFN_HALO

    def body(da_ref, dan_ref, g_ref, v_ref, gh_ref, vh_ref, gn_ref, vn_ref, wg_ref, wv_ref, bg_ref, bv_ref,
             og_ref, ov_ref, dbg_ref, dbv_ref, dwg_ref, dwv_ref, extg, extv, extdg, extdv, accg, accv):
        i = pl.program_id(1)
        first = i == 0
        last = i == pl.num_programs(1) - 1

        @pl.when(first)
        def _():
            for r in (dbg_ref, dbv_ref, dwg_ref, dwv_ref):
                r[...] = jnp.zeros_like(r)

        for ext, h_ref, c_ref, n_ref in ((extg, gh_ref, g_ref, gn_ref), (extv, vh_ref, v_ref, vn_ref)):
            ext[0:FFN_HALO, :] = jnp.where(first, 0.0, h_ref[...])
            ext[FFN_HALO:FFN_HALO + tm, :] = c_ref[...]
            ext[FFN_HALO + tm:, :] = n_ref[...]

        def fold8(x):
            acc = x[0:8, :]
            for q in range(1, x.shape[0] // 8):
                acc = acc + x[q * 8:(q + 1) * 8, :]
            return acc

        def taps(ext, o, n):
            return [ext[FFN_HALO - 2 + k + o:FFN_HALO - 2 + k + o + n, :] for k in range(3)]

        accg[...] = jnp.zeros_like(accg)
        accv[...] = jnp.zeros_like(accv)

        def gate_bwd(o, n, da, own_rows):
            xg, xv = taps(extg, o, n), taps(extv, o, n)
            ug = wg_ref[0:1, :] * xg[0] + wg_ref[1:2, :] * xg[1] + wg_ref[2:3, :] * xg[2] + bg_ref[...]
            uv = wv_ref[0:1, :] * xv[0] + wv_ref[1:2, :] * xv[1] + wv_ref[2:3, :] * xv[2] + bv_ref[...]
            sg = jax.nn.sigmoid(ug)
            dgt = da * uv * (sg * (1.0 + ug * (1.0 - sg)))
            dvl = da * (ug * sg)
            extdg[o:o + n, :] = dgt
            extdv[o:o + n, :] = dvl
            if own_rows:
                for acc, d, x in ((accg, dgt, xg), (accv, dvl, xv)):
                    acc[0] += fold8(d)
                    for k in range(3):
                        acc[1 + k] += fold8(d * x[k])

        for r in range(tm // FFN_STRIP):
            gate_bwd(r * FFN_STRIP, FFN_STRIP, da_ref[r * FFN_STRIP:(r + 1) * FFN_STRIP, :], True)
        gate_bwd(tm, FFN_HALO, jnp.where(last, 0.0, dan_ref[...]), False)

        for extd, w_ref, o_ref, db_ref, dw_ref, acc in ((extdg, wg_ref, og_ref, dbg_ref, dwg_ref, accg),
                                                        (extdv, wv_ref, ov_ref, dbv_ref, dwv_ref, accv)):
            for r in range(tm // FFN_STRIP):
                o = r * FFN_STRIP
                o_ref[o:o + FFN_STRIP, :] = (w_ref[2:3, :] * extd[o:o + FFN_STRIP, :] + w_ref[1:2, :] * extd[o + 1:o + 1 + FFN_STRIP, :]
                                             + w_ref[0:1, :] * extd[o + 2:o + 2 + FFN_STRIP, :]).astype(BF16)
            db_ref[...] += jnp.sum(acc[0], axis=0, keepdims=True)
            for k in range(3):
                dw_ref[k:k + 1, :] += jnp.sum(acc[1 + k], axis=0, keepdims=True)

    cur_g, cur_v, halo_g, halo_v, w_g, w_v, b_g, b_v = _ffn_specs(tm, tn, nbj, s, True)
    nxt_row = lambda i: jnp.minimum((i + 1) * (tm // FFN_HALO), s // FFN_HALO - 1)
    cur = pl.BlockSpec((tm, tn), lambda j, i: (i, j))
    nxt = pl.BlockSpec((FFN_HALO, tn), lambda j, i: (nxt_row(i), j))
    nxt_v = pl.BlockSpec((FFN_HALO, tn), lambda j, i: (nxt_row(i), j + nbj))
    acc1 = pl.BlockSpec((1, tn), lambda j, i: (0, j))
    acc8 = pl.BlockSpec((8, tn), lambda j, i: (0, j))
    return pl.pallas_call(
        body, grid=(nbj, s // tm), in_specs=[cur, nxt, cur_g, cur_v, halo_g, halo_v, nxt, nxt_v, w_g, w_v, b_g, b_v],
        out_specs=[cur, cur, acc1, acc1, acc8, acc8],
        out_shape=[SDS((s, D_FF), BF16), SDS((s, D_FF), BF16), SDS((1, D_FF), F32), SDS((1, D_FF), F32),
                   SDS((8, D_FF), F32), SDS((8, D_FF), F32)],
        scratch_shapes=[pltpu.VMEM((tm + 2 * FFN_HALO, tn), F32), pltpu.VMEM((tm + 2 * FFN_HALO, tn), F32),
                        pltpu.VMEM((te, tn), F32), pltpu.VMEM((te, tn), F32),
                        pltpu.VMEM((4, 8, tn), F32), pltpu.VMEM((4, 8, tn), F32)],
        compiler_params=_cp(2), name="ffn_bwd")(d_act, d_act, up0, up0, up0, up0, up0, up0, w8, w8, b, b)


def _down_loss(act, w_down, x2, target, tm):
    s = act.shape[0]

    def body(a_ref, w_ref, x_ref, t_ref, dyf_ref, dyb_ref, ls_ref):
        @pl.when(pl.program_id(0) == 0)
        def _():
            ls_ref[...] = jnp.zeros_like(ls_ref)

        y = x_ref[...] + jnp.dot(a_ref[...], w_ref[...], preferred_element_type=F32)
        e = y - t_ref[...]
        ls_ref[...] += jnp.sum(e * e)
        dy = e * (1.0 / D_MODEL)
        dyf_ref[...] = dy
        dyb_ref[...] = dy.astype(BF16)

    row = lambda i: (i, 0)
    return pl.pallas_call(
        body, grid=(s // tm,),
        in_specs=[pl.BlockSpec((tm, D_FF), row), pl.BlockSpec((D_FF, D_MODEL), lambda i: (0, 0)),
                  pl.BlockSpec((tm, D_MODEL), row), pl.BlockSpec((tm, D_MODEL), row)],
        out_specs=[pl.BlockSpec((tm, D_MODEL), row), pl.BlockSpec((tm, D_MODEL), row), pl.BlockSpec((8, 128), lambda i: (0, 0))],
        out_shape=[SDS((s, D_MODEL), F32), SDS((s, D_MODEL), BF16), SDS((8, 128), F32)],
        compiler_params=_cp(1), name="down_loss")(act, w_down, x2, target)


def _adamw_math(w, g, m, v):
    mn = ADAM_B1 * m + (1.0 - ADAM_B1) * g
    vn = ADAM_B2 * v + (1.0 - ADAM_B2) * (g * g)
    m_hat = mn / (1.0 - ADAM_B1 ** ADAM_STEP)
    v_hat = vn / (1.0 - ADAM_B2 ** ADAM_STEP)
    return -ADAM_LR * (m_hat / (jnp.sqrt(v_hat) + ADAM_EPS) + ADAM_WD * w), mn, vn


def _adamw(w, g, m, v, name):
    rows, cols = w.shape
    tr = rows if rows <= 256 else (256 if rows % 256 == 0 else rows // 2)

    def body(w_ref, g_ref, m_ref, v_ref, d_ref, mo_ref, vo_ref):
        d_ref[...], mo_ref[...], vo_ref[...] = _adamw_math(w_ref[...], g_ref[...], m_ref[...], v_ref[...])

    blk = pl.BlockSpec((tr, cols), lambda i: (i, 0))
    return pl.pallas_call(body, grid=(rows // tr,), in_specs=[blk] * 4, out_specs=[blk] * 3,
                          out_shape=[SDS((rows, cols), F32)] * 3, compiler_params=_cp(1), name=name)(w, g, m, v)


def _adamw_small(ws, gs, ms, vs):
    n = len(ws)

    def body(*refs):
        ins, outs = refs[:4 * n], refs[4 * n:]
        for k in range(n):
            d, mn, vn = _adamw_math(ins[k][...], ins[n + k][...], ins[2 * n + k][...], ins[3 * n + k][...])
            outs[k][...] = d
            outs[n + k][...] = mn
            outs[2 * n + k][...] = vn

    vm = pl.BlockSpec(memory_space=pltpu.VMEM)
    outs = pl.pallas_call(body, in_specs=[vm] * (4 * n), out_specs=[vm] * (3 * n),
                          out_shape=[SDS(w.shape, F32) for w in ws] * 3, name="adamw_small")(*ws, *gs, *ms, *vs)
    return outs[:n], outs[n:2 * n], outs[2 * n:]


ANY = pl.BlockSpec(memory_space=pl.ANY)


def _coords():
    return lax.axis_index("x"), lax.axis_index("y"), lax.axis_index("c")


def _other_chips(x, y):
    return [(1 - x, y), (x, 1 - y), (1 - x, 1 - y)]


D2D_CHUNKS = 8
ICI_CHUNKS = 4


def _row_chunks(n_rows, n_chunks, align):
    step = -(-n_rows // (n_chunks * align)) * align
    return [(r, min(step, n_rows - r)) for r in range(0, n_rows, step)]


def _ag_copy(out_ref, send_sems, recv_sems, k, shard, base, r0, nr, to, src=None):
    rows_ = pl.ds(pl.multiple_of(base + r0, 16), nr)
    dst = out_ref.at[shard, rows_]
    return pltpu.make_async_remote_copy(src_ref=dst if src is None else src.at[rows_], dst_ref=dst, send_sem=send_sems.at[k],
                                        recv_sem=recv_sems.at[k], device_id=to, device_id_type=MESH)


def _ag_send(w_ref, out_ref, send_sems, recv_sems):
    x, y, c = _coords()
    half_rows = w_ref.shape[0] // 2
    for k, (px, py) in enumerate(_other_chips(x, y)):
        for r0, nr in _row_chunks(half_rows, ICI_CHUNKS, 16):
            _ag_copy(out_ref, send_sems, recv_sems, k, 2 * x + y, c * half_rows, r0, nr, (px, py, c), src=w_ref).start()


def _ag_finish(w_ref, out_ref, send_sems, recv_sems):
    x, y, c = _coords()
    half_rows = w_ref.shape[0] // 2
    chips = _other_chips(x, y)
    sibling = (x, y, 1 - c)
    for k, (px, py) in enumerate(chips):
        _ag_copy(out_ref, send_sems, recv_sems, k, 2 * px + py, c * half_rows, 0, half_rows, (px, py, c)).wait_recv()
        for r0, nr in _row_chunks(half_rows, ICI_CHUNKS, 16):
            _ag_copy(out_ref, send_sems, recv_sems, 3 + k, 2 * px + py, c * half_rows, r0, nr, sibling).start()
    for k, (px, py) in enumerate(chips):
        _ag_copy(out_ref, send_sems, recv_sems, 3 + k, 2 * px + py, (1 - c) * half_rows, 0, half_rows, sibling).wait_recv()
    for k in range(6):
        _ag_copy(out_ref, send_sems, recv_sems, k, 2 * x + y, c * half_rows, 0, half_rows, sibling).wait_send()


def _ag_weights_and_small(wsh, v):
    rows, cols = wsh.shape
    vr, vc = v.shape

    def body(w_ref, v_ref, out_ref, vsum_ref, buf, send_sems, recv_sems, vsend, vrecv):
        _ag_send(w_ref, out_ref, send_sems, recv_sems)
        x, y, c = _coords()
        me = 4 * x + 2 * y + c
        buf[me] = v_ref[...]
        cps = []
        for r in range(1, 8):
            dx, dy, dc = (r >> 2) & 1, (r >> 1) & 1, r & 1
            to = (x + dx - 2 * x * dx, y + dy - 2 * y * dy, c + dc - 2 * c * dc)
            cp = pltpu.make_async_remote_copy(src_ref=v_ref, dst_ref=buf.at[me], send_sem=vsend.at[r - 1],
                                              recv_sem=vrecv.at[r - 1], device_id=to, device_id_type=MESH)
            cp.start()
            cps.append(cp)
        for cp in cps:
            cp.wait()
        acc = buf[0]
        for d in range(1, 8):
            acc = acc + buf[d]
        vsum_ref[...] = acc
        _ag_finish(w_ref, out_ref, send_sems, recv_sems)

    vm = pl.BlockSpec(memory_space=pltpu.VMEM)
    return pl.pallas_call(
        body, in_specs=[ANY, vm], out_specs=[ANY, vm], out_shape=[SDS((4, rows, cols), wsh.dtype), SDS((vr, vc), F32)],
        scratch_shapes=[pltpu.VMEM((8, vr, vc), F32), pltpu.SemaphoreType.DMA((6,)), pltpu.SemaphoreType.DMA((6,)),
                        pltpu.SemaphoreType.DMA((7,)), pltpu.SemaphoreType.DMA((7,))],
        name="ag_weights_and_small")(wsh, v)


def _rs_swap_halves(gfull, tag):
    n_sh, rows, cols = gfull.shape
    half_rows = rows // 2

    def body(g_ref, recv_ref, send_sem, recv_sem):
        x, y, c = _coords()
        sib_base = (1 - c) * half_rows
        for sh in range(n_sh):
            for r0, nr in _row_chunks(half_rows, D2D_CHUNKS, 8):
                pltpu.make_async_remote_copy(
                    src_ref=g_ref.at[sh, pl.ds(pl.multiple_of(sib_base + r0, 8), nr)], dst_ref=recv_ref.at[sh, pl.ds(r0, nr)],
                    send_sem=send_sem, recv_sem=recv_sem, device_id=(x, y, 1 - c), device_id_type=MESH).start()
        pltpu.make_async_remote_copy(src_ref=recv_ref, dst_ref=recv_ref, send_sem=send_sem, recv_sem=recv_sem,
                                     device_id=(x, y, 1 - c), device_id_type=MESH).wait()

    return pl.pallas_call(
        body, in_specs=[ANY], out_specs=ANY, out_shape=SDS((n_sh, half_rows, cols), gfull.dtype),
        scratch_shapes=[pltpu.SemaphoreType.DMA, pltpu.SemaphoreType.DMA], name="rs_swap_halves" + tag)(gfull)


def _rs_add_pair(gfull, recv, core, tr, tag):
    n_sh, rows, cols = gfull.shape
    half_rows = rows // 2
    nblk = half_rows // tr

    def body(c_ref, g_ref, r_ref, o_ref, ob_ref):
        acc = g_ref[...] + r_ref[...]
        o_ref[...] = acc
        ob_ref[...] = acc.astype(BF16)

    out = pl.BlockSpec((None, tr, cols), lambda sh, i, c_ref: (sh, i, 0))
    gs = pltpu.PrefetchScalarGridSpec(
        num_scalar_prefetch=1, grid=(n_sh, nblk),
        in_specs=[pl.BlockSpec((None, tr, cols), lambda sh, i, c_ref: (sh, c_ref[0] * nblk + i, 0)), out],
        out_specs=[out, out])
    return pl.pallas_call(body, grid_spec=gs, out_shape=[SDS((n_sh, half_rows, cols), F32), SDS((n_sh, half_rows, cols), BF16)],
                          compiler_params=_cp(2), name="rs_add_pair" + tag)(core, gfull, recv)


def _rs_send(cs_ref, recv_ref, send_sems, recv_sems):
    x, y, c = _coords()
    half_rows = cs_ref.shape[1]
    for k, (px, py) in enumerate(_other_chips(x, y)):
        for r0, nr in _row_chunks(half_rows, ICI_CHUNKS, 16):
            pltpu.make_async_remote_copy(
                src_ref=cs_ref.at[2 * px + py, pl.ds(r0, nr)], dst_ref=recv_ref.at[k, pl.ds(r0, nr)],
                send_sem=send_sems.at[k], recv_sem=recv_sems.at[k], device_id=(px, py, c), device_id_type=MESH).start()


def _rs_wait(recv_ref, send_sems, recv_sems):
    x, y, c = _coords()
    for k, (px, py) in enumerate(_other_chips(x, y)):
        pltpu.make_async_remote_copy(src_ref=recv_ref.at[k], dst_ref=recv_ref.at[k], send_sem=send_sems.at[k],
                                     recv_sem=recv_sems.at[k], device_id=(px, py, c), device_id_type=MESH).wait()


def _rs_to_owner(chipsum):
    n_sh, half_rows, cols = chipsum.shape

    def body(cs_ref, recv_ref, send_sems, recv_sems):
        _rs_send(cs_ref, recv_ref, send_sems, recv_sems)
        _rs_wait(recv_ref, send_sems, recv_sems)

    return pl.pallas_call(
        body, in_specs=[ANY], out_specs=ANY, out_shape=SDS((3, half_rows, cols), chipsum.dtype),
        scratch_shapes=[pltpu.SemaphoreType.DMA((3,)), pltpu.SemaphoreType.DMA((3,))], name="rs_to_owner")(chipsum)


def _rs_add_chips(chipsum, recv, shard_core, tr, tag):
    _, half_rows, cols = chipsum.shape

    def body(s_ref, m_ref, r0_ref, r1_ref, r2_ref, o_ref):
        o_ref[...] = ((m_ref[...] + r0_ref[...].astype(F32)) + r1_ref[...].astype(F32)) + r2_ref[...].astype(F32)

    gs = pltpu.PrefetchScalarGridSpec(
        num_scalar_prefetch=1, grid=(half_rows // tr,),
        in_specs=[pl.BlockSpec((None, tr, cols), lambda i, s_ref: (s_ref[0], i, 0))]
        + [pl.BlockSpec((None, tr, cols), (lambda k: lambda i, s_ref: (k, i, 0))(k)) for k in range(3)],
        out_specs=pl.BlockSpec((None, tr, cols), lambda i, s_ref: (s_ref[1], i, 0)))
    return pl.pallas_call(body, grid_spec=gs, out_shape=SDS((2, half_rows, cols), F32),
                          compiler_params=_cp(1), name="rs_add_chips" + tag)(shard_core, chipsum, recv, recv, recv)


def _rs_join_halves(buf, tag):
    _, half_rows, cols = buf.shape

    def body(b_ref, out_ref, send_sem, recv_sem):
        x, y, c = _coords()
        for r0, nr in _row_chunks(half_rows, D2D_CHUNKS, 8):
            pltpu.make_async_remote_copy(src_ref=out_ref.at[c, pl.ds(r0, nr)], dst_ref=out_ref.at[c, pl.ds(r0, nr)], send_sem=send_sem,
                                         recv_sem=recv_sem, device_id=(x, y, 1 - c), device_id_type=MESH).start()
        pltpu.make_async_remote_copy(src_ref=out_ref.at[c], dst_ref=out_ref.at[c], send_sem=send_sem, recv_sem=recv_sem,
                                     device_id=(x, y, 1 - c), device_id_type=MESH).wait()

    return pl.pallas_call(
        body, in_specs=[ANY], out_specs=ANY, out_shape=SDS(buf.shape, buf.dtype), input_output_aliases={0: 0},
        scratch_shapes=[pltpu.SemaphoreType.DMA, pltpu.SemaphoreType.DMA], name="rs_join_halves" + tag)(buf)


BIG = [("w_in", (1024, 1440), 1), ("w_uq", (256, 768), 1), ("w_ukv", (128, 1024), 1), ("w_out", (1024, 1024), 0),
       ("w_mem_q", (1024, 1024), 0), ("w_mem_kv", (1024, 2048), 1), ("w_mem_o", (1024, 1024), 0),
       ("w_up", (1024, 5632), 1), ("w_down", (2816, 1024), 0)]
SMALL_REP = [("mix_norm_g", 1024), ("b_conv_in", 1024), ("b_conv_dw", 512), ("conv_ln_g", 512), ("conv_ln_b", 512),
             ("q_lat_norm_g", 256), ("kv_lat_norm_g", 128), ("q_norm_g", 96), ("k_norm_g", 96), ("mem_norm_x_g", 1024),
             ("mem_norm_m_g", 1024), ("mem_q_norm_g", 256), ("mem_k_norm_g", 256), ("ffn_norm_g", 1024), ("b_ffn_dw", 5632)]
SMALL_SH = [("w_conv_dw", (31, 512)), ("w_ffn_dw", (3, 5632))]


def _shard_shape(shape, axis):
    return tuple(d // 4 if a == axis else d for a, d in enumerate(shape))


def _pack_rows(parts, rows, cols):
    flat = jnp.concatenate([p.reshape(-1) for p in parts])
    flat = jnp.pad(flat, (0, rows * cols - flat.shape[0]))
    return flat.reshape(rows, cols)


AG_EARLY, AG_LATE = BIG[:3], BIG[3:]
RS_REST, RS_FFN = AG_EARLY, AG_LATE


def _group_rows(group):
    used = sum(_shard_shape(shape, axis)[0] * _shard_shape(shape, axis)[1] // PACK_COLS for _, shape, axis in group)
    return -(-used // 512) * 512


def _pick_rows(n, cap=384):
    return max(r for r in range(16, cap + 1, 16) if n % r == 0)


def _pack_big_shards(ws, group):
    parts = [ws[n].reshape(-1, PACK_COLS) for n, _, _ in group]
    used = sum(p.shape[0] for p in parts)
    pad = _group_rows(group) - used
    return jnp.concatenate(parts + ([jnp.zeros((pad, PACK_COLS), parts[0].dtype)] if pad else []), axis=0)


def _unpack_big_shards(packed, group):
    out, r = {}, 0
    for n, shape, axis in group:
        sh = _shard_shape(shape, axis)
        nr = sh[0] * sh[1] // PACK_COLS
        out[n] = packed[r:r + nr].reshape(sh)
        r += nr
    return out


def _unpack_gathered(g, group):
    out, r = {}, 0
    for n, shape, axis in group:
        sh = _shard_shape(shape, axis)
        nr = sh[0] * sh[1] // PACK_COLS
        part = g[:, r:r + nr]
        if axis == 0:
            out[n] = part.reshape(shape)
        else:
            out[n] = part.reshape((4,) + sh).transpose(1, 0, 2).reshape(shape)
        r += nr
    return out


def _pack_full_grads(gs, group):
    parts = []
    for n, shape, axis in group:
        sh = _shard_shape(shape, axis)
        nr = sh[0] * sh[1] // PACK_COLS
        if axis == 0:
            parts.append(gs[n].reshape(4, nr, PACK_COLS))
        else:
            parts.append(gs[n].reshape(shape[0], 4, sh[1]).transpose(1, 0, 2).reshape(4, nr, PACK_COLS))
    pad = _group_rows(group) - sum(p.shape[1] for p in parts)
    return jnp.concatenate(parts + ([jnp.zeros((4, pad, PACK_COLS), F32)] if pad else []), axis=1)


def _rs_first(gfull, core_idx, tag):
    tr = _pick_rows(gfull.shape[1] // 2)
    return _rs_add_pair(gfull, _rs_swap_halves(gfull, tag), core_idx.reshape(1), tr, tag)


def _rs_last(chipsum, recv, shard_idx, core_idx, tag):
    tr = _pick_rows(chipsum.shape[1])
    red = _rs_add_chips(chipsum, recv, jnp.stack([shard_idx, core_idx]), tr, tag)
    return _rs_join_halves(red, tag).reshape(2 * chipsum.shape[1], chipsum.shape[2])


def _rope_tables(positions):
    inv_freq = ROPE_THETA ** (-jnp.arange(0, ROPE, 2, dtype=F32) / ROPE)
    ang = positions.astype(F32)[:, None] * inv_freq
    cos, sin = jnp.cos(ang), jnp.sin(ang)
    s = positions.shape[0]
    cosf = jnp.concatenate([jnp.ones((s, NOPE), F32), cos, cos, jnp.ones((s, HEAD_PAD - HEAD_DIM), F32)], axis=-1)
    sinf = jnp.concatenate([jnp.zeros((s, NOPE), F32), -sin, sin, jnp.zeros((s, HEAD_PAD - HEAD_DIM), F32)], axis=-1)
    return cosf, sinf


def _pad_heads(w, per_head):
    k = w.shape[0]
    w3 = w.reshape(k, HEADS, per_head)
    return jnp.pad(w3, ((0, 0), (0, 0), (0, HEAD_PAD - per_head))).reshape(k, HEADS * HEAD_PAD)


def _layer_grads(x, mem, positions, target, wf, w_late, sp, shard_idx, core_idx):
    wf = dict(wf)
    s = x.shape[0]
    tm = _row_tile(s, 512)
    tc = _row_tile(s, 256)
    tb = 512 if s % (512 * ATT_GROUP) == 0 else 128
    row2 = lambda a: a.reshape(1, -1)

    w_in = wf["w_in"]
    w_in_pad = jnp.concatenate([w_in[:, :1408], jnp.zeros((D_MODEL, NOPE), BF16), w_in[:, 1408:],
                                jnp.zeros((D_MODEL, HEAD_PAD - HEAD_DIM), BF16)], axis=1)
    w_uq_pad = _pad_heads(wf["w_uq"], HEAD_DIM)
    w_ukv = wf["w_ukv"]
    gq_pad = jnp.pad(sp["q_norm_g"], (0, HEAD_PAD - HEAD_DIM)).reshape(1, HEAD_PAD)
    gk_pad = jnp.pad(sp["k_norm_g"], (0, HEAD_PAD - HEAD_DIM)).reshape(1, HEAD_PAD)
    w_dw32 = jnp.pad(sp["w_conv_dw"], ((0, 1), (0, 0)))
    w_ffn8 = jnp.pad(sp["w_ffn_dw"], ((0, 5), (0, 0)))
    b_ffn = row2(sp["b_ffn_dw"])
    cosf, sinf = _rope_tables(positions)

    z, h1 = _norm_linear(x, 0, D_MODEL, row2(sp["mix_norm_g"]), w_in_pad, F32, tm, IN_COLS_PAD, "in_proj")
    u, u0, u1 = _conv_fwd(z, row2(sp["b_conv_in"]), w_dw32, row2(sp["b_conv_dw"]), row2(sp["conv_ln_g"]), row2(sp["conv_ln_b"]), tc)
    q_raw, cqn = _norm_linear(z, 1024 // Q_RANK, Q_RANK, row2(sp["q_lat_norm_g"]), w_uq_pad, F32, tm, 1024, "q_up")
    kv_raw, ckvn = _norm_linear(z, 1280 // KV_RANK, KV_RANK, row2(sp["kv_lat_norm_g"]), w_ukv, F32, tm, 1024, "kv_up")
    qp, kp, vp = _mla_prep(q_raw, kv_raw, z, cosf, sinf, gq_pad, gk_pad, tc)
    o_f, o_b, lse, gathered = _attn_fwd(qp, kp, vp, tb, w_late)
    wf.update(_unpack_gathered(lax.dynamic_update_index_in_dim(gathered, w_late, shard_idx, 0), AG_LATE))
    w_out_u = wf["w_out"][:CONV_CH]
    w_out_o = jnp.pad(wf["w_out"][CONV_CH:].reshape(HEADS, NOPE, D_MODEL), ((0, 0), (NOPE, 0), (0, 0))).reshape(HEADS * HEAD_PAD, D_MODEL)
    w_up_g, w_up_v = wf["w_up"][:, :D_FF], wf["w_up"][:, D_FF:]
    (x1,) = _linear([(u, w_out_u), (o_b, w_out_o)], False, x, [F32], tm, 1024, "out_proj")

    qm, hq = _norm_linear(x1, 0, D_MODEL, row2(sp["mem_norm_x_g"]), wf["w_mem_q"], F32, tm, 1024, "memq_proj")
    kvm, hm = _norm_linear(mem, 0, D_MODEL, row2(sp["mem_norm_m_g"]), wf["w_mem_kv"], F32, MEM_LEN, 1024, "memkv_proj")
    gmq, gmk = row2(sp["mem_q_norm_g"]), row2(sp["mem_k_norm_g"])
    o_m = _memattn_fwd(qm, kvm, gmq, gmk, tm)
    (x2,) = _linear([(o_m, wf["w_mem_o"])], False, x1, [F32], tm, 1024, "memo_proj")

    up0, h3 = _norm_linear(x2, 0, D_MODEL, row2(sp["ffn_norm_g"]), wf["w_up"], F32, _row_tile(s, 1024), D_FF // 2, "up_proj")
    act = _ffn_fwd(up0, w_ffn8, b_ffn, tc, D_FF // 2)
    dy_f, dy_b, lsum = _down_loss(act, wf["w_down"], x2, target, tm)

    g = {}
    (d_act,) = _linear([(dy_b, wf["w_down"])], True, None, [F32], _row_tile(s, 1024), D_FF // 2, "down_bwd")
    g["w_down"] = _dw(act, dy_b, "dw_down")
    d_up0g, d_up0v, dbg, dbv, dwg, dwv = _ffn_bwd(d_act, up0, w_ffn8, b_ffn, tc, D_FF // 2)
    g["b_ffn_dw"] = jnp.concatenate([dbg, dbv], axis=1).reshape(-1)
    g["w_ffn_dw"] = jnp.concatenate([dwg[:3], dwv[:3]], axis=1)
    g["w_up"] = jnp.concatenate([_dw(h3, d_up0g, "dw_up_g"), _dw(h3, d_up0v, "dw_up_v")], axis=1)
    d_x2f, d_x2b, dg = _linear_normbwd([(d_up0g, w_up_g), (d_up0v, w_up_v)], x2, 0, row2(sp["ffn_norm_g"]), dy_f,
                                       [F32, BF16], tc, "up_bwd")
    g["ffn_norm_g"] = dg.reshape(-1)

    (d_om,) = _linear([(d_x2b, wf["w_mem_o"])], True, None, [BF16], tm, 1024, "memo_bwd")
    g["w_mem_o"] = _dw(o_m, d_x2b, "dw_mem_o")
    d_qm, d_km, d_vm, dgq, dgk = _memattn_bwd(qm, kvm, d_om, gmq, gmk, tm)
    g["mem_q_norm_g"], g["mem_k_norm_g"] = dgq.reshape(-1), dgk.reshape(-1)
    d_kvm = jnp.concatenate([d_km, d_vm], axis=1)
    g["w_mem_q"] = _dw(hq, d_qm, "dw_mem_q")
    g["w_mem_kv"] = _dw(hm, d_kvm, "dw_mem_kv")
    d_x1f, d_x1b, dg = _linear_normbwd([(d_qm, wf["w_mem_q"])], x1, 0, row2(sp["mem_norm_x_g"]), d_x2f, [F32, BF16], tm, "memq_bwd")
    g["mem_norm_x_g"] = dg.reshape(-1)
    _, dg = _linear_normbwd([(d_kvm, wf["w_mem_kv"])], mem, 0, row2(sp["mem_norm_m_g"]), None, [BF16], MEM_LEN, "memkv_bwd")
    g["mem_norm_m_g"] = dg.reshape(-1)

    (d_u,) = _linear([(d_x1b, w_out_u)], True, None, [F32], tm, CONV_CH, "out_bwd_u")
    d_of, d_ob = _linear([(d_x1b, w_out_o)], True, None, [F32, BF16], tm, 1024, "out_bwd_o")
    dw_out_u = _dw(u, d_x1b, "dw_out_u")
    dw_out_o = _dw(o_b, d_x1b, "dw_out_o")
    g["w_out"] = jnp.concatenate([dw_out_u, dw_out_o.reshape(HEADS, HEAD_PAD, D_MODEL)[:, NOPE:].reshape(HEADS * NOPE, D_MODEL)], axis=0)
    chipsum_ffn, wire_ffn = _rs_first(_pack_full_grads(g, RS_FFN), core_idx, "_ffn")
    delta = _attn_delta(d_of, o_f, tb)
    dqp, dkp, dvp, recv_ffn = _attn_bwd(qp, kp, vp, d_ob, lse, delta, tb, wire_ffn)
    g_ffn_packed = _rs_last(chipsum_ffn, recv_ffn, shard_idx, core_idx, "_ffn")
    d_qraw, d_kvraw, d_kr, dgq, dgk = _mla_prep_bwd(dqp, dkp, dvp, q_raw, kv_raw, z, cosf, sinf, gq_pad, gk_pad, tc)
    g["q_norm_g"], g["k_norm_g"] = dgq.reshape(-1)[:HEAD_DIM], dgk.reshape(-1)[:HEAD_DIM]
    g["w_uq"] = _dw(cqn, d_qraw, "dw_uq").reshape(Q_RANK, HEADS, HEAD_PAD)[:, :, :HEAD_DIM].reshape(Q_RANK, HEADS * HEAD_DIM)
    g["w_ukv"] = _dw(ckvn, d_kvraw, "dw_ukv")
    d_cq, dg = _linear_normbwd([(d_qraw, w_uq_pad)], z, 1024 // Q_RANK, row2(sp["q_lat_norm_g"]), None, [BF16], tm, "q_up_bwd")
    g["q_lat_norm_g"] = dg.reshape(-1)
    d_ckv, dg = _linear_normbwd([(d_kvraw, w_ukv)], z, 1280 // KV_RANK, row2(sp["kv_lat_norm_g"]), None, [BF16], tm, "kv_up_bwd")
    g["kv_lat_norm_g"] = dg.reshape(-1)
    d_u1, dlg, dlb, dbdw = _conv_bwd_ln(d_u, u1, row2(sp["conv_ln_g"]), row2(sp["conv_ln_b"]), tc)
    g["conv_ln_g"], g["conv_ln_b"], g["b_conv_dw"] = dlg.reshape(-1), dlb.reshape(-1), dbdw.reshape(-1)
    d_conv, dw_dw, dbin = _conv_bwd_dw(d_u1, u0, z, row2(sp["b_conv_in"]), w_dw32, tc)
    g["w_conv_dw"], g["b_conv_in"] = dw_dw[:CONV_WIDTH], dbin.reshape(-1)
    pieces = [(d_conv, w_in_pad[:, :1024]), (d_cq, w_in_pad[:, 1024:1280]), (d_ckv, w_in_pad[:, 1280:1408]), (d_kr, w_in_pad[:, 1408:])]
    dw_in = [_dw(h1, d, "dw_in_%d" % k) for k, (d, _) in enumerate(pieces)]
    g["w_in"] = jnp.concatenate([dw_in[0], dw_in[1], dw_in[2], dw_in[3][:, NOPE:HEAD_DIM]], axis=1)
    grad_x, dg = _linear_normbwd(pieces, x, 0, row2(sp["mix_norm_g"]), d_x1f, [F32], tm, "in_bwd")
    g["mix_norm_g"] = dg.reshape(-1)
    return lsum[0, 0], grad_x, g, g_ffn_packed


def kernel(x, mem, positions, mix_norm_g, w_in, b_conv_in, w_conv_dw, b_conv_dw, conv_ln_g, conv_ln_b, q_lat_norm_g, w_uq, kv_lat_norm_g, w_ukv, q_norm_g, k_norm_g, w_out, mem_norm_x_g, mem_norm_m_g, w_mem_q, w_mem_kv, mem_q_norm_g, mem_k_norm_g, w_mem_o, ffn_norm_g, w_up, w_ffn_dw, b_ffn_dw, w_down, loss_target, m_mix_norm_g, m_w_in, m_b_conv_in, m_w_conv_dw, m_b_conv_dw, m_conv_ln_g, m_conv_ln_b, m_q_lat_norm_g, m_w_uq, m_kv_lat_norm_g, m_w_ukv, m_q_norm_g, m_k_norm_g, m_w_out, m_mem_norm_x_g, m_mem_norm_m_g, m_w_mem_q, m_w_mem_kv, m_mem_q_norm_g, m_mem_k_norm_g, m_w_mem_o, m_ffn_norm_g, m_w_up, m_w_ffn_dw, m_b_ffn_dw, m_w_down, v_mix_norm_g, v_w_in, v_b_conv_in, v_w_conv_dw, v_b_conv_dw, v_conv_ln_g, v_conv_ln_b, v_q_lat_norm_g, v_w_uq, v_kv_lat_norm_g, v_w_ukv, v_q_norm_g, v_k_norm_g, v_w_out, v_mem_norm_x_g, v_mem_norm_m_g, v_w_mem_q, v_w_mem_kv, v_mem_q_norm_g, v_mem_k_norm_g, v_w_mem_o, v_ffn_norm_g, v_w_up, v_w_ffn_dw, v_b_ffn_dw, v_w_down):
    names = ["mix_norm_g", "w_in", "b_conv_in", "w_conv_dw", "b_conv_dw", "conv_ln_g", "conv_ln_b", "q_lat_norm_g", "w_uq",
             "kv_lat_norm_g", "w_ukv", "q_norm_g", "k_norm_g", "w_out", "mem_norm_x_g", "mem_norm_m_g", "w_mem_q", "w_mem_kv",
             "mem_q_norm_g", "mem_k_norm_g", "w_mem_o", "ffn_norm_g", "w_up", "w_ffn_dw", "b_ffn_dw", "w_down"]
    loc = locals()
    w = {n: loc[n] for n in names}
    m = {n: loc["m_" + n] for n in names}
    v = {n: loc["v_" + n] for n in names}
    shard_idx = 2 * lax.axis_index("x") + lax.axis_index("y")

    shard_idx = shard_idx.astype(jnp.int32)
    core_idx = lax.axis_index("c").astype(jnp.int32)

    w_local = {n: w[n][0] for n, _, _ in BIG}
    w_early = _pack_big_shards(w_local, AG_EARLY).astype(BF16)
    w_late = _pack_big_shards(w_local, AG_LATE).astype(BF16)

    small_sh_full = {}
    gather_in = []
    for n, (r, c) in SMALL_SH:
        csh = c // 4
        slab = lax.dynamic_update_slice(jnp.zeros((r, c), F32), w[n][0], (0, shard_idx * csh))
        gather_in.append(slab.reshape(-1))
    gather_rows = 256
    gathered_early, gathered_small = _ag_weights_and_small(w_early, _pack_rows(gather_in, gather_rows, SMALL_COLS))
    wf = _unpack_gathered(lax.dynamic_update_index_in_dim(gathered_early, w_early, shard_idx, 0), AG_EARLY)
    gathered_small = gathered_small * 0.5
    off = 0
    for n, (r, c) in SMALL_SH:
        small_sh_full[n] = gathered_small.reshape(-1)[off:off + r * c].reshape(r, c)
        off += r * c
    sp = {n: w[n][0] for n, _ in SMALL_REP}
    sp.update(small_sh_full)

    lsum, grad_x, g, g_ffn_packed = _layer_grads(x[0], mem[0], positions[0], loss_target[0], wf, w_late, sp, shard_idx, core_idx)

    small_parts = [jnp.full((SMALL_COLS,), lsum, F32)] + [g[n] for n, _ in SMALL_REP] + [g[n] for n, _ in SMALL_SH]
    small_rows = 368
    small_sum = _allreduce_small_named(_pack_rows(small_parts, small_rows, SMALL_COLS), "allreduce_small").reshape(-1)
    loss = small_sum[0] * (0.5 / D_MODEL)
    gs = {}
    off = SMALL_COLS
    for n, sz in SMALL_REP:
        gs[n] = small_sum[off:off + sz].reshape(w[n].shape)
        off += sz
    for n, (r, c) in SMALL_SH:
        full = small_sum[off:off + r * c].reshape(r, c)
        gs[n] = lax.dynamic_slice(full, (0, shard_idx * (c // 4)), (r, c // 4)).reshape(w[n].shape)
        off += r * c

    chipsum, chipsum_wire = _rs_first(_pack_full_grads(g, RS_REST), core_idx, "_rest")
    g_rest_packed = _rs_last(chipsum, _rs_to_owner(chipsum_wire), shard_idx, core_idx, "_rest")
    g_big = {**_unpack_big_shards(g_rest_packed, RS_REST), **_unpack_big_shards(g_ffn_packed, RS_FFN)}
    gs.update({n: a[None] for n, a in g_big.items()})

    delta, new_m, new_v = {}, {}, {}
    for n, _, _ in BIG:
        d_n, m_n, v_n = _adamw(w[n][0], g_big[n], m[n][0], v[n][0], "adamw_" + n)
        delta[n], new_m[n], new_v[n] = d_n[None], m_n[None], v_n[None]
    small_names = [n for n, _ in SMALL_REP] + [n for n, _ in SMALL_SH]
    as2d = lambda a: a.reshape(-1, a.shape[-1])
    d_s, m_s, v_s = _adamw_small(*[[as2d(d[n]) for n in small_names] for d in (w, gs, m, v)])
    for k, n in enumerate(small_names):
        delta[n], new_m[n], new_v[n] = d_s[k].reshape(w[n].shape), m_s[k].reshape(w[n].shape), v_s[k].reshape(w[n].shape)

    return (loss, grad_x[None], *[gs[n] for n in names], *[delta[n] for n in names], *[new_m[n] for n in names],
            *[new_v[n] for n in names])


def _allreduce_small_named(v, name):
    rows, cols = v.shape

    def body(v_ref, out_ref, buf, send_sems, recv_sems):
        x, y, c = _coords()
        me = 4 * x + 2 * y + c
        buf[me] = v_ref[...]
        cps = []
        for r in range(1, 8):
            dx, dy, dc = (r >> 2) & 1, (r >> 1) & 1, r & 1
            to = (x + dx - 2 * x * dx, y + dy - 2 * y * dy, c + dc - 2 * c * dc)
            cp = pltpu.make_async_remote_copy(src_ref=v_ref, dst_ref=buf.at[me], send_sem=send_sems.at[r - 1],
                                              recv_sem=recv_sems.at[r - 1], device_id=to, device_id_type=MESH)
            cp.start()
            cps.append(cp)
        for cp in cps:
            cp.wait()
        acc = buf[0]
        for d in range(1, 8):
            acc = acc + buf[d]
        out_ref[...] = acc

    vm = pl.BlockSpec(memory_space=pltpu.VMEM)
    return pl.pallas_call(
        body, in_specs=[vm], out_specs=vm, out_shape=SDS((rows, cols), F32),
        scratch_shapes=[pltpu.VMEM((8, rows, cols), F32), pltpu.SemaphoreType.DMA((7,)), pltpu.SemaphoreType.DMA((7,))],
        name=name)(v)
```
